```python
import jax
import jax.numpy as jnp
from jax import lax
import numpy as np

D_MODEL = 2048
BATCH = 4
SEQ = 8192
DEPTH = 2

GRID_W = 64
CTX_LEN = 256
HEAD_DIM = 128
N_HEADS_TOTAL = D_MODEL // HEAD_DIM
N_HEADS_MLA = N_HEADS_TOTAL // 2
N_HEADS_NAT = N_HEADS_TOTAL // 4
N_HEADS_RET = N_HEADS_TOTAL - N_HEADS_MLA - N_HEADS_NAT
MLA_NOPE_DIM = HEAD_DIM
MLA_ROPE_DIM = 64
MLA_V_DIM = HEAD_DIM
MLA_Q_LORA = D_MODEL // 4
MLA_KV_LORA = D_MODEL // 8
MLA_SCALE = (MLA_NOPE_DIM + MLA_ROPE_DIM) ** -0.5
NAT_ROWS = 8
NAT_COLS = 16
RET_CHUNK = 128
MLA_W = N_HEADS_MLA * MLA_V_DIM
NAT_W = N_HEADS_NAT * HEAD_DIM
RET_W = N_HEADS_RET * HEAD_DIM
D_MIX = MLA_W + NAT_W + RET_W
IN_SPLITS = (MLA_Q_LORA, MLA_KV_LORA, MLA_ROPE_DIM, NAT_W, NAT_W, NAT_W, RET_W, RET_W, RET_W, RET_W, RET_W)
IN_COLS = sum(IN_SPLITS)
N_EXPERTS = 16
EXPERT_FF = D_MODEL // 2
EC_CAPACITY = 2
Q_BLOCK = 128
ROPE_BASE = 10000.0
EPS = 1e-6

kernel_name = 'hybrid_mla_nat_retention_ecmoe_dit'


def _rmsnorm(x, g):
    xf = x.astype(jnp.float32)
    y = xf * lax.rsqrt(jnp.mean(xf * xf, axis=-1, keepdims=True) + EPS)
    return (y * g.astype(jnp.float32)).astype(x.dtype)


def _heads(t, n_heads):
    return t.reshape(t.shape[:-1] + (n_heads, t.shape[-1] // n_heads))


def _split_cols(p):
    return jnp.split(p, np.cumsum(IN_SPLITS)[:-1].tolist(), axis=-1)


def _modulate(h, shift, scale):
    return h * (1 + scale) + shift


def _rope_tables(n):
    t = jnp.arange(n)
    n_freq = MLA_ROPE_DIM // 4
    inv = ROPE_BASE ** (-jnp.arange(n_freq, dtype=jnp.float32) / n_freq)
    ang_r = (t // GRID_W).astype(jnp.float32)[:, None] * inv
    ang_c = (t % GRID_W).astype(jnp.float32)[:, None] * inv
    return (jnp.cos(ang_r)[:, None], jnp.sin(ang_r)[:, None], jnp.cos(ang_c)[:, None], jnp.sin(ang_c)[:, None])


def _rot_half(x, cos, sin):
    x1, x2 = jnp.split(x, 2, axis=-1)
    return jnp.concatenate([x1 * cos - x2 * sin, x1 * sin + x2 * cos], axis=-1)


def _axial_rope(x, tables):
    cr, sr, cc, sc = tables
    xr, xc = jnp.split(x.astype(jnp.float32), 2, axis=-1)
    return jnp.concatenate([_rot_half(xr, cr, sr), _rot_half(xc, cc, sc)], axis=-1).astype(x.dtype)


def _block_attn(q, k, v, scale):
    b, sq, h, dq = q.shape
    qb = q.reshape(b, sq // Q_BLOCK, Q_BLOCK, h, dq).transpose(1, 0, 2, 3, 4)

    def one_block(q_blk):
        s = jnp.einsum('bqhc,bkc->bhqk', q_blk, k).astype(jnp.float32) * scale
        p = jax.nn.softmax(s, axis=-1).astype(v.dtype)
        return jnp.einsum('bhqk,bkc->bqhc', p, v)

    o = lax.map(one_block, qb)
    return o.transpose(1, 0, 2, 3, 4).reshape(b, sq, h, v.shape[-1])


def _mla_keys(ckv_raw, kr_raw, kv_g, tables):
    ckv = _rmsnorm(ckv_raw, kv_g)
    kr = kr_raw if tables is None else _axial_rope(kr_raw[:, :, None], tables)[:, :, 0]
    return jnp.concatenate([ckv, kr], axis=-1), ckv


def _mla_queries(cq_raw, q_g, w_uq, w_uk, tables):
    q = _heads(_rmsnorm(cq_raw, q_g) @ w_uq, N_HEADS_MLA)
    q_nope, q_rope = jnp.split(q, [MLA_NOPE_DIM], axis=-1)
    if tables is not None:
        q_rope = _axial_rope(q_rope, tables)
    q_abs = jnp.einsum('bnhd,chd->bnhc', q_nope, w_uk)
    return jnp.concatenate([q_abs, q_rope], axis=-1)


def _mla_out(o_lat, w_uv):
    o = jnp.einsum('bnhc,chd->bnhd', o_lat, w_uv)
    return o.reshape(o.shape[0], o.shape[1], -1)


def _nat_latent(q, k, v, k_ctx, v_ctx, rpb):
    b, n, h, dh = q.shape
    rows = n // GRID_W
    kr = min(NAT_ROWS, rows)
    scale = dh ** -0.5
    qg = q.reshape(b, rows, GRID_W, h, dh).transpose(1, 0, 2, 3, 4)
    kg = k.reshape(b, rows, GRID_W, h, dh)
    vg = v.reshape(b, rows, GRID_W, h, dh)
    col = jnp.arange(GRID_W)
    col_idx = jnp.clip(col - NAT_COLS // 2, 0, GRID_W - NAT_COLS)[:, None] + jnp.arange(NAT_COLS)
    bias_c = rpb[:, :, col_idx - col[:, None] + NAT_COLS - 1]

    def row_block(args):
        r, q_r = args
        r0 = jnp.clip(r - kr // 2, 0, rows - kr)
        k_win = lax.dynamic_slice_in_dim(kg, r0, kr, axis=1)[:, :, col_idx]
        v_win = lax.dynamic_slice_in_dim(vg, r0, kr, axis=1)[:, :, col_idx]
        dr = r0 + jnp.arange(kr) - r + NAT_ROWS - 1
        bias = jnp.take(bias_c, dr, axis=1).transpose(0, 2, 1, 3)
        s_win = jnp.einsum('bqhd,brqchd->bhqrc', q_r, k_win).astype(jnp.float32) * scale + bias
        s_ctx = jnp.einsum('bqhd,bkhd->bhqk', q_r, k_ctx).astype(jnp.float32) * scale
        s = jnp.concatenate([s_win.reshape(b, h, GRID_W, kr * NAT_COLS), s_ctx], axis=-1)
        p = jax.nn.softmax(s, axis=-1).astype(v.dtype)
        p_win = p[..., :kr * NAT_COLS].reshape(b, h, GRID_W, kr, NAT_COLS)
        return (jnp.einsum('bhqrc,brqchd->bqhd', p_win, v_win)
                + jnp.einsum('bhqk,bkhd->bqhd', p[..., kr * NAT_COLS:], v_ctx))

    o = lax.map(row_block, (jnp.arange(rows), qg))
    return o.transpose(1, 0, 2, 3, 4).reshape(b, n, h * dh)


def _nat_context(q, k, v):
    b, L, h, dh = q.shape
    s = jnp.einsum('bqhd,bkhd->bhqk', q, k).astype(jnp.float32) * dh ** -0.5
    p = jax.nn.softmax(s, axis=-1).astype(v.dtype)
    return jnp.einsum('bhqk,bkhd->bqhd', p, v).reshape(b, L, h * dh)


def _retention_dir(q, k, v, log_g, s0):
    b, h, L, dk = k.shape
    dv = v.shape[-1]
    n = L // RET_CHUNK
    kc = k.reshape(b, h, n, RET_CHUNK, dk)
    vc = v.reshape(b, h, n, RET_CHUNK, dv)
    pos = jnp.arange(RET_CHUNK, dtype=jnp.float32)
    w_k = jnp.exp(log_g[:, None] * (RET_CHUNK - 1 - pos))
    kv = jnp.einsum('bhncd,hc,bhnce->nbhde', kc, w_k, vc)
    g_chunk = jnp.exp(log_g * RET_CHUNK)[None, :, None, None]

    def step(s, kv_n):
        return g_chunk * s + kv_n, s

    s_final, s_prev = lax.scan(step, s0, kv)
    if q is None:
        return None, s_final
    qc = q.reshape(b, h, n, RET_CHUNK, dk)
    rel = pos[:, None] - pos[None, :]
    dmat = jnp.where(rel >= 0, jnp.exp(log_g[:, None, None] * jnp.maximum(rel, 0.0)), 0.0)
    scores = jnp.einsum('bhncd,bhnsd->bhncs', qc, kc) * dmat[:, None]
    o_inner = jnp.einsum('bhncs,bhnse->bhnce', scores, vc)
    w_q = jnp.exp(log_g[:, None] * (pos + 1.0))
    o_cross = jnp.einsum('bhncd,hc,nbhde->bhnce', qc, w_q, s_prev)
    return (o_inner + o_cross).reshape(b, h, L, dv), s_final


def _head_groupnorm(o):
    mu = jnp.mean(o, axis=-1, keepdims=True)
    var = jnp.mean(jnp.square(o - mu), axis=-1, keepdims=True)
    return (o - mu) * lax.rsqrt(var + EPS)


def _ret_merge(o_f, o_b, g_f, g_b):
    b, h, L, dv = o_f.shape

    def to_tok(o):
        return _head_groupnorm(o).transpose(0, 2, 1, 3).reshape(b, L, h * dv)

    y = jax.nn.silu(g_f.astype(jnp.float32)) * to_tok(o_f) + jax.nn.silu(g_b.astype(jnp.float32)) * to_tok(o_b)
    return y.astype(g_f.dtype)


def _retention(cols_c, cols_l, logit_f, logit_b, ctx_out):
    def prep(t):
        return _heads(t, N_HEADS_RET).transpose(0, 2, 1, 3).astype(jnp.float32)

    k_scale = HEAD_DIM ** -0.5
    qc, kc, vc = prep(cols_c[0]), prep(cols_c[1]) * k_scale, prep(cols_c[2])
    ql, kl, vl = prep(cols_l[0]), prep(cols_l[1]) * k_scale, prep(cols_l[2])
    lg_f = -jax.nn.softplus(-logit_f.astype(jnp.float32))
    lg_b = -jax.nn.softplus(-logit_b.astype(jnp.float32))
    s0 = jnp.zeros(kc.shape[:2] + (HEAD_DIM, HEAD_DIM), jnp.float32)

    def flip(t):
        return t[:, :, ::-1]

    o_cf, s_cf = _retention_dir(qc if ctx_out else None, kc, vc, lg_f, s0)
    o_cb, s_cb = _retention_dir(flip(qc) if ctx_out else None, flip(kc), flip(vc), lg_b, s0)
    o_lf, _ = _retention_dir(ql, kl, vl, lg_f, s_cf)
    o_lb, _ = _retention_dir(flip(ql), flip(kl), flip(vl), lg_b, s_cb)
    y_l = _ret_merge(o_lf, flip(o_lb), cols_l[3], cols_l[4])
    y_c = _ret_merge(o_cf, flip(o_cb), cols_c[3], cols_c[4]) if ctx_out else None
    return y_c, y_l


def _merge_groups(ys, g):
    gs = jnp.split(g, [MLA_W, MLA_W + NAT_W])
    return jnp.concatenate([_rmsnorm(y, gi) for y, gi in zip(ys, gs)], axis=-1)


def _ec_moe(h, w_router, w_gate, w_up, w_down):
    b, n, d = h.shape
    cap = EC_CAPACITY * n // N_EXPERTS
    aff = jax.nn.softmax((h @ w_router).astype(jnp.float32), axis=-1)
    gate, idx = lax.top_k(jnp.swapaxes(aff, 1, 2), cap)
    xs = jax.vmap(lambda hb, ib: hb[ib])(h, idx)
    a = jnp.einsum('becd,edf->becf', xs, w_gate)
    u = jnp.einsum('becd,edf->becf', xs, w_up)
    y = jnp.einsum('becf,efd->becd', jax.nn.silu(a) * u, w_down) * gate[..., None].astype(h.dtype)
    return jax.vmap(lambda ib, yb: jnp.zeros((n, d), h.dtype).at[ib.reshape(-1)].add(yb.reshape(-1, d)))(idx, y)


def _layer(x_l, x_c, c, c_ctx, lp, last):
    mod_l = jax.nn.silu(c) @ lp['w_mod'] + lp['b_mod']
    mod_c = jax.nn.silu(c_ctx) @ lp['w_mod'] + lp['b_mod']
    sh1_l, sc1_l, g1_l, sh2_l, sc2_l, g2_l = jnp.split(mod_l[:, None, :], 6, axis=-1)
    sh1_c, sc1_c, g1_c, sh2_c, sc2_c, g2_c = jnp.split(mod_c, 6, axis=-1)
    p_l = _split_cols(_modulate(_rmsnorm(x_l, lp['norm1_g']), sh1_l, sc1_l) @ lp['w_in'])
    p_c = _split_cols(_modulate(_rmsnorm(x_c, lp['norm1_g']), sh1_c, sc1_c) @ lp['w_in'])
    tables = _rope_tables(x_l.shape[1])

    k_c, v_c = _mla_keys(p_c[1], p_c[2], lp['mla_kv_norm_g'], None)
    k_l, v_l = _mla_keys(p_l[1], p_l[2], lp['mla_kv_norm_g'], tables)
    q_l = _mla_queries(p_l[0], lp['mla_q_norm_g'], lp['mla_w_uq'], lp['mla_w_uk'], tables)
    a_l = _mla_out(_block_attn(q_l, jnp.concatenate([k_c, k_l], axis=1), jnp.concatenate([v_c, v_l], axis=1), MLA_SCALE), lp['mla_w_uv'])

    nk_c, nv_c = _heads(p_c[4], N_HEADS_NAT), _heads(p_c[5], N_HEADS_NAT)
    b_l = _nat_latent(_heads(p_l[3], N_HEADS_NAT), _heads(p_l[4], N_HEADS_NAT), _heads(p_l[5], N_HEADS_NAT), nk_c, nv_c, lp['nat_rpb'])

    r_c, r_l = _retention(p_c[6:], p_l[6:], lp['ret_decay_f'], lp['ret_decay_b'], not last)

    x_l = x_l + g1_l * (_merge_groups((a_l, b_l, r_l), lp['out_norm_g']) @ lp['w_out'])
    x_l = x_l + g2_l * _ec_moe(_modulate(_rmsnorm(x_l, lp['norm2_g']), sh2_l, sc2_l),
                               lp['w_router'], lp['w_gate'], lp['w_up'], lp['w_down'])
    if last:
        return x_l, None

    q_c = _mla_queries(p_c[0], lp['mla_q_norm_g'], lp['mla_w_uq'], lp['mla_w_uk'], None)
    a_c = _mla_out(_block_attn(q_c, k_c, v_c, MLA_SCALE), lp['mla_w_uv'])
    b_c = _nat_context(_heads(p_c[3], N_HEADS_NAT), nk_c, nv_c)
    x_c = x_c + g1_c * (_merge_groups((a_c, b_c, r_c), lp['out_norm_g']) @ lp['w_out'])
    x_c = x_c + g2_c * _ec_moe(_modulate(_rmsnorm(x_c, lp['norm2_g']), sh2_c, sc2_c),
                               lp['w_router'], lp['w_gate'], lp['w_up'], lp['w_down'])
    return x_l, x_c


def setup_inputs(seed: int = 0) -> dict:
    key = jax.random.key(seed)
    ks = jax.random.split(key, 24)
    f32 = jnp.float32

    def nrm(k, shape, s):
        return jax.random.normal(k, shape, f32) * s

    def gain(k, shape):
        return 1.0 + 0.02 * jax.random.normal(k, shape, f32)

    base_logit = jnp.log(2.0 ** (5.0 + jnp.arange(N_HEADS_RET, dtype=f32)) - 1.0)
    return {
        'x': nrm(ks[0], (BATCH, SEQ, D_MODEL), 1.0),
        'c': nrm(ks[1], (BATCH, D_MODEL), 1.0),
        'ctx': nrm(ks[2], (BATCH, CTX_LEN, D_MODEL), 1.0),
        'c_ctx': nrm(ks[3], (D_MODEL,), 1.0),
        'w_mod': nrm(ks[4], (DEPTH, D_MODEL, 6 * D_MODEL), D_MODEL ** -0.5),
        'b_mod': nrm(ks[5], (DEPTH, 6 * D_MODEL), 0.01),
        'norm1_g': gain(ks[6], (DEPTH, D_MODEL)),
        'w_in': nrm(ks[7], (DEPTH, D_MODEL, IN_COLS), D_MODEL ** -0.5),
        'mla_q_norm_g': gain(ks[8], (DEPTH, MLA_Q_LORA)),
        'mla_kv_norm_g': gain(ks[9], (DEPTH, MLA_KV_LORA)),
        'mla_w_uq': nrm(ks[10], (DEPTH, MLA_Q_LORA, N_HEADS_MLA * (MLA_NOPE_DIM + MLA_ROPE_DIM)), MLA_Q_LORA ** -0.5),
        'mla_w_uk': nrm(ks[11], (DEPTH, MLA_KV_LORA, N_HEADS_MLA, MLA_NOPE_DIM), MLA_KV_LORA ** -0.5),
        'mla_w_uv': nrm(ks[12], (DEPTH, MLA_KV_LORA, N_HEADS_MLA, MLA_V_DIM), MLA_KV_LORA ** -0.5),
        'nat_rpb': nrm(ks[13], (DEPTH, N_HEADS_NAT, 2 * NAT_ROWS - 1, 2 * NAT_COLS - 1), 0.1),
        'ret_decay_f': base_logit + nrm(ks[14], (DEPTH, N_HEADS_RET), 0.1),
        'ret_decay_b': base_logit + nrm(ks[15], (DEPTH, N_HEADS_RET), 0.1),
        'out_norm_g': gain(ks[16], (DEPTH, D_MIX)),
        'w_out': nrm(ks[17], (DEPTH, D_MIX, D_MODEL), D_MIX ** -0.5),
        'norm2_g': gain(ks[18], (DEPTH, D_MODEL)),
        'w_router': nrm(ks[19], (DEPTH, D_MODEL, N_EXPERTS), D_MODEL ** -0.5),
        'w_gate': nrm(ks[20], (DEPTH, N_EXPERTS, D_MODEL, EXPERT_FF), D_MODEL ** -0.5),
        'w_up': nrm(ks[21], (DEPTH, N_EXPERTS, D_MODEL, EXPERT_FF), D_MODEL ** -0.5),
        'w_down': nrm(ks[22], (DEPTH, N_EXPERTS, EXPERT_FF, D_MODEL), EXPERT_FF ** -0.5),
        'final_norm_g': gain(ks[23], (D_MODEL,)),
    }


def reference(x, c, ctx, c_ctx, w_mod, b_mod, norm1_g, w_in, mla_q_norm_g, mla_kv_norm_g, mla_w_uq, mla_w_uk,
              mla_w_uv, nat_rpb, ret_decay_f, ret_decay_b, out_norm_g, w_out, norm2_g, w_router, w_gate, w_up,
              w_down, final_norm_g):
    x_l, x_c = x, ctx
    for i in range(DEPTH):
        lp = {
            'w_mod': w_mod[i], 'b_mod': b_mod[i], 'norm1_g': norm1_g[i], 'w_in': w_in[i],
            'mla_q_norm_g': mla_q_norm_g[i], 'mla_kv_norm_g': mla_kv_norm_g[i], 'mla_w_uq': mla_w_uq[i],
            'mla_w_uk': mla_w_uk[i], 'mla_w_uv': mla_w_uv[i], 'nat_rpb': nat_rpb[i],
            'ret_decay_f': ret_decay_f[i], 'ret_decay_b': ret_decay_b[i], 'out_norm_g': out_norm_g[i],
            'w_out': w_out[i], 'norm2_g': norm2_g[i], 'w_router': w_router[i], 'w_gate': w_gate[i],
            'w_up': w_up[i], 'w_down': w_down[i],
        }
        x_l, x_c = _layer(x_l, x_c, c, c_ctx, lp, i == DEPTH - 1)
    return _rmsnorm(x_l, final_norm_g)
```

```python
import functools

import numpy as np
import jax
import jax.numpy as jnp
from jax import lax
from jax.experimental import pallas as pl
from jax.experimental.pallas import tpu as pltpu

F32 = jnp.float32
BF16 = jnp.bfloat16

GRID_W = 64
HEAD_DIM = 128
N_HEADS_MLA = 8
N_HEADS_NAT = 4
N_HEADS_RET = 4
MLA_ROPE_DIM = 64
MLA_Q_LORA = 512
MLA_KV_LORA = 256
MLA_SCALE = (HEAD_DIM + MLA_ROPE_DIM) ** -0.5
MLA_QK_PAD = 256
NAT_ROWS = 8
NAT_COLS = 16
RET_CHUNK = 128
N_EXPERTS = 16
EC_CAPACITY = 2
ROPE_BASE = 10000.0
EPS = 1e-6
NEG_BIG = -1e30
LANE = 128

COL_CQ = 0
COL_CKV = 4
COL_KR = 6
COL_NAT_Q = 7
COL_NAT_K = 11
COL_NAT_V = 15
COL_RET_Q = 19
COL_RET_K = 23
COL_RET_V = 27
COL_RET_GF = 31
COL_RET_GB = 35
IN_COLS_PAD = 39 * LANE


def _cparams(sem, vmem_mib):
    return pltpu.CompilerParams(dimension_semantics=sem, vmem_limit_bytes=vmem_mib * 1024 * 1024)


def _silu(a):
    return a * (1.0 / (1.0 + jnp.exp(-a)))


def _rms(x):
    return x * lax.rsqrt(jnp.mean(x * x, axis=-1, keepdims=True) + EPS)


def _mod_kernel(ct_ref, w_ref, b_ref, o_ref, a_scr, *, n_rows, k_chunk):
    ct = ct_ref[...]
    a_scr[...] = _silu(ct)
    d = w_ref.shape[0]
    tn = w_ref.shape[1]

    def body(kc, accs):
        k0 = pl.multiple_of(kc * k_chunk, k_chunk)
        wblk = w_ref[pl.ds(k0, k_chunk), :]
        out = []
        for r in range(n_rows):
            col = a_scr[pl.ds(k0, k_chunk), r:r + 1]
            out.append(accs[r] + jnp.sum((col * wblk).reshape(k_chunk // 8, 8, tn), axis=0))
        return tuple(out)

    accs = lax.fori_loop(0, d // k_chunk, body, tuple(jnp.zeros((8, tn), F32) for _ in range(n_rows)))
    o_ref[...] = jnp.zeros(o_ref.shape, F32)
    for r in range(n_rows):
        o_ref[r:r + 1, :] = jnp.sum(accs[r], axis=0, keepdims=True) + b_ref[...]


def _modulation(cvecs, w_mod, b_mod):
    n_rows, d = cvecs.shape
    n_out = w_mod.shape[1]
    tn = 1024
    ct = jnp.zeros((d, 8), F32).at[:, :n_rows].set(cvecs.T)
    return pl.pallas_call(
        functools.partial(_mod_kernel, n_rows=n_rows, k_chunk=64),
        grid=(n_out // tn,),
        in_specs=[pl.BlockSpec((d, 8), lambda j: (0, 0)),
                  pl.BlockSpec((d, tn), lambda j: (0, j)),
                  pl.BlockSpec((1, tn), lambda j: (0, j))],
        out_specs=pl.BlockSpec((8, tn), lambda j: (0, j)),
        out_shape=jax.ShapeDtypeStruct((8, n_out), F32),
        scratch_shapes=[pltpu.VMEM((d, 8), F32)],
        compiler_params=_cparams(("parallel",), 40),
        name="modulation",
    )(ct, w_mod, b_mod.reshape(1, n_out))


def _norm_proj_kernel(x_ref, g_ref, sh_ref, sc_ref, w_ref, o_ref, h_scr):
    @pl.when(pl.program_id(2) == 0)
    def _():
        y = _rms(x_ref[...]) * g_ref[...]
        h_scr[...] = (y * (1.0 + sc_ref[...]) + sh_ref[...]).astype(BF16)

    o_ref[...] = jnp.dot(h_scr[...], w_ref[...], preferred_element_type=F32)


def _norm_proj(x, g, shift, scale, w):
    b, n, d = x.shape
    c = w.shape[1]
    tm = min(n, 512)
    tn = c // 3 if (c % (3 * LANE) == 0) else c
    return pl.pallas_call(
        _norm_proj_kernel,
        grid=(b, n // tm, c // tn),
        in_specs=[pl.BlockSpec((None, tm, d), lambda bb, i, j: (bb, i, 0)),
                  pl.BlockSpec((1, d), lambda bb, i, j: (0, 0)),
                  pl.BlockSpec((None, 1, d), lambda bb, i, j: (bb, 0, 0)),
                  pl.BlockSpec((None, 1, d), lambda bb, i, j: (bb, 0, 0)),
                  pl.BlockSpec((d, tn), lambda bb, i, j: (0, j))],
        out_specs=pl.BlockSpec((None, tm, tn), lambda bb, i, j: (bb, i, j)),
        out_shape=jax.ShapeDtypeStruct((b, n, c), F32),
        scratch_shapes=[pltpu.VMEM((tm, d), BF16)],
        compiler_params=_cparams(("parallel", "parallel", "arbitrary"), 48),
        name="norm_proj",
    )(x, g.reshape(1, d), shift, scale, w)


def _rope_rotate(r, cos, sin):
    return r * cos + pltpu.roll(r, 64, 1) * sin


def _mla_q_kernel(cq_ref, g_ref, w_ref, cos_ref, sin_ref, o_ref, *, n_heads):
    cqn = (_rms(cq_ref[...]) * g_ref[...]).astype(BF16)
    qe = jnp.dot(cqn, w_ref[...], preferred_element_type=F32)
    cos = cos_ref[...]
    sin = sin_ref[...]
    for h in range(n_heads):
        c0 = h * MLA_QK_PAD
        o_ref[:, c0:c0 + LANE] = (qe[:, c0:c0 + LANE] * MLA_SCALE).astype(BF16)
        rot = _rope_rotate(qe[:, c0 + LANE:c0 + 2 * LANE], cos, sin)
        o_ref[:, c0 + LANE:c0 + 2 * LANE] = (rot * MLA_SCALE).astype(BF16)


def _mla_q_prep(p, g, w_q, cos, sin):
    b, n, _ = p.shape
    tm = min(n, 512)
    cw = N_HEADS_MLA * MLA_QK_PAD
    return pl.pallas_call(
        functools.partial(_mla_q_kernel, n_heads=N_HEADS_MLA),
        grid=(b, n // tm),
        in_specs=[pl.BlockSpec((None, tm, MLA_Q_LORA), lambda bb, i: (bb, i, COL_CQ * LANE // MLA_Q_LORA)),
                  pl.BlockSpec((1, MLA_Q_LORA), lambda bb, i: (0, 0)),
                  pl.BlockSpec((MLA_Q_LORA, cw), lambda bb, i: (0, 0)),
                  pl.BlockSpec((tm, LANE), lambda bb, i: (i, 0)),
                  pl.BlockSpec((tm, LANE), lambda bb, i: (i, 0))],
        out_specs=pl.BlockSpec((None, tm, cw), lambda bb, i: (bb, i, 0)),
        out_shape=jax.ShapeDtypeStruct((b, n, cw), BF16),
        compiler_params=_cparams(("parallel", "parallel"), 40),
        name="mla_q_prep",
    )(p, g.reshape(1, MLA_Q_LORA), w_q, cos, sin)


def _mla_kv_kernel(ckv_ref, kr_ref, g_ref, wuk_ref, wuv_ref, cos_ref, sin_ref, k_ref, v_ref, *, n_heads):
    ckvn = (_rms(ckv_ref[...]) * g_ref[...]).astype(BF16)
    kn = jnp.dot(ckvn, wuk_ref[...], preferred_element_type=F32)
    v_ref[...] = jnp.dot(ckvn, wuv_ref[...], preferred_element_type=F32).astype(BF16)
    rot = _rope_rotate(kr_ref[...], cos_ref[...], sin_ref[...]).astype(BF16)
    for h in range(n_heads):
        c0 = h * MLA_QK_PAD
        k_ref[:, c0:c0 + LANE] = kn[:, h * LANE:(h + 1) * LANE].astype(BF16)
        k_ref[:, c0 + LANE:c0 + 2 * LANE] = rot


def _mla_kv_prep(p, g, w_uk, w_uv, cos, sin):
    b, n, _ = p.shape
    tm = min(n, 512)
    kw = N_HEADS_MLA * MLA_QK_PAD
    vw = N_HEADS_MLA * HEAD_DIM
    return pl.pallas_call(
        functools.partial(_mla_kv_kernel, n_heads=N_HEADS_MLA),
        grid=(b, n // tm),
        in_specs=[pl.BlockSpec((None, tm, MLA_KV_LORA), lambda bb, i: (bb, i, COL_CKV * LANE // MLA_KV_LORA)),
                  pl.BlockSpec((None, tm, LANE), lambda bb, i: (bb, i, COL_KR)),
                  pl.BlockSpec((1, MLA_KV_LORA), lambda bb, i: (0, 0)),
                  pl.BlockSpec((MLA_KV_LORA, vw), lambda bb, i: (0, 0)),
                  pl.BlockSpec((MLA_KV_LORA, vw), lambda bb, i: (0, 0)),
                  pl.BlockSpec((tm, LANE), lambda bb, i: (i, 0)),
                  pl.BlockSpec((tm, LANE), lambda bb, i: (i, 0))],
        out_specs=[pl.BlockSpec((None, tm, kw), lambda bb, i: (bb, i, 0)),
                   pl.BlockSpec((None, tm, vw), lambda bb, i: (bb, i, 0))],
        out_shape=[jax.ShapeDtypeStruct((b, n, kw), BF16), jax.ShapeDtypeStruct((b, n, vw), BF16)],
        compiler_params=_cparams(("parallel", "parallel"), 40),
        name="mla_kv_prep",
    )(p, p, g.reshape(1, MLA_KV_LORA), w_uk, w_uv, cos, sin)


def _attn_kernel(*refs, n_src, scale, tk_max):
    q_ref = refs[0]
    o_ref = refs[1 + 2 * n_src]
    q = q_ref[...]
    if scale != 1.0:
        q = q * scale
    q = q.astype(BF16)
    tq = q.shape[0]
    dv = o_ref.shape[-1]

    def step(k, v, carry):
        m, l, acc = carry
        s = lax.dot_general(q, k, (((1,), (1,)), ((), ())), preferred_element_type=F32)
        m_new = jnp.maximum(m, jnp.max(s, axis=-1, keepdims=True))
        p = jnp.exp(s - m_new)
        alpha = jnp.exp(m - m_new)
        l = alpha * l + jnp.sum(p, axis=-1, keepdims=True)
        acc = alpha * acc + jnp.dot(p.astype(BF16), v, preferred_element_type=F32)
        return m_new, l, acc

    carry = (jnp.full((tq, 1), NEG_BIG, F32), jnp.zeros((tq, 1), F32), jnp.zeros((tq, dv), F32))
    for s_i in range(n_src):
        k_ref = refs[1 + 2 * s_i]
        v_ref = refs[2 + 2 * s_i]
        nk = k_ref.shape[0]
        tk = min(nk, tk_max)
        if nk == tk:
            carry = step(k_ref[...].astype(BF16), v_ref[...].astype(BF16), carry)
        else:
            def body(c, cr, k_ref=k_ref, v_ref=v_ref, tk=tk):
                k0 = pl.multiple_of(c * tk, tk)
                return step(k_ref[pl.ds(k0, tk), :].astype(BF16), v_ref[pl.ds(k0, tk), :].astype(BF16), cr)
            carry = lax.fori_loop(0, nk // tk, body, carry)
    _, l, acc = carry
    o_ref[...] = acc / l


def _attention(q_arr, q_blk0, dq, sources, dv, n_heads, scale, tq=256, tk_max=512):
    b, nq, _ = q_arr.shape
    tq = min(nq, tq)
    in_specs = [pl.BlockSpec((None, tq, dq), lambda bb, h, i: (bb, i, q_blk0 + h))]
    args = [q_arr]
    for k_arr, k_blk0, v_arr, v_blk0 in sources:
        nk = k_arr.shape[1]
        in_specs.append(pl.BlockSpec((None, nk, dq), lambda bb, h, i, o=k_blk0: (bb, 0, o + h)))
        in_specs.append(pl.BlockSpec((None, nk, dv), lambda bb, h, i, o=v_blk0: (bb, 0, o + h)))
        args += [k_arr, v_arr]
    return pl.pallas_call(
        functools.partial(_attn_kernel, n_src=len(sources), scale=scale, tk_max=tk_max),
        grid=(b, n_heads, nq // tq),
        in_specs=in_specs,
        out_specs=pl.BlockSpec((None, tq, dv), lambda bb, h, i: (bb, i, h)),
        out_shape=jax.ShapeDtypeStruct((b, nq, n_heads * dv), F32),
        compiler_params=_cparams(("parallel", "parallel", "arbitrary"), 48),
        name="attention",
    )(*args)


def _nat_kernel(q_ref, k_ref, v_ref, kc_ref, vc_ref, bias_ref, o_ref, *, rows_per_step, n_rows):
    i = pl.program_id(2)
    scale = HEAD_DIM ** -0.5
    kc = kc_ref[...].astype(BF16)
    vc = vc_ref[...].astype(BF16)
    nt = (((1,), (1,)), ((), ()))
    win = NAT_ROWS * GRID_W
    for rr in range(rows_per_step):
        r = i * rows_per_step + rr
        r0 = jnp.clip(r - NAT_ROWS // 2, 0, n_rows - NAT_ROWS)
        base = r0 - r + NAT_ROWS - 1
        t0 = pl.multiple_of(r0 * GRID_W, GRID_W)
        q = (q_ref[rr * GRID_W:(rr + 1) * GRID_W, :] * scale).astype(BF16)
        ks = k_ref[pl.ds(t0, win), :].astype(BF16)
        vs = v_ref[pl.ds(t0, win), :].astype(BF16)
        s_w = lax.dot_general(q, ks, nt, preferred_element_type=F32) + bias_ref[base]
        s_c = lax.dot_general(q, kc, nt, preferred_element_type=F32)
        m = jnp.maximum(jnp.max(s_w, axis=-1, keepdims=True), jnp.max(s_c, axis=-1, keepdims=True))
        p_w = jnp.exp(s_w - m)
        p_c = jnp.exp(s_c - m)
        l = jnp.sum(p_w, axis=-1, keepdims=True) + jnp.sum(p_c, axis=-1, keepdims=True)
        o = (jnp.dot(p_w.astype(BF16), vs, preferred_element_type=F32)
             + jnp.dot(p_c.astype(BF16), vc, preferred_element_type=F32))
        o_ref[rr * GRID_W:(rr + 1) * GRID_W, :] = o / l


def _nat_bias_table(rpb):
    col = jnp.arange(GRID_W)
    c0 = jnp.clip(col - NAT_COLS // 2, 0, GRID_W - NAT_COLS)
    kc = jnp.arange(GRID_W)
    inside = (kc[None, :] >= c0[:, None]) & (kc[None, :] < c0[:, None] + NAT_COLS)
    dc = jnp.clip(kc[None, :] - col[:, None] + NAT_COLS - 1, 0, 2 * NAT_COLS - 2)
    per_dr = jnp.where(inside[None, None], rpb[:, :, dc], NEG_BIG)
    dr = jnp.arange(NAT_ROWS)[:, None] + jnp.arange(NAT_ROWS)[None, :]
    tbl = per_dr[:, dr]
    h = rpb.shape[0]
    return tbl.transpose(0, 1, 3, 2, 4).reshape(h, NAT_ROWS, GRID_W, NAT_ROWS * GRID_W).astype(F32)


def _nat_latent(p_l, p_c, bias_tbl):
    b, n, _ = p_l.shape
    n_ctx = p_c.shape[1]
    n_rows = n // GRID_W
    rps = 8
    tq = rps * GRID_W
    win = NAT_ROWS * GRID_W
    return pl.pallas_call(
        functools.partial(_nat_kernel, rows_per_step=rps, n_rows=n_rows),
        grid=(b, N_HEADS_NAT, n_rows // rps),
        in_specs=[pl.BlockSpec((None, tq, LANE), lambda bb, h, i: (bb, i, COL_NAT_Q + h)),
                  pl.BlockSpec((None, n, LANE), lambda bb, h, i: (bb, 0, COL_NAT_K + h)),
                  pl.BlockSpec((None, n, LANE), lambda bb, h, i: (bb, 0, COL_NAT_V + h)),
                  pl.BlockSpec((None, n_ctx, LANE), lambda bb, h, i: (bb, 0, COL_NAT_K + h)),
                  pl.BlockSpec((None, n_ctx, LANE), lambda bb, h, i: (bb, 0, COL_NAT_V + h)),
                  pl.BlockSpec((None, NAT_ROWS, GRID_W, win), lambda bb, h, i: (h, 0, 0, 0))],
        out_specs=pl.BlockSpec((None, tq, LANE), lambda bb, h, i: (bb, i, h)),
        out_shape=jax.ShapeDtypeStruct((b, n, N_HEADS_NAT * HEAD_DIM), F32),
        compiler_params=_cparams(("parallel", "parallel", "arbitrary"), 48),
        name="nat_latent",
    )(p_l, p_l, p_l, p_c, p_c, bias_tbl)


def _ret_kernel(*refs, reverse, n_chunks, add_prev):
    if add_prev:
        logit_ref, q_ref, k_ref, v_ref, g_ref, s0_ref, prev_ref, y_ref, s_ref = refs
    else:
        logit_ref, q_ref, k_ref, v_ref, g_ref, s0_ref, y_ref, s_ref = refs
        prev_ref = None
    h = pl.program_id(1)
    c_len = RET_CHUNK

    @pl.when(pl.program_id(2) == 0)
    def _():
        s_ref[...] = s0_ref[...]

    z = jnp.full((c_len, c_len), -logit_ref[h], F32)
    lg = -(jnp.maximum(z, 0.0) + jnp.log1p(jnp.exp(-jnp.abs(z))))
    ci = lax.broadcasted_iota(jnp.int32, (c_len, c_len), 0).astype(F32)
    si = lax.broadcasted_iota(jnp.int32, (c_len, c_len), 1).astype(F32)
    if reverse:
        rel = si - ci
        wq = jnp.exp(lg * (c_len - ci))
        wk = jnp.exp(lg * ci)
    else:
        rel = ci - si
        wq = jnp.exp(lg * (ci + 1.0))
        wk = jnp.exp(lg * (c_len - 1.0 - ci))
    dmat = jnp.where(rel >= 0, jnp.exp(lg * jnp.maximum(rel, 0.0)), 0.0)
    g_chunk = jnp.exp(lg * c_len)
    k_scale = HEAD_DIM ** -0.5
    nt = (((1,), (1,)), ((), ()))

    state = s_ref[...]
    order = range(n_chunks - 1, -1, -1) if reverse else range(n_chunks)
    for c in order:
        sl = slice(c * c_len, (c + 1) * c_len)
        qc = q_ref[sl, :]
        kc = k_ref[sl, :] * k_scale
        vb = v_ref[sl, :].astype(BF16)
        sc = lax.dot_general(qc.astype(BF16), kc.astype(BF16), nt, preferred_element_type=F32) * dmat
        o = (jnp.dot(sc.astype(BF16), vb, preferred_element_type=F32)
             + jnp.dot((qc * wq).astype(BF16), state.astype(BF16), preferred_element_type=F32))
        state = g_chunk * state + jnp.dot((kc * wk).T.astype(BF16), vb, preferred_element_type=F32)
        mu = jnp.mean(o, axis=-1, keepdims=True)
        d = o - mu
        gn = d * lax.rsqrt(jnp.mean(d * d, axis=-1, keepdims=True) + EPS)
        y = _silu(g_ref[sl, :]) * gn
        if add_prev:
            y = y + prev_ref[sl, :]
        y_ref[sl, :] = y
    s_ref[...] = state


def _retention_dir(p, logits, s0, prev, reverse):
    b, n, _ = p.shape
    hh = N_HEADS_RET
    tb = min(n, 4 * RET_CHUNK)
    nblk = n // tb
    gate_col = COL_RET_GB if reverse else COL_RET_GF

    def row(i):
        return (nblk - 1 - i) if reverse else i

    def col_spec(c0):
        return pl.BlockSpec((None, tb, LANE), lambda bb, h, i: (bb, row(i), c0 + h))

    in_specs = [pl.BlockSpec(memory_space=pltpu.SMEM),
                col_spec(COL_RET_Q), col_spec(COL_RET_K), col_spec(COL_RET_V), col_spec(gate_col),
                pl.BlockSpec((None, None, HEAD_DIM, HEAD_DIM), lambda bb, h, i: (bb, h, 0, 0))]
    args = [logits.astype(F32), p, p, p, p, s0]
    if prev is not None:
        in_specs.append(pl.BlockSpec((None, tb, LANE), lambda bb, h, i: (bb, row(i), h)))
        args.append(prev)
    return pl.pallas_call(
        functools.partial(_ret_kernel, reverse=reverse, n_chunks=tb // RET_CHUNK, add_prev=prev is not None),
        grid=(b, hh, nblk),
        in_specs=in_specs,
        out_specs=[pl.BlockSpec((None, tb, LANE), lambda bb, h, i: (bb, row(i), h)),
                   pl.BlockSpec((None, None, HEAD_DIM, HEAD_DIM), lambda bb, h, i: (bb, h, 0, 0))],
        out_shape=[jax.ShapeDtypeStruct((b, n, hh * HEAD_DIM), F32),
                   jax.ShapeDtypeStruct((b, hh, HEAD_DIM, HEAD_DIM), F32)],
        compiler_params=_cparams(("parallel", "parallel", "arbitrary"), 32),
        name="retention_bwd" if reverse else "retention_fwd",
    )(*args)


def _retention(p_c, p_l, logit_f, logit_b):
    b = p_c.shape[0]
    s0 = jnp.zeros((b, N_HEADS_RET, HEAD_DIM, HEAD_DIM), F32)
    yc_f, s_cf = _retention_dir(p_c, logit_f, s0, None, False)
    y_c, s_cb = _retention_dir(p_c, logit_b, s0, yc_f, True)
    yl_f, _ = _retention_dir(p_l, logit_f, s_cf, None, False)
    y_l, _ = _retention_dir(p_l, logit_b, s_cb, yl_f, True)
    return y_c, y_l


def _merge_kernel(x_ref, ya_ref, yb_ref, yr_ref, g_ref, gate_ref, w_ref, o_ref):
    wa = ya_ref.shape[-1]
    wb = yb_ref.shape[-1]
    na = (_rms(ya_ref[...]) * g_ref[:, :wa]).astype(BF16)
    nb = (_rms(yb_ref[...]) * g_ref[:, wa:wa + wb]).astype(BF16)
    nr = (_rms(yr_ref[...]) * g_ref[:, wa + wb:]).astype(BF16)
    acc = jnp.dot(na, w_ref[:wa, :], preferred_element_type=F32)
    acc = acc + jnp.dot(nb, w_ref[wa:wa + wb, :], preferred_element_type=F32)
    acc = acc + jnp.dot(nr, w_ref[wa + wb:, :], preferred_element_type=F32)
    o_ref[...] = x_ref[...] + gate_ref[...] * acc


def _merge_out(x, ya, yb, yr, g, gate, w_out):
    b, n, d = x.shape
    tm = min(n, 256)
    dm = w_out.shape[0]

    def tok(width):
        return pl.BlockSpec((None, tm, width), lambda bb, i: (bb, i, 0))

    return pl.pallas_call(
        _merge_kernel,
        grid=(b, n // tm),
        in_specs=[tok(d), tok(ya.shape[-1]), tok(yb.shape[-1]), tok(yr.shape[-1]),
                  pl.BlockSpec((1, dm), lambda bb, i: (0, 0)),
                  pl.BlockSpec((None, 1, d), lambda bb, i: (bb, 0, 0)),
                  pl.BlockSpec((dm, d), lambda bb, i: (0, 0))],
        out_specs=tok(d),
        out_shape=jax.ShapeDtypeStruct((b, n, d), F32),
        compiler_params=_cparams(("parallel", "parallel"), 48),
        name="merge_out",
    )(x, ya, yb, yr, g.reshape(1, dm), gate, w_out)


def _moe_pre_kernel(x_ref, g_ref, sh_ref, sc_ref, wr_ref, h_ref, aff_ref, *, n_experts):
    h = (_rms(x_ref[...]) * g_ref[...]) * (1.0 + sc_ref[...]) + sh_ref[...]
    h_ref[...] = h.astype(BF16)
    h_hi = h.astype(BF16)
    h_lo = (h - h_hi.astype(F32)).astype(BF16)
    w = wr_ref[...]
    w_hi = w.astype(BF16)
    w_lo = (w - w_hi.astype(F32)).astype(BF16)
    logits = (jnp.dot(h_hi, w_hi, preferred_element_type=F32)
              + (jnp.dot(h_hi, w_lo, preferred_element_type=F32) + jnp.dot(h_lo, w_hi, preferred_element_type=F32)))
    lane = lax.broadcasted_iota(jnp.int32, logits.shape, 1)
    logits = jnp.where(lane < n_experts, logits, NEG_BIG)
    e = jnp.exp(logits - jnp.max(logits, axis=-1, keepdims=True))
    aff_ref[...] = e / jnp.sum(e, axis=-1, keepdims=True)


def _moe_pre(x, g, shift, scale, w_router_pad):
    b, n, d = x.shape
    tm = min(n, 256)
    return pl.pallas_call(
        functools.partial(_moe_pre_kernel, n_experts=N_EXPERTS),
        grid=(b, n // tm),
        in_specs=[pl.BlockSpec((None, tm, d), lambda bb, i: (bb, i, 0)),
                  pl.BlockSpec((1, d), lambda bb, i: (0, 0)),
                  pl.BlockSpec((None, 1, d), lambda bb, i: (bb, 0, 0)),
                  pl.BlockSpec((None, 1, d), lambda bb, i: (bb, 0, 0)),
                  pl.BlockSpec((d, LANE), lambda bb, i: (0, 0))],
        out_specs=[pl.BlockSpec((None, tm, d), lambda bb, i: (bb, i, 0)),
                   pl.BlockSpec((None, tm, LANE), lambda bb, i: (bb, i, 0))],
        out_shape=[jax.ShapeDtypeStruct((b, n, d), BF16), jax.ShapeDtypeStruct((b, n, LANE), F32)],
        compiler_params=_cparams(("parallel", "parallel"), 40),
        name="moe_pre",
    )(x, g.reshape(1, d), shift, scale, w_router_pad)


def _ffn_kernel(xs_ref, gate_ref, wg_ref, wu_ref, wd_ref, o_ref):
    xs = xs_ref[...]
    a = jnp.dot(xs, wg_ref[...], preferred_element_type=F32)
    u = jnp.dot(xs, wu_ref[...], preferred_element_type=F32)
    hm = (_silu(a) * u).astype(BF16)
    o_ref[...] = jnp.dot(hm, wd_ref[...], preferred_element_type=F32) * gate_ref[...]


def _expert_ffn(xs, gates, w_gate, w_up, w_down):
    e, t, d = xs.shape
    f = w_gate.shape[-1]
    tm = min(t, 512)
    return pl.pallas_call(
        _ffn_kernel,
        grid=(e, t // tm),
        in_specs=[pl.BlockSpec((None, tm, d), lambda ee, i: (ee, i, 0)),
                  pl.BlockSpec((None, tm, 1), lambda ee, i: (ee, i, 0)),
                  pl.BlockSpec((None, d, f), lambda ee, i: (ee, 0, 0)),
                  pl.BlockSpec((None, d, f), lambda ee, i: (ee, 0, 0)),
                  pl.BlockSpec((None, f, d), lambda ee, i: (ee, 0, 0))],
        out_specs=pl.BlockSpec((None, tm, d), lambda ee, i: (ee, i, 0)),
        out_shape=jax.ShapeDtypeStruct((e, t, d), F32),
        compiler_params=_cparams(("parallel", "arbitrary"), 56),
        name="expert_ffn",
    )(xs, gates, w_gate, w_up, w_down)


def _residual_kernel(x_ref, m_ref, gate_ref, gn_ref, o_ref, *, final_norm):
    y = x_ref[...] + gate_ref[...] * m_ref[...]
    if final_norm:
        y = _rms(y) * gn_ref[...]
    o_ref[...] = y


def _gated_residual(x, moe, gate, norm_g, final_norm):
    b, n, d = x.shape
    tm = min(n, 512)
    tok = pl.BlockSpec((None, tm, d), lambda bb, i: (bb, i, 0))
    return pl.pallas_call(
        functools.partial(_residual_kernel, final_norm=final_norm),
        grid=(b, n // tm),
        in_specs=[tok, tok,
                  pl.BlockSpec((None, 1, d), lambda bb, i: (bb, 0, 0)),
                  pl.BlockSpec((1, d), lambda bb, i: (0, 0))],
        out_specs=tok,
        out_shape=jax.ShapeDtypeStruct((b, n, d), F32),
        compiler_params=_cparams(("parallel", "parallel"), 40),
        name="gated_residual",
    )(x, moe, gate, norm_g.reshape(1, d))


def _ec_moe(x, g, shift, scale, w_router_pad, w_gate, w_up, w_down):
    b, n, d = x.shape
    cap = EC_CAPACITY * n // N_EXPERTS
    h, aff = _moe_pre(x, g, shift, scale, w_router_pad)
    gate, idx = lax.top_k(jnp.swapaxes(aff[..., :N_EXPERTS], 1, 2), cap)
    flat = (idx + (jnp.arange(b, dtype=idx.dtype) * n)[:, None, None]).transpose(1, 0, 2).reshape(-1)
    xs = jnp.take(h.reshape(b * n, d), flat, axis=0).reshape(N_EXPERTS, b * cap, d)
    gates = gate.transpose(1, 0, 2).reshape(N_EXPERTS, b * cap, 1)
    y = _expert_ffn(xs, gates, w_gate, w_up, w_down)
    out = jnp.zeros((b * n, d), F32).at[flat].add(y.reshape(-1, d))
    return out.reshape(b, n, d)


_ROPE_SRC = np.concatenate([np.arange(16, 32), np.arange(0, 16), np.arange(48, 64), np.arange(32, 48)])
_ROPE_SIGN = np.concatenate([-np.ones(16), np.ones(16), -np.ones(16), np.ones(16)]).astype(np.float32)


def _rope_perm_cols(w):
    return w[:, _ROPE_SRC] * _ROPE_SIGN


def _pad_w_in(w_in):
    c_kr = MLA_Q_LORA + MLA_KV_LORA
    kr = w_in[:, c_kr:c_kr + MLA_ROPE_DIM]
    return jnp.concatenate([w_in[:, :c_kr + MLA_ROPE_DIM], _rope_perm_cols(kr), w_in[:, c_kr + MLA_ROPE_DIM:]],
                           axis=1).astype(BF16)


def _expand_w_uq(w_uq):
    w = w_uq.reshape(MLA_Q_LORA, N_HEADS_MLA, HEAD_DIM + MLA_ROPE_DIM)
    nope = w[:, :, :HEAD_DIM]
    rope = w[:, :, HEAD_DIM:]
    partner = rope[:, :, _ROPE_SRC] * _ROPE_SIGN
    return jnp.concatenate([nope, rope, partner], axis=-1).reshape(MLA_Q_LORA, N_HEADS_MLA * MLA_QK_PAD).astype(BF16)


def _rope_tables(n):
    t = jnp.arange(n)
    n_freq = MLA_ROPE_DIM // 4
    inv = ROPE_BASE ** (-jnp.arange(n_freq, dtype=F32) / n_freq)
    ang_r = (t // GRID_W).astype(F32)[:, None] * inv
    ang_c = (t % GRID_W).astype(F32)[:, None] * inv
    zeros = jnp.zeros((n, LANE - MLA_ROPE_DIM), F32)
    cos = jnp.concatenate([jnp.cos(ang_r), jnp.cos(ang_r), jnp.cos(ang_c), jnp.cos(ang_c), zeros], axis=1)
    sin = jnp.concatenate([jnp.sin(ang_r), jnp.sin(ang_r), jnp.sin(ang_c), jnp.sin(ang_c), zeros], axis=1)
    return cos, sin


def _identity_tables(n):
    ones = jnp.ones((n, MLA_ROPE_DIM), F32)
    zeros = jnp.zeros((n, LANE - MLA_ROPE_DIM), F32)
    return jnp.concatenate([ones, zeros], axis=1), jnp.zeros((n, LANE), F32)


def _layer(x_l, x_c, mod, lp, last, final_g):
    b, n, d = x_l.shape
    n_ctx = x_c.shape[1]

    def chunk(k, ctx):
        rows = jnp.broadcast_to(mod[b, k * d:(k + 1) * d], (b, d)) if ctx else mod[:b, k * d:(k + 1) * d]
        return rows.reshape(b, 1, d)

    w_in = _pad_w_in(lp['w_in'])
    p_l = _norm_proj(x_l, lp['norm1_g'], chunk(0, False), chunk(1, False), w_in)
    p_c = _norm_proj(x_c, lp['norm1_g'], chunk(0, True), chunk(1, True), w_in)

    cos_l, sin_l = _rope_tables(n)
    cos_c, sin_c = _identity_tables(n_ctx)
    w_uk = lp['mla_w_uk'].reshape(MLA_KV_LORA, -1).astype(BF16)
    w_uv = lp['mla_w_uv'].reshape(MLA_KV_LORA, -1).astype(BF16)
    w_q = _expand_w_uq(lp['mla_w_uq'])
    k_l, v_l = _mla_kv_prep(p_l, lp['mla_kv_norm_g'], w_uk, w_uv, cos_l, sin_l)
    k_c, v_c = _mla_kv_prep(p_c, lp['mla_kv_norm_g'], w_uk, w_uv, cos_c, sin_c)
    q_l = _mla_q_prep(p_l, lp['mla_q_norm_g'], w_q, cos_l, sin_l)
    a_l = _attention(q_l, 0, MLA_QK_PAD, [(k_c, 0, v_c, 0), (k_l, 0, v_l, 0)], HEAD_DIM, N_HEADS_MLA, 1.0)

    b_l = _nat_latent(p_l, p_c, _nat_bias_table(lp['nat_rpb']))
    r_c, r_l = _retention(p_c, p_l, lp['ret_decay_f'], lp['ret_decay_b'])

    w_out = lp['w_out'].astype(BF16)
    w_router = jnp.zeros((d, LANE), F32).at[:, :N_EXPERTS].set(lp['w_router'])
    w_gate = lp['w_gate'].astype(BF16)
    w_up = lp['w_up'].astype(BF16)
    w_down = lp['w_down'].astype(BF16)

    x_l = _merge_out(x_l, a_l, b_l, r_l, lp['out_norm_g'], chunk(2, False), w_out)
    moe_l = _ec_moe(x_l, lp['norm2_g'], chunk(3, False), chunk(4, False), w_router, w_gate, w_up, w_down)
    x_l = _gated_residual(x_l, moe_l, chunk(5, False), final_g, last)
    if last:
        return x_l, None

    q_c = _mla_q_prep(p_c, lp['mla_q_norm_g'], w_q, cos_c, sin_c)
    a_c = _attention(q_c, 0, MLA_QK_PAD, [(k_c, 0, v_c, 0)], HEAD_DIM, N_HEADS_MLA, 1.0)
    b_c = _attention(p_c, COL_NAT_Q, HEAD_DIM, [(p_c, COL_NAT_K, p_c, COL_NAT_V)], HEAD_DIM, N_HEADS_NAT,
                     HEAD_DIM ** -0.5)
    x_c = _merge_out(x_c, a_c, b_c, r_c, lp['out_norm_g'], chunk(2, True), w_out)
    moe_c = _ec_moe(x_c, lp['norm2_g'], chunk(3, True), chunk(4, True), w_router, w_gate, w_up, w_down)
    x_c = _gated_residual(x_c, moe_c, chunk(5, True), final_g, False)
    return x_l, x_c


def kernel(x, c, ctx, c_ctx, w_mod, b_mod, norm1_g, w_in, mla_q_norm_g, mla_kv_norm_g, mla_w_uq, mla_w_uk,
           mla_w_uv, nat_rpb, ret_decay_f, ret_decay_b, out_norm_g, w_out, norm2_g, w_router, w_gate, w_up,
           w_down, final_norm_g):
    depth = w_mod.shape[0]
    cvecs = jnp.concatenate([c, c_ctx[None, :]], axis=0)
    x_l, x_c = x, ctx
    for i in range(depth):
        lp = {
            'norm1_g': norm1_g[i], 'w_in': w_in[i], 'mla_q_norm_g': mla_q_norm_g[i],
            'mla_kv_norm_g': mla_kv_norm_g[i], 'mla_w_uq': mla_w_uq[i], 'mla_w_uk': mla_w_uk[i],
            'mla_w_uv': mla_w_uv[i], 'nat_rpb': nat_rpb[i], 'ret_decay_f': ret_decay_f[i],
            'ret_decay_b': ret_decay_b[i], 'out_norm_g': out_norm_g[i], 'w_out': w_out[i],
            'norm2_g': norm2_g[i], 'w_router': w_router[i], 'w_gate': w_gate[i], 'w_up': w_up[i],
            'w_down': w_down[i],
        }
        mod = _modulation(cvecs, w_mod[i], b_mod[i])
        x_l, x_c = _layer(x_l, x_c, mod, lp, i == depth - 1, final_norm_g)
    return x_l
```

```python
import functools

import numpy as np
import jax
import jax.numpy as jnp
from jax import lax
from jax.experimental import pallas as pl
from jax.experimental.pallas import tpu as pltpu

F32 = jnp.float32
BF16 = jnp.bfloat16

GRID_W = 64
HEAD_DIM = 128
N_HEADS_MLA = 8
N_HEADS_NAT = 4
N_HEADS_RET = 4
MLA_ROPE_DIM = 64
MLA_Q_LORA = 512
MLA_KV_LORA = 256
MLA_SCALE = (HEAD_DIM + MLA_ROPE_DIM) ** -0.5
LOG2E = 1.4426950408889634
MLA_Q_SCALE = MLA_SCALE * LOG2E
MLA_QK_PAD = 256
NAT_ROWS = 8
NAT_COLS = 16
RET_CHUNK = 128
N_EXPERTS = 16
EC_CAPACITY = 2
ROPE_BASE = 10000.0
EPS = 1e-6
NEG_BIG = -1e30
LANE = 128

COL_CQ = 0
COL_CKV = 4
COL_KR = 6
COL_NAT_Q = 7
COL_NAT_K = 11
COL_NAT_V = 15
COL_RET_Q = 19
COL_RET_K = 23
COL_RET_V = 27
COL_RET_GF = 31
COL_RET_GB = 35
IN_COLS_PAD = 39 * LANE


def _cparams(sem, vmem_mib):
    return pltpu.CompilerParams(dimension_semantics=sem, vmem_limit_bytes=vmem_mib * 1024 * 1024)


def _silu(a):
    return a * (1.0 / (1.0 + jnp.exp(-a)))


def _rms(x):
    return x * lax.rsqrt(jnp.mean(x * x, axis=-1, keepdims=True) + EPS)


def _mod_kernel(ct_ref, w_ref, b_ref, o_ref, a_scr, *, n_rows, k_chunk):
    ct = ct_ref[...]
    a_scr[...] = _silu(ct)
    d = w_ref.shape[0]
    tn = w_ref.shape[1]

    def body(kc, accs):
        k0 = pl.multiple_of(kc * k_chunk, k_chunk)
        wblk = w_ref[pl.ds(k0, k_chunk), :]
        out = []
        for r in range(n_rows):
            col = a_scr[pl.ds(k0, k_chunk), r:r + 1]
            out.append(accs[r] + jnp.sum((col * wblk).reshape(k_chunk // 8, 8, tn), axis=0))
        return tuple(out)

    accs = lax.fori_loop(0, d // k_chunk, body, tuple(jnp.zeros((8, tn), F32) for _ in range(n_rows)))
    o_ref[...] = jnp.zeros(o_ref.shape, F32)
    for r in range(n_rows):
        o_ref[r:r + 1, :] = jnp.sum(accs[r], axis=0, keepdims=True) + b_ref[...]


def _modulation(cvecs, w_mod, b_mod):
    n_rows, d = cvecs.shape
    n_out = w_mod.shape[1]
    tn = 1024
    ct = jnp.zeros((d, 8), F32).at[:, :n_rows].set(cvecs.T)
    return pl.pallas_call(
        functools.partial(_mod_kernel, n_rows=n_rows, k_chunk=64),
        grid=(n_out // tn,),
        in_specs=[pl.BlockSpec((d, 8), lambda j: (0, 0)),
                  pl.BlockSpec((d, tn), lambda j: (0, j)),
                  pl.BlockSpec((1, tn), lambda j: (0, j))],
        out_specs=pl.BlockSpec((8, tn), lambda j: (0, j)),
        out_shape=jax.ShapeDtypeStruct((8, n_out), F32),
        scratch_shapes=[pltpu.VMEM((d, 8), F32)],
        compiler_params=_cparams(("parallel",), 40),
        name="modulation",
    )(ct, w_mod, b_mod.reshape(1, n_out))


def _norm_proj_kernel(x_ref, g_ref, sh_ref, sc_ref, w_ref, o_ref, h_scr):
    @pl.when(pl.program_id(2) == 0)
    def _():
        y = _rms(x_ref[...]) * g_ref[...]
        h_scr[...] = (y * (1.0 + sc_ref[...]) + sh_ref[...]).astype(BF16)

    o_ref[...] = jnp.dot(h_scr[...], w_ref[...], preferred_element_type=F32)


def _norm_proj(x, g, shift, scale, w):
    b, n, d = x.shape
    c = w.shape[1]
    tm = min(n, 512)
    tn = c // 3 if (c % (3 * LANE) == 0) else c
    return pl.pallas_call(
        _norm_proj_kernel,
        grid=(b, n // tm, c // tn),
        in_specs=[pl.BlockSpec((None, tm, d), lambda bb, i, j: (bb, i, 0)),
                  pl.BlockSpec((1, d), lambda bb, i, j: (0, 0)),
                  pl.BlockSpec((None, 1, d), lambda bb, i, j: (bb, 0, 0)),
                  pl.BlockSpec((None, 1, d), lambda bb, i, j: (bb, 0, 0)),
                  pl.BlockSpec((d, tn), lambda bb, i, j: (0, j))],
        out_specs=pl.BlockSpec((None, tm, tn), lambda bb, i, j: (bb, i, j)),
        out_shape=jax.ShapeDtypeStruct((b, n, c), F32),
        scratch_shapes=[pltpu.VMEM((tm, d), BF16)],
        compiler_params=_cparams(("parallel", "parallel", "arbitrary"), 48),
        name="norm_proj",
    )(x, g.reshape(1, d), shift, scale, w)


def _rope_rotate(r, cos, sin):
    return r * cos + pltpu.roll(r, 64, 1) * sin


def _mla_q_kernel(cq_ref, g_ref, w_ref, cos_ref, sin_ref, o_ref, *, n_heads):
    cqn = (_rms(cq_ref[...]) * g_ref[...]).astype(BF16)
    qe = jnp.dot(cqn, w_ref[...], preferred_element_type=F32)
    cos = cos_ref[...]
    sin = sin_ref[...]
    for h in range(n_heads):
        c0 = h * MLA_QK_PAD
        o_ref[c0:c0 + LANE, :] = (qe[:, c0:c0 + LANE] * MLA_Q_SCALE).T.astype(BF16)
        rot = _rope_rotate(qe[:, c0 + LANE:c0 + 2 * LANE], cos, sin)
        o_ref[c0 + LANE:c0 + 2 * LANE, :] = (rot * MLA_Q_SCALE).T.astype(BF16)


def _mla_q_prep(p, g, w_q, cos, sin):
    b, n, _ = p.shape
    tm = min(n, 512)
    cw = N_HEADS_MLA * MLA_QK_PAD
    return pl.pallas_call(
        functools.partial(_mla_q_kernel, n_heads=N_HEADS_MLA),
        grid=(b, n // tm),
        in_specs=[pl.BlockSpec((None, tm, MLA_Q_LORA), lambda bb, i: (bb, i, COL_CQ * LANE // MLA_Q_LORA)),
                  pl.BlockSpec((1, MLA_Q_LORA), lambda bb, i: (0, 0)),
                  pl.BlockSpec((MLA_Q_LORA, cw), lambda bb, i: (0, 0)),
                  pl.BlockSpec((tm, LANE), lambda bb, i: (i, 0)),
                  pl.BlockSpec((tm, LANE), lambda bb, i: (i, 0))],
        out_specs=pl.BlockSpec((None, cw, tm), lambda bb, i: (bb, 0, i)),
        out_shape=jax.ShapeDtypeStruct((b, cw, n), BF16),
        compiler_params=_cparams(("parallel", "parallel"), 40),
        name="mla_q_prep",
    )(p, g.reshape(1, MLA_Q_LORA), w_q, cos, sin)


def _mla_kv_kernel(ckv_ref, kr_ref, g_ref, wuk_ref, wuv_ref, cos_ref, sin_ref, k_ref, v_ref, *, n_heads):
    ckvn = (_rms(ckv_ref[...]) * g_ref[...]).astype(BF16)
    kn = jnp.dot(ckvn, wuk_ref[...], preferred_element_type=F32)
    v_ref[...] = jnp.dot(ckvn, wuv_ref[...], preferred_element_type=F32).T.astype(BF16)
    rot =_rope_rotate(kr_ref[...], cos_ref[...], sin_ref[...]).astype(BF16)
    for h in range(n_heads):
        c0 = h * MLA_QK_PAD
        k_ref[:, c0:c0 + LANE] = kn[:, h * LANE:(h + 1) * LANE].astype(BF16)
        k_ref[:, c0 + LANE:c0 + 2 * LANE] = rot


def _mla_kv_prep(p, g, w_uk, w_uv, cos, sin):
    b, n, _ = p.shape
    tm = min(n, 512)
    kw = N_HEADS_MLA * MLA_QK_PAD
    vw = N_HEADS_MLA * HEAD_DIM
    return pl.pallas_call(
        functools.partial(_mla_kv_kernel, n_heads=N_HEADS_MLA),
        grid=(b, n // tm),
        in_specs=[pl.BlockSpec((None, tm, MLA_KV_LORA), lambda bb, i: (bb, i, COL_CKV * LANE // MLA_KV_LORA)),
                  pl.BlockSpec((None, tm, LANE), lambda bb, i: (bb, i, COL_KR)),
                  pl.BlockSpec((1, MLA_KV_LORA), lambda bb, i: (0, 0)),
                  pl.BlockSpec((MLA_KV_LORA, vw), lambda bb, i: (0, 0)),
                  pl.BlockSpec((MLA_KV_LORA, vw), lambda bb, i: (0, 0)),
                  pl.BlockSpec((tm, LANE), lambda bb, i: (i, 0)),
                  pl.BlockSpec((tm, LANE), lambda bb, i: (i, 0))],
        out_specs=[pl.BlockSpec((None, tm, kw), lambda bb, i: (bb, i, 0)),
                   pl.BlockSpec((None, vw, tm), lambda bb, i: (bb, 0, i))],
        out_shape=[jax.ShapeDtypeStruct((b, n, kw), BF16), jax.ShapeDtypeStruct((b, vw, n), BF16)],
        compiler_params=_cparams(("parallel", "parallel"), 40),
        name="mla_kv_prep",
    )(p, p, g.reshape(1, MLA_KV_LORA), w_uk, w_uv, cos, sin)


def _attn_kernel(*refs, n_src, scale, tk_max):
    q_ref = refs[0]
    o_ref = refs[1 + 2 * n_src]
    q = q_ref[...]
    if scale != 1.0:
        q = q * scale
    q = q.astype(BF16)
    tq = q.shape[0]
    dv = o_ref.shape[-1]

    def step(k, v, carry):
        m, l, acc = carry
        s = lax.dot_general(q, k, (((1,), (1,)), ((), ())), preferred_element_type=F32)
        m_new = jnp.maximum(m, jnp.max(s, axis=-1, keepdims=True))
        p = jnp.exp(s - m_new)
        alpha = jnp.exp(m - m_new)
        l = alpha * l + jnp.sum(p, axis=-1, keepdims=True)
        acc = alpha * acc + jnp.dot(p.astype(BF16), v, preferred_element_type=F32)
        return m_new, l, acc

    carry = (jnp.full((tq, 1), NEG_BIG, F32), jnp.zeros((tq, 1), F32), jnp.zeros((tq, dv), F32))
    for s_i in range(n_src):
        k_ref = refs[1 + 2 * s_i]
        v_ref = refs[2 + 2 * s_i]
        nk = k_ref.shape[0]
        tk = min(nk, tk_max)
        if nk == tk:
            carry = step(k_ref[...].astype(BF16), v_ref[...].astype(BF16), carry)
        else:
            def body(c, cr, k_ref=k_ref, v_ref=v_ref, tk=tk):
                k0 = pl.multiple_of(c * tk, tk)
                return step(k_ref[pl.ds(k0, tk), :].astype(BF16), v_ref[pl.ds(k0, tk), :].astype(BF16), cr)
            carry = lax.fori_loop(0, nk // tk, body, carry)
    _, l, acc = carry
    o_ref[...] = acc / l


def _attention(q_arr, q_blk0, dq, sources, dv, n_heads, scale, tq=256, tk_max=512):
    b, nq, _ = q_arr.shape
    tq = min(nq, tq)
    in_specs = [pl.BlockSpec((None, tq, dq), lambda bb, h, i: (bb, i, q_blk0 + h))]
    args = [q_arr]
    for k_arr, k_blk0, v_arr, v_blk0 in sources:
        nk = k_arr.shape[1]
        in_specs.append(pl.BlockSpec((None, nk, dq), lambda bb, h, i, o=k_blk0: (bb, 0, o + h)))
        in_specs.append(pl.BlockSpec((None, nk, dv), lambda bb, h, i, o=v_blk0: (bb, 0, o + h)))
        args += [k_arr, v_arr]
    return pl.pallas_call(
        functools.partial(_attn_kernel, n_src=len(sources), scale=scale, tk_max=tk_max),
        grid=(b, n_heads, nq // tq),
        in_specs=in_specs,
        out_specs=pl.BlockSpec((None, tq, dv), lambda bb, h, i: (bb, i, h)),
        out_shape=jax.ShapeDtypeStruct((b, nq, n_heads * dv), F32),
        compiler_params=_cparams(("parallel", "parallel", "arbitrary"), 48),
        name="attention",
    )(*args)


def _attn_t_kernel(*refs, n_src, tk_max, unroll, n_chains):
    q_ref = refs[0]
    o_ref = refs[1 + 2 * n_src]
    tq = q_ref.shape[1] // n_chains
    dv = o_ref.shape[-1]
    qs = [q_ref[:, j * tq:(j + 1) * tq] for j in range(n_chains)]

    def step(k, vt, carry):
        out = []
        for j in range(n_chains):
            m, l, acc = carry[j]
            s = jnp.dot(k, qs[j], preferred_element_type=F32)
            m_new = jnp.maximum(m, jnp.max(s, axis=0, keepdims=True))
            p = jnp.exp2(s - m_new)
            alpha = jnp.exp2(m - m_new)
            l = alpha * l + jnp.sum(p, axis=0, keepdims=True)
            acc = alpha * acc + jnp.dot(vt, p.astype(BF16), preferred_element_type=F32)
            out.append((m_new, l, acc))
        return tuple(out)

    carry = tuple((jnp.full((1, tq), NEG_BIG, F32), jnp.zeros((1, tq), F32), jnp.zeros((dv, tq), F32))
                  for _ in range(n_chains))
    for s_i in range(n_src):
        k_ref = refs[1 + 2 * s_i]
        vt_ref = refs[2 + 2 * s_i]
        nk = k_ref.shape[0]
        tk = min(nk, tk_max)
        if nk == tk:
            carry = step(k_ref[...], vt_ref[...], carry)
        else:
            def body(c, cr, k_ref=k_ref, vt_ref=vt_ref, tk=tk):
                k0 = pl.multiple_of(c * tk, tk)
                return step(k_ref[pl.ds(k0, tk), :], vt_ref[:, pl.ds(k0, tk)], cr)
            carry = lax.fori_loop(0, nk // tk, body, carry, unroll=unroll)
    for j in range(n_chains):
        _, l, acc = carry[j]
        o_ref[j * tq:(j + 1) * tq, :] = (acc / l).T


def _attention_t(q_arr, sources, n_heads, dq, dv, tq=1024, tk_max=1024, unroll=1, n_chains=1):
    b, _, nq = q_arr.shape
    if nq < tq * n_chains:
        tq, n_chains = nq, 1
    tq = tq * n_chains
    in_specs = [pl.BlockSpec((None, dq, tq), lambda bb, h, i: (bb, h, i))]
    args = [q_arr]
    for k_arr, vt_arr in sources:
        nk = k_arr.shape[1]
        in_specs.append(pl.BlockSpec((None, nk, dq), lambda bb, h, i: (bb, 0, h)))
        in_specs.append(pl.BlockSpec((None, dv, nk), lambda bb, h, i: (bb, h, 0)))
        args += [k_arr, vt_arr]
    return pl.pallas_call(
        functools.partial(_attn_t_kernel, n_src=len(sources), tk_max=tk_max, unroll=unroll, n_chains=n_chains),
        grid=(b, n_heads, nq // tq),
        in_specs=in_specs,
        out_specs=pl.BlockSpec((None, tq, dv), lambda bb, h, i: (bb, i, h)),
        out_shape=jax.ShapeDtypeStruct((b, nq, n_heads * dv), F32),
        compiler_params=_cparams(("parallel", "parallel", "arbitrary"), 48),
        name="attention_t",
    )(*args)


def _nat_kernel(q_ref, k_ref, v_ref, kc_ref, vc_ref, bias_ref, o_ref, *, rows_per_step, n_rows):
    i = pl.program_id(2)
    scale = HEAD_DIM ** -0.5
    kc = kc_ref[...].astype(BF16)
    vc = vc_ref[...].astype(BF16)
    nt = (((1,), (1,)), ((), ()))
    win = NAT_ROWS * GRID_W
    for rr in range(rows_per_step):
        r = i * rows_per_step + rr
        r0 = jnp.clip(r - NAT_ROWS // 2, 0, n_rows - NAT_ROWS)
        base = r0 - r + NAT_ROWS - 1
        t0 = pl.multiple_of(r0 * GRID_W, GRID_W)
        q = (q_ref[rr * GRID_W:(rr + 1) * GRID_W, :] * scale).astype(BF16)
        ks = k_ref[pl.ds(t0, win), :].astype(BF16)
        vs = v_ref[pl.ds(t0, win), :].astype(BF16)
        s_w = lax.dot_general(q, ks, nt, preferred_element_type=F32) + bias_ref[base]
        s_c = lax.dot_general(q, kc, nt, preferred_element_type=F32)
        m = jnp.maximum(jnp.max(s_w, axis=-1, keepdims=True), jnp.max(s_c, axis=-1, keepdims=True))
        p_w = jnp.exp(s_w - m)
        p_c = jnp.exp(s_c - m)
        l = jnp.sum(p_w, axis=-1, keepdims=True) + jnp.sum(p_c, axis=-1, keepdims=True)
        o = (jnp.dot(p_w.astype(BF16), vs, preferred_element_type=F32)
             + jnp.dot(p_c.astype(BF16), vc, preferred_element_type=F32))
        o_ref[rr * GRID_W:(rr + 1) * GRID_W, :] = o / l


def _nat_bias_table(rpb):
    col = jnp.arange(GRID_W)
    c0 = jnp.clip(col - NAT_COLS // 2, 0, GRID_W - NAT_COLS)
    kc = jnp.arange(GRID_W)
    inside = (kc[None, :] >= c0[:, None]) & (kc[None, :] < c0[:, None] + NAT_COLS)
    dc = jnp.clip(kc[None, :] - col[:, None] + NAT_COLS - 1, 0, 2 * NAT_COLS - 2)
    per_dr = jnp.where(inside[None, None], rpb[:, :, dc], NEG_BIG)
    dr = jnp.arange(NAT_ROWS)[:, None] + jnp.arange(NAT_ROWS)[None, :]
    tbl = per_dr[:, dr]
    h = rpb.shape[0]
    return tbl.transpose(0, 1, 3, 2, 4).reshape(h, NAT_ROWS, GRID_W, NAT_ROWS * GRID_W).astype(F32)


def _nat_latent(p_l, p_c, bias_tbl):
    b, n, _ = p_l.shape
    n_ctx = p_c.shape[1]
    n_rows = n // GRID_W
    rps = 8
    tq = rps * GRID_W
    win = NAT_ROWS * GRID_W
    return pl.pallas_call(
        functools.partial(_nat_kernel, rows_per_step=rps, n_rows=n_rows),
        grid=(b, N_HEADS_NAT, n_rows // rps),
        in_specs=[pl.BlockSpec((None, tq, LANE), lambda bb, h, i: (bb, i, COL_NAT_Q + h)),
                  pl.BlockSpec((None, n, LANE), lambda bb, h, i: (bb, 0, COL_NAT_K + h)),
                  pl.BlockSpec((None, n, LANE), lambda bb, h, i: (bb, 0, COL_NAT_V + h)),
                  pl.BlockSpec((None, n_ctx, LANE), lambda bb, h, i: (bb, 0, COL_NAT_K + h)),
                  pl.BlockSpec((None, n_ctx, LANE), lambda bb, h, i: (bb, 0, COL_NAT_V + h)),
                  pl.BlockSpec((None, NAT_ROWS, GRID_W, win), lambda bb, h, i: (h, 0, 0, 0))],
        out_specs=pl.BlockSpec((None, tq, LANE), lambda bb, h, i: (bb, i, h)),
        out_shape=jax.ShapeDtypeStruct((b, n, N_HEADS_NAT * HEAD_DIM), F32),
        compiler_params=_cparams(("parallel", "parallel", "arbitrary"), 48),
        name="nat_latent",
    )(p_l, p_l, p_l, p_c, p_c, bias_tbl)


def _ret_kernel(*refs, reverse, n_chunks, add_prev):
    if add_prev:
        logit_ref, q_ref, k_ref, v_ref, g_ref, s0_ref, prev_ref, y_ref, s_ref = refs
    else:
        logit_ref, q_ref, k_ref, v_ref, g_ref, s0_ref, y_ref, s_ref = refs
        prev_ref = None
    h = pl.program_id(1)
    c_len = RET_CHUNK

    @pl.when(pl.program_id(2) == 0)
    def _():
        s_ref[...] = s0_ref[...]

    z = jnp.full((c_len, c_len), -logit_ref[h], F32)
    lg = -(jnp.maximum(z, 0.0) + jnp.log1p(jnp.exp(-jnp.abs(z))))
    ci = lax.broadcasted_iota(jnp.int32, (c_len, c_len), 0).astype(F32)
    si = lax.broadcasted_iota(jnp.int32, (c_len, c_len), 1).astype(F32)
    if reverse:
        rel = si - ci
        wq = jnp.exp(lg * (c_len - ci))
        wk = jnp.exp(lg * ci)
    else:
        rel = ci - si
        wq = jnp.exp(lg * (ci + 1.0))
        wk = jnp.exp(lg * (c_len - 1.0 - ci))
    dmat = jnp.where(rel >= 0, jnp.exp(lg * jnp.maximum(rel, 0.0)), 0.0)
    g_chunk = jnp.exp(lg * c_len)
    k_scale = HEAD_DIM ** -0.5
    nt = (((1,), (1,)), ((), ()))

    state = s_ref[...]
    order = range(n_chunks - 1, -1, -1) if reverse else range(n_chunks)
    for c in order:
        sl = slice(c * c_len, (c + 1) * c_len)
        qc = q_ref[sl, :]
        kc = k_ref[sl, :] * k_scale
        vb = v_ref[sl, :].astype(BF16)
        sc = lax.dot_general(qc.astype(BF16), kc.astype(BF16), nt, preferred_element_type=F32) * dmat
        o = (jnp.dot(sc.astype(BF16), vb, preferred_element_type=F32)
             + jnp.dot((qc * wq).astype(BF16), state.astype(BF16), preferred_element_type=F32))
        state = g_chunk * state + jnp.dot((kc * wk).T.astype(BF16), vb, preferred_element_type=F32)
        mu = jnp.mean(o, axis=-1, keepdims=True)
        d = o - mu
        gn = d * lax.rsqrt(jnp.mean(d * d, axis=-1, keepdims=True) + EPS)
        y = _silu(g_ref[sl, :]) * gn
        if add_prev:
            y = y + prev_ref[sl, :]
        y_ref[sl, :] = y
    s_ref[...] = state


def _retention_dir(p, logits, s0, prev, reverse):
    b, n, _ = p.shape
    hh = N_HEADS_RET
    tb = min(n, 4 * RET_CHUNK)
    nblk = n // tb
    gate_col = COL_RET_GB if reverse else COL_RET_GF

    def row(i):
        return (nblk - 1 - i) if reverse else i

    def col_spec(c0):
        return pl.BlockSpec((None, tb, LANE), lambda bb, h, i: (bb, row(i), c0 + h))

    in_specs = [pl.BlockSpec(memory_space=pltpu.SMEM),
                col_spec(COL_RET_Q), col_spec(COL_RET_K), col_spec(COL_RET_V), col_spec(gate_col),
                pl.BlockSpec((None, None, HEAD_DIM, HEAD_DIM), lambda bb, h, i: (bb, h, 0, 0))]
    args = [logits.astype(F32), p, p, p, p, s0]
    if prev is not None:
        in_specs.append(pl.BlockSpec((None, tb, LANE), lambda bb, h, i: (bb, row(i), h)))
        args.append(prev)
    return pl.pallas_call(
        functools.partial(_ret_kernel, reverse=reverse, n_chunks=tb // RET_CHUNK, add_prev=prev is not None),
        grid=(b, hh, nblk),
        in_specs=in_specs,
        out_specs=[pl.BlockSpec((None, tb, LANE), lambda bb, h, i: (bb, row(i), h)),
                   pl.BlockSpec((None, None, HEAD_DIM, HEAD_DIM), lambda bb, h, i: (bb, h, 0, 0))],
        out_shape=[jax.ShapeDtypeStruct((b, n, hh * HEAD_DIM), F32),
                   jax.ShapeDtypeStruct((b, hh, HEAD_DIM, HEAD_DIM), F32)],
        compiler_params=_cparams(("parallel", "parallel", "arbitrary"), 32),
        name="retention_bwd" if reverse else "retention_fwd",
    )(*args)


def _retention(p_c, p_l, logit_f, logit_b):
    b = p_c.shape[0]
    s0 = jnp.zeros((b, N_HEADS_RET, HEAD_DIM, HEAD_DIM), F32)
    yc_f, s_cf = _retention_dir(p_c, logit_f, s0, None, False)
    y_c, s_cb = _retention_dir(p_c, logit_b, s0, yc_f, True)
    yl_f, _ = _retention_dir(p_l, logit_f, s_cf, None, False)
    y_l, _ = _retention_dir(p_l, logit_b, s_cb, yl_f, True)
    return y_c, y_l


def _merge_kernel(x_ref, ya_ref, yb_ref, yr_ref, g_ref, gate_ref, w_ref, o_ref):
    wa = ya_ref.shape[-1]
    wb = yb_ref.shape[-1]
    na = (_rms(ya_ref[...]) * g_ref[:, :wa]).astype(BF16)
    nb = (_rms(yb_ref[...]) * g_ref[:, wa:wa + wb]).astype(BF16)
    nr = (_rms(yr_ref[...]) * g_ref[:, wa + wb:]).astype(BF16)
    acc = jnp.dot(na, w_ref[:wa, :], preferred_element_type=F32)
    acc = acc + jnp.dot(nb, w_ref[wa:wa + wb, :], preferred_element_type=F32)
    acc = acc + jnp.dot(nr, w_ref[wa + wb:, :], preferred_element_type=F32)
    o_ref[...] = x_ref[...] + gate_ref[...] * acc


def _merge_out(x, ya, yb, yr, g, gate, w_out):
    b, n, d = x.shape
    tm = min(n, 256)
    dm = w_out.shape[0]

    def tok(width):
        return pl.BlockSpec((None, tm, width), lambda bb, i: (bb, i, 0))

    return pl.pallas_call(
        _merge_kernel,
        grid=(b, n // tm),
        in_specs=[tok(d), tok(ya.shape[-1]), tok(yb.shape[-1]), tok(yr.shape[-1]),
                  pl.BlockSpec((1, dm), lambda bb, i: (0, 0)),
                  pl.BlockSpec((None, 1, d), lambda bb, i: (bb, 0, 0)),
                  pl.BlockSpec((dm, d), lambda bb, i: (0, 0))],
        out_specs=tok(d),
        out_shape=jax.ShapeDtypeStruct((b, n, d), F32),
        compiler_params=_cparams(("parallel", "parallel"), 48),
        name="merge_out",
    )(x, ya, yb, yr, g.reshape(1, dm), gate, w_out)


def _moe_pre_kernel(x_ref, g_ref, sh_ref, sc_ref, wr_ref, h_ref, aff_ref, *, n_experts):
    h = (_rms(x_ref[...]) * g_ref[...]) * (1.0 + sc_ref[...]) + sh_ref[...]
    h_ref[...] = h.astype(BF16)
    h_hi = h.astype(BF16)
    h_lo = (h - h_hi.astype(F32)).astype(BF16)
    w = wr_ref[...]
    w_hi = w.astype(BF16)
    w_lo = (w - w_hi.astype(F32)).astype(BF16)
    logits = (jnp.dot(h_hi, w_hi, preferred_element_type=F32)
              + (jnp.dot(h_hi, w_lo, preferred_element_type=F32) + jnp.dot(h_lo, w_hi, preferred_element_type=F32)))
    lane = lax.broadcasted_iota(jnp.int32, logits.shape, 1)
    logits = jnp.where(lane < n_experts, logits, NEG_BIG)
    e = jnp.exp(logits - jnp.max(logits, axis=-1, keepdims=True))
    aff_ref[...] = e / jnp.sum(e, axis=-1, keepdims=True)


def _moe_pre(x, g, shift, scale, w_router_pad):
    b, n, d = x.shape
    tm = min(n, 256)
    return pl.pallas_call(
        functools.partial(_moe_pre_kernel, n_experts=N_EXPERTS),
        grid=(b, n // tm),
        in_specs=[pl.BlockSpec((None, tm, d), lambda bb, i: (bb, i, 0)),
                  pl.BlockSpec((1, d), lambda bb, i: (0, 0)),
                  pl.BlockSpec((None, 1, d), lambda bb, i: (bb, 0, 0)),
                  pl.BlockSpec((None, 1, d), lambda bb, i: (bb, 0, 0)),
                  pl.BlockSpec((d, LANE), lambda bb, i: (0, 0))],
        out_specs=[pl.BlockSpec((None, tm, d), lambda bb, i: (bb, i, 0)),
                   pl.BlockSpec((None, tm, LANE), lambda bb, i: (bb, i, 0))],
        out_shape=[jax.ShapeDtypeStruct((b, n, d), BF16), jax.ShapeDtypeStruct((b, n, LANE), F32)],
        compiler_params=_cparams(("parallel", "parallel"), 40),
        name="moe_pre",
    )(x, g.reshape(1, d), shift, scale, w_router_pad)


def _ffn_kernel(xs_ref, gate_ref, wg_ref, wu_ref, wd_ref, o_ref):
    xs = xs_ref[...]
    a = jnp.dot(xs, wg_ref[...], preferred_element_type=F32)
    u = jnp.dot(xs, wu_ref[...], preferred_element_type=F32)
    hm = (_silu(a) * u).astype(BF16)
    o_ref[...] = jnp.dot(hm, wd_ref[...], preferred_element_type=F32) * gate_ref[...]


def _expert_ffn(xs, gates, w_gate, w_up, w_down):
    e, t, d = xs.shape
    f = w_gate.shape[-1]
    tm = min(t, 512)
    return pl.pallas_call(
        _ffn_kernel,
        grid=(e, t // tm),
        in_specs=[pl.BlockSpec((None, tm, d), lambda ee, i: (ee, i, 0)),
                  pl.BlockSpec((None, tm, 1), lambda ee, i: (ee, i, 0)),
                  pl.BlockSpec((None, d, f), lambda ee, i: (ee, 0, 0)),
                  pl.BlockSpec((None, d, f), lambda ee, i: (ee, 0, 0)),
                  pl.BlockSpec((None, f, d), lambda ee, i: (ee, 0, 0))],
        out_specs=pl.BlockSpec((None, tm, d), lambda ee, i: (ee, i, 0)),
        out_shape=jax.ShapeDtypeStruct((e, t, d), F32),
        compiler_params=_cparams(("parallel", "arbitrary"), 56),
        name="expert_ffn",
    )(xs, gates, w_gate, w_up, w_down)


def _residual_kernel(x_ref, m_ref, gate_ref, gn_ref, o_ref, *, final_norm):
    y = x_ref[...] + gate_ref[...] * m_ref[...]
    if final_norm:
        y = _rms(y) * gn_ref[...]
    o_ref[...] = y


def _gated_residual(x, moe, gate, norm_g, final_norm):
    b, n, d = x.shape
    tm = min(n, 512)
    tok = pl.BlockSpec((None, tm, d), lambda bb, i: (bb, i, 0))
    return pl.pallas_call(
        functools.partial(_residual_kernel, final_norm=final_norm),
        grid=(b, n // tm),
        in_specs=[tok, tok,
                  pl.BlockSpec((None, 1, d), lambda bb, i: (bb, 0, 0)),
                  pl.BlockSpec((1, d), lambda bb, i: (0, 0))],
        out_specs=tok,
        out_shape=jax.ShapeDtypeStruct((b, n, d), F32),
        compiler_params=_cparams(("parallel", "parallel"), 40),
        name="gated_residual",
    )(x, moe, gate, norm_g.reshape(1, d))


def _ec_moe(x, g, shift, scale, w_router_pad, w_gate, w_up, w_down):
    b, n, d = x.shape
    cap = EC_CAPACITY * n // N_EXPERTS
    h, aff = _moe_pre(x, g, shift, scale, w_router_pad)
    gate, idx = lax.top_k(jnp.swapaxes(aff[..., :N_EXPERTS], 1, 2), cap)
    flat = (idx + (jnp.arange(b, dtype=idx.dtype) * n)[:, None, None]).transpose(1, 0, 2).reshape(-1)
    xs = jnp.take(h.reshape(b * n, d), flat, axis=0).reshape(N_EXPERTS, b * cap, d)
    gates = gate.transpose(1, 0, 2).reshape(N_EXPERTS, b * cap, 1)
    y = _expert_ffn(xs, gates, w_gate, w_up, w_down)
    out = jnp.zeros((b * n, d), F32).at[flat].add(y.reshape(-1, d))
    return out.reshape(b, n, d)


_ROPE_SRC = np.concatenate([np.arange(16, 32), np.arange(0, 16), np.arange(48, 64), np.arange(32, 48)])
_ROPE_SIGN = np.concatenate([-np.ones(16), np.ones(16), -np.ones(16), np.ones(16)]).astype(np.float32)


def _rope_perm_cols(w):
    return w[:, _ROPE_SRC] * _ROPE_SIGN


def _pad_w_in(w_in):
    c_kr = MLA_Q_LORA + MLA_KV_LORA
    kr = w_in[:, c_kr:c_kr + MLA_ROPE_DIM]
    return jnp.concatenate([w_in[:, :c_kr + MLA_ROPE_DIM], _rope_perm_cols(kr), w_in[:, c_kr + MLA_ROPE_DIM:]],
                           axis=1).astype(BF16)


def _expand_w_uq(w_uq):
    w = w_uq.reshape(MLA_Q_LORA, N_HEADS_MLA, HEAD_DIM + MLA_ROPE_DIM)
    nope = w[:, :, :HEAD_DIM]
    rope = w[:, :, HEAD_DIM:]
    partner = rope[:, :, _ROPE_SRC] * _ROPE_SIGN
    return jnp.concatenate([nope, rope, partner], axis=-1).reshape(MLA_Q_LORA, N_HEADS_MLA * MLA_QK_PAD).astype(BF16)


def _rope_tables(n):
    t = jnp.arange(n)
    n_freq = MLA_ROPE_DIM // 4
    inv = ROPE_BASE ** (-jnp.arange(n_freq, dtype=F32) / n_freq)
    ang_r = (t // GRID_W).astype(F32)[:, None] * inv
    ang_c = (t % GRID_W).astype(F32)[:, None] * inv
    zeros = jnp.zeros((n, LANE - MLA_ROPE_DIM), F32)
    cos = jnp.concatenate([jnp.cos(ang_r), jnp.cos(ang_r), jnp.cos(ang_c), jnp.cos(ang_c), zeros], axis=1)
    sin = jnp.concatenate([jnp.sin(ang_r), jnp.sin(ang_r), jnp.sin(ang_c), jnp.sin(ang_c), zeros], axis=1)
    return cos, sin


def _identity_tables(n):
    ones = jnp.ones((n, MLA_ROPE_DIM), F32)
    zeros = jnp.zeros((n, LANE - MLA_ROPE_DIM), F32)
    return jnp.concatenate([ones, zeros], axis=1), jnp.zeros((n, LANE), F32)


def _layer(x_l, x_c, mod, lp, last, final_g):
    b, n, d = x_l.shape
    n_ctx = x_c.shape[1]

    def chunk(k, ctx):
        rows = jnp.broadcast_to(mod[b, k * d:(k + 1) * d], (b, d)) if ctx else mod[:b, k * d:(k + 1) * d]
        return rows.reshape(b, 1, d)

    w_in = _pad_w_in(lp['w_in'])
    p_l = _norm_proj(x_l, lp['norm1_g'], chunk(0, False), chunk(1, False), w_in)
    p_c = _norm_proj(x_c, lp['norm1_g'], chunk(0, True), chunk(1, True), w_in)

    cos_l, sin_l = _rope_tables(n)
    cos_c, sin_c = _identity_tables(n_ctx)
    w_uk = lp['mla_w_uk'].reshape(MLA_KV_LORA, -1).astype(BF16)
    w_uv = lp['mla_w_uv'].reshape(MLA_KV_LORA, -1).astype(BF16)
    w_q = _expand_w_uq(lp['mla_w_uq'])
    k_l, v_l = _mla_kv_prep(p_l, lp['mla_kv_norm_g'], w_uk, w_uv, cos_l, sin_l)
    k_c, v_c = _mla_kv_prep(p_c, lp['mla_kv_norm_g'], w_uk, w_uv, cos_c, sin_c)
    q_l = _mla_q_prep(p_l, lp['mla_q_norm_g'], w_q, cos_l, sin_l)
    a_l = _attention_t(q_l, [(k_c, v_c), (k_l, v_l)], N_HEADS_MLA, MLA_QK_PAD, HEAD_DIM)

    b_l = _nat_latent(p_l, p_c, _nat_bias_table(lp['nat_rpb']))
    r_c, r_l = _retention(p_c, p_l, lp['ret_decay_f'], lp['ret_decay_b'])

    w_out = lp['w_out'].astype(BF16)
    w_router = jnp.zeros((d, LANE), F32).at[:, :N_EXPERTS].set(lp['w_router'])
    w_gate = lp['w_gate'].astype(BF16)
    w_up = lp['w_up'].astype(BF16)
    w_down = lp['w_down'].astype(BF16)

    x_l = _merge_out(x_l, a_l, b_l, r_l, lp['out_norm_g'], chunk(2, False), w_out)
    moe_l = _ec_moe(x_l, lp['norm2_g'], chunk(3, False), chunk(4, False), w_router, w_gate, w_up, w_down)
    x_l = _gated_residual(x_l, moe_l, chunk(5, False), final_g, last)
    if last:
        return x_l, None

    q_c = _mla_q_prep(p_c, lp['mla_q_norm_g'], w_q, cos_c, sin_c)
    a_c = _attention_t(q_c, [(k_c, v_c)], N_HEADS_MLA, MLA_QK_PAD, HEAD_DIM)
    b_c = _attention(p_c, COL_NAT_Q, HEAD_DIM, [(p_c, COL_NAT_K, p_c, COL_NAT_V)], HEAD_DIM, N_HEADS_NAT,
                     HEAD_DIM ** -0.5)
    x_c = _merge_out(x_c, a_c, b_c, r_c, lp['out_norm_g'], chunk(2, True), w_out)
    moe_c = _ec_moe(x_c, lp['norm2_g'], chunk(3, True), chunk(4, True), w_router, w_gate, w_up, w_down)
    x_c = _gated_residual(x_c, moe_c, chunk(5, True), final_g, False)
    return x_l, x_c


def kernel(x, c, ctx, c_ctx, w_mod, b_mod, norm1_g, w_in, mla_q_norm_g, mla_kv_norm_g, mla_w_uq, mla_w_uk,
           mla_w_uv, nat_rpb, ret_decay_f, ret_decay_b, out_norm_g, w_out, norm2_g, w_router, w_gate, w_up,
           w_down, final_norm_g):
    depth = w_mod.shape[0]
    cvecs = jnp.concatenate([c, c_ctx[None, :]], axis=0)
    x_l, x_c = x, ctx
    for i in range(depth):
        lp = {
            'norm1_g': norm1_g[i], 'w_in': w_in[i], 'mla_q_norm_g': mla_q_norm_g[i],
            'mla_kv_norm_g': mla_kv_norm_g[i], 'mla_w_uq': mla_w_uq[i], 'mla_w_uk': mla_w_uk[i],
            'mla_w_uv': mla_w_uv[i], 'nat_rpb': nat_rpb[i], 'ret_decay_f': ret_decay_f[i],
            'ret_decay_b': ret_decay_b[i], 'out_norm_g': out_norm_g[i], 'w_out': w_out[i],
            'norm2_g': norm2_g[i], 'w_router': w_router[i], 'w_gate': w_gate[i], 'w_up': w_up[i],
            'w_down': w_down[i],
        }
        mod = _modulation(cvecs, w_mod[i], b_mod[i])
        x_l, x_c = _layer(x_l, x_c, mod, lp, i == depth - 1, final_norm_g)
    return x_l
```

```python
import functools

import numpy as np
import jax
import jax.numpy as jnp
from jax import lax
from jax.experimental import pallas as pl
from jax.experimental.pallas import tpu as pltpu

F32 = jnp.float32
BF16 = jnp.bfloat16

GRID_W = 64
HEAD_DIM = 128
N_HEADS_MLA = 8
N_HEADS_NAT = 4
N_HEADS_RET = 4
MLA_ROPE_DIM = 64
MLA_Q_LORA = 512
MLA_KV_LORA = 256
MLA_SCALE = (HEAD_DIM + MLA_ROPE_DIM) ** -0.5
LOG2E = 1.4426950408889634
MLA_Q_SCALE = MLA_SCALE * LOG2E
MLA_QK_PAD = 256
NAT_ROWS = 8
NAT_COLS = 16
RET_CHUNK = 128
N_EXPERTS = 16
EC_CAPACITY = 2
ROPE_BASE = 10000.0
EPS = 1e-6
NEG_BIG = -1e30
LANE = 128

COL_CQ = 0
COL_CKV = 4
COL_KR = 6
COL_NAT_Q = 7
COL_NAT_K = 11
COL_NAT_V = 15
COL_RET_Q = 19
COL_RET_K = 23
COL_RET_V = 27
COL_RET_GF = 31
COL_RET_GB = 35
IN_COLS_PAD = 39 * LANE


def _cparams(sem, vmem_mib):
    return pltpu.CompilerParams(dimension_semantics=sem, vmem_limit_bytes=vmem_mib * 1024 * 1024)


def _silu(a):
    return a * (1.0 / (1.0 + jnp.exp(-a)))


def _rms(x):
    return x * lax.rsqrt(jnp.mean(x * x, axis=-1, keepdims=True) + EPS)


def _mod_kernel(ct_ref, w_ref, b_ref, o_ref, a_scr, *, n_rows, k_chunk):
    ct = ct_ref[...]
    a_scr[...] = _silu(ct)
    d = w_ref.shape[0]
    tn = w_ref.shape[1]

    def body(kc, accs):
        k0 = pl.multiple_of(kc * k_chunk, k_chunk)
        wblk = w_ref[pl.ds(k0, k_chunk), :]
        out = []
        for r in range(n_rows):
            col = a_scr[pl.ds(k0, k_chunk), r:r + 1]
            out.append(accs[r] + jnp.sum((col * wblk).reshape(k_chunk // 8, 8, tn), axis=0))
        return tuple(out)

    accs = lax.fori_loop(0, d // k_chunk, body, tuple(jnp.zeros((8, tn), F32) for _ in range(n_rows)))
    o_ref[...] = jnp.zeros(o_ref.shape, F32)
    for r in range(n_rows):
        o_ref[r:r + 1, :] = jnp.sum(accs[r], axis=0, keepdims=True) + b_ref[...]


def _modulation(cvecs, w_mod, b_mod):
    n_rows, d = cvecs.shape
    n_out = w_mod.shape[1]
    tn = 1024
    ct = jnp.zeros((d, 8), F32).at[:, :n_rows].set(cvecs.T)
    return pl.pallas_call(
        functools.partial(_mod_kernel, n_rows=n_rows, k_chunk=64),
        grid=(n_out // tn,),
        in_specs=[pl.BlockSpec((d, 8), lambda j: (0, 0)),
                  pl.BlockSpec((d, tn), lambda j: (0, j)),
                  pl.BlockSpec((1, tn), lambda j: (0, j))],
        out_specs=pl.BlockSpec((8, tn), lambda j: (0, j)),
        out_shape=jax.ShapeDtypeStruct((8, n_out), F32),
        scratch_shapes=[pltpu.VMEM((d, 8), F32)],
        compiler_params=_cparams(("parallel",), 40),
        name="modulation",
    )(ct, w_mod, b_mod.reshape(1, n_out))


def _norm_proj_kernel(x_ref, g_ref, sh_ref, sc_ref, w_ref, o_ref, h_scr):
    @pl.when(pl.program_id(2) == 0)
    def _():
        y = _rms(x_ref[...]) * g_ref[...]
        h_scr[...] = (y * (1.0 + sc_ref[...]) + sh_ref[...]).astype(BF16)

    o_ref[...] = jnp.dot(h_scr[...], w_ref[...], preferred_element_type=F32).astype(o_ref.dtype)


def _norm_proj(x, g, shift, scale, w):
    b, n, d = x.shape
    c = w.shape[1]
    tm = min(n, 1024)
    tn = c // 3 if (c % (3 * LANE) == 0) else c
    return pl.pallas_call(
        _norm_proj_kernel,
        grid=(b, n // tm, c // tn),
        in_specs=[pl.BlockSpec((None, tm, d), lambda bb, i, j: (bb, i, 0)),
                  pl.BlockSpec((1, d), lambda bb, i, j: (0, 0)),
                  pl.BlockSpec((None, 1, d), lambda bb, i, j: (bb, 0, 0)),
                  pl.BlockSpec((None, 1, d), lambda bb, i, j: (bb, 0, 0)),
                  pl.BlockSpec((d, tn), lambda bb, i, j: (0, j))],
        out_specs=pl.BlockSpec((None, tm, tn), lambda bb, i, j: (bb, i, j)),
        out_shape=jax.ShapeDtypeStruct((b, n, c), BF16),
        scratch_shapes=[pltpu.VMEM((tm, d), BF16)],
        compiler_params=_cparams(("parallel", "parallel", "arbitrary"), 56),
        name="norm_proj",
    )(x, g.reshape(1, d), shift, scale, w)


def _rope_rotate(r, cos, sin):
    return r * cos + pltpu.roll(r, 64, 1) * sin


def _mla_q_kernel(cq_ref, g_ref, w_ref, cos_ref, sin_ref, o_ref, *, n_heads):
    cqn = (_rms(cq_ref[...].astype(F32)) * g_ref[...]).astype(BF16)
    qe = jnp.dot(cqn, w_ref[...], preferred_element_type=F32)
    cos = cos_ref[...]
    sin = sin_ref[...]
    for h in range(n_heads):
        c0 = h * MLA_QK_PAD
        o_ref[c0:c0 + LANE, :] = (qe[:, c0:c0 + LANE] * MLA_Q_SCALE).T.astype(BF16)
        rot = _rope_rotate(qe[:, c0 + LANE:c0 + 2 * LANE], cos, sin)
        o_ref[c0 + LANE:c0 + 2 * LANE, :] = (rot * MLA_Q_SCALE).T.astype(BF16)


def _mla_q_prep(p, g, w_q, cos, sin):
    b, n, _ = p.shape
    tm = min(n, 512)
    cw = N_HEADS_MLA * MLA_QK_PAD
    return pl.pallas_call(
        functools.partial(_mla_q_kernel, n_heads=N_HEADS_MLA),
        grid=(b, n // tm),
        in_specs=[pl.BlockSpec((None, tm, MLA_Q_LORA), lambda bb, i: (bb, i, COL_CQ * LANE // MLA_Q_LORA)),
                  pl.BlockSpec((1, MLA_Q_LORA), lambda bb, i: (0, 0)),
                  pl.BlockSpec((MLA_Q_LORA, cw), lambda bb, i: (0, 0)),
                  pl.BlockSpec((tm, LANE), lambda bb, i: (i, 0)),
                  pl.BlockSpec((tm, LANE), lambda bb, i: (i, 0))],
        out_specs=pl.BlockSpec((None, cw, tm), lambda bb, i: (bb, 0, i)),
        out_shape=jax.ShapeDtypeStruct((b, cw, n), BF16),
        compiler_params=_cparams(("parallel", "parallel"), 40),
        name="mla_q_prep",
    )(p, g.reshape(1, MLA_Q_LORA), w_q, cos, sin)


def _mla_kv_kernel(ckv_ref, kr_ref, g_ref, wuk_ref, wuv_ref, cos_ref, sin_ref, k_ref, v_ref, *, n_heads):
    ckvn = (_rms(ckv_ref[...].astype(F32)) * g_ref[...]).astype(BF16)
    kn = jnp.dot(ckvn, wuk_ref[...], preferred_element_type=F32)
    v_ref[...] = jnp.dot(ckvn, wuv_ref[...], preferred_element_type=F32).T.astype(BF16)
    rot = _rope_rotate(kr_ref[...].astype(F32), cos_ref[...], sin_ref[...]).astype(BF16)
    for h in range(n_heads):
        c0 = h * MLA_QK_PAD
        k_ref[:, c0:c0 + LANE] = kn[:, h * LANE:(h + 1) * LANE].astype(BF16)
        k_ref[:, c0 + LANE:c0 + 2 * LANE] = rot


def _mla_kv_prep(p, g, w_uk, w_uv, cos, sin):
    b, n, _ = p.shape
    tm = min(n, 512)
    kw = N_HEADS_MLA * MLA_QK_PAD
    vw = N_HEADS_MLA * HEAD_DIM
    return pl.pallas_call(
        functools.partial(_mla_kv_kernel, n_heads=N_HEADS_MLA),
        grid=(b, n // tm),
        in_specs=[pl.BlockSpec((None, tm, MLA_KV_LORA), lambda bb, i: (bb, i, COL_CKV * LANE // MLA_KV_LORA)),
                  pl.BlockSpec((None, tm, LANE), lambda bb, i: (bb, i, COL_KR)),
                  pl.BlockSpec((1, MLA_KV_LORA), lambda bb, i: (0, 0)),
                  pl.BlockSpec((MLA_KV_LORA, vw), lambda bb, i: (0, 0)),
                  pl.BlockSpec((MLA_KV_LORA, vw), lambda bb, i: (0, 0)),
                  pl.BlockSpec((tm, LANE), lambda bb, i: (i, 0)),
                  pl.BlockSpec((tm, LANE), lambda bb, i: (i, 0))],
        out_specs=[pl.BlockSpec((None, tm, kw), lambda bb, i: (bb, i, 0)),
                   pl.BlockSpec((None, vw, tm), lambda bb, i: (bb, 0, i))],
        out_shape=[jax.ShapeDtypeStruct((b, n, kw), BF16), jax.ShapeDtypeStruct((b, vw, n), BF16)],
        compiler_params=_cparams(("parallel", "parallel"), 40),
        name="mla_kv_prep",
    )(p, p, g.reshape(1, MLA_KV_LORA), w_uk, w_uv, cos, sin)


def _attn_kernel(*refs, n_src, scale, tk_max):
    q_ref = refs[0]
    o_ref = refs[1 + 2 * n_src]
    q = q_ref[...]
    if scale != 1.0:
        q = q.astype(F32) * scale
    q = q.astype(BF16)
    tq = q.shape[0]
    dv = o_ref.shape[-1]

    def step(k, v, carry):
        m, l, acc = carry
        s = lax.dot_general(q, k, (((1,), (1,)), ((), ())), preferred_element_type=F32)
        m_new = jnp.maximum(m, jnp.max(s, axis=-1, keepdims=True))
        p = jnp.exp(s - m_new)
        alpha = jnp.exp(m - m_new)
        l = alpha * l + jnp.sum(p, axis=-1, keepdims=True)
        acc = alpha * acc + jnp.dot(p.astype(BF16), v, preferred_element_type=F32)
        return m_new, l, acc

    carry = (jnp.full((tq, 1), NEG_BIG, F32), jnp.zeros((tq, 1), F32), jnp.zeros((tq, dv), F32))
    for s_i in range(n_src):
        k_ref = refs[1 + 2 * s_i]
        v_ref = refs[2 + 2 * s_i]
        nk = k_ref.shape[0]
        tk = min(nk, tk_max)
        if nk == tk:
            carry = step(k_ref[...].astype(BF16), v_ref[...].astype(BF16), carry)
        else:
            def body(c, cr, k_ref=k_ref, v_ref=v_ref, tk=tk):
                k0 = pl.multiple_of(c * tk, tk)
                return step(k_ref[pl.ds(k0, tk), :].astype(BF16), v_ref[pl.ds(k0, tk), :].astype(BF16), cr)
            carry = lax.fori_loop(0, nk // tk, body, carry)
    _, l, acc = carry
    o_ref[...] = acc / l


def _attention(q_arr, q_blk0, dq, sources, dv, n_heads, scale, tq=256, tk_max=512):
    b, nq, _ = q_arr.shape
    tq = min(nq, tq)
    in_specs = [pl.BlockSpec((None, tq, dq), lambda bb, h, i: (bb, i, q_blk0 + h))]
    args = [q_arr]
    for k_arr, k_blk0, v_arr, v_blk0 in sources:
        nk = k_arr.shape[1]
        in_specs.append(pl.BlockSpec((None, nk, dq), lambda bb, h, i, o=k_blk0: (bb, 0, o + h)))
        in_specs.append(pl.BlockSpec((None, nk, dv), lambda bb, h, i, o=v_blk0: (bb, 0, o + h)))
        args += [k_arr, v_arr]
    return pl.pallas_call(
        functools.partial(_attn_kernel, n_src=len(sources), scale=scale, tk_max=tk_max),
        grid=(b, n_heads, nq // tq),
        in_specs=in_specs,
        out_specs=pl.BlockSpec((None, tq, dv), lambda bb, h, i: (bb, i, h)),
        out_shape=jax.ShapeDtypeStruct((b, nq, n_heads * dv), F32),
        compiler_params=_cparams(("parallel", "parallel", "arbitrary"), 48),
        name="attention",
    )(*args)


def _attn_t_kernel(*refs, n_src, tk_max, unroll, n_chains):
    q_ref = refs[0]
    o_ref = refs[1 + 2 * n_src]
    tq = q_ref.shape[1] // n_chains
    dv = o_ref.shape[-1]
    qs = [q_ref[:, j * tq:(j + 1) * tq] for j in range(n_chains)]

    def scores(k):
        return [jnp.dot(k, qs[j], preferred_element_type=F32) for j in range(n_chains)]

    def update(ss, vt, carry):
        out = []
        for j in range(n_chains):
            m, l, acc = carry[j]
            m_new = jnp.maximum(m, jnp.max(ss[j], axis=0, keepdims=True))
            p = jnp.exp2(ss[j] - m_new)
            alpha = jnp.exp2(m - m_new)
            l = alpha * l + jnp.sum(p, axis=0, keepdims=True)
            acc = alpha * acc + jnp.dot(vt, p.astype(BF16), preferred_element_type=F32)
            out.append((m_new, l, acc))
        return tuple(out)

    def step(k, vt, carry):
        return update(scores(k), vt, carry)

    carry = tuple((jnp.full((1, tq), NEG_BIG, F32), jnp.zeros((1, tq), F32), jnp.zeros((dv, tq), F32))
                  for _ in range(n_chains))
    for s_i in range(n_src):
        k_ref = refs[1 + 2 * s_i]
        vt_ref = refs[2 + 2 * s_i]
        nk = k_ref.shape[0]
        tk = min(nk, tk_max)
        if nk == tk:
            carry = step(k_ref[...], vt_ref[...], carry)
        elif (nk // tk) % 2 == 0:
            def body2(c2, cr, k_ref=k_ref, vt_ref=vt_ref, tk=tk):
                k0 = pl.multiple_of(c2 * (2 * tk), 2 * tk)
                k1 = pl.multiple_of(k0 + tk, tk)
                s0 = scores(k_ref[pl.ds(k0, tk), :])
                s1 = scores(k_ref[pl.ds(k1, tk), :])
                cr = update(s0, vt_ref[:, pl.ds(k0, tk)], cr)
                return update(s1, vt_ref[:, pl.ds(k1, tk)], cr)
            carry = lax.fori_loop(0, nk // (2 * tk), body2, carry)
        else:
            def body(c, cr, k_ref=k_ref, vt_ref=vt_ref, tk=tk):
                k0 = pl.multiple_of(c * tk, tk)
                return step(k_ref[pl.ds(k0, tk), :], vt_ref[:, pl.ds(k0, tk)], cr)
            carry = lax.fori_loop(0, nk // tk, body, carry, unroll=unroll)
    for j in range(n_chains):
        _, l, acc = carry[j]
        o_ref[j * tq:(j + 1) * tq, :] = (acc / l).T


def _attention_t(q_arr, sources, n_heads, dq, dv, tq=1024, tk_max=1024, unroll=1, n_chains=1):
    b, _, nq = q_arr.shape
    if nq < tq * n_chains:
        tq, n_chains = nq, 1
    tq = tq * n_chains
    in_specs = [pl.BlockSpec((None, dq, tq), lambda bb, h, i: (bb, h, i))]
    args = [q_arr]
    for k_arr, vt_arr in sources:
        nk = k_arr.shape[1]
        in_specs.append(pl.BlockSpec((None, nk, dq), lambda bb, h, i: (bb, 0, h)))
        in_specs.append(pl.BlockSpec((None, dv, nk), lambda bb, h, i: (bb, h, 0)))
        args += [k_arr, vt_arr]
    return pl.pallas_call(
        functools.partial(_attn_t_kernel, n_src=len(sources), tk_max=tk_max, unroll=unroll, n_chains=n_chains),
        grid=(b, n_heads, nq // tq),
        in_specs=in_specs,
        out_specs=pl.BlockSpec((None, tq, dv), lambda bb, h, i: (bb, i, h)),
        out_shape=jax.ShapeDtypeStruct((b, nq, n_heads * dv), F32),
        compiler_params=_cparams(("parallel", "parallel", "arbitrary"), 56),
        name="attention_t",
    )(*args)


def _nat_kernel(q_ref, k_ref, v_ref, kc_ref, vc_ref, bias_ref, o_ref, *, rows_per_step, n_rows):
    i = pl.program_id(2)
    scale = HEAD_DIM ** -0.5
    kc = kc_ref[...].astype(BF16)
    vc = vc_ref[...].astype(BF16)
    nt = (((1,), (1,)), ((), ()))
    win = NAT_ROWS * GRID_W
    for rr in range(rows_per_step):
        r = i * rows_per_step + rr
        r0 = jnp.clip(r - NAT_ROWS // 2, 0, n_rows - NAT_ROWS)
        base = r0 - r + NAT_ROWS - 1
        t0 = pl.multiple_of(r0 * GRID_W, GRID_W)
        q = (q_ref[rr * GRID_W:(rr + 1) * GRID_W, :].astype(F32) * scale).astype(BF16)
        ks = k_ref[pl.ds(t0, win), :].astype(BF16)
        vs = v_ref[pl.ds(t0, win), :].astype(BF16)
        s_w = lax.dot_general(q, ks, nt, preferred_element_type=F32) + bias_ref[base]
        s_c = lax.dot_general(q, kc, nt, preferred_element_type=F32)
        m = jnp.maximum(jnp.max(s_w, axis=-1, keepdims=True), jnp.max(s_c, axis=-1, keepdims=True))
        p_w = jnp.exp(s_w - m)
        p_c = jnp.exp(s_c - m)
        l = jnp.sum(p_w, axis=-1, keepdims=True) + jnp.sum(p_c, axis=-1, keepdims=True)
        o = (jnp.dot(p_w.astype(BF16), vs, preferred_element_type=F32)
             + jnp.dot(p_c.astype(BF16), vc, preferred_element_type=F32))
        o_ref[rr * GRID_W:(rr + 1) * GRID_W, :] = o / l


def _nat_bias_table(rpb):
    col = jnp.arange(GRID_W)
    c0 = jnp.clip(col - NAT_COLS // 2, 0, GRID_W - NAT_COLS)
    kc = jnp.arange(GRID_W)
    inside = (kc[None, :] >= c0[:, None]) & (kc[None, :] < c0[:, None] + NAT_COLS)
    dc = jnp.clip(kc[None, :] - col[:, None] + NAT_COLS - 1, 0, 2 * NAT_COLS - 2)
    per_dr = jnp.where(inside[None, None], rpb[:, :, dc], NEG_BIG)
    dr = jnp.arange(NAT_ROWS)[:, None] + jnp.arange(NAT_ROWS)[None, :]
    tbl = per_dr[:, dr]
    h = rpb.shape[0]
    return tbl.transpose(0, 1, 3, 2, 4).reshape(h, NAT_ROWS, GRID_W, NAT_ROWS * GRID_W).astype(F32)


def _nat_latent(p_l, p_c, bias_tbl):
    b, n, _ = p_l.shape
    n_ctx = p_c.shape[1]
    n_rows = n // GRID_W
    rps = 8
    tq = rps * GRID_W
    win = NAT_ROWS * GRID_W
    return pl.pallas_call(
        functools.partial(_nat_kernel, rows_per_step=rps, n_rows=n_rows),
        grid=(b, N_HEADS_NAT, n_rows // rps),
        in_specs=[pl.BlockSpec((None, tq, LANE), lambda bb, h, i: (bb, i, COL_NAT_Q + h)),
                  pl.BlockSpec((None, n, LANE), lambda bb, h, i: (bb, 0, COL_NAT_K + h)),
                  pl.BlockSpec((None, n, LANE), lambda bb, h, i: (bb, 0, COL_NAT_V + h)),
                  pl.BlockSpec((None, n_ctx, LANE), lambda bb, h, i: (bb, 0, COL_NAT_K + h)),
                  pl.BlockSpec((None, n_ctx, LANE), lambda bb, h, i: (bb, 0, COL_NAT_V + h)),
                  pl.BlockSpec((None, NAT_ROWS, GRID_W, win), lambda bb, h, i: (h, 0, 0, 0))],
        out_specs=pl.BlockSpec((None, tq, LANE), lambda bb, h, i: (bb, i, h)),
        out_shape=jax.ShapeDtypeStruct((b, n, N_HEADS_NAT * HEAD_DIM), F32),
        compiler_params=_cparams(("parallel", "parallel", "arbitrary"), 48),
        name="nat_latent",
    )(p_l, p_l, p_l, p_c, p_c, bias_tbl)


def _ret_consts(logit, reverse):
    c_len = RET_CHUNK
    z = jnp.full((c_len, c_len), -logit, F32)
    lg = -(jnp.maximum(z, 0.0) + jnp.log1p(jnp.exp(-jnp.abs(z))))
    ci = lax.broadcasted_iota(jnp.int32, (c_len, c_len), 0).astype(F32)
    si = lax.broadcasted_iota(jnp.int32, (c_len, c_len), 1).astype(F32)
    if reverse:
        rel = si - ci
        wq = jnp.exp(lg * (c_len - ci))
        wk = jnp.exp(lg * ci)
    else:
        rel = ci - si
        wq = jnp.exp(lg * (ci + 1.0))
        wk = jnp.exp(lg * (c_len - 1.0 - ci))
    dmat = jnp.where(rel >= 0, jnp.exp(lg * jnp.maximum(rel, 0.0)), 0.0)
    return wq, wk, dmat, jnp.exp(lg * c_len)


def _ret_chunk(q_ref, k_ref, v_ref, g_ref, y_ref, sl, state, consts):
    wq, wk, dmat, g_chunk = consts
    nt = (((1,), (1,)), ((), ()))
    qc = q_ref[sl, :].astype(F32)
    kc = k_ref[sl, :].astype(F32) * (HEAD_DIM ** -0.5)
    vb = v_ref[sl, :].astype(BF16)
    sc = lax.dot_general(qc.astype(BF16), kc.astype(BF16), nt, preferred_element_type=F32) * dmat
    o = (jnp.dot(sc.astype(BF16), vb, preferred_element_type=F32)
         + jnp.dot((qc * wq).astype(BF16), state.astype(BF16), preferred_element_type=F32))
    state = g_chunk * state + jnp.dot((kc * wk).T.astype(BF16), vb, preferred_element_type=F32)
    mu = jnp.mean(o, axis=-1, keepdims=True)
    d = o - mu
    gn = d * lax.rsqrt(jnp.mean(d * d, axis=-1, keepdims=True) + EPS)
    y_ref[sl, :] = _silu(g_ref[sl, :].astype(F32)) * gn
    return state


def _ret_kernel(lf_ref, lb_ref, qf_ref, kf_ref, vf_ref, gf_ref, qb_ref, kb_ref, vb_ref, gb_ref, s0f_ref, s0b_ref,
                yf_ref, yb_ref, sf_ref, sb_ref, c_scr, *, n_chunks):
    h = pl.program_id(1)

    @pl.when(pl.program_id(2) == 0)
    def _():
        sf_ref[...] = s0f_ref[...]
        sb_ref[...] = s0b_ref[...]
        for j, arr in enumerate(_ret_consts(lf_ref[h], False) + _ret_consts(lb_ref[h], True)):
            c_scr[j] = arr

    cf = tuple(c_scr[j] for j in range(4))
    cb = tuple(c_scr[4 + j] for j in range(4))
    st_f = sf_ref[...]
    st_b = sb_ref[...]
    c_len = RET_CHUNK
    for c in range(n_chunks):
        cr = n_chunks - 1 - c
        st_f = _ret_chunk(qf_ref, kf_ref, vf_ref, gf_ref, yf_ref, slice(c * c_len, (c + 1) * c_len), st_f, cf)
        st_b = _ret_chunk(qb_ref, kb_ref, vb_ref, gb_ref, yb_ref, slice(cr * c_len, (cr + 1) * c_len), st_b, cb)
    sf_ref[...] = st_f
    sb_ref[...] = st_b


def _retention_scan(p, logit_f, logit_b, s0_f, s0_b):
    b, n, _ = p.shape
    hh = N_HEADS_RET
    tb = min(n, 4 * RET_CHUNK)
    nblk = n // tb

    def fwd(c0):
        return pl.BlockSpec((None, tb, LANE), lambda bb, h, i: (bb, i, c0 + h))

    def bwd(c0):
        return pl.BlockSpec((None, tb, LANE), lambda bb, h, i: (bb, nblk - 1 - i, c0 + h))

    state = pl.BlockSpec((None, None, HEAD_DIM, HEAD_DIM), lambda bb, h, i: (bb, h, 0, 0))
    smem = pl.BlockSpec(memory_space=pltpu.SMEM)
    y_shape = jax.ShapeDtypeStruct((b, n, hh * HEAD_DIM), F32)
    s_shape = jax.ShapeDtypeStruct((b, hh, HEAD_DIM, HEAD_DIM), F32)
    return pl.pallas_call(
        functools.partial(_ret_kernel, n_chunks=tb // RET_CHUNK),
        grid=(b, hh, nblk),
        in_specs=[smem, smem,
                  fwd(COL_RET_Q), fwd(COL_RET_K), fwd(COL_RET_V), fwd(COL_RET_GF),
                  bwd(COL_RET_Q), bwd(COL_RET_K), bwd(COL_RET_V), bwd(COL_RET_GB),
                  state, state],
        out_specs=[fwd(0), bwd(0), state, state],
        out_shape=[y_shape, y_shape, s_shape, s_shape],
        scratch_shapes=[pltpu.VMEM((8, RET_CHUNK, RET_CHUNK), F32)],
        compiler_params=_cparams(("parallel", "parallel", "arbitrary"), 32),
        name="retention",
    )(logit_f.astype(F32), logit_b.astype(F32), p, p, p, p, p, p, p, p, s0_f, s0_b)


def _retention(p_c, p_l, logit_f, logit_b):
    b = p_c.shape[0]
    s0 = jnp.zeros((b, N_HEADS_RET, HEAD_DIM, HEAD_DIM), F32)
    yc_f, yc_b, s_cf, s_cb = _retention_scan(p_c, logit_f, logit_b, s0, s0)
    yl_f, yl_b, _, _ = _retention_scan(p_l, logit_f, logit_b, s_cf, s_cb)
    return (yc_f, yc_b), (yl_f, yl_b)


def _merge_kernel(x_ref, ya_ref, yb_ref, yrf_ref, yrb_ref, g_ref, gate_ref, w_ref, o_ref):
    wa = ya_ref.shape[-1]
    wb = yb_ref.shape[-1]
    na = (_rms(ya_ref[...]) * g_ref[:, :wa]).astype(BF16)
    nb = (_rms(yb_ref[...]) * g_ref[:, wa:wa + wb]).astype(BF16)
    nr = (_rms(yrf_ref[...] + yrb_ref[...]) * g_ref[:, wa + wb:]).astype(BF16)
    acc = jnp.dot(na, w_ref[:wa, :], preferred_element_type=F32)
    acc = acc + jnp.dot(nb, w_ref[wa:wa + wb, :], preferred_element_type=F32)
    acc = acc + jnp.dot(nr, w_ref[wa + wb:, :], preferred_element_type=F32)
    o_ref[...] = x_ref[...] + gate_ref[...] * acc


def _merge_out(x, ya, yb, yr, g, gate, w_out):
    b, n, d = x.shape
    yrf, yrb = yr
    tm = min(n, 256)
    dm = w_out.shape[0]

    def tok(width):
        return pl.BlockSpec((None, tm, width), lambda bb, i: (bb, i, 0))

    return pl.pallas_call(
        _merge_kernel,
        grid=(b, n // tm),
        in_specs=[tok(d), tok(ya.shape[-1]), tok(yb.shape[-1]), tok(yrf.shape[-1]), tok(yrb.shape[-1]),
                  pl.BlockSpec((1, dm), lambda bb, i: (0, 0)),
                  pl.BlockSpec((None, 1, d), lambda bb, i: (bb, 0, 0)),
                  pl.BlockSpec((dm, d), lambda bb, i: (0, 0))],
        out_specs=tok(d),
        out_shape=jax.ShapeDtypeStruct((b, n, d), F32),
        compiler_params=_cparams(("parallel", "parallel"), 48),
        name="merge_out",
    )(x, ya, yb, yrf, yrb, g.reshape(1, dm), gate, w_out)


def _moe_pre_kernel(x_ref, g_ref, sh_ref, sc_ref, wr_ref, h_ref, aff_ref, *, n_experts):
    h = (_rms(x_ref[...]) * g_ref[...]) * (1.0 + sc_ref[...]) + sh_ref[...]
    h_ref[...] = h.astype(BF16)
    h_hi = h.astype(BF16)
    h_lo = (h - h_hi.astype(F32)).astype(BF16)
    w = wr_ref[...]
    w_hi = w.astype(BF16)
    w_lo = (w - w_hi.astype(F32)).astype(BF16)
    logits = (jnp.dot(h_hi, w_hi, preferred_element_type=F32)
              + (jnp.dot(h_hi, w_lo, preferred_element_type=F32) + jnp.dot(h_lo, w_hi, preferred_element_type=F32)))
    lane = lax.broadcasted_iota(jnp.int32, logits.shape, 1)
    logits = jnp.where(lane < n_experts, logits, NEG_BIG)
    e = jnp.exp(logits - jnp.max(logits, axis=-1, keepdims=True))
    aff_ref[...] = e / jnp.sum(e, axis=-1, keepdims=True)


def _moe_pre(x, g, shift, scale, w_router_pad):
    b, n, d = x.shape
    tm = min(n, 256)
    return pl.pallas_call(
        functools.partial(_moe_pre_kernel, n_experts=N_EXPERTS),
        grid=(b, n // tm),
        in_specs=[pl.BlockSpec((None, tm, d), lambda bb, i: (bb, i, 0)),
                  pl.BlockSpec((1, d), lambda bb, i: (0, 0)),
                  pl.BlockSpec((None, 1, d), lambda bb, i: (bb, 0, 0)),
                  pl.BlockSpec((None, 1, d), lambda bb, i: (bb, 0, 0)),
                  pl.BlockSpec((d, LANE), lambda bb, i: (0, 0))],
        out_specs=[pl.BlockSpec((None, tm, d), lambda bb, i: (bb, i, 0)),
                   pl.BlockSpec((None, tm, LANE), lambda bb, i: (bb, i, 0))],
        out_shape=[jax.ShapeDtypeStruct((b, n, d), BF16), jax.ShapeDtypeStruct((b, n, LANE), F32)],
        compiler_params=_cparams(("parallel", "parallel"), 40),
        name="moe_pre",
    )(x, g.reshape(1, d), shift, scale, w_router_pad)


def _ffn_kernel(xs_ref, gate_ref, wg_ref, wu_ref, wd_ref, o_ref):
    xs = xs_ref[...]
    a = jnp.dot(xs, wg_ref[...], preferred_element_type=F32)
    u = jnp.dot(xs, wu_ref[...], preferred_element_type=F32)
    hm = (_silu(a) * u).astype(BF16)
    o_ref[...] = jnp.dot(hm, wd_ref[...], preferred_element_type=F32) * gate_ref[...]


def _expert_ffn(xs, gates, w_gate, w_up, w_down):
    e, t, d = xs.shape
    f = w_gate.shape[-1]
    tm = min(t, 512)
    return pl.pallas_call(
        _ffn_kernel,
        grid=(e, t // tm),
        in_specs=[pl.BlockSpec((None, tm, d), lambda ee, i: (ee, i, 0)),
                  pl.BlockSpec((None, tm, 1), lambda ee, i: (ee, i, 0)),
                  pl.BlockSpec((None, d, f), lambda ee, i: (ee, 0, 0)),
                  pl.BlockSpec((None, d, f), lambda ee, i: (ee, 0, 0)),
                  pl.BlockSpec((None, f, d), lambda ee, i: (ee, 0, 0))],
        out_specs=pl.BlockSpec((None, tm, d), lambda ee, i: (ee, i, 0)),
        out_shape=jax.ShapeDtypeStruct((e, t, d), F32),
        compiler_params=_cparams(("parallel", "arbitrary"), 56),
        name="expert_ffn",
    )(xs, gates, w_gate, w_up, w_down)


def _residual_kernel(x_ref, m_ref, gate_ref, gn_ref, o_ref, *, final_norm):
    y = x_ref[...] + gate_ref[...] * m_ref[...]
    if final_norm:
        y = _rms(y) * gn_ref[...]
    o_ref[...] = y


def _gated_residual(x, moe, gate, norm_g, final_norm):
    b, n, d = x.shape
    tm = min(n, 512)
    tok = pl.BlockSpec((None, tm, d), lambda bb, i: (bb, i, 0))
    return pl.pallas_call(
        functools.partial(_residual_kernel, final_norm=final_norm),
        grid=(b, n // tm),
        in_specs=[tok, tok,
                  pl.BlockSpec((None, 1, d), lambda bb, i: (bb, 0, 0)),
                  pl.BlockSpec((1, d), lambda bb, i: (0, 0))],
        out_specs=tok,
        out_shape=jax.ShapeDtypeStruct((b, n, d), F32),
        compiler_params=_cparams(("parallel", "parallel"), 40),
        name="gated_residual",
    )(x, moe, gate, norm_g.reshape(1, d))


def _ec_moe(x, g, shift, scale, w_router_pad, w_gate, w_up, w_down):
    b, n, d = x.shape
    cap = EC_CAPACITY * n // N_EXPERTS
    h, aff = _moe_pre(x, g, shift, scale, w_router_pad)
    gate, idx = lax.top_k(jnp.swapaxes(aff[..., :N_EXPERTS], 1, 2), cap)
    flat = (idx + (jnp.arange(b, dtype=idx.dtype) * n)[:, None, None]).transpose(1, 0, 2).reshape(-1)
    xs = jnp.take(h.reshape(b * n, d), flat, axis=0).reshape(N_EXPERTS, b * cap, d)
    gates = gate.transpose(1, 0, 2).reshape(N_EXPERTS, b * cap, 1)
    y = _expert_ffn(xs, gates, w_gate, w_up, w_down)
    out = jnp.zeros((b * n, d), F32).at[flat].add(y.reshape(-1, d))
    return out.reshape(b, n, d)


_ROPE_SRC = np.concatenate([np.arange(16, 32), np.arange(0, 16), np.arange(48, 64), np.arange(32, 48)])
_ROPE_SIGN = np.concatenate([-np.ones(16), np.ones(16), -np.ones(16), np.ones(16)]).astype(np.float32)


def _rope_perm_cols(w):
    return w[:, _ROPE_SRC] * _ROPE_SIGN


def _pad_w_in(w_in):
    c_kr = MLA_Q_LORA + MLA_KV_LORA
    kr = w_in[:, c_kr:c_kr + MLA_ROPE_DIM]
    return jnp.concatenate([w_in[:, :c_kr + MLA_ROPE_DIM], _rope_perm_cols(kr), w_in[:, c_kr + MLA_ROPE_DIM:]],
                           axis=1).astype(BF16)


def _expand_w_uq(w_uq):
    w = w_uq.reshape(MLA_Q_LORA, N_HEADS_MLA, HEAD_DIM + MLA_ROPE_DIM)
    nope = w[:, :, :HEAD_DIM]
    rope = w[:, :, HEAD_DIM:]
    partner = rope[:, :, _ROPE_SRC] * _ROPE_SIGN
    return jnp.concatenate([nope, rope, partner], axis=-1).reshape(MLA_Q_LORA, N_HEADS_MLA * MLA_QK_PAD).astype(BF16)


def _rope_tables(n):
    t = jnp.arange(n)
    n_freq = MLA_ROPE_DIM // 4
    inv = ROPE_BASE ** (-jnp.arange(n_freq, dtype=F32) / n_freq)
    ang_r = (t // GRID_W).astype(F32)[:, None] * inv
    ang_c = (t % GRID_W).astype(F32)[:, None] * inv
    zeros = jnp.zeros((n, LANE - MLA_ROPE_DIM), F32)
    cos = jnp.concatenate([jnp.cos(ang_r), jnp.cos(ang_r), jnp.cos(ang_c), jnp.cos(ang_c), zeros], axis=1)
    sin = jnp.concatenate([jnp.sin(ang_r), jnp.sin(ang_r), jnp.sin(ang_c), jnp.sin(ang_c), zeros], axis=1)
    return cos, sin


def _identity_tables(n):
    ones = jnp.ones((n, MLA_ROPE_DIM), F32)
    zeros = jnp.zeros((n, LANE - MLA_ROPE_DIM), F32)
    return jnp.concatenate([ones, zeros], axis=1), jnp.zeros((n, LANE), F32)


def _layer(x_l, x_c, mod, lp, last, final_g):
    b, n, d = x_l.shape
    n_ctx = x_c.shape[1]

    def chunk(k, ctx):
        rows = jnp.broadcast_to(mod[b, k * d:(k + 1) * d], (b, d)) if ctx else mod[:b, k * d:(k + 1) * d]
        return rows.reshape(b, 1, d)

    w_in = _pad_w_in(lp['w_in'])
    p_l = _norm_proj(x_l, lp['norm1_g'], chunk(0, False), chunk(1, False), w_in)
    p_c = _norm_proj(x_c, lp['norm1_g'], chunk(0, True), chunk(1, True), w_in)

    cos_l, sin_l = _rope_tables(n)
    cos_c, sin_c = _identity_tables(n_ctx)
    w_uk = lp['mla_w_uk'].reshape(MLA_KV_LORA, -1).astype(BF16)
    w_uv = lp['mla_w_uv'].reshape(MLA_KV_LORA, -1).astype(BF16)
    w_q = _expand_w_uq(lp['mla_w_uq'])
    k_l, v_l = _mla_kv_prep(p_l, lp['mla_kv_norm_g'], w_uk, w_uv, cos_l, sin_l)
    k_c, v_c = _mla_kv_prep(p_c, lp['mla_kv_norm_g'], w_uk, w_uv, cos_c, sin_c)
    q_l = _mla_q_prep(p_l, lp['mla_q_norm_g'], w_q, cos_l, sin_l)
    a_l = _attention_t(q_l, [(k_c, v_c), (k_l, v_l)], N_HEADS_MLA, MLA_QK_PAD, HEAD_DIM)

    b_l = _nat_latent(p_l, p_c, _nat_bias_table(lp['nat_rpb']))
    r_c, r_l = _retention(p_c, p_l, lp['ret_decay_f'], lp['ret_decay_b'])

    w_out = lp['w_out'].astype(BF16)
    w_router = jnp.zeros((d, LANE), F32).at[:, :N_EXPERTS].set(lp['w_router'])
    w_gate = lp['w_gate'].astype(BF16)
    w_up = lp['w_up'].astype(BF16)
    w_down = lp['w_down'].astype(BF16)

    x_l = _merge_out(x_l, a_l, b_l, r_l, lp['out_norm_g'], chunk(2, False), w_out)
    moe_l = _ec_moe(x_l, lp['norm2_g'], chunk(3, False), chunk(4, False), w_router, w_gate, w_up, w_down)
    x_l = _gated_residual(x_l, moe_l, chunk(5, False), final_g, last)
    if last:
        return x_l, None

    q_c = _mla_q_prep(p_c, lp['mla_q_norm_g'], w_q, cos_c, sin_c)
    a_c = _attention_t(q_c, [(k_c, v_c)], N_HEADS_MLA, MLA_QK_PAD, HEAD_DIM)
    b_c = _attention(p_c, COL_NAT_Q, HEAD_DIM, [(p_c, COL_NAT_K, p_c, COL_NAT_V)], HEAD_DIM, N_HEADS_NAT,
                     HEAD_DIM ** -0.5)
    x_c = _merge_out(x_c, a_c, b_c, r_c, lp['out_norm_g'], chunk(2, True), w_out)
    moe_c = _ec_moe(x_c, lp['norm2_g'], chunk(3, True), chunk(4, True), w_router, w_gate, w_up, w_down)
    x_c = _gated_residual(x_c, moe_c, chunk(5, True), final_g, False)
    return x_l, x_c


def kernel(x, c, ctx, c_ctx, w_mod, b_mod, norm1_g, w_in, mla_q_norm_g, mla_kv_norm_g, mla_w_uq, mla_w_uk,
           mla_w_uv, nat_rpb, ret_decay_f, ret_decay_b, out_norm_g, w_out, norm2_g, w_router, w_gate, w_up,
           w_down, final_norm_g):
    depth = w_mod.shape[0]
    cvecs = jnp.concatenate([c, c_ctx[None, :]], axis=0)
    x_l, x_c = x, ctx
    for i in range(depth):
        lp = {
            'norm1_g': norm1_g[i], 'w_in': w_in[i], 'mla_q_norm_g': mla_q_norm_g[i],
            'mla_kv_norm_g': mla_kv_norm_g[i], 'mla_w_uq': mla_w_uq[i], 'mla_w_uk': mla_w_uk[i],
            'mla_w_uv': mla_w_uv[i], 'nat_rpb': nat_rpb[i], 'ret_decay_f': ret_decay_f[i],
            'ret_decay_b': ret_decay_b[i], 'out_norm_g': out_norm_g[i], 'w_out': w_out[i],
            'norm2_g': norm2_g[i], 'w_router': w_router[i], 'w_gate': w_gate[i], 'w_up': w_up[i],
            'w_down': w_down[i],
        }
        mod = _modulation(cvecs, w_mod[i], b_mod[i])
        x_l, x_c = _layer(x_l, x_c, mod, lp, i == depth - 1, final_norm_g)
    return x_l
```

```python
import functools

import numpy as np
import jax
import jax.numpy as jnp
from jax import lax
from jax.experimental import pallas as pl
from jax.experimental.pallas import tpu as pltpu

F32 = jnp.float32
BF16 = jnp.bfloat16

GRID_W = 64
HEAD_DIM = 128
N_HEADS_MLA = 8
N_HEADS_NAT = 4
N_HEADS_RET = 4
MLA_ROPE_DIM = 64
MLA_Q_LORA = 512
MLA_KV_LORA = 256
MLA_SCALE = (HEAD_DIM + MLA_ROPE_DIM) ** -0.5
LOG2E = 1.4426950408889634
MLA_Q_SCALE = MLA_SCALE * LOG2E
MLA_QK_PAD = 256
NAT_ROWS = 8
NAT_COLS = 16
RET_CHUNK = 128
N_EXPERTS = 16
EC_CAPACITY = 2
ROPE_BASE = 10000.0
EPS = 1e-6
NEG_BIG = -1e30
LANE = 128

COL_CQ = 0
COL_CKV = 4
COL_KR = 6
COL_NAT_Q = 7
COL_NAT_K = 11
COL_NAT_V = 15
COL_RET_Q = 19
COL_RET_K = 23
COL_RET_V = 27
COL_RET_GF = 31
COL_RET_GB = 35
IN_COLS_PAD = 39 * LANE


def _cparams(sem, vmem_mib):
    return pltpu.CompilerParams(dimension_semantics=sem, vmem_limit_bytes=vmem_mib * 1024 * 1024)


def _silu(a):
    return a * (1.0 / (1.0 + jnp.exp(-a)))


def _rms(x):
    return x * lax.rsqrt(jnp.mean(x * x, axis=-1, keepdims=True) + EPS)


def _mod_kernel(ct_ref, w_ref, b_ref, o_ref, a_scr, *, n_rows, k_chunk):
    ct = ct_ref[...]
    a_scr[...] = _silu(ct)
    d = w_ref.shape[0]
    tn = w_ref.shape[1]

    def body(kc, accs):
        k0 = pl.multiple_of(kc * k_chunk, k_chunk)
        wblk = w_ref[pl.ds(k0, k_chunk), :]
        out = []
        for r in range(n_rows):
            col = a_scr[pl.ds(k0, k_chunk), r:r + 1]
            out.append(accs[r] + jnp.sum((col * wblk).reshape(k_chunk // 8, 8, tn), axis=0))
        return tuple(out)

    accs = lax.fori_loop(0, d // k_chunk, body, tuple(jnp.zeros((8, tn), F32) for _ in range(n_rows)))
    o_ref[...] = jnp.zeros(o_ref.shape, F32)
    for r in range(n_rows):
        o_ref[r:r + 1, :] = jnp.sum(accs[r], axis=0, keepdims=True) + b_ref[...]


def _modulation(cvecs, w_mod, b_mod):
    n_rows, d = cvecs.shape
    n_out = w_mod.shape[1]
    tn = 1024
    ct = jnp.zeros((d, 8), F32).at[:, :n_rows].set(cvecs.T)
    return pl.pallas_call(
        functools.partial(_mod_kernel, n_rows=n_rows, k_chunk=64),
        grid=(n_out // tn,),
        in_specs=[pl.BlockSpec((d, 8), lambda j: (0, 0)),
                  pl.BlockSpec((d, tn), lambda j: (0, j)),
                  pl.BlockSpec((1, tn), lambda j: (0, j))],
        out_specs=pl.BlockSpec((8, tn), lambda j: (0, j)),
        out_shape=jax.ShapeDtypeStruct((8, n_out), F32),
        scratch_shapes=[pltpu.VMEM((d, 8), F32)],
        compiler_params=_cparams(("parallel",), 40),
        name="modulation",
    )(ct, w_mod, b_mod.reshape(1, n_out))


def _norm_proj_kernel(x_ref, g_ref, sh_ref, sc_ref, w_ref, o_ref, h_scr):
    @pl.when(pl.program_id(2) == 0)
    def _():
        y = _rms(x_ref[...]) * g_ref[...]
        h_scr[...] = (y * (1.0 + sc_ref[...]) + sh_ref[...]).astype(BF16)

    o_ref[...] = jnp.dot(h_scr[...], w_ref[...], preferred_element_type=F32).astype(o_ref.dtype)


def _norm_proj(x, g, shift, scale, w):
    b, n, d = x.shape
    c = w.shape[1]
    tm = min(n, 1024)
    tn = c // 3 if (c % (3 * LANE) == 0) else c
    return pl.pallas_call(
        _norm_proj_kernel,
        grid=(b, n // tm, c // tn),
        in_specs=[pl.BlockSpec((None, tm, d), lambda bb, i, j: (bb, i, 0)),
                  pl.BlockSpec((1, d), lambda bb, i, j: (0, 0)),
                  pl.BlockSpec((None, 1, d), lambda bb, i, j: (bb, 0, 0)),
                  pl.BlockSpec((None, 1, d), lambda bb, i, j: (bb, 0, 0)),
                  pl.BlockSpec((d, tn), lambda bb, i, j: (0, j))],
        out_specs=pl.BlockSpec((None, tm, tn), lambda bb, i, j: (bb, i, j)),
        out_shape=jax.ShapeDtypeStruct((b, n, c), BF16),
        scratch_shapes=[pltpu.VMEM((tm, d), BF16)],
        compiler_params=_cparams(("parallel", "parallel", "arbitrary"), 56),
        name="norm_proj",
    )(x, g.reshape(1, d), shift, scale, w)


def _rope_rotate(r, cos, sin):
    return r * cos + pltpu.roll(r, 64, 1) * sin


def _mla_q_kernel(cq_ref, g_ref, w_ref, cos_ref, sin_ref, o_ref, *, n_heads):
    cqn = (_rms(cq_ref[...].astype(F32)) * g_ref[...]).astype(BF16)
    qe = jnp.dot(cqn, w_ref[...], preferred_element_type=F32)
    cos = cos_ref[...]
    sin = sin_ref[...]
    for h in range(n_heads):
        c0 = h * MLA_QK_PAD
        o_ref[c0:c0 + LANE, :] = (qe[:, c0:c0 + LANE] * MLA_Q_SCALE).T.astype(BF16)
        rot = _rope_rotate(qe[:, c0 + LANE:c0 + 2 * LANE], cos, sin)
        o_ref[c0 + LANE:c0 + 2 * LANE, :] = (rot * MLA_Q_SCALE).T.astype(BF16)


def _mla_q_prep(p, g, w_q, cos, sin):
    b, n, _ = p.shape
    tm = min(n, 512)
    cw = N_HEADS_MLA * MLA_QK_PAD
    return pl.pallas_call(
        functools.partial(_mla_q_kernel, n_heads=N_HEADS_MLA),
        grid=(b, n // tm),
        in_specs=[pl.BlockSpec((None, tm, MLA_Q_LORA), lambda bb, i: (bb, i, COL_CQ * LANE // MLA_Q_LORA)),
                  pl.BlockSpec((1, MLA_Q_LORA), lambda bb, i: (0, 0)),
                  pl.BlockSpec((MLA_Q_LORA, cw), lambda bb, i: (0, 0)),
                  pl.BlockSpec((tm, LANE), lambda bb, i: (i, 0)),
                  pl.BlockSpec((tm, LANE), lambda bb, i: (i, 0))],
        out_specs=pl.BlockSpec((None, cw, tm), lambda bb, i: (bb, 0, i)),
        out_shape=jax.ShapeDtypeStruct((b, cw, n), BF16),
        compiler_params=_cparams(("parallel", "parallel"), 40),
        name="mla_q_prep",
    )(p, g.reshape(1, MLA_Q_LORA), w_q, cos, sin)


def _mla_kv_kernel(ckv_ref, kr_ref, g_ref, wuk_ref, wuv_ref, cos_ref, sin_ref, k_ref, v_ref, *, n_heads):
    ckvn = (_rms(ckv_ref[...].astype(F32)) * g_ref[...]).astype(BF16)
    kn = jnp.dot(ckvn, wuk_ref[...], preferred_element_type=F32)
    v_ref[...] = jnp.dot(ckvn, wuv_ref[...], preferred_element_type=F32).T.astype(BF16)
    rot = _rope_rotate(kr_ref[...].astype(F32), cos_ref[...], sin_ref[...]).astype(BF16)
    for h in range(n_heads):
        c0 = h * MLA_QK_PAD
        k_ref[:, c0:c0 + LANE] = kn[:, h * LANE:(h + 1) * LANE].astype(BF16)
        k_ref[:, c0 + LANE:c0 + 2 * LANE] = rot


def _mla_kv_prep(p, g, w_uk, w_uv, cos, sin):
    b, n, _ = p.shape
    tm = min(n, 512)
    kw = N_HEADS_MLA * MLA_QK_PAD
    vw = N_HEADS_MLA * HEAD_DIM
    return pl.pallas_call(
        functools.partial(_mla_kv_kernel, n_heads=N_HEADS_MLA),
        grid=(b, n // tm),
        in_specs=[pl.BlockSpec((None, tm, MLA_KV_LORA), lambda bb, i: (bb, i, COL_CKV * LANE // MLA_KV_LORA)),
                  pl.BlockSpec((None, tm, LANE), lambda bb, i: (bb, i, COL_KR)),
                  pl.BlockSpec((1, MLA_KV_LORA), lambda bb, i: (0, 0)),
                  pl.BlockSpec((MLA_KV_LORA, vw), lambda bb, i: (0, 0)),
                  pl.BlockSpec((MLA_KV_LORA, vw), lambda bb, i: (0, 0)),
                  pl.BlockSpec((tm, LANE), lambda bb, i: (i, 0)),
                  pl.BlockSpec((tm, LANE), lambda bb, i: (i, 0))],
        out_specs=[pl.BlockSpec((None, tm, kw), lambda bb, i: (bb, i, 0)),
                   pl.BlockSpec((None, vw, tm), lambda bb, i: (bb, 0, i))],
        out_shape=[jax.ShapeDtypeStruct((b, n, kw), BF16), jax.ShapeDtypeStruct((b, vw, n), BF16)],
        compiler_params=_cparams(("parallel", "parallel"), 40),
        name="mla_kv_prep",
    )(p, p, g.reshape(1, MLA_KV_LORA), w_uk, w_uv, cos, sin)


def _attn_kernel(*refs, n_src, scale, tk_max):
    q_ref = refs[0]
    o_ref = refs[1 + 2 * n_src]
    q = q_ref[...]
    if scale != 1.0:
        q = q.astype(F32) * scale
    q = q.astype(BF16)
    tq = q.shape[0]
    dv = o_ref.shape[-1]

    def step(k, v, carry):
        m, l, acc = carry
        s = lax.dot_general(q, k, (((1,), (1,)), ((), ())), preferred_element_type=F32)
        m_new = jnp.maximum(m, jnp.max(s, axis=-1, keepdims=True))
        p = jnp.exp(s - m_new)
        alpha = jnp.exp(m - m_new)
        l = alpha * l + jnp.sum(p, axis=-1, keepdims=True)
        acc = alpha * acc + jnp.dot(p.astype(BF16), v, preferred_element_type=F32)
        return m_new, l, acc

    carry = (jnp.full((tq, 1), NEG_BIG, F32), jnp.zeros((tq, 1), F32), jnp.zeros((tq, dv), F32))
    for s_i in range(n_src):
        k_ref = refs[1 + 2 * s_i]
        v_ref = refs[2 + 2 * s_i]
        nk = k_ref.shape[0]
        tk = min(nk, tk_max)
        if nk == tk:
            carry = step(k_ref[...].astype(BF16), v_ref[...].astype(BF16), carry)
        else:
            def body(c, cr, k_ref=k_ref, v_ref=v_ref, tk=tk):
                k0 = pl.multiple_of(c * tk, tk)
                return step(k_ref[pl.ds(k0, tk), :].astype(BF16), v_ref[pl.ds(k0, tk), :].astype(BF16), cr)
            carry = lax.fori_loop(0, nk // tk, body, carry)
    _, l, acc = carry
    o_ref[...] = acc / l


def _attention(q_arr, q_blk0, dq, sources, dv, n_heads, scale, tq=256, tk_max=512):
    b, nq, _ = q_arr.shape
    tq = min(nq, tq)
    in_specs = [pl.BlockSpec((None, tq, dq), lambda bb, h, i: (bb, i, q_blk0 + h))]
    args = [q_arr]
    for k_arr, k_blk0, v_arr, v_blk0 in sources:
        nk = k_arr.shape[1]
        in_specs.append(pl.BlockSpec((None, nk, dq), lambda bb, h, i, o=k_blk0: (bb, 0, o + h)))
        in_specs.append(pl.BlockSpec((None, nk, dv), lambda bb, h, i, o=v_blk0: (bb, 0, o + h)))
        args += [k_arr, v_arr]
    return pl.pallas_call(
        functools.partial(_attn_kernel, n_src=len(sources), scale=scale, tk_max=tk_max),
        grid=(b, n_heads, nq // tq),
        in_specs=in_specs,
        out_specs=pl.BlockSpec((None, tq, dv), lambda bb, h, i: (bb, i, h)),
        out_shape=jax.ShapeDtypeStruct((b, nq, n_heads * dv), F32),
        compiler_params=_cparams(("parallel", "parallel", "arbitrary"), 48),
        name="attention",
    )(*args)


def _attn_t_kernel(*refs, n_src, tk_max, unroll, n_chains):
    q_ref = refs[0]
    o_ref = refs[1 + 2 * n_src]
    tq = q_ref.shape[1] // n_chains
    dv = o_ref.shape[-1]
    qs = [q_ref[:, j * tq:(j + 1) * tq] for j in range(n_chains)]

    def scores(k):
        return [jnp.dot(k, qs[j], preferred_element_type=F32) for j in range(n_chains)]

    def update(ss, vt, carry):
        out = []
        for j in range(n_chains):
            m, l, acc = carry[j]
            m_new = jnp.maximum(m, jnp.max(ss[j], axis=0, keepdims=True))
            p = jnp.exp2(ss[j] - m_new)
            alpha = jnp.exp2(m - m_new)
            l = alpha * l + jnp.sum(p, axis=0, keepdims=True)
            acc = alpha * acc + jnp.dot(vt, p.astype(BF16), preferred_element_type=F32)
            out.append((m_new, l, acc))
        return tuple(out)

    def step(k, vt, carry):
        return update(scores(k), vt, carry)

    carry = tuple((jnp.full((1, tq), NEG_BIG, F32), jnp.zeros((1, tq), F32), jnp.zeros((dv, tq), F32))
                  for _ in range(n_chains))
    for s_i in range(n_src):
        k_ref = refs[1 + 2 * s_i]
        vt_ref = refs[2 + 2 * s_i]
        nk = k_ref.shape[0]
        tk = min(nk, tk_max)
        if nk == tk:
            carry = step(k_ref[...], vt_ref[...], carry)
        elif (nk // tk) % 2 == 0:
            def body2(c2, cr, k_ref=k_ref, vt_ref=vt_ref, tk=tk):
                k0 = pl.multiple_of(c2 * (2 * tk), 2 * tk)
                k1 = pl.multiple_of(k0 + tk, tk)
                s0 = scores(k_ref[pl.ds(k0, tk), :])
                s1 = scores(k_ref[pl.ds(k1, tk), :])
                cr = update(s0, vt_ref[:, pl.ds(k0, tk)], cr)
                return update(s1, vt_ref[:, pl.ds(k1, tk)], cr)
            carry = lax.fori_loop(0, nk // (2 * tk), body2, carry)
        else:
            def body(c, cr, k_ref=k_ref, vt_ref=vt_ref, tk=tk):
                k0 = pl.multiple_of(c * tk, tk)
                return step(k_ref[pl.ds(k0, tk), :], vt_ref[:, pl.ds(k0, tk)], cr)
            carry = lax.fori_loop(0, nk // tk, body, carry, unroll=unroll)
    for j in range(n_chains):
        _, l, acc = carry[j]
        o_ref[j * tq:(j + 1) * tq, :] = (acc / l).T


def _attention_t(q_arr, sources, n_heads, dq, dv, tq=1024, tk_max=1024, unroll=1, n_chains=1):
    b, _, nq = q_arr.shape
    if nq < tq * n_chains:
        tq, n_chains = nq, 1
    tq = tq * n_chains
    in_specs = [pl.BlockSpec((None, dq, tq), lambda bb, h, i: (bb, h, i))]
    args = [q_arr]
    for k_arr, vt_arr in sources:
        nk = k_arr.shape[1]
        in_specs.append(pl.BlockSpec((None, nk, dq), lambda bb, h, i: (bb, 0, h)))
        in_specs.append(pl.BlockSpec((None, dv, nk), lambda bb, h, i: (bb, h, 0)))
        args += [k_arr, vt_arr]
    return pl.pallas_call(
        functools.partial(_attn_t_kernel, n_src=len(sources), tk_max=tk_max, unroll=unroll, n_chains=n_chains),
        grid=(b, n_heads, nq // tq),
        in_specs=in_specs,
        out_specs=pl.BlockSpec((None, tq, dv), lambda bb, h, i: (bb, i, h)),
        out_shape=jax.ShapeDtypeStruct((b, nq, n_heads * dv), F32),
        compiler_params=_cparams(("parallel", "parallel", "arbitrary"), 56),
        name="attention_t",
    )(*args)


NAT_GROUP = 8
NAT_SLAB = NAT_GROUP + NAT_ROWS


def _nat_kernel(q_ref, k_ref, v_ref, kc_ref, vc_ref, bias_ref, o_ref, *, n_rows):
    i = pl.program_id(2)
    scale = HEAD_DIM ** -0.5
    nt = (((1,), (1,)), ((), ()))
    u0 = jnp.clip(i * NAT_GROUP - NAT_ROWS // 2, 0, n_rows - NAT_SLAB)
    t0 = pl.multiple_of(u0 * GRID_W, GRID_W)
    q = (q_ref[...].astype(F32) * scale).astype(BF16)
    ks = k_ref[pl.ds(t0, NAT_SLAB * GRID_W), :].astype(BF16)
    vs = v_ref[pl.ds(t0, NAT_SLAB * GRID_W), :].astype(BF16)
    s_w = lax.dot_general(q, ks, nt, preferred_element_type=F32) + bias_ref[...]
    s_c = lax.dot_general(q, kc_ref[...].astype(BF16), nt, preferred_element_type=F32)
    m = jnp.maximum(jnp.max(s_w, axis=-1, keepdims=True), jnp.max(s_c, axis=-1, keepdims=True))
    p_w = jnp.exp(s_w - m)
    p_c = jnp.exp(s_c - m)
    l = jnp.sum(p_w, axis=-1, keepdims=True) + jnp.sum(p_c, axis=-1, keepdims=True)
    o = (jnp.dot(p_w.astype(BF16), vs, preferred_element_type=F32)
         + jnp.dot(p_c.astype(BF16), vc_ref[...].astype(BF16), preferred_element_type=F32))
    o_ref[...] = o / l


def _nat_bias_table(rpb):
    h = rpb.shape[0]
    col = jnp.arange(GRID_W)
    c0 = jnp.clip(col - NAT_COLS // 2, 0, GRID_W - NAT_COLS)
    kc = jnp.arange(GRID_W)
    col_ok = (kc[None, :] >= c0[:, None]) & (kc[None, :] < c0[:, None] + NAT_COLS)
    dc = jnp.clip(kc[None, :] - col[:, None] + NAT_COLS - 1, 0, 2 * NAT_COLS - 2)
    g = jnp.arange(NAT_GROUP)
    u = jnp.arange(NAT_SLAB)
    variants = []
    for delta in (0, NAT_ROWS // 2, NAT_ROWS):
        w0 = jnp.clip(g + delta - NAT_ROWS // 2, 0, NAT_SLAB - NAT_ROWS)
        row_ok = (u[None, :] >= w0[:, None]) & (u[None, :] < w0[:, None] + NAT_ROWS)
        dr = jnp.clip(u[None, :] - (g[:, None] + delta) + NAT_ROWS - 1, 0, 2 * NAT_ROWS - 2)
        bias = rpb[:, dr][:, :, :, dc]
        ok = row_ok[None, :, :, None, None] & col_ok[None, None, None]
        bias = jnp.where(ok, bias, NEG_BIG).transpose(0, 1, 3, 2, 4)
        variants.append(bias.reshape(h, NAT_GROUP * GRID_W, NAT_SLAB * GRID_W))
    return jnp.stack(variants, axis=1).astype(F32)


def _nat_latent(p_l, p_c, bias_tbl):
    b, n, _ = p_l.shape
    n_ctx = p_c.shape[1]
    n_rows = n // GRID_W
    assert n_rows >= NAT_SLAB and n_rows % NAT_GROUP == 0
    n_groups = n_rows // NAT_GROUP
    tq = NAT_GROUP * GRID_W
    tk = NAT_SLAB * GRID_W

    def variant(i):
        return jnp.where(i == 0, 0, jnp.where(i == n_groups - 1, 2, 1))

    return pl.pallas_call(
        functools.partial(_nat_kernel, n_rows=n_rows),
        grid=(b, N_HEADS_NAT, n_groups),
        in_specs=[pl.BlockSpec((None, tq, LANE), lambda bb, h, i: (bb, i, COL_NAT_Q + h)),
                  pl.BlockSpec((None, n, LANE), lambda bb, h, i: (bb, 0, COL_NAT_K + h)),
                  pl.BlockSpec((None, n, LANE), lambda bb, h, i: (bb, 0, COL_NAT_V + h)),
                  pl.BlockSpec((None, n_ctx, LANE), lambda bb, h, i: (bb, 0, COL_NAT_K + h)),
                  pl.BlockSpec((None, n_ctx, LANE), lambda bb, h, i: (bb, 0, COL_NAT_V + h)),
                  pl.BlockSpec((None, None, tq, tk), lambda bb, h, i: (h, variant(i), 0, 0))],
        out_specs=pl.BlockSpec((None, tq, LANE), lambda bb, h, i: (bb, i, h)),
        out_shape=jax.ShapeDtypeStruct((b, n, N_HEADS_NAT * HEAD_DIM), F32),
        compiler_params=_cparams(("parallel", "parallel", "arbitrary"), 48),
        name="nat_latent",
    )(p_l, p_l, p_l, p_c, p_c, bias_tbl)


def _ret_consts(logit, reverse):
    c_len = RET_CHUNK
    z = jnp.full((c_len, c_len), -logit, F32)
    lg = -(jnp.maximum(z, 0.0) + jnp.log1p(jnp.exp(-jnp.abs(z))))
    ci = lax.broadcasted_iota(jnp.int32, (c_len, c_len), 0).astype(F32)
    si = lax.broadcasted_iota(jnp.int32, (c_len, c_len), 1).astype(F32)
    if reverse:
        rel = si - ci
        wq = jnp.exp(lg * (c_len - ci))
        wk = jnp.exp(lg * ci)
    else:
        rel = ci - si
        wq = jnp.exp(lg * (ci + 1.0))
        wk = jnp.exp(lg * (c_len - 1.0 - ci))
    dmat = jnp.where(rel >= 0, jnp.exp(lg * jnp.maximum(rel, 0.0)), 0.0)
    return wq, wk, dmat, jnp.exp(lg * c_len)


def _ret_chunk(q_ref, k_ref, v_ref, g_ref, y_ref, sl, state, consts):
    wq, wk, dmat, g_chunk = consts
    nt = (((1,), (1,)), ((), ()))
    qc = q_ref[sl, :].astype(F32)
    kc = k_ref[sl, :].astype(F32) * (HEAD_DIM ** -0.5)
    vb = v_ref[sl, :].astype(BF16)
    sc = lax.dot_general(qc.astype(BF16), kc.astype(BF16), nt, preferred_element_type=F32) * dmat
    o = (jnp.dot(sc.astype(BF16), vb, preferred_element_type=F32)
         + jnp.dot((qc * wq).astype(BF16), state.astype(BF16), preferred_element_type=F32))
    state = g_chunk * state + jnp.dot((kc * wk).T.astype(BF16), vb, preferred_element_type=F32)
    mu = jnp.mean(o, axis=-1, keepdims=True)
    d = o - mu
    gn = d * lax.rsqrt(jnp.mean(d * d, axis=-1, keepdims=True) + EPS)
    y_ref[sl, :] = _silu(g_ref[sl, :].astype(F32)) * gn
    return state


def _ret_kernel(lf_ref, lb_ref, qf_ref, kf_ref, vf_ref, gf_ref, qb_ref, kb_ref, vb_ref, gb_ref, s0f_ref, s0b_ref,
                yf_ref, yb_ref, sf_ref, sb_ref, c_scr, *, n_chunks):
    h = pl.program_id(1)

    @pl.when(pl.program_id(2) == 0)
    def _():
        sf_ref[...] = s0f_ref[...]
        sb_ref[...] = s0b_ref[...]
        for j, arr in enumerate(_ret_consts(lf_ref[h], False) + _ret_consts(lb_ref[h], True)):
            c_scr[j] = arr

    cf = tuple(c_scr[j] for j in range(4))
    cb = tuple(c_scr[4 + j] for j in range(4))
    st_f = sf_ref[...]
    st_b = sb_ref[...]
    c_len = RET_CHUNK
    for c in range(n_chunks):
        cr = n_chunks - 1 - c
        st_f = _ret_chunk(qf_ref, kf_ref, vf_ref, gf_ref, yf_ref, slice(c * c_len, (c + 1) * c_len), st_f, cf)
        st_b = _ret_chunk(qb_ref, kb_ref, vb_ref, gb_ref, yb_ref, slice(cr * c_len, (cr + 1) * c_len), st_b, cb)
    sf_ref[...] = st_f
    sb_ref[...] = st_b


def _retention_scan(p, logit_f, logit_b, s0_f, s0_b):
    b, n, _ = p.shape
    hh = N_HEADS_RET
    tb = min(n, 4 * RET_CHUNK)
    nblk = n // tb

    def fwd(c0):
        return pl.BlockSpec((None, tb, LANE), lambda bb, h, i: (bb, i, c0 + h))

    def bwd(c0):
        return pl.BlockSpec((None, tb, LANE), lambda bb, h, i: (bb, nblk - 1 - i, c0 + h))

    state = pl.BlockSpec((None, None, HEAD_DIM, HEAD_DIM), lambda bb, h, i: (bb, h, 0, 0))
    smem = pl.BlockSpec(memory_space=pltpu.SMEM)
    y_shape = jax.ShapeDtypeStruct((b, n, hh * HEAD_DIM), F32)
    s_shape = jax.ShapeDtypeStruct((b, hh, HEAD_DIM, HEAD_DIM), F32)
    return pl.pallas_call(
        functools.partial(_ret_kernel, n_chunks=tb // RET_CHUNK),
        grid=(b, hh, nblk),
        in_specs=[smem, smem,
                  fwd(COL_RET_Q), fwd(COL_RET_K), fwd(COL_RET_V), fwd(COL_RET_GF),
                  bwd(COL_RET_Q), bwd(COL_RET_K), bwd(COL_RET_V), bwd(COL_RET_GB),
                  state, state],
        out_specs=[fwd(0), bwd(0), state, state],
        out_shape=[y_shape, y_shape, s_shape, s_shape],
        scratch_shapes=[pltpu.VMEM((8, RET_CHUNK, RET_CHUNK), F32)],
        compiler_params=_cparams(("parallel", "parallel", "arbitrary"), 32),
        name="retention",
    )(logit_f.astype(F32), logit_b.astype(F32), p, p, p, p, p, p, p, p, s0_f, s0_b)


def _retention(p_c, p_l, logit_f, logit_b):
    b = p_c.shape[0]
    s0 = jnp.zeros((b, N_HEADS_RET, HEAD_DIM, HEAD_DIM), F32)
    yc_f, yc_b, s_cf, s_cb = _retention_scan(p_c, logit_f, logit_b, s0, s0)
    yl_f, yl_b, _, _ = _retention_scan(p_l, logit_f, logit_b, s_cf, s_cb)
    return (yc_f, yc_b), (yl_f, yl_b)


def _merge_kernel(x_ref, ya_ref, yb_ref, yrf_ref, yrb_ref, g_ref, gate_ref, w_ref, o_ref):
    wa = ya_ref.shape[-1]
    wb = yb_ref.shape[-1]
    na = (_rms(ya_ref[...]) * g_ref[:, :wa]).astype(BF16)
    nb = (_rms(yb_ref[...]) * g_ref[:, wa:wa + wb]).astype(BF16)
    nr = (_rms(yrf_ref[...] + yrb_ref[...]) * g_ref[:, wa + wb:]).astype(BF16)
    acc = jnp.dot(na, w_ref[:wa, :], preferred_element_type=F32)
    acc = acc + jnp.dot(nb, w_ref[wa:wa + wb, :], preferred_element_type=F32)
    acc = acc + jnp.dot(nr, w_ref[wa + wb:, :], preferred_element_type=F32)
    o_ref[...] = x_ref[...] + gate_ref[...] * acc


def _merge_out(x, ya, yb, yr, g, gate, w_out):
    b, n, d = x.shape
    yrf, yrb = yr
    tm = min(n, 256)
    dm = w_out.shape[0]

    def tok(width):
        return pl.BlockSpec((None, tm, width), lambda bb, i: (bb, i, 0))

    return pl.pallas_call(
        _merge_kernel,
        grid=(b, n // tm),
        in_specs=[tok(d), tok(ya.shape[-1]), tok(yb.shape[-1]), tok(yrf.shape[-1]), tok(yrb.shape[-1]),
                  pl.BlockSpec((1, dm), lambda bb, i: (0, 0)),
                  pl.BlockSpec((None, 1, d), lambda bb, i: (bb, 0, 0)),
                  pl.BlockSpec((dm, d), lambda bb, i: (0, 0))],
        out_specs=tok(d),
        out_shape=jax.ShapeDtypeStruct((b, n, d), F32),
        compiler_params=_cparams(("parallel", "parallel"), 48),
        name="merge_out",
    )(x, ya, yb, yrf, yrb, g.reshape(1, dm), gate, w_out)


def _moe_pre_kernel(x_ref, g_ref, sh_ref, sc_ref, wr_ref, h_ref, aff_ref, *, n_experts):
    h = (_rms(x_ref[...]) * g_ref[...]) * (1.0 + sc_ref[...]) + sh_ref[...]
    h_ref[...] = h.astype(BF16)
    h_hi = h.astype(BF16)
    h_lo = (h - h_hi.astype(F32)).astype(BF16)
    w = wr_ref[...]
    w_hi = w.astype(BF16)
    w_lo = (w - w_hi.astype(F32)).astype(BF16)
    logits = (jnp.dot(h_hi, w_hi, preferred_element_type=F32)
              + (jnp.dot(h_hi, w_lo, preferred_element_type=F32) + jnp.dot(h_lo, w_hi, preferred_element_type=F32)))
    lane = lax.broadcasted_iota(jnp.int32, logits.shape, 1)
    logits = jnp.where(lane < n_experts, logits, NEG_BIG)
    e = jnp.exp(logits - jnp.max(logits, axis=-1, keepdims=True))
    aff_ref[...] = e / jnp.sum(e, axis=-1, keepdims=True)


def _moe_pre(x, g, shift, scale, w_router_pad):
    b, n, d = x.shape
    tm = min(n, 256)
    return pl.pallas_call(
        functools.partial(_moe_pre_kernel, n_experts=N_EXPERTS),
        grid=(b, n // tm),
        in_specs=[pl.BlockSpec((None, tm, d), lambda bb, i: (bb, i, 0)),
                  pl.BlockSpec((1, d), lambda bb, i: (0, 0)),
                  pl.BlockSpec((None, 1, d), lambda bb, i: (bb, 0, 0)),
                  pl.BlockSpec((None, 1, d), lambda bb, i: (bb, 0, 0)),
                  pl.BlockSpec((d, LANE), lambda bb, i: (0, 0))],
        out_specs=[pl.BlockSpec((None, tm, d), lambda bb, i: (bb, i, 0)),
                   pl.BlockSpec((None, tm, LANE), lambda bb, i: (bb, i, 0))],
        out_shape=[jax.ShapeDtypeStruct((b, n, d), BF16), jax.ShapeDtypeStruct((b, n, LANE), F32)],
        compiler_params=_cparams(("parallel", "parallel"), 40),
        name="moe_pre",
    )(x, g.reshape(1, d), shift, scale, w_router_pad)


def _ffn_kernel(xs_ref, gate_ref, wg_ref, wu_ref, wd_ref, o_ref):
    xs = xs_ref[...]
    a = jnp.dot(xs, wg_ref[...], preferred_element_type=F32)
    u = jnp.dot(xs, wu_ref[...], preferred_element_type=F32)
    hm = (_silu(a) * u).astype(BF16)
    o_ref[...] = (jnp.dot(hm, wd_ref[...], preferred_element_type=F32) * gate_ref[...]).astype(o_ref.dtype)


def _expert_ffn(xs, gates, w_gate, w_up, w_down):
    e, t, d = xs.shape
    f = w_gate.shape[-1]
    tm = min(t, 512)
    return pl.pallas_call(
        _ffn_kernel,
        grid=(e, t // tm),
        in_specs=[pl.BlockSpec((None, tm, d), lambda ee, i: (ee, i, 0)),
                  pl.BlockSpec((None, tm, 1), lambda ee, i: (ee, i, 0)),
                  pl.BlockSpec((None, d, f), lambda ee, i: (ee, 0, 0)),
                  pl.BlockSpec((None, d, f), lambda ee, i: (ee, 0, 0)),
                  pl.BlockSpec((None, f, d), lambda ee, i: (ee, 0, 0))],
        out_specs=pl.BlockSpec((None, tm, d), lambda ee, i: (ee, i, 0)),
        out_shape=jax.ShapeDtypeStruct((e, t, d), BF16),
        compiler_params=_cparams(("parallel", "arbitrary"), 56),
        name="expert_ffn",
    )(xs, gates, w_gate, w_up, w_down)


COMBINE_TOKENS = 256
COMBINE_ROWS = 256


def _combine_kernel(tile_ref, win_ref, flag_ref, x_ref, yp_ref, tok_ref, gate_ref, gn_ref, o_ref, acc_scr, *,
                    final_norm):
    s = pl.program_id(0)
    flags = flag_ref[s]
    t = acc_scr.shape[0]

    @pl.when((flags & 1) != 0)
    def _():
        acc_scr[...] = jnp.zeros(acc_scr.shape, F32)

    @pl.when((flags & 2) != 0)
    def _():
        tcol = tile_ref[s] * t + lax.broadcasted_iota(jnp.int32, (t, 1), 0)
        onehot = jnp.where(tcol == tok_ref[...], 1.0, 0.0).astype(BF16)
        acc_scr[...] += jnp.dot(onehot, yp_ref[...], preferred_element_type=F32)

    @pl.when((flags & 4) != 0)
    def _():
        y = x_ref[...] + gate_ref[...] * acc_scr[...]
        if final_norm:
            y = _rms(y) * gn_ref[...]
        o_ref[...] = y


def _moe_combine(x, y_rows, tok_rows, gate, norm_g, final_norm):
    b, n, d = x.shape
    r = y_rows.shape[0]
    t = min(COMBINE_TOKENS, n)
    w = COMBINE_ROWS
    ntiles = b * n // t
    nwin_total = r // w
    tiles_per_batch = n // t
    n_steps = nwin_total + 2 * ntiles
    i32 = jnp.int32
    tok_sorted, order = lax.sort_key_val(tok_rows.astype(i32), jnp.arange(r, dtype=i32))
    yp = jnp.take(y_rows, order, axis=0)
    bounds = jnp.searchsorted(tok_sorted, jnp.arange(ntiles + 1, dtype=i32) * t, side='left').astype(i32)
    w_lo = jnp.minimum(bounds[:-1] // w, nwin_total - 1)
    w_hi = jnp.maximum((bounds[1:] + w - 1) // w, w_lo + 1)
    nwin = w_hi - w_lo
    step0 = jnp.cumsum(nwin) - nwin
    total = step0[-1] + nwin[-1]
    s = jnp.arange(n_steps, dtype=i32)
    tile_s = jnp.clip(jnp.searchsorted(step0, s, side='right').astype(i32) - 1, 0, ntiles - 1)
    win_s = jnp.minimum(w_lo[tile_s] + (s - step0[tile_s]), w_hi[tile_s] - 1)
    valid = s < total
    first = valid & (s == step0[tile_s])
    last = valid & (s == step0[tile_s] + nwin[tile_s] - 1)
    flags = first.astype(i32) + 2 * valid.astype(i32) + 4 * last.astype(i32)
    out = pl.pallas_call(
        functools.partial(_combine_kernel, final_norm=final_norm),
        grid_spec=pltpu.PrefetchScalarGridSpec(
            num_scalar_prefetch=3,
            grid=(n_steps,),
            in_specs=[pl.BlockSpec((t, d), lambda s, tl, wn, fl: (tl[s], 0)),
                      pl.BlockSpec((w, d), lambda s, tl, wn, fl: (wn[s], 0)),
                      pl.BlockSpec((None, 1, w), lambda s, tl, wn, fl: (wn[s], 0, 0)),
                      pl.BlockSpec((None, 1, d), lambda s, tl, wn, fl: (tl[s] // tiles_per_batch, 0, 0)),
                      pl.BlockSpec((1, d), lambda s, tl, wn, fl: (0, 0))],
            out_specs=pl.BlockSpec((t, d), lambda s, tl, wn, fl: (tl[s], 0)),
            scratch_shapes=[pltpu.VMEM((t, d), F32)]),
        out_shape=jax.ShapeDtypeStruct((b * n, d), F32),
        compiler_params=_cparams(("arbitrary",), 40),
        name="moe_combine",
    )(tile_s, win_s, flags, x.reshape(b * n, d), yp, tok_sorted.reshape(nwin_total, 1, w), gate,
      norm_g.reshape(1, d))
    return out.reshape(b, n, d)


def _ec_moe(x, g, shift, scale, gate2, w_router_pad, w_gate, w_up, w_down, norm_g, final_norm):
    b, n, d = x.shape
    cap = EC_CAPACITY * n // N_EXPERTS
    h, aff = _moe_pre(x, g, shift, scale, w_router_pad)
    gate, idx = lax.top_k(jnp.swapaxes(aff[..., :N_EXPERTS], 1, 2), cap)
    flat = (idx + (jnp.arange(b, dtype=idx.dtype) * n)[:, None, None]).transpose(1, 0, 2).reshape(-1)
    xs = jnp.take(h.reshape(b * n, d), flat, axis=0).reshape(N_EXPERTS, b * cap, d)
    gates = gate.transpose(1, 0, 2).reshape(N_EXPERTS, b * cap, 1)
    y = _expert_ffn(xs, gates, w_gate, w_up, w_down)
    return _moe_combine(x, y.reshape(-1, d), flat, gate2, norm_g, final_norm)


_ROPE_SRC = np.concatenate([np.arange(16, 32), np.arange(0, 16), np.arange(48, 64), np.arange(32, 48)])
_ROPE_SIGN = np.concatenate([-np.ones(16), np.ones(16), -np.ones(16), np.ones(16)]).astype(np.float32)


def _rope_perm_cols(w):
    return w[:, _ROPE_SRC] * _ROPE_SIGN


def _pad_w_in(w_in):
    c_kr = MLA_Q_LORA + MLA_KV_LORA
    kr = w_in[:, c_kr:c_kr + MLA_ROPE_DIM]
    return jnp.concatenate([w_in[:, :c_kr + MLA_ROPE_DIM], _rope_perm_cols(kr), w_in[:, c_kr + MLA_ROPE_DIM:]],
                           axis=1).astype(BF16)


def _expand_w_uq(w_uq):
    w = w_uq.reshape(MLA_Q_LORA, N_HEADS_MLA, HEAD_DIM + MLA_ROPE_DIM)
    nope = w[:, :, :HEAD_DIM]
    rope = w[:, :, HEAD_DIM:]
    partner = rope[:, :, _ROPE_SRC] * _ROPE_SIGN
    return jnp.concatenate([nope, rope, partner], axis=-1).reshape(MLA_Q_LORA, N_HEADS_MLA * MLA_QK_PAD).astype(BF16)


def _rope_tables(n):
    t = jnp.arange(n)
    n_freq = MLA_ROPE_DIM // 4
    inv = ROPE_BASE ** (-jnp.arange(n_freq, dtype=F32) / n_freq)
    ang_r = (t // GRID_W).astype(F32)[:, None] * inv
    ang_c = (t % GRID_W).astype(F32)[:, None] * inv
    zeros = jnp.zeros((n, LANE - MLA_ROPE_DIM), F32)
    cos = jnp.concatenate([jnp.cos(ang_r), jnp.cos(ang_r), jnp.cos(ang_c), jnp.cos(ang_c), zeros], axis=1)
    sin = jnp.concatenate([jnp.sin(ang_r), jnp.sin(ang_r), jnp.sin(ang_c), jnp.sin(ang_c), zeros], axis=1)
    return cos, sin


def _identity_tables(n):
    ones = jnp.ones((n, MLA_ROPE_DIM), F32)
    zeros = jnp.zeros((n, LANE - MLA_ROPE_DIM), F32)
    return jnp.concatenate([ones, zeros], axis=1), jnp.zeros((n, LANE), F32)


def _layer(x_l, x_c, mod, lp, last, final_g):
    b, n, d = x_l.shape
    n_ctx = x_c.shape[1]

    def chunk(k, ctx):
        rows = jnp.broadcast_to(mod[b, k * d:(k + 1) * d], (b, d)) if ctx else mod[:b, k * d:(k + 1) * d]
        return rows.reshape(b, 1, d)

    w_in = _pad_w_in(lp['w_in'])
    p_l = _norm_proj(x_l, lp['norm1_g'], chunk(0, False), chunk(1, False), w_in)
    p_c = _norm_proj(x_c, lp['norm1_g'], chunk(0, True), chunk(1, True), w_in)

    cos_l, sin_l = _rope_tables(n)
    cos_c, sin_c = _identity_tables(n_ctx)
    w_uk = lp['mla_w_uk'].reshape(MLA_KV_LORA, -1).astype(BF16)
    w_uv = lp['mla_w_uv'].reshape(MLA_KV_LORA, -1).astype(BF16)
    w_q = _expand_w_uq(lp['mla_w_uq'])
    k_l, v_l = _mla_kv_prep(p_l, lp['mla_kv_norm_g'], w_uk, w_uv, cos_l, sin_l)
    k_c, v_c = _mla_kv_prep(p_c, lp['mla_kv_norm_g'], w_uk, w_uv, cos_c, sin_c)
    q_l = _mla_q_prep(p_l, lp['mla_q_norm_g'], w_q, cos_l, sin_l)
    a_l = _attention_t(q_l, [(k_c, v_c), (k_l, v_l)], N_HEADS_MLA, MLA_QK_PAD, HEAD_DIM)

    b_l = _nat_latent(p_l, p_c, _nat_bias_table(lp['nat_rpb']))
    r_c, r_l = _retention(p_c, p_l, lp['ret_decay_f'], lp['ret_decay_b'])

    w_out = lp['w_out'].astype(BF16)
    w_router = jnp.zeros((d, LANE), F32).at[:, :N_EXPERTS].set(lp['w_router'])
    w_gate = lp['w_gate'].astype(BF16)
    w_up = lp['w_up'].astype(BF16)
    w_down = lp['w_down'].astype(BF16)

    x_l = _merge_out(x_l, a_l, b_l, r_l, lp['out_norm_g'], chunk(2, False), w_out)
    x_l = _ec_moe(x_l, lp['norm2_g'], chunk(3, False), chunk(4, False), chunk(5, False), w_router, w_gate, w_up,
                  w_down, final_g, last)
    if last:
        return x_l, None

    q_c = _mla_q_prep(p_c, lp['mla_q_norm_g'], w_q, cos_c, sin_c)
    a_c = _attention_t(q_c, [(k_c, v_c)], N_HEADS_MLA, MLA_QK_PAD, HEAD_DIM)
    b_c = _attention(p_c, COL_NAT_Q, HEAD_DIM, [(p_c, COL_NAT_K, p_c, COL_NAT_V)], HEAD_DIM, N_HEADS_NAT,
                     HEAD_DIM ** -0.5)
    x_c = _merge_out(x_c, a_c, b_c, r_c, lp['out_norm_g'], chunk(2, True), w_out)
    x_c = _ec_moe(x_c, lp['norm2_g'], chunk(3, True), chunk(4, True), chunk(5, True), w_router, w_gate, w_up,
                  w_down, final_g, False)
    return x_l, x_c


def kernel(x, c, ctx, c_ctx, w_mod, b_mod, norm1_g, w_in, mla_q_norm_g, mla_kv_norm_g, mla_w_uq, mla_w_uk,
           mla_w_uv, nat_rpb, ret_decay_f, ret_decay_b, out_norm_g, w_out, norm2_g, w_router, w_gate, w_up,
           w_down, final_norm_g):
    depth = w_mod.shape[0]
    cvecs = jnp.concatenate([c, c_ctx[None, :]], axis=0)
    x_l, x_c = x, ctx
    for i in range(depth):
        lp = {
            'norm1_g': norm1_g[i], 'w_in': w_in[i], 'mla_q_norm_g': mla_q_norm_g[i],
            'mla_kv_norm_g': mla_kv_norm_g[i], 'mla_w_uq': mla_w_uq[i], 'mla_w_uk': mla_w_uk[i],
            'mla_w_uv': mla_w_uv[i], 'nat_rpb': nat_rpb[i], 'ret_decay_f': ret_decay_f[i],
            'ret_decay_b': ret_decay_b[i], 'out_norm_g': out_norm_g[i], 'w_out': w_out[i],
            'norm2_g': norm2_g[i], 'w_router': w_router[i], 'w_gate': w_gate[i], 'w_up': w_up[i],
            'w_down': w_down[i],
        }
        mod = _modulation(cvecs, w_mod[i], b_mod[i])
        x_l, x_c = _layer(x_l, x_c, mod, lp, i == depth - 1, final_norm_g)
    return x_l
```

```python
import functools

import numpy as np
import jax
import jax.numpy as jnp
from jax import lax
from jax.experimental import pallas as pl
from jax.experimental.pallas import tpu as pltpu

F32 = jnp.float32
BF16 = jnp.bfloat16

GRID_W = 64
HEAD_DIM = 128
N_HEADS_MLA = 8
N_HEADS_NAT = 4
N_HEADS_RET = 4
MLA_ROPE_DIM = 64
MLA_Q_LORA = 512
MLA_KV_LORA = 256
MLA_SCALE = (HEAD_DIM + MLA_ROPE_DIM) ** -0.5
LOG2E = 1.4426950408889634
MLA_Q_SCALE = MLA_SCALE * LOG2E
MLA_QK_PAD = 256
NAT_ROWS = 8
NAT_COLS = 16
RET_CHUNK = 128
N_EXPERTS = 16
EC_CAPACITY = 2
ROPE_BASE = 10000.0
EPS = 1e-6
NEG_BIG = -1e30
LANE = 128

COL_CQ = 0
COL_CKV = 4
COL_KR = 6
COL_NAT_Q = 7
COL_NAT_K = 11
COL_NAT_V = 15
COL_RET_Q = 19
COL_RET_K = 23
COL_RET_V = 27
COL_RET_GF = 31
COL_RET_GB = 35
IN_COLS_PAD = 39 * LANE


def _cparams(sem, vmem_mib):
    return pltpu.CompilerParams(dimension_semantics=sem, vmem_limit_bytes=vmem_mib * 1024 * 1024)


def _silu(a):
    return a * (1.0 / (1.0 + jnp.exp(-a)))


def _rms(x):
    return x * lax.rsqrt(jnp.mean(x * x, axis=-1, keepdims=True) + EPS)


def _mod_kernel(ct_ref, w_ref, b_ref, o_ref, a_scr, *, n_rows, k_chunk):
    ct = ct_ref[...]
    a_scr[...] = _silu(ct)
    d = w_ref.shape[0]
    tn = w_ref.shape[1]

    def body(kc, accs):
        k0 = pl.multiple_of(kc * k_chunk, k_chunk)
        wblk = w_ref[pl.ds(k0, k_chunk), :]
        out = []
        for r in range(n_rows):
            col = a_scr[pl.ds(k0, k_chunk), r:r + 1]
            out.append(accs[r] + jnp.sum((col * wblk).reshape(k_chunk // 8, 8, tn), axis=0))
        return tuple(out)

    accs = lax.fori_loop(0, d // k_chunk, body, tuple(jnp.zeros((8, tn), F32) for _ in range(n_rows)))
    o_ref[...] = jnp.zeros(o_ref.shape, F32)
    for r in range(n_rows):
        o_ref[r:r + 1, :] = jnp.sum(accs[r], axis=0, keepdims=True) + b_ref[...]


def _modulation(cvecs, w_mod, b_mod):
    n_rows, d = cvecs.shape
    n_out = w_mod.shape[1]
    tn = 1024
    ct = jnp.zeros((d, 8), F32).at[:, :n_rows].set(cvecs.T)
    return pl.pallas_call(
        functools.partial(_mod_kernel, n_rows=n_rows, k_chunk=64),
        grid=(n_out // tn,),
        in_specs=[pl.BlockSpec((d, 8), lambda j: (0, 0)),
                  pl.BlockSpec((d, tn), lambda j: (0, j)),
                  pl.BlockSpec((1, tn), lambda j: (0, j))],
        out_specs=pl.BlockSpec((8, tn), lambda j: (0, j)),
        out_shape=jax.ShapeDtypeStruct((8, n_out), F32),
        scratch_shapes=[pltpu.VMEM((d, 8), F32)],
        compiler_params=_cparams(("parallel",), 40),
        name="modulation",
    )(ct, w_mod, b_mod.reshape(1, n_out))


def _norm_proj_kernel(x_ref, g_ref, sh_ref, sc_ref, w_ref, o_ref, h_scr):
    @pl.when(pl.program_id(2) == 0)
    def _():
        y = _rms(x_ref[...]) * g_ref[...]
        h_scr[...] = (y * (1.0 + sc_ref[...]) + sh_ref[...]).astype(BF16)

    o_ref[...] = jnp.dot(h_scr[...], w_ref[...], preferred_element_type=F32).astype(o_ref.dtype)


def _norm_proj(x, g, shift, scale, w):
    b, n, d = x.shape
    c = w.shape[1]
    tm = min(n, 1024)
    tn = c // 3 if (c % (3 * LANE) == 0) else c
    return pl.pallas_call(
        _norm_proj_kernel,
        grid=(b, n // tm, c // tn),
        in_specs=[pl.BlockSpec((None, tm, d), lambda bb, i, j: (bb, i, 0)),
                  pl.BlockSpec((1, d), lambda bb, i, j: (0, 0)),
                  pl.BlockSpec((None, 1, d), lambda bb, i, j: (bb, 0, 0)),
                  pl.BlockSpec((None, 1, d), lambda bb, i, j: (bb, 0, 0)),
                  pl.BlockSpec((d, tn), lambda bb, i, j: (0, j))],
        out_specs=pl.BlockSpec((None, tm, tn), lambda bb, i, j: (bb, i, j)),
        out_shape=jax.ShapeDtypeStruct((b, n, c), BF16),
        scratch_shapes=[pltpu.VMEM((tm, d), BF16)],
        compiler_params=_cparams(("parallel", "parallel", "arbitrary"), 56),
        name="norm_proj",
    )(x, g.reshape(1, d), shift, scale, w)


def _rope_rotate(r, cos, sin):
    return r * cos + pltpu.roll(r, 64, 1) * sin


def _mla_q_kernel(cq_ref, g_ref, w_ref, cos_ref, sin_ref, o_ref, *, n_heads):
    cqn = (_rms(cq_ref[...].astype(F32)) * g_ref[...]).astype(BF16)
    qe = jnp.dot(cqn, w_ref[...], preferred_element_type=F32)
    cos = cos_ref[...]
    sin = sin_ref[...]
    for h in range(n_heads):
        c0 = h * MLA_QK_PAD
        o_ref[c0:c0 + LANE, :] = (qe[:, c0:c0 + LANE] * MLA_Q_SCALE).T.astype(BF16)
        rot = _rope_rotate(qe[:, c0 + LANE:c0 + 2 * LANE], cos, sin)
        o_ref[c0 + LANE:c0 + 2 * LANE, :] = (rot * MLA_Q_SCALE).T.astype(BF16)


def _mla_q_prep(p, g, w_q, cos, sin):
    b, n, _ = p.shape
    tm = min(n, 512)
    cw = N_HEADS_MLA * MLA_QK_PAD
    return pl.pallas_call(
        functools.partial(_mla_q_kernel, n_heads=N_HEADS_MLA),
        grid=(b, n // tm),
        in_specs=[pl.BlockSpec((None, tm, MLA_Q_LORA), lambda bb, i: (bb, i, COL_CQ * LANE // MLA_Q_LORA)),
                  pl.BlockSpec((1, MLA_Q_LORA), lambda bb, i: (0, 0)),
                  pl.BlockSpec((MLA_Q_LORA, cw), lambda bb, i: (0, 0)),
                  pl.BlockSpec((tm, LANE), lambda bb, i: (i, 0)),
                  pl.BlockSpec((tm, LANE), lambda bb, i: (i, 0))],
        out_specs=pl.BlockSpec((None, cw, tm), lambda bb, i: (bb, 0, i)),
        out_shape=jax.ShapeDtypeStruct((b, cw, n), BF16),
        compiler_params=_cparams(("parallel", "parallel"), 40),
        name="mla_q_prep",
    )(p, g.reshape(1, MLA_Q_LORA), w_q, cos, sin)


def _mla_kv_kernel(ckv_ref, kr_ref, g_ref, wuk_ref, wuv_ref, cos_ref, sin_ref, k_ref, v_ref, *, n_heads):
    ckvn = (_rms(ckv_ref[...].astype(F32)) * g_ref[...]).astype(BF16)
    kn = jnp.dot(ckvn, wuk_ref[...], preferred_element_type=F32)
    v_ref[...] = jnp.dot(ckvn, wuv_ref[...], preferred_element_type=F32).T.astype(BF16)
    rot = _rope_rotate(kr_ref[...].astype(F32), cos_ref[...], sin_ref[...]).astype(BF16)
    for h in range(n_heads):
        c0 = h * MLA_QK_PAD
        k_ref[:, c0:c0 + LANE] = kn[:, h * LANE:(h + 1) * LANE].astype(BF16)
        k_ref[:, c0 + LANE:c0 + 2 * LANE] = rot


def _mla_kv_prep(p, g, w_uk, w_uv, cos, sin):
    b, n, _ = p.shape
    tm = min(n, 512)
    kw = N_HEADS_MLA * MLA_QK_PAD
    vw = N_HEADS_MLA * HEAD_DIM
    return pl.pallas_call(
        functools.partial(_mla_kv_kernel, n_heads=N_HEADS_MLA),
        grid=(b, n // tm),
        in_specs=[pl.BlockSpec((None, tm, MLA_KV_LORA), lambda bb, i: (bb, i, COL_CKV * LANE // MLA_KV_LORA)),
                  pl.BlockSpec((None, tm, LANE), lambda bb, i: (bb, i, COL_KR)),
                  pl.BlockSpec((1, MLA_KV_LORA), lambda bb, i: (0, 0)),
                  pl.BlockSpec((MLA_KV_LORA, vw), lambda bb, i: (0, 0)),
                  pl.BlockSpec((MLA_KV_LORA, vw), lambda bb, i: (0, 0)),
                  pl.BlockSpec((tm, LANE), lambda bb, i: (i, 0)),
                  pl.BlockSpec((tm, LANE), lambda bb, i: (i, 0))],
        out_specs=[pl.BlockSpec((None, tm, kw), lambda bb, i: (bb, i, 0)),
                   pl.BlockSpec((None, vw, tm), lambda bb, i: (bb, 0, i))],
        out_shape=[jax.ShapeDtypeStruct((b, n, kw), BF16), jax.ShapeDtypeStruct((b, vw, n), BF16)],
        compiler_params=_cparams(("parallel", "parallel"), 40),
        name="mla_kv_prep",
    )(p, p, g.reshape(1, MLA_KV_LORA), w_uk, w_uv, cos, sin)


def _attn_kernel(*refs, n_src, scale, tk_max):
    q_ref = refs[0]
    o_ref = refs[1 + 2 * n_src]
    q = q_ref[...]
    if scale != 1.0:
        q = q.astype(F32) * scale
    q = q.astype(BF16)
    tq = q.shape[0]
    dv = o_ref.shape[-1]

    def step(k, v, carry):
        m, l, acc = carry
        s = lax.dot_general(q, k, (((1,), (1,)), ((), ())), preferred_element_type=F32)
        m_new = jnp.maximum(m, jnp.max(s, axis=-1, keepdims=True))
        p = jnp.exp(s - m_new)
        alpha = jnp.exp(m - m_new)
        l = alpha * l + jnp.sum(p, axis=-1, keepdims=True)
        acc = alpha * acc + jnp.dot(p.astype(BF16), v, preferred_element_type=F32)
        return m_new, l, acc

    carry = (jnp.full((tq, 1), NEG_BIG, F32), jnp.zeros((tq, 1), F32), jnp.zeros((tq, dv), F32))
    for s_i in range(n_src):
        k_ref = refs[1 + 2 * s_i]
        v_ref = refs[2 + 2 * s_i]
        nk = k_ref.shape[0]
        tk = min(nk, tk_max)
        if nk == tk:
            carry = step(k_ref[...].astype(BF16), v_ref[...].astype(BF16), carry)
        else:
            def body(c, cr, k_ref=k_ref, v_ref=v_ref, tk=tk):
                k0 = pl.multiple_of(c * tk, tk)
                return step(k_ref[pl.ds(k0, tk), :].astype(BF16), v_ref[pl.ds(k0, tk), :].astype(BF16), cr)
            carry = lax.fori_loop(0, nk // tk, body, carry)
    _, l, acc = carry
    o_ref[...] = acc / l


def _attention(q_arr, q_blk0, dq, sources, dv, n_heads, scale, tq=256, tk_max=512):
    b, nq, _ = q_arr.shape
    tq = min(nq, tq)
    in_specs = [pl.BlockSpec((None, tq, dq), lambda bb, h, i: (bb, i, q_blk0 + h))]
    args = [q_arr]
    for k_arr, k_blk0, v_arr, v_blk0 in sources:
        nk = k_arr.shape[1]
        in_specs.append(pl.BlockSpec((None, nk, dq), lambda bb, h, i, o=k_blk0: (bb, 0, o + h)))
        in_specs.append(pl.BlockSpec((None, nk, dv), lambda bb, h, i, o=v_blk0: (bb, 0, o + h)))
        args += [k_arr, v_arr]
    return pl.pallas_call(
        functools.partial(_attn_kernel, n_src=len(sources), scale=scale, tk_max=tk_max),
        grid=(b, n_heads, nq // tq),
        in_specs=in_specs,
        out_specs=pl.BlockSpec((None, tq, dv), lambda bb, h, i: (bb, i, h)),
        out_shape=jax.ShapeDtypeStruct((b, nq, n_heads * dv), F32),
        compiler_params=_cparams(("parallel", "parallel", "arbitrary"), 48),
        name="attention",
    )(*args)


def _attn_t_kernel(*refs, n_src, tk_max, unroll, n_chains):
    q_ref = refs[0]
    o_ref = refs[1 + 2 * n_src]
    tq = q_ref.shape[1] // n_chains
    dv = o_ref.shape[-1]
    qs = [q_ref[:, j * tq:(j + 1) * tq] for j in range(n_chains)]

    def scores(k):
        return [jnp.dot(k, qs[j], preferred_element_type=F32) for j in range(n_chains)]

    def update(ss, vt, carry):
        out = []
        for j in range(n_chains):
            m, l, acc = carry[j]
            m_new = jnp.maximum(m, jnp.max(ss[j], axis=0, keepdims=True))
            p = jnp.exp2(ss[j] - m_new)
            alpha = jnp.exp2(m - m_new)
            l = alpha * l + jnp.sum(p, axis=0, keepdims=True)
            acc = alpha * acc + jnp.dot(vt, p.astype(BF16), preferred_element_type=F32)
            out.append((m_new, l, acc))
        return tuple(out)

    def step(k, vt, carry):
        return update(scores(k), vt, carry)

    carry = tuple((jnp.full((1, tq), NEG_BIG, F32), jnp.zeros((1, tq), F32), jnp.zeros((dv, tq), F32))
                  for _ in range(n_chains))
    for s_i in range(n_src):
        k_ref = refs[1 + 2 * s_i]
        vt_ref = refs[2 + 2 * s_i]
        nk = k_ref.shape[0]
        tk = min(nk, tk_max)
        if nk == tk:
            carry = step(k_ref[...], vt_ref[...], carry)
        elif (nk // tk) % 2 == 0:
            def body2(c2, cr, k_ref=k_ref, vt_ref=vt_ref, tk=tk):
                k0 = pl.multiple_of(c2 * (2 * tk), 2 * tk)
                k1 = pl.multiple_of(k0 + tk, tk)
                s0 = scores(k_ref[pl.ds(k0, tk), :])
                s1 = scores(k_ref[pl.ds(k1, tk), :])
                cr = update(s0, vt_ref[:, pl.ds(k0, tk)], cr)
                return update(s1, vt_ref[:, pl.ds(k1, tk)], cr)
            carry = lax.fori_loop(0, nk // (2 * tk), body2, carry)
        else:
            def body(c, cr, k_ref=k_ref, vt_ref=vt_ref, tk=tk):
                k0 = pl.multiple_of(c * tk, tk)
                return step(k_ref[pl.ds(k0, tk), :], vt_ref[:, pl.ds(k0, tk)], cr)
            carry = lax.fori_loop(0, nk // tk, body, carry, unroll=unroll)
    for j in range(n_chains):
        _, l, acc = carry[j]
        o_ref[j * tq:(j + 1) * tq, :] = (acc / l).T


def _attention_t(q_arr, sources, n_heads, dq, dv, tq=1024, tk_max=1024, unroll=1, n_chains=1):
    b, _, nq = q_arr.shape
    if nq < tq * n_chains:
        tq, n_chains = nq, 1
    tq = tq * n_chains
    in_specs = [pl.BlockSpec((None, dq, tq), lambda bb, h, i: (bb, h, i))]
    args = [q_arr]
    for k_arr, vt_arr in sources:
        nk = k_arr.shape[1]
        in_specs.append(pl.BlockSpec((None, nk, dq), lambda bb, h, i: (bb, 0, h)))
        in_specs.append(pl.BlockSpec((None, dv, nk), lambda bb, h, i: (bb, h, 0)))
        args += [k_arr, vt_arr]
    return pl.pallas_call(
        functools.partial(_attn_t_kernel, n_src=len(sources), tk_max=tk_max, unroll=unroll, n_chains=n_chains),
        grid=(b, n_heads, nq // tq),
        in_specs=in_specs,
        out_specs=pl.BlockSpec((None, tq, dv), lambda bb, h, i: (bb, i, h)),
        out_shape=jax.ShapeDtypeStruct((b, nq, n_heads * dv), F32),
        compiler_params=_cparams(("parallel", "parallel", "arbitrary"), 56),
        name="attention_t",
    )(*args)


NAT_GROUP = 8
NAT_SLAB = NAT_GROUP + NAT_ROWS
NAT_HALVES = 2


def _nat_kernel(q_ref, k_ref, v_ref, kc_ref, vc_ref, bias_ref, o_ref, *, n_rows):
    i = pl.program_id(2)
    scale = HEAD_DIM ** -0.5
    nt = (((1,), (1,)), ((), ()))
    u0 = jnp.clip(i * NAT_GROUP - NAT_ROWS // 2, 0, n_rows - NAT_SLAB)
    t0 = pl.multiple_of(u0 * GRID_W, GRID_W)
    ks = k_ref[pl.ds(t0, NAT_SLAB * GRID_W), :].astype(BF16)
    vs = v_ref[pl.ds(t0, NAT_SLAB * GRID_W), :].astype(BF16)
    kc = kc_ref[...].astype(BF16)
    vc = vc_ref[...].astype(BF16)
    hq = q_ref.shape[0] // NAT_HALVES
    scores = []
    for j in range(NAT_HALVES):
        rows = slice(j * hq, (j + 1) * hq)
        q = (q_ref[rows, :].astype(F32) * scale).astype(BF16)
        scores.append((lax.dot_general(q, ks, nt, preferred_element_type=F32) + bias_ref[rows, :],
                       lax.dot_general(q, kc, nt, preferred_element_type=F32)))
    for j in range(NAT_HALVES):
        s_w, s_c = scores[j]
        m = jnp.maximum(jnp.max(s_w, axis=-1, keepdims=True), jnp.max(s_c, axis=-1, keepdims=True))
        p_w = jnp.exp(s_w - m)
        p_c = jnp.exp(s_c - m)
        l = jnp.sum(p_w, axis=-1, keepdims=True) + jnp.sum(p_c, axis=-1, keepdims=True)
        o = (jnp.dot(p_w.astype(BF16), vs, preferred_element_type=F32)
             + jnp.dot(p_c.astype(BF16), vc, preferred_element_type=F32))
        o_ref[j * hq:(j + 1) * hq, :] = o / l


def _nat_bias_table(rpb):
    h = rpb.shape[0]
    col = jnp.arange(GRID_W)
    c0 = jnp.clip(col - NAT_COLS // 2, 0, GRID_W - NAT_COLS)
    kc = jnp.arange(GRID_W)
    col_ok = (kc[None, :] >= c0[:, None]) & (kc[None, :] < c0[:, None] + NAT_COLS)
    dc = jnp.clip(kc[None, :] - col[:, None] + NAT_COLS - 1, 0, 2 * NAT_COLS - 2)
    g = jnp.arange(NAT_GROUP)
    u = jnp.arange(NAT_SLAB)
    variants = []
    for delta in (0, NAT_ROWS // 2, NAT_ROWS):
        w0 = jnp.clip(g + delta - NAT_ROWS // 2, 0, NAT_SLAB - NAT_ROWS)
        row_ok = (u[None, :] >= w0[:, None]) & (u[None, :] < w0[:, None] + NAT_ROWS)
        dr = jnp.clip(u[None, :] - (g[:, None] + delta) + NAT_ROWS - 1, 0, 2 * NAT_ROWS - 2)
        bias = rpb[:, dr][:, :, :, dc]
        ok = row_ok[None, :, :, None, None] & col_ok[None, None, None]
        bias = jnp.where(ok, bias, NEG_BIG).transpose(0, 1, 3, 2, 4)
        variants.append(bias.reshape(h, NAT_GROUP * GRID_W, NAT_SLAB * GRID_W))
    return jnp.stack(variants, axis=1).astype(F32)


def _nat_latent(p_l, p_c, bias_tbl):
    b, n, _ = p_l.shape
    n_ctx = p_c.shape[1]
    n_rows = n // GRID_W
    assert n_rows >= NAT_SLAB and n_rows % NAT_GROUP == 0
    n_groups = n_rows // NAT_GROUP
    tq = NAT_GROUP * GRID_W
    tk = NAT_SLAB * GRID_W

    def variant(i):
        return jnp.where(i == 0, 0, jnp.where(i == n_groups - 1, 2, 1))

    return pl.pallas_call(
        functools.partial(_nat_kernel, n_rows=n_rows),
        grid=(b, N_HEADS_NAT, n_groups),
        in_specs=[pl.BlockSpec((None, tq, LANE), lambda bb, h, i: (bb, i, COL_NAT_Q + h)),
                  pl.BlockSpec((None, n, LANE), lambda bb, h, i: (bb, 0, COL_NAT_K + h)),
                  pl.BlockSpec((None, n, LANE), lambda bb, h, i: (bb, 0, COL_NAT_V + h)),
                  pl.BlockSpec((None, n_ctx, LANE), lambda bb, h, i: (bb, 0, COL_NAT_K + h)),
                  pl.BlockSpec((None, n_ctx, LANE), lambda bb, h, i: (bb, 0, COL_NAT_V + h)),
                  pl.BlockSpec((None, None, tq, tk), lambda bb, h, i: (h, variant(i), 0, 0))],
        out_specs=pl.BlockSpec((None, tq, LANE), lambda bb, h, i: (bb, i, h)),
        out_shape=jax.ShapeDtypeStruct((b, n, N_HEADS_NAT * HEAD_DIM), F32),
        compiler_params=_cparams(("parallel", "parallel", "arbitrary"), 48),
        name="nat_latent",
    )(p_l, p_l, p_l, p_c, p_c, bias_tbl)


def _ret_consts(logit, reverse):
    c_len = RET_CHUNK
    z = jnp.full((c_len, c_len), -logit, F32)
    lg = -(jnp.maximum(z, 0.0) + jnp.log1p(jnp.exp(-jnp.abs(z))))
    ci = lax.broadcasted_iota(jnp.int32, (c_len, c_len), 0).astype(F32)
    si = lax.broadcasted_iota(jnp.int32, (c_len, c_len), 1).astype(F32)
    if reverse:
        rel = si - ci
        wq = jnp.exp(lg * (c_len - ci))
        wk = jnp.exp(lg * ci)
    else:
        rel = ci - si
        wq = jnp.exp(lg * (ci + 1.0))
        wk = jnp.exp(lg * (c_len - 1.0 - ci))
    dmat = jnp.where(rel >= 0, jnp.exp(lg * jnp.maximum(rel, 0.0)), 0.0)
    return wq, wk, dmat, jnp.exp(lg * c_len)


def _ret_kernel(lf_ref, lb_ref, qf_ref, kf_ref, vf_ref, gf_ref, qb_ref, kb_ref, vb_ref, gb_ref, s0f_ref, s0b_ref,
                yf_ref, yb_ref, sf_ref, sb_ref, c_scr, *, n_chunks):
    h = pl.program_id(1)

    @pl.when(pl.program_id(2) == 0)
    def _():
        sf_ref[...] = s0f_ref[...]
        sb_ref[...] = s0b_ref[...]
        for j, arr in enumerate(_ret_consts(lf_ref[h], False) + _ret_consts(lb_ref[h], True)):
            c_scr[j] = arr

    c_len = RET_CHUNK
    nt = (((1,), (1,)), ((), ()))
    k_scale = HEAD_DIM ** -0.5
    dirs = []
    for d, refs in enumerate(((qf_ref, kf_ref, vf_ref, gf_ref, yf_ref), (qb_ref, kb_ref, vb_ref, gb_ref, yb_ref))):
        order = list(range(n_chunks)) if d == 0 else list(range(n_chunks - 1, -1, -1))
        dirs.append(refs + tuple(c_scr[4 * d + j] for j in range(4)) + (order,))

    units = []
    for step in range(n_chunks):
        for d in range(2):
            q_ref, k_ref, v_ref, _, _, _, wk, dmat, _, order = dirs[d]
            sl = slice(order[step] * c_len, (order[step] + 1) * c_len)
            qb = q_ref[sl, :].astype(BF16)
            kb = k_ref[sl, :]
            vb = v_ref[sl, :].astype(BF16)
            sc = lax.dot_general(qb, kb.astype(BF16), nt, preferred_element_type=F32) * (dmat * k_scale)
            kv = jnp.dot((kb.astype(F32) * (wk * k_scale)).T.astype(BF16), vb, preferred_element_type=F32)
            units.append((d, sl, qb, vb, sc, kv))
    states = [sf_ref[...], sb_ref[...]]
    cross = []
    for (d, sl, qb, vb, sc, kv) in units:
        wq, g_chunk = dirs[d][5], dirs[d][8]
        cross.append(wq * jnp.dot(qb, states[d].astype(BF16), preferred_element_type=F32))
        states[d] = g_chunk * states[d] + kv
    sf_ref[...] = states[0]
    sb_ref[...] = states[1]
    for (d, sl, qb, vb, sc, kv), o_cross in zip(units, cross):
        g_ref, y_ref = dirs[d][3], dirs[d][4]
        o = jnp.dot(sc.astype(BF16), vb, preferred_element_type=F32) + o_cross
        mu = jnp.mean(o, axis=-1, keepdims=True)
        dev = o - mu
        gn = dev * lax.rsqrt(jnp.mean(dev * dev, axis=-1, keepdims=True) + EPS)
        y_ref[sl, :] = _silu(g_ref[sl, :].astype(F32)) * gn


def _retention_scan(p, logit_f, logit_b, s0_f, s0_b):
    b, n, _ = p.shape
    hh = N_HEADS_RET
    tb = min(n, 8 * RET_CHUNK)
    nblk = n // tb

    def fwd(c0):
        return pl.BlockSpec((None, tb, LANE), lambda bb, h, i: (bb, i, c0 + h))

    def bwd(c0):
        return pl.BlockSpec((None, tb, LANE), lambda bb, h, i: (bb, nblk - 1 - i, c0 + h))

    state = pl.BlockSpec((None, None, HEAD_DIM, HEAD_DIM), lambda bb, h, i: (bb, h, 0, 0))
    smem = pl.BlockSpec(memory_space=pltpu.SMEM)
    y_shape = jax.ShapeDtypeStruct((b, n, hh * HEAD_DIM), F32)
    s_shape = jax.ShapeDtypeStruct((b, hh, HEAD_DIM, HEAD_DIM), F32)
    return pl.pallas_call(
        functools.partial(_ret_kernel, n_chunks=tb // RET_CHUNK),
        grid=(b, hh, nblk),
        in_specs=[smem, smem,
                  fwd(COL_RET_Q), fwd(COL_RET_K), fwd(COL_RET_V), fwd(COL_RET_GF),
                  bwd(COL_RET_Q), bwd(COL_RET_K), bwd(COL_RET_V), bwd(COL_RET_GB),
                  state, state],
        out_specs=[fwd(0), bwd(0), state, state],
        out_shape=[y_shape, y_shape, s_shape, s_shape],
        scratch_shapes=[pltpu.VMEM((8, RET_CHUNK, RET_CHUNK), F32)],
        compiler_params=_cparams(("parallel", "parallel", "arbitrary"), 32),
        name="retention",
    )(logit_f.astype(F32), logit_b.astype(F32), p, p, p, p, p, p, p, p, s0_f, s0_b)


def _retention(p_c, p_l, logit_f, logit_b):
    b = p_c.shape[0]
    s0 = jnp.zeros((b, N_HEADS_RET, HEAD_DIM, HEAD_DIM), F32)
    yc_f, yc_b, s_cf, s_cb = _retention_scan(p_c, logit_f, logit_b, s0, s0)
    yl_f, yl_b, _, _ = _retention_scan(p_l, logit_f, logit_b, s_cf, s_cb)
    return (yc_f, yc_b), (yl_f, yl_b)


def _merge_kernel(x_ref, ya_ref, yb_ref, yrf_ref, yrb_ref, g_ref, gate_ref, w_ref, o_ref):
    wa = ya_ref.shape[-1]
    wb = yb_ref.shape[-1]
    na = (_rms(ya_ref[...]) * g_ref[:, :wa]).astype(BF16)
    nb = (_rms(yb_ref[...]) * g_ref[:, wa:wa + wb]).astype(BF16)
    nr = (_rms(yrf_ref[...] + yrb_ref[...]) * g_ref[:, wa + wb:]).astype(BF16)
    acc = jnp.dot(na, w_ref[:wa, :], preferred_element_type=F32)
    acc = acc + jnp.dot(nb, w_ref[wa:wa + wb, :], preferred_element_type=F32)
    acc = acc + jnp.dot(nr, w_ref[wa + wb:, :], preferred_element_type=F32)
    o_ref[...] = x_ref[...] + gate_ref[...] * acc


def _merge_out(x, ya, yb, yr, g, gate, w_out):
    b, n, d = x.shape
    yrf, yrb = yr
    tm = min(n, 256)
    dm = w_out.shape[0]

    def tok(width):
        return pl.BlockSpec((None, tm, width), lambda bb, i: (bb, i, 0))

    return pl.pallas_call(
        _merge_kernel,
        grid=(b, n // tm),
        in_specs=[tok(d), tok(ya.shape[-1]), tok(yb.shape[-1]), tok(yrf.shape[-1]), tok(yrb.shape[-1]),
                  pl.BlockSpec((1, dm), lambda bb, i: (0, 0)),
                  pl.BlockSpec((None, 1, d), lambda bb, i: (bb, 0, 0)),
                  pl.BlockSpec((dm, d), lambda bb, i: (0, 0))],
        out_specs=tok(d),
        out_shape=jax.ShapeDtypeStruct((b, n, d), F32),
        compiler_params=_cparams(("parallel", "parallel"), 48),
        name="merge_out",
    )(x, ya, yb, yrf, yrb, g.reshape(1, dm), gate, w_out)


def _moe_pre_kernel(x_ref, g_ref, sh_ref, sc_ref, wr_ref, h_ref, aff_ref, *, n_experts):
    h = (_rms(x_ref[...]) * g_ref[...]) * (1.0 + sc_ref[...]) + sh_ref[...]
    h_ref[...] = h.astype(BF16)
    h_hi = h.astype(BF16)
    h_lo = (h - h_hi.astype(F32)).astype(BF16)
    w = wr_ref[...]
    w_hi = w.astype(BF16)
    w_lo = (w - w_hi.astype(F32)).astype(BF16)
    logits = (jnp.dot(h_hi, w_hi, preferred_element_type=F32)
              + (jnp.dot(h_hi, w_lo, preferred_element_type=F32) + jnp.dot(h_lo, w_hi, preferred_element_type=F32)))
    lane = lax.broadcasted_iota(jnp.int32, logits.shape, 1)
    logits = jnp.where(lane < n_experts, logits, NEG_BIG)
    e = jnp.exp(logits - jnp.max(logits, axis=-1, keepdims=True))
    aff_ref[...] = e / jnp.sum(e, axis=-1, keepdims=True)


def _moe_pre(x, g, shift, scale, w_router_pad):
    b, n, d = x.shape
    tm = min(n, 256)
    return pl.pallas_call(
        functools.partial(_moe_pre_kernel, n_experts=N_EXPERTS),
        grid=(b, n // tm),
        in_specs=[pl.BlockSpec((None, tm, d), lambda bb, i: (bb, i, 0)),
                  pl.BlockSpec((1, d), lambda bb, i: (0, 0)),
                  pl.BlockSpec((None, 1, d), lambda bb, i: (bb, 0, 0)),
                  pl.BlockSpec((None, 1, d), lambda bb, i: (bb, 0, 0)),
                  pl.BlockSpec((d, LANE), lambda bb, i: (0, 0))],
        out_specs=[pl.BlockSpec((None, tm, d), lambda bb, i: (bb, i, 0)),
                   pl.BlockSpec((None, tm, LANE), lambda bb, i: (bb, i, 0))],
        out_shape=[jax.ShapeDtypeStruct((b, n, d), BF16), jax.ShapeDtypeStruct((b, n, LANE), F32)],
        compiler_params=_cparams(("parallel", "parallel"), 40),
        name="moe_pre",
    )(x, g.reshape(1, d), shift, scale, w_router_pad)


def _ffn_kernel(xs_ref, gate_ref, wg_ref, wu_ref, wd_ref, o_ref):
    xs = xs_ref[...]
    a = jnp.dot(xs, wg_ref[...], preferred_element_type=F32)
    u = jnp.dot(xs, wu_ref[...], preferred_element_type=F32)
    hm = (_silu(a) * u).astype(BF16)
    o_ref[...] = (jnp.dot(hm, wd_ref[...], preferred_element_type=F32) * gate_ref[...]).astype(o_ref.dtype)


def _expert_ffn(xs, gates, w_gate, w_up, w_down):
    e, t, d = xs.shape
    f = w_gate.shape[-1]
    tm = min(t, 512)
    return pl.pallas_call(
        _ffn_kernel,
        grid=(e, t // tm),
        in_specs=[pl.BlockSpec((None, tm, d), lambda ee, i: (ee, i, 0)),
                  pl.BlockSpec((None, tm, 1), lambda ee, i: (ee, i, 0)),
                  pl.BlockSpec((None, d, f), lambda ee, i: (ee, 0, 0)),
                  pl.BlockSpec((None, d, f), lambda ee, i: (ee, 0, 0)),
                  pl.BlockSpec((None, f, d), lambda ee, i: (ee, 0, 0))],
        out_specs=pl.BlockSpec((None, tm, d), lambda ee, i: (ee, i, 0)),
        out_shape=jax.ShapeDtypeStruct((e, t, d), BF16),
        compiler_params=_cparams(("parallel", "arbitrary"), 56),
        name="expert_ffn",
    )(xs, gates, w_gate, w_up, w_down)


COMBINE_TOKENS = 256
COMBINE_ROWS = 256


def _combine_kernel(tile_ref, win_ref, flag_ref, x_ref, yp_ref, tok_ref, gate_ref, gn_ref, o_ref, acc_scr, *,
                    final_norm):
    s = pl.program_id(0)
    flags = flag_ref[s]
    t = acc_scr.shape[0]

    @pl.when((flags & 1) != 0)
    def _():
        acc_scr[...] = jnp.zeros(acc_scr.shape, F32)

    @pl.when((flags & 2) != 0)
    def _():
        tcol = tile_ref[s] * t + lax.broadcasted_iota(jnp.int32, (t, 1), 0)
        onehot = jnp.where(tcol == tok_ref[...], 1.0, 0.0).astype(BF16)
        acc_scr[...] += jnp.dot(onehot, yp_ref[...], preferred_element_type=F32)

    @pl.when((flags & 4) != 0)
    def _():
        y = x_ref[...] + gate_ref[...] * acc_scr[...]
        if final_norm:
            y = _rms(y) * gn_ref[...]
        o_ref[...] = y


def _moe_combine(x, y_rows, tok_rows, gate, norm_g, final_norm):
    b, n, d = x.shape
    r = y_rows.shape[0]
    t = min(COMBINE_TOKENS, n)
    w = COMBINE_ROWS
    ntiles = b * n // t
    nwin_total = r // w
    tiles_per_batch = n // t
    n_steps = nwin_total + 2 * ntiles
    i32 = jnp.int32
    tok_sorted, order = lax.sort_key_val(tok_rows.astype(i32), jnp.arange(r, dtype=i32))
    yp = jnp.take(y_rows, order, axis=0, mode="clip")
    edges = jnp.arange(ntiles + 1, dtype=i32) * t
    bounds = jnp.sum((tok_sorted[None, :] < edges[:, None]).astype(i32), axis=1)
    w_lo = jnp.minimum(bounds[:-1] // w, nwin_total - 1)
    w_hi = jnp.maximum((bounds[1:] + w - 1) // w, w_lo + 1)
    nwin = w_hi - w_lo
    step0 = jnp.cumsum(nwin) - nwin
    total = jnp.sum(nwin)
    s = jnp.arange(n_steps, dtype=i32)
    tile_s = jnp.sum((s[:, None] >= step0[None, :]).astype(i32), axis=1) - 1
    sel = tile_s[:, None] == jnp.arange(ntiles, dtype=i32)[None, :]

    def of_tile(v):
        return jnp.sum(jnp.where(sel, v[None, :], 0), axis=1)

    step0_s, w_lo_s, nwin_s = of_tile(step0), of_tile(w_lo), of_tile(nwin)
    win_s = jnp.minimum(w_lo_s + (s - step0_s), w_lo_s + nwin_s - 1)
    valid = s < total
    first = valid & (s == step0_s)
    last = valid & (s == step0_s + nwin_s - 1)
    flags = first.astype(i32) + 2 * valid.astype(i32) + 4 * last.astype(i32)
    out = pl.pallas_call(
        functools.partial(_combine_kernel, final_norm=final_norm),
        grid_spec=pltpu.PrefetchScalarGridSpec(
            num_scalar_prefetch=3,
            grid=(n_steps,),
            in_specs=[pl.BlockSpec((t, d), lambda s, tl, wn, fl: (tl[s], 0)),
                      pl.BlockSpec((w, d), lambda s, tl, wn, fl: (wn[s], 0)),
                      pl.BlockSpec((None, 1, w), lambda s, tl, wn, fl: (wn[s], 0, 0)),
                      pl.BlockSpec((None, 1, d), lambda s, tl, wn, fl: (tl[s] // tiles_per_batch, 0, 0)),
                      pl.BlockSpec((1, d), lambda s, tl, wn, fl: (0, 0))],
            out_specs=pl.BlockSpec((t, d), lambda s, tl, wn, fl: (tl[s], 0)),
            scratch_shapes=[pltpu.VMEM((t, d), F32)]),
        out_shape=jax.ShapeDtypeStruct((b * n, d), F32),
        compiler_params=_cparams(("arbitrary",), 40),
        name="moe_combine",
    )(tile_s, win_s, flags, x.reshape(b * n, d), yp, tok_sorted.reshape(nwin_total, 1, w), gate,
      norm_g.reshape(1, d))
    return out.reshape(b, n, d)


def _ec_moe(x, g, shift, scale, gate2, w_router_pad, w_gate, w_up, w_down, norm_g, final_norm):
    b, n, d = x.shape
    cap = EC_CAPACITY * n // N_EXPERTS
    h, aff = _moe_pre(x, g, shift, scale, w_router_pad)
    gate, idx = lax.top_k(jnp.swapaxes(aff[..., :N_EXPERTS], 1, 2), cap)
    flat = (idx + (jnp.arange(b, dtype=idx.dtype) * n)[:, None, None]).transpose(1, 0, 2).reshape(-1)
    xs = jnp.take(h.reshape(b * n, d), flat, axis=0, mode="clip").reshape(N_EXPERTS, b * cap, d)
    gates = gate.transpose(1, 0, 2).reshape(N_EXPERTS, b * cap, 1)
    y = _expert_ffn(xs, gates, w_gate, w_up, w_down)
    return _moe_combine(x, y.reshape(-1, d), flat, gate2, norm_g, final_norm)


_ROPE_SRC = np.concatenate([np.arange(16, 32), np.arange(0, 16), np.arange(48, 64), np.arange(32, 48)])
_ROPE_SIGN = np.concatenate([-np.ones(16), np.ones(16), -np.ones(16), np.ones(16)]).astype(np.float32)


def _rope_perm_cols(w):
    return w[:, _ROPE_SRC] * _ROPE_SIGN


def _pad_w_in(w_in):
    c_kr = MLA_Q_LORA + MLA_KV_LORA
    kr = w_in[:, c_kr:c_kr + MLA_ROPE_DIM]
    return jnp.concatenate([w_in[:, :c_kr + MLA_ROPE_DIM], _rope_perm_cols(kr), w_in[:, c_kr + MLA_ROPE_DIM:]],
                           axis=1).astype(BF16)


def _expand_w_uq(w_uq):
    w = w_uq.reshape(MLA_Q_LORA, N_HEADS_MLA, HEAD_DIM + MLA_ROPE_DIM)
    nope = w[:, :, :HEAD_DIM]
    rope = w[:, :, HEAD_DIM:]
    partner = rope[:, :, _ROPE_SRC] * _ROPE_SIGN
    return jnp.concatenate([nope, rope, partner], axis=-1).reshape(MLA_Q_LORA, N_HEADS_MLA * MLA_QK_PAD).astype(BF16)


def _rope_tables(n):
    t = jnp.arange(n)
    n_freq = MLA_ROPE_DIM // 4
    inv = ROPE_BASE ** (-jnp.arange(n_freq, dtype=F32) / n_freq)
    ang_r = (t // GRID_W).astype(F32)[:, None] * inv
    ang_c = (t % GRID_W).astype(F32)[:, None] * inv
    zeros = jnp.zeros((n, LANE - MLA_ROPE_DIM), F32)
    cos = jnp.concatenate([jnp.cos(ang_r), jnp.cos(ang_r), jnp.cos(ang_c), jnp.cos(ang_c), zeros], axis=1)
    sin = jnp.concatenate([jnp.sin(ang_r), jnp.sin(ang_r), jnp.sin(ang_c), jnp.sin(ang_c), zeros], axis=1)
    return cos, sin


def _identity_tables(n):
    ones = jnp.ones((n, MLA_ROPE_DIM), F32)
    zeros = jnp.zeros((n, LANE - MLA_ROPE_DIM), F32)
    return jnp.concatenate([ones, zeros], axis=1), jnp.zeros((n, LANE), F32)


def _layer(x_l, x_c, mod, lp, last, final_g):
    b, n, d = x_l.shape
    n_ctx = x_c.shape[1]

    def chunk(k, ctx):
        rows = jnp.broadcast_to(mod[b, k * d:(k + 1) * d], (b, d)) if ctx else mod[:b, k * d:(k + 1) * d]
        return rows.reshape(b, 1, d)

    w_in = _pad_w_in(lp['w_in'])
    p_l = _norm_proj(x_l, lp['norm1_g'], chunk(0, False), chunk(1, False), w_in)
    p_c = _norm_proj(x_c, lp['norm1_g'], chunk(0, True), chunk(1, True), w_in)

    cos_l, sin_l = _rope_tables(n)
    cos_c, sin_c = _identity_tables(n_ctx)
    w_uk = lp['mla_w_uk'].reshape(MLA_KV_LORA, -1).astype(BF16)
    w_uv = lp['mla_w_uv'].reshape(MLA_KV_LORA, -1).astype(BF16)
    w_q = _expand_w_uq(lp['mla_w_uq'])
    k_l, v_l = _mla_kv_prep(p_l, lp['mla_kv_norm_g'], w_uk, w_uv, cos_l, sin_l)
    k_c, v_c = _mla_kv_prep(p_c, lp['mla_kv_norm_g'], w_uk, w_uv, cos_c, sin_c)
    q_l = _mla_q_prep(p_l, lp['mla_q_norm_g'], w_q, cos_l, sin_l)
    a_l = _attention_t(q_l, [(k_c, v_c), (k_l, v_l)], N_HEADS_MLA, MLA_QK_PAD, HEAD_DIM)

    b_l = _nat_latent(p_l, p_c, _nat_bias_table(lp['nat_rpb']))
    r_c, r_l = _retention(p_c, p_l, lp['ret_decay_f'], lp['ret_decay_b'])

    w_out = lp['w_out'].astype(BF16)
    w_router = jnp.zeros((d, LANE), F32).at[:, :N_EXPERTS].set(lp['w_router'])
    w_gate = lp['w_gate'].astype(BF16)
    w_up = lp['w_up'].astype(BF16)
    w_down = lp['w_down'].astype(BF16)

    x_l = _merge_out(x_l, a_l, b_l, r_l, lp['out_norm_g'], chunk(2, False), w_out)
    x_l = _ec_moe(x_l, lp['norm2_g'], chunk(3, False), chunk(4, False), chunk(5, False), w_router, w_gate, w_up,
                  w_down, final_g, last)
    if last:
        return x_l, None

    q_c = _mla_q_prep(p_c, lp['mla_q_norm_g'], w_q, cos_c, sin_c)
    a_c = _attention_t(q_c, [(k_c, v_c)], N_HEADS_MLA, MLA_QK_PAD, HEAD_DIM)
    b_c = _attention(p_c, COL_NAT_Q, HEAD_DIM, [(p_c, COL_NAT_K, p_c, COL_NAT_V)], HEAD_DIM, N_HEADS_NAT,
                     HEAD_DIM ** -0.5)
    x_c = _merge_out(x_c, a_c, b_c, r_c, lp['out_norm_g'], chunk(2, True), w_out)
    x_c = _ec_moe(x_c, lp['norm2_g'], chunk(3, True), chunk(4, True), chunk(5, True), w_router, w_gate, w_up,
                  w_down, final_g, False)
    return x_l, x_c


def kernel(x, c, ctx, c_ctx, w_mod, b_mod, norm1_g, w_in, mla_q_norm_g, mla_kv_norm_g, mla_w_uq, mla_w_uk,
           mla_w_uv, nat_rpb, ret_decay_f, ret_decay_b, out_norm_g, w_out, norm2_g, w_router, w_gate, w_up,
           w_down, final_norm_g):
    depth = w_mod.shape[0]
    cvecs = jnp.concatenate([c, c_ctx[None, :]], axis=0)
    x_l, x_c = x, ctx
    for i in range(depth):
        lp = {
            'norm1_g': norm1_g[i], 'w_in': w_in[i], 'mla_q_norm_g': mla_q_norm_g[i],
            'mla_kv_norm_g': mla_kv_norm_g[i], 'mla_w_uq': mla_w_uq[i], 'mla_w_uk': mla_w_uk[i],
            'mla_w_uv': mla_w_uv[i], 'nat_rpb': nat_rpb[i], 'ret_decay_f': ret_decay_f[i],
            'ret_decay_b': ret_decay_b[i], 'out_norm_g': out_norm_g[i], 'w_out': w_out[i],
            'norm2_g': norm2_g[i], 'w_router': w_router[i], 'w_gate': w_gate[i], 'w_up': w_up[i],
            'w_down': w_down[i],
        }
        mod = _modulation(cvecs, w_mod[i], b_mod[i])
        x_l, x_c = _layer(x_l, x_c, mod, lp, i == depth - 1, final_norm_g)
    return x_l
```

```python
import functools

import numpy as np
import jax
import jax.numpy as jnp
from jax import lax
from jax.experimental import pallas as pl
from jax.experimental.pallas import tpu as pltpu

F32 = jnp.float32
BF16 = jnp.bfloat16

GRID_W = 64
HEAD_DIM = 128
N_HEADS_MLA = 8
N_HEADS_NAT = 4
N_HEADS_RET = 4
MLA_ROPE_DIM = 64
MLA_Q_LORA = 512
MLA_KV_LORA = 256
MLA_SCALE = (HEAD_DIM + MLA_ROPE_DIM) ** -0.5
LOG2E = 1.4426950408889634
MLA_Q_SCALE = MLA_SCALE * LOG2E
MLA_QK_PAD = 256
NAT_ROWS = 8
NAT_COLS = 16
RET_CHUNK = 128
N_EXPERTS = 16
EC_CAPACITY = 2
ROPE_BASE = 10000.0
EPS = 1e-6
NEG_BIG = -1e30
LANE = 128

COL_CQ = 0
COL_CKV = 4
COL_KR = 6
COL_NAT_Q = 7
COL_NAT_K = 11
COL_NAT_V = 15
COL_RET_Q = 19
COL_RET_K = 23
COL_RET_V = 27
COL_RET_GF = 31
COL_RET_GB = 35
IN_COLS_PAD = 39 * LANE


def _cparams(sem, vmem_mib):
    return pltpu.CompilerParams(dimension_semantics=sem, vmem_limit_bytes=vmem_mib * 1024 * 1024)


def _silu(a):
    return a * (1.0 / (1.0 + jnp.exp(-a)))


def _rms(x):
    return x * lax.rsqrt(jnp.mean(x * x, axis=-1, keepdims=True) + EPS)


def _mod_kernel(ct_ref, w_ref, b_ref, o_ref, a_scr, *, n_rows, k_chunk):
    ct = ct_ref[...]
    a_scr[...] = _silu(ct)
    d = w_ref.shape[0]
    tn = w_ref.shape[1]

    def body(kc, accs):
        k0 = pl.multiple_of(kc * k_chunk, k_chunk)
        wblk = w_ref[pl.ds(k0, k_chunk), :]
        out = []
        for r in range(n_rows):
            col = a_scr[pl.ds(k0, k_chunk), r:r + 1]
            out.append(accs[r] + jnp.sum((col * wblk).reshape(k_chunk // 8, 8, tn), axis=0))
        return tuple(out)

    accs = lax.fori_loop(0, d // k_chunk, body, tuple(jnp.zeros((8, tn), F32) for _ in range(n_rows)))
    o_ref[...] = jnp.zeros(o_ref.shape, F32)
    for r in range(n_rows):
        o_ref[r:r + 1, :] = jnp.sum(accs[r], axis=0, keepdims=True) + b_ref[...]


def _modulation(cvecs, w_mod, b_mod):
    n_rows, d = cvecs.shape
    n_out = w_mod.shape[1]
    tn = 1024
    ct = jnp.zeros((d, 8), F32).at[:, :n_rows].set(cvecs.T)
    return pl.pallas_call(
        functools.partial(_mod_kernel, n_rows=n_rows, k_chunk=64),
        grid=(n_out // tn,),
        in_specs=[pl.BlockSpec((d, 8), lambda j: (0, 0)),
                  pl.BlockSpec((d, tn), lambda j: (0, j)),
                  pl.BlockSpec((1, tn), lambda j: (0, j))],
        out_specs=pl.BlockSpec((8, tn), lambda j: (0, j)),
        out_shape=jax.ShapeDtypeStruct((8, n_out), F32),
        scratch_shapes=[pltpu.VMEM((d, 8), F32)],
        compiler_params=_cparams(("parallel",), 40),
        name="modulation",
    )(ct, w_mod, b_mod.reshape(1, n_out))


def _norm_proj_kernel(x_ref, g_ref, sh_ref, sc_ref, w_ref, o_ref, h_scr):
    @pl.when(pl.program_id(2) == 0)
    def _():
        y = _rms(x_ref[...]) * g_ref[...]
        h_scr[...] = (y * (1.0 + sc_ref[...]) + sh_ref[...]).astype(BF16)

    o_ref[...] = jnp.dot(h_scr[...], w_ref[...], preferred_element_type=F32).astype(o_ref.dtype)


def _norm_proj(x, g, shift, scale, w):
    b, n, d = x.shape
    c = w.shape[1]
    tm = min(n, 1024)
    tn = c // 3 if (c % (3 * LANE) == 0) else c
    return pl.pallas_call(
        _norm_proj_kernel,
        grid=(b, n // tm, c // tn),
        in_specs=[pl.BlockSpec((None, tm, d), lambda bb, i, j: (bb, i, 0)),
                  pl.BlockSpec((1, d), lambda bb, i, j: (0, 0)),
                  pl.BlockSpec((None, 1, d), lambda bb, i, j: (bb, 0, 0)),
                  pl.BlockSpec((None, 1, d), lambda bb, i, j: (bb, 0, 0)),
                  pl.BlockSpec((d, tn), lambda bb, i, j: (0, j))],
        out_specs=pl.BlockSpec((None, tm, tn), lambda bb, i, j: (bb, i, j)),
        out_shape=jax.ShapeDtypeStruct((b, n, c), BF16),
        scratch_shapes=[pltpu.VMEM((tm, d), BF16)],
        compiler_params=_cparams(("parallel", "parallel", "arbitrary"), 56),
        name="norm_proj",
    )(x, g.reshape(1, d), shift, scale, w)


def _rope_rotate(r, cos, sin):
    return r * cos + pltpu.roll(r, 64, 1) * sin


def _mla_q_kernel(cq_ref, g_ref, w_ref, cos_ref, sin_ref, o_ref, *, n_heads):
    cqn = (_rms(cq_ref[...].astype(F32)) * g_ref[...]).astype(BF16)
    qe = jnp.dot(cqn, w_ref[...], preferred_element_type=F32)
    cos = cos_ref[...]
    sin = sin_ref[...]
    for h in range(n_heads):
        c0 = h * MLA_QK_PAD
        o_ref[c0:c0 + LANE, :] = (qe[:, c0:c0 + LANE] * MLA_Q_SCALE).T.astype(BF16)
        rot = _rope_rotate(qe[:, c0 + LANE:c0 + 2 * LANE], cos, sin)
        o_ref[c0 + LANE:c0 + 2 * LANE, :] = (rot * MLA_Q_SCALE).T.astype(BF16)


def _mla_q_prep(p, g, w_q, cos, sin):
    b, n, _ = p.shape
    tm = min(n, 512)
    cw = N_HEADS_MLA * MLA_QK_PAD
    return pl.pallas_call(
        functools.partial(_mla_q_kernel, n_heads=N_HEADS_MLA),
        grid=(b, n // tm),
        in_specs=[pl.BlockSpec((None, tm, MLA_Q_LORA), lambda bb, i: (bb, i, COL_CQ * LANE // MLA_Q_LORA)),
                  pl.BlockSpec((1, MLA_Q_LORA), lambda bb, i: (0, 0)),
                  pl.BlockSpec((MLA_Q_LORA, cw), lambda bb, i: (0, 0)),
                  pl.BlockSpec((tm, LANE), lambda bb, i: (i, 0)),
                  pl.BlockSpec((tm, LANE), lambda bb, i: (i, 0))],
        out_specs=pl.BlockSpec((None, cw, tm), lambda bb, i: (bb, 0, i)),
        out_shape=jax.ShapeDtypeStruct((b, cw, n), BF16),
        compiler_params=_cparams(("parallel", "parallel"), 40),
        name="mla_q_prep",
    )(p, g.reshape(1, MLA_Q_LORA), w_q, cos, sin)


def _mla_kv_kernel(ckv_ref, kr_ref, g_ref, wuk_ref, wuv_ref, cos_ref, sin_ref, k_ref, v_ref, *, n_heads):
    ckvn = (_rms(ckv_ref[...].astype(F32)) * g_ref[...]).astype(BF16)
    kn = jnp.dot(ckvn, wuk_ref[...], preferred_element_type=F32)
    v_ref[...] = jnp.dot(ckvn, wuv_ref[...], preferred_element_type=F32).T.astype(BF16)
    rot = _rope_rotate(kr_ref[...].astype(F32), cos_ref[...], sin_ref[...]).astype(BF16)
    for h in range(n_heads):
        c0 = h * MLA_QK_PAD
        k_ref[:, c0:c0 + LANE] = kn[:, h * LANE:(h + 1) * LANE].astype(BF16)
        k_ref[:, c0 + LANE:c0 + 2 * LANE] = rot


def _mla_kv_prep(p, g, w_uk, w_uv, cos, sin):
    b, n, _ = p.shape
    tm = min(n, 512)
    kw = N_HEADS_MLA * MLA_QK_PAD
    vw = N_HEADS_MLA * HEAD_DIM
    return pl.pallas_call(
        functools.partial(_mla_kv_kernel, n_heads=N_HEADS_MLA),
        grid=(b, n // tm),
        in_specs=[pl.BlockSpec((None, tm, MLA_KV_LORA), lambda bb, i: (bb, i, COL_CKV * LANE // MLA_KV_LORA)),
                  pl.BlockSpec((None, tm, LANE), lambda bb, i: (bb, i, COL_KR)),
                  pl.BlockSpec((1, MLA_KV_LORA), lambda bb, i: (0, 0)),
                  pl.BlockSpec((MLA_KV_LORA, vw), lambda bb, i: (0, 0)),
                  pl.BlockSpec((MLA_KV_LORA, vw), lambda bb, i: (0, 0)),
                  pl.BlockSpec((tm, LANE), lambda bb, i: (i, 0)),
                  pl.BlockSpec((tm, LANE), lambda bb, i: (i, 0))],
        out_specs=[pl.BlockSpec((None, tm, kw), lambda bb, i: (bb, i, 0)),
                   pl.BlockSpec((None, vw, tm), lambda bb, i: (bb, 0, i))],
        out_shape=[jax.ShapeDtypeStruct((b, n, kw), BF16), jax.ShapeDtypeStruct((b, vw, n), BF16)],
        compiler_params=_cparams(("parallel", "parallel"), 40),
        name="mla_kv_prep",
    )(p, p, g.reshape(1, MLA_KV_LORA), w_uk, w_uv, cos, sin)


def _attn_kernel(*refs, n_src, scale, tk_max):
    q_ref = refs[0]
    o_ref = refs[1 + 2 * n_src]
    q = q_ref[...]
    if scale != 1.0:
        q = q.astype(F32) * scale
    q = q.astype(BF16)
    tq = q.shape[0]
    dv = o_ref.shape[-1]

    def step(k, v, carry):
        m, l, acc = carry
        s = lax.dot_general(q, k, (((1,), (1,)), ((), ())), preferred_element_type=F32)
        m_new = jnp.maximum(m, jnp.max(s, axis=-1, keepdims=True))
        p = jnp.exp(s - m_new)
        alpha = jnp.exp(m - m_new)
        l = alpha * l + jnp.sum(p, axis=-1, keepdims=True)
        acc = alpha * acc + jnp.dot(p.astype(BF16), v, preferred_element_type=F32)
        return m_new, l, acc

    carry = (jnp.full((tq, 1), NEG_BIG, F32), jnp.zeros((tq, 1), F32), jnp.zeros((tq, dv), F32))
    for s_i in range(n_src):
        k_ref = refs[1 + 2 * s_i]
        v_ref = refs[2 + 2 * s_i]
        nk = k_ref.shape[0]
        tk = min(nk, tk_max)
        if nk == tk:
            carry = step(k_ref[...].astype(BF16), v_ref[...].astype(BF16), carry)
        else:
            def body(c, cr, k_ref=k_ref, v_ref=v_ref, tk=tk):
                k0 = pl.multiple_of(c * tk, tk)
                return step(k_ref[pl.ds(k0, tk), :].astype(BF16), v_ref[pl.ds(k0, tk), :].astype(BF16), cr)
            carry = lax.fori_loop(0, nk // tk, body, carry)
    _, l, acc = carry
    o_ref[...] = acc / l


def _attention(q_arr, q_blk0, dq, sources, dv, n_heads, scale, tq=256, tk_max=512):
    b, nq, _ = q_arr.shape
    tq = min(nq, tq)
    in_specs = [pl.BlockSpec((None, tq, dq), lambda bb, h, i: (bb, i, q_blk0 + h))]
    args = [q_arr]
    for k_arr, k_blk0, v_arr, v_blk0 in sources:
        nk = k_arr.shape[1]
        in_specs.append(pl.BlockSpec((None, nk, dq), lambda bb, h, i, o=k_blk0: (bb, 0, o + h)))
        in_specs.append(pl.BlockSpec((None, nk, dv), lambda bb, h, i, o=v_blk0: (bb, 0, o + h)))
        args += [k_arr, v_arr]
    return pl.pallas_call(
        functools.partial(_attn_kernel, n_src=len(sources), scale=scale, tk_max=tk_max),
        grid=(b, n_heads, nq // tq),
        in_specs=in_specs,
        out_specs=pl.BlockSpec((None, tq, dv), lambda bb, h, i: (bb, i, h)),
        out_shape=jax.ShapeDtypeStruct((b, nq, n_heads * dv), F32),
        compiler_params=_cparams(("parallel", "parallel", "arbitrary"), 48),
        name="attention",
    )(*args)


ATTN_TQ = 2048
ATTN_TK = 512
ATTN_GROUP = 4


def _attn_t_kernel(*refs, n_src):
    q_ref = refs[0]
    o_ref = refs[1 + 2 * n_src]
    q = q_ref[...]
    tq = q.shape[1]
    dv = o_ref.shape[-1]

    def scores(k):
        return jnp.dot(k, q, preferred_element_type=F32)

    def update(s, vt, carry):
        m, l, acc = carry
        m_new = jnp.maximum(m, jnp.max(s, axis=0, keepdims=True))
        p = jnp.exp2(s - m_new)
        alpha = jnp.exp2(m - m_new)
        l = alpha * l + jnp.sum(p, axis=0, keepdims=True)
        acc = alpha * acc + jnp.dot(vt, p.astype(BF16), preferred_element_type=F32)
        return m_new, l, acc

    carry = (jnp.full((1, tq), NEG_BIG, F32), jnp.zeros((1, tq), F32), jnp.zeros((dv, tq), F32))
    for s_i in range(n_src):
        k_ref = refs[1 + 2 * s_i]
        vt_ref = refs[2 + 2 * s_i]
        nk = k_ref.shape[0]
        tk = min(nk, ATTN_TK)
        n_chunks = nk // tk
        if n_chunks == 1:
            carry = update(scores(k_ref[...]), vt_ref[...], carry)
            continue
        group = max(g for g in (ATTN_GROUP, 2, 1) if n_chunks % g == 0)

        def body(c, cr, k_ref=k_ref, vt_ref=vt_ref, tk=tk, group=group):
            starts = [pl.multiple_of((c * group + g) * tk, tk) for g in range(group)]
            ss = [scores(k_ref[pl.ds(k0, tk), :]) for k0 in starts]
            for s, k0 in zip(ss, starts):
                cr = update(s, vt_ref[:, pl.ds(k0, tk)], cr)
            return cr
        carry = lax.fori_loop(0, n_chunks // group, body, carry)
    _, l, acc = carry
    o_ref[...] = (acc / l).T


def _attention_t(q_arr, sources, n_heads, dq, dv):
    b, _, nq = q_arr.shape
    tq = min(nq, ATTN_TQ)
    in_specs = [pl.BlockSpec((None, dq, tq), lambda bb, h, i: (bb, h, i))]
    args = [q_arr]
    for k_arr, vt_arr in sources:
        nk = k_arr.shape[1]
        in_specs.append(pl.BlockSpec((None, nk, dq), lambda bb, h, i: (bb, 0, h)))
        in_specs.append(pl.BlockSpec((None, dv, nk), lambda bb, h, i: (bb, h, 0)))
        args += [k_arr, vt_arr]
    return pl.pallas_call(
        functools.partial(_attn_t_kernel, n_src=len(sources)),
        grid=(b, n_heads, nq // tq),
        in_specs=in_specs,
        out_specs=pl.BlockSpec((None, tq, dv), lambda bb, h, i: (bb, i, h)),
        out_shape=jax.ShapeDtypeStruct((b, nq, n_heads * dv), F32),
        compiler_params=_cparams(("parallel", "parallel", "arbitrary"), 56),
        name="attention_t",
    )(*args)


NAT_GROUP = 8
NAT_SLAB = NAT_GROUP + NAT_ROWS
NAT_HALVES = 2


def _nat_kernel(q_ref, k_ref, v_ref, kc_ref, vc_ref, bias_ref, o_ref, *, n_rows):
    i = pl.program_id(2)
    scale = HEAD_DIM ** -0.5
    nt = (((1,), (1,)), ((), ()))
    u0 = jnp.clip(i * NAT_GROUP - NAT_ROWS // 2, 0, n_rows - NAT_SLAB)
    t0 = pl.multiple_of(u0 * GRID_W, GRID_W)
    ks = k_ref[pl.ds(t0, NAT_SLAB * GRID_W), :].astype(BF16)
    vs = v_ref[pl.ds(t0, NAT_SLAB * GRID_W), :].astype(BF16)
    kc = kc_ref[...].astype(BF16)
    vc = vc_ref[...].astype(BF16)
    hq = q_ref.shape[0] // NAT_HALVES
    scores = []
    for j in range(NAT_HALVES):
        rows = slice(j * hq, (j + 1) * hq)
        q = (q_ref[rows, :].astype(F32) * scale).astype(BF16)
        scores.append((lax.dot_general(q, ks, nt, preferred_element_type=F32) + bias_ref[rows, :],
                       lax.dot_general(q, kc, nt, preferred_element_type=F32)))
    for j in range(NAT_HALVES):
        s_w, s_c = scores[j]
        m = jnp.maximum(jnp.max(s_w, axis=-1, keepdims=True), jnp.max(s_c, axis=-1, keepdims=True))
        p_w = jnp.exp(s_w - m)
        p_c = jnp.exp(s_c - m)
        l = jnp.sum(p_w, axis=-1, keepdims=True) + jnp.sum(p_c, axis=-1, keepdims=True)
        o = (jnp.dot(p_w.astype(BF16), vs, preferred_element_type=F32)
             + jnp.dot(p_c.astype(BF16), vc, preferred_element_type=F32))
        o_ref[j * hq:(j + 1) * hq, :] = o / l


def _nat_bias_table(rpb):
    h = rpb.shape[0]
    col = np.arange(GRID_W)
    c0 = np.clip(col - NAT_COLS // 2, 0, GRID_W - NAT_COLS)
    kc = np.arange(GRID_W)
    col_ok = (kc[None, :] >= c0[:, None]) & (kc[None, :] < c0[:, None] + NAT_COLS)
    dc = kc[None, :] - col[:, None] + NAT_COLS - 1
    sel_c = ((dc[None] == np.arange(2 * NAT_COLS - 1)[:, None, None]) & col_ok[None]).astype(np.float32)
    sel_c = sel_c.reshape(2 * NAT_COLS - 1, GRID_W * GRID_W)
    g = np.arange(NAT_GROUP)
    u = np.arange(NAT_SLAB)
    variants = []
    for delta in (0, NAT_ROWS // 2, NAT_ROWS):
        w0 = np.clip(g + delta - NAT_ROWS // 2, 0, NAT_SLAB - NAT_ROWS)
        row_ok = (u[None, :] >= w0[:, None]) & (u[None, :] < w0[:, None] + NAT_ROWS)
        dr = u[None, :] - (g[:, None] + delta) + NAT_ROWS - 1
        sel_r = ((dr[..., None] == np.arange(2 * NAT_ROWS - 1)) & row_ok[..., None]).astype(np.float32)
        sel_r = sel_r.reshape(NAT_GROUP * NAT_SLAB, 2 * NAT_ROWS - 1)
        bias = jnp.einsum('ad,hdc,cb->hab', sel_r, rpb.astype(F32), sel_c, precision=lax.Precision.HIGHEST)
        ok = (row_ok.reshape(-1)[:, None] & col_ok.reshape(-1)[None, :])[None]
        bias = jnp.where(ok, bias, NEG_BIG).reshape(h, NAT_GROUP, NAT_SLAB, GRID_W, GRID_W)
        variants.append(bias.transpose(0, 1, 3, 2, 4).reshape(h, NAT_GROUP * GRID_W, NAT_SLAB * GRID_W))
    return jnp.stack(variants, axis=1)


def _nat_latent(p_l, p_c, bias_tbl):
    b, n, _ = p_l.shape
    n_ctx = p_c.shape[1]
    n_rows = n // GRID_W
    assert n_rows >= NAT_SLAB and n_rows % NAT_GROUP == 0
    n_groups = n_rows // NAT_GROUP
    tq = NAT_GROUP * GRID_W
    tk = NAT_SLAB * GRID_W

    def variant(i):
        return jnp.where(i == 0, 0, jnp.where(i == n_groups - 1, 2, 1))

    return pl.pallas_call(
        functools.partial(_nat_kernel, n_rows=n_rows),
        grid=(b, N_HEADS_NAT, n_groups),
        in_specs=[pl.BlockSpec((None, tq, LANE), lambda bb, h, i: (bb, i, COL_NAT_Q + h)),
                  pl.BlockSpec((None, n, LANE), lambda bb, h, i: (bb, 0, COL_NAT_K + h)),
                  pl.BlockSpec((None, n, LANE), lambda bb, h, i: (bb, 0, COL_NAT_V + h)),
                  pl.BlockSpec((None, n_ctx, LANE), lambda bb, h, i: (bb, 0, COL_NAT_K + h)),
                  pl.BlockSpec((None, n_ctx, LANE), lambda bb, h, i: (bb, 0, COL_NAT_V + h)),
                  pl.BlockSpec((None, None, tq, tk), lambda bb, h, i: (h, variant(i), 0, 0))],
        out_specs=pl.BlockSpec((None, tq, LANE), lambda bb, h, i: (bb, i, h)),
        out_shape=jax.ShapeDtypeStruct((b, n, N_HEADS_NAT * HEAD_DIM), F32),
        compiler_params=_cparams(("parallel", "parallel", "arbitrary"), 48),
        name="nat_latent",
    )(p_l, p_l, p_l, p_c, p_c, bias_tbl)


def _ret_consts(logit, reverse):
    c_len = RET_CHUNK
    z = jnp.full((c_len, c_len), -logit, F32)
    lg = -(jnp.maximum(z, 0.0) + jnp.log1p(jnp.exp(-jnp.abs(z))))
    ci = lax.broadcasted_iota(jnp.int32, (c_len, c_len), 0).astype(F32)
    si = lax.broadcasted_iota(jnp.int32, (c_len, c_len), 1).astype(F32)
    if reverse:
        rel = si - ci
        wq = jnp.exp(lg * (c_len - ci))
        wk = jnp.exp(lg * ci)
    else:
        rel = ci - si
        wq = jnp.exp(lg * (ci + 1.0))
        wk = jnp.exp(lg * (c_len - 1.0 - ci))
    dmat = jnp.where(rel >= 0, jnp.exp(lg * jnp.maximum(rel, 0.0)), 0.0)
    return wq, wk, dmat, jnp.exp(lg * c_len)


def _ret_kernel(lf_ref, lb_ref, qf_ref, kf_ref, vf_ref, gf_ref, qb_ref, kb_ref, vb_ref, gb_ref, s0f_ref, s0b_ref,
                yf_ref, yb_ref, sf_ref, sb_ref, c_scr, *, n_chunks):
    h = pl.program_id(1)

    @pl.when(pl.program_id(2) == 0)
    def _():
        sf_ref[...] = s0f_ref[...]
        sb_ref[...] = s0b_ref[...]
        for j, arr in enumerate(_ret_consts(lf_ref[h], False) + _ret_consts(lb_ref[h], True)):
            c_scr[j] = arr

    c_len = RET_CHUNK
    nt = (((1,), (1,)), ((), ()))
    k_scale = HEAD_DIM ** -0.5
    dirs = []
    for d, refs in enumerate(((qf_ref, kf_ref, vf_ref, gf_ref, yf_ref), (qb_ref, kb_ref, vb_ref, gb_ref, yb_ref))):
        order = list(range(n_chunks)) if d == 0 else list(range(n_chunks - 1, -1, -1))
        dirs.append(refs + tuple(c_scr[4 * d + j] for j in range(4)) + (order,))

    units = []
    for step in range(n_chunks):
        for d in range(2):
            q_ref, k_ref, v_ref, _, _, _, wk, dmat, _, order = dirs[d]
            sl = slice(order[step] * c_len, (order[step] + 1) * c_len)
            qb = q_ref[sl, :].astype(BF16)
            kb = k_ref[sl, :]
            vb = v_ref[sl, :].astype(BF16)
            sc = lax.dot_general(qb, kb.astype(BF16), nt, preferred_element_type=F32) * (dmat * k_scale)
            kv = jnp.dot((kb.astype(F32) * (wk * k_scale)).T.astype(BF16), vb, preferred_element_type=F32)
            units.append((d, sl, qb, vb, sc, kv))
    states = [sf_ref[...], sb_ref[...]]
    cross = []
    for (d, sl, qb, vb, sc, kv) in units:
        wq, g_chunk = dirs[d][5], dirs[d][8]
        cross.append(wq * jnp.dot(qb, states[d].astype(BF16), preferred_element_type=F32))
        states[d] = g_chunk * states[d] + kv
    sf_ref[...] = states[0]
    sb_ref[...] = states[1]
    for (d, sl, qb, vb, sc, kv), o_cross in zip(units, cross):
        g_ref, y_ref = dirs[d][3], dirs[d][4]
        o = jnp.dot(sc.astype(BF16), vb, preferred_element_type=F32) + o_cross
        mu = jnp.mean(o, axis=-1, keepdims=True)
        dev = o - mu
        gn = dev * lax.rsqrt(jnp.mean(dev * dev, axis=-1, keepdims=True) + EPS)
        y_ref[sl, :] = _silu(g_ref[sl, :].astype(F32)) * gn


def _retention_scan(p, logit_f, logit_b, s0_f, s0_b):
    b, n, _ = p.shape
    hh = N_HEADS_RET
    tb = min(n, 8 * RET_CHUNK)
    nblk = n // tb

    def fwd(c0):
        return pl.BlockSpec((None, tb, LANE), lambda bb, h, i: (bb, i, c0 + h))

    def bwd(c0):
        return pl.BlockSpec((None, tb, LANE), lambda bb, h, i: (bb, nblk - 1 - i, c0 + h))

    state = pl.BlockSpec((None, None, HEAD_DIM, HEAD_DIM), lambda bb, h, i: (bb, h, 0, 0))
    smem = pl.BlockSpec(memory_space=pltpu.SMEM)
    y_shape = jax.ShapeDtypeStruct((b, n, hh * HEAD_DIM), F32)
    s_shape = jax.ShapeDtypeStruct((b, hh, HEAD_DIM, HEAD_DIM), F32)
    return pl.pallas_call(
        functools.partial(_ret_kernel, n_chunks=tb // RET_CHUNK),
        grid=(b, hh, nblk),
        in_specs=[smem, smem,
                  fwd(COL_RET_Q), fwd(COL_RET_K), fwd(COL_RET_V), fwd(COL_RET_GF),
                  bwd(COL_RET_Q), bwd(COL_RET_K), bwd(COL_RET_V), bwd(COL_RET_GB),
                  state, state],
        out_specs=[fwd(0), bwd(0), state, state],
        out_shape=[y_shape, y_shape, s_shape, s_shape],
        scratch_shapes=[pltpu.VMEM((8, RET_CHUNK, RET_CHUNK), F32)],
        compiler_params=_cparams(("parallel", "parallel", "arbitrary"), 32),
        name="retention",
    )(logit_f.astype(F32), logit_b.astype(F32), p, p, p, p, p, p, p, p, s0_f, s0_b)


def _retention(p_c, p_l, logit_f, logit_b):
    b = p_c.shape[0]
    s0 = jnp.zeros((b, N_HEADS_RET, HEAD_DIM, HEAD_DIM), F32)
    yc_f, yc_b, s_cf, s_cb = _retention_scan(p_c, logit_f, logit_b, s0, s0)
    yl_f, yl_b, _, _ = _retention_scan(p_l, logit_f, logit_b, s_cf, s_cb)
    return (yc_f, yc_b), (yl_f, yl_b)


def _merge_kernel(x_ref, ya_ref, yb_ref, yrf_ref, yrb_ref, g_ref, gate_ref, w_ref, g2_ref, sh2_ref, sc2_ref, wr_ref,
                  o_ref, h_ref, aff_ref):
    wa = ya_ref.shape[-1]
    wb = yb_ref.shape[-1]
    na = (_rms(ya_ref[...]) * g_ref[:, :wa]).astype(BF16)
    nb = (_rms(yb_ref[...]) * g_ref[:, wa:wa + wb]).astype(BF16)
    nr = (_rms(yrf_ref[...] + yrb_ref[...]) * g_ref[:, wa + wb:]).astype(BF16)
    acc = jnp.dot(na, w_ref[:wa, :], preferred_element_type=F32)
    acc = acc + jnp.dot(nb, w_ref[wa:wa + wb, :], preferred_element_type=F32)
    acc = acc + jnp.dot(nr, w_ref[wa + wb:, :], preferred_element_type=F32)
    x_new = x_ref[...] + gate_ref[...] * acc
    o_ref[...] = x_new
    h = (_rms(x_new) * g2_ref[...]) * (1.0 + sc2_ref[...]) + sh2_ref[...]
    h_hi = h.astype(BF16)
    h_ref[...] = h_hi
    h_lo = (h - h_hi.astype(F32)).astype(BF16)
    wr = wr_ref[...]
    w_hi = wr.astype(BF16)
    w_lo = (wr - w_hi.astype(F32)).astype(BF16)
    logits = (jnp.dot(h_hi, w_hi, preferred_element_type=F32)
              + (jnp.dot(h_hi, w_lo, preferred_element_type=F32) + jnp.dot(h_lo, w_hi, preferred_element_type=F32)))
    lane = lax.broadcasted_iota(jnp.int32, logits.shape, 1)
    logits = jnp.where(lane < N_EXPERTS, logits, NEG_BIG)
    e = jnp.exp(logits - jnp.max(logits, axis=-1, keepdims=True))
    aff_ref[...] = e / jnp.sum(e, axis=-1, keepdims=True)


def _merge_out(x, ya, yb, yr, g, gate, w_out, g2, shift2, scale2, w_router_pad):
    b, n, d = x.shape
    yrf, yrb = yr
    tm = min(n, 256)
    dm = w_out.shape[0]

    def tok(width):
        return pl.BlockSpec((None, tm, width), lambda bb, i: (bb, i, 0))

    row = pl.BlockSpec((1, d), lambda bb, i: (0, 0))
    per_batch = pl.BlockSpec((None, 1, d), lambda bb, i: (bb, 0, 0))
    return pl.pallas_call(
        _merge_kernel,
        grid=(b, n // tm),
        in_specs=[tok(d), tok(ya.shape[-1]), tok(yb.shape[-1]), tok(yrf.shape[-1]), tok(yrb.shape[-1]),
                  pl.BlockSpec((1, dm), lambda bb, i: (0, 0)), per_batch,
                  pl.BlockSpec((dm, d), lambda bb, i: (0, 0)),
                  row, per_batch, per_batch,
                  pl.BlockSpec((d, LANE), lambda bb, i: (0, 0))],
        out_specs=[tok(d), tok(d), tok(LANE)],
        out_shape=[jax.ShapeDtypeStruct((b, n, d), F32), jax.ShapeDtypeStruct((b, n, d), BF16),
                   jax.ShapeDtypeStruct((b, n, LANE), F32)],
        compiler_params=_cparams(("parallel", "parallel"), 48),
        name="merge_out",
    )(x, ya, yb, yrf, yrb, g.reshape(1, dm), gate, w_out, g2.reshape(1, d), shift2, scale2, w_router_pad)


def _ffn_kernel(xs_ref, gate_ref, wg_ref, wu_ref, wd_ref, o_ref):
    xs = xs_ref[...]
    a = jnp.dot(xs, wg_ref[...], preferred_element_type=F32)
    u = jnp.dot(xs, wu_ref[...], preferred_element_type=F32)
    hm = (_silu(a) * u).astype(BF16)
    o_ref[...] = (jnp.dot(hm, wd_ref[...], preferred_element_type=F32) * gate_ref[...]).astype(o_ref.dtype)


def _expert_ffn(xs, gates, w_gate, w_up, w_down):
    e, t, d = xs.shape
    f = w_gate.shape[-1]
    tm = min(t, 512)
    return pl.pallas_call(
        _ffn_kernel,
        grid=(e, t // tm),
        in_specs=[pl.BlockSpec((None, tm, d), lambda ee, i: (ee, i, 0)),
                  pl.BlockSpec((None, tm, 1), lambda ee, i: (ee, i, 0)),
                  pl.BlockSpec((None, d, f), lambda ee, i: (ee, 0, 0)),
                  pl.BlockSpec((None, d, f), lambda ee, i: (ee, 0, 0)),
                  pl.BlockSpec((None, f, d), lambda ee, i: (ee, 0, 0))],
        out_specs=pl.BlockSpec((None, tm, d), lambda ee, i: (ee, i, 0)),
        out_shape=jax.ShapeDtypeStruct((e, t, d), BF16),
        compiler_params=_cparams(("parallel", "arbitrary"), 56),
        name="expert_ffn",
    )(xs, gates, w_gate, w_up, w_down)


COMBINE_TOKENS = 256
COMBINE_ROWS = 256


def _combine_kernel(tile_ref, win_ref, flag_ref, x_ref, yp_ref, tok_ref, gate_ref, gn_ref, o_ref, acc_scr, *,
                    final_norm):
    s = pl.program_id(0)
    flags = flag_ref[s]
    t = acc_scr.shape[0]

    @pl.when((flags & 1) != 0)
    def _():
        acc_scr[...] = jnp.zeros(acc_scr.shape, F32)

    @pl.when((flags & 2) != 0)
    def _():
        tcol = tile_ref[s] * t + lax.broadcasted_iota(jnp.int32, (t, 1), 0)
        onehot = jnp.where(tcol == tok_ref[...], 1.0, 0.0).astype(BF16)
        acc_scr[...] += jnp.dot(onehot, yp_ref[...], preferred_element_type=F32)

    @pl.when((flags & 4) != 0)
    def _():
        y = x_ref[...] + gate_ref[...] * acc_scr[...]
        if final_norm:
            y = _rms(y) * gn_ref[...]
        o_ref[...] = y


def _moe_combine(x, y_rows, tok_rows, gate, norm_g, final_norm):
    b, n, d = x.shape
    r = y_rows.shape[0]
    t = min(COMBINE_TOKENS, n)
    w = COMBINE_ROWS
    ntiles = b * n // t
    nwin_total = r // w
    tiles_per_batch = n // t
    n_steps = nwin_total + 2 * ntiles
    i32 = jnp.int32
    tok_sorted, order = lax.sort_key_val(tok_rows.astype(i32), jnp.arange(r, dtype=i32))
    yp = jnp.take(y_rows, order, axis=0, mode="clip")
    edges = jnp.arange(ntiles + 1, dtype=i32) * t
    bounds = jnp.sum((tok_sorted[None, :] < edges[:, None]).astype(i32), axis=1)
    w_lo = jnp.minimum(bounds[:-1] // w, nwin_total - 1)
    w_hi = jnp.maximum((bounds[1:] + w - 1) // w, w_lo + 1)
    nwin = w_hi - w_lo
    step0 = jnp.cumsum(nwin) - nwin
    total = jnp.sum(nwin)
    s = jnp.arange(n_steps, dtype=i32)
    tile_s = jnp.sum((s[:, None] >= step0[None, :]).astype(i32), axis=1) - 1
    sel = tile_s[:, None] == jnp.arange(ntiles, dtype=i32)[None, :]

    def of_tile(v):
        return jnp.sum(jnp.where(sel, v[None, :], 0), axis=1)

    step0_s, w_lo_s, nwin_s = of_tile(step0), of_tile(w_lo), of_tile(nwin)
    win_s = jnp.minimum(w_lo_s + (s - step0_s), w_lo_s + nwin_s - 1)
    valid = s < total
    first = valid & (s == step0_s)
    last = valid & (s == step0_s + nwin_s - 1)
    flags = first.astype(i32) + 2 * valid.astype(i32) + 4 * last.astype(i32)
    out = pl.pallas_call(
        functools.partial(_combine_kernel, final_norm=final_norm),
        grid_spec=pltpu.PrefetchScalarGridSpec(
            num_scalar_prefetch=3,
            grid=(n_steps,),
            in_specs=[pl.BlockSpec((t, d), lambda s, tl, wn, fl: (tl[s], 0)),
                      pl.BlockSpec((w, d), lambda s, tl, wn, fl: (wn[s], 0)),
                      pl.BlockSpec((None, 1, w), lambda s, tl, wn, fl: (wn[s], 0, 0)),
                      pl.BlockSpec((None, 1, d), lambda s, tl, wn, fl: (tl[s] // tiles_per_batch, 0, 0)),
                      pl.BlockSpec((1, d), lambda s, tl, wn, fl: (0, 0))],
            out_specs=pl.BlockSpec((t, d), lambda s, tl, wn, fl: (tl[s], 0)),
            scratch_shapes=[pltpu.VMEM((t, d), F32)]),
        out_shape=jax.ShapeDtypeStruct((b * n, d), F32),
        compiler_params=_cparams(("arbitrary",), 40),
        name="moe_combine",
    )(tile_s, win_s, flags, x.reshape(b * n, d), yp, tok_sorted.reshape(nwin_total, 1, w), gate,
      norm_g.reshape(1, d))
    return out.reshape(b, n, d)


def _ec_moe(x, h, aff, gate2, w_gate, w_up, w_down, norm_g, final_norm):
    b, n, d = x.shape
    cap = EC_CAPACITY * n // N_EXPERTS
    gate, idx = lax.top_k(jnp.swapaxes(aff[..., :N_EXPERTS], 1, 2), cap)
    flat = (idx + (jnp.arange(b, dtype=idx.dtype) * n)[:, None, None]).transpose(1, 0, 2).reshape(-1)
    xs = jnp.take(h.reshape(b * n, d), flat, axis=0, mode="clip").reshape(N_EXPERTS, b * cap, d)
    gates = gate.transpose(1, 0, 2).reshape(N_EXPERTS, b * cap, 1)
    y = _expert_ffn(xs, gates, w_gate, w_up, w_down)
    return _moe_combine(x, y.reshape(-1, d), flat, gate2, norm_g, final_norm)


_ROPE_SRC = np.concatenate([np.arange(16, 32), np.arange(0, 16), np.arange(48, 64), np.arange(32, 48)])
_ROPE_SIGN = np.concatenate([-np.ones(16), np.ones(16), -np.ones(16), np.ones(16)]).astype(np.float32)


def _rope_perm_cols(w):
    return w[:, _ROPE_SRC] * _ROPE_SIGN


def _pad_w_in(w_in):
    c_kr = MLA_Q_LORA + MLA_KV_LORA
    kr = w_in[:, c_kr:c_kr + MLA_ROPE_DIM]
    return jnp.concatenate([w_in[:, :c_kr + MLA_ROPE_DIM], _rope_perm_cols(kr), w_in[:, c_kr + MLA_ROPE_DIM:]],
                           axis=1).astype(BF16)


def _expand_w_uq(w_uq):
    w = w_uq.reshape(MLA_Q_LORA, N_HEADS_MLA, HEAD_DIM + MLA_ROPE_DIM)
    nope = w[:, :, :HEAD_DIM]
    rope = w[:, :, HEAD_DIM:]
    partner = rope[:, :, _ROPE_SRC] * _ROPE_SIGN
    return jnp.concatenate([nope, rope, partner], axis=-1).reshape(MLA_Q_LORA, N_HEADS_MLA * MLA_QK_PAD).astype(BF16)


def _rope_tables(n):
    t = jnp.arange(n)
    n_freq = MLA_ROPE_DIM // 4
    inv = ROPE_BASE ** (-jnp.arange(n_freq, dtype=F32) / n_freq)
    ang_r = (t // GRID_W).astype(F32)[:, None] * inv
    ang_c = (t % GRID_W).astype(F32)[:, None] * inv
    zeros = jnp.zeros((n, LANE - MLA_ROPE_DIM), F32)
    cos = jnp.concatenate([jnp.cos(ang_r), jnp.cos(ang_r), jnp.cos(ang_c), jnp.cos(ang_c), zeros], axis=1)
    sin = jnp.concatenate([jnp.sin(ang_r), jnp.sin(ang_r), jnp.sin(ang_c), jnp.sin(ang_c), zeros], axis=1)
    return cos, sin


def _identity_tables(n):
    ones = jnp.ones((n, MLA_ROPE_DIM), F32)
    zeros = jnp.zeros((n, LANE - MLA_ROPE_DIM), F32)
    return jnp.concatenate([ones, zeros], axis=1), jnp.zeros((n, LANE), F32)


def _layer(x_l, x_c, mod, lp, last, final_g):
    b, n, d = x_l.shape
    n_ctx = x_c.shape[1]

    def chunk(k, ctx):
        rows = jnp.broadcast_to(mod[b, k * d:(k + 1) * d], (b, d)) if ctx else mod[:b, k * d:(k + 1) * d]
        return rows.reshape(b, 1, d)

    w_in = _pad_w_in(lp['w_in'])
    p_l = _norm_proj(x_l, lp['norm1_g'], chunk(0, False), chunk(1, False), w_in)
    p_c = _norm_proj(x_c, lp['norm1_g'], chunk(0, True), chunk(1, True), w_in)

    cos_l, sin_l = _rope_tables(n)
    cos_c, sin_c = _identity_tables(n_ctx)
    w_uk = lp['mla_w_uk'].reshape(MLA_KV_LORA, -1).astype(BF16)
    w_uv = lp['mla_w_uv'].reshape(MLA_KV_LORA, -1).astype(BF16)
    w_q = _expand_w_uq(lp['mla_w_uq'])
    k_l, v_l = _mla_kv_prep(p_l, lp['mla_kv_norm_g'], w_uk, w_uv, cos_l, sin_l)
    k_c, v_c = _mla_kv_prep(p_c, lp['mla_kv_norm_g'], w_uk, w_uv, cos_c, sin_c)
    q_l = _mla_q_prep(p_l, lp['mla_q_norm_g'], w_q, cos_l, sin_l)
    a_l = _attention_t(q_l, [(k_c, v_c), (k_l, v_l)], N_HEADS_MLA, MLA_QK_PAD, HEAD_DIM)

    b_l = _nat_latent(p_l, p_c, _nat_bias_table(lp['nat_rpb']))
    r_c, r_l = _retention(p_c, p_l, lp['ret_decay_f'], lp['ret_decay_b'])

    w_out = lp['w_out'].astype(BF16)
    w_router = jnp.zeros((d, LANE), F32).at[:, :N_EXPERTS].set(lp['w_router'])
    w_gate = lp['w_gate'].astype(BF16)
    w_up = lp['w_up'].astype(BF16)
    w_down = lp['w_down'].astype(BF16)

    x_l, h_l, aff_l = _merge_out(x_l, a_l, b_l, r_l, lp['out_norm_g'], chunk(2, False), w_out,
                                 lp['norm2_g'], chunk(3, False), chunk(4, False), w_router)
    x_l = _ec_moe(x_l, h_l, aff_l, chunk(5, False), w_gate, w_up, w_down, final_g, last)
    if last:
        return x_l, None

    q_c = _mla_q_prep(p_c, lp['mla_q_norm_g'], w_q, cos_c, sin_c)
    a_c = _attention_t(q_c, [(k_c, v_c)], N_HEADS_MLA, MLA_QK_PAD, HEAD_DIM)
    b_c = _attention(p_c, COL_NAT_Q, HEAD_DIM, [(p_c, COL_NAT_K, p_c, COL_NAT_V)], HEAD_DIM, N_HEADS_NAT,
                     HEAD_DIM ** -0.5)
    x_c, h_c, aff_c = _merge_out(x_c, a_c, b_c, r_c, lp['out_norm_g'], chunk(2, True), w_out,
                                 lp['norm2_g'], chunk(3, True), chunk(4, True), w_router)
    x_c = _ec_moe(x_c, h_c, aff_c, chunk(5, True), w_gate, w_up, w_down, final_g, False)
    return x_l, x_c


def kernel(x, c, ctx, c_ctx, w_mod, b_mod, norm1_g, w_in, mla_q_norm_g, mla_kv_norm_g, mla_w_uq, mla_w_uk,
           mla_w_uv, nat_rpb, ret_decay_f, ret_decay_b, out_norm_g, w_out, norm2_g, w_router, w_gate, w_up,
           w_down, final_norm_g):
    depth = w_mod.shape[0]
    cvecs = jnp.concatenate([c, c_ctx[None, :]], axis=0)
    x_l, x_c = x, ctx
    for i in range(depth):
        lp = {
            'norm1_g': norm1_g[i], 'w_in': w_in[i], 'mla_q_norm_g': mla_q_norm_g[i],
            'mla_kv_norm_g': mla_kv_norm_g[i], 'mla_w_uq': mla_w_uq[i], 'mla_w_uk': mla_w_uk[i],
            'mla_w_uv': mla_w_uv[i], 'nat_rpb': nat_rpb[i], 'ret_decay_f': ret_decay_f[i],
            'ret_decay_b': ret_decay_b[i], 'out_norm_g': out_norm_g[i], 'w_out': w_out[i],
            'norm2_g': norm2_g[i], 'w_router': w_router[i], 'w_gate': w_gate[i], 'w_up': w_up[i],
            'w_down': w_down[i],
        }
        mod = _modulation(cvecs, w_mod[i], b_mod[i])
        x_l, x_c = _layer(x_l, x_c, mod, lp, i == depth - 1, final_norm_g)
    return x_l
```

```python
import functools

import numpy as np
import jax
import jax.numpy as jnp
from jax import lax
from jax.experimental import pallas as pl
from jax.experimental.pallas import tpu as pltpu

F32 = jnp.float32
BF16 = jnp.bfloat16

GRID_W = 64
HEAD_DIM = 128
N_HEADS_MLA = 8
N_HEADS_NAT = 4
N_HEADS_RET = 4
MLA_ROPE_DIM = 64
MLA_Q_LORA = 512
MLA_KV_LORA = 256
MLA_SCALE = (HEAD_DIM + MLA_ROPE_DIM) ** -0.5
LOG2E = 1.4426950408889634
MLA_Q_SCALE = MLA_SCALE * LOG2E
MLA_QK_PAD = 256
NAT_ROWS = 8
NAT_COLS = 16
RET_CHUNK = 128
N_EXPERTS = 16
EC_CAPACITY = 2
ROPE_BASE = 10000.0
EPS = 1e-6
NEG_BIG = -1e30
LANE = 128

COL_CQ = 0
COL_CKV = 4
COL_KR = 6
COL_NAT_Q = 7
COL_NAT_K = 11
COL_NAT_V = 15
COL_RET_Q = 19
COL_RET_K = 23
COL_RET_V = 27
COL_RET_GF = 31
COL_RET_GB = 35
IN_COLS_PAD = 39 * LANE


def _cparams(sem, vmem_mib):
    return pltpu.CompilerParams(dimension_semantics=sem, vmem_limit_bytes=vmem_mib * 1024 * 1024)


def _silu(a):
    return a * (1.0 / (1.0 + jnp.exp(-a)))


def _rms(x):
    return x * lax.rsqrt(jnp.mean(x * x, axis=-1, keepdims=True) + EPS)


def _mod_kernel(ct_ref, w_ref, b_ref, o_ref, a_scr, *, n_rows, k_chunk):
    ct = ct_ref[...]
    a_scr[...] = _silu(ct)
    d = w_ref.shape[0]
    tn = w_ref.shape[1]

    def body(kc, accs):
        k0 = pl.multiple_of(kc * k_chunk, k_chunk)
        wblk = w_ref[pl.ds(k0, k_chunk), :]
        out = []
        for r in range(n_rows):
            col = a_scr[pl.ds(k0, k_chunk), r:r + 1]
            out.append(accs[r] + jnp.sum((col * wblk).reshape(k_chunk // 8, 8, tn), axis=0))
        return tuple(out)

    accs = lax.fori_loop(0, d // k_chunk, body, tuple(jnp.zeros((8, tn), F32) for _ in range(n_rows)))
    o_ref[...] = jnp.zeros(o_ref.shape, F32)
    for r in range(n_rows):
        o_ref[r:r + 1, :] = jnp.sum(accs[r], axis=0, keepdims=True) + b_ref[...]


def _modulation(cvecs, w_mod, b_mod):
    n_rows, d = cvecs.shape
    n_out = w_mod.shape[1]
    tn = 1024
    ct = jnp.zeros((d, 8), F32).at[:, :n_rows].set(cvecs.T)
    return pl.pallas_call(
        functools.partial(_mod_kernel, n_rows=n_rows, k_chunk=64),
        grid=(n_out // tn,),
        in_specs=[pl.BlockSpec((d, 8), lambda j: (0, 0)),
                  pl.BlockSpec((d, tn), lambda j: (0, j)),
                  pl.BlockSpec((1, tn), lambda j: (0, j))],
        out_specs=pl.BlockSpec((8, tn), lambda j: (0, j)),
        out_shape=jax.ShapeDtypeStruct((8, n_out), F32),
        scratch_shapes=[pltpu.VMEM((d, 8), F32)],
        compiler_params=_cparams(("parallel",), 40),
        name="modulation",
    )(ct, w_mod, b_mod.reshape(1, n_out))


def _norm_proj_kernel(x_ref, g_ref, sh_ref, sc_ref, w_ref, o_ref, h_scr):
    @pl.when(pl.program_id(2) == 0)
    def _():
        y = _rms(x_ref[...]) * g_ref[...]
        h_scr[...] = (y * (1.0 + sc_ref[...]) + sh_ref[...]).astype(BF16)

    o_ref[...] = jnp.dot(h_scr[...], w_ref[...], preferred_element_type=F32).astype(o_ref.dtype)


def _norm_proj(x, g, shift, scale, w):
    b, n, d = x.shape
    c = w.shape[1]
    tm = min(n, 1024)
    tn = c // 3 if (c % (3 * LANE) == 0) else c
    return pl.pallas_call(
        _norm_proj_kernel,
        grid=(b, n // tm, c // tn),
        in_specs=[pl.BlockSpec((None, tm, d), lambda bb, i, j: (bb, i, 0)),
                  pl.BlockSpec((1, d), lambda bb, i, j: (0, 0)),
                  pl.BlockSpec((None, 1, d), lambda bb, i, j: (bb, 0, 0)),
                  pl.BlockSpec((None, 1, d), lambda bb, i, j: (bb, 0, 0)),
                  pl.BlockSpec((d, tn), lambda bb, i, j: (0, j))],
        out_specs=pl.BlockSpec((None, tm, tn), lambda bb, i, j: (bb, i, j)),
        out_shape=jax.ShapeDtypeStruct((b, n, c), BF16),
        scratch_shapes=[pltpu.VMEM((tm, d), BF16)],
        compiler_params=_cparams(("parallel", "parallel", "arbitrary"), 56),
        name="norm_proj",
    )(x, g.reshape(1, d), shift, scale, w)


def _rope_rotate(r, cos, sin):
    return r * cos + pltpu.roll(r, 64, 1) * sin


def _mla_q_kernel(cq_ref, g_ref, w_ref, cos_ref, sin_ref, o_ref, *, n_heads):
    cqn = (_rms(cq_ref[...].astype(F32)) * g_ref[...]).astype(BF16)
    qe = jnp.dot(cqn, w_ref[...], preferred_element_type=F32)
    cos = cos_ref[...]
    sin = sin_ref[...]
    for h in range(n_heads):
        c0 = h * MLA_QK_PAD
        o_ref[c0:c0 + LANE, :] = (qe[:, c0:c0 + LANE] * MLA_Q_SCALE).T.astype(BF16)
        rot = _rope_rotate(qe[:, c0 + LANE:c0 + 2 * LANE], cos, sin)
        o_ref[c0 + LANE:c0 + 2 * LANE, :] = (rot * MLA_Q_SCALE).T.astype(BF16)


def _mla_q_prep(p, g, w_q, cos, sin):
    b, n, _ = p.shape
    tm = min(n, 512)
    cw = N_HEADS_MLA * MLA_QK_PAD
    return pl.pallas_call(
        functools.partial(_mla_q_kernel, n_heads=N_HEADS_MLA),
        grid=(b, n // tm),
        in_specs=[pl.BlockSpec((None, tm, MLA_Q_LORA), lambda bb, i: (bb, i, COL_CQ * LANE // MLA_Q_LORA)),
                  pl.BlockSpec((1, MLA_Q_LORA), lambda bb, i: (0, 0)),
                  pl.BlockSpec((MLA_Q_LORA, cw), lambda bb, i: (0, 0)),
                  pl.BlockSpec((tm, LANE), lambda bb, i: (i, 0)),
                  pl.BlockSpec((tm, LANE), lambda bb, i: (i, 0))],
        out_specs=pl.BlockSpec((None, cw, tm), lambda bb, i: (bb, 0, i)),
        out_shape=jax.ShapeDtypeStruct((b, cw, n), BF16),
        compiler_params=_cparams(("parallel", "parallel"), 40),
        name="mla_q_prep",
    )(p, g.reshape(1, MLA_Q_LORA), w_q, cos, sin)


def _mla_kv_kernel(ckv_ref, kr_ref, g_ref, wuk_ref, wuv_ref, cos_ref, sin_ref, k_ref, v_ref, *, n_heads):
    ckvn = (_rms(ckv_ref[...].astype(F32)) * g_ref[...]).astype(BF16)
    kn = jnp.dot(ckvn, wuk_ref[...], preferred_element_type=F32)
    v_ref[...] = jnp.dot(ckvn, wuv_ref[...], preferred_element_type=F32).T.astype(BF16)
    rot = _rope_rotate(kr_ref[...].astype(F32), cos_ref[...], sin_ref[...]).astype(BF16)
    for h in range(n_heads):
        c0 = h * MLA_QK_PAD
        k_ref[:, c0:c0 + LANE] = kn[:, h * LANE:(h + 1) * LANE].astype(BF16)
        k_ref[:, c0 + LANE:c0 + 2 * LANE] = rot


def _mla_kv_prep(p, g, w_uk, w_uv, cos, sin):
    b, n, _ = p.shape
    tm = min(n, 512)
    kw = N_HEADS_MLA * MLA_QK_PAD
    vw = N_HEADS_MLA * HEAD_DIM
    return pl.pallas_call(
        functools.partial(_mla_kv_kernel, n_heads=N_HEADS_MLA),
        grid=(b, n // tm),
        in_specs=[pl.BlockSpec((None, tm, MLA_KV_LORA), lambda bb, i: (bb, i, COL_CKV * LANE // MLA_KV_LORA)),
                  pl.BlockSpec((None, tm, LANE), lambda bb, i: (bb, i, COL_KR)),
                  pl.BlockSpec((1, MLA_KV_LORA), lambda bb, i: (0, 0)),
                  pl.BlockSpec((MLA_KV_LORA, vw), lambda bb, i: (0, 0)),
                  pl.BlockSpec((MLA_KV_LORA, vw), lambda bb, i: (0, 0)),
                  pl.BlockSpec((tm, LANE), lambda bb, i: (i, 0)),
                  pl.BlockSpec((tm, LANE), lambda bb, i: (i, 0))],
        out_specs=[pl.BlockSpec((None, tm, kw), lambda bb, i: (bb, i, 0)),
                   pl.BlockSpec((None, vw, tm), lambda bb, i: (bb, 0, i))],
        out_shape=[jax.ShapeDtypeStruct((b, n, kw), BF16), jax.ShapeDtypeStruct((b, vw, n), BF16)],
        compiler_params=_cparams(("parallel", "parallel"), 40),
        name="mla_kv_prep",
    )(p, p, g.reshape(1, MLA_KV_LORA), w_uk, w_uv, cos, sin)


def _attn_kernel(*refs, n_src, scale, tk_max):
    q_ref = refs[0]
    o_ref = refs[1 + 2 * n_src]
    q = q_ref[...]
    if scale != 1.0:
        q = q.astype(F32) * scale
    q = q.astype(BF16)
    tq = q.shape[0]
    dv = o_ref.shape[-1]

    def step(k, v, carry):
        m, l, acc = carry
        s = lax.dot_general(q, k, (((1,), (1,)), ((), ())), preferred_element_type=F32)
        m_new = jnp.maximum(m, jnp.max(s, axis=-1, keepdims=True))
        p = jnp.exp(s - m_new)
        alpha = jnp.exp(m - m_new)
        l = alpha * l + jnp.sum(p, axis=-1, keepdims=True)
        acc = alpha * acc + jnp.dot(p.astype(BF16), v, preferred_element_type=F32)
        return m_new, l, acc

    carry = (jnp.full((tq, 1), NEG_BIG, F32), jnp.zeros((tq, 1), F32), jnp.zeros((tq, dv), F32))
    for s_i in range(n_src):
        k_ref = refs[1 + 2 * s_i]
        v_ref = refs[2 + 2 * s_i]
        nk = k_ref.shape[0]
        tk = min(nk, tk_max)
        if nk == tk:
            carry = step(k_ref[...].astype(BF16), v_ref[...].astype(BF16), carry)
        else:
            def body(c, cr, k_ref=k_ref, v_ref=v_ref, tk=tk):
                k0 = pl.multiple_of(c * tk, tk)
                return step(k_ref[pl.ds(k0, tk), :].astype(BF16), v_ref[pl.ds(k0, tk), :].astype(BF16), cr)
            carry = lax.fori_loop(0, nk // tk, body, carry)
    _, l, acc = carry
    o_ref[...] = acc / l


def _attention(q_arr, q_blk0, dq, sources, dv, n_heads, scale, tq=256, tk_max=512):
    b, nq, _ = q_arr.shape
    tq = min(nq, tq)
    in_specs = [pl.BlockSpec((None, tq, dq), lambda bb, h, i: (bb, i, q_blk0 + h))]
    args = [q_arr]
    for k_arr, k_blk0, v_arr, v_blk0 in sources:
        nk = k_arr.shape[1]
        in_specs.append(pl.BlockSpec((None, nk, dq), lambda bb, h, i, o=k_blk0: (bb, 0, o + h)))
        in_specs.append(pl.BlockSpec((None, nk, dv), lambda bb, h, i, o=v_blk0: (bb, 0, o + h)))
        args += [k_arr, v_arr]
    return pl.pallas_call(
        functools.partial(_attn_kernel, n_src=len(sources), scale=scale, tk_max=tk_max),
        grid=(b, n_heads, nq // tq),
        in_specs=in_specs,
        out_specs=pl.BlockSpec((None, tq, dv), lambda bb, h, i: (bb, i, h)),
        out_shape=jax.ShapeDtypeStruct((b, nq, n_heads * dv), F32),
        compiler_params=_cparams(("parallel", "parallel", "arbitrary"), 48),
        name="attention",
    )(*args)


ATTN_TQ = 2048
ATTN_TK = 512
ATTN_GROUP = 4


def _attn_t_kernel(*refs, n_src):
    q_ref = refs[0]
    o_ref = refs[1 + 2 * n_src]
    q = q_ref[...]
    tq = q.shape[1]
    dv = o_ref.shape[-1]

    def scores(k):
        return jnp.dot(k, q, preferred_element_type=F32)

    def update(s, vt, carry):
        m, l, acc = carry
        m_new = jnp.maximum(m, jnp.max(s, axis=0, keepdims=True))
        p = jnp.exp2(s - m_new)
        alpha = jnp.exp2(m - m_new)
        l = alpha * l + jnp.sum(p, axis=0, keepdims=True)
        acc = alpha * acc + jnp.dot(vt, p.astype(BF16), preferred_element_type=F32)
        return m_new, l, acc

    carry = (jnp.full((1, tq), NEG_BIG, F32), jnp.zeros((1, tq), F32), jnp.zeros((dv, tq), F32))
    for s_i in range(n_src):
        k_ref = refs[1 + 2 * s_i]
        vt_ref = refs[2 + 2 * s_i]
        nk = k_ref.shape[0]
        tk = min(nk, ATTN_TK)
        n_chunks = nk // tk
        if n_chunks == 1:
            carry = update(scores(k_ref[...]), vt_ref[...], carry)
            continue
        group = max(g for g in (ATTN_GROUP, 2, 1) if n_chunks % g == 0)

        def body(c, cr, k_ref=k_ref, vt_ref=vt_ref, tk=tk, group=group):
            starts = [pl.multiple_of((c * group + g) * tk, tk) for g in range(group)]
            ss = [scores(k_ref[pl.ds(k0, tk), :]) for k0 in starts]
            for s, k0 in zip(ss, starts):
                cr = update(s, vt_ref[:, pl.ds(k0, tk)], cr)
            return cr
        carry = lax.fori_loop(0, n_chunks // group, body, carry)
    _, l, acc = carry
    o_ref[...] = (acc / l).T


def _attention_t(q_arr, sources, n_heads, dq, dv):
    b, _, nq = q_arr.shape
    tq = min(nq, ATTN_TQ)
    in_specs = [pl.BlockSpec((None, dq, tq), lambda bb, h, i: (bb, h, i))]
    args = [q_arr]
    for k_arr, vt_arr in sources:
        nk = k_arr.shape[1]
        in_specs.append(pl.BlockSpec((None, nk, dq), lambda bb, h, i: (bb, 0, h)))
        in_specs.append(pl.BlockSpec((None, dv, nk), lambda bb, h, i: (bb, h, 0)))
        args += [k_arr, vt_arr]
    return pl.pallas_call(
        functools.partial(_attn_t_kernel, n_src=len(sources)),
        grid=(b, n_heads, nq // tq),
        in_specs=in_specs,
        out_specs=pl.BlockSpec((None, tq, dv), lambda bb, h, i: (bb, i, h)),
        out_shape=jax.ShapeDtypeStruct((b, nq, n_heads * dv), F32),
        compiler_params=_cparams(("parallel", "parallel", "arbitrary"), 56),
        name="attention_t",
    )(*args)


NAT_GROUP = 8
NAT_SLAB = NAT_GROUP + NAT_ROWS
NAT_HALVES = 2


def _nat_kernel(q_ref, k_ref, v_ref, kc_ref, vc_ref, bias_ref, o_ref, *, n_rows):
    i = pl.program_id(2)
    scale = HEAD_DIM ** -0.5
    nt = (((1,), (1,)), ((), ()))
    u0 = jnp.clip(i * NAT_GROUP - NAT_ROWS // 2, 0, n_rows - NAT_SLAB)
    t0 = pl.multiple_of(u0 * GRID_W, GRID_W)
    ks = k_ref[pl.ds(t0, NAT_SLAB * GRID_W), :].astype(BF16)
    vs = v_ref[pl.ds(t0, NAT_SLAB * GRID_W), :].astype(BF16)
    kc = kc_ref[...].astype(BF16)
    vc = vc_ref[...].astype(BF16)
    hq = q_ref.shape[0] // NAT_HALVES
    scores = []
    for j in range(NAT_HALVES):
        rows = slice(j * hq, (j + 1) * hq)
        q = (q_ref[rows, :].astype(F32) * scale).astype(BF16)
        scores.append((lax.dot_general(q, ks, nt, preferred_element_type=F32) + bias_ref[rows, :],
                       lax.dot_general(q, kc, nt, preferred_element_type=F32)))
    for j in range(NAT_HALVES):
        s_w, s_c = scores[j]
        m = jnp.maximum(jnp.max(s_w, axis=-1, keepdims=True), jnp.max(s_c, axis=-1, keepdims=True))
        p_w = jnp.exp(s_w - m)
        p_c = jnp.exp(s_c - m)
        l = jnp.sum(p_w, axis=-1, keepdims=True) + jnp.sum(p_c, axis=-1, keepdims=True)
        o = (jnp.dot(p_w.astype(BF16), vs, preferred_element_type=F32)
             + jnp.dot(p_c.astype(BF16), vc, preferred_element_type=F32))
        o_ref[j * hq:(j + 1) * hq, :] = o / l


def _nat_bias_table(rpb):
    h = rpb.shape[0]
    col = np.arange(GRID_W)
    c0 = np.clip(col - NAT_COLS // 2, 0, GRID_W - NAT_COLS)
    kc = np.arange(GRID_W)
    col_ok = (kc[None, :] >= c0[:, None]) & (kc[None, :] < c0[:, None] + NAT_COLS)
    dc = kc[None, :] - col[:, None] + NAT_COLS - 1
    sel_c = ((dc[None] == np.arange(2 * NAT_COLS - 1)[:, None, None]) & col_ok[None]).astype(np.float32)
    sel_c = sel_c.reshape(2 * NAT_COLS - 1, GRID_W * GRID_W)
    g = np.arange(NAT_GROUP)
    u = np.arange(NAT_SLAB)
    variants = []
    for delta in (0, NAT_ROWS // 2, NAT_ROWS):
        w0 = np.clip(g + delta - NAT_ROWS // 2, 0, NAT_SLAB - NAT_ROWS)
        row_ok = (u[None, :] >= w0[:, None]) & (u[None, :] < w0[:, None] + NAT_ROWS)
        dr = u[None, :] - (g[:, None] + delta) + NAT_ROWS - 1
        sel_r = ((dr[..., None] == np.arange(2 * NAT_ROWS - 1)) & row_ok[..., None]).astype(np.float32)
        sel_r = sel_r.reshape(NAT_GROUP * NAT_SLAB, 2 * NAT_ROWS - 1)
        bias = jnp.einsum('ad,hdc,cb->hab', sel_r, rpb.astype(F32), sel_c, precision=lax.Precision.HIGHEST)
        ok = (row_ok.reshape(-1)[:, None] & col_ok.reshape(-1)[None, :])[None]
        bias = jnp.where(ok, bias, NEG_BIG).reshape(h, NAT_GROUP, NAT_SLAB, GRID_W, GRID_W)
        variants.append(bias.transpose(0, 1, 3, 2, 4).reshape(h, NAT_GROUP * GRID_W, NAT_SLAB * GRID_W))
    return jnp.stack(variants, axis=1)


def _nat_latent(p_l, p_c, bias_tbl):
    b, n, _ = p_l.shape
    n_ctx = p_c.shape[1]
    n_rows = n // GRID_W
    assert n_rows >= NAT_SLAB and n_rows % NAT_GROUP == 0
    n_groups = n_rows // NAT_GROUP
    tq = NAT_GROUP * GRID_W
    tk = NAT_SLAB * GRID_W

    def variant(i):
        return jnp.where(i == 0, 0, jnp.where(i == n_groups - 1, 2, 1))

    return pl.pallas_call(
        functools.partial(_nat_kernel, n_rows=n_rows),
        grid=(b, N_HEADS_NAT, n_groups),
        in_specs=[pl.BlockSpec((None, tq, LANE), lambda bb, h, i: (bb, i, COL_NAT_Q + h)),
                  pl.BlockSpec((None, n, LANE), lambda bb, h, i: (bb, 0, COL_NAT_K + h)),
                  pl.BlockSpec((None, n, LANE), lambda bb, h, i: (bb, 0, COL_NAT_V + h)),
                  pl.BlockSpec((None, n_ctx, LANE), lambda bb, h, i: (bb, 0, COL_NAT_K + h)),
                  pl.BlockSpec((None, n_ctx, LANE), lambda bb, h, i: (bb, 0, COL_NAT_V + h)),
                  pl.BlockSpec((None, None, tq, tk), lambda bb, h, i: (h, variant(i), 0, 0))],
        out_specs=pl.BlockSpec((None, tq, LANE), lambda bb, h, i: (bb, i, h)),
        out_shape=jax.ShapeDtypeStruct((b, n, N_HEADS_NAT * HEAD_DIM), F32),
        compiler_params=_cparams(("parallel", "parallel", "arbitrary"), 48),
        name="nat_latent",
    )(p_l, p_l, p_l, p_c, p_c, bias_tbl)


def _ret_consts(logit, reverse):
    c_len = RET_CHUNK
    z = jnp.full((c_len, c_len), -logit, F32)
    lg = -(jnp.maximum(z, 0.0) + jnp.log1p(jnp.exp(-jnp.abs(z))))
    ci = lax.broadcasted_iota(jnp.int32, (c_len, c_len), 0).astype(F32)
    si = lax.broadcasted_iota(jnp.int32, (c_len, c_len), 1).astype(F32)
    if reverse:
        rel = si - ci
        wq = jnp.exp(lg * (c_len - ci))
        wk = jnp.exp(lg * ci)
    else:
        rel = ci - si
        wq = jnp.exp(lg * (ci + 1.0))
        wk = jnp.exp(lg * (c_len - 1.0 - ci))
    dmat = jnp.where(rel >= 0, jnp.exp(lg * jnp.maximum(rel, 0.0)), 0.0)
    return wq, wk, dmat, jnp.exp(lg * c_len)


def _ret_kernel(lf_ref, lb_ref, qf_ref, kf_ref, vf_ref, gf_ref, qb_ref, kb_ref, vb_ref, gb_ref, s0f_ref, s0b_ref,
                yf_ref, yb_ref, sf_ref, sb_ref, c_scr, *, n_chunks):
    h = pl.program_id(1)

    @pl.when(pl.program_id(2) == 0)
    def _():
        sf_ref[...] = s0f_ref[...]
        sb_ref[...] = s0b_ref[...]
        for j, arr in enumerate(_ret_consts(lf_ref[h], False) + _ret_consts(lb_ref[h], True)):
            c_scr[j] = arr

    c_len = RET_CHUNK
    nt = (((1,), (1,)), ((), ()))
    k_scale = HEAD_DIM ** -0.5
    dirs = []
    for d, refs in enumerate(((qf_ref, kf_ref, vf_ref, gf_ref, yf_ref), (qb_ref, kb_ref, vb_ref, gb_ref, yb_ref))):
        order = list(range(n_chunks)) if d == 0 else list(range(n_chunks - 1, -1, -1))
        dirs.append(refs + tuple(c_scr[4 * d + j] for j in range(4)) + (order,))

    units = []
    for step in range(n_chunks):
        for d in range(2):
            q_ref, k_ref, v_ref, _, _, _, wk, dmat, _, order = dirs[d]
            sl = slice(order[step] * c_len, (order[step] + 1) * c_len)
            qb = q_ref[sl, :].astype(BF16)
            kb = k_ref[sl, :]
            vb = v_ref[sl, :].astype(BF16)
            sc = lax.dot_general(qb, kb.astype(BF16), nt, preferred_element_type=F32) * (dmat * k_scale)
            kv = jnp.dot((kb.astype(F32) * (wk * k_scale)).T.astype(BF16), vb, preferred_element_type=F32)
            units.append((d, sl, qb, vb, sc, kv))
    states = [sf_ref[...], sb_ref[...]]
    cross = []
    for (d, sl, qb, vb, sc, kv) in units:
        wq, g_chunk = dirs[d][5], dirs[d][8]
        cross.append(wq * jnp.dot(qb, states[d].astype(BF16), preferred_element_type=F32))
        states[d] = g_chunk * states[d] + kv
    sf_ref[...] = states[0]
    sb_ref[...] = states[1]
    for (d, sl, qb, vb, sc, kv), o_cross in zip(units, cross):
        g_ref, y_ref = dirs[d][3], dirs[d][4]
        o = jnp.dot(sc.astype(BF16), vb, preferred_element_type=F32) + o_cross
        mu = jnp.mean(o, axis=-1, keepdims=True)
        dev = o - mu
        gn = dev * lax.rsqrt(jnp.mean(dev * dev, axis=-1, keepdims=True) + EPS)
        y_ref[sl, :] = _silu(g_ref[sl, :].astype(F32)) * gn


def _retention_scan(p, logit_f, logit_b, s0_f, s0_b):
    b, n, _ = p.shape
    hh = N_HEADS_RET
    tb = min(n, 8 * RET_CHUNK)
    nblk = n // tb

    def fwd(c0):
        return pl.BlockSpec((None, tb, LANE), lambda bb, h, i: (bb, i, c0 + h))

    def bwd(c0):
        return pl.BlockSpec((None, tb, LANE), lambda bb, h, i: (bb, nblk - 1 - i, c0 + h))

    state = pl.BlockSpec((None, None, HEAD_DIM, HEAD_DIM), lambda bb, h, i: (bb, h, 0, 0))
    smem = pl.BlockSpec(memory_space=pltpu.SMEM)
    y_shape = jax.ShapeDtypeStruct((b, n, hh * HEAD_DIM), F32)
    s_shape = jax.ShapeDtypeStruct((b, hh, HEAD_DIM, HEAD_DIM), F32)
    return pl.pallas_call(
        functools.partial(_ret_kernel, n_chunks=tb // RET_CHUNK),
        grid=(b, hh, nblk),
        in_specs=[smem, smem,
                  fwd(COL_RET_Q), fwd(COL_RET_K), fwd(COL_RET_V), fwd(COL_RET_GF),
                  bwd(COL_RET_Q), bwd(COL_RET_K), bwd(COL_RET_V), bwd(COL_RET_GB),
                  state, state],
        out_specs=[fwd(0), bwd(0), state, state],
        out_shape=[y_shape, y_shape, s_shape, s_shape],
        scratch_shapes=[pltpu.VMEM((8, RET_CHUNK, RET_CHUNK), F32)],
        compiler_params=_cparams(("parallel", "parallel", "arbitrary"), 32),
        name="retention",
    )(logit_f.astype(F32), logit_b.astype(F32), p, p, p, p, p, p, p, p, s0_f, s0_b)


def _retention(p_c, p_l, logit_f, logit_b):
    b = p_c.shape[0]
    s0 = jnp.zeros((b, N_HEADS_RET, HEAD_DIM, HEAD_DIM), F32)
    yc_f, yc_b, s_cf, s_cb = _retention_scan(p_c, logit_f, logit_b, s0, s0)
    yl_f, yl_b, _, _ = _retention_scan(p_l, logit_f, logit_b, s_cf, s_cb)
    return (yc_f, yc_b), (yl_f, yl_b)


def _merge_kernel(x_ref, ya_ref, yb_ref, yrf_ref, yrb_ref, g_ref, gate_ref, w_ref, g2_ref, sh2_ref, sc2_ref, wr_ref,
                  o_ref, h_ref, aff_ref):
    wa = ya_ref.shape[-1]
    wb = yb_ref.shape[-1]
    na = (_rms(ya_ref[...]) * g_ref[:, :wa]).astype(BF16)
    nb = (_rms(yb_ref[...]) * g_ref[:, wa:wa + wb]).astype(BF16)
    nr = (_rms(yrf_ref[...] + yrb_ref[...]) * g_ref[:, wa + wb:]).astype(BF16)
    acc = jnp.dot(na, w_ref[:wa, :], preferred_element_type=F32)
    acc = acc + jnp.dot(nb, w_ref[wa:wa + wb, :], preferred_element_type=F32)
    acc = acc + jnp.dot(nr, w_ref[wa + wb:, :], preferred_element_type=F32)
    x_new = x_ref[...] + gate_ref[...] * acc
    o_ref[...] = x_new
    h = (_rms(x_new) * g2_ref[...]) * (1.0 + sc2_ref[...]) + sh2_ref[...]
    h_hi = h.astype(BF16)
    h_ref[...] = h_hi
    h_lo = (h - h_hi.astype(F32)).astype(BF16)
    wr = wr_ref[...]
    w_hi = wr.astype(BF16)
    w_lo = (wr - w_hi.astype(F32)).astype(BF16)
    logits = (jnp.dot(h_hi, w_hi, preferred_element_type=F32)
              + (jnp.dot(h_hi, w_lo, preferred_element_type=F32) + jnp.dot(h_lo, w_hi, preferred_element_type=F32)))
    lane = lax.broadcasted_iota(jnp.int32, logits.shape, 1)
    logits = jnp.where(lane < N_EXPERTS, logits, NEG_BIG)
    e = jnp.exp(logits - jnp.max(logits, axis=-1, keepdims=True))
    aff_ref[...] = e / jnp.sum(e, axis=-1, keepdims=True)


def _merge_out(x, ya, yb, yr, g, gate, w_out, g2, shift2, scale2, w_router_pad):
    b, n, d = x.shape
    yrf, yrb = yr
    tm = min(n, 256)
    dm = w_out.shape[0]

    def tok(width):
        return pl.BlockSpec((None, tm, width), lambda bb, i: (bb, i, 0))

    row = pl.BlockSpec((1, d), lambda bb, i: (0, 0))
    per_batch = pl.BlockSpec((None, 1, d), lambda bb, i: (bb, 0, 0))
    return pl.pallas_call(
        _merge_kernel,
        grid=(b, n // tm),
        in_specs=[tok(d), tok(ya.shape[-1]), tok(yb.shape[-1]), tok(yrf.shape[-1]), tok(yrb.shape[-1]),
                  pl.BlockSpec((1, dm), lambda bb, i: (0, 0)), per_batch,
                  pl.BlockSpec((dm, d), lambda bb, i: (0, 0)),
                  row, per_batch, per_batch,
                  pl.BlockSpec((d, LANE), lambda bb, i: (0, 0))],
        out_specs=[tok(d), tok(d), tok(LANE)],
        out_shape=[jax.ShapeDtypeStruct((b, n, d), F32), jax.ShapeDtypeStruct((b, n, d), BF16),
                   jax.ShapeDtypeStruct((b, n, LANE), F32)],
        compiler_params=_cparams(("parallel", "parallel"), 48),
        name="merge_out",
    )(x, ya, yb, yrf, yrb, g.reshape(1, dm), gate, w_out, g2.reshape(1, d), shift2, scale2, w_router_pad)


def _ffn_up_kernel(xs_ref, wg_ref, wu_ref, hm_ref, wg_scr, wu_scr):
    @pl.when(pl.program_id(1) == 0)
    def _():
        wg_scr[...] = wg_ref[...].astype(BF16)
        wu_scr[...] = wu_ref[...].astype(BF16)

    xs = xs_ref[...]
    a = jnp.dot(xs, wg_scr[...], preferred_element_type=F32)
    u = jnp.dot(xs, wu_scr[...], preferred_element_type=F32)
    hm_ref[...] = (_silu(a) * u).astype(BF16)


def _ffn_down_kernel(hm_ref, gate_ref, wd_ref, o_ref, wd_scr):
    @pl.when(pl.program_id(1) == 0)
    def _():
        wd_scr[...] = wd_ref[...].astype(BF16)

    o_ref[...] = (jnp.dot(hm_ref[...], wd_scr[...], preferred_element_type=F32) * gate_ref[...]).astype(o_ref.dtype)


def _expert_ffn(xs, gates, w_gate, w_up, w_down, layer):
    e, t, d = xs.shape
    f = w_gate.shape[-1]
    tm = min(t, 256)
    tm_down = min(t, 512)
    hm = pl.pallas_call(
        _ffn_up_kernel,
        grid=(e, t // tm),
        in_specs=[pl.BlockSpec((None, tm, d), lambda ee, i: (ee, i, 0)),
                  pl.BlockSpec((None, None, d, f), lambda ee, i: (layer, ee, 0, 0)),
                  pl.BlockSpec((None, None, d, f), lambda ee, i: (layer, ee, 0, 0))],
        out_specs=pl.BlockSpec((None, tm, f), lambda ee, i: (ee, i, 0)),
        out_shape=jax.ShapeDtypeStruct((e, t, f), BF16),
        scratch_shapes=[pltpu.VMEM((d, f), BF16), pltpu.VMEM((d, f), BF16)],
        compiler_params=_cparams(("parallel", "arbitrary"), 56),
        name="expert_ffn_up",
    )(xs, w_gate, w_up)
    return pl.pallas_call(
        _ffn_down_kernel,
        grid=(e, t // tm_down),
        in_specs=[pl.BlockSpec((None, tm_down, f), lambda ee, i: (ee, i, 0)),
                  pl.BlockSpec((None, tm_down, 1), lambda ee, i: (ee, i, 0)),
                  pl.BlockSpec((None, None, f, d), lambda ee, i: (layer, ee, 0, 0))],
        out_specs=pl.BlockSpec((None, tm_down, d), lambda ee, i: (ee, i, 0)),
        out_shape=jax.ShapeDtypeStruct((e, t, d), BF16),
        scratch_shapes=[pltpu.VMEM((f, d), BF16)],
        compiler_params=_cparams(("parallel", "arbitrary"), 48),
        name="expert_ffn_down",
    )(hm, gates, w_down)


COMBINE_TOKENS = 256
COMBINE_ROWS = 256


def _combine_kernel(tile_ref, win_ref, flag_ref, x_ref, yp_ref, tok_ref, gate_ref, gn_ref, o_ref, acc_scr, *,
                    final_norm):
    s = pl.program_id(0)
    flags = flag_ref[s]
    t = acc_scr.shape[0]

    @pl.when((flags & 1) != 0)
    def _():
        acc_scr[...] = jnp.zeros(acc_scr.shape, F32)

    @pl.when((flags & 2) != 0)
    def _():
        tcol = tile_ref[s] * t + lax.broadcasted_iota(jnp.int32, (t, 1), 0)
        onehot = jnp.where(tcol == tok_ref[...], 1.0, 0.0).astype(BF16)
        acc_scr[...] += jnp.dot(onehot, yp_ref[...], preferred_element_type=F32)

    @pl.when((flags & 4) != 0)
    def _():
        y = x_ref[...] + gate_ref[...] * acc_scr[...]
        if final_norm:
            y = _rms(y) * gn_ref[...]
        o_ref[...] = y


def _moe_combine(x, y_rows, tok_rows, gate, norm_g, final_norm):
    b, n, d = x.shape
    r = y_rows.shape[0]
    t = min(COMBINE_TOKENS, n)
    w = COMBINE_ROWS
    ntiles = b * n // t
    nwin_total = r // w
    tiles_per_batch = n // t
    n_steps = nwin_total + 2 * ntiles
    i32 = jnp.int32
    tok_sorted, order = lax.sort_key_val(tok_rows.astype(i32), jnp.arange(r, dtype=i32))
    yp = jnp.take(y_rows, order, axis=0, mode="clip")
    edges = jnp.arange(ntiles + 1, dtype=i32) * t
    bounds = jnp.sum((tok_sorted[None, :] < edges[:, None]).astype(i32), axis=1)
    w_lo = jnp.minimum(bounds[:-1] // w, nwin_total - 1)
    w_hi = jnp.maximum((bounds[1:] + w - 1) // w, w_lo + 1)
    nwin = w_hi - w_lo
    step0 = jnp.cumsum(nwin) - nwin
    total = jnp.sum(nwin)
    s = jnp.arange(n_steps, dtype=i32)
    tile_s = jnp.sum((s[:, None] >= step0[None, :]).astype(i32), axis=1) - 1
    sel = tile_s[:, None] == jnp.arange(ntiles, dtype=i32)[None, :]

    def of_tile(v):
        return jnp.sum(jnp.where(sel, v[None, :], 0), axis=1)

    step0_s, w_lo_s, nwin_s = of_tile(step0), of_tile(w_lo), of_tile(nwin)
    win_s = jnp.minimum(w_lo_s + (s - step0_s), w_lo_s + nwin_s - 1)
    valid = s < total
    first = valid & (s == step0_s)
    last = valid & (s == step0_s + nwin_s - 1)
    flags = first.astype(i32) + 2 * valid.astype(i32) + 4 * last.astype(i32)
    out = pl.pallas_call(
        functools.partial(_combine_kernel, final_norm=final_norm),
        grid_spec=pltpu.PrefetchScalarGridSpec(
            num_scalar_prefetch=3,
            grid=(n_steps,),
            in_specs=[pl.BlockSpec((t, d), lambda s, tl, wn, fl: (tl[s], 0)),
                      pl.BlockSpec((w, d), lambda s, tl, wn, fl: (wn[s], 0)),
                      pl.BlockSpec((None, 1, w), lambda s, tl, wn, fl: (wn[s], 0, 0)),
                      pl.BlockSpec((None, 1, d), lambda s, tl, wn, fl: (tl[s] // tiles_per_batch, 0, 0)),
                      pl.BlockSpec((1, d), lambda s, tl, wn, fl: (0, 0))],
            out_specs=pl.BlockSpec((t, d), lambda s, tl, wn, fl: (tl[s], 0)),
            scratch_shapes=[pltpu.VMEM((t, d), F32)]),
        out_shape=jax.ShapeDtypeStruct((b * n, d), F32),
        compiler_params=_cparams(("arbitrary",), 40),
        name="moe_combine",
    )(tile_s, win_s, flags, x.reshape(b * n, d), yp, tok_sorted.reshape(nwin_total, 1, w), gate,
      norm_g.reshape(1, d))
    return out.reshape(b, n, d)


def _ec_moe(x, h, aff, gate2, experts, norm_g, final_norm):
    b, n, d = x.shape
    cap = EC_CAPACITY * n // N_EXPERTS
    gate, idx = lax.top_k(jnp.swapaxes(aff[..., :N_EXPERTS], 1, 2), cap)
    flat = (idx + (jnp.arange(b, dtype=idx.dtype) * n)[:, None, None]).transpose(1, 0, 2).reshape(-1)
    xs = jnp.take(h.reshape(b * n, d), flat, axis=0, mode="clip").reshape(N_EXPERTS, b * cap, d)
    gates = gate.transpose(1, 0, 2).reshape(N_EXPERTS, b * cap, 1)
    y = _expert_ffn(xs, gates, *experts)
    return _moe_combine(x, y.reshape(-1, d), flat, gate2, norm_g, final_norm)


_ROPE_SRC = np.concatenate([np.arange(16, 32), np.arange(0, 16), np.arange(48, 64), np.arange(32, 48)])
_ROPE_SIGN = np.concatenate([-np.ones(16), np.ones(16), -np.ones(16), np.ones(16)]).astype(np.float32)


def _rope_perm_cols(w):
    return w[:, _ROPE_SRC] * _ROPE_SIGN


def _pad_w_in(w_in):
    c_kr = MLA_Q_LORA + MLA_KV_LORA
    kr = w_in[:, c_kr:c_kr + MLA_ROPE_DIM]
    return jnp.concatenate([w_in[:, :c_kr + MLA_ROPE_DIM], _rope_perm_cols(kr), w_in[:, c_kr + MLA_ROPE_DIM:]],
                           axis=1).astype(BF16)


def _expand_w_uq(w_uq):
    w = w_uq.reshape(MLA_Q_LORA, N_HEADS_MLA, HEAD_DIM + MLA_ROPE_DIM)
    nope = w[:, :, :HEAD_DIM]
    rope = w[:, :, HEAD_DIM:]
    partner = rope[:, :, _ROPE_SRC] * _ROPE_SIGN
    return jnp.concatenate([nope, rope, partner], axis=-1).reshape(MLA_Q_LORA, N_HEADS_MLA * MLA_QK_PAD).astype(BF16)


def _rope_tables(n):
    t = jnp.arange(n)
    n_freq = MLA_ROPE_DIM // 4
    inv = ROPE_BASE ** (-jnp.arange(n_freq, dtype=F32) / n_freq)
    ang_r = (t // GRID_W).astype(F32)[:, None] * inv
    ang_c = (t % GRID_W).astype(F32)[:, None] * inv
    zeros = jnp.zeros((n, LANE - MLA_ROPE_DIM), F32)
    cos = jnp.concatenate([jnp.cos(ang_r), jnp.cos(ang_r), jnp.cos(ang_c), jnp.cos(ang_c), zeros], axis=1)
    sin = jnp.concatenate([jnp.sin(ang_r), jnp.sin(ang_r), jnp.sin(ang_c), jnp.sin(ang_c), zeros], axis=1)
    return cos, sin


def _identity_tables(n):
    ones = jnp.ones((n, MLA_ROPE_DIM), F32)
    zeros = jnp.zeros((n, LANE - MLA_ROPE_DIM), F32)
    return jnp.concatenate([ones, zeros], axis=1), jnp.zeros((n, LANE), F32)


def _layer(x_l, x_c, mod, lp, last, final_g):
    b, n, d = x_l.shape
    n_ctx = x_c.shape[1]

    def chunk(k, ctx):
        rows = jnp.broadcast_to(mod[b, k * d:(k + 1) * d], (b, d)) if ctx else mod[:b, k * d:(k + 1) * d]
        return rows.reshape(b, 1, d)

    w_in = _pad_w_in(lp['w_in'])
    p_l = _norm_proj(x_l, lp['norm1_g'], chunk(0, False), chunk(1, False), w_in)
    p_c = _norm_proj(x_c, lp['norm1_g'], chunk(0, True), chunk(1, True), w_in)

    cos_l, sin_l = _rope_tables(n)
    cos_c, sin_c = _identity_tables(n_ctx)
    w_uk = lp['mla_w_uk'].reshape(MLA_KV_LORA, -1).astype(BF16)
    w_uv = lp['mla_w_uv'].reshape(MLA_KV_LORA, -1).astype(BF16)
    w_q = _expand_w_uq(lp['mla_w_uq'])
    k_l, v_l = _mla_kv_prep(p_l, lp['mla_kv_norm_g'], w_uk, w_uv, cos_l, sin_l)
    k_c, v_c = _mla_kv_prep(p_c, lp['mla_kv_norm_g'], w_uk, w_uv, cos_c, sin_c)
    q_l = _mla_q_prep(p_l, lp['mla_q_norm_g'], w_q, cos_l, sin_l)
    a_l = _attention_t(q_l, [(k_c, v_c), (k_l, v_l)], N_HEADS_MLA, MLA_QK_PAD, HEAD_DIM)

    b_l = _nat_latent(p_l, p_c, _nat_bias_table(lp['nat_rpb']))
    r_c, r_l = _retention(p_c, p_l, lp['ret_decay_f'], lp['ret_decay_b'])

    w_out = lp['w_out'].astype(BF16)
    w_router = jnp.zeros((d, LANE), F32).at[:, :N_EXPERTS].set(lp['w_router'])

    x_l, h_l, aff_l = _merge_out(x_l, a_l, b_l, r_l, lp['out_norm_g'], chunk(2, False), w_out,
                                 lp['norm2_g'], chunk(3, False), chunk(4, False), w_router)
    x_l = _ec_moe(x_l, h_l, aff_l, chunk(5, False), lp['experts'], final_g, last)
    if last:
        return x_l, None

    q_c = _mla_q_prep(p_c, lp['mla_q_norm_g'], w_q, cos_c, sin_c)
    a_c = _attention_t(q_c, [(k_c, v_c)], N_HEADS_MLA, MLA_QK_PAD, HEAD_DIM)
    b_c = _attention(p_c, COL_NAT_Q, HEAD_DIM, [(p_c, COL_NAT_K, p_c, COL_NAT_V)], HEAD_DIM, N_HEADS_NAT,
                     HEAD_DIM ** -0.5)
    x_c, h_c, aff_c = _merge_out(x_c, a_c, b_c, r_c, lp['out_norm_g'], chunk(2, True), w_out,
                                 lp['norm2_g'], chunk(3, True), chunk(4, True), w_router)
    x_c = _ec_moe(x_c, h_c, aff_c, chunk(5, True), lp['experts'], final_g, False)
    return x_l, x_c


def kernel(x, c, ctx, c_ctx, w_mod, b_mod, norm1_g, w_in, mla_q_norm_g, mla_kv_norm_g, mla_w_uq, mla_w_uk,
           mla_w_uv, nat_rpb, ret_decay_f, ret_decay_b, out_norm_g, w_out, norm2_g, w_router, w_gate, w_up,
           w_down, final_norm_g):
    depth = w_mod.shape[0]
    cvecs = jnp.concatenate([c, c_ctx[None, :]], axis=0)
    x_l, x_c = x, ctx
    for i in range(depth):
        lp = {
            'norm1_g': norm1_g[i], 'w_in': w_in[i], 'mla_q_norm_g': mla_q_norm_g[i],
            'mla_kv_norm_g': mla_kv_norm_g[i], 'mla_w_uq': mla_w_uq[i], 'mla_w_uk': mla_w_uk[i],
            'mla_w_uv': mla_w_uv[i], 'nat_rpb': nat_rpb[i], 'ret_decay_f': ret_decay_f[i],
            'ret_decay_b': ret_decay_b[i], 'out_norm_g': out_norm_g[i], 'w_out': w_out[i],
            'norm2_g': norm2_g[i], 'w_router': w_router[i], 'experts': (w_gate, w_up, w_down, i),
        }
        mod = _modulation(cvecs, w_mod[i], b_mod[i])
        x_l, x_c = _layer(x_l, x_c, mod, lp, i == depth - 1, final_norm_g)
    return x_l
```

```python
import functools

import numpy as np
import jax
import jax.numpy as jnp
from jax import lax
from jax.experimental import pallas as pl
from jax.experimental.pallas import tpu as pltpu

F32 = jnp.float32
BF16 = jnp.bfloat16

GRID_W = 64
HEAD_DIM = 128
N_HEADS_MLA = 8
N_HEADS_NAT = 4
N_HEADS_RET = 4
MLA_ROPE_DIM = 64
MLA_Q_LORA = 512
MLA_KV_LORA = 256
MLA_SCALE = (HEAD_DIM + MLA_ROPE_DIM) ** -0.5
LOG2E = 1.4426950408889634
MLA_Q_SCALE = MLA_SCALE * LOG2E
MLA_QK_PAD = 256
NAT_ROWS = 8
NAT_COLS = 16
RET_CHUNK = 128
N_EXPERTS = 16
EC_CAPACITY = 2
ROPE_BASE = 10000.0
EPS = 1e-6
NEG_BIG = -1e30
LANE = 128

COL_CQ = 0
COL_CKV = 4
COL_KR = 6
COL_NAT_Q = 7
COL_NAT_K = 11
COL_NAT_V = 15
COL_RET_Q = 19
COL_RET_K = 23
COL_RET_V = 27
COL_RET_GF = 31
COL_RET_GB = 35
IN_COLS_PAD = 39 * LANE


def _cparams(sem, vmem_mib):
    return pltpu.CompilerParams(dimension_semantics=sem, vmem_limit_bytes=vmem_mib * 1024 * 1024)


def _silu(a):
    return a * (1.0 / (1.0 + jnp.exp(-a)))


def _rms(x):
    return x * lax.rsqrt(jnp.mean(x * x, axis=-1, keepdims=True) + EPS)


def _mod_kernel(ct_ref, w_ref, b_ref, o_ref, a_scr, *, n_rows, k_chunk):
    ct = ct_ref[...]
    a_scr[...] = _silu(ct)
    d = w_ref.shape[0]
    tn = w_ref.shape[1]

    def body(kc, accs):
        k0 = pl.multiple_of(kc * k_chunk, k_chunk)
        wblk = w_ref[pl.ds(k0, k_chunk), :]
        out = []
        for r in range(n_rows):
            col = a_scr[pl.ds(k0, k_chunk), r:r + 1]
            out.append(accs[r] + jnp.sum((col * wblk).reshape(k_chunk // 8, 8, tn), axis=0))
        return tuple(out)

    accs = lax.fori_loop(0, d // k_chunk, body, tuple(jnp.zeros((8, tn), F32) for _ in range(n_rows)))
    o_ref[...] = jnp.zeros(o_ref.shape, F32)
    for r in range(n_rows):
        o_ref[r:r + 1, :] = jnp.sum(accs[r], axis=0, keepdims=True) + b_ref[...]


def _modulation(cvecs, w_mod, b_mod):
    n_rows, d = cvecs.shape
    n_out = w_mod.shape[1]
    tn = 1024
    ct = jnp.zeros((d, 8), F32).at[:, :n_rows].set(cvecs.T)
    return pl.pallas_call(
        functools.partial(_mod_kernel, n_rows=n_rows, k_chunk=64),
        grid=(n_out // tn,),
        in_specs=[pl.BlockSpec((d, 8), lambda j: (0, 0)),
                  pl.BlockSpec((d, tn), lambda j: (0, j)),
                  pl.BlockSpec((1, tn), lambda j: (0, j))],
        out_specs=pl.BlockSpec((8, tn), lambda j: (0, j)),
        out_shape=jax.ShapeDtypeStruct((8, n_out), F32),
        scratch_shapes=[pltpu.VMEM((d, 8), F32)],
        compiler_params=_cparams(("parallel",), 40),
        name="modulation",
    )(ct, w_mod, b_mod.reshape(1, n_out))


def _norm_proj_kernel(x_ref, g_ref, sh_ref, sc_ref, w_ref, o_ref, h_scr):
    @pl.when(pl.program_id(2) == 0)
    def _():
        y = _rms(x_ref[...]) * g_ref[...]
        h_scr[...] = (y * (1.0 + sc_ref[...]) + sh_ref[...]).astype(BF16)

    o_ref[...] = jnp.dot(h_scr[...], w_ref[...], preferred_element_type=F32).astype(o_ref.dtype)


def _norm_proj(x, g, shift, scale, w):
    b, n, d = x.shape
    c = w.shape[1]
    tm = min(n, 1024)
    tn = c // 3 if (c % (3 * LANE) == 0) else c
    return pl.pallas_call(
        _norm_proj_kernel,
        grid=(b, n // tm, c // tn),
        in_specs=[pl.BlockSpec((None, tm, d), lambda bb, i, j: (bb, i, 0)),
                  pl.BlockSpec((1, d), lambda bb, i, j: (0, 0)),
                  pl.BlockSpec((None, 1, d), lambda bb, i, j: (bb, 0, 0)),
                  pl.BlockSpec((None, 1, d), lambda bb, i, j: (bb, 0, 0)),
                  pl.BlockSpec((d, tn), lambda bb, i, j: (0, j))],
        out_specs=pl.BlockSpec((None, tm, tn), lambda bb, i, j: (bb, i, j)),
        out_shape=jax.ShapeDtypeStruct((b, n, c), BF16),
        scratch_shapes=[pltpu.VMEM((tm, d), BF16)],
        compiler_params=_cparams(("parallel", "parallel", "arbitrary"), 56),
        name="norm_proj",
    )(x, g.reshape(1, d), shift, scale, w)


def _rope_rotate(r, cos, sin):
    return r * cos + pltpu.roll(r, 64, 1) * sin


def _mla_q_kernel(cq_ref, g_ref, w_ref, cos_ref, sin_ref, o_ref, *, n_heads):
    cqn = (_rms(cq_ref[...].astype(F32)) * g_ref[...]).astype(BF16)
    qe = jnp.dot(cqn, w_ref[...], preferred_element_type=F32)
    cos = cos_ref[...]
    sin = sin_ref[...]
    for h in range(n_heads):
        c0 = h * MLA_QK_PAD
        o_ref[c0:c0 + LANE, :] = (qe[:, c0:c0 + LANE] * MLA_Q_SCALE).T.astype(BF16)
        rot = _rope_rotate(qe[:, c0 + LANE:c0 + 2 * LANE], cos, sin)
        o_ref[c0 + LANE:c0 + 2 * LANE, :] = (rot * MLA_Q_SCALE).T.astype(BF16)


def _mla_q_prep(p, g, w_q, cos, sin):
    b, n, _ = p.shape
    tm = min(n, 512)
    cw = N_HEADS_MLA * MLA_QK_PAD
    return pl.pallas_call(
        functools.partial(_mla_q_kernel, n_heads=N_HEADS_MLA),
        grid=(b, n // tm),
        in_specs=[pl.BlockSpec((None, tm, MLA_Q_LORA), lambda bb, i: (bb, i, COL_CQ * LANE // MLA_Q_LORA)),
                  pl.BlockSpec((1, MLA_Q_LORA), lambda bb, i: (0, 0)),
                  pl.BlockSpec((MLA_Q_LORA, cw), lambda bb, i: (0, 0)),
                  pl.BlockSpec((tm, LANE), lambda bb, i: (i, 0)),
                  pl.BlockSpec((tm, LANE), lambda bb, i: (i, 0))],
        out_specs=pl.BlockSpec((None, cw, tm), lambda bb, i: (bb, 0, i)),
        out_shape=jax.ShapeDtypeStruct((b, cw, n), BF16),
        compiler_params=_cparams(("parallel", "parallel"), 40),
        name="mla_q_prep",
    )(p, g.reshape(1, MLA_Q_LORA), w_q, cos, sin)


def _mla_kv_kernel(ckv_ref, kr_ref, g_ref, wuk_ref, wuv_ref, cos_ref, sin_ref, k_ref, v_ref, *, n_heads):
    ckvn = (_rms(ckv_ref[...].astype(F32)) * g_ref[...]).astype(BF16)
    kn = jnp.dot(ckvn, wuk_ref[...], preferred_element_type=F32)
    v_ref[...] = jnp.dot(ckvn, wuv_ref[...], preferred_element_type=F32).T.astype(BF16)
    rot = _rope_rotate(kr_ref[...].astype(F32), cos_ref[...], sin_ref[...]).astype(BF16)
    for h in range(n_heads):
        c0 = h * MLA_QK_PAD
        k_ref[:, c0:c0 + LANE] = kn[:, h * LANE:(h + 1) * LANE].astype(BF16)
        k_ref[:, c0 + LANE:c0 + 2 * LANE] = rot


def _mla_kv_prep(p, g, w_uk, w_uv, cos, sin):
    b, n, _ = p.shape
    tm = min(n, 512)
    kw = N_HEADS_MLA * MLA_QK_PAD
    vw = N_HEADS_MLA * HEAD_DIM
    return pl.pallas_call(
        functools.partial(_mla_kv_kernel, n_heads=N_HEADS_MLA),
        grid=(b, n // tm),
        in_specs=[pl.BlockSpec((None, tm, MLA_KV_LORA), lambda bb, i: (bb, i, COL_CKV * LANE // MLA_KV_LORA)),
                  pl.BlockSpec((None, tm, LANE), lambda bb, i: (bb, i, COL_KR)),
                  pl.BlockSpec((1, MLA_KV_LORA), lambda bb, i: (0, 0)),
                  pl.BlockSpec((MLA_KV_LORA, vw), lambda bb, i: (0, 0)),
                  pl.BlockSpec((MLA_KV_LORA, vw), lambda bb, i: (0, 0)),
                  pl.BlockSpec((tm, LANE), lambda bb, i: (i, 0)),
                  pl.BlockSpec((tm, LANE), lambda bb, i: (i, 0))],
        out_specs=[pl.BlockSpec((None, tm, kw), lambda bb, i: (bb, i, 0)),
                   pl.BlockSpec((None, vw, tm), lambda bb, i: (bb, 0, i))],
        out_shape=[jax.ShapeDtypeStruct((b, n, kw), BF16), jax.ShapeDtypeStruct((b, vw, n), BF16)],
        compiler_params=_cparams(("parallel", "parallel"), 40),
        name="mla_kv_prep",
    )(p, p, g.reshape(1, MLA_KV_LORA), w_uk, w_uv, cos, sin)


def _attn_kernel(*refs, n_src, scale, tk_max):
    q_ref = refs[0]
    o_ref = refs[1 + 2 * n_src]
    q = q_ref[...]
    if scale != 1.0:
        q = q.astype(F32) * scale
    q = q.astype(BF16)
    tq = q.shape[0]
    dv = o_ref.shape[-1]

    def step(k, v, carry):
        m, l, acc = carry
        s = lax.dot_general(q, k, (((1,), (1,)), ((), ())), preferred_element_type=F32)
        m_new = jnp.maximum(m, jnp.max(s, axis=-1, keepdims=True))
        p = jnp.exp(s - m_new)
        alpha = jnp.exp(m - m_new)
        l = alpha * l + jnp.sum(p, axis=-1, keepdims=True)
        acc = alpha * acc + jnp.dot(p.astype(BF16), v, preferred_element_type=F32)
        return m_new, l, acc

    carry = (jnp.full((tq, 1), NEG_BIG, F32), jnp.zeros((tq, 1), F32), jnp.zeros((tq, dv), F32))
    for s_i in range(n_src):
        k_ref = refs[1 + 2 * s_i]
        v_ref = refs[2 + 2 * s_i]
        nk = k_ref.shape[0]
        tk = min(nk, tk_max)
        if nk == tk:
            carry = step(k_ref[...].astype(BF16), v_ref[...].astype(BF16), carry)
        else:
            def body(c, cr, k_ref=k_ref, v_ref=v_ref, tk=tk):
                k0 = pl.multiple_of(c * tk, tk)
                return step(k_ref[pl.ds(k0, tk), :].astype(BF16), v_ref[pl.ds(k0, tk), :].astype(BF16), cr)
            carry = lax.fori_loop(0, nk // tk, body, carry)
    _, l, acc = carry
    o_ref[...] = acc / l


def _attention(q_arr, q_blk0, dq, sources, dv, n_heads, scale, tq=256, tk_max=512):
    b, nq, _ = q_arr.shape
    tq = min(nq, tq)
    in_specs = [pl.BlockSpec((None, tq, dq), lambda bb, h, i: (bb, i, q_blk0 + h))]
    args = [q_arr]
    for k_arr, k_blk0, v_arr, v_blk0 in sources:
        nk = k_arr.shape[1]
        in_specs.append(pl.BlockSpec((None, nk, dq), lambda bb, h, i, o=k_blk0: (bb, 0, o + h)))
        in_specs.append(pl.BlockSpec((None, nk, dv), lambda bb, h, i, o=v_blk0: (bb, 0, o + h)))
        args += [k_arr, v_arr]
    return pl.pallas_call(
        functools.partial(_attn_kernel, n_src=len(sources), scale=scale, tk_max=tk_max),
        grid=(b, n_heads, nq // tq),
        in_specs=in_specs,
        out_specs=pl.BlockSpec((None, tq, dv), lambda bb, h, i: (bb, i, h)),
        out_shape=jax.ShapeDtypeStruct((b, nq, n_heads * dv), F32),
        compiler_params=_cparams(("parallel", "parallel", "arbitrary"), 48),
        name="attention",
    )(*args)


ATTN_TQ = 2048
ATTN_TK = 512
ATTN_GROUP = 4


def _attn_t_kernel(*refs, n_src):
    q_ref = refs[0]
    o_ref = refs[1 + 2 * n_src]
    q = q_ref[...]
    tq = q.shape[1]
    dv = o_ref.shape[-1]

    def scores(k):
        return jnp.dot(k, q, preferred_element_type=F32)

    def update(s, vt, carry):
        m, l, acc = carry
        m_new = jnp.maximum(m, jnp.max(s, axis=0, keepdims=True))
        p = jnp.exp2(s - m_new)
        alpha = jnp.exp2(m - m_new)
        l = alpha * l + jnp.sum(p, axis=0, keepdims=True)
        acc = alpha * acc + jnp.dot(vt, p.astype(BF16), preferred_element_type=F32)
        return m_new, l, acc

    carry = (jnp.full((1, tq), NEG_BIG, F32), jnp.zeros((1, tq), F32), jnp.zeros((dv, tq), F32))
    for s_i in range(n_src):
        k_ref = refs[1 + 2 * s_i]
        vt_ref = refs[2 + 2 * s_i]
        nk = k_ref.shape[0]
        tk = min(nk, ATTN_TK)
        n_chunks = nk // tk
        if n_chunks == 1:
            carry = update(scores(k_ref[...]), vt_ref[...], carry)
            continue
        group = max(g for g in (ATTN_GROUP, 2, 1) if n_chunks % g == 0)

        def body(c, cr, k_ref=k_ref, vt_ref=vt_ref, tk=tk, group=group):
            starts = [pl.multiple_of((c * group + g) * tk, tk) for g in range(group)]
            ss = [scores(k_ref[pl.ds(k0, tk), :]) for k0 in starts]
            for s, k0 in zip(ss, starts):
                cr = update(s, vt_ref[:, pl.ds(k0, tk)], cr)
            return cr
        carry = lax.fori_loop(0, n_chunks // group, body, carry)
    _, l, acc = carry
    o_ref[...] = (acc / l).T


def _attention_t(q_arr, sources, n_heads, dq, dv):
    b, _, nq = q_arr.shape
    tq = min(nq, ATTN_TQ)
    in_specs = [pl.BlockSpec((None, dq, tq), lambda bb, h, i: (bb, h, i))]
    args = [q_arr]
    for k_arr, vt_arr in sources:
        nk = k_arr.shape[1]
        in_specs.append(pl.BlockSpec((None, nk, dq), lambda bb, h, i: (bb, 0, h)))
        in_specs.append(pl.BlockSpec((None, dv, nk), lambda bb, h, i: (bb, h, 0)))
        args += [k_arr, vt_arr]
    return pl.pallas_call(
        functools.partial(_attn_t_kernel, n_src=len(sources)),
        grid=(b, n_heads, nq // tq),
        in_specs=in_specs,
        out_specs=pl.BlockSpec((None, tq, dv), lambda bb, h, i: (bb, i, h)),
        out_shape=jax.ShapeDtypeStruct((b, nq, n_heads * dv), F32),
        compiler_params=_cparams(("parallel", "parallel", "arbitrary"), 56),
        name="attention_t",
    )(*args)


NAT_GROUP = 8
NAT_SLAB = NAT_GROUP + NAT_ROWS
NAT_HALVES = 2


def _nat_kernel(q_ref, k_ref, v_ref, kc_ref, vc_ref, bias_ref, o_ref, *, n_rows):
    i = pl.program_id(2)
    scale = HEAD_DIM ** -0.5
    nt = (((1,), (1,)), ((), ()))
    u0 = jnp.clip(i * NAT_GROUP - NAT_ROWS // 2, 0, n_rows - NAT_SLAB)
    t0 = pl.multiple_of(u0 * GRID_W, GRID_W)
    ks = k_ref[pl.ds(t0, NAT_SLAB * GRID_W), :].astype(BF16)
    vs = v_ref[pl.ds(t0, NAT_SLAB * GRID_W), :].astype(BF16)
    kc = kc_ref[...].astype(BF16)
    vc = vc_ref[...].astype(BF16)
    hq = q_ref.shape[0] // NAT_HALVES
    scores = []
    for j in range(NAT_HALVES):
        rows = slice(j * hq, (j + 1) * hq)
        q = (q_ref[rows, :].astype(F32) * scale).astype(BF16)
        scores.append((lax.dot_general(q, ks, nt, preferred_element_type=F32) + bias_ref[rows, :],
                       lax.dot_general(q, kc, nt, preferred_element_type=F32)))
    for j in range(NAT_HALVES):
        s_w, s_c = scores[j]
        m = jnp.maximum(jnp.max(s_w, axis=-1, keepdims=True), jnp.max(s_c, axis=-1, keepdims=True))
        p_w = jnp.exp(s_w - m)
        p_c = jnp.exp(s_c - m)
        l = jnp.sum(p_w, axis=-1, keepdims=True) + jnp.sum(p_c, axis=-1, keepdims=True)
        o = (jnp.dot(p_w.astype(BF16), vs, preferred_element_type=F32)
             + jnp.dot(p_c.astype(BF16), vc, preferred_element_type=F32))
        o_ref[j * hq:(j + 1) * hq, :] = o / l


def _nat_bias_table(rpb):
    h = rpb.shape[0]
    col = np.arange(GRID_W)
    c0 = np.clip(col - NAT_COLS // 2, 0, GRID_W - NAT_COLS)
    kc = np.arange(GRID_W)
    col_ok = (kc[None, :] >= c0[:, None]) & (kc[None, :] < c0[:, None] + NAT_COLS)
    dc = kc[None, :] - col[:, None] + NAT_COLS - 1
    sel_c = ((dc[None] == np.arange(2 * NAT_COLS - 1)[:, None, None]) & col_ok[None]).astype(np.float32)
    sel_c = sel_c.reshape(2 * NAT_COLS - 1, GRID_W * GRID_W)
    per_dr = jnp.einsum('hdc,cb->hdb', rpb.astype(F32), sel_c, precision=lax.Precision.HIGHEST)
    per_dr = jnp.where(col_ok.reshape(-1)[None, None], per_dr, NEG_BIG).reshape(h, 2 * NAT_ROWS - 1, GRID_W, GRID_W)
    masked = jnp.full((h, GRID_W, GRID_W), NEG_BIG, F32)
    variants = []
    for delta in (0, NAT_ROWS // 2, NAT_ROWS):
        groups = []
        for g in range(NAT_GROUP):
            w0 = min(max(g + delta - NAT_ROWS // 2, 0), NAT_SLAB - NAT_ROWS)
            rows = [per_dr[:, u - (g + delta) + NAT_ROWS - 1] if w0 <= u < w0 + NAT_ROWS else masked
                    for u in range(NAT_SLAB)]
            groups.append(jnp.stack(rows, axis=2))
        variants.append(jnp.stack(groups, axis=1).reshape(h, NAT_GROUP * GRID_W, NAT_SLAB * GRID_W))
    return jnp.stack(variants, axis=1)


def _nat_latent(p_l, p_c, bias_tbl):
    b, n, _ = p_l.shape
    n_ctx = p_c.shape[1]
    n_rows = n // GRID_W
    assert n_rows >= NAT_SLAB and n_rows % NAT_GROUP == 0
    n_groups = n_rows // NAT_GROUP
    tq = NAT_GROUP * GRID_W
    tk = NAT_SLAB * GRID_W

    def variant(i):
        return jnp.where(i == 0, 0, jnp.where(i == n_groups - 1, 2, 1))

    return pl.pallas_call(
        functools.partial(_nat_kernel, n_rows=n_rows),
        grid=(b, N_HEADS_NAT, n_groups),
        in_specs=[pl.BlockSpec((None, tq, LANE), lambda bb, h, i: (bb, i, COL_NAT_Q + h)),
                  pl.BlockSpec((None, n, LANE), lambda bb, h, i: (bb, 0, COL_NAT_K + h)),
                  pl.BlockSpec((None, n, LANE), lambda bb, h, i: (bb, 0, COL_NAT_V + h)),
                  pl.BlockSpec((None, n_ctx, LANE), lambda bb, h, i: (bb, 0, COL_NAT_K + h)),
                  pl.BlockSpec((None, n_ctx, LANE), lambda bb, h, i: (bb, 0, COL_NAT_V + h)),
                  pl.BlockSpec((None, None, tq, tk), lambda bb, h, i: (h, variant(i), 0, 0))],
        out_specs=pl.BlockSpec((None, tq, LANE), lambda bb, h, i: (bb, i, h)),
        out_shape=jax.ShapeDtypeStruct((b, n, N_HEADS_NAT * HEAD_DIM), F32),
        compiler_params=_cparams(("parallel", "parallel", "arbitrary"), 48),
        name="nat_latent",
    )(p_l, p_l, p_l, p_c, p_c, bias_tbl)


def _ret_consts(logit, reverse):
    c_len = RET_CHUNK
    z = jnp.full((c_len, c_len), -logit, F32)
    lg = -(jnp.maximum(z, 0.0) + jnp.log1p(jnp.exp(-jnp.abs(z))))
    ci = lax.broadcasted_iota(jnp.int32, (c_len, c_len), 0).astype(F32)
    si = lax.broadcasted_iota(jnp.int32, (c_len, c_len), 1).astype(F32)
    if reverse:
        rel = si - ci
        wq = jnp.exp(lg * (c_len - ci))
        wk = jnp.exp(lg * ci)
    else:
        rel = ci - si
        wq = jnp.exp(lg * (ci + 1.0))
        wk = jnp.exp(lg * (c_len - 1.0 - ci))
    dmat = jnp.where(rel >= 0, jnp.exp(lg * jnp.maximum(rel, 0.0)), 0.0)
    return wq, wk, dmat, jnp.exp(lg * c_len)


def _ret_kernel(lf_ref, lb_ref, qf_ref, kf_ref, vf_ref, gf_ref, qb_ref, kb_ref, vb_ref, gb_ref, s0f_ref, s0b_ref,
                yf_ref, yb_ref, sf_ref, sb_ref, c_scr, *, n_chunks):
    h = pl.program_id(1)

    @pl.when(pl.program_id(2) == 0)
    def _():
        sf_ref[...] = s0f_ref[...]
        sb_ref[...] = s0b_ref[...]
        for j, arr in enumerate(_ret_consts(lf_ref[h], False) + _ret_consts(lb_ref[h], True)):
            c_scr[j] = arr

    c_len = RET_CHUNK
    nt = (((1,), (1,)), ((), ()))
    k_scale = HEAD_DIM ** -0.5
    dirs = []
    for d, refs in enumerate(((qf_ref, kf_ref, vf_ref, gf_ref, yf_ref), (qb_ref, kb_ref, vb_ref, gb_ref, yb_ref))):
        order = list(range(n_chunks)) if d == 0 else list(range(n_chunks - 1, -1, -1))
        dirs.append(refs + tuple(c_scr[4 * d + j] for j in range(4)) + (order,))

    units = []
    for step in range(n_chunks):
        for d in range(2):
            q_ref, k_ref, v_ref, _, _, _, wk, dmat, _, order = dirs[d]
            sl = slice(order[step] * c_len, (order[step] + 1) * c_len)
            qb = q_ref[sl, :].astype(BF16)
            kb = k_ref[sl, :]
            vb = v_ref[sl, :].astype(BF16)
            sc = lax.dot_general(qb, kb.astype(BF16), nt, preferred_element_type=F32) * (dmat * k_scale)
            kv = jnp.dot((kb.astype(F32) * (wk * k_scale)).T.astype(BF16), vb, preferred_element_type=F32)
            units.append((d, sl, qb, vb, sc, kv))
    states = [sf_ref[...], sb_ref[...]]
    cross = []
    for (d, sl, qb, vb, sc, kv) in units:
        wq, g_chunk = dirs[d][5], dirs[d][8]
        cross.append(wq * jnp.dot(qb, states[d].astype(BF16), preferred_element_type=F32))
        states[d] = g_chunk * states[d] + kv
    sf_ref[...] = states[0]
    sb_ref[...] = states[1]
    for (d, sl, qb, vb, sc, kv), o_cross in zip(units, cross):
        g_ref, y_ref = dirs[d][3], dirs[d][4]
        o = jnp.dot(sc.astype(BF16), vb, preferred_element_type=F32) + o_cross
        mu = jnp.mean(o, axis=-1, keepdims=True)
        dev = o - mu
        gn = dev * lax.rsqrt(jnp.mean(dev * dev, axis=-1, keepdims=True) + EPS)
        y_ref[sl, :] = _silu(g_ref[sl, :].astype(F32)) * gn


def _retention_scan(p, logit_f, logit_b, s0_f, s0_b):
    b, n, _ = p.shape
    hh = N_HEADS_RET
    tb = min(n, 8 * RET_CHUNK)
    nblk = n // tb

    def fwd(c0):
        return pl.BlockSpec((None, tb, LANE), lambda bb, h, i: (bb, i, c0 + h))

    def bwd(c0):
        return pl.BlockSpec((None, tb, LANE), lambda bb, h, i: (bb, nblk - 1 - i, c0 + h))

    state = pl.BlockSpec((None, None, HEAD_DIM, HEAD_DIM), lambda bb, h, i: (bb, h, 0, 0))
    smem = pl.BlockSpec(memory_space=pltpu.SMEM)
    y_shape = jax.ShapeDtypeStruct((b, n, hh * HEAD_DIM), F32)
    s_shape = jax.ShapeDtypeStruct((b, hh, HEAD_DIM, HEAD_DIM), F32)
    return pl.pallas_call(
        functools.partial(_ret_kernel, n_chunks=tb // RET_CHUNK),
        grid=(b, hh, nblk),
        in_specs=[smem, smem,
                  fwd(COL_RET_Q), fwd(COL_RET_K), fwd(COL_RET_V), fwd(COL_RET_GF),
                  bwd(COL_RET_Q), bwd(COL_RET_K), bwd(COL_RET_V), bwd(COL_RET_GB),
                  state, state],
        out_specs=[fwd(0), bwd(0), state, state],
        out_shape=[y_shape, y_shape, s_shape, s_shape],
        scratch_shapes=[pltpu.VMEM((8, RET_CHUNK, RET_CHUNK), F32)],
        compiler_params=_cparams(("parallel", "parallel", "arbitrary"), 32),
        name="retention",
    )(logit_f.astype(F32), logit_b.astype(F32), p, p, p, p, p, p, p, p, s0_f, s0_b)


def _retention(p_c, p_l, logit_f, logit_b):
    b = p_c.shape[0]
    s0 = jnp.zeros((b, N_HEADS_RET, HEAD_DIM, HEAD_DIM), F32)
    yc_f, yc_b, s_cf, s_cb = _retention_scan(p_c, logit_f, logit_b, s0, s0)
    yl_f, yl_b, _, _ = _retention_scan(p_l, logit_f, logit_b, s_cf, s_cb)
    return (yc_f, yc_b), (yl_f, yl_b)


def _merge_kernel(x_ref, ya_ref, yb_ref, yrf_ref, yrb_ref, g_ref, gate_ref, w_ref, g2_ref, sh2_ref, sc2_ref, wr_ref,
                  o_ref, h_ref, aff_ref):
    wa = ya_ref.shape[-1]
    wb = yb_ref.shape[-1]
    na = (_rms(ya_ref[...]) * g_ref[:, :wa]).astype(BF16)
    nb = (_rms(yb_ref[...]) * g_ref[:, wa:wa + wb]).astype(BF16)
    nr = (_rms(yrf_ref[...] + yrb_ref[...]) * g_ref[:, wa + wb:]).astype(BF16)
    acc = jnp.dot(na, w_ref[:wa, :], preferred_element_type=F32)
    acc = acc + jnp.dot(nb, w_ref[wa:wa + wb, :], preferred_element_type=F32)
    acc = acc + jnp.dot(nr, w_ref[wa + wb:, :], preferred_element_type=F32)
    x_new = x_ref[...] + gate_ref[...] * acc
    o_ref[...] = x_new
    h = (_rms(x_new) * g2_ref[...]) * (1.0 + sc2_ref[...]) + sh2_ref[...]
    h_hi = h.astype(BF16)
    h_ref[...] = h_hi
    h_lo = (h - h_hi.astype(F32)).astype(BF16)
    wr = wr_ref[...]
    w_hi = wr.astype(BF16)
    w_lo = (wr - w_hi.astype(F32)).astype(BF16)
    logits = (jnp.dot(h_hi, w_hi, preferred_element_type=F32)
              + (jnp.dot(h_hi, w_lo, preferred_element_type=F32) + jnp.dot(h_lo, w_hi, preferred_element_type=F32)))
    lane = lax.broadcasted_iota(jnp.int32, logits.shape, 1)
    logits = jnp.where(lane < N_EXPERTS, logits, NEG_BIG)
    e = jnp.exp(logits - jnp.max(logits, axis=-1, keepdims=True))
    aff_ref[...] = e / jnp.sum(e, axis=-1, keepdims=True)


def _merge_out(x, ya, yb, yr, g, gate, w_out, g2, shift2, scale2, w_router_pad):
    b, n, d = x.shape
    yrf, yrb = yr
    tm = min(n, 256)
    dm = w_out.shape[0]

    def tok(width):
        return pl.BlockSpec((None, tm, width), lambda bb, i: (bb, i, 0))

    row = pl.BlockSpec((1, d), lambda bb, i: (0, 0))
    per_batch = pl.BlockSpec((None, 1, d), lambda bb, i: (bb, 0, 0))
    return pl.pallas_call(
        _merge_kernel,
        grid=(b, n // tm),
        in_specs=[tok(d), tok(ya.shape[-1]), tok(yb.shape[-1]), tok(yrf.shape[-1]), tok(yrb.shape[-1]),
                  pl.BlockSpec((1, dm), lambda bb, i: (0, 0)), per_batch,
                  pl.BlockSpec((dm, d), lambda bb, i: (0, 0)),
                  row, per_batch, per_batch,
                  pl.BlockSpec((d, LANE), lambda bb, i: (0, 0))],
        out_specs=[tok(d), tok(d), tok(LANE)],
        out_shape=[jax.ShapeDtypeStruct((b, n, d), F32), jax.ShapeDtypeStruct((b, n, d), BF16),
                   jax.ShapeDtypeStruct((b, n, LANE), F32)],
        compiler_params=_cparams(("parallel", "parallel"), 48),
        name="merge_out",
    )(x, ya, yb, yrf, yrb, g.reshape(1, dm), gate, w_out, g2.reshape(1, d), shift2, scale2, w_router_pad)


def _ffn_up_kernel(xs_ref, wg_ref, wu_ref, hm_ref, wg_scr, wu_scr):
    @pl.when(pl.program_id(1) == 0)
    def _():
        wg_scr[...] = wg_ref[...].astype(BF16)
        wu_scr[...] = wu_ref[...].astype(BF16)

    xs = xs_ref[...]
    a = jnp.dot(xs, wg_scr[...], preferred_element_type=F32)
    u = jnp.dot(xs, wu_scr[...], preferred_element_type=F32)
    hm_ref[...] = (_silu(a) * u).astype(BF16)


def _ffn_down_kernel(hm_ref, gate_ref, wd_ref, o_ref, wd_scr):
    @pl.when(pl.program_id(1) == 0)
    def _():
        wd_scr[...] = wd_ref[...].astype(BF16)

    o_ref[...] = (jnp.dot(hm_ref[...], wd_scr[...], preferred_element_type=F32) * gate_ref[...]).astype(o_ref.dtype)


def _expert_ffn(xs, gates, w_gate, w_up, w_down, layer):
    e, t, d = xs.shape
    f = w_gate.shape[-1]
    tm = min(t, 256)
    tm_down = min(t, 512)
    hm = pl.pallas_call(
        _ffn_up_kernel,
        grid=(e, t // tm),
        in_specs=[pl.BlockSpec((None, tm, d), lambda ee, i: (ee, i, 0)),
                  pl.BlockSpec((None, None, d, f), lambda ee, i: (layer, ee, 0, 0)),
                  pl.BlockSpec((None, None, d, f), lambda ee, i: (layer, ee, 0, 0))],
        out_specs=pl.BlockSpec((None, tm, f), lambda ee, i: (ee, i, 0)),
        out_shape=jax.ShapeDtypeStruct((e, t, f), BF16),
        scratch_shapes=[pltpu.VMEM((d, f), BF16), pltpu.VMEM((d, f), BF16)],
        compiler_params=_cparams(("parallel", "arbitrary"), 56),
        name="expert_ffn_up",
    )(xs, w_gate, w_up)
    return pl.pallas_call(
        _ffn_down_kernel,
        grid=(e, t // tm_down),
        in_specs=[pl.BlockSpec((None, tm_down, f), lambda ee, i: (ee, i, 0)),
                  pl.BlockSpec((None, tm_down, 1), lambda ee, i: (ee, i, 0)),
                  pl.BlockSpec((None, None, f, d), lambda ee, i: (layer, ee, 0, 0))],
        out_specs=pl.BlockSpec((None, tm_down, d), lambda ee, i: (ee, i, 0)),
        out_shape=jax.ShapeDtypeStruct((e, t, d), BF16),
        scratch_shapes=[pltpu.VMEM((f, d), BF16)],
        compiler_params=_cparams(("parallel", "arbitrary"), 48),
        name="expert_ffn_down",
    )(hm, gates, w_down)


COMBINE_TOKENS = 256
COMBINE_ROWS = 512


def _combine_kernel(tile_ref, win_ref, flag_ref, x_ref, yp_ref, tok_ref, gate_ref, gn_ref, o_ref, acc_scr, *,
                    final_norm):
    s = pl.program_id(0)
    flags = flag_ref[s]
    t = acc_scr.shape[0]

    @pl.when((flags & 1) != 0)
    def _():
        acc_scr[...] = jnp.zeros(acc_scr.shape, F32)

    @pl.when((flags & 2) != 0)
    def _():
        tcol = tile_ref[s] * t + lax.broadcasted_iota(jnp.int32, (t, 1), 0)
        onehot = jnp.where(tcol == tok_ref[...], 1.0, 0.0).astype(BF16)
        acc_scr[...] += jnp.dot(onehot, yp_ref[...], preferred_element_type=F32)

    @pl.when((flags & 4) != 0)
    def _():
        y = x_ref[...] + gate_ref[...] * acc_scr[...]
        if final_norm:
            y = _rms(y) * gn_ref[...]
        o_ref[...] = y


def _moe_combine(x, y_rows, tok_rows, gate, norm_g, final_norm):
    b, n, d = x.shape
    r = y_rows.shape[0]
    t = min(COMBINE_TOKENS, n)
    w = COMBINE_ROWS
    ntiles = b * n // t
    nwin_total = r // w
    tiles_per_batch = n // t
    n_steps = nwin_total + 2 * ntiles
    i32 = jnp.int32
    tok_sorted, order = lax.sort_key_val(tok_rows.astype(i32), jnp.arange(r, dtype=i32))
    yp = jnp.take(y_rows, order, axis=0, mode="clip")
    edges = jnp.arange(ntiles + 1, dtype=i32) * t
    bounds = jnp.sum((tok_sorted[None, :] < edges[:, None]).astype(i32), axis=1)
    w_lo = jnp.minimum(bounds[:-1] // w, nwin_total - 1)
    w_hi = jnp.maximum((bounds[1:] + w - 1) // w, w_lo + 1)
    nwin = w_hi - w_lo
    step0 = jnp.cumsum(nwin) - nwin
    total = jnp.sum(nwin)
    s = jnp.arange(n_steps, dtype=i32)
    tile_s = jnp.sum((s[:, None] >= step0[None, :]).astype(i32), axis=1) - 1
    sel = tile_s[:, None] == jnp.arange(ntiles, dtype=i32)[None, :]

    def of_tile(v):
        return jnp.sum(jnp.where(sel, v[None, :], 0), axis=1)

    step0_s, w_lo_s, nwin_s = of_tile(step0), of_tile(w_lo), of_tile(nwin)
    win_s = jnp.minimum(w_lo_s + (s - step0_s), w_lo_s + nwin_s - 1)
    valid = s < total
    first = valid & (s == step0_s)
    last = valid & (s == step0_s + nwin_s - 1)
    flags = first.astype(i32) + 2 * valid.astype(i32) + 4 * last.astype(i32)
    out = pl.pallas_call(
        functools.partial(_combine_kernel, final_norm=final_norm),
        grid_spec=pltpu.PrefetchScalarGridSpec(
            num_scalar_prefetch=3,
            grid=(n_steps,),
            in_specs=[pl.BlockSpec((t, d), lambda s, tl, wn, fl: (tl[s], 0)),
                      pl.BlockSpec((w, d), lambda s, tl, wn, fl: (wn[s], 0)),
                      pl.BlockSpec((None, 1, w), lambda s, tl, wn, fl: (wn[s], 0, 0)),
                      pl.BlockSpec((None, 1, d), lambda s, tl, wn, fl: (tl[s] // tiles_per_batch, 0, 0)),
                      pl.BlockSpec((1, d), lambda s, tl, wn, fl: (0, 0))],
            out_specs=pl.BlockSpec((t, d), lambda s, tl, wn, fl: (tl[s], 0)),
            scratch_shapes=[pltpu.VMEM((t, d), F32)]),
        out_shape=jax.ShapeDtypeStruct((b * n, d), F32),
        compiler_params=_cparams(("arbitrary",), 40),
        name="moe_combine",
    )(tile_s, win_s, flags, x.reshape(b * n, d), yp, tok_sorted.reshape(nwin_total, 1, w), gate,
      norm_g.reshape(1, d))
    return out.reshape(b, n, d)


def _ec_moe(x, h, aff, gate2, experts, norm_g, final_norm):
    b, n, d = x.shape
    cap = EC_CAPACITY * n // N_EXPERTS
    gate, idx = lax.top_k(jnp.swapaxes(aff[..., :N_EXPERTS], 1, 2), cap)
    flat = (idx + (jnp.arange(b, dtype=idx.dtype) * n)[:, None, None]).transpose(1, 0, 2).reshape(-1)
    xs = jnp.take(h.reshape(b * n, d), flat, axis=0, mode="clip").reshape(N_EXPERTS, b * cap, d)
    gates = gate.transpose(1, 0, 2).reshape(N_EXPERTS, b * cap, 1)
    y = _expert_ffn(xs, gates, *experts)
    return _moe_combine(x, y.reshape(-1, d), flat, gate2, norm_g, final_norm)


_ROPE_SRC = np.concatenate([np.arange(16, 32), np.arange(0, 16), np.arange(48, 64), np.arange(32, 48)])
_ROPE_SIGN = np.concatenate([-np.ones(16), np.ones(16), -np.ones(16), np.ones(16)]).astype(np.float32)


def _rope_perm_cols(w):
    return w[:, _ROPE_SRC] * _ROPE_SIGN


def _pad_w_in(w_in):
    c_kr = MLA_Q_LORA + MLA_KV_LORA
    kr = w_in[:, c_kr:c_kr + MLA_ROPE_DIM]
    return jnp.concatenate([w_in[:, :c_kr + MLA_ROPE_DIM], _rope_perm_cols(kr), w_in[:, c_kr + MLA_ROPE_DIM:]],
                           axis=1).astype(BF16)


def _expand_w_uq(w_uq):
    w = w_uq.reshape(MLA_Q_LORA, N_HEADS_MLA, HEAD_DIM + MLA_ROPE_DIM)
    nope = w[:, :, :HEAD_DIM]
    rope = w[:, :, HEAD_DIM:]
    partner = rope[:, :, _ROPE_SRC] * _ROPE_SIGN
    return jnp.concatenate([nope, rope, partner], axis=-1).reshape(MLA_Q_LORA, N_HEADS_MLA * MLA_QK_PAD).astype(BF16)


def _rope_tables(n):
    t = jnp.arange(n)
    n_freq = MLA_ROPE_DIM // 4
    inv = ROPE_BASE ** (-jnp.arange(n_freq, dtype=F32) / n_freq)
    ang_r = (t // GRID_W).astype(F32)[:, None] * inv
    ang_c = (t % GRID_W).astype(F32)[:, None] * inv
    zeros = jnp.zeros((n, LANE - MLA_ROPE_DIM), F32)
    cos = jnp.concatenate([jnp.cos(ang_r), jnp.cos(ang_r), jnp.cos(ang_c), jnp.cos(ang_c), zeros], axis=1)
    sin = jnp.concatenate([jnp.sin(ang_r), jnp.sin(ang_r), jnp.sin(ang_c), jnp.sin(ang_c), zeros], axis=1)
    return cos, sin


def _identity_tables(n):
    ones = jnp.ones((n, MLA_ROPE_DIM), F32)
    zeros = jnp.zeros((n, LANE - MLA_ROPE_DIM), F32)
    return jnp.concatenate([ones, zeros], axis=1), jnp.zeros((n, LANE), F32)


def _layer(x_l, x_c, mod, lp, last, final_g):
    b, n, d = x_l.shape
    n_ctx = x_c.shape[1]

    def chunk(k, ctx):
        rows = jnp.broadcast_to(mod[b, k * d:(k + 1) * d], (b, d)) if ctx else mod[:b, k * d:(k + 1) * d]
        return rows.reshape(b, 1, d)

    w_in = _pad_w_in(lp['w_in'])
    p_l = _norm_proj(x_l, lp['norm1_g'], chunk(0, False), chunk(1, False), w_in)
    p_c = _norm_proj(x_c, lp['norm1_g'], chunk(0, True), chunk(1, True), w_in)

    cos_l, sin_l = _rope_tables(n)
    cos_c, sin_c = _identity_tables(n_ctx)
    w_uk = lp['mla_w_uk'].reshape(MLA_KV_LORA, -1).astype(BF16)
    w_uv = lp['mla_w_uv'].reshape(MLA_KV_LORA, -1).astype(BF16)
    w_q = _expand_w_uq(lp['mla_w_uq'])
    k_l, v_l = _mla_kv_prep(p_l, lp['mla_kv_norm_g'], w_uk, w_uv, cos_l, sin_l)
    k_c, v_c = _mla_kv_prep(p_c, lp['mla_kv_norm_g'], w_uk, w_uv, cos_c, sin_c)
    q_l = _mla_q_prep(p_l, lp['mla_q_norm_g'], w_q, cos_l, sin_l)
    a_l = _attention_t(q_l, [(k_c, v_c), (k_l, v_l)], N_HEADS_MLA, MLA_QK_PAD, HEAD_DIM)

    b_l = _nat_latent(p_l, p_c, _nat_bias_table(lp['nat_rpb']))
    r_c, r_l = _retention(p_c, p_l, lp['ret_decay_f'], lp['ret_decay_b'])

    w_out = lp['w_out'].astype(BF16)
    w_router = jnp.zeros((d, LANE), F32).at[:, :N_EXPERTS].set(lp['w_router'])

    x_l, h_l, aff_l = _merge_out(x_l, a_l, b_l, r_l, lp['out_norm_g'], chunk(2, False), w_out,
                                 lp['norm2_g'], chunk(3, False), chunk(4, False), w_router)
    x_l = _ec_moe(x_l, h_l, aff_l, chunk(5, False), lp['experts'], final_g, last)
    if last:
        return x_l, None

    q_c = _mla_q_prep(p_c, lp['mla_q_norm_g'], w_q, cos_c, sin_c)
    a_c = _attention_t(q_c, [(k_c, v_c)], N_HEADS_MLA, MLA_QK_PAD, HEAD_DIM)
    b_c = _attention(p_c, COL_NAT_Q, HEAD_DIM, [(p_c, COL_NAT_K, p_c, COL_NAT_V)], HEAD_DIM, N_HEADS_NAT,
                     HEAD_DIM ** -0.5)
    x_c, h_c, aff_c = _merge_out(x_c, a_c, b_c, r_c, lp['out_norm_g'], chunk(2, True), w_out,
                                 lp['norm2_g'], chunk(3, True), chunk(4, True), w_router)
    x_c = _ec_moe(x_c, h_c, aff_c, chunk(5, True), lp['experts'], final_g, False)
    return x_l, x_c


def kernel(x, c, ctx, c_ctx, w_mod, b_mod, norm1_g, w_in, mla_q_norm_g, mla_kv_norm_g, mla_w_uq, mla_w_uk,
           mla_w_uv, nat_rpb, ret_decay_f, ret_decay_b, out_norm_g, w_out, norm2_g, w_router, w_gate, w_up,
           w_down, final_norm_g):
    depth = w_mod.shape[0]
    cvecs = jnp.concatenate([c, c_ctx[None, :]], axis=0)
    x_l, x_c = x, ctx
    for i in range(depth):
        lp = {
            'norm1_g': norm1_g[i], 'w_in': w_in[i], 'mla_q_norm_g': mla_q_norm_g[i],
            'mla_kv_norm_g': mla_kv_norm_g[i], 'mla_w_uq': mla_w_uq[i], 'mla_w_uk': mla_w_uk[i],
            'mla_w_uv': mla_w_uv[i], 'nat_rpb': nat_rpb[i], 'ret_decay_f': ret_decay_f[i],
            'ret_decay_b': ret_decay_b[i], 'out_norm_g': out_norm_g[i], 'w_out': w_out[i],
            'norm2_g': norm2_g[i], 'w_router': w_router[i], 'experts': (w_gate, w_up, w_down, i),
        }
        mod = _modulation(cvecs, w_mod[i], b_mod[i])
        x_l, x_c = _layer(x_l, x_c, mod, lp, i == depth - 1, final_norm_g)
    return x_l
```

```python
import functools

import numpy as np
import jax
import jax.numpy as jnp
from jax import lax
from jax.experimental import pallas as pl
from jax.experimental.pallas import tpu as pltpu

F32 = jnp.float32
BF16 = jnp.bfloat16

GRID_W = 64
HEAD_DIM = 128
N_HEADS_MLA = 8
N_HEADS_NAT = 4
N_HEADS_RET = 4
MLA_ROPE_DIM = 64
MLA_Q_LORA = 512
MLA_KV_LORA = 256
MLA_SCALE = (HEAD_DIM + MLA_ROPE_DIM) ** -0.5
LOG2E = 1.4426950408889634
MLA_Q_SCALE = MLA_SCALE * LOG2E
MLA_QK_PAD = 256
NAT_ROWS = 8
NAT_COLS = 16
RET_CHUNK = 128
N_EXPERTS = 16
EC_CAPACITY = 2
ROPE_BASE = 10000.0
EPS = 1e-6
NEG_BIG = -1e30
LANE = 128

COL_CQ = 0
COL_CKV = 4
COL_KR = 6
COL_NAT_Q = 7
COL_NAT_K = 11
COL_NAT_V = 15
COL_RET_Q = 19
COL_RET_K = 23
COL_RET_V = 27
COL_RET_GF = 31
COL_RET_GB = 35
IN_COLS_PAD = 39 * LANE


def _cparams(sem, vmem_mib):
    return pltpu.CompilerParams(dimension_semantics=sem, vmem_limit_bytes=vmem_mib * 1024 * 1024)


def _silu(a):
    return a * (1.0 / (1.0 + jnp.exp(-a)))


def _rms(x):
    return x * lax.rsqrt(jnp.mean(x * x, axis=-1, keepdims=True) + EPS)


def _mod_kernel(ct_ref, w_ref, b_ref, o_ref, a_scr, *, n_rows, k_chunk):
    ct = ct_ref[...]
    a_scr[...] = _silu(ct)
    d = w_ref.shape[0]
    tn = w_ref.shape[1]

    def body(kc, accs):
        k0 = pl.multiple_of(kc * k_chunk, k_chunk)
        wblk = w_ref[pl.ds(k0, k_chunk), :]
        out = []
        for r in range(n_rows):
            col = a_scr[pl.ds(k0, k_chunk), r:r + 1]
            out.append(accs[r] + jnp.sum((col * wblk).reshape(k_chunk // 8, 8, tn), axis=0))
        return tuple(out)

    accs = lax.fori_loop(0, d // k_chunk, body, tuple(jnp.zeros((8, tn), F32) for _ in range(n_rows)))
    o_ref[...] = jnp.zeros(o_ref.shape, F32)
    for r in range(n_rows):
        o_ref[r:r + 1, :] = jnp.sum(accs[r], axis=0, keepdims=True) + b_ref[...]


def _modulation(cvecs, w_mod, b_mod):
    n_rows, d = cvecs.shape
    n_out = w_mod.shape[1]
    tn = 1024
    ct = jnp.zeros((d, 8), F32).at[:, :n_rows].set(cvecs.T)
    return pl.pallas_call(
        functools.partial(_mod_kernel, n_rows=n_rows, k_chunk=64),
        grid=(n_out // tn,),
        in_specs=[pl.BlockSpec((d, 8), lambda j: (0, 0)),
                  pl.BlockSpec((d, tn), lambda j: (0, j)),
                  pl.BlockSpec((1, tn), lambda j: (0, j))],
        out_specs=pl.BlockSpec((8, tn), lambda j: (0, j)),
        out_shape=jax.ShapeDtypeStruct((8, n_out), F32),
        scratch_shapes=[pltpu.VMEM((d, 8), F32)],
        compiler_params=_cparams(("parallel",), 40),
        name="modulation",
    )(ct, w_mod, b_mod.reshape(1, n_out))


def _norm_proj_kernel(x_ref, g_ref, sh_ref, sc_ref, w_ref, o_ref, h_scr):
    @pl.when(pl.program_id(2) == 0)
    def _():
        y = _rms(x_ref[...]) * g_ref[...]
        h_scr[...] = (y * (1.0 + sc_ref[...]) + sh_ref[...]).astype(BF16)

    o_ref[...] = jnp.dot(h_scr[...], w_ref[...], preferred_element_type=F32).astype(o_ref.dtype)


def _norm_proj(x, g, shift, scale, w):
    b, n, d = x.shape
    c = w.shape[1]
    tm = min(n, 1024)
    tn = c // 3 if (c % (3 * LANE) == 0) else c
    return pl.pallas_call(
        _norm_proj_kernel,
        grid=(b, n // tm, c // tn),
        in_specs=[pl.BlockSpec((None, tm, d), lambda bb, i, j: (bb, i, 0)),
                  pl.BlockSpec((1, d), lambda bb, i, j: (0, 0)),
                  pl.BlockSpec((None, 1, d), lambda bb, i, j: (bb, 0, 0)),
                  pl.BlockSpec((None, 1, d), lambda bb, i, j: (bb, 0, 0)),
                  pl.BlockSpec((d, tn), lambda bb, i, j: (0, j))],
        out_specs=pl.BlockSpec((None, tm, tn), lambda bb, i, j: (bb, i, j)),
        out_shape=jax.ShapeDtypeStruct((b, n, c), BF16),
        scratch_shapes=[pltpu.VMEM((tm, d), BF16)],
        compiler_params=_cparams(("parallel", "parallel", "arbitrary"), 56),
        name="norm_proj",
    )(x, g.reshape(1, d), shift, scale, w)


def _rope_rotate(r, cos, sin):
    return r * cos + pltpu.roll(r, 64, 1) * sin


def _mla_q_kernel(cq_ref, g_ref, w_ref, cos_ref, sin_ref, o_ref, *, n_heads):
    cqn = (_rms(cq_ref[...].astype(F32)) * g_ref[...]).astype(BF16)
    qe = jnp.dot(cqn, w_ref[...], preferred_element_type=F32)
    cos = cos_ref[...]
    sin = sin_ref[...]
    for h in range(n_heads):
        c0 = h * MLA_QK_PAD
        o_ref[c0:c0 + LANE, :] = (qe[:, c0:c0 + LANE] * MLA_Q_SCALE).T.astype(BF16)
        rot = _rope_rotate(qe[:, c0 + LANE:c0 + 2 * LANE], cos, sin)
        o_ref[c0 + LANE:c0 + 2 * LANE, :] = (rot * MLA_Q_SCALE).T.astype(BF16)


def _mla_q_prep(p, g, w_q, cos, sin):
    b, n, _ = p.shape
    tm = min(n, 512)
    cw = N_HEADS_MLA * MLA_QK_PAD
    return pl.pallas_call(
        functools.partial(_mla_q_kernel, n_heads=N_HEADS_MLA),
        grid=(b, n // tm),
        in_specs=[pl.BlockSpec((None, tm, MLA_Q_LORA), lambda bb, i: (bb, i, COL_CQ * LANE // MLA_Q_LORA)),
                  pl.BlockSpec((1, MLA_Q_LORA), lambda bb, i: (0, 0)),
                  pl.BlockSpec((MLA_Q_LORA, cw), lambda bb, i: (0, 0)),
                  pl.BlockSpec((tm, LANE), lambda bb, i: (i, 0)),
                  pl.BlockSpec((tm, LANE), lambda bb, i: (i, 0))],
        out_specs=pl.BlockSpec((None, cw, tm), lambda bb, i: (bb, 0, i)),
        out_shape=jax.ShapeDtypeStruct((b, cw, n), BF16),
        compiler_params=_cparams(("parallel", "parallel"), 40),
        name="mla_q_prep",
    )(p, g.reshape(1, MLA_Q_LORA), w_q, cos, sin)


def _mla_kv_kernel(ckv_ref, kr_ref, g_ref, wuk_ref, wuv_ref, cos_ref, sin_ref, k_ref, v_ref, *, n_heads):
    ckvn = (_rms(ckv_ref[...].astype(F32)) * g_ref[...]).astype(BF16)
    kn = jnp.dot(ckvn, wuk_ref[...], preferred_element_type=F32)
    v_ref[...] = jnp.dot(ckvn, wuv_ref[...], preferred_element_type=F32).T.astype(BF16)
    rot = _rope_rotate(kr_ref[...].astype(F32), cos_ref[...], sin_ref[...]).astype(BF16)
    for h in range(n_heads):
        c0 = h * MLA_QK_PAD
        k_ref[:, c0:c0 + LANE] = kn[:, h * LANE:(h + 1) * LANE].astype(BF16)
        k_ref[:, c0 + LANE:c0 + 2 * LANE] = rot


def _mla_kv_prep(p, g, w_uk, w_uv, cos, sin):
    b, n, _ = p.shape
    tm = min(n, 512)
    kw = N_HEADS_MLA * MLA_QK_PAD
    vw = N_HEADS_MLA * HEAD_DIM
    return pl.pallas_call(
        functools.partial(_mla_kv_kernel, n_heads=N_HEADS_MLA),
        grid=(b, n // tm),
        in_specs=[pl.BlockSpec((None, tm, MLA_KV_LORA), lambda bb, i: (bb, i, COL_CKV * LANE // MLA_KV_LORA)),
                  pl.BlockSpec((None, tm, LANE), lambda bb, i: (bb, i, COL_KR)),
                  pl.BlockSpec((1, MLA_KV_LORA), lambda bb, i: (0, 0)),
                  pl.BlockSpec((MLA_KV_LORA, vw), lambda bb, i: (0, 0)),
                  pl.BlockSpec((MLA_KV_LORA, vw), lambda bb, i: (0, 0)),
                  pl.BlockSpec((tm, LANE), lambda bb, i: (i, 0)),
                  pl.BlockSpec((tm, LANE), lambda bb, i: (i, 0))],
        out_specs=[pl.BlockSpec((None, tm, kw), lambda bb, i: (bb, i, 0)),
                   pl.BlockSpec((None, vw, tm), lambda bb, i: (bb, 0, i))],
        out_shape=[jax.ShapeDtypeStruct((b, n, kw), BF16), jax.ShapeDtypeStruct((b, vw, n), BF16)],
        compiler_params=_cparams(("parallel", "parallel"), 40),
        name="mla_kv_prep",
    )(p, p, g.reshape(1, MLA_KV_LORA), w_uk, w_uv, cos, sin)


def _attn_kernel(*refs, n_src, scale, tk_max):
    q_ref = refs[0]
    o_ref = refs[1 + 2 * n_src]
    q = q_ref[...]
    if scale != 1.0:
        q = q.astype(F32) * scale
    q = q.astype(BF16)
    tq = q.shape[0]
    dv = o_ref.shape[-1]

    def step(k, v, carry):
        m, l, acc = carry
        s = lax.dot_general(q, k, (((1,), (1,)), ((), ())), preferred_element_type=F32)
        m_new = jnp.maximum(m, jnp.max(s, axis=-1, keepdims=True))
        p = jnp.exp(s - m_new)
        alpha = jnp.exp(m - m_new)
        l = alpha * l + jnp.sum(p, axis=-1, keepdims=True)
        acc = alpha * acc + jnp.dot(p.astype(BF16), v, preferred_element_type=F32)
        return m_new, l, acc

    carry = (jnp.full((tq, 1), NEG_BIG, F32), jnp.zeros((tq, 1), F32), jnp.zeros((tq, dv), F32))
    for s_i in range(n_src):
        k_ref = refs[1 + 2 * s_i]
        v_ref = refs[2 + 2 * s_i]
        nk = k_ref.shape[0]
        tk = min(nk, tk_max)
        if nk == tk:
            carry = step(k_ref[...].astype(BF16), v_ref[...].astype(BF16), carry)
        else:
            def body(c, cr, k_ref=k_ref, v_ref=v_ref, tk=tk):
                k0 = pl.multiple_of(c * tk, tk)
                return step(k_ref[pl.ds(k0, tk), :].astype(BF16), v_ref[pl.ds(k0, tk), :].astype(BF16), cr)
            carry = lax.fori_loop(0, nk // tk, body, carry)
    _, l, acc = carry
    o_ref[...] = acc / l


def _attention(q_arr, q_blk0, dq, sources, dv, n_heads, scale, tq=256, tk_max=512):
    b, nq, _ = q_arr.shape
    tq = min(nq, tq)
    in_specs = [pl.BlockSpec((None, tq, dq), lambda bb, h, i: (bb, i, q_blk0 + h))]
    args = [q_arr]
    for k_arr, k_blk0, v_arr, v_blk0 in sources:
        nk = k_arr.shape[1]
        in_specs.append(pl.BlockSpec((None, nk, dq), lambda bb, h, i, o=k_blk0: (bb, 0, o + h)))
        in_specs.append(pl.BlockSpec((None, nk, dv), lambda bb, h, i, o=v_blk0: (bb, 0, o + h)))
        args += [k_arr, v_arr]
    return pl.pallas_call(
        functools.partial(_attn_kernel, n_src=len(sources), scale=scale, tk_max=tk_max),
        grid=(b, n_heads, nq // tq),
        in_specs=in_specs,
        out_specs=pl.BlockSpec((None, tq, dv), lambda bb, h, i: (bb, i, h)),
        out_shape=jax.ShapeDtypeStruct((b, nq, n_heads * dv), F32),
        compiler_params=_cparams(("parallel", "parallel", "arbitrary"), 48),
        name="attention",
    )(*args)


ATTN_TQ = 2048
ATTN_TK = 512
ATTN_GROUP = 4


def _attn_t_kernel(*refs, n_src):
    q_ref = refs[0]
    o_ref = refs[1 + 2 * n_src]
    q = q_ref[...]
    tq = q.shape[1]
    dv = o_ref.shape[-1]

    def scores(k):
        return jnp.dot(k, q, preferred_element_type=F32)

    def update(s, vt, carry):
        m, l, acc = carry
        m_new = jnp.maximum(m, jnp.max(s, axis=0, keepdims=True))
        p = jnp.exp2(s - m_new)
        alpha = jnp.exp2(m - m_new)
        l = alpha * l + jnp.sum(p, axis=0, keepdims=True)
        acc = alpha * acc + jnp.dot(vt, p.astype(BF16), preferred_element_type=F32)
        return m_new, l, acc

    carry = (jnp.full((1, tq), NEG_BIG, F32), jnp.zeros((1, tq), F32), jnp.zeros((dv, tq), F32))
    for s_i in range(n_src):
        k_ref = refs[1 + 2 * s_i]
        vt_ref = refs[2 + 2 * s_i]
        nk = k_ref.shape[0]
        tk = min(nk, ATTN_TK)
        n_chunks = nk // tk
        if n_chunks == 1:
            carry = update(scores(k_ref[...]), vt_ref[...], carry)
            continue
        group = max(g for g in (ATTN_GROUP, 2, 1) if n_chunks % g == 0)

        def body(c, cr, k_ref=k_ref, vt_ref=vt_ref, tk=tk, group=group):
            starts = [pl.multiple_of((c * group + g) * tk, tk) for g in range(group)]
            ss = [scores(k_ref[pl.ds(k0, tk), :]) for k0 in starts]
            for s, k0 in zip(ss, starts):
                cr = update(s, vt_ref[:, pl.ds(k0, tk)], cr)
            return cr
        carry = lax.fori_loop(0, n_chunks // group, body, carry)
    _, l, acc = carry
    o_ref[...] = (acc / l).T


def _attention_t(q_arr, sources, n_heads, dq, dv):
    b, _, nq = q_arr.shape
    tq = min(nq, ATTN_TQ)
    in_specs = [pl.BlockSpec((None, dq, tq), lambda bb, h, i: (bb, h, i))]
    args = [q_arr]
    for k_arr, vt_arr in sources:
        nk = k_arr.shape[1]
        in_specs.append(pl.BlockSpec((None, nk, dq), lambda bb, h, i: (bb, 0, h)))
        in_specs.append(pl.BlockSpec((None, dv, nk), lambda bb, h, i: (bb, h, 0)))
        args += [k_arr, vt_arr]
    return pl.pallas_call(
        functools.partial(_attn_t_kernel, n_src=len(sources)),
        grid=(b, n_heads, nq // tq),
        in_specs=in_specs,
        out_specs=pl.BlockSpec((None, tq, dv), lambda bb, h, i: (bb, i, h)),
        out_shape=jax.ShapeDtypeStruct((b, nq, n_heads * dv), F32),
        compiler_params=_cparams(("parallel", "parallel", "arbitrary"), 56),
        name="attention_t",
    )(*args)


NAT_GROUP = 8
NAT_SLAB = NAT_GROUP + NAT_ROWS
NAT_HALVES = 2


def _nat_kernel(q_ref, k_ref, v_ref, kc_ref, vc_ref, bias_ref, o_ref, *, n_rows):
    i = pl.program_id(2)
    scale = HEAD_DIM ** -0.5
    nt = (((1,), (1,)), ((), ()))
    u0 = jnp.clip(i * NAT_GROUP - NAT_ROWS // 2, 0, n_rows - NAT_SLAB)
    t0 = pl.multiple_of(u0 * GRID_W, GRID_W)
    ks = k_ref[pl.ds(t0, NAT_SLAB * GRID_W), :].astype(BF16)
    vs = v_ref[pl.ds(t0, NAT_SLAB * GRID_W), :].astype(BF16)
    kc = kc_ref[...].astype(BF16)
    vc = vc_ref[...].astype(BF16)
    hq = q_ref.shape[0] // NAT_HALVES
    scores = []
    for j in range(NAT_HALVES):
        rows = slice(j * hq, (j + 1) * hq)
        q = (q_ref[rows, :].astype(F32) * scale).astype(BF16)
        scores.append((lax.dot_general(q, ks, nt, preferred_element_type=F32) + bias_ref[rows, :],
                       lax.dot_general(q, kc, nt, preferred_element_type=F32)))
    for j in range(NAT_HALVES):
        s_w, s_c = scores[j]
        m = jnp.maximum(jnp.max(s_w, axis=-1, keepdims=True), jnp.max(s_c, axis=-1, keepdims=True))
        p_w = jnp.exp(s_w - m)
        p_c = jnp.exp(s_c - m)
        l = jnp.sum(p_w, axis=-1, keepdims=True) + jnp.sum(p_c, axis=-1, keepdims=True)
        o = (jnp.dot(p_w.astype(BF16), vs, preferred_element_type=F32)
             + jnp.dot(p_c.astype(BF16), vc, preferred_element_type=F32))
        o_ref[j * hq:(j + 1) * hq, :] = o / l


def _nat_bias_table(rpb):
    h = rpb.shape[0]
    col = np.arange(GRID_W)
    c0 = np.clip(col - NAT_COLS // 2, 0, GRID_W - NAT_COLS)
    kc = np.arange(GRID_W)
    col_ok = (kc[None, :] >= c0[:, None]) & (kc[None, :] < c0[:, None] + NAT_COLS)
    dc = kc[None, :] - col[:, None] + NAT_COLS - 1
    sel_c = ((dc[None] == np.arange(2 * NAT_COLS - 1)[:, None, None]) & col_ok[None]).astype(np.float32)
    sel_c = sel_c.reshape(2 * NAT_COLS - 1, GRID_W * GRID_W)
    per_dr = jnp.einsum('hdc,cb->hdb', rpb.astype(F32), sel_c, precision=lax.Precision.HIGHEST)
    per_dr = jnp.where(col_ok.reshape(-1)[None, None], per_dr, NEG_BIG).reshape(h, 2 * NAT_ROWS - 1, GRID_W, GRID_W)
    return pl.pallas_call(
        _nat_table_kernel,
        grid=(h, 3),
        in_specs=[pl.BlockSpec((None, 2 * NAT_ROWS - 1, GRID_W, GRID_W), lambda hh, v: (hh, 0, 0, 0))],
        out_specs=pl.BlockSpec((None, None, NAT_GROUP * GRID_W, NAT_SLAB * GRID_W), lambda hh, v: (hh, v, 0, 0)),
        out_shape=jax.ShapeDtypeStruct((h, 3, NAT_GROUP * GRID_W, NAT_SLAB * GRID_W), F32),
        compiler_params=_cparams(("parallel", "arbitrary"), 32),
        name="nat_bias_table",
    )(per_dr)


def _nat_table_kernel(per_dr_ref, o_ref):
    masked = jnp.full((GRID_W, GRID_W), NEG_BIG, F32)
    for variant, delta in enumerate((0, NAT_ROWS // 2, NAT_ROWS)):
        @pl.when(pl.program_id(1) == variant)
        def _(delta=delta):
            for g in range(NAT_GROUP):
                w0 = min(max(g + delta - NAT_ROWS // 2, 0), NAT_SLAB - NAT_ROWS)
                tiles = [per_dr_ref[u - (g + delta) + NAT_ROWS - 1] if w0 <= u < w0 + NAT_ROWS else masked
                         for u in range(NAT_SLAB)]
                o_ref[g * GRID_W:(g + 1) * GRID_W, :] = jnp.concatenate(tiles, axis=1)


def _nat_latent(p_l, p_c, bias_tbl):
    b, n, _ = p_l.shape
    n_ctx = p_c.shape[1]
    n_rows = n // GRID_W
    assert n_rows >= NAT_SLAB and n_rows % NAT_GROUP == 0
    n_groups = n_rows // NAT_GROUP
    tq = NAT_GROUP * GRID_W
    tk = NAT_SLAB * GRID_W

    def variant(i):
        return jnp.where(i == 0, 0, jnp.where(i == n_groups - 1, 2, 1))

    return pl.pallas_call(
        functools.partial(_nat_kernel, n_rows=n_rows),
        grid=(b, N_HEADS_NAT, n_groups),
        in_specs=[pl.BlockSpec((None, tq, LANE), lambda bb, h, i: (bb, i, COL_NAT_Q + h)),
                  pl.BlockSpec((None, n, LANE), lambda bb, h, i: (bb, 0, COL_NAT_K + h)),
                  pl.BlockSpec((None, n, LANE), lambda bb, h, i: (bb, 0, COL_NAT_V + h)),
                  pl.BlockSpec((None, n_ctx, LANE), lambda bb, h, i: (bb, 0, COL_NAT_K + h)),
                  pl.BlockSpec((None, n_ctx, LANE), lambda bb, h, i: (bb, 0, COL_NAT_V + h)),
                  pl.BlockSpec((None, None, tq, tk), lambda bb, h, i: (h, variant(i), 0, 0))],
        out_specs=pl.BlockSpec((None, tq, LANE), lambda bb, h, i: (bb, i, h)),
        out_shape=jax.ShapeDtypeStruct((b, n, N_HEADS_NAT * HEAD_DIM), F32),
        compiler_params=_cparams(("parallel", "parallel", "arbitrary"), 48),
        name="nat_latent",
    )(p_l, p_l, p_l, p_c, p_c, bias_tbl)


def _ret_consts(logit, reverse):
    c_len = RET_CHUNK
    z = jnp.full((c_len, c_len), -logit, F32)
    lg = -(jnp.maximum(z, 0.0) + jnp.log1p(jnp.exp(-jnp.abs(z))))
    ci = lax.broadcasted_iota(jnp.int32, (c_len, c_len), 0).astype(F32)
    si = lax.broadcasted_iota(jnp.int32, (c_len, c_len), 1).astype(F32)
    if reverse:
        rel = si - ci
        wq = jnp.exp(lg * (c_len - ci))
        wk = jnp.exp(lg * ci)
    else:
        rel = ci - si
        wq = jnp.exp(lg * (ci + 1.0))
        wk = jnp.exp(lg * (c_len - 1.0 - ci))
    dmat = jnp.where(rel >= 0, jnp.exp(lg * jnp.maximum(rel, 0.0)), 0.0)
    return wq, wk, dmat, jnp.exp(lg * c_len)


def _ret_kernel(lf_ref, lb_ref, qf_ref, kf_ref, vf_ref, gf_ref, qb_ref, kb_ref, vb_ref, gb_ref, s0f_ref, s0b_ref,
                yf_ref, yb_ref, sf_ref, sb_ref, c_scr, *, n_chunks):
    h = pl.program_id(1)

    @pl.when(pl.program_id(2) == 0)
    def _():
        sf_ref[...] = s0f_ref[...]
        sb_ref[...] = s0b_ref[...]
        for j, arr in enumerate(_ret_consts(lf_ref[h], False) + _ret_consts(lb_ref[h], True)):
            c_scr[j] = arr

    c_len = RET_CHUNK
    nt = (((1,), (1,)), ((), ()))
    k_scale = HEAD_DIM ** -0.5
    dirs = []
    for d, refs in enumerate(((qf_ref, kf_ref, vf_ref, gf_ref, yf_ref), (qb_ref, kb_ref, vb_ref, gb_ref, yb_ref))):
        order = list(range(n_chunks)) if d == 0 else list(range(n_chunks - 1, -1, -1))
        dirs.append(refs + tuple(c_scr[4 * d + j] for j in range(4)) + (order,))

    units = []
    for step in range(n_chunks):
        for d in range(2):
            q_ref, k_ref, v_ref, _, _, _, wk, dmat, _, order = dirs[d]
            sl = slice(order[step] * c_len, (order[step] + 1) * c_len)
            qb = q_ref[sl, :].astype(BF16)
            kb = k_ref[sl, :]
            vb = v_ref[sl, :].astype(BF16)
            sc = lax.dot_general(qb, kb.astype(BF16), nt, preferred_element_type=F32) * (dmat * k_scale)
            kv = jnp.dot((kb.astype(F32) * (wk * k_scale)).T.astype(BF16), vb, preferred_element_type=F32)
            units.append((d, sl, qb, vb, sc, kv))
    states = [sf_ref[...], sb_ref[...]]
    cross = []
    for (d, sl, qb, vb, sc, kv) in units:
        wq, g_chunk = dirs[d][5], dirs[d][8]
        cross.append(wq * jnp.dot(qb, states[d].astype(BF16), preferred_element_type=F32))
        states[d] = g_chunk * states[d] + kv
    sf_ref[...] = states[0]
    sb_ref[...] = states[1]
    for (d, sl, qb, vb, sc, kv), o_cross in zip(units, cross):
        g_ref, y_ref = dirs[d][3], dirs[d][4]
        o = jnp.dot(sc.astype(BF16), vb, preferred_element_type=F32) + o_cross
        mu = jnp.mean(o, axis=-1, keepdims=True)
        dev = o - mu
        gn = dev * lax.rsqrt(jnp.mean(dev * dev, axis=-1, keepdims=True) + EPS)
        y_ref[sl, :] = _silu(g_ref[sl, :].astype(F32)) * gn


def _retention_scan(p, logit_f, logit_b, s0_f, s0_b):
    b, n, _ = p.shape
    hh = N_HEADS_RET
    tb = min(n, 8 * RET_CHUNK)
    nblk = n // tb

    def fwd(c0):
        return pl.BlockSpec((None, tb, LANE), lambda bb, h, i: (bb, i, c0 + h))

    def bwd(c0):
        return pl.BlockSpec((None, tb, LANE), lambda bb, h, i: (bb, nblk - 1 - i, c0 + h))

    state = pl.BlockSpec((None, None, HEAD_DIM, HEAD_DIM), lambda bb, h, i: (bb, h, 0, 0))
    smem = pl.BlockSpec(memory_space=pltpu.SMEM)
    y_shape = jax.ShapeDtypeStruct((b, n, hh * HEAD_DIM), F32)
    s_shape = jax.ShapeDtypeStruct((b, hh, HEAD_DIM, HEAD_DIM), F32)
    return pl.pallas_call(
        functools.partial(_ret_kernel, n_chunks=tb // RET_CHUNK),
        grid=(b, hh, nblk),
        in_specs=[smem, smem,
                  fwd(COL_RET_Q), fwd(COL_RET_K), fwd(COL_RET_V), fwd(COL_RET_GF),
                  bwd(COL_RET_Q), bwd(COL_RET_K), bwd(COL_RET_V), bwd(COL_RET_GB),
                  state, state],
        out_specs=[fwd(0), bwd(0), state, state],
        out_shape=[y_shape, y_shape, s_shape, s_shape],
        scratch_shapes=[pltpu.VMEM((8, RET_CHUNK, RET_CHUNK), F32)],
        compiler_params=_cparams(("parallel", "parallel", "arbitrary"), 32),
        name="retention",
    )(logit_f.astype(F32), logit_b.astype(F32), p, p, p, p, p, p, p, p, s0_f, s0_b)


def _retention(p_c, p_l, logit_f, logit_b):
    b = p_c.shape[0]
    s0 = jnp.zeros((b, N_HEADS_RET, HEAD_DIM, HEAD_DIM), F32)
    yc_f, yc_b, s_cf, s_cb = _retention_scan(p_c, logit_f, logit_b, s0, s0)
    yl_f, yl_b, _, _ = _retention_scan(p_l, logit_f, logit_b, s_cf, s_cb)
    return (yc_f, yc_b), (yl_f, yl_b)


MERGE_ROWS = 512
MERGE_HALVES = 2


def _merge_kernel(x_ref, ya_ref, yb_ref, yrf_ref, yrb_ref, g_ref, gate_ref, w_ref, g2_ref, sh2_ref, sc2_ref, wr_ref,
                  o_ref, h_ref, aff_ref):
    wa = ya_ref.shape[-1]
    wb = yb_ref.shape[-1]
    hr = x_ref.shape[0] // MERGE_HALVES
    halves = [slice(j * hr, (j + 1) * hr) for j in range(MERGE_HALVES)]
    wr = wr_ref[...]
    w_hi = wr.astype(BF16)
    w_lo = (wr - w_hi.astype(F32)).astype(BF16)
    normed = []
    for rows in halves:
        na = (_rms(ya_ref[rows, :]) * g_ref[:, :wa]).astype(BF16)
        nb = (_rms(yb_ref[rows, :]) * g_ref[:, wa:wa + wb]).astype(BF16)
        nr = (_rms(yrf_ref[rows, :] + yrb_ref[rows, :]) * g_ref[:, wa + wb:]).astype(BF16)
        normed.append((na, nb, nr))
    accs = []
    for na, nb, nr in normed:
        acc = jnp.dot(na, w_ref[:wa, :], preferred_element_type=F32)
        acc = acc + jnp.dot(nb, w_ref[wa:wa + wb, :], preferred_element_type=F32)
        accs.append(acc + jnp.dot(nr, w_ref[wa + wb:, :], preferred_element_type=F32))
    for rows, acc in zip(halves, accs):
        x_new = x_ref[rows, :] + gate_ref[...] * acc
        o_ref[rows, :] = x_new
        h = (_rms(x_new) * g2_ref[...]) * (1.0 + sc2_ref[...]) + sh2_ref[...]
        h_hi = h.astype(BF16)
        h_ref[rows, :] = h_hi
        h_lo = (h - h_hi.astype(F32)).astype(BF16)
        logits = (jnp.dot(h_hi, w_hi, preferred_element_type=F32)
                  + (jnp.dot(h_hi, w_lo, preferred_element_type=F32)
                     + jnp.dot(h_lo, w_hi, preferred_element_type=F32)))
        lane = lax.broadcasted_iota(jnp.int32, logits.shape, 1)
        logits = jnp.where(lane < N_EXPERTS, logits, NEG_BIG)
        e = jnp.exp(logits - jnp.max(logits, axis=-1, keepdims=True))
        aff_ref[rows, :] = e / jnp.sum(e, axis=-1, keepdims=True)


def _merge_out(x, ya, yb, yr, g, gate, w_out, g2, shift2, scale2, w_router_pad):
    b, n, d = x.shape
    yrf, yrb = yr
    tm = min(n, MERGE_ROWS)
    dm = w_out.shape[0]

    def tok(width):
        return pl.BlockSpec((None, tm, width), lambda bb, i: (bb, i, 0))

    row = pl.BlockSpec((1, d), lambda bb, i: (0, 0))
    per_batch = pl.BlockSpec((None, 1, d), lambda bb, i: (bb, 0, 0))
    return pl.pallas_call(
        _merge_kernel,
        grid=(b, n // tm),
        in_specs=[tok(d), tok(ya.shape[-1]), tok(yb.shape[-1]), tok(yrf.shape[-1]), tok(yrb.shape[-1]),
                  pl.BlockSpec((1, dm), lambda bb, i: (0, 0)), per_batch,
                  pl.BlockSpec((dm, d), lambda bb, i: (0, 0)),
                  row, per_batch, per_batch,
                  pl.BlockSpec((d, LANE), lambda bb, i: (0, 0))],
        out_specs=[tok(d), tok(d), tok(LANE)],
        out_shape=[jax.ShapeDtypeStruct((b, n, d), F32), jax.ShapeDtypeStruct((b, n, d), BF16),
                   jax.ShapeDtypeStruct((b, n, LANE), F32)],
        compiler_params=_cparams(("parallel", "parallel"), 58),
        name="merge_out",
    )(x, ya, yb, yrf, yrb, g.reshape(1, dm), gate, w_out, g2.reshape(1, d), shift2, scale2, w_router_pad)


def _ffn_up_kernel(xs_ref, wg_ref, wu_ref, hm_ref, wg_scr, wu_scr):
    @pl.when(pl.program_id(1) == 0)
    def _():
        wg_scr[...] = wg_ref[...].astype(BF16)
        wu_scr[...] = wu_ref[...].astype(BF16)

    xs = xs_ref[...]
    a = jnp.dot(xs, wg_scr[...], preferred_element_type=F32)
    u = jnp.dot(xs, wu_scr[...], preferred_element_type=F32)
    hm_ref[...] = (_silu(a) * u).astype(BF16)


def _ffn_down_kernel(hm_ref, gate_ref, wd_ref, o_ref, wd_scr):
    @pl.when(pl.program_id(1) == 0)
    def _():
        wd_scr[...] = wd_ref[...].astype(BF16)

    o_ref[...] = (jnp.dot(hm_ref[...], wd_scr[...], preferred_element_type=F32) * gate_ref[...]).astype(o_ref.dtype)


def _expert_ffn(xs, gates, w_gate, w_up, w_down, layer):
    e, t, d = xs.shape
    f = w_gate.shape[-1]
    tm = min(t, 256)
    tm_down = min(t, 1024)
    hm = pl.pallas_call(
        _ffn_up_kernel,
        grid=(e, t // tm),
        in_specs=[pl.BlockSpec((None, tm, d), lambda ee, i: (ee, i, 0)),
                  pl.BlockSpec((None, None, d, f), lambda ee, i: (layer, ee, 0, 0)),
                  pl.BlockSpec((None, None, d, f), lambda ee, i: (layer, ee, 0, 0))],
        out_specs=pl.BlockSpec((None, tm, f), lambda ee, i: (ee, i, 0)),
        out_shape=jax.ShapeDtypeStruct((e, t, f), BF16),
        scratch_shapes=[pltpu.VMEM((d, f), BF16), pltpu.VMEM((d, f), BF16)],
        compiler_params=_cparams(("parallel", "arbitrary"), 56),
        name="expert_ffn_up",
    )(xs, w_gate, w_up)
    return pl.pallas_call(
        _ffn_down_kernel,
        grid=(e, t // tm_down),
        in_specs=[pl.BlockSpec((None, tm_down, f), lambda ee, i: (ee, i, 0)),
                  pl.BlockSpec((None, tm_down, 1), lambda ee, i: (ee, i, 0)),
                  pl.BlockSpec((None, None, f, d), lambda ee, i: (layer, ee, 0, 0))],
        out_specs=pl.BlockSpec((None, tm_down, d), lambda ee, i: (ee, i, 0)),
        out_shape=jax.ShapeDtypeStruct((e, t, d), BF16),
        scratch_shapes=[pltpu.VMEM((f, d), BF16)],
        compiler_params=_cparams(("parallel", "arbitrary"), 56),
        name="expert_ffn_down",
    )(hm, gates, w_down)


COMBINE_TOKENS = 256
COMBINE_ROWS = 512


def _combine_kernel(tile_ref, win_ref, flag_ref, x_ref, yp_ref, tok_ref, gate_ref, gn_ref, o_ref, acc_scr, *,
                    final_norm):
    s = pl.program_id(0)
    flags = flag_ref[s]
    t = acc_scr.shape[0]

    @pl.when((flags & 1) != 0)
    def _():
        acc_scr[...] = jnp.zeros(acc_scr.shape, F32)

    @pl.when((flags & 2) != 0)
    def _():
        tcol = tile_ref[s] * t + lax.broadcasted_iota(jnp.int32, (t, 1), 0)
        onehot = jnp.where(tcol == tok_ref[...], 1.0, 0.0).astype(BF16)
        acc_scr[...] += jnp.dot(onehot, yp_ref[...], preferred_element_type=F32)

    @pl.when((flags & 4) != 0)
    def _():
        y = x_ref[...] + gate_ref[...] * acc_scr[...]
        if final_norm:
            y = _rms(y) * gn_ref[...]
        o_ref[...] = y


def _moe_combine(x, y_rows, tok_rows, gate, norm_g, final_norm):
    b, n, d = x.shape
    r = y_rows.shape[0]
    t = min(COMBINE_TOKENS, n)
    w = COMBINE_ROWS
    ntiles = b * n // t
    nwin_total = r // w
    tiles_per_batch = n // t
    n_steps = nwin_total + 2 * ntiles
    i32 = jnp.int32
    tok_sorted, order = lax.sort_key_val(tok_rows.astype(i32), jnp.arange(r, dtype=i32))
    yp = jnp.take(y_rows, order, axis=0, mode="clip")
    edges = jnp.arange(ntiles + 1, dtype=i32) * t
    bounds = jnp.sum((tok_sorted[None, :] < edges[:, None]).astype(i32), axis=1)
    w_lo = jnp.minimum(bounds[:-1] // w, nwin_total - 1)
    w_hi = jnp.maximum((bounds[1:] + w - 1) // w, w_lo + 1)
    nwin = w_hi - w_lo
    step0 = jnp.cumsum(nwin) - nwin
    total = jnp.sum(nwin)
    s = jnp.arange(n_steps, dtype=i32)
    tile_s = jnp.sum((s[:, None] >= step0[None, :]).astype(i32), axis=1) - 1
    sel = tile_s[:, None] == jnp.arange(ntiles, dtype=i32)[None, :]

    def of_tile(v):
        return jnp.sum(jnp.where(sel, v[None, :], 0), axis=1)

    step0_s, w_lo_s, nwin_s = of_tile(step0), of_tile(w_lo), of_tile(nwin)
    win_s = jnp.minimum(w_lo_s + (s - step0_s), w_lo_s + nwin_s - 1)
    valid = s < total
    first = valid & (s == step0_s)
    last = valid & (s == step0_s + nwin_s - 1)
    flags = first.astype(i32) + 2 * valid.astype(i32) + 4 * last.astype(i32)
    out = pl.pallas_call(
        functools.partial(_combine_kernel, final_norm=final_norm),
        grid_spec=pltpu.PrefetchScalarGridSpec(
            num_scalar_prefetch=3,
            grid=(n_steps,),
            in_specs=[pl.BlockSpec((t, d), lambda s, tl, wn, fl: (tl[s], 0)),
                      pl.BlockSpec((w, d), lambda s, tl, wn, fl: (wn[s], 0)),
                      pl.BlockSpec((None, 1, w), lambda s, tl, wn, fl: (wn[s], 0, 0)),
                      pl.BlockSpec((None, 1, d), lambda s, tl, wn, fl: (tl[s] // tiles_per_batch, 0, 0)),
                      pl.BlockSpec((1, d), lambda s, tl, wn, fl: (0, 0))],
            out_specs=pl.BlockSpec((t, d), lambda s, tl, wn, fl: (tl[s], 0)),
            scratch_shapes=[pltpu.VMEM((t, d), F32)]),
        out_shape=jax.ShapeDtypeStruct((b * n, d), F32),
        compiler_params=_cparams(("arbitrary",), 40),
        name="moe_combine",
    )(tile_s, win_s, flags, x.reshape(b * n, d), yp, tok_sorted.reshape(nwin_total, 1, w), gate,
      norm_g.reshape(1, d))
    return out.reshape(b, n, d)


def _ec_moe(x, h, aff, gate2, experts, norm_g, final_norm):
    b, n, d = x.shape
    cap = EC_CAPACITY * n // N_EXPERTS
    gate, idx = lax.top_k(jnp.swapaxes(aff[..., :N_EXPERTS], 1, 2), cap)
    flat = (idx + (jnp.arange(b, dtype=idx.dtype) * n)[:, None, None]).transpose(1, 0, 2).reshape(-1)
    xs = jnp.take(h.reshape(b * n, d), flat, axis=0, mode="clip").reshape(N_EXPERTS, b * cap, d)
    gates = gate.transpose(1, 0, 2).reshape(N_EXPERTS, b * cap, 1)
    y = _expert_ffn(xs, gates, *experts)
    return _moe_combine(x, y.reshape(-1, d), flat, gate2, norm_g, final_norm)


_ROPE_SRC = np.concatenate([np.arange(16, 32), np.arange(0, 16), np.arange(48, 64), np.arange(32, 48)])
_ROPE_SIGN = np.concatenate([-np.ones(16), np.ones(16), -np.ones(16), np.ones(16)]).astype(np.float32)


def _rope_perm_cols(w):
    return w[:, _ROPE_SRC] * _ROPE_SIGN


def _pad_w_in(w_in):
    c_kr = MLA_Q_LORA + MLA_KV_LORA
    kr = w_in[:, c_kr:c_kr + MLA_ROPE_DIM]
    return jnp.concatenate([w_in[:, :c_kr + MLA_ROPE_DIM], _rope_perm_cols(kr), w_in[:, c_kr + MLA_ROPE_DIM:]],
                           axis=1).astype(BF16)


def _expand_w_uq(w_uq):
    w = w_uq.reshape(MLA_Q_LORA, N_HEADS_MLA, HEAD_DIM + MLA_ROPE_DIM)
    nope = w[:, :, :HEAD_DIM]
    rope = w[:, :, HEAD_DIM:]
    partner = rope[:, :, _ROPE_SRC] * _ROPE_SIGN
    return jnp.concatenate([nope, rope, partner], axis=-1).reshape(MLA_Q_LORA, N_HEADS_MLA * MLA_QK_PAD).astype(BF16)


def _rope_tables(n):
    t = jnp.arange(n)
    n_freq = MLA_ROPE_DIM // 4
    inv = ROPE_BASE ** (-jnp.arange(n_freq, dtype=F32) / n_freq)
    ang_r = (t // GRID_W).astype(F32)[:, None] * inv
    ang_c = (t % GRID_W).astype(F32)[:, None] * inv
    zeros = jnp.zeros((n, LANE - MLA_ROPE_DIM), F32)
    cos = jnp.concatenate([jnp.cos(ang_r), jnp.cos(ang_r), jnp.cos(ang_c), jnp.cos(ang_c), zeros], axis=1)
    sin = jnp.concatenate([jnp.sin(ang_r), jnp.sin(ang_r), jnp.sin(ang_c), jnp.sin(ang_c), zeros], axis=1)
    return cos, sin


def _identity_tables(n):
    ones = jnp.ones((n, MLA_ROPE_DIM), F32)
    zeros = jnp.zeros((n, LANE - MLA_ROPE_DIM), F32)
    return jnp.concatenate([ones, zeros], axis=1), jnp.zeros((n, LANE), F32)


def _layer(x_l, x_c, mod, lp, last, final_g):
    b, n, d = x_l.shape
    n_ctx = x_c.shape[1]

    def chunk(k, ctx):
        rows = jnp.broadcast_to(mod[b, k * d:(k + 1) * d], (b, d)) if ctx else mod[:b, k * d:(k + 1) * d]
        return rows.reshape(b, 1, d)

    w_in = _pad_w_in(lp['w_in'])
    p_l = _norm_proj(x_l, lp['norm1_g'], chunk(0, False), chunk(1, False), w_in)
    p_c = _norm_proj(x_c, lp['norm1_g'], chunk(0, True), chunk(1, True), w_in)

    cos_l, sin_l = _rope_tables(n)
    cos_c, sin_c = _identity_tables(n_ctx)
    w_uk = lp['mla_w_uk'].reshape(MLA_KV_LORA, -1).astype(BF16)
    w_uv = lp['mla_w_uv'].reshape(MLA_KV_LORA, -1).astype(BF16)
    w_q = _expand_w_uq(lp['mla_w_uq'])
    k_l, v_l = _mla_kv_prep(p_l, lp['mla_kv_norm_g'], w_uk, w_uv, cos_l, sin_l)
    k_c, v_c = _mla_kv_prep(p_c, lp['mla_kv_norm_g'], w_uk, w_uv, cos_c, sin_c)
    q_l = _mla_q_prep(p_l, lp['mla_q_norm_g'], w_q, cos_l, sin_l)
    a_l = _attention_t(q_l, [(k_c, v_c), (k_l, v_l)], N_HEADS_MLA, MLA_QK_PAD, HEAD_DIM)

    b_l = _nat_latent(p_l, p_c, _nat_bias_table(lp['nat_rpb']))
    r_c, r_l = _retention(p_c, p_l, lp['ret_decay_f'], lp['ret_decay_b'])

    w_out = lp['w_out'].astype(BF16)
    w_router = jnp.zeros((d, LANE), F32).at[:, :N_EXPERTS].set(lp['w_router'])

    x_l, h_l, aff_l = _merge_out(x_l, a_l, b_l, r_l, lp['out_norm_g'], chunk(2, False), w_out,
                                 lp['norm2_g'], chunk(3, False), chunk(4, False), w_router)
    x_l = _ec_moe(x_l, h_l, aff_l, chunk(5, False), lp['experts'], final_g, last)
    if last:
        return x_l, None

    q_c = _mla_q_prep(p_c, lp['mla_q_norm_g'], w_q, cos_c, sin_c)
    a_c = _attention_t(q_c, [(k_c, v_c)], N_HEADS_MLA, MLA_QK_PAD, HEAD_DIM)
    b_c = _attention(p_c, COL_NAT_Q, HEAD_DIM, [(p_c, COL_NAT_K, p_c, COL_NAT_V)], HEAD_DIM, N_HEADS_NAT,
                     HEAD_DIM ** -0.5)
    x_c, h_c, aff_c = _merge_out(x_c, a_c, b_c, r_c, lp['out_norm_g'], chunk(2, True), w_out,
                                 lp['norm2_g'], chunk(3, True), chunk(4, True), w_router)
    x_c = _ec_moe(x_c, h_c, aff_c, chunk(5, True), lp['experts'], final_g, False)
    return x_l, x_c


def kernel(x, c, ctx, c_ctx, w_mod, b_mod, norm1_g, w_in, mla_q_norm_g, mla_kv_norm_g, mla_w_uq, mla_w_uk,
           mla_w_uv, nat_rpb, ret_decay_f, ret_decay_b, out_norm_g, w_out, norm2_g, w_router, w_gate, w_up,
           w_down, final_norm_g):
    depth = w_mod.shape[0]
    cvecs = jnp.concatenate([c, c_ctx[None, :]], axis=0)
    x_l, x_c = x, ctx
    for i in range(depth):
        lp = {
            'norm1_g': norm1_g[i], 'w_in': w_in[i], 'mla_q_norm_g': mla_q_norm_g[i],
            'mla_kv_norm_g': mla_kv_norm_g[i], 'mla_w_uq': mla_w_uq[i], 'mla_w_uk': mla_w_uk[i],
            'mla_w_uv': mla_w_uv[i], 'nat_rpb': nat_rpb[i], 'ret_decay_f': ret_decay_f[i],
            'ret_decay_b': ret_decay_b[i], 'out_norm_g': out_norm_g[i], 'w_out': w_out[i],
            'norm2_g': norm2_g[i], 'w_router': w_router[i], 'experts': (w_gate, w_up, w_down, i),
        }
        mod = _modulation(cvecs, w_mod[i], b_mod[i])
        x_l, x_c = _layer(x_l, x_c, mod, lp, i == depth - 1, final_norm_g)
    return x_l
```

```python
import functools

import numpy as np
import jax
import jax.numpy as jnp
from jax import lax
from jax.experimental import pallas as pl
from jax.experimental.pallas import tpu as pltpu

F32 = jnp.float32
BF16 = jnp.bfloat16

GRID_W = 64
HEAD_DIM = 128
N_HEADS_MLA = 8
N_HEADS_NAT = 4
N_HEADS_RET = 4
MLA_ROPE_DIM = 64
MLA_Q_LORA = 512
MLA_KV_LORA = 256
MLA_SCALE = (HEAD_DIM + MLA_ROPE_DIM) ** -0.5
LOG2E = 1.4426950408889634
MLA_Q_SCALE = MLA_SCALE * LOG2E
MLA_QK_PAD = 256
NAT_ROWS = 8
NAT_COLS = 16
RET_CHUNK = 128
N_EXPERTS = 16
EC_CAPACITY = 2
ROPE_BASE = 10000.0
EPS = 1e-6
NEG_BIG = -1e30
LANE = 128

COL_CQ = 0
COL_CKV = 4
COL_KR = 6
COL_NAT_Q = 7
COL_NAT_K = 11
COL_NAT_V = 15
COL_RET_Q = 19
COL_RET_K = 23
COL_RET_V = 27
COL_RET_GF = 31
COL_RET_GB = 35
IN_COLS_PAD = 39 * LANE


def _cparams(sem, vmem_mib):
    return pltpu.CompilerParams(dimension_semantics=sem, vmem_limit_bytes=vmem_mib * 1024 * 1024)


def _silu(a):
    return a * (1.0 / (1.0 + jnp.exp(-a)))


def _rms(x):
    return x * lax.rsqrt(jnp.mean(x * x, axis=-1, keepdims=True) + EPS)


def _mod_kernel(ct_ref, w_ref, b_ref, o_ref, a_scr, *, n_rows, k_chunk):
    ct = ct_ref[...]
    a_scr[...] = _silu(ct)
    d = w_ref.shape[0]
    tn = w_ref.shape[1]

    def body(kc, accs):
        k0 = pl.multiple_of(kc * k_chunk, k_chunk)
        wblk = w_ref[pl.ds(k0, k_chunk), :]
        out = []
        for r in range(n_rows):
            col = a_scr[pl.ds(k0, k_chunk), r:r + 1]
            out.append(accs[r] + jnp.sum((col * wblk).reshape(k_chunk // 8, 8, tn), axis=0))
        return tuple(out)

    accs = lax.fori_loop(0, d // k_chunk, body, tuple(jnp.zeros((8, tn), F32) for _ in range(n_rows)))
    o_ref[...] = jnp.zeros(o_ref.shape, F32)
    for r in range(n_rows):
        o_ref[r:r + 1, :] = jnp.sum(accs[r], axis=0, keepdims=True) + b_ref[...]


def _modulation(cvecs, w_mod, b_mod):
    n_rows, d = cvecs.shape
    n_out = w_mod.shape[1]
    tn = 1024
    ct = jnp.zeros((d, 8), F32).at[:, :n_rows].set(cvecs.T)
    return pl.pallas_call(
        functools.partial(_mod_kernel, n_rows=n_rows, k_chunk=64),
        grid=(n_out // tn,),
        in_specs=[pl.BlockSpec((d, 8), lambda j: (0, 0)),
                  pl.BlockSpec((d, tn), lambda j: (0, j)),
                  pl.BlockSpec((1, tn), lambda j: (0, j))],
        out_specs=pl.BlockSpec((8, tn), lambda j: (0, j)),
        out_shape=jax.ShapeDtypeStruct((8, n_out), F32),
        scratch_shapes=[pltpu.VMEM((d, 8), F32)],
        compiler_params=_cparams(("parallel",), 40),
        name="modulation",
    )(ct, w_mod, b_mod.reshape(1, n_out))


NORM_PROJ_PARTS = 2


def _norm_proj_kernel(x_ref, g_ref, sh_ref, sc_ref, w_ref, o_ref, h_scr):
    j = pl.program_id(2)
    pr = x_ref.shape[0] // NORM_PROJ_PARTS
    parts = [slice(k * pr, (k + 1) * pr) for k in range(NORM_PROJ_PARTS)]

    @pl.when(j == 0)
    def _():
        hs = []
        for rows in parts:
            y = _rms(x_ref[rows, :]) * g_ref[...]
            h = (y * (1.0 + sc_ref[...]) + sh_ref[...]).astype(BF16)
            h_scr[rows, :] = h
            hs.append(h)
        for rows, h in zip(parts, hs):
            o_ref[rows, :] = jnp.dot(h, w_ref[...], preferred_element_type=F32).astype(o_ref.dtype)

    @pl.when(j != 0)
    def _():
        o_ref[...] = jnp.dot(h_scr[...], w_ref[...], preferred_element_type=F32).astype(o_ref.dtype)


def _norm_proj(x, g, shift, scale, w):
    b, n, d = x.shape
    c = w.shape[1]
    tm = min(n, 1024)
    tn = c // 3 if (c % (3 * LANE) == 0) else c
    return pl.pallas_call(
        _norm_proj_kernel,
        grid=(b, n // tm, c // tn),
        in_specs=[pl.BlockSpec((None, tm, d), lambda bb, i, j: (bb, i, 0)),
                  pl.BlockSpec((1, d), lambda bb, i, j: (0, 0)),
                  pl.BlockSpec((None, 1, d), lambda bb, i, j: (bb, 0, 0)),
                  pl.BlockSpec((None, 1, d), lambda bb, i, j: (bb, 0, 0)),
                  pl.BlockSpec((d, tn), lambda bb, i, j: (0, j))],
        out_specs=pl.BlockSpec((None, tm, tn), lambda bb, i, j: (bb, i, j)),
        out_shape=jax.ShapeDtypeStruct((b, n, c), BF16),
        scratch_shapes=[pltpu.VMEM((tm, d), BF16)],
        compiler_params=_cparams(("parallel", "parallel", "arbitrary"), 56),
        name="norm_proj",
    )(x, g.reshape(1, d), shift, scale, w)


def _rope_rotate(r, cos, sin):
    return r * cos + pltpu.roll(r, 64, 1) * sin


def _mla_q_kernel(cq_ref, g_ref, wt_ref, cos_ref, sin_ref, o_ref, *, n_heads):
    cqn_t = (_rms(cq_ref[...].astype(F32)) * g_ref[...]).T.astype(BF16)
    qe = jnp.dot(wt_ref[...], cqn_t, preferred_element_type=F32)
    cos = cos_ref[...]
    sin = sin_ref[...]
    for h in range(n_heads):
        c0 = h * MLA_QK_PAD
        o_ref[c0:c0 + LANE, :] = (qe[c0:c0 + LANE, :] * MLA_Q_SCALE).astype(BF16)
        r = qe[c0 + LANE:c0 + 2 * LANE, :]
        swapped = jnp.concatenate([r[MLA_ROPE_DIM:, :], r[:MLA_ROPE_DIM, :]], axis=0)
        o_ref[c0 + LANE:c0 + 2 * LANE, :] = ((r * cos + swapped * sin) * MLA_Q_SCALE).astype(BF16)


def _mla_q_prep(p, g, w_q_t, cos_t, sin_t):
    b, n, _ = p.shape
    tm = min(n, 512)
    cw = N_HEADS_MLA * MLA_QK_PAD
    return pl.pallas_call(
        functools.partial(_mla_q_kernel, n_heads=N_HEADS_MLA),
        grid=(b, n // tm),
        in_specs=[pl.BlockSpec((None, tm, MLA_Q_LORA), lambda bb, i: (bb, i, COL_CQ * LANE // MLA_Q_LORA)),
                  pl.BlockSpec((1, MLA_Q_LORA), lambda bb, i: (0, 0)),
                  pl.BlockSpec((cw, MLA_Q_LORA), lambda bb, i: (0, 0)),
                  pl.BlockSpec((LANE, tm), lambda bb, i: (0, i)),
                  pl.BlockSpec((LANE, tm), lambda bb, i: (0, i))],
        out_specs=pl.BlockSpec((None, cw, tm), lambda bb, i: (bb, 0, i)),
        out_shape=jax.ShapeDtypeStruct((b, cw, n), BF16),
        compiler_params=_cparams(("parallel", "parallel"), 40),
        name="mla_q_prep",
    )(p, g.reshape(1, MLA_Q_LORA), w_q_t, cos_t, sin_t)


def _mla_kv_kernel(ckv_ref, kr_ref, g_ref, wuk_ref, wuvt_ref, cos_ref, sin_ref, k_ref, v_ref, *, n_heads):
    ckvn = _rms(ckv_ref[...].astype(F32)) * g_ref[...]
    kn = jnp.dot(ckvn.astype(BF16), wuk_ref[...], preferred_element_type=F32)
    v_ref[...] = jnp.dot(wuvt_ref[...], ckvn.T.astype(BF16), preferred_element_type=F32).astype(BF16)
    rot = _rope_rotate(kr_ref[...].astype(F32), cos_ref[...], sin_ref[...]).astype(BF16)
    for h in range(n_heads):
        c0 = h * MLA_QK_PAD
        k_ref[:, c0:c0 + LANE] = kn[:, h * LANE:(h + 1) * LANE].astype(BF16)
        k_ref[:, c0 + LANE:c0 + 2 * LANE] = rot


def _mla_kv_prep(p, g, w_uk, w_uv_t, cos, sin):
    b, n, _ = p.shape
    tm = min(n, 512)
    kw = N_HEADS_MLA * MLA_QK_PAD
    vw = N_HEADS_MLA * HEAD_DIM
    return pl.pallas_call(
        functools.partial(_mla_kv_kernel, n_heads=N_HEADS_MLA),
        grid=(b, n // tm),
        in_specs=[pl.BlockSpec((None, tm, MLA_KV_LORA), lambda bb, i: (bb, i, COL_CKV * LANE // MLA_KV_LORA)),
                  pl.BlockSpec((None, tm, LANE), lambda bb, i: (bb, i, COL_KR)),
                  pl.BlockSpec((1, MLA_KV_LORA), lambda bb, i: (0, 0)),
                  pl.BlockSpec((MLA_KV_LORA, vw), lambda bb, i: (0, 0)),
                  pl.BlockSpec((vw, MLA_KV_LORA), lambda bb, i: (0, 0)),
                  pl.BlockSpec((tm, LANE), lambda bb, i: (i, 0)),
                  pl.BlockSpec((tm, LANE), lambda bb, i: (i, 0))],
        out_specs=[pl.BlockSpec((None, tm, kw), lambda bb, i: (bb, i, 0)),
                   pl.BlockSpec((None, vw, tm), lambda bb, i: (bb, 0, i))],
        out_shape=[jax.ShapeDtypeStruct((b, n, kw), BF16), jax.ShapeDtypeStruct((b, vw, n), BF16)],
        compiler_params=_cparams(("parallel", "parallel"), 40),
        name="mla_kv_prep",
    )(p, p, g.reshape(1, MLA_KV_LORA), w_uk, w_uv_t, cos, sin)


def _attn_kernel(*refs, n_src, scale, tk_max):
    q_ref = refs[0]
    o_ref = refs[1 + 2 * n_src]
    q = q_ref[...]
    if scale != 1.0:
        q = q.astype(F32) * scale
    q = q.astype(BF16)
    tq = q.shape[0]
    dv = o_ref.shape[-1]

    def step(k, v, carry):
        m, l, acc = carry
        s = lax.dot_general(q, k, (((1,), (1,)), ((), ())), preferred_element_type=F32)
        m_new = jnp.maximum(m, jnp.max(s, axis=-1, keepdims=True))
        p = jnp.exp(s - m_new)
        alpha = jnp.exp(m - m_new)
        l = alpha * l + jnp.sum(p, axis=-1, keepdims=True)
        acc = alpha * acc + jnp.dot(p.astype(BF16), v, preferred_element_type=F32)
        return m_new, l, acc

    carry = (jnp.full((tq, 1), NEG_BIG, F32), jnp.zeros((tq, 1), F32), jnp.zeros((tq, dv), F32))
    for s_i in range(n_src):
        k_ref = refs[1 + 2 * s_i]
        v_ref = refs[2 + 2 * s_i]
        nk = k_ref.shape[0]
        tk = min(nk, tk_max)
        if nk == tk:
            carry = step(k_ref[...].astype(BF16), v_ref[...].astype(BF16), carry)
        else:
            def body(c, cr, k_ref=k_ref, v_ref=v_ref, tk=tk):
                k0 = pl.multiple_of(c * tk, tk)
                return step(k_ref[pl.ds(k0, tk), :].astype(BF16), v_ref[pl.ds(k0, tk), :].astype(BF16), cr)
            carry = lax.fori_loop(0, nk // tk, body, carry)
    _, l, acc = carry
    o_ref[...] = acc / l


def _attention(q_arr, q_blk0, dq, sources, dv, n_heads, scale, tq=256, tk_max=512):
    b, nq, _ = q_arr.shape
    tq = min(nq, tq)
    in_specs = [pl.BlockSpec((None, tq, dq), lambda bb, h, i: (bb, i, q_blk0 + h))]
    args = [q_arr]
    for k_arr, k_blk0, v_arr, v_blk0 in sources:
        nk = k_arr.shape[1]
        in_specs.append(pl.BlockSpec((None, nk, dq), lambda bb, h, i, o=k_blk0: (bb, 0, o + h)))
        in_specs.append(pl.BlockSpec((None, nk, dv), lambda bb, h, i, o=v_blk0: (bb, 0, o + h)))
        args += [k_arr, v_arr]
    return pl.pallas_call(
        functools.partial(_attn_kernel, n_src=len(sources), scale=scale, tk_max=tk_max),
        grid=(b, n_heads, nq // tq),
        in_specs=in_specs,
        out_specs=pl.BlockSpec((None, tq, dv), lambda bb, h, i: (bb, i, h)),
        out_shape=jax.ShapeDtypeStruct((b, nq, n_heads * dv), F32),
        compiler_params=_cparams(("parallel", "parallel", "arbitrary"), 48),
        name="attention",
    )(*args)


ATTN_TQ = 2048
ATTN_TK = 512
ATTN_GROUP = 4


def _attn_t_kernel(*refs, n_src):
    q_ref = refs[0]
    o_ref = refs[1 + 2 * n_src]
    q = q_ref[...]
    tq = q.shape[1]
    dv = o_ref.shape[-1]

    def scores(k):
        return jnp.dot(k, q, preferred_element_type=F32)

    def update(s, vt, carry):
        m, l, acc = carry
        m_new = jnp.maximum(m, jnp.max(s, axis=0, keepdims=True))
        p = jnp.exp2(s - m_new)
        alpha = jnp.exp2(m - m_new)
        l = alpha * l + jnp.sum(p, axis=0, keepdims=True)
        acc = alpha * acc + jnp.dot(vt, p.astype(BF16), preferred_element_type=F32)
        return m_new, l, acc

    carry = (jnp.full((1, tq), NEG_BIG, F32), jnp.zeros((1, tq), F32), jnp.zeros((dv, tq), F32))
    for s_i in range(n_src):
        k_ref = refs[1 + 2 * s_i]
        vt_ref = refs[2 + 2 * s_i]
        nk = k_ref.shape[0]
        tk = min(nk, ATTN_TK)
        n_chunks = nk // tk
        if n_chunks == 1:
            carry = update(scores(k_ref[...]), vt_ref[...], carry)
            continue
        group = max(g for g in (ATTN_GROUP, 2, 1) if n_chunks % g == 0)

        def body(c, cr, k_ref=k_ref, vt_ref=vt_ref, tk=tk, group=group):
            starts = [pl.multiple_of((c * group + g) * tk, tk) for g in range(group)]
            ss = [scores(k_ref[pl.ds(k0, tk), :]) for k0 in starts]
            for s, k0 in zip(ss, starts):
                cr = update(s, vt_ref[:, pl.ds(k0, tk)], cr)
            return cr
        carry = lax.fori_loop(0, n_chunks // group, body, carry)
    _, l, acc = carry
    o_ref[...] = (acc / l).T


def _attention_t(q_arr, sources, n_heads, dq, dv):
    b, _, nq = q_arr.shape
    tq = min(nq, ATTN_TQ)
    in_specs = [pl.BlockSpec((None, dq, tq), lambda bb, h, i: (bb, h, i))]
    args = [q_arr]
    for k_arr, vt_arr in sources:
        nk = k_arr.shape[1]
        in_specs.append(pl.BlockSpec((None, nk, dq), lambda bb, h, i: (bb, 0, h)))
        in_specs.append(pl.BlockSpec((None, dv, nk), lambda bb, h, i: (bb, h, 0)))
        args += [k_arr, vt_arr]
    return pl.pallas_call(
        functools.partial(_attn_t_kernel, n_src=len(sources)),
        grid=(b, n_heads, nq // tq),
        in_specs=in_specs,
        out_specs=pl.BlockSpec((None, tq, dv), lambda bb, h, i: (bb, i, h)),
        out_shape=jax.ShapeDtypeStruct((b, nq, n_heads * dv), F32),
        compiler_params=_cparams(("parallel", "parallel", "arbitrary"), 56),
        name="attention_t",
    )(*args)


NAT_GROUP = 8
NAT_SLAB = NAT_GROUP + NAT_ROWS
NAT_HALVES = 2


def _nat_kernel(q_ref, k_ref, v_ref, kc_ref, vc_ref, bias_ref, o_ref, *, n_rows):
    i = pl.program_id(2)
    scale = HEAD_DIM ** -0.5
    nt = (((1,), (1,)), ((), ()))
    u0 = jnp.clip(i * NAT_GROUP - NAT_ROWS // 2, 0, n_rows - NAT_SLAB)
    t0 = pl.multiple_of(u0 * GRID_W, GRID_W)
    ks = k_ref[pl.ds(t0, NAT_SLAB * GRID_W), :].astype(BF16)
    vs = v_ref[pl.ds(t0, NAT_SLAB * GRID_W), :].astype(BF16)
    kc = kc_ref[...].astype(BF16)
    vc = vc_ref[...].astype(BF16)
    hq = q_ref.shape[0] // NAT_HALVES
    scores = []
    for j in range(NAT_HALVES):
        rows = slice(j * hq, (j + 1) * hq)
        q = (q_ref[rows, :].astype(F32) * scale).astype(BF16)
        scores.append((lax.dot_general(q, ks, nt, preferred_element_type=F32) + bias_ref[rows, :],
                       lax.dot_general(q, kc, nt, preferred_element_type=F32)))
    for j in range(NAT_HALVES):
        s_w, s_c = scores[j]
        m = jnp.maximum(jnp.max(s_w, axis=-1, keepdims=True), jnp.max(s_c, axis=-1, keepdims=True))
        p_w = jnp.exp(s_w - m)
        p_c = jnp.exp(s_c - m)
        l = jnp.sum(p_w, axis=-1, keepdims=True) + jnp.sum(p_c, axis=-1, keepdims=True)
        o = (jnp.dot(p_w.astype(BF16), vs, preferred_element_type=F32)
             + jnp.dot(p_c.astype(BF16), vc, preferred_element_type=F32))
        o_ref[j * hq:(j + 1) * hq, :] = o / l


def _nat_bias_table(rpb):
    h = rpb.shape[0]
    col = np.arange(GRID_W)
    c0 = np.clip(col - NAT_COLS // 2, 0, GRID_W - NAT_COLS)
    kc = np.arange(GRID_W)
    col_ok = (kc[None, :] >= c0[:, None]) & (kc[None, :] < c0[:, None] + NAT_COLS)
    dc = kc[None, :] - col[:, None] + NAT_COLS - 1
    sel_c = ((dc[None] == np.arange(2 * NAT_COLS - 1)[:, None, None]) & col_ok[None]).astype(np.float32)
    sel_c = sel_c.reshape(2 * NAT_COLS - 1, GRID_W * GRID_W)
    per_dr = jnp.einsum('hdc,cb->hdb', rpb.astype(F32), sel_c, precision=lax.Precision.HIGHEST)
    per_dr = jnp.where(col_ok.reshape(-1)[None, None], per_dr, NEG_BIG).reshape(h, 2 * NAT_ROWS - 1, GRID_W, GRID_W)
    return pl.pallas_call(
        _nat_table_kernel,
        grid=(h, 3),
        in_specs=[pl.BlockSpec((None, 2 * NAT_ROWS - 1, GRID_W, GRID_W), lambda hh, v: (hh, 0, 0, 0))],
        out_specs=pl.BlockSpec((None, None, NAT_GROUP * GRID_W, NAT_SLAB * GRID_W), lambda hh, v: (hh, v, 0, 0)),
        out_shape=jax.ShapeDtypeStruct((h, 3, NAT_GROUP * GRID_W, NAT_SLAB * GRID_W), F32),
        compiler_params=_cparams(("parallel", "arbitrary"), 32),
        name="nat_bias_table",
    )(per_dr)


def _nat_table_kernel(per_dr_ref, o_ref):
    masked = jnp.full((GRID_W, GRID_W), NEG_BIG, F32)
    for variant, delta in enumerate((0, NAT_ROWS // 2, NAT_ROWS)):
        @pl.when(pl.program_id(1) == variant)
        def _(delta=delta):
            for g in range(NAT_GROUP):
                w0 = min(max(g + delta - NAT_ROWS // 2, 0), NAT_SLAB - NAT_ROWS)
                tiles = [per_dr_ref[u - (g + delta) + NAT_ROWS - 1] if w0 <= u < w0 + NAT_ROWS else masked
                         for u in range(NAT_SLAB)]
                o_ref[g * GRID_W:(g + 1) * GRID_W, :] = jnp.concatenate(tiles, axis=1)


def _nat_latent(p_l, p_c, bias_tbl):
    b, n, _ = p_l.shape
    n_ctx = p_c.shape[1]
    n_rows = n // GRID_W
    assert n_rows >= NAT_SLAB and n_rows % NAT_GROUP == 0
    n_groups = n_rows // NAT_GROUP
    tq = NAT_GROUP * GRID_W
    tk = NAT_SLAB * GRID_W

    def variant(i):
        return jnp.where(i == 0, 0, jnp.where(i == n_groups - 1, 2, 1))

    return pl.pallas_call(
        functools.partial(_nat_kernel, n_rows=n_rows),
        grid=(b, N_HEADS_NAT, n_groups),
        in_specs=[pl.BlockSpec((None, tq, LANE), lambda bb, h, i: (bb, i, COL_NAT_Q + h)),
                  pl.BlockSpec((None, n, LANE), lambda bb, h, i: (bb, 0, COL_NAT_K + h)),
                  pl.BlockSpec((None, n, LANE), lambda bb, h, i: (bb, 0, COL_NAT_V + h)),
                  pl.BlockSpec((None, n_ctx, LANE), lambda bb, h, i: (bb, 0, COL_NAT_K + h)),
                  pl.BlockSpec((None, n_ctx, LANE), lambda bb, h, i: (bb, 0, COL_NAT_V + h)),
                  pl.BlockSpec((None, None, tq, tk), lambda bb, h, i: (h, variant(i), 0, 0))],
        out_specs=pl.BlockSpec((None, tq, LANE), lambda bb, h, i: (bb, i, h)),
        out_shape=jax.ShapeDtypeStruct((b, n, N_HEADS_NAT * HEAD_DIM), F32),
        compiler_params=_cparams(("parallel", "parallel", "arbitrary"), 48),
        name="nat_latent",
    )(p_l, p_l, p_l, p_c, p_c, bias_tbl)


def _ret_consts(logit, reverse):
    c_len = RET_CHUNK
    z = jnp.full((c_len, c_len), -logit, F32)
    lg = -(jnp.maximum(z, 0.0) + jnp.log1p(jnp.exp(-jnp.abs(z))))
    ci = lax.broadcasted_iota(jnp.int32, (c_len, c_len), 0).astype(F32)
    si = lax.broadcasted_iota(jnp.int32, (c_len, c_len), 1).astype(F32)
    if reverse:
        rel = si - ci
        wq = jnp.exp(lg * (c_len - ci))
        wk = jnp.exp(lg * ci)
    else:
        rel = ci - si
        wq = jnp.exp(lg * (ci + 1.0))
        wk = jnp.exp(lg * (c_len - 1.0 - ci))
    dmat = jnp.where(rel >= 0, jnp.exp(lg * jnp.maximum(rel, 0.0)), 0.0)
    return wq, wk, dmat, jnp.exp(lg * c_len)


def _ret_kernel(lf_ref, lb_ref, qf_ref, kf_ref, vf_ref, gf_ref, qb_ref, kb_ref, vb_ref, gb_ref, s0f_ref, s0b_ref,
                yf_ref, yb_ref, sf_ref, sb_ref, c_scr, *, n_chunks):
    h = pl.program_id(1)

    @pl.when(pl.program_id(2) == 0)
    def _():
        sf_ref[...] = s0f_ref[...]
        sb_ref[...] = s0b_ref[...]
        for j, arr in enumerate(_ret_consts(lf_ref[h], False) + _ret_consts(lb_ref[h], True)):
            c_scr[j] = arr

    c_len = RET_CHUNK
    nt = (((1,), (1,)), ((), ()))
    k_scale = HEAD_DIM ** -0.5
    dirs = []
    for d, refs in enumerate(((qf_ref, kf_ref, vf_ref, gf_ref, yf_ref), (qb_ref, kb_ref, vb_ref, gb_ref, yb_ref))):
        order = list(range(n_chunks)) if d == 0 else list(range(n_chunks - 1, -1, -1))
        dirs.append(refs + tuple(c_scr[4 * d + j] for j in range(4)) + (order,))

    units = []
    for step in range(n_chunks):
        for d in range(2):
            q_ref, k_ref, v_ref, _, _, _, wk, dmat, _, order = dirs[d]
            sl = slice(order[step] * c_len, (order[step] + 1) * c_len)
            qb = q_ref[sl, :].astype(BF16)
            kb = k_ref[sl, :]
            vb = v_ref[sl, :].astype(BF16)
            sc = lax.dot_general(qb, kb.astype(BF16), nt, preferred_element_type=F32) * (dmat * k_scale)
            kv = jnp.dot((kb.astype(F32) * (wk * k_scale)).T.astype(BF16), vb, preferred_element_type=F32)
            units.append((d, sl, qb, vb, sc, kv))
    states = [sf_ref[...], sb_ref[...]]
    cross = []
    for (d, sl, qb, vb, sc, kv) in units:
        wq, g_chunk = dirs[d][5], dirs[d][8]
        cross.append(wq * jnp.dot(qb, states[d].astype(BF16), preferred_element_type=F32))
        states[d] = g_chunk * states[d] + kv
    sf_ref[...] = states[0]
    sb_ref[...] = states[1]
    for (d, sl, qb, vb, sc, kv), o_cross in zip(units, cross):
        g_ref, y_ref = dirs[d][3], dirs[d][4]
        o = jnp.dot(sc.astype(BF16), vb, preferred_element_type=F32) + o_cross
        mu = jnp.mean(o, axis=-1, keepdims=True)
        dev = o - mu
        gn = dev * lax.rsqrt(jnp.mean(dev * dev, axis=-1, keepdims=True) + EPS)
        y_ref[sl, :] = _silu(g_ref[sl, :].astype(F32)) * gn


def _retention_scan(p, logit_f, logit_b, s0_f, s0_b):
    b, n, _ = p.shape
    hh = N_HEADS_RET
    tb = min(n, 8 * RET_CHUNK)
    nblk = n // tb

    def fwd(c0):
        return pl.BlockSpec((None, tb, LANE), lambda bb, h, i: (bb, i, c0 + h))

    def bwd(c0):
        return pl.BlockSpec((None, tb, LANE), lambda bb, h, i: (bb, nblk - 1 - i, c0 + h))

    state = pl.BlockSpec((None, None, HEAD_DIM, HEAD_DIM), lambda bb, h, i: (bb, h, 0, 0))
    smem = pl.BlockSpec(memory_space=pltpu.SMEM)
    y_shape = jax.ShapeDtypeStruct((b, n, hh * HEAD_DIM), F32)
    s_shape = jax.ShapeDtypeStruct((b, hh, HEAD_DIM, HEAD_DIM), F32)
    return pl.pallas_call(
        functools.partial(_ret_kernel, n_chunks=tb // RET_CHUNK),
        grid=(b, hh, nblk),
        in_specs=[smem, smem,
                  fwd(COL_RET_Q), fwd(COL_RET_K), fwd(COL_RET_V), fwd(COL_RET_GF),
                  bwd(COL_RET_Q), bwd(COL_RET_K), bwd(COL_RET_V), bwd(COL_RET_GB),
                  state, state],
        out_specs=[fwd(0), bwd(0), state, state],
        out_shape=[y_shape, y_shape, s_shape, s_shape],
        scratch_shapes=[pltpu.VMEM((8, RET_CHUNK, RET_CHUNK), F32)],
        compiler_params=_cparams(("parallel", "parallel", "arbitrary"), 32),
        name="retention",
    )(logit_f.astype(F32), logit_b.astype(F32), p, p, p, p, p, p, p, p, s0_f, s0_b)


def _retention(p_c, p_l, logit_f, logit_b):
    b = p_c.shape[0]
    s0 = jnp.zeros((b, N_HEADS_RET, HEAD_DIM, HEAD_DIM), F32)
    yc_f, yc_b, s_cf, s_cb = _retention_scan(p_c, logit_f, logit_b, s0, s0)
    yl_f, yl_b, _, _ = _retention_scan(p_l, logit_f, logit_b, s_cf, s_cb)
    return (yc_f, yc_b), (yl_f, yl_b)


MERGE_ROWS = 512
MERGE_HALVES = 2


def _merge_kernel(x_ref, ya_ref, yb_ref, yrf_ref, yrb_ref, g_ref, gate_ref, w_ref, g2_ref, sh2_ref, sc2_ref, wr_ref,
                  o_ref, h_ref, aff_ref):
    wa = ya_ref.shape[-1]
    wb = yb_ref.shape[-1]
    hr = x_ref.shape[0] // MERGE_HALVES
    halves = [slice(j * hr, (j + 1) * hr) for j in range(MERGE_HALVES)]
    wr = wr_ref[...]
    w_hi = wr.astype(BF16)
    w_lo = (wr - w_hi.astype(F32)).astype(BF16)
    normed = []
    for rows in halves:
        na = (_rms(ya_ref[rows, :]) * g_ref[:, :wa]).astype(BF16)
        nb = (_rms(yb_ref[rows, :]) * g_ref[:, wa:wa + wb]).astype(BF16)
        nr = (_rms(yrf_ref[rows, :] + yrb_ref[rows, :]) * g_ref[:, wa + wb:]).astype(BF16)
        normed.append((na, nb, nr))
    accs = []
    for na, nb, nr in normed:
        acc = jnp.dot(na, w_ref[:wa, :], preferred_element_type=F32)
        acc = acc + jnp.dot(nb, w_ref[wa:wa + wb, :], preferred_element_type=F32)
        accs.append(acc + jnp.dot(nr, w_ref[wa + wb:, :], preferred_element_type=F32))
    for rows, acc in zip(halves, accs):
        x_new = x_ref[rows, :] + gate_ref[...] * acc
        o_ref[rows, :] = x_new
        h = (_rms(x_new) * g2_ref[...]) * (1.0 + sc2_ref[...]) + sh2_ref[...]
        h_hi = h.astype(BF16)
        h_ref[rows, :] = h_hi
        h_lo = (h - h_hi.astype(F32)).astype(BF16)
        logits = (jnp.dot(h_hi, w_hi, preferred_element_type=F32)
                  + (jnp.dot(h_hi, w_lo, preferred_element_type=F32)
                     + jnp.dot(h_lo, w_hi, preferred_element_type=F32)))
        lane = lax.broadcasted_iota(jnp.int32, logits.shape, 1)
        logits = jnp.where(lane < N_EXPERTS, logits, NEG_BIG)
        e = jnp.exp(logits - jnp.max(logits, axis=-1, keepdims=True))
        aff_ref[rows, :] = e / jnp.sum(e, axis=-1, keepdims=True)


def _merge_out(x, ya, yb, yr, g, gate, w_out, g2, shift2, scale2, w_router_pad):
    b, n, d = x.shape
    yrf, yrb = yr
    tm = min(n, MERGE_ROWS)
    dm = w_out.shape[0]

    def tok(width):
        return pl.BlockSpec((None, tm, width), lambda bb, i: (bb, i, 0))

    row = pl.BlockSpec((1, d), lambda bb, i: (0, 0))
    per_batch = pl.BlockSpec((None, 1, d), lambda bb, i: (bb, 0, 0))
    return pl.pallas_call(
        _merge_kernel,
        grid=(b, n // tm),
        in_specs=[tok(d), tok(ya.shape[-1]), tok(yb.shape[-1]), tok(yrf.shape[-1]), tok(yrb.shape[-1]),
                  pl.BlockSpec((1, dm), lambda bb, i: (0, 0)), per_batch,
                  pl.BlockSpec((dm, d), lambda bb, i: (0, 0)),
                  row, per_batch, per_batch,
                  pl.BlockSpec((d, LANE), lambda bb, i: (0, 0))],
        out_specs=[tok(d), tok(d), tok(LANE)],
        out_shape=[jax.ShapeDtypeStruct((b, n, d), F32), jax.ShapeDtypeStruct((b, n, d), BF16),
                   jax.ShapeDtypeStruct((b, n, LANE), F32)],
        compiler_params=_cparams(("parallel", "parallel"), 58),
        name="merge_out",
    )(x, ya, yb, yrf, yrb, g.reshape(1, dm), gate, w_out, g2.reshape(1, d), shift2, scale2, w_router_pad)


def _ffn_up_kernel(xs_ref, wg_ref, wu_ref, hm_ref, wg_scr, wu_scr):
    @pl.when(pl.program_id(1) == 0)
    def _():
        wg_scr[...] = wg_ref[...].astype(BF16)
        wu_scr[...] = wu_ref[...].astype(BF16)

    xs = xs_ref[...]
    a = jnp.dot(xs, wg_scr[...], preferred_element_type=F32)
    u = jnp.dot(xs, wu_scr[...], preferred_element_type=F32)
    hm_ref[...] = (_silu(a) * u).astype(BF16)


def _ffn_down_kernel(hm_ref, gate_ref, wd_ref, o_ref, wd_scr):
    @pl.when(pl.program_id(1) == 0)
    def _():
        wd_scr[...] = wd_ref[...].astype(BF16)

    o_ref[...] = (jnp.dot(hm_ref[...], wd_scr[...], preferred_element_type=F32) * gate_ref[...]).astype(o_ref.dtype)


def _expert_ffn(xs, gates, w_gate, w_up, w_down, layer):
    e, t, d = xs.shape
    f = w_gate.shape[-1]
    tm = min(t, 256)
    tm_down = min(t, 1024)
    hm = pl.pallas_call(
        _ffn_up_kernel,
        grid=(e, t // tm),
        in_specs=[pl.BlockSpec((None, tm, d), lambda ee, i: (ee, i, 0)),
                  pl.BlockSpec((None, None, d, f), lambda ee, i: (layer, ee, 0, 0)),
                  pl.BlockSpec((None, None, d, f), lambda ee, i: (layer, ee, 0, 0))],
        out_specs=pl.BlockSpec((None, tm, f), lambda ee, i: (ee, i, 0)),
        out_shape=jax.ShapeDtypeStruct((e, t, f), BF16),
        scratch_shapes=[pltpu.VMEM((d, f), BF16), pltpu.VMEM((d, f), BF16)],
        compiler_params=_cparams(("parallel", "arbitrary"), 56),
        name="expert_ffn_up",
    )(xs, w_gate, w_up)
    return pl.pallas_call(
        _ffn_down_kernel,
        grid=(e, t // tm_down),
        in_specs=[pl.BlockSpec((None, tm_down, f), lambda ee, i: (ee, i, 0)),
                  pl.BlockSpec((None, tm_down, 1), lambda ee, i: (ee, i, 0)),
                  pl.BlockSpec((None, None, f, d), lambda ee, i: (layer, ee, 0, 0))],
        out_specs=pl.BlockSpec((None, tm_down, d), lambda ee, i: (ee, i, 0)),
        out_shape=jax.ShapeDtypeStruct((e, t, d), BF16),
        scratch_shapes=[pltpu.VMEM((f, d), BF16)],
        compiler_params=_cparams(("parallel", "arbitrary"), 56),
        name="expert_ffn_down",
    )(hm, gates, w_down)


COMBINE_TOKENS = 256
COMBINE_ROWS = 512


def _combine_kernel(tile_ref, win_ref, flag_ref, x_ref, yp_ref, tok_ref, gate_ref, gn_ref, o_ref, acc_scr, *,
                    final_norm):
    s = pl.program_id(0)
    flags = flag_ref[s]
    t = acc_scr.shape[0]

    @pl.when((flags & 1) != 0)
    def _():
        acc_scr[...] = jnp.zeros(acc_scr.shape, F32)

    @pl.when((flags & 2) != 0)
    def _():
        tcol = tile_ref[s] * t + lax.broadcasted_iota(jnp.int32, (t, 1), 0)
        onehot = jnp.where(tcol == tok_ref[...], 1.0, 0.0).astype(BF16)
        acc_scr[...] += jnp.dot(onehot, yp_ref[...], preferred_element_type=F32)

    @pl.when((flags & 4) != 0)
    def _():
        y = x_ref[...] + gate_ref[...] * acc_scr[...]
        if final_norm:
            y = _rms(y) * gn_ref[...]
        o_ref[...] = y


def _moe_combine(x, y_rows, tok_rows, gate, norm_g, final_norm):
    b, n, d = x.shape
    r = y_rows.shape[0]
    t = min(COMBINE_TOKENS, n)
    w = COMBINE_ROWS
    ntiles = b * n // t
    nwin_total = r // w
    tiles_per_batch = n // t
    n_steps = nwin_total + 2 * ntiles
    i32 = jnp.int32
    tok_sorted, order = lax.sort_key_val(tok_rows.astype(i32), jnp.arange(r, dtype=i32))
    yp = jnp.take(y_rows, order, axis=0, mode="clip")
    edges = jnp.arange(ntiles + 1, dtype=i32) * t
    bounds = jnp.sum((tok_sorted[None, :] < edges[:, None]).astype(i32), axis=1)
    w_lo = jnp.minimum(bounds[:-1] // w, nwin_total - 1)
    w_hi = jnp.maximum((bounds[1:] + w - 1) // w, w_lo + 1)
    nwin = w_hi - w_lo
    step0 = jnp.cumsum(nwin) - nwin
    total = jnp.sum(nwin)
    s = jnp.arange(n_steps, dtype=i32)
    tile_s = jnp.sum((s[:, None] >= step0[None, :]).astype(i32), axis=1) - 1
    sel = tile_s[:, None] == jnp.arange(ntiles, dtype=i32)[None, :]

    def of_tile(v):
        return jnp.sum(jnp.where(sel, v[None, :], 0), axis=1)

    step0_s, w_lo_s, nwin_s = of_tile(step0), of_tile(w_lo), of_tile(nwin)
    win_s = jnp.minimum(w_lo_s + (s - step0_s), w_lo_s + nwin_s - 1)
    valid = s < total
    first = valid & (s == step0_s)
    last = valid & (s == step0_s + nwin_s - 1)
    flags = first.astype(i32) + 2 * valid.astype(i32) + 4 * last.astype(i32)
    out = pl.pallas_call(
        functools.partial(_combine_kernel, final_norm=final_norm),
        grid_spec=pltpu.PrefetchScalarGridSpec(
            num_scalar_prefetch=3,
            grid=(n_steps,),
            in_specs=[pl.BlockSpec((t, d), lambda s, tl, wn, fl: (tl[s], 0)),
                      pl.BlockSpec((w, d), lambda s, tl, wn, fl: (wn[s], 0)),
                      pl.BlockSpec((None, 1, w), lambda s, tl, wn, fl: (wn[s], 0, 0)),
                      pl.BlockSpec((None, 1, d), lambda s, tl, wn, fl: (tl[s] // tiles_per_batch, 0, 0)),
                      pl.BlockSpec((1, d), lambda s, tl, wn, fl: (0, 0))],
            out_specs=pl.BlockSpec((t, d), lambda s, tl, wn, fl: (tl[s], 0)),
            scratch_shapes=[pltpu.VMEM((t, d), F32)]),
        out_shape=jax.ShapeDtypeStruct((b * n, d), F32),
        compiler_params=_cparams(("arbitrary",), 40),
        name="moe_combine",
    )(tile_s, win_s, flags, x.reshape(b * n, d), yp, tok_sorted.reshape(nwin_total, 1, w), gate,
      norm_g.reshape(1, d))
    return out.reshape(b, n, d)


def _ec_moe(x, h, aff, gate2, experts, norm_g, final_norm):
    b, n, d = x.shape
    cap = EC_CAPACITY * n // N_EXPERTS
    gate, idx = lax.top_k(jnp.swapaxes(aff[..., :N_EXPERTS], 1, 2), cap)
    flat = (idx + (jnp.arange(b, dtype=idx.dtype) * n)[:, None, None]).transpose(1, 0, 2).reshape(-1)
    xs = jnp.take(h.reshape(b * n, d), flat, axis=0, mode="clip").reshape(N_EXPERTS, b * cap, d)
    gates = gate.transpose(1, 0, 2).reshape(N_EXPERTS, b * cap, 1)
    y = _expert_ffn(xs, gates, *experts)
    return _moe_combine(x, y.reshape(-1, d), flat, gate2, norm_g, final_norm)


_ROPE_SRC = np.concatenate([np.arange(16, 32), np.arange(0, 16), np.arange(48, 64), np.arange(32, 48)])
_ROPE_SIGN = np.concatenate([-np.ones(16), np.ones(16), -np.ones(16), np.ones(16)]).astype(np.float32)


def _rope_perm_cols(w):
    return w[:, _ROPE_SRC] * _ROPE_SIGN


def _pad_w_in(w_in):
    c_kr = MLA_Q_LORA + MLA_KV_LORA
    kr = w_in[:, c_kr:c_kr + MLA_ROPE_DIM]
    return jnp.concatenate([w_in[:, :c_kr + MLA_ROPE_DIM], _rope_perm_cols(kr), w_in[:, c_kr + MLA_ROPE_DIM:]],
                           axis=1).astype(BF16)


def _expand_w_uq(w_uq):
    w = w_uq.reshape(MLA_Q_LORA, N_HEADS_MLA, HEAD_DIM + MLA_ROPE_DIM)
    nope = w[:, :, :HEAD_DIM]
    rope = w[:, :, HEAD_DIM:]
    partner = rope[:, :, _ROPE_SRC] * _ROPE_SIGN
    return jnp.concatenate([nope, rope, partner], axis=-1).reshape(MLA_Q_LORA, N_HEADS_MLA * MLA_QK_PAD).astype(BF16)


def _rope_tables(n):
    t = jnp.arange(n)
    n_freq = MLA_ROPE_DIM // 4
    inv = ROPE_BASE ** (-jnp.arange(n_freq, dtype=F32) / n_freq)
    ang_r = (t // GRID_W).astype(F32)[:, None] * inv
    ang_c = (t % GRID_W).astype(F32)[:, None] * inv
    zeros = jnp.zeros((n, LANE - MLA_ROPE_DIM), F32)
    cos = jnp.concatenate([jnp.cos(ang_r), jnp.cos(ang_r), jnp.cos(ang_c), jnp.cos(ang_c), zeros], axis=1)
    sin = jnp.concatenate([jnp.sin(ang_r), jnp.sin(ang_r), jnp.sin(ang_c), jnp.sin(ang_c), zeros], axis=1)
    return cos, sin


def _identity_tables(n):
    ones = jnp.ones((n, MLA_ROPE_DIM), F32)
    zeros = jnp.zeros((n, LANE - MLA_ROPE_DIM), F32)
    return jnp.concatenate([ones, zeros], axis=1), jnp.zeros((n, LANE), F32)


def _layer(x_l, x_c, mod, lp, last, final_g):
    b, n, d = x_l.shape
    n_ctx = x_c.shape[1]

    def chunk(k, ctx):
        rows = jnp.broadcast_to(mod[b, k * d:(k + 1) * d], (b, d)) if ctx else mod[:b, k * d:(k + 1) * d]
        return rows.reshape(b, 1, d)

    w_in = _pad_w_in(lp['w_in'])
    p_l = _norm_proj(x_l, lp['norm1_g'], chunk(0, False), chunk(1, False), w_in)
    p_c = _norm_proj(x_c, lp['norm1_g'], chunk(0, True), chunk(1, True), w_in)

    cos_l, sin_l = _rope_tables(n)
    cos_c, sin_c = _identity_tables(n_ctx)
    w_uk = lp['mla_w_uk'].reshape(MLA_KV_LORA, -1).astype(BF16)
    w_uv_t = lp['mla_w_uv'].reshape(MLA_KV_LORA, -1).T.astype(BF16)
    w_q_t = _expand_w_uq(lp['mla_w_uq']).T
    k_l, v_l = _mla_kv_prep(p_l, lp['mla_kv_norm_g'], w_uk, w_uv_t, cos_l, sin_l)
    k_c, v_c = _mla_kv_prep(p_c, lp['mla_kv_norm_g'], w_uk, w_uv_t, cos_c, sin_c)
    q_l = _mla_q_prep(p_l, lp['mla_q_norm_g'], w_q_t, cos_l.T, sin_l.T)
    a_l = _attention_t(q_l, [(k_c, v_c), (k_l, v_l)], N_HEADS_MLA, MLA_QK_PAD, HEAD_DIM)

    b_l = _nat_latent(p_l, p_c, _nat_bias_table(lp['nat_rpb']))
    r_c, r_l = _retention(p_c, p_l, lp['ret_decay_f'], lp['ret_decay_b'])

    w_out = lp['w_out'].astype(BF16)
    w_router = jnp.zeros((d, LANE), F32).at[:, :N_EXPERTS].set(lp['w_router'])

    x_l, h_l, aff_l = _merge_out(x_l, a_l, b_l, r_l, lp['out_norm_g'], chunk(2, False), w_out,
                                 lp['norm2_g'], chunk(3, False), chunk(4, False), w_router)
    x_l = _ec_moe(x_l, h_l, aff_l, chunk(5, False), lp['experts'], final_g, last)
    if last:
        return x_l, None

    q_c = _mla_q_prep(p_c, lp['mla_q_norm_g'], w_q_t, cos_c.T, sin_c.T)
    a_c = _attention_t(q_c, [(k_c, v_c)], N_HEADS_MLA, MLA_QK_PAD, HEAD_DIM)
    b_c = _attention(p_c, COL_NAT_Q, HEAD_DIM, [(p_c, COL_NAT_K, p_c, COL_NAT_V)], HEAD_DIM, N_HEADS_NAT,
                     HEAD_DIM ** -0.5)
    x_c, h_c, aff_c = _merge_out(x_c, a_c, b_c, r_c, lp['out_norm_g'], chunk(2, True), w_out,
                                 lp['norm2_g'], chunk(3, True), chunk(4, True), w_router)
    x_c = _ec_moe(x_c, h_c, aff_c, chunk(5, True), lp['experts'], final_g, False)
    return x_l, x_c


def kernel(x, c, ctx, c_ctx, w_mod, b_mod, norm1_g, w_in, mla_q_norm_g, mla_kv_norm_g, mla_w_uq, mla_w_uk,
           mla_w_uv, nat_rpb, ret_decay_f, ret_decay_b, out_norm_g, w_out, norm2_g, w_router, w_gate, w_up,
           w_down, final_norm_g):
    depth = w_mod.shape[0]
    cvecs = jnp.concatenate([c, c_ctx[None, :]], axis=0)
    x_l, x_c = x, ctx
    for i in range(depth):
        lp = {
            'norm1_g': norm1_g[i], 'w_in': w_in[i], 'mla_q_norm_g': mla_q_norm_g[i],
            'mla_kv_norm_g': mla_kv_norm_g[i], 'mla_w_uq': mla_w_uq[i], 'mla_w_uk': mla_w_uk[i],
            'mla_w_uv': mla_w_uv[i], 'nat_rpb': nat_rpb[i], 'ret_decay_f': ret_decay_f[i],
            'ret_decay_b': ret_decay_b[i], 'out_norm_g': out_norm_g[i], 'w_out': w_out[i],
            'norm2_g': norm2_g[i], 'w_router': w_router[i], 'experts': (w_gate, w_up, w_down, i),
        }
        mod = _modulation(cvecs, w_mod[i], b_mod[i])
        x_l, x_c = _layer(x_l, x_c, mod, lp, i == depth - 1, final_norm_g)
    return x_l
```

```python
import functools

import numpy as np
import jax
import jax.numpy as jnp
from jax import lax
from jax.experimental import pallas as pl
from jax.experimental.pallas import tpu as pltpu

F32 = jnp.float32
BF16 = jnp.bfloat16

GRID_W = 64
HEAD_DIM = 128
N_HEADS_MLA = 8
N_HEADS_NAT = 4
N_HEADS_RET = 4
MLA_ROPE_DIM = 64
MLA_Q_LORA = 512
MLA_KV_LORA = 256
MLA_SCALE = (HEAD_DIM + MLA_ROPE_DIM) ** -0.5
LOG2E = 1.4426950408889634
MLA_Q_SCALE = MLA_SCALE * LOG2E
MLA_QK_PAD = 256
NAT_ROWS = 8
NAT_COLS = 16
RET_CHUNK = 128
N_EXPERTS = 16
EC_CAPACITY = 2
ROPE_BASE = 10000.0
EPS = 1e-6
NEG_BIG = -1e30
LANE = 128

COL_CQ = 0
COL_CKV = 4
COL_KR = 6
COL_NAT_Q = 7
COL_NAT_K = 11
COL_NAT_V = 15
COL_RET_Q = 19
COL_RET_K = 23
COL_RET_V = 27
COL_RET_GF = 31
COL_RET_GB = 35
IN_COLS_PAD = 39 * LANE


VMEM_MIB_SMALL = 32
VMEM_MIB_MEDIUM = 40
VMEM_MIB_LARGE = 48
VMEM_MIB_XLARGE = 56
VMEM_MIB_MAX = 58


def _cparams(sem, vmem_mib):
    return pltpu.CompilerParams(dimension_semantics=sem, vmem_limit_bytes=vmem_mib * 1024 * 1024)


def _silu(a):
    return a * (1.0 / (1.0 + jnp.exp(-a)))


def _rms(x):
    return x * lax.rsqrt(jnp.mean(x * x, axis=-1, keepdims=True) + EPS)


def _mod_kernel(ct_ref, w_ref, b_ref, o_ref, a_scr, *, n_rows, k_chunk):
    ct = ct_ref[...]
    a_scr[...] = _silu(ct)
    d = w_ref.shape[0]
    tn = w_ref.shape[1]

    def body(kc, accs):
        k0 = pl.multiple_of(kc * k_chunk, k_chunk)
        wblk = w_ref[pl.ds(k0, k_chunk), :]
        out = []
        for r in range(n_rows):
            col = a_scr[pl.ds(k0, k_chunk), r:r + 1]
            out.append(accs[r] + jnp.sum((col * wblk).reshape(k_chunk // 8, 8, tn), axis=0))
        return tuple(out)

    accs = lax.fori_loop(0, d // k_chunk, body, tuple(jnp.zeros((8, tn), F32) for _ in range(n_rows)))
    o_ref[...] = jnp.zeros(o_ref.shape, F32)
    for r in range(n_rows):
        o_ref[r:r + 1, :] = jnp.sum(accs[r], axis=0, keepdims=True) + b_ref[...]


def _modulation(cvecs, w_mod, b_mod, layer):
    n_rows, d = cvecs.shape
    n_out = w_mod.shape[-1]
    tn = 1024
    ct = jnp.zeros((d, 8), F32).at[:, :n_rows].set(cvecs.T)
    return pl.pallas_call(
        functools.partial(_mod_kernel, n_rows=n_rows, k_chunk=64),
        grid=(n_out // tn,),
        in_specs=[pl.BlockSpec((d, 8), lambda j: (0, 0)),
                  pl.BlockSpec((None, d, tn), lambda j: (layer, 0, j)),
                  pl.BlockSpec((1, tn), lambda j: (0, j))],
        out_specs=pl.BlockSpec((8, tn), lambda j: (0, j)),
        out_shape=jax.ShapeDtypeStruct((8, n_out), F32),
        scratch_shapes=[pltpu.VMEM((d, 8), F32)],
        compiler_params=_cparams(("parallel",), VMEM_MIB_MEDIUM),
        name="modulation",
    )(ct, w_mod, b_mod.reshape(1, n_out))


NORM_PROJ_PARTS = 2


def _norm_proj_kernel(x_ref, g_ref, sh_ref, sc_ref, w_ref, o_ref, h_scr):
    j = pl.program_id(2)
    pr = x_ref.shape[0] // NORM_PROJ_PARTS
    parts = [slice(k * pr, (k + 1) * pr) for k in range(NORM_PROJ_PARTS)]

    @pl.when(j == 0)
    def _():
        hs = []
        for rows in parts:
            y = _rms(x_ref[rows, :]) * g_ref[...]
            h = (y * (1.0 + sc_ref[...]) + sh_ref[...]).astype(BF16)
            h_scr[rows, :] = h
            hs.append(h)
        for rows, h in zip(parts, hs):
            o_ref[rows, :] = jnp.dot(h, w_ref[...], preferred_element_type=F32).astype(o_ref.dtype)

    @pl.when(j != 0)
    def _():
        o_ref[...] = jnp.dot(h_scr[...], w_ref[...], preferred_element_type=F32).astype(o_ref.dtype)


def _norm_proj(x, g, shift, scale, w):
    b, n, d = x.shape
    c = w.shape[1]
    tm = min(n, 1024)
    tn = c // 3 if (c % (3 * LANE) == 0) else c
    return pl.pallas_call(
        _norm_proj_kernel,
        grid=(b, n // tm, c // tn),
        in_specs=[pl.BlockSpec((None, tm, d), lambda bb, i, j: (bb, i, 0)),
                  pl.BlockSpec((1, d), lambda bb, i, j: (0, 0)),
                  pl.BlockSpec((None, 1, d), lambda bb, i, j: (bb, 0, 0)),
                  pl.BlockSpec((None, 1, d), lambda bb, i, j: (bb, 0, 0)),
                  pl.BlockSpec((d, tn), lambda bb, i, j: (0, j))],
        out_specs=pl.BlockSpec((None, tm, tn), lambda bb, i, j: (bb, i, j)),
        out_shape=jax.ShapeDtypeStruct((b, n, c), BF16),
        scratch_shapes=[pltpu.VMEM((tm, d), BF16)],
        compiler_params=_cparams(("parallel", "parallel", "arbitrary"), VMEM_MIB_XLARGE),
        name="norm_proj",
    )(x, g.reshape(1, d), shift, scale, w)


def _rope_rotate(r, cos, sin):
    return r * cos + pltpu.roll(r, 64, 1) * sin


def _mla_q_kernel(cq_ref, g_ref, wt_ref, cos_ref, sin_ref, o_ref, *, n_heads):
    cqn_t = (_rms(cq_ref[...].astype(F32)) * g_ref[...]).T.astype(BF16)
    qe = jnp.dot(wt_ref[...], cqn_t, preferred_element_type=F32)
    cos = cos_ref[...]
    sin = sin_ref[...]
    for h in range(n_heads):
        c0 = h * MLA_QK_PAD
        o_ref[c0:c0 + LANE, :] = (qe[c0:c0 + LANE, :] * MLA_Q_SCALE).astype(BF16)
        r = qe[c0 + LANE:c0 + 2 * LANE, :]
        swapped = jnp.concatenate([r[MLA_ROPE_DIM:, :], r[:MLA_ROPE_DIM, :]], axis=0)
        o_ref[c0 + LANE:c0 + 2 * LANE, :] = ((r * cos + swapped * sin) * MLA_Q_SCALE).astype(BF16)


def _mla_q_prep(p, g, w_q_t, cos_t, sin_t):
    b, n, _ = p.shape
    tm = min(n, 512)
    cw = N_HEADS_MLA * MLA_QK_PAD
    return pl.pallas_call(
        functools.partial(_mla_q_kernel, n_heads=N_HEADS_MLA),
        grid=(b, n // tm),
        in_specs=[pl.BlockSpec((None, tm, MLA_Q_LORA), lambda bb, i: (bb, i, COL_CQ * LANE // MLA_Q_LORA)),
                  pl.BlockSpec((1, MLA_Q_LORA), lambda bb, i: (0, 0)),
                  pl.BlockSpec((cw, MLA_Q_LORA), lambda bb, i: (0, 0)),
                  pl.BlockSpec((LANE, tm), lambda bb, i: (0, i)),
                  pl.BlockSpec((LANE, tm), lambda bb, i: (0, i))],
        out_specs=pl.BlockSpec((None, cw, tm), lambda bb, i: (bb, 0, i)),
        out_shape=jax.ShapeDtypeStruct((b, cw, n), BF16),
        compiler_params=_cparams(("parallel", "parallel"), VMEM_MIB_MEDIUM),
        name="mla_q_prep",
    )(p, g.reshape(1, MLA_Q_LORA), w_q_t, cos_t, sin_t)


def _mla_kv_kernel(ckv_ref, kr_ref, g_ref, wuk_ref, wuvt_ref, cos_ref, sin_ref, k_ref, v_ref, *, n_heads):
    ckvn = _rms(ckv_ref[...].astype(F32)) * g_ref[...]
    kn = jnp.dot(ckvn.astype(BF16), wuk_ref[...], preferred_element_type=F32)
    v_ref[...] = jnp.dot(wuvt_ref[...], ckvn.T.astype(BF16), preferred_element_type=F32).astype(BF16)
    rot = _rope_rotate(kr_ref[...].astype(F32), cos_ref[...], sin_ref[...]).astype(BF16)
    for h in range(n_heads):
        c0 = h * MLA_QK_PAD
        k_ref[:, c0:c0 + LANE] = kn[:, h * LANE:(h + 1) * LANE].astype(BF16)
        k_ref[:, c0 + LANE:c0 + 2 * LANE] = rot


def _mla_kv_prep(p, g, w_uk, w_uv_t, cos, sin):
    b, n, _ = p.shape
    tm = min(n, 512)
    kw = N_HEADS_MLA * MLA_QK_PAD
    vw = N_HEADS_MLA * HEAD_DIM
    return pl.pallas_call(
        functools.partial(_mla_kv_kernel, n_heads=N_HEADS_MLA),
        grid=(b, n // tm),
        in_specs=[pl.BlockSpec((None, tm, MLA_KV_LORA), lambda bb, i: (bb, i, COL_CKV * LANE // MLA_KV_LORA)),
                  pl.BlockSpec((None, tm, LANE), lambda bb, i: (bb, i, COL_KR)),
                  pl.BlockSpec((1, MLA_KV_LORA), lambda bb, i: (0, 0)),
                  pl.BlockSpec((MLA_KV_LORA, vw), lambda bb, i: (0, 0)),
                  pl.BlockSpec((vw, MLA_KV_LORA), lambda bb, i: (0, 0)),
                  pl.BlockSpec((tm, LANE), lambda bb, i: (i, 0)),
                  pl.BlockSpec((tm, LANE), lambda bb, i: (i, 0))],
        out_specs=[pl.BlockSpec((None, tm, kw), lambda bb, i: (bb, i, 0)),
                   pl.BlockSpec((None, vw, tm), lambda bb, i: (bb, 0, i))],
        out_shape=[jax.ShapeDtypeStruct((b, n, kw), BF16), jax.ShapeDtypeStruct((b, vw, n), BF16)],
        compiler_params=_cparams(("parallel", "parallel"), VMEM_MIB_MEDIUM),
        name="mla_kv_prep",
    )(p, p, g.reshape(1, MLA_KV_LORA), w_uk, w_uv_t, cos, sin)


def _attn_kernel(*refs, n_src, scale, tk_max):
    q_ref = refs[0]
    o_ref = refs[1 + 2 * n_src]
    q = q_ref[...]
    if scale != 1.0:
        q = q.astype(F32) * scale
    q = q.astype(BF16)
    tq = q.shape[0]
    dv = o_ref.shape[-1]

    def step(k, v, carry):
        m, l, acc = carry
        s = lax.dot_general(q, k, (((1,), (1,)), ((), ())), preferred_element_type=F32)
        m_new = jnp.maximum(m, jnp.max(s, axis=-1, keepdims=True))
        p = jnp.exp(s - m_new)
        alpha = jnp.exp(m - m_new)
        l = alpha * l + jnp.sum(p, axis=-1, keepdims=True)
        acc = alpha * acc + jnp.dot(p.astype(BF16), v, preferred_element_type=F32)
        return m_new, l, acc

    carry = (jnp.full((tq, 1), NEG_BIG, F32), jnp.zeros((tq, 1), F32), jnp.zeros((tq, dv), F32))
    for s_i in range(n_src):
        k_ref = refs[1 + 2 * s_i]
        v_ref = refs[2 + 2 * s_i]
        nk = k_ref.shape[0]
        tk = min(nk, tk_max)
        if nk == tk:
            carry = step(k_ref[...].astype(BF16), v_ref[...].astype(BF16), carry)
        else:
            def body(c, cr, k_ref=k_ref, v_ref=v_ref, tk=tk):
                k0 = pl.multiple_of(c * tk, tk)
                return step(k_ref[pl.ds(k0, tk), :].astype(BF16), v_ref[pl.ds(k0, tk), :].astype(BF16), cr)
            carry = lax.fori_loop(0, nk // tk, body, carry)
    _, l, acc = carry
    o_ref[...] = acc / l


def _attention(q_arr, q_blk0, dq, sources, dv, n_heads, scale, tq=256, tk_max=512):
    b, nq, _ = q_arr.shape
    tq = min(nq, tq)
    in_specs = [pl.BlockSpec((None, tq, dq), lambda bb, h, i: (bb, i, q_blk0 + h))]
    args = [q_arr]
    for k_arr, k_blk0, v_arr, v_blk0 in sources:
        nk = k_arr.shape[1]
        in_specs.append(pl.BlockSpec((None, nk, dq), lambda bb, h, i, o=k_blk0: (bb, 0, o + h)))
        in_specs.append(pl.BlockSpec((None, nk, dv), lambda bb, h, i, o=v_blk0: (bb, 0, o + h)))
        args += [k_arr, v_arr]
    return pl.pallas_call(
        functools.partial(_attn_kernel, n_src=len(sources), scale=scale, tk_max=tk_max),
        grid=(b, n_heads, nq // tq),
        in_specs=in_specs,
        out_specs=pl.BlockSpec((None, tq, dv), lambda bb, h, i: (bb, i, h)),
        out_shape=jax.ShapeDtypeStruct((b, nq, n_heads * dv), F32),
        compiler_params=_cparams(("parallel", "parallel", "arbitrary"), VMEM_MIB_LARGE),
        name="attention",
    )(*args)


ATTN_TQ = 2048
ATTN_TK = 512
ATTN_GROUP = 4


def _attn_t_kernel(*refs, n_src):
    q_ref = refs[0]
    o_ref = refs[1 + 2 * n_src]
    q = q_ref[...]
    tq = q.shape[1]
    dv = o_ref.shape[-1]

    def scores(k):
        return jnp.dot(k, q, preferred_element_type=F32)

    def update(s, vt, carry):
        m, l, acc = carry
        m_new = jnp.maximum(m, jnp.max(s, axis=0, keepdims=True))
        p = jnp.exp2(s - m_new)
        alpha = jnp.exp2(m - m_new)
        l = alpha * l + jnp.sum(p, axis=0, keepdims=True)
        acc = alpha * acc + jnp.dot(vt, p.astype(BF16), preferred_element_type=F32)
        return m_new, l, acc

    carry = (jnp.full((1, tq), NEG_BIG, F32), jnp.zeros((1, tq), F32), jnp.zeros((dv, tq), F32))
    for s_i in range(n_src):
        k_ref = refs[1 + 2 * s_i]
        vt_ref = refs[2 + 2 * s_i]
        nk = k_ref.shape[0]
        tk = min(nk, ATTN_TK)
        n_chunks = nk // tk
        if n_chunks == 1:
            carry = update(scores(k_ref[...]), vt_ref[...], carry)
            continue
        group = max(g for g in (ATTN_GROUP, 2, 1) if n_chunks % g == 0)

        def body(c, cr, k_ref=k_ref, vt_ref=vt_ref, tk=tk, group=group):
            starts = [pl.multiple_of((c * group + g) * tk, tk) for g in range(group)]
            ss = [scores(k_ref[pl.ds(k0, tk), :]) for k0 in starts]
            for s, k0 in zip(ss, starts):
                cr = update(s, vt_ref[:, pl.ds(k0, tk)], cr)
            return cr
        carry = lax.fori_loop(0, n_chunks // group, body, carry)
    _, l, acc = carry
    o_ref[...] = (acc / l).T


def _attention_t(q_arr, sources, n_heads, dq, dv):
    b, _, nq = q_arr.shape
    tq = min(nq, ATTN_TQ)
    in_specs = [pl.BlockSpec((None, dq, tq), lambda bb, h, i: (bb, h, i))]
    args = [q_arr]
    for k_arr, vt_arr in sources:
        nk = k_arr.shape[1]
        in_specs.append(pl.BlockSpec((None, nk, dq), lambda bb, h, i: (bb, 0, h)))
        in_specs.append(pl.BlockSpec((None, dv, nk), lambda bb, h, i: (bb, h, 0)))
        args += [k_arr, vt_arr]
    return pl.pallas_call(
        functools.partial(_attn_t_kernel, n_src=len(sources)),
        grid=(b, n_heads, nq // tq),
        in_specs=in_specs,
        out_specs=pl.BlockSpec((None, tq, dv), lambda bb, h, i: (bb, i, h)),
        out_shape=jax.ShapeDtypeStruct((b, nq, n_heads * dv), F32),
        compiler_params=_cparams(("parallel", "parallel", "arbitrary"), VMEM_MIB_XLARGE),
        name="attention_t",
    )(*args)


NAT_GROUP = 8
NAT_SLAB = NAT_GROUP + NAT_ROWS
NAT_HALVES = 2


def _nat_kernel(q_ref, k_ref, v_ref, kc_ref, vc_ref, bias_ref, o_ref, *, n_rows):
    i = pl.program_id(2)
    scale = HEAD_DIM ** -0.5
    nt = (((1,), (1,)), ((), ()))
    u0 = jnp.clip(i * NAT_GROUP - NAT_ROWS // 2, 0, n_rows - NAT_SLAB)
    t0 = pl.multiple_of(u0 * GRID_W, GRID_W)
    ks = k_ref[pl.ds(t0, NAT_SLAB * GRID_W), :].astype(BF16)
    vs = v_ref[pl.ds(t0, NAT_SLAB * GRID_W), :].astype(BF16)
    kc = kc_ref[...].astype(BF16)
    vc = vc_ref[...].astype(BF16)
    hq = q_ref.shape[0] // NAT_HALVES
    scores = []
    for j in range(NAT_HALVES):
        rows = slice(j * hq, (j + 1) * hq)
        q = (q_ref[rows, :].astype(F32) * scale).astype(BF16)
        scores.append((lax.dot_general(q, ks, nt, preferred_element_type=F32) + bias_ref[rows, :],
                       lax.dot_general(q, kc, nt, preferred_element_type=F32)))
    for j in range(NAT_HALVES):
        s_w, s_c = scores[j]
        m = jnp.maximum(jnp.max(s_w, axis=-1, keepdims=True), jnp.max(s_c, axis=-1, keepdims=True))
        p_w = jnp.exp(s_w - m)
        p_c = jnp.exp(s_c - m)
        l = jnp.sum(p_w, axis=-1, keepdims=True) + jnp.sum(p_c, axis=-1, keepdims=True)
        o = (jnp.dot(p_w.astype(BF16), vs, preferred_element_type=F32)
             + jnp.dot(p_c.astype(BF16), vc, preferred_element_type=F32))
        o_ref[j * hq:(j + 1) * hq, :] = o / l


def _nat_bias_table(rpb):
    h = rpb.shape[0]
    col = np.arange(GRID_W)
    c0 = np.clip(col - NAT_COLS // 2, 0, GRID_W - NAT_COLS)
    kc = np.arange(GRID_W)
    col_ok = (kc[None, :] >= c0[:, None]) & (kc[None, :] < c0[:, None] + NAT_COLS)
    dc = kc[None, :] - col[:, None] + NAT_COLS - 1
    sel_c = ((dc[None] == np.arange(2 * NAT_COLS - 1)[:, None, None]) & col_ok[None]).astype(np.float32)
    sel_c = sel_c.reshape(2 * NAT_COLS - 1, GRID_W * GRID_W)
    per_dr = jnp.einsum('hdc,cb->hdb', rpb.astype(F32), sel_c, precision=lax.Precision.HIGHEST)
    per_dr = jnp.where(col_ok.reshape(-1)[None, None], per_dr, NEG_BIG).reshape(h, 2 * NAT_ROWS - 1, GRID_W, GRID_W)
    return pl.pallas_call(
        _nat_table_kernel,
        grid=(h, 3),
        in_specs=[pl.BlockSpec((None, 2 * NAT_ROWS - 1, GRID_W, GRID_W), lambda hh, v: (hh, 0, 0, 0))],
        out_specs=pl.BlockSpec((None, None, NAT_GROUP * GRID_W, NAT_SLAB * GRID_W), lambda hh, v: (hh, v, 0, 0)),
        out_shape=jax.ShapeDtypeStruct((h, 3, NAT_GROUP * GRID_W, NAT_SLAB * GRID_W), F32),
        compiler_params=_cparams(("parallel", "arbitrary"), VMEM_MIB_SMALL),
        name="nat_bias_table",
    )(per_dr)


def _nat_table_kernel(per_dr_ref, o_ref):
    masked = jnp.full((GRID_W, GRID_W), NEG_BIG, F32)
    for variant, delta in enumerate((0, NAT_ROWS // 2, NAT_ROWS)):
        @pl.when(pl.program_id(1) == variant)
        def _(delta=delta):
            for g in range(NAT_GROUP):
                w0 = min(max(g + delta - NAT_ROWS // 2, 0), NAT_SLAB - NAT_ROWS)
                tiles = [per_dr_ref[u - (g + delta) + NAT_ROWS - 1] if w0 <= u < w0 + NAT_ROWS else masked
                         for u in range(NAT_SLAB)]
                o_ref[g * GRID_W:(g + 1) * GRID_W, :] = jnp.concatenate(tiles, axis=1)


def _nat_latent(p_l, p_c, bias_tbl):
    b, n, _ = p_l.shape
    n_ctx = p_c.shape[1]
    n_rows = n // GRID_W
    assert n_rows >= NAT_SLAB and n_rows % NAT_GROUP == 0
    n_groups = n_rows // NAT_GROUP
    tq = NAT_GROUP * GRID_W
    tk = NAT_SLAB * GRID_W

    def variant(i):
        return jnp.where(i == 0, 0, jnp.where(i == n_groups - 1, 2, 1))

    return pl.pallas_call(
        functools.partial(_nat_kernel, n_rows=n_rows),
        grid=(b, N_HEADS_NAT, n_groups),
        in_specs=[pl.BlockSpec((None, tq, LANE), lambda bb, h, i: (bb, i, COL_NAT_Q + h)),
                  pl.BlockSpec((None, n, LANE), lambda bb, h, i: (bb, 0, COL_NAT_K + h)),
                  pl.BlockSpec((None, n, LANE), lambda bb, h, i: (bb, 0, COL_NAT_V + h)),
                  pl.BlockSpec((None, n_ctx, LANE), lambda bb, h, i: (bb, 0, COL_NAT_K + h)),
                  pl.BlockSpec((None, n_ctx, LANE), lambda bb, h, i: (bb, 0, COL_NAT_V + h)),
                  pl.BlockSpec((None, None, tq, tk), lambda bb, h, i: (h, variant(i), 0, 0))],
        out_specs=pl.BlockSpec((None, tq, LANE), lambda bb, h, i: (bb, i, h)),
        out_shape=jax.ShapeDtypeStruct((b, n, N_HEADS_NAT * HEAD_DIM), F32),
        compiler_params=_cparams(("parallel", "parallel", "arbitrary"), VMEM_MIB_LARGE),
        name="nat_latent",
    )(p_l, p_l, p_l, p_c, p_c, bias_tbl)


def _ret_consts(logit, reverse):
    c_len = RET_CHUNK
    z = jnp.full((c_len, c_len), -logit, F32)
    lg = -(jnp.maximum(z, 0.0) + jnp.log1p(jnp.exp(-jnp.abs(z))))
    ci = lax.broadcasted_iota(jnp.int32, (c_len, c_len), 0).astype(F32)
    si = lax.broadcasted_iota(jnp.int32, (c_len, c_len), 1).astype(F32)
    if reverse:
        rel = si - ci
        wq = jnp.exp(lg * (c_len - ci))
        wk = jnp.exp(lg * ci)
    else:
        rel = ci - si
        wq = jnp.exp(lg * (ci + 1.0))
        wk = jnp.exp(lg * (c_len - 1.0 - ci))
    dmat = jnp.where(rel >= 0, jnp.exp(lg * jnp.maximum(rel, 0.0)), 0.0)
    return wq, wk, dmat, jnp.exp(lg * c_len)


def _ret_kernel(lf_ref, lb_ref, qf_ref, kf_ref, vf_ref, gf_ref, qb_ref, kb_ref, vb_ref, gb_ref, s0f_ref, s0b_ref,
                yf_ref, yb_ref, sf_ref, sb_ref, c_scr, *, n_chunks):
    h = pl.program_id(1)

    @pl.when(pl.program_id(2) == 0)
    def _():
        sf_ref[...] = s0f_ref[...]
        sb_ref[...] = s0b_ref[...]
        for j, arr in enumerate(_ret_consts(lf_ref[h], False) + _ret_consts(lb_ref[h], True)):
            c_scr[j] = arr

    c_len = RET_CHUNK
    nt = (((1,), (1,)), ((), ()))
    k_scale = HEAD_DIM ** -0.5
    dirs = []
    for d, refs in enumerate(((qf_ref, kf_ref, vf_ref, gf_ref, yf_ref), (qb_ref, kb_ref, vb_ref, gb_ref, yb_ref))):
        order = list(range(n_chunks)) if d == 0 else list(range(n_chunks - 1, -1, -1))
        dirs.append(refs + tuple(c_scr[4 * d + j] for j in range(4)) + (order,))

    units = []
    for step in range(n_chunks):
        for d in range(2):
            q_ref, k_ref, v_ref, _, _, _, wk, dmat, _, order = dirs[d]
            sl = slice(order[step] * c_len, (order[step] + 1) * c_len)
            qb = q_ref[sl, :].astype(BF16)
            kb = k_ref[sl, :]
            vb = v_ref[sl, :].astype(BF16)
            sc = lax.dot_general(qb, kb.astype(BF16), nt, preferred_element_type=F32) * (dmat * k_scale)
            kv = jnp.dot((kb.astype(F32) * (wk * k_scale)).T.astype(BF16), vb, preferred_element_type=F32)
            units.append((d, sl, qb, vb, sc, kv))
    states = [sf_ref[...], sb_ref[...]]
    cross = []
    for (d, sl, qb, vb, sc, kv) in units:
        wq, g_chunk = dirs[d][5], dirs[d][8]
        cross.append(wq * jnp.dot(qb, states[d].astype(BF16), preferred_element_type=F32))
        states[d] = g_chunk * states[d] + kv
    sf_ref[...] = states[0]
    sb_ref[...] = states[1]
    for (d, sl, qb, vb, sc, kv), o_cross in zip(units, cross):
        g_ref, y_ref = dirs[d][3], dirs[d][4]
        o = jnp.dot(sc.astype(BF16), vb, preferred_element_type=F32) + o_cross
        mu = jnp.mean(o, axis=-1, keepdims=True)
        dev = o - mu
        gn = dev * lax.rsqrt(jnp.mean(dev * dev, axis=-1, keepdims=True) + EPS)
        y_ref[sl, :] = _silu(g_ref[sl, :].astype(F32)) * gn


def _retention_scan(p, logit_f, logit_b, s0_f, s0_b):
    b, n, _ = p.shape
    hh = N_HEADS_RET
    tb = min(n, 16 * RET_CHUNK)
    nblk = n // tb

    def fwd(c0):
        return pl.BlockSpec((None, tb, LANE), lambda bb, h, i: (bb, i, c0 + h))

    def bwd(c0):
        return pl.BlockSpec((None, tb, LANE), lambda bb, h, i: (bb, nblk - 1 - i, c0 + h))

    state = pl.BlockSpec((None, None, HEAD_DIM, HEAD_DIM), lambda bb, h, i: (bb, h, 0, 0))
    smem = pl.BlockSpec(memory_space=pltpu.SMEM)
    y_shape = jax.ShapeDtypeStruct((b, n, hh * HEAD_DIM), F32)
    s_shape = jax.ShapeDtypeStruct((b, hh, HEAD_DIM, HEAD_DIM), F32)
    return pl.pallas_call(
        functools.partial(_ret_kernel, n_chunks=tb // RET_CHUNK),
        grid=(b, hh, nblk),
        in_specs=[smem, smem,
                  fwd(COL_RET_Q), fwd(COL_RET_K), fwd(COL_RET_V), fwd(COL_RET_GF),
                  bwd(COL_RET_Q), bwd(COL_RET_K), bwd(COL_RET_V), bwd(COL_RET_GB),
                  state, state],
        out_specs=[fwd(0), bwd(0), state, state],
        out_shape=[y_shape, y_shape, s_shape, s_shape],
        scratch_shapes=[pltpu.VMEM((8, RET_CHUNK, RET_CHUNK), F32)],
        compiler_params=_cparams(("parallel", "parallel", "arbitrary"), VMEM_MIB_SMALL),
        name="retention",
    )(logit_f.astype(F32), logit_b.astype(F32), p, p, p, p, p, p, p, p, s0_f, s0_b)


def _retention(p_c, p_l, logit_f, logit_b):
    b = p_c.shape[0]
    s0 = jnp.zeros((b, N_HEADS_RET, HEAD_DIM, HEAD_DIM), F32)
    yc_f, yc_b, s_cf, s_cb = _retention_scan(p_c, logit_f, logit_b, s0, s0)
    yl_f, yl_b, _, _ = _retention_scan(p_l, logit_f, logit_b, s_cf, s_cb)
    return (yc_f, yc_b), (yl_f, yl_b)


MERGE_ROWS = 512
MERGE_HALVES = 2


def _merge_kernel(x_ref, ya_ref, yb_ref, yrf_ref, yrb_ref, g_ref, gate_ref, w_ref, g2_ref, sh2_ref, sc2_ref, wr_ref,
                  o_ref, h_ref, aff_ref):
    wa = ya_ref.shape[-1]
    wb = yb_ref.shape[-1]
    hr = x_ref.shape[0] // MERGE_HALVES
    halves = [slice(j * hr, (j + 1) * hr) for j in range(MERGE_HALVES)]
    wr = wr_ref[...]
    w_hi = wr.astype(BF16)
    w_lo = (wr - w_hi.astype(F32)).astype(BF16)
    normed = []
    for rows in halves:
        na = (_rms(ya_ref[rows, :]) * g_ref[:, :wa]).astype(BF16)
        nb = (_rms(yb_ref[rows, :]) * g_ref[:, wa:wa + wb]).astype(BF16)
        nr = (_rms(yrf_ref[rows, :] + yrb_ref[rows, :]) * g_ref[:, wa + wb:]).astype(BF16)
        normed.append((na, nb, nr))
    accs = []
    for na, nb, nr in normed:
        acc = jnp.dot(na, w_ref[:wa, :], preferred_element_type=F32)
        acc = acc + jnp.dot(nb, w_ref[wa:wa + wb, :], preferred_element_type=F32)
        accs.append(acc + jnp.dot(nr, w_ref[wa + wb:, :], preferred_element_type=F32))
    for rows, acc in zip(halves, accs):
        x_new = x_ref[rows, :] + gate_ref[...] * acc
        o_ref[rows, :] = x_new
        h = (_rms(x_new) * g2_ref[...]) * (1.0 + sc2_ref[...]) + sh2_ref[...]
        h_hi = h.astype(BF16)
        h_ref[rows, :] = h_hi
        h_lo = (h - h_hi.astype(F32)).astype(BF16)
        logits = (jnp.dot(h_hi, w_hi, preferred_element_type=F32)
                  + (jnp.dot(h_hi, w_lo, preferred_element_type=F32)
                     + jnp.dot(h_lo, w_hi, preferred_element_type=F32)))
        lane = lax.broadcasted_iota(jnp.int32, logits.shape, 1)
        logits = jnp.where(lane < N_EXPERTS, logits, NEG_BIG)
        e = jnp.exp(logits - jnp.max(logits, axis=-1, keepdims=True))
        aff_ref[rows, :] = e / jnp.sum(e, axis=-1, keepdims=True)


def _merge_out(x, ya, yb, yr, g, gate, w_out, g2, shift2, scale2, w_router_pad):
    b, n, d = x.shape
    yrf, yrb = yr
    tm = min(n, MERGE_ROWS)
    dm = w_out.shape[0]

    def tok(width):
        return pl.BlockSpec((None, tm, width), lambda bb, i: (bb, i, 0))

    row = pl.BlockSpec((1, d), lambda bb, i: (0, 0))
    per_batch = pl.BlockSpec((None, 1, d), lambda bb, i: (bb, 0, 0))
    return pl.pallas_call(
        _merge_kernel,
        grid=(b, n // tm),
        in_specs=[tok(d), tok(ya.shape[-1]), tok(yb.shape[-1]), tok(yrf.shape[-1]), tok(yrb.shape[-1]),
                  pl.BlockSpec((1, dm), lambda bb, i: (0, 0)), per_batch,
                  pl.BlockSpec((dm, d), lambda bb, i: (0, 0)),
                  row, per_batch, per_batch,
                  pl.BlockSpec((d, LANE), lambda bb, i: (0, 0))],
        out_specs=[tok(d), tok(d), tok(LANE)],
        out_shape=[jax.ShapeDtypeStruct((b, n, d), F32), jax.ShapeDtypeStruct((b, n, d), BF16),
                   jax.ShapeDtypeStruct((b, n, LANE), F32)],
        compiler_params=_cparams(("parallel", "parallel"), VMEM_MIB_MAX),
        name="merge_out",
    )(x, ya, yb, yrf, yrb, g.reshape(1, dm), gate, w_out, g2.reshape(1, d), shift2, scale2, w_router_pad)


def _ffn_up_kernel(xs_ref, wg_ref, wu_ref, hm_ref, wg_scr, wu_scr):
    @pl.when(pl.program_id(1) == 0)
    def _():
        wg_scr[...] = wg_ref[...].astype(BF16)
        wu_scr[...] = wu_ref[...].astype(BF16)

    xs = xs_ref[...]
    a = jnp.dot(xs, wg_scr[...], preferred_element_type=F32)
    u = jnp.dot(xs, wu_scr[...], preferred_element_type=F32)
    hm_ref[...] = (_silu(a) * u).astype(BF16)


def _ffn_down_kernel(hm_ref, gate_ref, wd_ref, o_ref, wd_scr):
    @pl.when(pl.program_id(1) == 0)
    def _():
        wd_scr[...] = wd_ref[...].astype(BF16)

    o_ref[...] = (jnp.dot(hm_ref[...], wd_scr[...], preferred_element_type=F32) * gate_ref[...]).astype(o_ref.dtype)


def _expert_ffn(xs, gates, w_gate, w_up, w_down, layer):
    e, t, d = xs.shape
    f = w_gate.shape[-1]
    tm = min(t, 256)
    tm_down = min(t, 1024)
    hm = pl.pallas_call(
        _ffn_up_kernel,
        grid=(e, t // tm),
        in_specs=[pl.BlockSpec((None, tm, d), lambda ee, i: (ee, i, 0)),
                  pl.BlockSpec((None, None, d, f), lambda ee, i: (layer, ee, 0, 0)),
                  pl.BlockSpec((None, None, d, f), lambda ee, i: (layer, ee, 0, 0))],
        out_specs=pl.BlockSpec((None, tm, f), lambda ee, i: (ee, i, 0)),
        out_shape=jax.ShapeDtypeStruct((e, t, f), BF16),
        scratch_shapes=[pltpu.VMEM((d, f), BF16), pltpu.VMEM((d, f), BF16)],
        compiler_params=_cparams(("parallel", "arbitrary"), VMEM_MIB_XLARGE),
        name="expert_ffn_up",
    )(xs, w_gate, w_up)
    return pl.pallas_call(
        _ffn_down_kernel,
        grid=(e, t // tm_down),
        in_specs=[pl.BlockSpec((None, tm_down, f), lambda ee, i: (ee, i, 0)),
                  pl.BlockSpec((None, tm_down, 1), lambda ee, i: (ee, i, 0)),
                  pl.BlockSpec((None, None, f, d), lambda ee, i: (layer, ee, 0, 0))],
        out_specs=pl.BlockSpec((None, tm_down, d), lambda ee, i: (ee, i, 0)),
        out_shape=jax.ShapeDtypeStruct((e, t, d), BF16),
        scratch_shapes=[pltpu.VMEM((f, d), BF16)],
        compiler_params=_cparams(("parallel", "arbitrary"), VMEM_MIB_XLARGE),
        name="expert_ffn_down",
    )(hm, gates, w_down)


COMBINE_TOKENS = 256
COMBINE_ROWS = 512


def _combine_kernel(tile_ref, win_ref, flag_ref, x_ref, yp_ref, tok_ref, gate_ref, gn_ref, o_ref, acc_scr, *,
                    final_norm):
    s = pl.program_id(0)
    flags = flag_ref[s]
    t = acc_scr.shape[0]

    @pl.when((flags & 1) != 0)
    def _():
        acc_scr[...] = jnp.zeros(acc_scr.shape, F32)

    @pl.when((flags & 2) != 0)
    def _():
        tcol = tile_ref[s] * t + lax.broadcasted_iota(jnp.int32, (t, 1), 0)
        onehot = jnp.where(tcol == tok_ref[...], 1.0, 0.0).astype(BF16)
        acc_scr[...] += jnp.dot(onehot, yp_ref[...], preferred_element_type=F32)

    @pl.when((flags & 4) != 0)
    def _():
        y = x_ref[...] + gate_ref[...] * acc_scr[...]
        if final_norm:
            y = _rms(y) * gn_ref[...]
        o_ref[...] = y


def _moe_combine(x, y_rows, tok_rows, gate, norm_g, final_norm):
    b, n, d = x.shape
    r = y_rows.shape[0]
    t = min(COMBINE_TOKENS, n)
    w = COMBINE_ROWS
    ntiles = b * n // t
    nwin_total = r // w
    tiles_per_batch = n // t
    n_steps = nwin_total + 2 * ntiles
    i32 = jnp.int32
    tok_sorted, order = lax.sort_key_val(tok_rows.astype(i32), jnp.arange(r, dtype=i32))
    yp = jnp.take(y_rows, order, axis=0, mode="clip")
    edges = jnp.arange(ntiles + 1, dtype=i32) * t
    bounds = jnp.sum((tok_sorted[None, :] < edges[:, None]).astype(i32), axis=1)
    w_lo = jnp.minimum(bounds[:-1] // w, nwin_total - 1)
    w_hi = jnp.maximum((bounds[1:] + w - 1) // w, w_lo + 1)
    nwin = w_hi - w_lo
    step0 = jnp.cumsum(nwin) - nwin
    total = jnp.sum(nwin)
    s = jnp.arange(n_steps, dtype=i32)
    tile_s = jnp.sum((s[:, None] >= step0[None, :]).astype(i32), axis=1) - 1
    sel = tile_s[:, None] == jnp.arange(ntiles, dtype=i32)[None, :]

    def of_tile(v):
        return jnp.sum(jnp.where(sel, v[None, :], 0), axis=1)

    step0_s, w_lo_s, nwin_s = of_tile(step0), of_tile(w_lo), of_tile(nwin)
    win_s = jnp.minimum(w_lo_s + (s - step0_s), w_lo_s + nwin_s - 1)
    valid = s < total
    first = valid & (s == step0_s)
    last = valid & (s == step0_s + nwin_s - 1)
    flags = first.astype(i32) + 2 * valid.astype(i32) + 4 * last.astype(i32)
    out = pl.pallas_call(
        functools.partial(_combine_kernel, final_norm=final_norm),
        grid_spec=pltpu.PrefetchScalarGridSpec(
            num_scalar_prefetch=3,
            grid=(n_steps,),
            in_specs=[pl.BlockSpec((t, d), lambda s, tl, wn, fl: (tl[s], 0)),
                      pl.BlockSpec((w, d), lambda s, tl, wn, fl: (wn[s], 0)),
                      pl.BlockSpec((None, 1, w), lambda s, tl, wn, fl: (wn[s], 0, 0)),
                      pl.BlockSpec((None, 1, d), lambda s, tl, wn, fl: (tl[s] // tiles_per_batch, 0, 0)),
                      pl.BlockSpec((1, d), lambda s, tl, wn, fl: (0, 0))],
            out_specs=pl.BlockSpec((t, d), lambda s, tl, wn, fl: (tl[s], 0)),
            scratch_shapes=[pltpu.VMEM((t, d), F32)]),
        out_shape=jax.ShapeDtypeStruct((b * n, d), F32),
        compiler_params=_cparams(("arbitrary",), VMEM_MIB_MEDIUM),
        name="moe_combine",
    )(tile_s, win_s, flags, x.reshape(b * n, d), yp, tok_sorted.reshape(nwin_total, 1, w), gate,
      norm_g.reshape(1, d))
    return out.reshape(b, n, d)


def _moe_select(aff):
    b, n, _ = aff.shape
    cap = EC_CAPACITY * n // N_EXPERTS
    gate, idx = lax.top_k(jnp.swapaxes(aff[..., :N_EXPERTS], 1, 2), cap)
    flat = (idx + (jnp.arange(b, dtype=idx.dtype) * n)[:, None, None]).transpose(1, 0, 2).reshape(-1)
    return flat, gate.transpose(1, 0, 2).reshape(N_EXPERTS, b * cap, 1)


def _moe_gather(h, flat):
    b, n, d = h.shape
    return jnp.take(h.reshape(b * n, d), flat, axis=0, mode="clip").reshape(N_EXPERTS, -1, d)


def _moe_experts(x, xs, gates, flat, gate2, experts, norm_g, final_norm):
    y = _expert_ffn(xs, gates, *experts)
    return _moe_combine(x, y.reshape(-1, x.shape[-1]), flat, gate2, norm_g, final_norm)


_ROPE_SRC = np.concatenate([np.arange(16, 32), np.arange(0, 16), np.arange(48, 64), np.arange(32, 48)])
_ROPE_SIGN = np.concatenate([-np.ones(16), np.ones(16), -np.ones(16), np.ones(16)]).astype(np.float32)


def _rope_perm_cols(w):
    return w[:, _ROPE_SRC] * _ROPE_SIGN


def _pad_w_in(w_in):
    c_kr = MLA_Q_LORA + MLA_KV_LORA
    kr = w_in[:, c_kr:c_kr + MLA_ROPE_DIM]
    return jnp.concatenate([w_in[:, :c_kr + MLA_ROPE_DIM], _rope_perm_cols(kr), w_in[:, c_kr + MLA_ROPE_DIM:]],
                           axis=1).astype(BF16)


def _expand_w_uq(w_uq):
    w = w_uq.reshape(MLA_Q_LORA, N_HEADS_MLA, HEAD_DIM + MLA_ROPE_DIM)
    nope = w[:, :, :HEAD_DIM]
    rope = w[:, :, HEAD_DIM:]
    partner = rope[:, :, _ROPE_SRC] * _ROPE_SIGN
    return jnp.concatenate([nope, rope, partner], axis=-1).reshape(MLA_Q_LORA, N_HEADS_MLA * MLA_QK_PAD).astype(BF16)


def _rope_tables(n):
    t = jnp.arange(n)
    n_freq = MLA_ROPE_DIM // 4
    inv = ROPE_BASE ** (-jnp.arange(n_freq, dtype=F32) / n_freq)
    ang_r = (t // GRID_W).astype(F32)[:, None] * inv
    ang_c = (t % GRID_W).astype(F32)[:, None] * inv
    zeros = jnp.zeros((n, LANE - MLA_ROPE_DIM), F32)
    cos = jnp.concatenate([jnp.cos(ang_r), jnp.cos(ang_r), jnp.cos(ang_c), jnp.cos(ang_c), zeros], axis=1)
    sin = jnp.concatenate([jnp.sin(ang_r), jnp.sin(ang_r), jnp.sin(ang_c), jnp.sin(ang_c), zeros], axis=1)
    return cos, sin


def _identity_tables(n):
    ones = jnp.ones((n, MLA_ROPE_DIM), F32)
    zeros = jnp.zeros((n, LANE - MLA_ROPE_DIM), F32)
    return jnp.concatenate([ones, zeros], axis=1), jnp.zeros((n, LANE), F32)


def _layer(x_l, x_c, mod, lp, last, final_g):
    b, n, d = x_l.shape
    n_ctx = x_c.shape[1]

    def chunk(k, ctx):
        rows = jnp.broadcast_to(mod[b, k * d:(k + 1) * d], (b, d)) if ctx else mod[:b, k * d:(k + 1) * d]
        return rows.reshape(b, 1, d)

    w_in = _pad_w_in(lp['w_in'])
    p_l = _norm_proj(x_l, lp['norm1_g'], chunk(0, False), chunk(1, False), w_in)
    p_c = _norm_proj(x_c, lp['norm1_g'], chunk(0, True), chunk(1, True), w_in)

    cos_l, sin_l = _rope_tables(n)
    cos_c, sin_c = _identity_tables(n_ctx)
    w_uk = lp['mla_w_uk'].reshape(MLA_KV_LORA, -1).astype(BF16)
    w_uv_t = lp['mla_w_uv'].reshape(MLA_KV_LORA, -1).T.astype(BF16)
    w_q_t = _expand_w_uq(lp['mla_w_uq']).T
    k_l, v_l = _mla_kv_prep(p_l, lp['mla_kv_norm_g'], w_uk, w_uv_t, cos_l, sin_l)
    k_c, v_c = _mla_kv_prep(p_c, lp['mla_kv_norm_g'], w_uk, w_uv_t, cos_c, sin_c)
    q_l = _mla_q_prep(p_l, lp['mla_q_norm_g'], w_q_t, cos_l.T, sin_l.T)
    a_l = _attention_t(q_l, [(k_c, v_c), (k_l, v_l)], N_HEADS_MLA, MLA_QK_PAD, HEAD_DIM)

    b_l = _nat_latent(p_l, p_c, _nat_bias_table(lp['nat_rpb']))
    r_c, r_l = _retention(p_c, p_l, lp['ret_decay_f'], lp['ret_decay_b'])

    w_out = lp['w_out'].astype(BF16)
    w_router = jnp.zeros((d, LANE), F32).at[:, :N_EXPERTS].set(lp['w_router'])

    x_l, h_l, aff_l = _merge_out(x_l, a_l, b_l, r_l, lp['out_norm_g'], chunk(2, False), w_out,
                                 lp['norm2_g'], chunk(3, False), chunk(4, False), w_router)
    flat_l, gates_l = _moe_select(aff_l)
    if last:
        xs_l = _moe_gather(h_l, flat_l)
        return _moe_experts(x_l, xs_l, gates_l, flat_l, chunk(5, False), lp['experts'], final_g, True), None

    flat_l, p_c, x_c = lax.optimization_barrier((flat_l, p_c, x_c))
    xs_l = _moe_gather(h_l, flat_l)
    q_c = _mla_q_prep(p_c, lp['mla_q_norm_g'], w_q_t, cos_c.T, sin_c.T)
    a_c = _attention_t(q_c, [(k_c, v_c)], N_HEADS_MLA, MLA_QK_PAD, HEAD_DIM)
    b_c = _attention(p_c, COL_NAT_Q, HEAD_DIM, [(p_c, COL_NAT_K, p_c, COL_NAT_V)], HEAD_DIM, N_HEADS_NAT,
                     HEAD_DIM ** -0.5)
    x_c, h_c, aff_c = _merge_out(x_c, a_c, b_c, r_c, lp['out_norm_g'], chunk(2, True), w_out,
                                 lp['norm2_g'], chunk(3, True), chunk(4, True), w_router)
    flat_c, gates_c = _moe_select(aff_c)
    x_c = _moe_experts(x_c, _moe_gather(h_c, flat_c), gates_c, flat_c, chunk(5, True), lp['experts'], final_g, False)
    xs_l, x_c = lax.optimization_barrier((xs_l, x_c))
    x_l = _moe_experts(x_l, xs_l, gates_l, flat_l, chunk(5, False), lp['experts'], final_g, False)
    return x_l, x_c


def kernel(x, c, ctx, c_ctx, w_mod, b_mod, norm1_g, w_in, mla_q_norm_g, mla_kv_norm_g, mla_w_uq, mla_w_uk,
           mla_w_uv, nat_rpb, ret_decay_f, ret_decay_b, out_norm_g, w_out, norm2_g, w_router, w_gate, w_up,
           w_down, final_norm_g):
    depth = w_mod.shape[0]
    cvecs = jnp.concatenate([c, c_ctx[None, :]], axis=0)
    x_l, x_c = x, ctx
    for i in range(depth):
        lp = {
            'norm1_g': norm1_g[i], 'w_in': w_in[i], 'mla_q_norm_g': mla_q_norm_g[i],
            'mla_kv_norm_g': mla_kv_norm_g[i], 'mla_w_uq': mla_w_uq[i], 'mla_w_uk': mla_w_uk[i],
            'mla_w_uv': mla_w_uv[i], 'nat_rpb': nat_rpb[i], 'ret_decay_f': ret_decay_f[i],
            'ret_decay_b': ret_decay_b[i], 'out_norm_g': out_norm_g[i], 'w_out': w_out[i],
            'norm2_g': norm2_g[i], 'w_router': w_router[i], 'experts': (w_gate, w_up, w_down, i),
        }
        mod = _modulation(cvecs, w_mod, b_mod[i], i)
        x_l, x_c = _layer(x_l, x_c, mod, lp, i == depth - 1, final_norm_g)
    return x_l
```

```python
import functools

import numpy as np
import jax
import jax.numpy as jnp
from jax import lax
from jax.experimental import pallas as pl
from jax.experimental.pallas import tpu as pltpu

F32 = jnp.float32
BF16 = jnp.bfloat16

GRID_W = 64
HEAD_DIM = 128
N_HEADS_MLA = 8
N_HEADS_NAT = 4
N_HEADS_RET = 4
MLA_ROPE_DIM = 64
MLA_Q_LORA = 512
MLA_KV_LORA = 256
MLA_SCALE = (HEAD_DIM + MLA_ROPE_DIM) ** -0.5
LOG2E = 1.4426950408889634
MLA_Q_SCALE = MLA_SCALE * LOG2E
MLA_QK_PAD = 256
NAT_ROWS = 8
NAT_COLS = 16
RET_CHUNK = 128
N_EXPERTS = 16
EC_CAPACITY = 2
ROPE_BASE = 10000.0
EPS = 1e-6
NEG_BIG = -1e30
LANE = 128

COL_CQ = 0
COL_CKV = 4
COL_KR = 6
COL_NAT_Q = 7
COL_NAT_K = 11
COL_NAT_V = 15
COL_RET_Q = 19
COL_RET_K = 23
COL_RET_V = 27
COL_RET_GF = 31
COL_RET_GB = 35
IN_COLS_PAD = 39 * LANE


VMEM_MIB_SMALL = 32
VMEM_MIB_MEDIUM = 40
VMEM_MIB_LARGE = 48
VMEM_MIB_XLARGE = 56
VMEM_MIB_MAX = 58


def _cparams(sem, vmem_mib):
    return pltpu.CompilerParams(dimension_semantics=sem, vmem_limit_bytes=vmem_mib * 1024 * 1024)


def _silu(a):
    return a * (1.0 / (1.0 + jnp.exp(-a)))


def _rms(x):
    return x * lax.rsqrt(jnp.mean(x * x, axis=-1, keepdims=True) + EPS)


def _mod_kernel(ct_ref, w_ref, b_ref, o_ref, a_scr, *, n_rows, k_chunk):
    ct = ct_ref[...]
    a_scr[...] = _silu(ct)
    d = w_ref.shape[0]
    tn = w_ref.shape[1]

    def body(kc, accs):
        k0 = pl.multiple_of(kc * k_chunk, k_chunk)
        wblk = w_ref[pl.ds(k0, k_chunk), :]
        out = []
        for r in range(n_rows):
            col = a_scr[pl.ds(k0, k_chunk), r:r + 1]
            out.append(accs[r] + jnp.sum((col * wblk).reshape(k_chunk // 8, 8, tn), axis=0))
        return tuple(out)

    accs = lax.fori_loop(0, d // k_chunk, body, tuple(jnp.zeros((8, tn), F32) for _ in range(n_rows)))
    o_ref[...] = jnp.zeros(o_ref.shape, F32)
    for r in range(n_rows):
        o_ref[r:r + 1, :] = jnp.sum(accs[r], axis=0, keepdims=True) + b_ref[...]


def _modulation(cvecs, w_mod, b_mod, layer):
    n_rows, d = cvecs.shape
    n_out = w_mod.shape[-1]
    tn = 1024
    ct = jnp.zeros((d, 8), F32).at[:, :n_rows].set(cvecs.T)
    return pl.pallas_call(
        functools.partial(_mod_kernel, n_rows=n_rows, k_chunk=64),
        grid=(n_out // tn,),
        in_specs=[pl.BlockSpec((d, 8), lambda j: (0, 0)),
                  pl.BlockSpec((None, d, tn), lambda j: (layer, 0, j)),
                  pl.BlockSpec((1, tn), lambda j: (0, j))],
        out_specs=pl.BlockSpec((8, tn), lambda j: (0, j)),
        out_shape=jax.ShapeDtypeStruct((8, n_out), F32),
        scratch_shapes=[pltpu.VMEM((d, 8), F32)],
        compiler_params=_cparams(("parallel",), VMEM_MIB_MEDIUM),
        name="modulation",
    )(ct, w_mod, b_mod.reshape(1, n_out))


NORM_PROJ_PARTS = 2


def _norm_proj_kernel(x_ref, g_ref, sh_ref, sc_ref, w_ref, o_ref, h_scr):
    j = pl.program_id(2)
    pr = x_ref.shape[0] // NORM_PROJ_PARTS
    parts = [slice(k * pr, (k + 1) * pr) for k in range(NORM_PROJ_PARTS)]

    @pl.when(j == 0)
    def _():
        hs = []
        for rows in parts:
            y = _rms(x_ref[rows, :]) * g_ref[...]
            h = (y * (1.0 + sc_ref[...]) + sh_ref[...]).astype(BF16)
            h_scr[rows, :] = h
            hs.append(h)
        for rows, h in zip(parts, hs):
            o_ref[rows, :] = jnp.dot(h, w_ref[...], preferred_element_type=F32).astype(o_ref.dtype)

    @pl.when(j != 0)
    def _():
        o_ref[...] = jnp.dot(h_scr[...], w_ref[...], preferred_element_type=F32).astype(o_ref.dtype)


def _norm_proj(x, g, shift, scale, w):
    b, n, d = x.shape
    c = w.shape[1]
    tm = min(n, 1024)
    tn = c // 3 if (c % (3 * LANE) == 0) else c
    return pl.pallas_call(
        _norm_proj_kernel,
        grid=(b, n // tm, c // tn),
        in_specs=[pl.BlockSpec((None, tm, d), lambda bb, i, j: (bb, i, 0)),
                  pl.BlockSpec((1, d), lambda bb, i, j: (0, 0)),
                  pl.BlockSpec((None, 1, d), lambda bb, i, j: (bb, 0, 0)),
                  pl.BlockSpec((None, 1, d), lambda bb, i, j: (bb, 0, 0)),
                  pl.BlockSpec((d, tn), lambda bb, i, j: (0, j))],
        out_specs=pl.BlockSpec((None, tm, tn), lambda bb, i, j: (bb, i, j)),
        out_shape=jax.ShapeDtypeStruct((b, n, c), BF16),
        scratch_shapes=[pltpu.VMEM((tm, d), BF16)],
        compiler_params=_cparams(("parallel", "parallel", "arbitrary"), VMEM_MIB_XLARGE),
        name="norm_proj",
    )(x, g.reshape(1, d), shift, scale, w)


def _rope_rotate(r, cos, sin):
    return r * cos + pltpu.roll(r, 64, 1) * sin


def _mla_q_kernel(cq_ref, g_ref, wt_ref, cos_ref, sin_ref, o_ref, *, n_heads):
    cqn_t = (_rms(cq_ref[...].astype(F32)) * g_ref[...]).T.astype(BF16)
    qe = jnp.dot(wt_ref[...], cqn_t, preferred_element_type=F32)
    cos = cos_ref[...]
    sin = sin_ref[...]
    for h in range(n_heads):
        c0 = h * MLA_QK_PAD
        o_ref[c0:c0 + LANE, :] = (qe[c0:c0 + LANE, :] * MLA_Q_SCALE).astype(BF16)
        r = qe[c0 + LANE:c0 + 2 * LANE, :]
        swapped = jnp.concatenate([r[MLA_ROPE_DIM:, :], r[:MLA_ROPE_DIM, :]], axis=0)
        o_ref[c0 + LANE:c0 + 2 * LANE, :] = ((r * cos + swapped * sin) * MLA_Q_SCALE).astype(BF16)


def _mla_q_prep(p, g, w_q_t, cos_t, sin_t):
    b, n, _ = p.shape
    tm = min(n, 512)
    cw = N_HEADS_MLA * MLA_QK_PAD
    return pl.pallas_call(
        functools.partial(_mla_q_kernel, n_heads=N_HEADS_MLA),
        grid=(b, n // tm),
        in_specs=[pl.BlockSpec((None, tm, MLA_Q_LORA), lambda bb, i: (bb, i, COL_CQ * LANE // MLA_Q_LORA)),
                  pl.BlockSpec((1, MLA_Q_LORA), lambda bb, i: (0, 0)),
                  pl.BlockSpec((cw, MLA_Q_LORA), lambda bb, i: (0, 0)),
                  pl.BlockSpec((LANE, tm), lambda bb, i: (0, i)),
                  pl.BlockSpec((LANE, tm), lambda bb, i: (0, i))],
        out_specs=pl.BlockSpec((None, cw, tm), lambda bb, i: (bb, 0, i)),
        out_shape=jax.ShapeDtypeStruct((b, cw, n), BF16),
        compiler_params=_cparams(("parallel", "parallel"), VMEM_MIB_MEDIUM),
        name="mla_q_prep",
    )(p, g.reshape(1, MLA_Q_LORA), w_q_t, cos_t, sin_t)


def _mla_kv_kernel(ckv_ref, kr_ref, g_ref, wuk_ref, wuvt_ref, cos_ref, sin_ref, k_ref, v_ref, *, n_heads):
    ckvn = _rms(ckv_ref[...].astype(F32)) * g_ref[...]
    kn = jnp.dot(ckvn.astype(BF16), wuk_ref[...], preferred_element_type=F32)
    v_ref[...] = jnp.dot(wuvt_ref[...], ckvn.T.astype(BF16), preferred_element_type=F32).astype(BF16)
    rot = _rope_rotate(kr_ref[...].astype(F32), cos_ref[...], sin_ref[...]).astype(BF16)
    for h in range(n_heads):
        c0 = h * MLA_QK_PAD
        k_ref[:, c0:c0 + LANE] = kn[:, h * LANE:(h + 1) * LANE].astype(BF16)
        k_ref[:, c0 + LANE:c0 + 2 * LANE] = rot


def _mla_kv_prep(p, g, w_uk, w_uv_t, cos, sin):
    b, n, _ = p.shape
    tm = min(n, 512)
    kw = N_HEADS_MLA * MLA_QK_PAD
    vw = N_HEADS_MLA * HEAD_DIM
    return pl.pallas_call(
        functools.partial(_mla_kv_kernel, n_heads=N_HEADS_MLA),
        grid=(b, n // tm),
        in_specs=[pl.BlockSpec((None, tm, MLA_KV_LORA), lambda bb, i: (bb, i, COL_CKV * LANE // MLA_KV_LORA)),
                  pl.BlockSpec((None, tm, LANE), lambda bb, i: (bb, i, COL_KR)),
                  pl.BlockSpec((1, MLA_KV_LORA), lambda bb, i: (0, 0)),
                  pl.BlockSpec((MLA_KV_LORA, vw), lambda bb, i: (0, 0)),
                  pl.BlockSpec((vw, MLA_KV_LORA), lambda bb, i: (0, 0)),
                  pl.BlockSpec((tm, LANE), lambda bb, i: (i, 0)),
                  pl.BlockSpec((tm, LANE), lambda bb, i: (i, 0))],
        out_specs=[pl.BlockSpec((None, tm, kw), lambda bb, i: (bb, i, 0)),
                   pl.BlockSpec((None, vw, tm), lambda bb, i: (bb, 0, i))],
        out_shape=[jax.ShapeDtypeStruct((b, n, kw), BF16), jax.ShapeDtypeStruct((b, vw, n), BF16)],
        compiler_params=_cparams(("parallel", "parallel"), VMEM_MIB_MEDIUM),
        name="mla_kv_prep",
    )(p, p, g.reshape(1, MLA_KV_LORA), w_uk, w_uv_t, cos, sin)


def _attn_kernel(*refs, n_src, scale, tk_max):
    q_ref = refs[0]
    o_ref = refs[1 + 2 * n_src]
    q = q_ref[...]
    if scale != 1.0:
        q = q.astype(F32) * scale
    q = q.astype(BF16)
    tq = q.shape[0]
    dv = o_ref.shape[-1]

    def step(k, v, carry):
        m, l, acc = carry
        s = lax.dot_general(q, k, (((1,), (1,)), ((), ())), preferred_element_type=F32)
        m_new = jnp.maximum(m, jnp.max(s, axis=-1, keepdims=True))
        p = jnp.exp(s - m_new)
        alpha = jnp.exp(m - m_new)
        l = alpha * l + jnp.sum(p, axis=-1, keepdims=True)
        acc = alpha * acc + jnp.dot(p.astype(BF16), v, preferred_element_type=F32)
        return m_new, l, acc

    carry = (jnp.full((tq, 1), NEG_BIG, F32), jnp.zeros((tq, 1), F32), jnp.zeros((tq, dv), F32))
    for s_i in range(n_src):
        k_ref = refs[1 + 2 * s_i]
        v_ref = refs[2 + 2 * s_i]
        nk = k_ref.shape[0]
        tk = min(nk, tk_max)
        if nk == tk:
            carry = step(k_ref[...].astype(BF16), v_ref[...].astype(BF16), carry)
        else:
            def body(c, cr, k_ref=k_ref, v_ref=v_ref, tk=tk):
                k0 = pl.multiple_of(c * tk, tk)
                return step(k_ref[pl.ds(k0, tk), :].astype(BF16), v_ref[pl.ds(k0, tk), :].astype(BF16), cr)
            carry = lax.fori_loop(0, nk // tk, body, carry)
    _, l, acc = carry
    o_ref[...] = acc / l


def _attention(q_arr, q_blk0, dq, sources, dv, n_heads, scale, tq=256, tk_max=512):
    b, nq, _ = q_arr.shape
    tq = min(nq, tq)
    in_specs = [pl.BlockSpec((None, tq, dq), lambda bb, h, i: (bb, i, q_blk0 + h))]
    args = [q_arr]
    for k_arr, k_blk0, v_arr, v_blk0 in sources:
        nk = k_arr.shape[1]
        in_specs.append(pl.BlockSpec((None, nk, dq), lambda bb, h, i, o=k_blk0: (bb, 0, o + h)))
        in_specs.append(pl.BlockSpec((None, nk, dv), lambda bb, h, i, o=v_blk0: (bb, 0, o + h)))
        args += [k_arr, v_arr]
    return pl.pallas_call(
        functools.partial(_attn_kernel, n_src=len(sources), scale=scale, tk_max=tk_max),
        grid=(b, n_heads, nq // tq),
        in_specs=in_specs,
        out_specs=pl.BlockSpec((None, tq, dv), lambda bb, h, i: (bb, i, h)),
        out_shape=jax.ShapeDtypeStruct((b, nq, n_heads * dv), F32),
        compiler_params=_cparams(("parallel", "parallel", "arbitrary"), VMEM_MIB_LARGE),
        name="attention",
    )(*args)


ATTN_TQ = 2048
ATTN_TK = 512
ATTN_GROUP = 4


def _attn_t_kernel(*refs, n_src):
    q_ref = refs[0]
    o_ref = refs[1 + 2 * n_src]
    q = q_ref[...]
    tq = q.shape[1]
    dv = o_ref.shape[-1]

    def scores(k):
        return jnp.dot(k, q, preferred_element_type=F32)

    def update(s, vt, carry):
        m, l, acc = carry
        m_new = jnp.maximum(m, jnp.max(s, axis=0, keepdims=True))
        p = jnp.exp2(s - m_new)
        alpha = jnp.exp2(m - m_new)
        l = alpha * l + jnp.sum(p, axis=0, keepdims=True)
        acc = alpha * acc + jnp.dot(vt, p.astype(BF16), preferred_element_type=F32)
        return m_new, l, acc

    carry = (jnp.full((1, tq), NEG_BIG, F32), jnp.zeros((1, tq), F32), jnp.zeros((dv, tq), F32))
    for s_i in range(n_src):
        k_ref = refs[1 + 2 * s_i]
        vt_ref = refs[2 + 2 * s_i]
        nk = k_ref.shape[0]
        tk = min(nk, ATTN_TK)
        n_chunks = nk // tk
        if n_chunks == 1:
            carry = update(scores(k_ref[...]), vt_ref[...], carry)
            continue
        group = max(g for g in (ATTN_GROUP, 2, 1) if n_chunks % g == 0)

        def body(c, cr, k_ref=k_ref, vt_ref=vt_ref, tk=tk, group=group):
            starts = [pl.multiple_of((c * group + g) * tk, tk) for g in range(group)]
            ss = [scores(k_ref[pl.ds(k0, tk), :]) for k0 in starts]
            for s, k0 in zip(ss, starts):
                cr = update(s, vt_ref[:, pl.ds(k0, tk)], cr)
            return cr
        carry = lax.fori_loop(0, n_chunks // group, body, carry)
    _, l, acc = carry
    o_ref[...] = (acc / l).T


def _attention_t(q_arr, sources, n_heads, dq, dv):
    b, _, nq = q_arr.shape
    tq = min(nq, ATTN_TQ)
    in_specs = [pl.BlockSpec((None, dq, tq), lambda bb, h, i: (bb, h, i))]
    args = [q_arr]
    for k_arr, vt_arr in sources:
        nk = k_arr.shape[1]
        in_specs.append(pl.BlockSpec((None, nk, dq), lambda bb, h, i: (bb, 0, h)))
        in_specs.append(pl.BlockSpec((None, dv, nk), lambda bb, h, i: (bb, h, 0)))
        args += [k_arr, vt_arr]
    return pl.pallas_call(
        functools.partial(_attn_t_kernel, n_src=len(sources)),
        grid=(b, n_heads, nq // tq),
        in_specs=in_specs,
        out_specs=pl.BlockSpec((None, tq, dv), lambda bb, h, i: (bb, i, h)),
        out_shape=jax.ShapeDtypeStruct((b, nq, n_heads * dv), F32),
        compiler_params=_cparams(("parallel", "parallel", "arbitrary"), VMEM_MIB_XLARGE),
        name="attention_t",
    )(*args)


NAT_GROUP = 8
NAT_SLAB = NAT_GROUP + NAT_ROWS
NAT_HALVES = 2


def _nat_kernel(q_ref, k_ref, v_ref, kc_ref, vc_ref, bias_ref, o_ref, *, n_rows):
    i = pl.program_id(2)
    scale = HEAD_DIM ** -0.5
    nt = (((1,), (1,)), ((), ()))
    u0 = jnp.clip(i * NAT_GROUP - NAT_ROWS // 2, 0, n_rows - NAT_SLAB)
    t0 = pl.multiple_of(u0 * GRID_W, GRID_W)
    ks = k_ref[pl.ds(t0, NAT_SLAB * GRID_W), :].astype(BF16)
    vs = v_ref[pl.ds(t0, NAT_SLAB * GRID_W), :].astype(BF16)
    kc = kc_ref[...].astype(BF16)
    vc = vc_ref[...].astype(BF16)
    hq = q_ref.shape[0] // NAT_HALVES
    scores = []
    for j in range(NAT_HALVES):
        rows = slice(j * hq, (j + 1) * hq)
        q = (q_ref[rows, :].astype(F32) * scale).astype(BF16)
        scores.append((lax.dot_general(q, ks, nt, preferred_element_type=F32) + bias_ref[rows, :],
                       lax.dot_general(q, kc, nt, preferred_element_type=F32)))
    for j in range(NAT_HALVES):
        s_w, s_c = scores[j]
        m = jnp.maximum(jnp.max(s_w, axis=-1, keepdims=True), jnp.max(s_c, axis=-1, keepdims=True))
        p_w = jnp.exp(s_w - m)
        p_c = jnp.exp(s_c - m)
        l = jnp.sum(p_w, axis=-1, keepdims=True) + jnp.sum(p_c, axis=-1, keepdims=True)
        o = (jnp.dot(p_w.astype(BF16), vs, preferred_element_type=F32)
             + jnp.dot(p_c.astype(BF16), vc, preferred_element_type=F32))
        o_ref[j * hq:(j + 1) * hq, :] = o / l


def _nat_bias_table(rpb):
    h = rpb.shape[0]
    col = np.arange(GRID_W)
    c0 = np.clip(col - NAT_COLS // 2, 0, GRID_W - NAT_COLS)
    kc = np.arange(GRID_W)
    col_ok = (kc[None, :] >= c0[:, None]) & (kc[None, :] < c0[:, None] + NAT_COLS)
    dc = kc[None, :] - col[:, None] + NAT_COLS - 1
    sel_c = ((dc[None] == np.arange(2 * NAT_COLS - 1)[:, None, None]) & col_ok[None]).astype(np.float32)
    sel_c = sel_c.reshape(2 * NAT_COLS - 1, GRID_W * GRID_W)
    per_dr = jnp.einsum('hdc,cb->hdb', rpb.astype(F32), sel_c, precision=lax.Precision.HIGHEST)
    per_dr = jnp.where(col_ok.reshape(-1)[None, None], per_dr, NEG_BIG).reshape(h, 2 * NAT_ROWS - 1, GRID_W, GRID_W)
    return pl.pallas_call(
        _nat_table_kernel,
        grid=(h, 3),
        in_specs=[pl.BlockSpec((None, 2 * NAT_ROWS - 1, GRID_W, GRID_W), lambda hh, v: (hh, 0, 0, 0))],
        out_specs=pl.BlockSpec((None, None, NAT_GROUP * GRID_W, NAT_SLAB * GRID_W), lambda hh, v: (hh, v, 0, 0)),
        out_shape=jax.ShapeDtypeStruct((h, 3, NAT_GROUP * GRID_W, NAT_SLAB * GRID_W), F32),
        compiler_params=_cparams(("parallel", "arbitrary"), VMEM_MIB_SMALL),
        name="nat_bias_table",
    )(per_dr)


def _nat_table_kernel(per_dr_ref, o_ref):
    masked = jnp.full((GRID_W, GRID_W), NEG_BIG, F32)
    for variant, delta in enumerate((0, NAT_ROWS // 2, NAT_ROWS)):
        @pl.when(pl.program_id(1) == variant)
        def _(delta=delta):
            for g in range(NAT_GROUP):
                w0 = min(max(g + delta - NAT_ROWS // 2, 0), NAT_SLAB - NAT_ROWS)
                tiles = [per_dr_ref[u - (g + delta) + NAT_ROWS - 1] if w0 <= u < w0 + NAT_ROWS else masked
                         for u in range(NAT_SLAB)]
                o_ref[g * GRID_W:(g + 1) * GRID_W, :] = jnp.concatenate(tiles, axis=1)


def _nat_latent(p_l, p_c, bias_tbl):
    b, n, _ = p_l.shape
    n_ctx = p_c.shape[1]
    n_rows = n // GRID_W
    assert n_rows >= NAT_SLAB and n_rows % NAT_GROUP == 0
    n_groups = n_rows // NAT_GROUP
    tq = NAT_GROUP * GRID_W
    tk = NAT_SLAB * GRID_W

    def variant(i):
        return jnp.where(i == 0, 0, jnp.where(i == n_groups - 1, 2, 1))

    return pl.pallas_call(
        functools.partial(_nat_kernel, n_rows=n_rows),
        grid=(b, N_HEADS_NAT, n_groups),
        in_specs=[pl.BlockSpec((None, tq, LANE), lambda bb, h, i: (bb, i, COL_NAT_Q + h)),
                  pl.BlockSpec((None, n, LANE), lambda bb, h, i: (bb, 0, COL_NAT_K + h)),
                  pl.BlockSpec((None, n, LANE), lambda bb, h, i: (bb, 0, COL_NAT_V + h)),
                  pl.BlockSpec((None, n_ctx, LANE), lambda bb, h, i: (bb, 0, COL_NAT_K + h)),
                  pl.BlockSpec((None, n_ctx, LANE), lambda bb, h, i: (bb, 0, COL_NAT_V + h)),
                  pl.BlockSpec((None, None, tq, tk), lambda bb, h, i: (h, variant(i), 0, 0))],
        out_specs=pl.BlockSpec((None, tq, LANE), lambda bb, h, i: (bb, i, h)),
        out_shape=jax.ShapeDtypeStruct((b, n, N_HEADS_NAT * HEAD_DIM), F32),
        compiler_params=_cparams(("parallel", "parallel", "arbitrary"), VMEM_MIB_LARGE),
        name="nat_latent",
    )(p_l, p_l, p_l, p_c, p_c, bias_tbl)


def _ret_consts(logit, reverse):
    c_len = RET_CHUNK
    z = jnp.full((c_len, c_len), -logit, F32)
    lg = -(jnp.maximum(z, 0.0) + jnp.log1p(jnp.exp(-jnp.abs(z))))
    ci = lax.broadcasted_iota(jnp.int32, (c_len, c_len), 0).astype(F32)
    si = lax.broadcasted_iota(jnp.int32, (c_len, c_len), 1).astype(F32)
    if reverse:
        rel = si - ci
        wq = jnp.exp(lg * (c_len - ci))
        wk = jnp.exp(lg * ci)
    else:
        rel = ci - si
        wq = jnp.exp(lg * (ci + 1.0))
        wk = jnp.exp(lg * (c_len - 1.0 - ci))
    dmat = jnp.where(rel >= 0, jnp.exp(lg * jnp.maximum(rel, 0.0)), 0.0)
    return wq, wk, dmat, jnp.exp(lg * c_len)


def _ret_kernel(lf_ref, lb_ref, qf_ref, kf_ref, vf_ref, gf_ref, qb_ref, kb_ref, vb_ref, gb_ref, s0f_ref, s0b_ref,
                yf_ref, yb_ref, sf_ref, sb_ref, c_scr, *, n_chunks):
    h = pl.program_id(1)

    @pl.when(pl.program_id(2) == 0)
    def _():
        sf_ref[...] = s0f_ref[...]
        sb_ref[...] = s0b_ref[...]
        for j, arr in enumerate(_ret_consts(lf_ref[h], False) + _ret_consts(lb_ref[h], True)):
            c_scr[j] = arr

    c_len = RET_CHUNK
    nt = (((1,), (1,)), ((), ()))
    k_scale = HEAD_DIM ** -0.5
    dirs = []
    for d, refs in enumerate(((qf_ref, kf_ref, vf_ref, gf_ref, yf_ref), (qb_ref, kb_ref, vb_ref, gb_ref, yb_ref))):
        order = list(range(n_chunks)) if d == 0 else list(range(n_chunks - 1, -1, -1))
        dirs.append(refs + tuple(c_scr[4 * d + j] for j in range(4)) + (order,))

    units = []
    for step in range(n_chunks):
        for d in range(2):
            q_ref, k_ref, v_ref, _, _, _, wk, dmat, _, order = dirs[d]
            sl = slice(order[step] * c_len, (order[step] + 1) * c_len)
            qb = q_ref[sl, :].astype(BF16)
            kb = k_ref[sl, :]
            vb = v_ref[sl, :].astype(BF16)
            sc = lax.dot_general(qb, kb.astype(BF16), nt, preferred_element_type=F32) * (dmat * k_scale)
            kv = jnp.dot((kb.astype(F32) * (wk * k_scale)).T.astype(BF16), vb, preferred_element_type=F32)
            units.append((d, sl, qb, vb, sc, kv))
    states = [sf_ref[...], sb_ref[...]]
    cross = []
    for (d, sl, qb, vb, sc, kv) in units:
        wq, g_chunk = dirs[d][5], dirs[d][8]
        cross.append(wq * jnp.dot(qb, states[d].astype(BF16), preferred_element_type=F32))
        states[d] = g_chunk * states[d] + kv
    sf_ref[...] = states[0]
    sb_ref[...] = states[1]
    for (d, sl, qb, vb, sc, kv), o_cross in zip(units, cross):
        g_ref, y_ref = dirs[d][3], dirs[d][4]
        o = jnp.dot(sc.astype(BF16), vb, preferred_element_type=F32) + o_cross
        mu = jnp.mean(o, axis=-1, keepdims=True)
        dev = o - mu
        gn = dev * lax.rsqrt(jnp.mean(dev * dev, axis=-1, keepdims=True) + EPS)
        y_ref[sl, :] = _silu(g_ref[sl, :].astype(F32)) * gn


def _retention_scan(p, logit_f, logit_b, s0_f, s0_b):
    b, n, _ = p.shape
    hh = N_HEADS_RET
    tb = min(n, 16 * RET_CHUNK)
    nblk = n // tb

    def fwd(c0):
        return pl.BlockSpec((None, tb, LANE), lambda bb, h, i: (bb, i, c0 + h))

    def bwd(c0):
        return pl.BlockSpec((None, tb, LANE), lambda bb, h, i: (bb, nblk - 1 - i, c0 + h))

    state = pl.BlockSpec((None, None, HEAD_DIM, HEAD_DIM), lambda bb, h, i: (bb, h, 0, 0))
    smem = pl.BlockSpec(memory_space=pltpu.SMEM)
    y_shape = jax.ShapeDtypeStruct((b, n, hh * HEAD_DIM), F32)
    s_shape = jax.ShapeDtypeStruct((b, hh, HEAD_DIM, HEAD_DIM), F32)
    return pl.pallas_call(
        functools.partial(_ret_kernel, n_chunks=tb // RET_CHUNK),
        grid=(b, hh, nblk),
        in_specs=[smem, smem,
                  fwd(COL_RET_Q), fwd(COL_RET_K), fwd(COL_RET_V), fwd(COL_RET_GF),
                  bwd(COL_RET_Q), bwd(COL_RET_K), bwd(COL_RET_V), bwd(COL_RET_GB),
                  state, state],
        out_specs=[fwd(0), bwd(0), state, state],
        out_shape=[y_shape, y_shape, s_shape, s_shape],
        scratch_shapes=[pltpu.VMEM((8, RET_CHUNK, RET_CHUNK), F32)],
        compiler_params=_cparams(("parallel", "parallel", "arbitrary"), VMEM_MIB_SMALL),
        name="retention",
    )(logit_f.astype(F32), logit_b.astype(F32), p, p, p, p, p, p, p, p, s0_f, s0_b)


def _retention(p_c, p_l, logit_f, logit_b):
    b = p_c.shape[0]
    s0 = jnp.zeros((b, N_HEADS_RET, HEAD_DIM, HEAD_DIM), F32)
    yc_f, yc_b, s_cf, s_cb = _retention_scan(p_c, logit_f, logit_b, s0, s0)
    yl_f, yl_b, _, _ = _retention_scan(p_l, logit_f, logit_b, s_cf, s_cb)
    return (yc_f, yc_b), (yl_f, yl_b)


MERGE_ROWS = 512
MERGE_HALVES = 2


def _merge_kernel(x_ref, ya_ref, yb_ref, yrf_ref, yrb_ref, g_ref, gate_ref, w_ref, g2_ref, sh2_ref, sc2_ref, wr_ref,
                  o_ref, h_ref, aff_ref):
    wa = ya_ref.shape[-1]
    wb = yb_ref.shape[-1]
    hr = x_ref.shape[0] // MERGE_HALVES
    halves = [slice(j * hr, (j + 1) * hr) for j in range(MERGE_HALVES)]
    wr = wr_ref[...]
    w_hi = wr.astype(BF16)
    w_lo = (wr - w_hi.astype(F32)).astype(BF16)
    normed = []
    for rows in halves:
        na = (_rms(ya_ref[rows, :]) * g_ref[:, :wa]).astype(BF16)
        nb = (_rms(yb_ref[rows, :]) * g_ref[:, wa:wa + wb]).astype(BF16)
        nr = (_rms(yrf_ref[rows, :] + yrb_ref[rows, :]) * g_ref[:, wa + wb:]).astype(BF16)
        normed.append((na, nb, nr))
    accs = []
    for na, nb, nr in normed:
        acc = jnp.dot(na, w_ref[:wa, :], preferred_element_type=F32)
        acc = acc + jnp.dot(nb, w_ref[wa:wa + wb, :], preferred_element_type=F32)
        accs.append(acc + jnp.dot(nr, w_ref[wa + wb:, :], preferred_element_type=F32))
    for rows, acc in zip(halves, accs):
        x_new = x_ref[rows, :] + gate_ref[...] * acc
        o_ref[rows, :] = x_new
        h = (_rms(x_new) * g2_ref[...]) * (1.0 + sc2_ref[...]) + sh2_ref[...]
        h_hi = h.astype(BF16)
        h_ref[rows, :] = h_hi
        h_lo = (h - h_hi.astype(F32)).astype(BF16)
        logits = (jnp.dot(h_hi, w_hi, preferred_element_type=F32)
                  + (jnp.dot(h_hi, w_lo, preferred_element_type=F32)
                     + jnp.dot(h_lo, w_hi, preferred_element_type=F32)))
        lane = lax.broadcasted_iota(jnp.int32, logits.shape, 1)
        logits = jnp.where(lane < N_EXPERTS, logits, NEG_BIG)
        e = jnp.exp(logits - jnp.max(logits, axis=-1, keepdims=True))
        aff_ref[rows, :] = e / jnp.sum(e, axis=-1, keepdims=True)


def _merge_out(x, ya, yb, yr, g, gate, w_out, g2, shift2, scale2, w_router_pad):
    b, n, d = x.shape
    yrf, yrb = yr
    tm = min(n, MERGE_ROWS)
    dm = w_out.shape[0]

    def tok(width):
        return pl.BlockSpec((None, tm, width), lambda bb, i: (bb, i, 0))

    row = pl.BlockSpec((1, d), lambda bb, i: (0, 0))
    per_batch = pl.BlockSpec((None, 1, d), lambda bb, i: (bb, 0, 0))
    return pl.pallas_call(
        _merge_kernel,
        grid=(b, n // tm),
        in_specs=[tok(d), tok(ya.shape[-1]), tok(yb.shape[-1]), tok(yrf.shape[-1]), tok(yrb.shape[-1]),
                  pl.BlockSpec((1, dm), lambda bb, i: (0, 0)), per_batch,
                  pl.BlockSpec((dm, d), lambda bb, i: (0, 0)),
                  row, per_batch, per_batch,
                  pl.BlockSpec((d, LANE), lambda bb, i: (0, 0))],
        out_specs=[tok(d), tok(d), tok(LANE)],
        out_shape=[jax.ShapeDtypeStruct((b, n, d), F32), jax.ShapeDtypeStruct((b, n, d), BF16),
                   jax.ShapeDtypeStruct((b, n, LANE), F32)],
        compiler_params=_cparams(("parallel", "parallel"), VMEM_MIB_MAX),
        name="merge_out",
    )(x, ya, yb, yrf, yrb, g.reshape(1, dm), gate, w_out, g2.reshape(1, d), shift2, scale2, w_router_pad)


def _ffn_up_kernel(xs_ref, wg_ref, wu_ref, hm_ref, wg_scr, wu_scr):
    @pl.when(pl.program_id(1) == 0)
    def _():
        wg_scr[...] = wg_ref[...].astype(BF16)
        wu_scr[...] = wu_ref[...].astype(BF16)

    xs = xs_ref[...]
    a = jnp.dot(xs, wg_scr[...], preferred_element_type=F32)
    u = jnp.dot(xs, wu_scr[...], preferred_element_type=F32)
    hm_ref[...] = (_silu(a) * u).astype(BF16)


def _ffn_down_kernel(hm_ref, gate_ref, wd_ref, o_ref, wd_scr):
    @pl.when(pl.program_id(1) == 0)
    def _():
        wd_scr[...] = wd_ref[...].astype(BF16)

    o_ref[...] = (jnp.dot(hm_ref[...], wd_scr[...], preferred_element_type=F32) * gate_ref[...]).astype(o_ref.dtype)


def _expert_ffn(xs, gates, w_gate, w_up, w_down, layer):
    e, t, d = xs.shape
    f = w_gate.shape[-1]
    tm = min(t, 256)
    tm_down = min(t, 1024)
    hm = pl.pallas_call(
        _ffn_up_kernel,
        grid=(e, t // tm),
        in_specs=[pl.BlockSpec((None, tm, d), lambda ee, i: (ee, i, 0)),
                  pl.BlockSpec((None, None, d, f), lambda ee, i: (layer, ee, 0, 0)),
                  pl.BlockSpec((None, None, d, f), lambda ee, i: (layer, ee, 0, 0))],
        out_specs=pl.BlockSpec((None, tm, f), lambda ee, i: (ee, i, 0)),
        out_shape=jax.ShapeDtypeStruct((e, t, f), BF16),
        scratch_shapes=[pltpu.VMEM((d, f), BF16), pltpu.VMEM((d, f), BF16)],
        compiler_params=_cparams(("parallel", "arbitrary"), VMEM_MIB_XLARGE),
        name="expert_ffn_up",
    )(xs, w_gate, w_up)
    return pl.pallas_call(
        _ffn_down_kernel,
        grid=(e, t // tm_down),
        in_specs=[pl.BlockSpec((None, tm_down, f), lambda ee, i: (ee, i, 0)),
                  pl.BlockSpec((None, tm_down, 1), lambda ee, i: (ee, i, 0)),
                  pl.BlockSpec((None, None, f, d), lambda ee, i: (layer, ee, 0, 0))],
        out_specs=pl.BlockSpec((None, tm_down, d), lambda ee, i: (ee, i, 0)),
        out_shape=jax.ShapeDtypeStruct((e, t, d), BF16),
        scratch_shapes=[pltpu.VMEM((f, d), BF16)],
        compiler_params=_cparams(("parallel", "arbitrary"), VMEM_MIB_XLARGE),
        name="expert_ffn_down",
    )(hm, gates, w_down)


COMBINE_TOKENS = 256
COMBINE_ROWS = 512


def _combine_kernel(*refs, final_norm, aliased):
    if aliased:
        tile_ref, win_ref, flag_ref, x_ref, yp_ref, tok_ref, gate_ref, gn_ref, _, o_ref, acc_scr = refs
    else:
        tile_ref, win_ref, flag_ref, x_ref, yp_ref, tok_ref, gate_ref, gn_ref, o_ref, acc_scr = refs
    s = pl.program_id(0)
    flags = flag_ref[s]
    t = acc_scr.shape[0]

    @pl.when((flags & 1) != 0)
    def _():
        acc_scr[...] = jnp.zeros(acc_scr.shape, F32)

    @pl.when((flags & 2) != 0)
    def _():
        tcol = tile_ref[s] * t + lax.broadcasted_iota(jnp.int32, (t, 1), 0)
        onehot = jnp.where(tcol == tok_ref[...], 1.0, 0.0).astype(BF16)
        acc_scr[...] += jnp.dot(onehot, yp_ref[...], preferred_element_type=F32)

    @pl.when((flags & 4) != 0)
    def _():
        y = x_ref[...] + gate_ref[...] * acc_scr[...]
        if final_norm:
            y = _rms(y) * gn_ref[...]
        o_ref[...] = y


def _moe_combine(x, y_rows, tok_rows, gate, norm_g, final_norm, batch0=0, n_batches=None, out_prev=None):
    b, n, d = x.shape
    r = y_rows.shape[0]
    t = min(COMBINE_TOKENS, n)
    w = COMBINE_ROWS
    tiles_per_batch = n // t
    ntiles = (b if n_batches is None else n_batches) * tiles_per_batch
    tile0 = batch0 * tiles_per_batch
    nwin_total = r // w
    n_steps = nwin_total + 2 * ntiles
    i32 = jnp.int32
    tok_sorted, order = lax.sort_key_val(tok_rows.astype(i32), jnp.arange(r, dtype=i32))
    yp = jnp.take(y_rows, order, axis=0, mode="clip")
    edges = (tile0 + jnp.arange(ntiles + 1, dtype=i32)) * t
    bounds = jnp.sum((tok_sorted[None, :] < edges[:, None]).astype(i32), axis=1)
    w_lo = jnp.minimum(bounds[:-1] // w, nwin_total - 1)
    w_hi = jnp.maximum((bounds[1:] + w - 1) // w, w_lo + 1)
    nwin = w_hi - w_lo
    step0 = jnp.cumsum(nwin) - nwin
    total = jnp.sum(nwin)
    s = jnp.arange(n_steps, dtype=i32)
    tile_s = jnp.sum((s[:, None] >= step0[None, :]).astype(i32), axis=1) - 1
    sel = tile_s[:, None] == jnp.arange(ntiles, dtype=i32)[None, :]
    tile_s = tile_s + tile0

    def of_tile(v):
        return jnp.sum(jnp.where(sel, v[None, :], 0), axis=1)

    step0_s, w_lo_s, nwin_s = of_tile(step0), of_tile(w_lo), of_tile(nwin)
    win_s = jnp.minimum(w_lo_s + (s - step0_s), w_lo_s + nwin_s - 1)
    valid = s < total
    first = valid & (s == step0_s)
    last = valid & (s == step0_s + nwin_s - 1)
    flags = first.astype(i32) + 2 * valid.astype(i32) + 4 * last.astype(i32)
    in_specs = [pl.BlockSpec((t, d), lambda s, tl, wn, fl: (tl[s], 0)),
                pl.BlockSpec((w, d), lambda s, tl, wn, fl: (wn[s], 0)),
                pl.BlockSpec((None, 1, w), lambda s, tl, wn, fl: (wn[s], 0, 0)),
                pl.BlockSpec((None, 1, d), lambda s, tl, wn, fl: (tl[s] // tiles_per_batch, 0, 0)),
                pl.BlockSpec((1, d), lambda s, tl, wn, fl: (0, 0))]
    args = [tile_s, win_s, flags, x.reshape(b * n, d), yp, tok_sorted.reshape(nwin_total, 1, w), gate,
            norm_g.reshape(1, d)]
    aliases = {}
    if out_prev is not None:
        in_specs.append(pl.BlockSpec(memory_space=pl.ANY))
        args.append(out_prev.reshape(b * n, d))
        aliases = {len(args) - 1: 0}
    out = pl.pallas_call(
        functools.partial(_combine_kernel, final_norm=final_norm, aliased=out_prev is not None),
        grid_spec=pltpu.PrefetchScalarGridSpec(
            num_scalar_prefetch=3,
            grid=(n_steps,),
            in_specs=in_specs,
            out_specs=pl.BlockSpec((t, d), lambda s, tl, wn, fl: (tl[s], 0)),
            scratch_shapes=[pltpu.VMEM((t, d), F32)]),
        out_shape=jax.ShapeDtypeStruct((b * n, d), F32),
        compiler_params=_cparams(("arbitrary",), VMEM_MIB_MEDIUM),
        input_output_aliases=aliases,
        name="moe_combine",
    )(*args)
    return out.reshape(b, n, d)


def _moe_select(aff):
    b, n, _ = aff.shape
    cap = EC_CAPACITY * n // N_EXPERTS
    gate, idx = lax.top_k(jnp.swapaxes(aff[..., :N_EXPERTS], 1, 2), cap)
    flat = (idx + (jnp.arange(b, dtype=idx.dtype) * n)[:, None, None]).transpose(1, 0, 2).reshape(-1)
    return flat, gate.transpose(1, 0, 2).reshape(N_EXPERTS, b * cap, 1)


def _moe_gather(h, flat):
    b, n, d = h.shape
    return jnp.take(h.reshape(b * n, d), flat, axis=0, mode="clip").reshape(N_EXPERTS, -1, d)


def _moe_latent_split(x, h, aff, gate2, experts, norm_g, final_norm):
    b, n, d = x.shape
    half = b // 2
    routed = []
    for b0 in (0, half):
        flat, gates = _moe_select(aff[b0:b0 + half])
        routed.append((b0, flat, gates, _moe_gather(h[b0:b0 + half], flat)))
    out = None
    for b0, flat, gates, xs in routed:
        y = _expert_ffn(xs, gates, *experts)
        out = _moe_combine(x, y.reshape(-1, d), flat + b0 * n, gate2, norm_g, final_norm, b0, half, out)
    return out


def _moe_experts(x, xs, gates, flat, gate2, experts, norm_g, final_norm):
    y = _expert_ffn(xs, gates, *experts)
    return _moe_combine(x, y.reshape(-1, x.shape[-1]), flat, gate2, norm_g, final_norm)


_ROPE_SRC = np.concatenate([np.arange(16, 32), np.arange(0, 16), np.arange(48, 64), np.arange(32, 48)])
_ROPE_SIGN = np.concatenate([-np.ones(16), np.ones(16), -np.ones(16), np.ones(16)]).astype(np.float32)


def _rope_perm_cols(w):
    return w[:, _ROPE_SRC] * _ROPE_SIGN


def _pad_w_in(w_in):
    c_kr = MLA_Q_LORA + MLA_KV_LORA
    kr = w_in[:, c_kr:c_kr + MLA_ROPE_DIM]
    return jnp.concatenate([w_in[:, :c_kr + MLA_ROPE_DIM], _rope_perm_cols(kr), w_in[:, c_kr + MLA_ROPE_DIM:]],
                           axis=1).astype(BF16)


def _expand_w_uq(w_uq):
    w = w_uq.reshape(MLA_Q_LORA, N_HEADS_MLA, HEAD_DIM + MLA_ROPE_DIM)
    nope = w[:, :, :HEAD_DIM]
    rope = w[:, :, HEAD_DIM:]
    partner = rope[:, :, _ROPE_SRC] * _ROPE_SIGN
    return jnp.concatenate([nope, rope, partner], axis=-1).reshape(MLA_Q_LORA, N_HEADS_MLA * MLA_QK_PAD).astype(BF16)


def _rope_tables(n):
    t = jnp.arange(n)
    n_freq = MLA_ROPE_DIM // 4
    inv = ROPE_BASE ** (-jnp.arange(n_freq, dtype=F32) / n_freq)
    ang_r = (t // GRID_W).astype(F32)[:, None] * inv
    ang_c = (t % GRID_W).astype(F32)[:, None] * inv
    zeros = jnp.zeros((n, LANE - MLA_ROPE_DIM), F32)
    cos = jnp.concatenate([jnp.cos(ang_r), jnp.cos(ang_r), jnp.cos(ang_c), jnp.cos(ang_c), zeros], axis=1)
    sin = jnp.concatenate([jnp.sin(ang_r), jnp.sin(ang_r), jnp.sin(ang_c), jnp.sin(ang_c), zeros], axis=1)
    return cos, sin


def _identity_tables(n):
    ones = jnp.ones((n, MLA_ROPE_DIM), F32)
    zeros = jnp.zeros((n, LANE - MLA_ROPE_DIM), F32)
    return jnp.concatenate([ones, zeros], axis=1), jnp.zeros((n, LANE), F32)


def _layer(x_l, x_c, mod, lp, last, final_g):
    b, n, d = x_l.shape
    n_ctx = x_c.shape[1]

    def chunk(k, ctx):
        rows = jnp.broadcast_to(mod[b, k * d:(k + 1) * d], (b, d)) if ctx else mod[:b, k * d:(k + 1) * d]
        return rows.reshape(b, 1, d)

    w_in = _pad_w_in(lp['w_in'])
    p_l = _norm_proj(x_l, lp['norm1_g'], chunk(0, False), chunk(1, False), w_in)
    p_c = _norm_proj(x_c, lp['norm1_g'], chunk(0, True), chunk(1, True), w_in)

    cos_l, sin_l = _rope_tables(n)
    cos_c, sin_c = _identity_tables(n_ctx)
    w_uk = lp['mla_w_uk'].reshape(MLA_KV_LORA, -1).astype(BF16)
    w_uv_t = lp['mla_w_uv'].reshape(MLA_KV_LORA, -1).T.astype(BF16)
    w_q_t = _expand_w_uq(lp['mla_w_uq']).T
    k_l, v_l = _mla_kv_prep(p_l, lp['mla_kv_norm_g'], w_uk, w_uv_t, cos_l, sin_l)
    k_c, v_c = _mla_kv_prep(p_c, lp['mla_kv_norm_g'], w_uk, w_uv_t, cos_c, sin_c)
    q_l = _mla_q_prep(p_l, lp['mla_q_norm_g'], w_q_t, cos_l.T, sin_l.T)
    a_l = _attention_t(q_l, [(k_c, v_c), (k_l, v_l)], N_HEADS_MLA, MLA_QK_PAD, HEAD_DIM)

    b_l = _nat_latent(p_l, p_c, _nat_bias_table(lp['nat_rpb']))
    r_c, r_l = _retention(p_c, p_l, lp['ret_decay_f'], lp['ret_decay_b'])

    w_out = lp['w_out'].astype(BF16)
    w_router = jnp.zeros((d, LANE), F32).at[:, :N_EXPERTS].set(lp['w_router'])

    x_l, h_l, aff_l = _merge_out(x_l, a_l, b_l, r_l, lp['out_norm_g'], chunk(2, False), w_out,
                                 lp['norm2_g'], chunk(3, False), chunk(4, False), w_router)
    if last:
        return _moe_latent_split(x_l, h_l, aff_l, chunk(5, False), lp['experts'], final_g, True), None
    flat_l, gates_l = _moe_select(aff_l)

    flat_l, p_c, x_c = lax.optimization_barrier((flat_l, p_c, x_c))
    xs_l = _moe_gather(h_l, flat_l)
    q_c = _mla_q_prep(p_c, lp['mla_q_norm_g'], w_q_t, cos_c.T, sin_c.T)
    a_c = _attention_t(q_c, [(k_c, v_c)], N_HEADS_MLA, MLA_QK_PAD, HEAD_DIM)
    b_c = _attention(p_c, COL_NAT_Q, HEAD_DIM, [(p_c, COL_NAT_K, p_c, COL_NAT_V)], HEAD_DIM, N_HEADS_NAT,
                     HEAD_DIM ** -0.5)
    x_c, h_c, aff_c = _merge_out(x_c, a_c, b_c, r_c, lp['out_norm_g'], chunk(2, True), w_out,
                                 lp['norm2_g'], chunk(3, True), chunk(4, True), w_router)
    flat_c, gates_c = _moe_select(aff_c)
    x_c = _moe_experts(x_c, _moe_gather(h_c, flat_c), gates_c, flat_c, chunk(5, True), lp['experts'], final_g, False)
    xs_l, x_c = lax.optimization_barrier((xs_l, x_c))
    x_l = _moe_experts(x_l, xs_l, gates_l, flat_l, chunk(5, False), lp['experts'], final_g, False)
    return x_l, x_c


def kernel(x, c, ctx, c_ctx, w_mod, b_mod, norm1_g, w_in, mla_q_norm_g, mla_kv_norm_g, mla_w_uq, mla_w_uk,
           mla_w_uv, nat_rpb, ret_decay_f, ret_decay_b, out_norm_g, w_out, norm2_g, w_router, w_gate, w_up,
           w_down, final_norm_g):
    depth = w_mod.shape[0]
    cvecs = jnp.concatenate([c, c_ctx[None, :]], axis=0)
    x_l, x_c = x, ctx
    for i in range(depth):
        lp = {
            'norm1_g': norm1_g[i], 'w_in': w_in[i], 'mla_q_norm_g': mla_q_norm_g[i],
            'mla_kv_norm_g': mla_kv_norm_g[i], 'mla_w_uq': mla_w_uq[i], 'mla_w_uk': mla_w_uk[i],
            'mla_w_uv': mla_w_uv[i], 'nat_rpb': nat_rpb[i], 'ret_decay_f': ret_decay_f[i],
            'ret_decay_b': ret_decay_b[i], 'out_norm_g': out_norm_g[i], 'w_out': w_out[i],
            'norm2_g': norm2_g[i], 'w_router': w_router[i], 'experts': (w_gate, w_up, w_down, i),
        }
        mod = _modulation(cvecs, w_mod, b_mod[i], i)
        x_l, x_c = _layer(x_l, x_c, mod, lp, i == depth - 1, final_norm_g)
    return x_l
```

```python
import functools

import numpy as np
import jax
import jax.numpy as jnp
from jax import lax
from jax.experimental import pallas as pl
from jax.experimental.pallas import tpu as pltpu

F32 = jnp.float32
BF16 = jnp.bfloat16

GRID_W = 64
HEAD_DIM = 128
N_HEADS_MLA = 8
N_HEADS_NAT = 4
N_HEADS_RET = 4
MLA_ROPE_DIM = 64
MLA_Q_LORA = 512
MLA_KV_LORA = 256
MLA_SCALE = (HEAD_DIM + MLA_ROPE_DIM) ** -0.5
LOG2E = 1.4426950408889634
MLA_Q_SCALE = MLA_SCALE * LOG2E
MLA_QK_PAD = 256
NAT_ROWS = 8
NAT_COLS = 16
RET_CHUNK = 128
N_EXPERTS = 16
EC_CAPACITY = 2
ROPE_BASE = 10000.0
EPS = 1e-6
NEG_BIG = -1e30
LANE = 128

COL_CQ = 0
COL_CKV = 4
COL_KR = 6
COL_NAT_Q = 7
COL_NAT_K = 11
COL_NAT_V = 15
COL_RET_Q = 19
COL_RET_K = 23
COL_RET_V = 27
COL_RET_GF = 31
COL_RET_GB = 35
IN_COLS_PAD = 39 * LANE


VMEM_MIB_SMALL = 32
VMEM_MIB_MEDIUM = 40
VMEM_MIB_LARGE = 48
VMEM_MIB_XLARGE = 56
VMEM_MIB_MAX = 58


def _cparams(sem, vmem_mib):
    return pltpu.CompilerParams(dimension_semantics=sem, vmem_limit_bytes=vmem_mib * 1024 * 1024)


def _silu(a):
    return a * (1.0 / (1.0 + jnp.exp(-a)))


def _rms(x):
    return x * lax.rsqrt(jnp.mean(x * x, axis=-1, keepdims=True) + EPS)


def _mod_kernel(ct_ref, w_ref, b_ref, o_ref, a_scr, *, n_rows, k_chunk):
    ct = ct_ref[...]
    a_scr[...] = _silu(ct)
    d = w_ref.shape[0]
    tn = w_ref.shape[1]

    def body(kc, accs):
        k0 = pl.multiple_of(kc * k_chunk, k_chunk)
        wblk = w_ref[pl.ds(k0, k_chunk), :]
        out = []
        for r in range(n_rows):
            col = a_scr[pl.ds(k0, k_chunk), r:r + 1]
            out.append(accs[r] + jnp.sum((col * wblk).reshape(k_chunk // 8, 8, tn), axis=0))
        return tuple(out)

    accs = lax.fori_loop(0, d // k_chunk, body, tuple(jnp.zeros((8, tn), F32) for _ in range(n_rows)))
    o_ref[...] = jnp.zeros(o_ref.shape, F32)
    for r in range(n_rows):
        o_ref[r:r + 1, :] = jnp.sum(accs[r], axis=0, keepdims=True) + b_ref[...]


def _modulation(cvecs, w_mod, b_mod, layer):
    n_rows, d = cvecs.shape
    n_out = w_mod.shape[-1]
    tn = 1024
    ct = jnp.zeros((d, 8), F32).at[:, :n_rows].set(cvecs.T)
    return pl.pallas_call(
        functools.partial(_mod_kernel, n_rows=n_rows, k_chunk=64),
        grid=(n_out // tn,),
        in_specs=[pl.BlockSpec((d, 8), lambda j: (0, 0)),
                  pl.BlockSpec((None, d, tn), lambda j: (layer, 0, j)),
                  pl.BlockSpec((1, tn), lambda j: (0, j))],
        out_specs=pl.BlockSpec((8, tn), lambda j: (0, j)),
        out_shape=jax.ShapeDtypeStruct((8, n_out), F32),
        scratch_shapes=[pltpu.VMEM((d, 8), F32)],
        compiler_params=_cparams(("parallel",), VMEM_MIB_MEDIUM),
        name="modulation",
    )(ct, w_mod, b_mod.reshape(1, n_out))


NORM_PROJ_PARTS = 2


def _norm_proj_kernel(x_ref, g_ref, sh_ref, sc_ref, w_ref, o_ref, h_scr):
    j = pl.program_id(2)
    pr = x_ref.shape[0] // NORM_PROJ_PARTS
    parts = [slice(k * pr, (k + 1) * pr) for k in range(NORM_PROJ_PARTS)]

    @pl.when(j == 0)
    def _():
        hs = []
        for rows in parts:
            y = _rms(x_ref[rows, :]) * g_ref[...]
            h = (y * (1.0 + sc_ref[...]) + sh_ref[...]).astype(BF16)
            h_scr[rows, :] = h
            hs.append(h)
        for rows, h in zip(parts, hs):
            o_ref[rows, :] = jnp.dot(h, w_ref[...], preferred_element_type=F32).astype(o_ref.dtype)

    @pl.when(j != 0)
    def _():
        o_ref[...] = jnp.dot(h_scr[...], w_ref[...], preferred_element_type=F32).astype(o_ref.dtype)


def _norm_proj(x, g, shift, scale, w):
    b, n, d = x.shape
    c = w.shape[1]
    tm = min(n, 1024)
    tn = c // 3 if (c % (3 * LANE) == 0) else c
    return pl.pallas_call(
        _norm_proj_kernel,
        grid=(b, n // tm, c // tn),
        in_specs=[pl.BlockSpec((None, tm, d), lambda bb, i, j: (bb, i, 0)),
                  pl.BlockSpec((1, d), lambda bb, i, j: (0, 0)),
                  pl.BlockSpec((None, 1, d), lambda bb, i, j: (bb, 0, 0)),
                  pl.BlockSpec((None, 1, d), lambda bb, i, j: (bb, 0, 0)),
                  pl.BlockSpec((d, tn), lambda bb, i, j: (0, j))],
        out_specs=pl.BlockSpec((None, tm, tn), lambda bb, i, j: (bb, i, j)),
        out_shape=jax.ShapeDtypeStruct((b, n, c), BF16),
        scratch_shapes=[pltpu.VMEM((tm, d), BF16)],
        compiler_params=_cparams(("parallel", "parallel", "arbitrary"), VMEM_MIB_XLARGE),
        name="norm_proj",
    )(x, g.reshape(1, d), shift, scale, w)


def _rope_rotate(r, cos, sin):
    return r * cos + pltpu.roll(r, 64, 1) * sin


def _mla_q_kernel(cq_ref, g_ref, wt_ref, cos_ref, sin_ref, o_ref, *, n_heads):
    cqn_t = (_rms(cq_ref[...].astype(F32)) * g_ref[...]).T.astype(BF16)
    qe = jnp.dot(wt_ref[...], cqn_t, preferred_element_type=F32)
    cos = cos_ref[...]
    sin = sin_ref[...]
    for h in range(n_heads):
        c0 = h * MLA_QK_PAD
        o_ref[c0:c0 + LANE, :] = (qe[c0:c0 + LANE, :] * MLA_Q_SCALE).astype(BF16)
        r = qe[c0 + LANE:c0 + 2 * LANE, :]
        swapped = jnp.concatenate([r[MLA_ROPE_DIM:, :], r[:MLA_ROPE_DIM, :]], axis=0)
        o_ref[c0 + LANE:c0 + 2 * LANE, :] = ((r * cos + swapped * sin) * MLA_Q_SCALE).astype(BF16)


def _mla_q_prep(p, g, w_q_t, cos_t, sin_t):
    b, n, _ = p.shape
    tm = min(n, 512)
    cw = N_HEADS_MLA * MLA_QK_PAD
    return pl.pallas_call(
        functools.partial(_mla_q_kernel, n_heads=N_HEADS_MLA),
        grid=(b, n // tm),
        in_specs=[pl.BlockSpec((None, tm, MLA_Q_LORA), lambda bb, i: (bb, i, COL_CQ * LANE // MLA_Q_LORA)),
                  pl.BlockSpec((1, MLA_Q_LORA), lambda bb, i: (0, 0)),
                  pl.BlockSpec((cw, MLA_Q_LORA), lambda bb, i: (0, 0)),
                  pl.BlockSpec((LANE, tm), lambda bb, i: (0, i)),
                  pl.BlockSpec((LANE, tm), lambda bb, i: (0, i))],
        out_specs=pl.BlockSpec((None, cw, tm), lambda bb, i: (bb, 0, i)),
        out_shape=jax.ShapeDtypeStruct((b, cw, n), BF16),
        compiler_params=_cparams(("parallel", "parallel"), VMEM_MIB_MEDIUM),
        name="mla_q_prep",
    )(p, g.reshape(1, MLA_Q_LORA), w_q_t, cos_t, sin_t)


def _mla_kv_kernel(ckv_ref, kr_ref, g_ref, wuk_ref, wuvt_ref, cos_ref, sin_ref, k_ref, v_ref, *, n_heads):
    ckvn = _rms(ckv_ref[...].astype(F32)) * g_ref[...]
    kn = jnp.dot(ckvn.astype(BF16), wuk_ref[...], preferred_element_type=F32)
    v_ref[...] = jnp.dot(wuvt_ref[...], ckvn.T.astype(BF16), preferred_element_type=F32).astype(BF16)
    rot = _rope_rotate(kr_ref[...].astype(F32), cos_ref[...], sin_ref[...]).astype(BF16)
    for h in range(n_heads):
        c0 = h * MLA_QK_PAD
        k_ref[:, c0:c0 + LANE] = kn[:, h * LANE:(h + 1) * LANE].astype(BF16)
        k_ref[:, c0 + LANE:c0 + 2 * LANE] = rot


def _mla_kv_prep(p, g, w_uk, w_uv_t, cos, sin):
    b, n, _ = p.shape
    tm = min(n, 512)
    kw = N_HEADS_MLA * MLA_QK_PAD
    vw = N_HEADS_MLA * HEAD_DIM
    return pl.pallas_call(
        functools.partial(_mla_kv_kernel, n_heads=N_HEADS_MLA),
        grid=(b, n // tm),
        in_specs=[pl.BlockSpec((None, tm, MLA_KV_LORA), lambda bb, i: (bb, i, COL_CKV * LANE // MLA_KV_LORA)),
                  pl.BlockSpec((None, tm, LANE), lambda bb, i: (bb, i, COL_KR)),
                  pl.BlockSpec((1, MLA_KV_LORA), lambda bb, i: (0, 0)),
                  pl.BlockSpec((MLA_KV_LORA, vw), lambda bb, i: (0, 0)),
                  pl.BlockSpec((vw, MLA_KV_LORA), lambda bb, i: (0, 0)),
                  pl.BlockSpec((tm, LANE), lambda bb, i: (i, 0)),
                  pl.BlockSpec((tm, LANE), lambda bb, i: (i, 0))],
        out_specs=[pl.BlockSpec((None, tm, kw), lambda bb, i: (bb, i, 0)),
                   pl.BlockSpec((None, vw, tm), lambda bb, i: (bb, 0, i))],
        out_shape=[jax.ShapeDtypeStruct((b, n, kw), BF16), jax.ShapeDtypeStruct((b, vw, n), BF16)],
        compiler_params=_cparams(("parallel", "parallel"), VMEM_MIB_MEDIUM),
        name="mla_kv_prep",
    )(p, p, g.reshape(1, MLA_KV_LORA), w_uk, w_uv_t, cos, sin)


def _attn_kernel(*refs, n_src, scale, tk_max):
    q_ref = refs[0]
    o_ref = refs[1 + 2 * n_src]
    q = q_ref[...]
    if scale != 1.0:
        q = q.astype(F32) * scale
    q = q.astype(BF16)
    tq = q.shape[0]
    dv = o_ref.shape[-1]

    def step(k, v, carry):
        m, l, acc = carry
        s = lax.dot_general(q, k, (((1,), (1,)), ((), ())), preferred_element_type=F32)
        m_new = jnp.maximum(m, jnp.max(s, axis=-1, keepdims=True))
        p = jnp.exp(s - m_new)
        alpha = jnp.exp(m - m_new)
        l = alpha * l + jnp.sum(p, axis=-1, keepdims=True)
        acc = alpha * acc + jnp.dot(p.astype(BF16), v, preferred_element_type=F32)
        return m_new, l, acc

    carry = (jnp.full((tq, 1), NEG_BIG, F32), jnp.zeros((tq, 1), F32), jnp.zeros((tq, dv), F32))
    for s_i in range(n_src):
        k_ref = refs[1 + 2 * s_i]
        v_ref = refs[2 + 2 * s_i]
        nk = k_ref.shape[0]
        tk = min(nk, tk_max)
        if nk == tk:
            carry = step(k_ref[...].astype(BF16), v_ref[...].astype(BF16), carry)
        else:
            def body(c, cr, k_ref=k_ref, v_ref=v_ref, tk=tk):
                k0 = pl.multiple_of(c * tk, tk)
                return step(k_ref[pl.ds(k0, tk), :].astype(BF16), v_ref[pl.ds(k0, tk), :].astype(BF16), cr)
            carry = lax.fori_loop(0, nk // tk, body, carry)
    _, l, acc = carry
    o_ref[...] = acc / l


def _attention(q_arr, q_blk0, dq, sources, dv, n_heads, scale, tq=256, tk_max=512):
    b, nq, _ = q_arr.shape
    tq = min(nq, tq)
    in_specs = [pl.BlockSpec((None, tq, dq), lambda bb, h, i: (bb, i, q_blk0 + h))]
    args = [q_arr]
    for k_arr, k_blk0, v_arr, v_blk0 in sources:
        nk = k_arr.shape[1]
        in_specs.append(pl.BlockSpec((None, nk, dq), lambda bb, h, i, o=k_blk0: (bb, 0, o + h)))
        in_specs.append(pl.BlockSpec((None, nk, dv), lambda bb, h, i, o=v_blk0: (bb, 0, o + h)))
        args += [k_arr, v_arr]
    return pl.pallas_call(
        functools.partial(_attn_kernel, n_src=len(sources), scale=scale, tk_max=tk_max),
        grid=(b, n_heads, nq // tq),
        in_specs=in_specs,
        out_specs=pl.BlockSpec((None, tq, dv), lambda bb, h, i: (bb, i, h)),
        out_shape=jax.ShapeDtypeStruct((b, nq, n_heads * dv), F32),
        compiler_params=_cparams(("parallel", "parallel", "arbitrary"), VMEM_MIB_LARGE),
        name="attention",
    )(*args)


ATTN_TQ = 2048
ATTN_TK = 512
ATTN_GROUP = 4


def _attn_t_kernel(*refs, n_src):
    q_ref = refs[0]
    o_ref = refs[1 + 2 * n_src]
    q = q_ref[...]
    tq = q.shape[1]
    dv = o_ref.shape[-1]

    def scores(k):
        return jnp.dot(k, q, preferred_element_type=F32)

    def update(s, vt, carry):
        m, l, acc = carry
        m_new = jnp.maximum(m, jnp.max(s, axis=0, keepdims=True))
        p = jnp.exp2(s - m_new)
        alpha = jnp.exp2(m - m_new)
        l = alpha * l + jnp.sum(p, axis=0, keepdims=True)
        acc = alpha * acc + jnp.dot(vt, p.astype(BF16), preferred_element_type=F32)
        return m_new, l, acc

    carry = (jnp.full((1, tq), NEG_BIG, F32), jnp.zeros((1, tq), F32), jnp.zeros((dv, tq), F32))
    for s_i in range(n_src):
        k_ref = refs[1 + 2 * s_i]
        vt_ref = refs[2 + 2 * s_i]
        nk = k_ref.shape[0]
        tk = min(nk, ATTN_TK)
        n_chunks = nk // tk
        if n_chunks == 1:
            carry = update(scores(k_ref[...]), vt_ref[...], carry)
            continue
        group = max(g for g in (ATTN_GROUP, 2, 1) if n_chunks % g == 0)

        def body(c, cr, k_ref=k_ref, vt_ref=vt_ref, tk=tk, group=group):
            starts = [pl.multiple_of((c * group + g) * tk, tk) for g in range(group)]
            ss = [scores(k_ref[pl.ds(k0, tk), :]) for k0 in starts]
            for s, k0 in zip(ss, starts):
                cr = update(s, vt_ref[:, pl.ds(k0, tk)], cr)
            return cr
        carry = lax.fori_loop(0, n_chunks // group, body, carry)
    _, l, acc = carry
    o_ref[...] = (acc / l).T


def _attention_t(q_arr, sources, n_heads, dq, dv):
    b, _, nq = q_arr.shape
    tq = min(nq, ATTN_TQ)
    in_specs = [pl.BlockSpec((None, dq, tq), lambda bb, h, i: (bb, h, i))]
    args = [q_arr]
    for k_arr, vt_arr in sources:
        nk = k_arr.shape[1]
        in_specs.append(pl.BlockSpec((None, nk, dq), lambda bb, h, i: (bb, 0, h)))
        in_specs.append(pl.BlockSpec((None, dv, nk), lambda bb, h, i: (bb, h, 0)))
        args += [k_arr, vt_arr]
    return pl.pallas_call(
        functools.partial(_attn_t_kernel, n_src=len(sources)),
        grid=(b, n_heads, nq // tq),
        in_specs=in_specs,
        out_specs=pl.BlockSpec((None, tq, dv), lambda bb, h, i: (bb, i, h)),
        out_shape=jax.ShapeDtypeStruct((b, nq, n_heads * dv), F32),
        compiler_params=_cparams(("parallel", "parallel", "arbitrary"), VMEM_MIB_XLARGE),
        name="attention_t",
    )(*args)


NAT_GROUP = 8
NAT_SLAB = NAT_GROUP + NAT_ROWS
NAT_HALVES = 2


def _nat_kernel(q_ref, k_ref, v_ref, kc_ref, vc_ref, bias_ref, o_ref, *, n_rows):
    i = pl.program_id(2)
    scale = HEAD_DIM ** -0.5
    nt = (((1,), (1,)), ((), ()))
    u0 = jnp.clip(i * NAT_GROUP - NAT_ROWS // 2, 0, n_rows - NAT_SLAB)
    t0 = pl.multiple_of(u0 * GRID_W, GRID_W)
    ks = k_ref[pl.ds(t0, NAT_SLAB * GRID_W), :].astype(BF16)
    vs = v_ref[pl.ds(t0, NAT_SLAB * GRID_W), :].astype(BF16)
    kc = kc_ref[...].astype(BF16)
    vc = vc_ref[...].astype(BF16)
    hq = q_ref.shape[0] // NAT_HALVES
    scores = []
    for j in range(NAT_HALVES):
        rows = slice(j * hq, (j + 1) * hq)
        q = (q_ref[rows, :].astype(F32) * scale).astype(BF16)
        scores.append((lax.dot_general(q, ks, nt, preferred_element_type=F32) + bias_ref[rows, :],
                       lax.dot_general(q, kc, nt, preferred_element_type=F32)))
    for j in range(NAT_HALVES):
        s_w, s_c = scores[j]
        m = jnp.maximum(jnp.max(s_w, axis=-1, keepdims=True), jnp.max(s_c, axis=-1, keepdims=True))
        p_w = jnp.exp(s_w - m)
        p_c = jnp.exp(s_c - m)
        l = jnp.sum(p_w, axis=-1, keepdims=True) + jnp.sum(p_c, axis=-1, keepdims=True)
        o = (jnp.dot(p_w.astype(BF16), vs, preferred_element_type=F32)
             + jnp.dot(p_c.astype(BF16), vc, preferred_element_type=F32))
        o_ref[j * hq:(j + 1) * hq, :] = o / l


def _nat_bias_table(rpb):
    h = rpb.shape[0]
    col = np.arange(GRID_W)
    c0 = np.clip(col - NAT_COLS // 2, 0, GRID_W - NAT_COLS)
    kc = np.arange(GRID_W)
    col_ok = (kc[None, :] >= c0[:, None]) & (kc[None, :] < c0[:, None] + NAT_COLS)
    dc = kc[None, :] - col[:, None] + NAT_COLS - 1
    sel_c = ((dc[None] == np.arange(2 * NAT_COLS - 1)[:, None, None]) & col_ok[None]).astype(np.float32)
    sel_c = sel_c.reshape(2 * NAT_COLS - 1, GRID_W * GRID_W)
    per_dr = jnp.einsum('hdc,cb->hdb', rpb.astype(F32), sel_c, precision=lax.Precision.HIGHEST)
    per_dr = jnp.where(col_ok.reshape(-1)[None, None], per_dr, NEG_BIG).reshape(h, 2 * NAT_ROWS - 1, GRID_W, GRID_W)
    return pl.pallas_call(
        _nat_table_kernel,
        grid=(h, 3),
        in_specs=[pl.BlockSpec((None, 2 * NAT_ROWS - 1, GRID_W, GRID_W), lambda hh, v: (hh, 0, 0, 0))],
        out_specs=pl.BlockSpec((None, None, NAT_GROUP * GRID_W, NAT_SLAB * GRID_W), lambda hh, v: (hh, v, 0, 0)),
        out_shape=jax.ShapeDtypeStruct((h, 3, NAT_GROUP * GRID_W, NAT_SLAB * GRID_W), F32),
        compiler_params=_cparams(("parallel", "arbitrary"), VMEM_MIB_SMALL),
        name="nat_bias_table",
    )(per_dr)


def _nat_table_kernel(per_dr_ref, o_ref):
    masked = jnp.full((GRID_W, GRID_W), NEG_BIG, F32)
    for variant, delta in enumerate((0, NAT_ROWS // 2, NAT_ROWS)):
        @pl.when(pl.program_id(1) == variant)
        def _(delta=delta):
            for g in range(NAT_GROUP):
                w0 = min(max(g + delta - NAT_ROWS // 2, 0), NAT_SLAB - NAT_ROWS)
                tiles = [per_dr_ref[u - (g + delta) + NAT_ROWS - 1] if w0 <= u < w0 + NAT_ROWS else masked
                         for u in range(NAT_SLAB)]
                o_ref[g * GRID_W:(g + 1) * GRID_W, :] = jnp.concatenate(tiles, axis=1)


def _nat_latent(p_l, p_c, bias_tbl):
    b, n, _ = p_l.shape
    n_ctx = p_c.shape[1]
    n_rows = n // GRID_W
    assert n_rows >= NAT_SLAB and n_rows % NAT_GROUP == 0
    n_groups = n_rows // NAT_GROUP
    tq = NAT_GROUP * GRID_W
    tk = NAT_SLAB * GRID_W

    def variant(i):
        return jnp.where(i == 0, 0, jnp.where(i == n_groups - 1, 2, 1))

    return pl.pallas_call(
        functools.partial(_nat_kernel, n_rows=n_rows),
        grid=(b, N_HEADS_NAT, n_groups),
        in_specs=[pl.BlockSpec((None, tq, LANE), lambda bb, h, i: (bb, i, COL_NAT_Q + h)),
                  pl.BlockSpec((None, n, LANE), lambda bb, h, i: (bb, 0, COL_NAT_K + h)),
                  pl.BlockSpec((None, n, LANE), lambda bb, h, i: (bb, 0, COL_NAT_V + h)),
                  pl.BlockSpec((None, n_ctx, LANE), lambda bb, h, i: (bb, 0, COL_NAT_K + h)),
                  pl.BlockSpec((None, n_ctx, LANE), lambda bb, h, i: (bb, 0, COL_NAT_V + h)),
                  pl.BlockSpec((None, None, tq, tk), lambda bb, h, i: (h, variant(i), 0, 0))],
        out_specs=pl.BlockSpec((None, tq, LANE), lambda bb, h, i: (bb, i, h)),
        out_shape=jax.ShapeDtypeStruct((b, n, N_HEADS_NAT * HEAD_DIM), F32),
        compiler_params=_cparams(("parallel", "parallel", "arbitrary"), VMEM_MIB_LARGE),
        name="nat_latent",
    )(p_l, p_l, p_l, p_c, p_c, bias_tbl)


def _ret_consts(logit, reverse):
    c_len = RET_CHUNK
    z = jnp.full((c_len, c_len), -logit, F32)
    lg = -(jnp.maximum(z, 0.0) + jnp.log1p(jnp.exp(-jnp.abs(z))))
    ci = lax.broadcasted_iota(jnp.int32, (c_len, c_len), 0).astype(F32)
    si = lax.broadcasted_iota(jnp.int32, (c_len, c_len), 1).astype(F32)
    if reverse:
        rel = si - ci
        wq = jnp.exp(lg * (c_len - ci))
        wk = jnp.exp(lg * ci)
    else:
        rel = ci - si
        wq = jnp.exp(lg * (ci + 1.0))
        wk = jnp.exp(lg * (c_len - 1.0 - ci))
    dmat = jnp.where(rel >= 0, jnp.exp(lg * jnp.maximum(rel, 0.0)), 0.0)
    return wq, wk, dmat, jnp.exp(lg * c_len)


def _ret_kernel(lf_ref, lb_ref, qf_ref, kf_ref, vf_ref, gf_ref, qb_ref, kb_ref, vb_ref, gb_ref, s0f_ref, s0b_ref,
                yf_ref, yb_ref, sf_ref, sb_ref, c_scr, *, n_chunks):
    h = pl.program_id(1)

    @pl.when(pl.program_id(2) == 0)
    def _():
        sf_ref[...] = s0f_ref[...]
        sb_ref[...] = s0b_ref[...]
        for j, arr in enumerate(_ret_consts(lf_ref[h], False) + _ret_consts(lb_ref[h], True)):
            c_scr[j] = arr

    c_len = RET_CHUNK
    nt = (((1,), (1,)), ((), ()))
    k_scale = HEAD_DIM ** -0.5
    dirs = []
    for d, refs in enumerate(((qf_ref, kf_ref, vf_ref, gf_ref, yf_ref), (qb_ref, kb_ref, vb_ref, gb_ref, yb_ref))):
        order = list(range(n_chunks)) if d == 0 else list(range(n_chunks - 1, -1, -1))
        dirs.append(refs + tuple(c_scr[4 * d + j] for j in range(4)) + (order,))

    units = []
    for step in range(n_chunks):
        for d in range(2):
            q_ref, k_ref, v_ref, _, _, _, wk, dmat, _, order = dirs[d]
            sl = slice(order[step] * c_len, (order[step] + 1) * c_len)
            qb = q_ref[sl, :].astype(BF16)
            kb = k_ref[sl, :]
            vb = v_ref[sl, :].astype(BF16)
            sc = lax.dot_general(qb, kb.astype(BF16), nt, preferred_element_type=F32) * (dmat * k_scale)
            kv = jnp.dot((kb.astype(F32) * (wk * k_scale)).T.astype(BF16), vb, preferred_element_type=F32)
            units.append((d, sl, qb, vb, sc, kv))
    states = [sf_ref[...], sb_ref[...]]
    cross = []
    for (d, sl, qb, vb, sc, kv) in units:
        wq, g_chunk = dirs[d][5], dirs[d][8]
        cross.append(wq * jnp.dot(qb, states[d].astype(BF16), preferred_element_type=F32))
        states[d] = g_chunk * states[d] + kv
    sf_ref[...] = states[0]
    sb_ref[...] = states[1]
    for (d, sl, qb, vb, sc, kv), o_cross in zip(units, cross):
        g_ref, y_ref = dirs[d][3], dirs[d][4]
        o = jnp.dot(sc.astype(BF16), vb, preferred_element_type=F32) + o_cross
        mu = jnp.mean(o, axis=-1, keepdims=True)
        dev = o - mu
        gn = dev * lax.rsqrt(jnp.mean(dev * dev, axis=-1, keepdims=True) + EPS)
        y_ref[sl, :] = _silu(g_ref[sl, :].astype(F32)) * gn


def _retention_scan(p, logit_f, logit_b, s0_f, s0_b):
    b, n, _ = p.shape
    hh = N_HEADS_RET
    tb = min(n, 16 * RET_CHUNK)
    nblk = n // tb

    def fwd(c0):
        return pl.BlockSpec((None, tb, LANE), lambda bb, h, i: (bb, i, c0 + h))

    def bwd(c0):
        return pl.BlockSpec((None, tb, LANE), lambda bb, h, i: (bb, nblk - 1 - i, c0 + h))

    state = pl.BlockSpec((None, None, HEAD_DIM, HEAD_DIM), lambda bb, h, i: (bb, h, 0, 0))
    smem = pl.BlockSpec(memory_space=pltpu.SMEM)
    y_shape = jax.ShapeDtypeStruct((b, n, hh * HEAD_DIM), F32)
    s_shape = jax.ShapeDtypeStruct((b, hh, HEAD_DIM, HEAD_DIM), F32)
    return pl.pallas_call(
        functools.partial(_ret_kernel, n_chunks=tb // RET_CHUNK),
        grid=(b, hh, nblk),
        in_specs=[smem, smem,
                  fwd(COL_RET_Q), fwd(COL_RET_K), fwd(COL_RET_V), fwd(COL_RET_GF),
                  bwd(COL_RET_Q), bwd(COL_RET_K), bwd(COL_RET_V), bwd(COL_RET_GB),
                  state, state],
        out_specs=[fwd(0), bwd(0), state, state],
        out_shape=[y_shape, y_shape, s_shape, s_shape],
        scratch_shapes=[pltpu.VMEM((8, RET_CHUNK, RET_CHUNK), F32)],
        compiler_params=_cparams(("parallel", "parallel", "arbitrary"), VMEM_MIB_SMALL),
        name="retention",
    )(logit_f.astype(F32), logit_b.astype(F32), p, p, p, p, p, p, p, p, s0_f, s0_b)


def _retention(p_c, p_l, logit_f, logit_b):
    b = p_c.shape[0]
    s0 = jnp.zeros((b, N_HEADS_RET, HEAD_DIM, HEAD_DIM), F32)
    yc_f, yc_b, s_cf, s_cb = _retention_scan(p_c, logit_f, logit_b, s0, s0)
    yl_f, yl_b, _, _ = _retention_scan(p_l, logit_f, logit_b, s_cf, s_cb)
    return (yc_f, yc_b), (yl_f, yl_b)


MERGE_ROWS = 512
MERGE_HALVES = 2


def _merge_kernel(x_ref, ya_ref, yb_ref, yrf_ref, yrb_ref, g_ref, gate_ref, w_ref, g2_ref, sh2_ref, sc2_ref, wr_ref,
                  o_ref, h_ref, aff_ref):
    wa = ya_ref.shape[-1]
    wb = yb_ref.shape[-1]
    hr = x_ref.shape[0] // MERGE_HALVES
    halves = [slice(j * hr, (j + 1) * hr) for j in range(MERGE_HALVES)]
    wr = wr_ref[...]
    w_hi = wr.astype(BF16)
    w_lo = (wr - w_hi.astype(F32)).astype(BF16)
    normed = []
    for rows in halves:
        na = (_rms(ya_ref[rows, :]) * g_ref[:, :wa]).astype(BF16)
        nb = (_rms(yb_ref[rows, :]) * g_ref[:, wa:wa + wb]).astype(BF16)
        nr = (_rms(yrf_ref[rows, :] + yrb_ref[rows, :]) * g_ref[:, wa + wb:]).astype(BF16)
        normed.append((na, nb, nr))
    accs = []
    for na, nb, nr in normed:
        acc = jnp.dot(na, w_ref[:wa, :], preferred_element_type=F32)
        acc = acc + jnp.dot(nb, w_ref[wa:wa + wb, :], preferred_element_type=F32)
        accs.append(acc + jnp.dot(nr, w_ref[wa + wb:, :], preferred_element_type=F32))
    for rows, acc in zip(halves, accs):
        x_new = x_ref[rows, :] + gate_ref[...] * acc
        o_ref[rows, :] = x_new
        h = (_rms(x_new) * g2_ref[...]) * (1.0 + sc2_ref[...]) + sh2_ref[...]
        h_hi = h.astype(BF16)
        h_ref[rows, :] = h_hi
        h_lo = (h - h_hi.astype(F32)).astype(BF16)
        logits = (jnp.dot(h_hi, w_hi, preferred_element_type=F32)
                  + (jnp.dot(h_hi, w_lo, preferred_element_type=F32)
                     + jnp.dot(h_lo, w_hi, preferred_element_type=F32)))
        lane = lax.broadcasted_iota(jnp.int32, logits.shape, 1)
        logits = jnp.where(lane < N_EXPERTS, logits, NEG_BIG)
        e = jnp.exp(logits - jnp.max(logits, axis=-1, keepdims=True))
        aff_ref[rows, :] = e / jnp.sum(e, axis=-1, keepdims=True)


def _merge_out(x, ya, yb, yr, g, gate, w_out, g2, shift2, scale2, w_router_pad):
    b, n, d = x.shape
    yrf, yrb = yr
    tm = min(n, MERGE_ROWS)
    dm = w_out.shape[0]

    def tok(width):
        return pl.BlockSpec((None, tm, width), lambda bb, i: (bb, i, 0))

    row = pl.BlockSpec((1, d), lambda bb, i: (0, 0))
    per_batch = pl.BlockSpec((None, 1, d), lambda bb, i: (bb, 0, 0))
    return pl.pallas_call(
        _merge_kernel,
        grid=(b, n // tm),
        in_specs=[tok(d), tok(ya.shape[-1]), tok(yb.shape[-1]), tok(yrf.shape[-1]), tok(yrb.shape[-1]),
                  pl.BlockSpec((1, dm), lambda bb, i: (0, 0)), per_batch,
                  pl.BlockSpec((dm, d), lambda bb, i: (0, 0)),
                  row, per_batch, per_batch,
                  pl.BlockSpec((d, LANE), lambda bb, i: (0, 0))],
        out_specs=[tok(d), tok(d), tok(LANE)],
        out_shape=[jax.ShapeDtypeStruct((b, n, d), F32), jax.ShapeDtypeStruct((b, n, d), BF16),
                   jax.ShapeDtypeStruct((b, n, LANE), F32)],
        compiler_params=_cparams(("parallel", "parallel"), VMEM_MIB_MAX),
        name="merge_out",
    )(x, ya, yb, yrf, yrb, g.reshape(1, dm), gate, w_out, g2.reshape(1, d), shift2, scale2, w_router_pad)


def _ffn_up_kernel(xs_ref, wg_ref, wu_ref, hm_ref, wg_scr, wu_scr):
    @pl.when(pl.program_id(1) == 0)
    def _():
        wg_scr[...] = wg_ref[...].astype(BF16)
        wu_scr[...] = wu_ref[...].astype(BF16)

    xs = xs_ref[...]
    a = jnp.dot(xs, wg_scr[...], preferred_element_type=F32)
    u = jnp.dot(xs, wu_scr[...], preferred_element_type=F32)
    hm_ref[...] = (_silu(a) * u).astype(BF16)


def _ffn_down_kernel(hm_ref, gate_ref, wd_ref, o_ref, wd_scr):
    @pl.when(pl.program_id(1) == 0)
    def _():
        wd_scr[...] = wd_ref[...].astype(BF16)

    o_ref[...] = (jnp.dot(hm_ref[...], wd_scr[...], preferred_element_type=F32) * gate_ref[...]).astype(o_ref.dtype)


def _expert_ffn(xs, gates, w_gate, w_up, w_down, layer):
    e, t, d = xs.shape
    f = w_gate.shape[-1]
    tm = min(t, 512)
    tm_down = min(t, 1024)
    hm = pl.pallas_call(
        _ffn_up_kernel,
        grid=(e, t // tm),
        in_specs=[pl.BlockSpec((None, tm, d), lambda ee, i: (ee, i, 0)),
                  pl.BlockSpec((None, None, d, f), lambda ee, i: (layer, ee, 0, 0)),
                  pl.BlockSpec((None, None, d, f), lambda ee, i: (layer, ee, 0, 0))],
        out_specs=pl.BlockSpec((None, tm, f), lambda ee, i: (ee, i, 0)),
        out_shape=jax.ShapeDtypeStruct((e, t, f), BF16),
        scratch_shapes=[pltpu.VMEM((d, f), BF16), pltpu.VMEM((d, f), BF16)],
        compiler_params=_cparams(("parallel", "arbitrary"), VMEM_MIB_XLARGE),
        name="expert_ffn_up",
    )(xs, w_gate, w_up)
    return pl.pallas_call(
        _ffn_down_kernel,
        grid=(e, t // tm_down),
        in_specs=[pl.BlockSpec((None, tm_down, f), lambda ee, i: (ee, i, 0)),
                  pl.BlockSpec((None, tm_down, 1), lambda ee, i: (ee, i, 0)),
                  pl.BlockSpec((None, None, f, d), lambda ee, i: (layer, ee, 0, 0))],
        out_specs=pl.BlockSpec((None, tm_down, d), lambda ee, i: (ee, i, 0)),
        out_shape=jax.ShapeDtypeStruct((e, t, d), BF16),
        scratch_shapes=[pltpu.VMEM((f, d), BF16)],
        compiler_params=_cparams(("parallel", "arbitrary"), VMEM_MIB_XLARGE),
        name="expert_ffn_down",
    )(hm, gates, w_down)


COMBINE_TOKENS = 256
COMBINE_ROWS = 512


def _combine_kernel(*refs, final_norm, aliased):
    if aliased:
        tile_ref, win_ref, flag_ref, x_ref, yp_ref, tok_ref, gate_ref, gn_ref, _, o_ref, acc_scr = refs
    else:
        tile_ref, win_ref, flag_ref, x_ref, yp_ref, tok_ref, gate_ref, gn_ref, o_ref, acc_scr = refs
    s = pl.program_id(0)
    flags = flag_ref[s]
    t = acc_scr.shape[0]

    @pl.when((flags & 1) != 0)
    def _():
        acc_scr[...] = jnp.zeros(acc_scr.shape, F32)

    @pl.when((flags & 2) != 0)
    def _():
        tcol = tile_ref[s] * t + lax.broadcasted_iota(jnp.int32, (t, 1), 0)
        onehot = jnp.where(tcol == tok_ref[...], 1.0, 0.0).astype(BF16)
        acc_scr[...] += jnp.dot(onehot, yp_ref[...], preferred_element_type=F32)

    @pl.when((flags & 4) != 0)
    def _():
        y = x_ref[...] + gate_ref[...] * acc_scr[...]
        if final_norm:
            y = _rms(y) * gn_ref[...]
        o_ref[...] = y


def _moe_combine(x, y_rows, tok_rows, gate, norm_g, final_norm, batch0=0, n_batches=None, out_prev=None):
    b, n, d = x.shape
    r = y_rows.shape[0]
    t = min(COMBINE_TOKENS, n)
    w = COMBINE_ROWS
    tiles_per_batch = n // t
    ntiles = (b if n_batches is None else n_batches) * tiles_per_batch
    tile0 = batch0 * tiles_per_batch
    nwin_total = r // w
    n_steps = nwin_total + 2 * ntiles
    i32 = jnp.int32
    tok_sorted, order = lax.sort_key_val(tok_rows.astype(i32), jnp.arange(r, dtype=i32))
    yp = jnp.take(y_rows, order, axis=0, mode="clip")
    edges = (tile0 + jnp.arange(ntiles + 1, dtype=i32)) * t
    bounds = jnp.sum((tok_sorted[None, :] < edges[:, None]).astype(i32), axis=1)
    w_lo = jnp.minimum(bounds[:-1] // w, nwin_total - 1)
    w_hi = jnp.maximum((bounds[1:] + w - 1) // w, w_lo + 1)
    nwin = w_hi - w_lo
    step0 = jnp.cumsum(nwin) - nwin
    total = jnp.sum(nwin)
    s = jnp.arange(n_steps, dtype=i32)
    tile_s = jnp.sum((s[:, None] >= step0[None, :]).astype(i32), axis=1) - 1
    sel = tile_s[:, None] == jnp.arange(ntiles, dtype=i32)[None, :]
    tile_s = tile_s + tile0

    def of_tile(v):
        return jnp.sum(jnp.where(sel, v[None, :], 0), axis=1)

    step0_s, w_lo_s, nwin_s = of_tile(step0), of_tile(w_lo), of_tile(nwin)
    win_s = jnp.minimum(w_lo_s + (s - step0_s), w_lo_s + nwin_s - 1)
    valid = s < total
    first = valid & (s == step0_s)
    last = valid & (s == step0_s + nwin_s - 1)
    flags = first.astype(i32) + 2 * valid.astype(i32) + 4 * last.astype(i32)
    in_specs = [pl.BlockSpec((t, d), lambda s, tl, wn, fl: (tl[s], 0)),
                pl.BlockSpec((w, d), lambda s, tl, wn, fl: (wn[s], 0)),
                pl.BlockSpec((None, 1, w), lambda s, tl, wn, fl: (wn[s], 0, 0)),
                pl.BlockSpec((None, 1, d), lambda s, tl, wn, fl: (tl[s] // tiles_per_batch, 0, 0)),
                pl.BlockSpec((1, d), lambda s, tl, wn, fl: (0, 0))]
    args = [tile_s, win_s, flags, x.reshape(b * n, d), yp, tok_sorted.reshape(nwin_total, 1, w), gate,
            norm_g.reshape(1, d)]
    aliases = {}
    if out_prev is not None:
        in_specs.append(pl.BlockSpec(memory_space=pl.ANY))
        args.append(out_prev.reshape(b * n, d))
        aliases = {len(args) - 1: 0}
    out = pl.pallas_call(
        functools.partial(_combine_kernel, final_norm=final_norm, aliased=out_prev is not None),
        grid_spec=pltpu.PrefetchScalarGridSpec(
            num_scalar_prefetch=3,
            grid=(n_steps,),
            in_specs=in_specs,
            out_specs=pl.BlockSpec((t, d), lambda s, tl, wn, fl: (tl[s], 0)),
            scratch_shapes=[pltpu.VMEM((t, d), F32)]),
        out_shape=jax.ShapeDtypeStruct((b * n, d), F32),
        compiler_params=_cparams(("arbitrary",), VMEM_MIB_MEDIUM),
        input_output_aliases=aliases,
        name="moe_combine",
    )(*args)
    return out.reshape(b, n, d)


def _moe_select(aff):
    b, n, _ = aff.shape
    cap = EC_CAPACITY * n // N_EXPERTS
    gate, idx = lax.top_k(jnp.swapaxes(aff[..., :N_EXPERTS], 1, 2), cap)
    flat = (idx + (jnp.arange(b, dtype=idx.dtype) * n)[:, None, None]).transpose(1, 0, 2).reshape(-1)
    return flat, gate.transpose(1, 0, 2).reshape(N_EXPERTS, b * cap, 1)


def _moe_gather(h, flat):
    b, n, d = h.shape
    return jnp.take(h.reshape(b * n, d), flat, axis=0, mode="clip").reshape(N_EXPERTS, -1, d)


def _moe_latent_split(x, h, aff, gate2, experts, norm_g, final_norm):
    b, n, d = x.shape
    size = b // 2 if b % 2 == 0 else b
    routed = []
    for b0 in range(0, b, size):
        flat, gates = _moe_select(aff[b0:b0 + size])
        flat = flat + b0 * n
        routed.append((b0, flat, gates, _moe_gather(h, flat)))
    out = None
    for b0, flat, gates, xs in routed:
        y = _expert_ffn(xs, gates, *experts)
        out = _moe_combine(x, y.reshape(-1, d), flat, gate2, norm_g, final_norm, b0, size, out)
    return out


def _moe_experts(x, xs, gates, flat, gate2, experts, norm_g, final_norm):
    y = _expert_ffn(xs, gates, *experts)
    return _moe_combine(x, y.reshape(-1, x.shape[-1]), flat, gate2, norm_g, final_norm)


_ROPE_SRC = np.concatenate([np.arange(16, 32), np.arange(0, 16), np.arange(48, 64), np.arange(32, 48)])
_ROPE_SIGN = np.concatenate([-np.ones(16), np.ones(16), -np.ones(16), np.ones(16)]).astype(np.float32)


def _rope_perm_cols(w):
    return w[:, _ROPE_SRC] * _ROPE_SIGN


def _pad_w_in(w_in):
    c_kr = MLA_Q_LORA + MLA_KV_LORA
    kr = w_in[:, c_kr:c_kr + MLA_ROPE_DIM]
    return jnp.concatenate([w_in[:, :c_kr + MLA_ROPE_DIM], _rope_perm_cols(kr), w_in[:, c_kr + MLA_ROPE_DIM:]],
                           axis=1).astype(BF16)


def _expand_w_uq(w_uq):
    w = w_uq.reshape(MLA_Q_LORA, N_HEADS_MLA, HEAD_DIM + MLA_ROPE_DIM)
    nope = w[:, :, :HEAD_DIM]
    rope = w[:, :, HEAD_DIM:]
    partner = rope[:, :, _ROPE_SRC] * _ROPE_SIGN
    return jnp.concatenate([nope, rope, partner], axis=-1).reshape(MLA_Q_LORA, N_HEADS_MLA * MLA_QK_PAD).astype(BF16)


def _rope_tables(n):
    t = jnp.arange(n)
    n_freq = MLA_ROPE_DIM // 4
    inv = ROPE_BASE ** (-jnp.arange(n_freq, dtype=F32) / n_freq)
    ang_r = (t // GRID_W).astype(F32)[:, None] * inv
    ang_c = (t % GRID_W).astype(F32)[:, None] * inv
    zeros = jnp.zeros((n, LANE - MLA_ROPE_DIM), F32)
    cos = jnp.concatenate([jnp.cos(ang_r), jnp.cos(ang_r), jnp.cos(ang_c), jnp.cos(ang_c), zeros], axis=1)
    sin = jnp.concatenate([jnp.sin(ang_r), jnp.sin(ang_r), jnp.sin(ang_c), jnp.sin(ang_c), zeros], axis=1)
    return cos, sin


def _identity_tables(n):
    ones = jnp.ones((n, MLA_ROPE_DIM), F32)
    zeros = jnp.zeros((n, LANE - MLA_ROPE_DIM), F32)
    return jnp.concatenate([ones, zeros], axis=1), jnp.zeros((n, LANE), F32)


def _layer(x_l, x_c, mod, lp, last, final_g):
    b, n, d = x_l.shape
    n_ctx = x_c.shape[1]

    def chunk(k, ctx):
        rows = jnp.broadcast_to(mod[b, k * d:(k + 1) * d], (b, d)) if ctx else mod[:b, k * d:(k + 1) * d]
        return rows.reshape(b, 1, d)

    w_in = _pad_w_in(lp['w_in'])
    p_l = _norm_proj(x_l, lp['norm1_g'], chunk(0, False), chunk(1, False), w_in)
    p_c = _norm_proj(x_c, lp['norm1_g'], chunk(0, True), chunk(1, True), w_in)

    cos_l, sin_l = _rope_tables(n)
    cos_c, sin_c = _identity_tables(n_ctx)
    w_uk = lp['mla_w_uk'].reshape(MLA_KV_LORA, -1).astype(BF16)
    w_uv_t = lp['mla_w_uv'].reshape(MLA_KV_LORA, -1).T.astype(BF16)
    w_q_t = _expand_w_uq(lp['mla_w_uq']).T
    k_l, v_l = _mla_kv_prep(p_l, lp['mla_kv_norm_g'], w_uk, w_uv_t, cos_l, sin_l)
    k_c, v_c = _mla_kv_prep(p_c, lp['mla_kv_norm_g'], w_uk, w_uv_t, cos_c, sin_c)
    q_l = _mla_q_prep(p_l, lp['mla_q_norm_g'], w_q_t, cos_l.T, sin_l.T)
    a_l = _attention_t(q_l, [(k_c, v_c), (k_l, v_l)], N_HEADS_MLA, MLA_QK_PAD, HEAD_DIM)

    b_l = _nat_latent(p_l, p_c, _nat_bias_table(lp['nat_rpb']))
    r_c, r_l = _retention(p_c, p_l, lp['ret_decay_f'], lp['ret_decay_b'])

    w_out = lp['w_out'].astype(BF16)
    w_router = jnp.zeros((d, LANE), F32).at[:, :N_EXPERTS].set(lp['w_router'])

    x_l, h_l, aff_l = _merge_out(x_l, a_l, b_l, r_l, lp['out_norm_g'], chunk(2, False), w_out,
                                 lp['norm2_g'], chunk(3, False), chunk(4, False), w_router)
    if last:
        return _moe_latent_split(x_l, h_l, aff_l, chunk(5, False), lp['experts'], final_g, True), None
    flat_l, gates_l = _moe_select(aff_l)

    flat_l, p_c, x_c = lax.optimization_barrier((flat_l, p_c, x_c))
    xs_l = _moe_gather(h_l, flat_l)
    q_c = _mla_q_prep(p_c, lp['mla_q_norm_g'], w_q_t, cos_c.T, sin_c.T)
    a_c = _attention_t(q_c, [(k_c, v_c)], N_HEADS_MLA, MLA_QK_PAD, HEAD_DIM)
    b_c = _attention(p_c, COL_NAT_Q, HEAD_DIM, [(p_c, COL_NAT_K, p_c, COL_NAT_V)], HEAD_DIM, N_HEADS_NAT,
                     HEAD_DIM ** -0.5)
    x_c, h_c, aff_c = _merge_out(x_c, a_c, b_c, r_c, lp['out_norm_g'], chunk(2, True), w_out,
                                 lp['norm2_g'], chunk(3, True), chunk(4, True), w_router)
    flat_c, gates_c = _moe_select(aff_c)
    x_c = _moe_experts(x_c, _moe_gather(h_c, flat_c), gates_c, flat_c, chunk(5, True), lp['experts'], final_g, False)
    xs_l, x_c = lax.optimization_barrier((xs_l, x_c))
    x_l = _moe_experts(x_l, xs_l, gates_l, flat_l, chunk(5, False), lp['experts'], final_g, False)
    return x_l, x_c


def kernel(x, c, ctx, c_ctx, w_mod, b_mod, norm1_g, w_in, mla_q_norm_g, mla_kv_norm_g, mla_w_uq, mla_w_uk,
           mla_w_uv, nat_rpb, ret_decay_f, ret_decay_b, out_norm_g, w_out, norm2_g, w_router, w_gate, w_up,
           w_down, final_norm_g):
    depth = w_mod.shape[0]
    cvecs = jnp.concatenate([c, c_ctx[None, :]], axis=0)
    x_l, x_c = x, ctx
    for i in range(depth):
        lp = {
            'norm1_g': norm1_g[i], 'w_in': w_in[i], 'mla_q_norm_g': mla_q_norm_g[i],
            'mla_kv_norm_g': mla_kv_norm_g[i], 'mla_w_uq': mla_w_uq[i], 'mla_w_uk': mla_w_uk[i],
            'mla_w_uv': mla_w_uv[i], 'nat_rpb': nat_rpb[i], 'ret_decay_f': ret_decay_f[i],
            'ret_decay_b': ret_decay_b[i], 'out_norm_g': out_norm_g[i], 'w_out': w_out[i],
            'norm2_g': norm2_g[i], 'w_router': w_router[i], 'experts': (w_gate, w_up, w_down, i),
        }
        mod = _modulation(cvecs, w_mod, b_mod[i], i)
        x_l, x_c = _layer(x_l, x_c, mod, lp, i == depth - 1, final_norm_g)
    return x_l
```

```python
import functools

import numpy as np
import jax
import jax.numpy as jnp
from jax import lax
from jax.experimental import pallas as pl
from jax.experimental.pallas import tpu as pltpu

F32 = jnp.float32
BF16 = jnp.bfloat16

GRID_W = 64
HEAD_DIM = 128
N_HEADS_MLA = 8
N_HEADS_NAT = 4
N_HEADS_RET = 4
MLA_ROPE_DIM = 64
MLA_Q_LORA = 512
MLA_KV_LORA = 256
MLA_SCALE = (HEAD_DIM + MLA_ROPE_DIM) ** -0.5
LOG2E = 1.4426950408889634
MLA_Q_SCALE = MLA_SCALE * LOG2E
MLA_QK_PAD = 256
NAT_ROWS = 8
NAT_COLS = 16
RET_CHUNK = 128
N_EXPERTS = 16
EC_CAPACITY = 2
ROPE_BASE = 10000.0
EPS = 1e-6
NEG_BIG = -1e30
LANE = 128

COL_CQ = 0
COL_CKV = 4
COL_KR = 6
COL_NAT_Q = 7
COL_NAT_K = 11
COL_NAT_V = 15
COL_RET_Q = 19
COL_RET_K = 23
COL_RET_V = 27
COL_RET_GF = 31
COL_RET_GB = 35
IN_COLS_PAD = 39 * LANE


VMEM_MIB_SMALL = 32
VMEM_MIB_MEDIUM = 40
VMEM_MIB_LARGE = 48
VMEM_MIB_XLARGE = 56
VMEM_MIB_MAX = 58


def _cparams(sem, vmem_mib):
    return pltpu.CompilerParams(dimension_semantics=sem, vmem_limit_bytes=vmem_mib * 1024 * 1024)


def _silu(a):
    return a * (1.0 / (1.0 + jnp.exp(-a)))


def _rms(x):
    return x * lax.rsqrt(jnp.mean(x * x, axis=-1, keepdims=True) + EPS)


def _mod_kernel(ct_ref, w_ref, b_ref, o_ref, a_scr, *, n_rows, k_chunk):
    ct = ct_ref[...]
    a_scr[...] = _silu(ct)
    d = w_ref.shape[0]
    tn = w_ref.shape[1]

    def body(kc, accs):
        k0 = pl.multiple_of(kc * k_chunk, k_chunk)
        wblk = w_ref[pl.ds(k0, k_chunk), :]
        out = []
        for r in range(n_rows):
            col = a_scr[pl.ds(k0, k_chunk), r:r + 1]
            out.append(accs[r] + jnp.sum((col * wblk).reshape(k_chunk // 8, 8, tn), axis=0))
        return tuple(out)

    accs = lax.fori_loop(0, d // k_chunk, body, tuple(jnp.zeros((8, tn), F32) for _ in range(n_rows)))
    o_ref[...] = jnp.zeros(o_ref.shape, F32)
    for r in range(n_rows):
        o_ref[r:r + 1, :] = jnp.sum(accs[r], axis=0, keepdims=True) + b_ref[...]


def _modulation(cvecs, w_mod, b_mod, layer):
    n_rows, d = cvecs.shape
    n_out = w_mod.shape[-1]
    tn = 1024
    ct = jnp.zeros((d, 8), F32).at[:, :n_rows].set(cvecs.T)
    return pl.pallas_call(
        functools.partial(_mod_kernel, n_rows=n_rows, k_chunk=64),
        grid=(n_out // tn,),
        in_specs=[pl.BlockSpec((d, 8), lambda j: (0, 0)),
                  pl.BlockSpec((None, d, tn), lambda j: (layer, 0, j)),
                  pl.BlockSpec((1, tn), lambda j: (0, j))],
        out_specs=pl.BlockSpec((8, tn), lambda j: (0, j)),
        out_shape=jax.ShapeDtypeStruct((8, n_out), F32),
        scratch_shapes=[pltpu.VMEM((d, 8), F32)],
        compiler_params=_cparams(("parallel",), VMEM_MIB_MEDIUM),
        name="modulation",
    )(ct, w_mod, b_mod.reshape(1, n_out))


NORM_PROJ_PARTS = 2


def _norm_proj_kernel(x_ref, g_ref, sh_ref, sc_ref, w_ref, o_ref, h_scr):
    j = pl.program_id(2)
    pr = x_ref.shape[0] // NORM_PROJ_PARTS
    parts = [slice(k * pr, (k + 1) * pr) for k in range(NORM_PROJ_PARTS)]

    @pl.when(j == 0)
    def _():
        hs = []
        for rows in parts:
            y = _rms(x_ref[rows, :]) * g_ref[...]
            h = (y * (1.0 + sc_ref[...]) + sh_ref[...]).astype(BF16)
            h_scr[rows, :] = h
            hs.append(h)
        for rows, h in zip(parts, hs):
            o_ref[rows, :] = jnp.dot(h, w_ref[...], preferred_element_type=F32).astype(o_ref.dtype)

    @pl.when(j != 0)
    def _():
        o_ref[...] = jnp.dot(h_scr[...], w_ref[...], preferred_element_type=F32).astype(o_ref.dtype)


def _norm_proj(x, g, shift, scale, w):
    b, n, d = x.shape
    c = w.shape[1]
    tm = min(n, 1024)
    tn = c // 3 if (c % (3 * LANE) == 0) else c
    return pl.pallas_call(
        _norm_proj_kernel,
        grid=(b, n // tm, c // tn),
        in_specs=[pl.BlockSpec((None, tm, d), lambda bb, i, j: (bb, i, 0)),
                  pl.BlockSpec((1, d), lambda bb, i, j: (0, 0)),
                  pl.BlockSpec((None, 1, d), lambda bb, i, j: (bb, 0, 0)),
                  pl.BlockSpec((None, 1, d), lambda bb, i, j: (bb, 0, 0)),
                  pl.BlockSpec((d, tn), lambda bb, i, j: (0, j))],
        out_specs=pl.BlockSpec((None, tm, tn), lambda bb, i, j: (bb, i, j)),
        out_shape=jax.ShapeDtypeStruct((b, n, c), BF16),
        scratch_shapes=[pltpu.VMEM((tm, d), BF16)],
        compiler_params=_cparams(("parallel", "parallel", "arbitrary"), VMEM_MIB_XLARGE),
        name="norm_proj",
    )(x, g.reshape(1, d), shift, scale, w)


def _rope_rotate(r, cos, sin):
    return r * cos + pltpu.roll(r, 64, 1) * sin


def _mla_q_kernel(cq_ref, g_ref, wt_ref, cos_ref, sin_ref, o_ref, *, n_heads):
    cqn_t = (_rms(cq_ref[...].astype(F32)) * g_ref[...]).T.astype(BF16)
    qe = jnp.dot(wt_ref[...], cqn_t, preferred_element_type=F32)
    cos = cos_ref[...]
    sin = sin_ref[...]
    for h in range(n_heads):
        c0 = h * MLA_QK_PAD
        o_ref[c0:c0 + LANE, :] = (qe[c0:c0 + LANE, :] * MLA_Q_SCALE).astype(BF16)
        r = qe[c0 + LANE:c0 + 2 * LANE, :]
        swapped = jnp.concatenate([r[MLA_ROPE_DIM:, :], r[:MLA_ROPE_DIM, :]], axis=0)
        o_ref[c0 + LANE:c0 + 2 * LANE, :] = ((r * cos + swapped * sin) * MLA_Q_SCALE).astype(BF16)


def _mla_q_prep(p, g, w_q_t, cos_t, sin_t):
    b, n, _ = p.shape
    tm = min(n, 512)
    cw = N_HEADS_MLA * MLA_QK_PAD
    return pl.pallas_call(
        functools.partial(_mla_q_kernel, n_heads=N_HEADS_MLA),
        grid=(b, n // tm),
        in_specs=[pl.BlockSpec((None, tm, MLA_Q_LORA), lambda bb, i: (bb, i, COL_CQ * LANE // MLA_Q_LORA)),
                  pl.BlockSpec((1, MLA_Q_LORA), lambda bb, i: (0, 0)),
                  pl.BlockSpec((cw, MLA_Q_LORA), lambda bb, i: (0, 0)),
                  pl.BlockSpec((LANE, tm), lambda bb, i: (0, i)),
                  pl.BlockSpec((LANE, tm), lambda bb, i: (0, i))],
        out_specs=pl.BlockSpec((None, cw, tm), lambda bb, i: (bb, 0, i)),
        out_shape=jax.ShapeDtypeStruct((b, cw, n), BF16),
        compiler_params=_cparams(("parallel", "parallel"), VMEM_MIB_MEDIUM),
        name="mla_q_prep",
    )(p, g.reshape(1, MLA_Q_LORA), w_q_t, cos_t, sin_t)


def _mla_kv_kernel(ckv_ref, kr_ref, g_ref, wuk_ref, wuvt_ref, cos_ref, sin_ref, k_ref, v_ref, *, n_heads):
    ckvn = _rms(ckv_ref[...].astype(F32)) * g_ref[...]
    kn = jnp.dot(ckvn.astype(BF16), wuk_ref[...], preferred_element_type=F32)
    v_ref[...] = jnp.dot(wuvt_ref[...], ckvn.T.astype(BF16), preferred_element_type=F32).astype(BF16)
    rot = _rope_rotate(kr_ref[...].astype(F32), cos_ref[...], sin_ref[...]).astype(BF16)
    for h in range(n_heads):
        c0 = h * MLA_QK_PAD
        k_ref[:, c0:c0 + LANE] = kn[:, h * LANE:(h + 1) * LANE].astype(BF16)
        k_ref[:, c0 + LANE:c0 + 2 * LANE] = rot


def _mla_kv_prep(p, g, w_uk, w_uv_t, cos, sin):
    b, n, _ = p.shape
    tm = min(n, 512)
    kw = N_HEADS_MLA * MLA_QK_PAD
    vw = N_HEADS_MLA * HEAD_DIM
    return pl.pallas_call(
        functools.partial(_mla_kv_kernel, n_heads=N_HEADS_MLA),
        grid=(b, n // tm),
        in_specs=[pl.BlockSpec((None, tm, MLA_KV_LORA), lambda bb, i: (bb, i, COL_CKV * LANE // MLA_KV_LORA)),
                  pl.BlockSpec((None, tm, LANE), lambda bb, i: (bb, i, COL_KR)),
                  pl.BlockSpec((1, MLA_KV_LORA), lambda bb, i: (0, 0)),
                  pl.BlockSpec((MLA_KV_LORA, vw), lambda bb, i: (0, 0)),
                  pl.BlockSpec((vw, MLA_KV_LORA), lambda bb, i: (0, 0)),
                  pl.BlockSpec((tm, LANE), lambda bb, i: (i, 0)),
                  pl.BlockSpec((tm, LANE), lambda bb, i: (i, 0))],
        out_specs=[pl.BlockSpec((None, tm, kw), lambda bb, i: (bb, i, 0)),
                   pl.BlockSpec((None, vw, tm), lambda bb, i: (bb, 0, i))],
        out_shape=[jax.ShapeDtypeStruct((b, n, kw), BF16), jax.ShapeDtypeStruct((b, vw, n), BF16)],
        compiler_params=_cparams(("parallel", "parallel"), VMEM_MIB_MEDIUM),
        name="mla_kv_prep",
    )(p, p, g.reshape(1, MLA_KV_LORA), w_uk, w_uv_t, cos, sin)


def _attn_kernel(*refs, n_src, scale, tk_max):
    q_ref = refs[0]
    o_ref = refs[1 + 2 * n_src]
    q = q_ref[...]
    if scale != 1.0:
        q = q.astype(F32) * scale
    q = q.astype(BF16)
    tq = q.shape[0]
    dv = o_ref.shape[-1]

    def step(k, v, carry):
        m, l, acc = carry
        s = lax.dot_general(q, k, (((1,), (1,)), ((), ())), preferred_element_type=F32)
        m_new = jnp.maximum(m, jnp.max(s, axis=-1, keepdims=True))
        p = jnp.exp(s - m_new)
        alpha = jnp.exp(m - m_new)
        l = alpha * l + jnp.sum(p, axis=-1, keepdims=True)
        acc = alpha * acc + jnp.dot(p.astype(BF16), v, preferred_element_type=F32)
        return m_new, l, acc

    carry = (jnp.full((tq, 1), NEG_BIG, F32), jnp.zeros((tq, 1), F32), jnp.zeros((tq, dv), F32))
    for s_i in range(n_src):
        k_ref = refs[1 + 2 * s_i]
        v_ref = refs[2 + 2 * s_i]
        nk = k_ref.shape[0]
        tk = min(nk, tk_max)
        if nk == tk:
            carry = step(k_ref[...].astype(BF16), v_ref[...].astype(BF16), carry)
        else:
            def body(c, cr, k_ref=k_ref, v_ref=v_ref, tk=tk):
                k0 = pl.multiple_of(c * tk, tk)
                return step(k_ref[pl.ds(k0, tk), :].astype(BF16), v_ref[pl.ds(k0, tk), :].astype(BF16), cr)
            carry = lax.fori_loop(0, nk // tk, body, carry)
    _, l, acc = carry
    o_ref[...] = acc / l


def _attention(q_arr, q_blk0, dq, sources, dv, n_heads, scale, tq=256, tk_max=512):
    b, nq, _ = q_arr.shape
    tq = min(nq, tq)
    in_specs = [pl.BlockSpec((None, tq, dq), lambda bb, h, i: (bb, i, q_blk0 + h))]
    args = [q_arr]
    for k_arr, k_blk0, v_arr, v_blk0 in sources:
        nk = k_arr.shape[1]
        in_specs.append(pl.BlockSpec((None, nk, dq), lambda bb, h, i, o=k_blk0: (bb, 0, o + h)))
        in_specs.append(pl.BlockSpec((None, nk, dv), lambda bb, h, i, o=v_blk0: (bb, 0, o + h)))
        args += [k_arr, v_arr]
    return pl.pallas_call(
        functools.partial(_attn_kernel, n_src=len(sources), scale=scale, tk_max=tk_max),
        grid=(b, n_heads, nq // tq),
        in_specs=in_specs,
        out_specs=pl.BlockSpec((None, tq, dv), lambda bb, h, i: (bb, i, h)),
        out_shape=jax.ShapeDtypeStruct((b, nq, n_heads * dv), F32),
        compiler_params=_cparams(("parallel", "parallel", "arbitrary"), VMEM_MIB_LARGE),
        name="attention",
    )(*args)


ATTN_TQ = 2048
ATTN_TK = 512
ATTN_GROUP = 4


def _attn_t_kernel(*refs, n_src):
    q_ref = refs[0]
    o_ref = refs[1 + 2 * n_src]
    q = q_ref[...]
    tq = q.shape[1]
    dv = o_ref.shape[-1]

    def scores(k):
        return jnp.dot(k, q, preferred_element_type=F32)

    def update(s, vt, carry):
        m, l, acc = carry
        m_new = jnp.maximum(m, jnp.max(s, axis=0, keepdims=True))
        p = jnp.exp2(s - m_new)
        alpha = jnp.exp2(m - m_new)
        l = alpha * l + jnp.sum(p, axis=0, keepdims=True)
        acc = alpha * acc + jnp.dot(vt, p.astype(BF16), preferred_element_type=F32)
        return m_new, l, acc

    carry = (jnp.full((1, tq), NEG_BIG, F32), jnp.zeros((1, tq), F32), jnp.zeros((dv, tq), F32))
    for s_i in range(n_src):
        k_ref = refs[1 + 2 * s_i]
        vt_ref = refs[2 + 2 * s_i]
        nk = k_ref.shape[0]
        tk = min(nk, ATTN_TK)
        n_chunks = nk // tk
        if n_chunks == 1:
            carry = update(scores(k_ref[...]), vt_ref[...], carry)
            continue
        group = max(g for g in (ATTN_GROUP, 2, 1) if n_chunks % g == 0)

        def body(c, cr, k_ref=k_ref, vt_ref=vt_ref, tk=tk, group=group):
            starts = [pl.multiple_of((c * group + g) * tk, tk) for g in range(group)]
            ss = [scores(k_ref[pl.ds(k0, tk), :]) for k0 in starts]
            for s, k0 in zip(ss, starts):
                cr = update(s, vt_ref[:, pl.ds(k0, tk)], cr)
            return cr
        carry = lax.fori_loop(0, n_chunks // group, body, carry)
    _, l, acc = carry
    o_ref[...] = (acc / l).T


def _attention_t(q_arr, sources, n_heads, dq, dv):
    b, _, nq = q_arr.shape
    tq = min(nq, ATTN_TQ)
    in_specs = [pl.BlockSpec((None, dq, tq), lambda bb, h, i: (bb, h, i))]
    args = [q_arr]
    for k_arr, vt_arr in sources:
        nk = k_arr.shape[1]
        in_specs.append(pl.BlockSpec((None, nk, dq), lambda bb, h, i: (bb, 0, h)))
        in_specs.append(pl.BlockSpec((None, dv, nk), lambda bb, h, i: (bb, h, 0)))
        args += [k_arr, vt_arr]
    return pl.pallas_call(
        functools.partial(_attn_t_kernel, n_src=len(sources)),
        grid=(b, n_heads, nq // tq),
        in_specs=in_specs,
        out_specs=pl.BlockSpec((None, tq, dv), lambda bb, h, i: (bb, i, h)),
        out_shape=jax.ShapeDtypeStruct((b, nq, n_heads * dv), F32),
        compiler_params=_cparams(("parallel", "parallel", "arbitrary"), VMEM_MIB_XLARGE),
        name="attention_t",
    )(*args)


NAT_GROUP = 8
NAT_SLAB = NAT_GROUP + NAT_ROWS
NAT_HALVES = 4


def _nat_kernel(q_ref, k_ref, v_ref, kc_ref, vc_ref, bias_ref, o_ref, *, n_rows):
    i = pl.program_id(2)
    scale = HEAD_DIM ** -0.5
    nt = (((1,), (1,)), ((), ()))
    u0 = jnp.clip(i * NAT_GROUP - NAT_ROWS // 2, 0, n_rows - NAT_SLAB)
    t0 = pl.multiple_of(u0 * GRID_W, GRID_W)
    ks = k_ref[pl.ds(t0, NAT_SLAB * GRID_W), :].astype(BF16)
    vs = v_ref[pl.ds(t0, NAT_SLAB * GRID_W), :].astype(BF16)
    kc = kc_ref[...].astype(BF16)
    vc = vc_ref[...].astype(BF16)
    hq = q_ref.shape[0] // NAT_HALVES
    scores = []
    for j in range(NAT_HALVES):
        rows = slice(j * hq, (j + 1) * hq)
        q = (q_ref[rows, :].astype(F32) * scale).astype(BF16)
        scores.append((lax.dot_general(q, ks, nt, preferred_element_type=F32) + bias_ref[rows, :],
                       lax.dot_general(q, kc, nt, preferred_element_type=F32)))
    for j in range(NAT_HALVES):
        s_w, s_c = scores[j]
        m = jnp.maximum(jnp.max(s_w, axis=-1, keepdims=True), jnp.max(s_c, axis=-1, keepdims=True))
        p_w = jnp.exp(s_w - m)
        p_c = jnp.exp(s_c - m)
        l = jnp.sum(p_w, axis=-1, keepdims=True) + jnp.sum(p_c, axis=-1, keepdims=True)
        o = (jnp.dot(p_w.astype(BF16), vs, preferred_element_type=F32)
             + jnp.dot(p_c.astype(BF16), vc, preferred_element_type=F32))
        o_ref[j * hq:(j + 1) * hq, :] = o / l


def _nat_bias_table(rpb):
    h = rpb.shape[0]
    col = np.arange(GRID_W)
    c0 = np.clip(col - NAT_COLS // 2, 0, GRID_W - NAT_COLS)
    kc = np.arange(GRID_W)
    col_ok = (kc[None, :] >= c0[:, None]) & (kc[None, :] < c0[:, None] + NAT_COLS)
    dc = kc[None, :] - col[:, None] + NAT_COLS - 1
    sel_c = ((dc[None] == np.arange(2 * NAT_COLS - 1)[:, None, None]) & col_ok[None]).astype(np.float32)
    sel_c = sel_c.reshape(2 * NAT_COLS - 1, GRID_W * GRID_W)
    per_dr = jnp.einsum('hdc,cb->hdb', rpb.astype(F32), sel_c, precision=lax.Precision.HIGHEST)
    per_dr = jnp.where(col_ok.reshape(-1)[None, None], per_dr, NEG_BIG).reshape(h, 2 * NAT_ROWS - 1, GRID_W, GRID_W)
    return pl.pallas_call(
        _nat_table_kernel,
        grid=(h, 3),
        in_specs=[pl.BlockSpec((None, 2 * NAT_ROWS - 1, GRID_W, GRID_W), lambda hh, v: (hh, 0, 0, 0))],
        out_specs=pl.BlockSpec((None, None, NAT_GROUP * GRID_W, NAT_SLAB * GRID_W), lambda hh, v: (hh, v, 0, 0)),
        out_shape=jax.ShapeDtypeStruct((h, 3, NAT_GROUP * GRID_W, NAT_SLAB * GRID_W), F32),
        compiler_params=_cparams(("parallel", "arbitrary"), VMEM_MIB_SMALL),
        name="nat_bias_table",
    )(per_dr)


def _nat_table_kernel(per_dr_ref, o_ref):
    masked = jnp.full((GRID_W, GRID_W), NEG_BIG, F32)
    for variant, delta in enumerate((0, NAT_ROWS // 2, NAT_ROWS)):
        @pl.when(pl.program_id(1) == variant)
        def _(delta=delta):
            for g in range(NAT_GROUP):
                w0 = min(max(g + delta - NAT_ROWS // 2, 0), NAT_SLAB - NAT_ROWS)
                tiles = [per_dr_ref[u - (g + delta) + NAT_ROWS - 1] if w0 <= u < w0 + NAT_ROWS else masked
                         for u in range(NAT_SLAB)]
                o_ref[g * GRID_W:(g + 1) * GRID_W, :] = jnp.concatenate(tiles, axis=1)


def _nat_latent(p_l, p_c, bias_tbl):
    b, n, _ = p_l.shape
    n_ctx = p_c.shape[1]
    n_rows = n // GRID_W
    assert n_rows >= NAT_SLAB and n_rows % NAT_GROUP == 0
    n_groups = n_rows // NAT_GROUP
    tq = NAT_GROUP * GRID_W
    tk = NAT_SLAB * GRID_W

    def variant(i):
        return jnp.where(i == 0, 0, jnp.where(i == n_groups - 1, 2, 1))

    return pl.pallas_call(
        functools.partial(_nat_kernel, n_rows=n_rows),
        grid=(b, N_HEADS_NAT, n_groups),
        in_specs=[pl.BlockSpec((None, tq, LANE), lambda bb, h, i: (bb, i, COL_NAT_Q + h)),
                  pl.BlockSpec((None, n, LANE), lambda bb, h, i: (bb, 0, COL_NAT_K + h)),
                  pl.BlockSpec((None, n, LANE), lambda bb, h, i: (bb, 0, COL_NAT_V + h)),
                  pl.BlockSpec((None, n_ctx, LANE), lambda bb, h, i: (bb, 0, COL_NAT_K + h)),
                  pl.BlockSpec((None, n_ctx, LANE), lambda bb, h, i: (bb, 0, COL_NAT_V + h)),
                  pl.BlockSpec((None, None, tq, tk), lambda bb, h, i: (h, variant(i), 0, 0))],
        out_specs=pl.BlockSpec((None, tq, LANE), lambda bb, h, i: (bb, i, h)),
        out_shape=jax.ShapeDtypeStruct((b, n, N_HEADS_NAT * HEAD_DIM), F32),
        compiler_params=_cparams(("parallel", "parallel", "arbitrary"), VMEM_MIB_LARGE),
        name="nat_latent",
    )(p_l, p_l, p_l, p_c, p_c, bias_tbl)


def _ret_consts(logit, reverse):
    c_len = RET_CHUNK
    z = jnp.full((c_len, c_len), -logit, F32)
    lg = -(jnp.maximum(z, 0.0) + jnp.log1p(jnp.exp(-jnp.abs(z))))
    ci = lax.broadcasted_iota(jnp.int32, (c_len, c_len), 0).astype(F32)
    si = lax.broadcasted_iota(jnp.int32, (c_len, c_len), 1).astype(F32)
    if reverse:
        rel = si - ci
        wq = jnp.exp(lg * (c_len - ci))
        wk = jnp.exp(lg * ci)
    else:
        rel = ci - si
        wq = jnp.exp(lg * (ci + 1.0))
        wk = jnp.exp(lg * (c_len - 1.0 - ci))
    dmat = jnp.where(rel >= 0, jnp.exp(lg * jnp.maximum(rel, 0.0)), 0.0)
    return wq, wk, dmat, jnp.exp(lg * c_len)


def _ret_kernel(lf_ref, lb_ref, qf_ref, kf_ref, vf_ref, gf_ref, qb_ref, kb_ref, vb_ref, gb_ref, s0f_ref, s0b_ref,
                yf_ref, yb_ref, sf_ref, sb_ref, c_scr, *, n_chunks):
    h = pl.program_id(1)

    @pl.when(pl.program_id(2) == 0)
    def _():
        sf_ref[...] = s0f_ref[...]
        sb_ref[...] = s0b_ref[...]
        for j, arr in enumerate(_ret_consts(lf_ref[h], False) + _ret_consts(lb_ref[h], True)):
            c_scr[j] = arr

    c_len = RET_CHUNK
    nt = (((1,), (1,)), ((), ()))
    k_scale = HEAD_DIM ** -0.5
    dirs = []
    for d, refs in enumerate(((qf_ref, kf_ref, vf_ref, gf_ref, yf_ref), (qb_ref, kb_ref, vb_ref, gb_ref, yb_ref))):
        order = list(range(n_chunks)) if d == 0 else list(range(n_chunks - 1, -1, -1))
        dirs.append(refs + tuple(c_scr[4 * d + j] for j in range(4)) + (order,))

    units = []
    for step in range(n_chunks):
        for d in range(2):
            q_ref, k_ref, v_ref, _, _, _, wk, dmat, _, order = dirs[d]
            sl = slice(order[step] * c_len, (order[step] + 1) * c_len)
            qb = q_ref[sl, :].astype(BF16)
            kb = k_ref[sl, :]
            vb = v_ref[sl, :].astype(BF16)
            sc = lax.dot_general(qb, kb.astype(BF16), nt, preferred_element_type=F32) * (dmat * k_scale)
            kv = jnp.dot((kb.astype(F32) * (wk * k_scale)).T.astype(BF16), vb, preferred_element_type=F32)
            units.append((d, sl, qb, vb, sc, kv))
    states = [sf_ref[...], sb_ref[...]]
    cross = []
    for (d, sl, qb, vb, sc, kv) in units:
        wq, g_chunk = dirs[d][5], dirs[d][8]
        cross.append(wq * jnp.dot(qb, states[d].astype(BF16), preferred_element_type=F32))
        states[d] = g_chunk * states[d] + kv
    sf_ref[...] = states[0]
    sb_ref[...] = states[1]
    for (d, sl, qb, vb, sc, kv), o_cross in zip(units, cross):
        g_ref, y_ref = dirs[d][3], dirs[d][4]
        o = jnp.dot(sc.astype(BF16), vb, preferred_element_type=F32) + o_cross
        mu = jnp.mean(o, axis=-1, keepdims=True)
        dev = o - mu
        gn = dev * lax.rsqrt(jnp.mean(dev * dev, axis=-1, keepdims=True) + EPS)
        y_ref[sl, :] = _silu(g_ref[sl, :].astype(F32)) * gn


def _retention_scan(p, logit_f, logit_b, s0_f, s0_b):
    b, n, _ = p.shape
    hh = N_HEADS_RET
    tb = min(n, 16 * RET_CHUNK)
    nblk = n // tb

    def fwd(c0):
        return pl.BlockSpec((None, tb, LANE), lambda bb, h, i: (bb, i, c0 + h))

    def bwd(c0):
        return pl.BlockSpec((None, tb, LANE), lambda bb, h, i: (bb, nblk - 1 - i, c0 + h))

    state = pl.BlockSpec((None, None, HEAD_DIM, HEAD_DIM), lambda bb, h, i: (bb, h, 0, 0))
    smem = pl.BlockSpec(memory_space=pltpu.SMEM)
    y_shape = jax.ShapeDtypeStruct((b, n, hh * HEAD_DIM), F32)
    s_shape = jax.ShapeDtypeStruct((b, hh, HEAD_DIM, HEAD_DIM), F32)
    return pl.pallas_call(
        functools.partial(_ret_kernel, n_chunks=tb // RET_CHUNK),
        grid=(b, hh, nblk),
        in_specs=[smem, smem,
                  fwd(COL_RET_Q), fwd(COL_RET_K), fwd(COL_RET_V), fwd(COL_RET_GF),
                  bwd(COL_RET_Q), bwd(COL_RET_K), bwd(COL_RET_V), bwd(COL_RET_GB),
                  state, state],
        out_specs=[fwd(0), bwd(0), state, state],
        out_shape=[y_shape, y_shape, s_shape, s_shape],
        scratch_shapes=[pltpu.VMEM((8, RET_CHUNK, RET_CHUNK), F32)],
        compiler_params=_cparams(("parallel", "parallel", "arbitrary"), VMEM_MIB_SMALL),
        name="retention",
    )(logit_f.astype(F32), logit_b.astype(F32), p, p, p, p, p, p, p, p, s0_f, s0_b)


def _retention(p_c, p_l, logit_f, logit_b):
    b = p_c.shape[0]
    s0 = jnp.zeros((b, N_HEADS_RET, HEAD_DIM, HEAD_DIM), F32)
    yc_f, yc_b, s_cf, s_cb = _retention_scan(p_c, logit_f, logit_b, s0, s0)
    yl_f, yl_b, _, _ = _retention_scan(p_l, logit_f, logit_b, s_cf, s_cb)
    return (yc_f, yc_b), (yl_f, yl_b)


MERGE_ROWS = 512
MERGE_HALVES = 2


def _merge_kernel(x_ref, ya_ref, yb_ref, yrf_ref, yrb_ref, g_ref, gate_ref, w_ref, g2_ref, sh2_ref, sc2_ref, wr_ref,
                  o_ref, h_ref, aff_ref):
    wa = ya_ref.shape[-1]
    wb = yb_ref.shape[-1]
    hr = x_ref.shape[0] // MERGE_HALVES
    halves = [slice(j * hr, (j + 1) * hr) for j in range(MERGE_HALVES)]
    wr = wr_ref[...]
    w_hi = wr.astype(BF16)
    w_lo_f = wr - w_hi.astype(F32)
    w_hi_lo = (w_hi.astype(F32) + pltpu.roll(w_lo_f, N_EXPERTS, 1)).astype(BF16)
    normed = []
    for rows in halves:
        na = (_rms(ya_ref[rows, :]) * g_ref[:, :wa]).astype(BF16)
        nb = (_rms(yb_ref[rows, :]) * g_ref[:, wa:wa + wb]).astype(BF16)
        nr = (_rms(yrf_ref[rows, :] + yrb_ref[rows, :]) * g_ref[:, wa + wb:]).astype(BF16)
        normed.append((na, nb, nr))
    accs = []
    for na, nb, nr in normed:
        acc = jnp.dot(na, w_ref[:wa, :], preferred_element_type=F32)
        acc = acc + jnp.dot(nb, w_ref[wa:wa + wb, :], preferred_element_type=F32)
        accs.append(acc + jnp.dot(nr, w_ref[wa + wb:, :], preferred_element_type=F32))
    for rows, acc in zip(halves, accs):
        x_new = x_ref[rows, :] + gate_ref[...] * acc
        o_ref[rows, :] = x_new
        h = (_rms(x_new) * g2_ref[...]) * (1.0 + sc2_ref[...]) + sh2_ref[...]
        h_hi = h.astype(BF16)
        h_ref[rows, :] = h_hi
        h_lo = (h - h_hi.astype(F32)).astype(BF16)
        hh = jnp.dot(h_hi, w_hi_lo, preferred_element_type=F32)
        logits = hh + (pltpu.roll(hh, LANE - N_EXPERTS, 1) + jnp.dot(h_lo, w_hi, preferred_element_type=F32))
        lane = lax.broadcasted_iota(jnp.int32, logits.shape, 1)
        logits = jnp.where(lane < N_EXPERTS, logits, NEG_BIG)
        e = jnp.exp(logits - jnp.max(logits, axis=-1, keepdims=True))
        aff_ref[rows, :] = e / jnp.sum(e, axis=-1, keepdims=True)


def _merge_out(x, ya, yb, yr, g, gate, w_out, g2, shift2, scale2, w_router_pad):
    b, n, d = x.shape
    yrf, yrb = yr
    tm = min(n, MERGE_ROWS)
    dm = w_out.shape[0]

    def tok(width):
        return pl.BlockSpec((None, tm, width), lambda bb, i: (bb, i, 0))

    row = pl.BlockSpec((1, d), lambda bb, i: (0, 0))
    per_batch = pl.BlockSpec((None, 1, d), lambda bb, i: (bb, 0, 0))
    return pl.pallas_call(
        _merge_kernel,
        grid=(b, n // tm),
        in_specs=[tok(d), tok(ya.shape[-1]), tok(yb.shape[-1]), tok(yrf.shape[-1]), tok(yrb.shape[-1]),
                  pl.BlockSpec((1, dm), lambda bb, i: (0, 0)), per_batch,
                  pl.BlockSpec((dm, d), lambda bb, i: (0, 0)),
                  row, per_batch, per_batch,
                  pl.BlockSpec((d, LANE), lambda bb, i: (0, 0))],
        out_specs=[tok(d), tok(d), tok(LANE)],
        out_shape=[jax.ShapeDtypeStruct((b, n, d), F32), jax.ShapeDtypeStruct((b, n, d), BF16),
                   jax.ShapeDtypeStruct((b, n, LANE), F32)],
        compiler_params=_cparams(("parallel", "parallel"), VMEM_MIB_MAX),
        name="merge_out",
    )(x, ya, yb, yrf, yrb, g.reshape(1, dm), gate, w_out, g2.reshape(1, d), shift2, scale2, w_router_pad)


def _ffn_up_kernel(xs_ref, wg_ref, wu_ref, hm_ref, wg_scr, wu_scr):
    @pl.when(pl.program_id(1) == 0)
    def _():
        wg_scr[...] = wg_ref[...].astype(BF16)
        wu_scr[...] = wu_ref[...].astype(BF16)

    xs = xs_ref[...]
    a = jnp.dot(xs, wg_scr[...], preferred_element_type=F32)
    u = jnp.dot(xs, wu_scr[...], preferred_element_type=F32)
    hm_ref[...] = (_silu(a) * u).astype(BF16)


def _ffn_down_kernel(hm_ref, gate_ref, wd_ref, o_ref, wd_scr):
    @pl.when(pl.program_id(1) == 0)
    def _():
        wd_scr[...] = wd_ref[...].astype(BF16)

    o_ref[...] = (jnp.dot(hm_ref[...], wd_scr[...], preferred_element_type=F32) * gate_ref[...]).astype(o_ref.dtype)


def _expert_ffn(xs, gates, w_gate, w_up, w_down, layer):
    e, t, d = xs.shape
    f = w_gate.shape[-1]
    tm = min(t, 512)
    tm_down = min(t, 1024)
    hm = pl.pallas_call(
        _ffn_up_kernel,
        grid=(e, t // tm),
        in_specs=[pl.BlockSpec((None, tm, d), lambda ee, i: (ee, i, 0)),
                  pl.BlockSpec((None, None, d, f), lambda ee, i: (layer, ee, 0, 0)),
                  pl.BlockSpec((None, None, d, f), lambda ee, i: (layer, ee, 0, 0))],
        out_specs=pl.BlockSpec((None, tm, f), lambda ee, i: (ee, i, 0)),
        out_shape=jax.ShapeDtypeStruct((e, t, f), BF16),
        scratch_shapes=[pltpu.VMEM((d, f), BF16), pltpu.VMEM((d, f), BF16)],
        compiler_params=_cparams(("parallel", "arbitrary"), VMEM_MIB_XLARGE),
        name="expert_ffn_up",
    )(xs, w_gate, w_up)
    return pl.pallas_call(
        _ffn_down_kernel,
        grid=(e, t // tm_down),
        in_specs=[pl.BlockSpec((None, tm_down, f), lambda ee, i: (ee, i, 0)),
                  pl.BlockSpec((None, tm_down, 1), lambda ee, i: (ee, i, 0)),
                  pl.BlockSpec((None, None, f, d), lambda ee, i: (layer, ee, 0, 0))],
        out_specs=pl.BlockSpec((None, tm_down, d), lambda ee, i: (ee, i, 0)),
        out_shape=jax.ShapeDtypeStruct((e, t, d), BF16),
        scratch_shapes=[pltpu.VMEM((f, d), BF16)],
        compiler_params=_cparams(("parallel", "arbitrary"), VMEM_MIB_XLARGE),
        name="expert_ffn_down",
    )(hm, gates, w_down)


COMBINE_TOKENS = 256
COMBINE_ROWS = 512


def _combine_kernel(*refs, final_norm, aliased):
    if aliased:
        tile_ref, win_ref, flag_ref, x_ref, yp_ref, tok_ref, gate_ref, gn_ref, _, o_ref, acc_scr = refs
    else:
        tile_ref, win_ref, flag_ref, x_ref, yp_ref, tok_ref, gate_ref, gn_ref, o_ref, acc_scr = refs
    s = pl.program_id(0)
    flags = flag_ref[s]
    t = acc_scr.shape[0]

    @pl.when((flags & 1) != 0)
    def _():
        acc_scr[...] = jnp.zeros(acc_scr.shape, F32)

    @pl.when((flags & 2) != 0)
    def _():
        tcol = tile_ref[s] * t + lax.broadcasted_iota(jnp.int32, (t, 1), 0)
        onehot = jnp.where(tcol == tok_ref[...], 1.0, 0.0).astype(BF16)
        acc_scr[...] += jnp.dot(onehot, yp_ref[...], preferred_element_type=F32)

    @pl.when((flags & 4) != 0)
    def _():
        y = x_ref[...] + gate_ref[...] * acc_scr[...]
        if final_norm:
            y = _rms(y) * gn_ref[...]
        o_ref[...] = y


def _moe_combine(x, y_rows, tok_rows, gate, norm_g, final_norm, batch0=0, n_batches=None, out_prev=None):
    b, n, d = x.shape
    r = y_rows.shape[0]
    t = min(COMBINE_TOKENS, n)
    w = COMBINE_ROWS
    tiles_per_batch = n // t
    ntiles = (b if n_batches is None else n_batches) * tiles_per_batch
    tile0 = batch0 * tiles_per_batch
    nwin_total = r // w
    n_steps = nwin_total + 2 * ntiles
    i32 = jnp.int32
    tok_sorted, order = lax.sort_key_val(tok_rows.astype(i32), jnp.arange(r, dtype=i32))
    yp = jnp.take(y_rows, order, axis=0, mode="clip")
    edges = (tile0 + jnp.arange(ntiles + 1, dtype=i32)) * t
    bounds = jnp.sum((tok_sorted[None, :] < edges[:, None]).astype(i32), axis=1)
    w_lo = jnp.minimum(bounds[:-1] // w, nwin_total - 1)
    w_hi = jnp.maximum((bounds[1:] + w - 1) // w, w_lo + 1)
    nwin = w_hi - w_lo
    step0 = jnp.cumsum(nwin) - nwin
    total = jnp.sum(nwin)
    s = jnp.arange(n_steps, dtype=i32)
    tile_s = jnp.sum((s[:, None] >= step0[None, :]).astype(i32), axis=1) - 1
    sel = tile_s[:, None] == jnp.arange(ntiles, dtype=i32)[None, :]
    tile_s = tile_s + tile0

    def of_tile(v):
        return jnp.sum(jnp.where(sel, v[None, :], 0), axis=1)

    step0_s, w_lo_s, nwin_s = of_tile(step0), of_tile(w_lo), of_tile(nwin)
    win_s = jnp.minimum(w_lo_s + (s - step0_s), w_lo_s + nwin_s - 1)
    valid = s < total
    first = valid & (s == step0_s)
    last = valid & (s == step0_s + nwin_s - 1)
    flags = first.astype(i32) + 2 * valid.astype(i32) + 4 * last.astype(i32)
    in_specs = [pl.BlockSpec((t, d), lambda s, tl, wn, fl: (tl[s], 0)),
                pl.BlockSpec((w, d), lambda s, tl, wn, fl: (wn[s], 0)),
                pl.BlockSpec((None, 1, w), lambda s, tl, wn, fl: (wn[s], 0, 0)),
                pl.BlockSpec((None, 1, d), lambda s, tl, wn, fl: (tl[s] // tiles_per_batch, 0, 0)),
                pl.BlockSpec((1, d), lambda s, tl, wn, fl: (0, 0))]
    args = [tile_s, win_s, flags, x.reshape(b * n, d), yp, tok_sorted.reshape(nwin_total, 1, w), gate,
            norm_g.reshape(1, d)]
    aliases = {}
    if out_prev is not None:
        in_specs.append(pl.BlockSpec(memory_space=pl.ANY))
        args.append(out_prev.reshape(b * n, d))
        aliases = {len(args) - 1: 0}
    out = pl.pallas_call(
        functools.partial(_combine_kernel, final_norm=final_norm, aliased=out_prev is not None),
        grid_spec=pltpu.PrefetchScalarGridSpec(
            num_scalar_prefetch=3,
            grid=(n_steps,),
            in_specs=in_specs,
            out_specs=pl.BlockSpec((t, d), lambda s, tl, wn, fl: (tl[s], 0)),
            scratch_shapes=[pltpu.VMEM((t, d), F32)]),
        out_shape=jax.ShapeDtypeStruct((b * n, d), F32),
        compiler_params=_cparams(("arbitrary",), VMEM_MIB_MEDIUM),
        input_output_aliases=aliases,
        name="moe_combine",
    )(*args)
    return out.reshape(b, n, d)


def _moe_select(aff):
    b, n, _ = aff.shape
    cap = EC_CAPACITY * n // N_EXPERTS
    gate, idx = lax.top_k(jnp.swapaxes(aff[..., :N_EXPERTS], 1, 2), cap)
    flat = (idx + (jnp.arange(b, dtype=idx.dtype) * n)[:, None, None]).transpose(1, 0, 2).reshape(-1)
    return flat, gate.transpose(1, 0, 2).reshape(N_EXPERTS, b * cap, 1)


def _moe_gather(h, flat):
    b, n, d = h.shape
    return jnp.take(h.reshape(b * n, d), flat, axis=0, mode="clip").reshape(N_EXPERTS, -1, d)


def _moe_latent_split(x, h, aff, gate2, experts, norm_g, final_norm):
    b, n, d = x.shape
    size = b // 2 if b % 2 == 0 else b
    routed = []
    for b0 in range(0, b, size):
        flat, gates = _moe_select(aff[b0:b0 + size])
        flat = flat + b0 * n
        routed.append((b0, flat, gates, _moe_gather(h, flat)))
    out = None
    for b0, flat, gates, xs in routed:
        y = _expert_ffn(xs, gates, *experts)
        out = _moe_combine(x, y.reshape(-1, d), flat, gate2, norm_g, final_norm, b0, size, out)
    return out


def _moe_experts(x, xs, gates, flat, gate2, experts, norm_g, final_norm):
    y = _expert_ffn(xs, gates, *experts)
    return _moe_combine(x, y.reshape(-1, x.shape[-1]), flat, gate2, norm_g, final_norm)


_ROPE_SRC = np.concatenate([np.arange(16, 32), np.arange(0, 16), np.arange(48, 64), np.arange(32, 48)])
_ROPE_SIGN = np.concatenate([-np.ones(16), np.ones(16), -np.ones(16), np.ones(16)]).astype(np.float32)


def _rope_perm_cols(w):
    return w[:, _ROPE_SRC] * _ROPE_SIGN


def _pad_w_in(w_in):
    c_kr = MLA_Q_LORA + MLA_KV_LORA
    kr = w_in[:, c_kr:c_kr + MLA_ROPE_DIM]
    return jnp.concatenate([w_in[:, :c_kr + MLA_ROPE_DIM], _rope_perm_cols(kr), w_in[:, c_kr + MLA_ROPE_DIM:]],
                           axis=1).astype(BF16)


def _expand_w_uq(w_uq):
    w = w_uq.reshape(MLA_Q_LORA, N_HEADS_MLA, HEAD_DIM + MLA_ROPE_DIM)
    nope = w[:, :, :HEAD_DIM]
    rope = w[:, :, HEAD_DIM:]
    partner = rope[:, :, _ROPE_SRC] * _ROPE_SIGN
    return jnp.concatenate([nope, rope, partner], axis=-1).reshape(MLA_Q_LORA, N_HEADS_MLA * MLA_QK_PAD).astype(BF16)


def _rope_tables(n):
    t = jnp.arange(n)
    n_freq = MLA_ROPE_DIM // 4
    inv = ROPE_BASE ** (-jnp.arange(n_freq, dtype=F32) / n_freq)
    ang_r = (t // GRID_W).astype(F32)[:, None] * inv
    ang_c = (t % GRID_W).astype(F32)[:, None] * inv
    zeros = jnp.zeros((n, LANE - MLA_ROPE_DIM), F32)
    cos = jnp.concatenate([jnp.cos(ang_r), jnp.cos(ang_r), jnp.cos(ang_c), jnp.cos(ang_c), zeros], axis=1)
    sin = jnp.concatenate([jnp.sin(ang_r), jnp.sin(ang_r), jnp.sin(ang_c), jnp.sin(ang_c), zeros], axis=1)
    return cos, sin


def _identity_tables(n):
    ones = jnp.ones((n, MLA_ROPE_DIM), F32)
    zeros = jnp.zeros((n, LANE - MLA_ROPE_DIM), F32)
    return jnp.concatenate([ones, zeros], axis=1), jnp.zeros((n, LANE), F32)


def _layer(x_l, x_c, mod, lp, last, final_g):
    b, n, d = x_l.shape
    n_ctx = x_c.shape[1]

    def chunk(k, ctx):
        rows = jnp.broadcast_to(mod[b, k * d:(k + 1) * d], (b, d)) if ctx else mod[:b, k * d:(k + 1) * d]
        return rows.reshape(b, 1, d)

    w_in = _pad_w_in(lp['w_in'])
    p_l = _norm_proj(x_l, lp['norm1_g'], chunk(0, False), chunk(1, False), w_in)
    p_c = _norm_proj(x_c, lp['norm1_g'], chunk(0, True), chunk(1, True), w_in)

    cos_l, sin_l = _rope_tables(n)
    cos_c, sin_c = _identity_tables(n_ctx)
    w_uk = lp['mla_w_uk'].reshape(MLA_KV_LORA, -1).astype(BF16)
    w_uv_t = lp['mla_w_uv'].reshape(MLA_KV_LORA, -1).T.astype(BF16)
    w_q_t = _expand_w_uq(lp['mla_w_uq']).T
    k_l, v_l = _mla_kv_prep(p_l, lp['mla_kv_norm_g'], w_uk, w_uv_t, cos_l, sin_l)
    k_c, v_c = _mla_kv_prep(p_c, lp['mla_kv_norm_g'], w_uk, w_uv_t, cos_c, sin_c)
    q_l = _mla_q_prep(p_l, lp['mla_q_norm_g'], w_q_t, cos_l.T, sin_l.T)
    a_l = _attention_t(q_l, [(k_c, v_c), (k_l, v_l)], N_HEADS_MLA, MLA_QK_PAD, HEAD_DIM)

    b_l = _nat_latent(p_l, p_c, _nat_bias_table(lp['nat_rpb']))
    r_c, r_l = _retention(p_c, p_l, lp['ret_decay_f'], lp['ret_decay_b'])

    w_out = lp['w_out'].astype(BF16)
    w_router = jnp.zeros((d, LANE), F32).at[:, :N_EXPERTS].set(lp['w_router'])

    x_l, h_l, aff_l = _merge_out(x_l, a_l, b_l, r_l, lp['out_norm_g'], chunk(2, False), w_out,
                                 lp['norm2_g'], chunk(3, False), chunk(4, False), w_router)
    if last:
        return _moe_latent_split(x_l, h_l, aff_l, chunk(5, False), lp['experts'], final_g, True), None
    flat_l, gates_l = _moe_select(aff_l)

    flat_l, p_c, x_c = lax.optimization_barrier((flat_l, p_c, x_c))
    xs_l = _moe_gather(h_l, flat_l)
    q_c = _mla_q_prep(p_c, lp['mla_q_norm_g'], w_q_t, cos_c.T, sin_c.T)
    a_c = _attention_t(q_c, [(k_c, v_c)], N_HEADS_MLA, MLA_QK_PAD, HEAD_DIM)
    b_c = _attention(p_c, COL_NAT_Q, HEAD_DIM, [(p_c, COL_NAT_K, p_c, COL_NAT_V)], HEAD_DIM, N_HEADS_NAT,
                     HEAD_DIM ** -0.5)
    x_c, h_c, aff_c = _merge_out(x_c, a_c, b_c, r_c, lp['out_norm_g'], chunk(2, True), w_out,
                                 lp['norm2_g'], chunk(3, True), chunk(4, True), w_router)
    flat_c, gates_c = _moe_select(aff_c)
    x_c = _moe_experts(x_c, _moe_gather(h_c, flat_c), gates_c, flat_c, chunk(5, True), lp['experts'], final_g, False)
    xs_l, x_c = lax.optimization_barrier((xs_l, x_c))
    x_l = _moe_experts(x_l, xs_l, gates_l, flat_l, chunk(5, False), lp['experts'], final_g, False)
    return x_l, x_c


def kernel(x, c, ctx, c_ctx, w_mod, b_mod, norm1_g, w_in, mla_q_norm_g, mla_kv_norm_g, mla_w_uq, mla_w_uk,
           mla_w_uv, nat_rpb, ret_decay_f, ret_decay_b, out_norm_g, w_out, norm2_g, w_router, w_gate, w_up,
           w_down, final_norm_g):
    depth = w_mod.shape[0]
    cvecs = jnp.concatenate([c, c_ctx[None, :]], axis=0)
    x_l, x_c = x, ctx
    for i in range(depth):
        lp = {
            'norm1_g': norm1_g[i], 'w_in': w_in[i], 'mla_q_norm_g': mla_q_norm_g[i],
            'mla_kv_norm_g': mla_kv_norm_g[i], 'mla_w_uq': mla_w_uq[i], 'mla_w_uk': mla_w_uk[i],
            'mla_w_uv': mla_w_uv[i], 'nat_rpb': nat_rpb[i], 'ret_decay_f': ret_decay_f[i],
            'ret_decay_b': ret_decay_b[i], 'out_norm_g': out_norm_g[i], 'w_out': w_out[i],
            'norm2_g': norm2_g[i], 'w_router': w_router[i], 'experts': (w_gate, w_up, w_down, i),
        }
        mod = _modulation(cvecs, w_mod, b_mod[i], i)
        x_l, x_c = _layer(x_l, x_c, mod, lp, i == depth - 1, final_norm_g)
    return x_l
```

```python
import functools

import numpy as np
import jax
import jax.numpy as jnp
from jax import lax
from jax.experimental import pallas as pl
from jax.experimental.pallas import tpu as pltpu

F32 = jnp.float32
BF16 = jnp.bfloat16

GRID_W = 64
HEAD_DIM = 128
N_HEADS_MLA = 8
N_HEADS_NAT = 4
N_HEADS_RET = 4
MLA_ROPE_DIM = 64
MLA_Q_LORA = 512
MLA_KV_LORA = 256
MLA_SCALE = (HEAD_DIM + MLA_ROPE_DIM) ** -0.5
LOG2E = 1.4426950408889634
MLA_Q_SCALE = MLA_SCALE * LOG2E
MLA_QK_PAD = 256
NAT_ROWS = 8
NAT_COLS = 16
RET_CHUNK = 128
N_EXPERTS = 16
EC_CAPACITY = 2
ROPE_BASE = 10000.0
EPS = 1e-6
NEG_BIG = -1e30
LANE = 128

COL_CQ = 0
COL_CKV = 4
COL_KR = 6
COL_NAT_Q = 7
COL_NAT_K = 11
COL_NAT_V = 15
COL_RET_Q = 19
COL_RET_K = 23
COL_RET_V = 27
COL_RET_GF = 31
COL_RET_GB = 35
IN_COLS_PAD = 39 * LANE


VMEM_MIB_SMALL = 32
VMEM_MIB_MEDIUM = 40
VMEM_MIB_LARGE = 48
VMEM_MIB_XLARGE = 56
VMEM_MIB_MAX = 58


def _cparams(sem, vmem_mib):
    return pltpu.CompilerParams(dimension_semantics=sem, vmem_limit_bytes=vmem_mib * 1024 * 1024)


def _silu(a):
    return a * (1.0 / (1.0 + jnp.exp(-a)))


def _rms(x):
    return x * lax.rsqrt(jnp.mean(x * x, axis=-1, keepdims=True) + EPS)


def _mod_kernel(c_ref, w_ref, b_ref, o_ref):
    a = _silu(c_ref[...])
    a_hi = a.astype(BF16)
    a_lo = (a - a_hi.astype(F32)).astype(BF16)
    w = w_ref[...]
    w_hi = w.astype(BF16)
    w_lo = (w - w_hi.astype(F32)).astype(BF16)
    r = jnp.dot(jnp.concatenate([a_hi, a_lo], axis=0), w_hi, preferred_element_type=F32)
    o_ref[...] = r[:8] + (r[8:] + jnp.dot(a_hi, w_lo, preferred_element_type=F32)) + b_ref[...]


def _modulation(cvecs, w_mod, b_mod, layer):
    n_rows, d = cvecs.shape
    n_out = w_mod.shape[-1]
    tn = 1024
    c8 = jnp.zeros((8, d), F32).at[:n_rows].set(cvecs)
    return pl.pallas_call(
        _mod_kernel,
        grid=(n_out // tn,),
        in_specs=[pl.BlockSpec((8, d), lambda j: (0, 0)),
                  pl.BlockSpec((None, d, tn), lambda j: (layer, 0, j)),
                  pl.BlockSpec((1, tn), lambda j: (0, j))],
        out_specs=pl.BlockSpec((8, tn), lambda j: (0, j)),
        out_shape=jax.ShapeDtypeStruct((8, n_out), F32),
        compiler_params=_cparams(("parallel",), VMEM_MIB_MEDIUM),
        name="modulation",
    )(c8, w_mod, b_mod.reshape(1, n_out))


NORM_PROJ_PARTS = 2


def _norm_proj_kernel(x_ref, g_ref, sh_ref, sc_ref, w_ref, o_ref, h_scr):
    j = pl.program_id(2)
    pr = x_ref.shape[0] // NORM_PROJ_PARTS
    parts = [slice(k * pr, (k + 1) * pr) for k in range(NORM_PROJ_PARTS)]

    @pl.when(j == 0)
    def _():
        hs = []
        for rows in parts:
            y = _rms(x_ref[rows, :]) * g_ref[...]
            h = (y * (1.0 + sc_ref[...]) + sh_ref[...]).astype(BF16)
            h_scr[rows, :] = h
            hs.append(h)
        for rows, h in zip(parts, hs):
            o_ref[rows, :] = jnp.dot(h, w_ref[...], preferred_element_type=F32).astype(o_ref.dtype)

    @pl.when(j != 0)
    def _():
        o_ref[...] = jnp.dot(h_scr[...], w_ref[...], preferred_element_type=F32).astype(o_ref.dtype)


def _norm_proj(x, g, shift, scale, w):
    b, n, d = x.shape
    c = w.shape[1]
    tm = min(n, 1024)
    tn = c // 3 if (c % (3 * LANE) == 0) else c
    return pl.pallas_call(
        _norm_proj_kernel,
        grid=(b, n // tm, c // tn),
        in_specs=[pl.BlockSpec((None, tm, d), lambda bb, i, j: (bb, i, 0)),
                  pl.BlockSpec((1, d), lambda bb, i, j: (0, 0)),
                  pl.BlockSpec((None, 1, d), lambda bb, i, j: (bb, 0, 0)),
                  pl.BlockSpec((None, 1, d), lambda bb, i, j: (bb, 0, 0)),
                  pl.BlockSpec((d, tn), lambda bb, i, j: (0, j))],
        out_specs=pl.BlockSpec((None, tm, tn), lambda bb, i, j: (bb, i, j)),
        out_shape=jax.ShapeDtypeStruct((b, n, c), BF16),
        scratch_shapes=[pltpu.VMEM((tm, d), BF16)],
        compiler_params=_cparams(("parallel", "parallel", "arbitrary"), VMEM_MIB_XLARGE),
        name="norm_proj",
    )(x, g.reshape(1, d), shift, scale, w)


def _rope_rotate(r, cos, sin):
    return r * cos + pltpu.roll(r, 64, 1) * sin


def _mla_q_kernel(cq_ref, g_ref, wt_ref, cos_ref, sin_ref, o_ref, *, n_heads):
    cqn_t = (_rms(cq_ref[...].astype(F32)) * g_ref[...]).T.astype(BF16)
    qe = jnp.dot(wt_ref[...], cqn_t, preferred_element_type=F32)
    cos = cos_ref[...]
    sin = sin_ref[...]
    for h in range(n_heads):
        c0 = h * MLA_QK_PAD
        o_ref[c0:c0 + LANE, :] = (qe[c0:c0 + LANE, :] * MLA_Q_SCALE).astype(BF16)
        r = qe[c0 + LANE:c0 + 2 * LANE, :]
        swapped = jnp.concatenate([r[MLA_ROPE_DIM:, :], r[:MLA_ROPE_DIM, :]], axis=0)
        o_ref[c0 + LANE:c0 + 2 * LANE, :] = ((r * cos + swapped * sin) * MLA_Q_SCALE).astype(BF16)


def _mla_q_prep(p, g, w_q_t, cos_t, sin_t):
    b, n, _ = p.shape
    tm = min(n, 512)
    cw = N_HEADS_MLA * MLA_QK_PAD
    return pl.pallas_call(
        functools.partial(_mla_q_kernel, n_heads=N_HEADS_MLA),
        grid=(b, n // tm),
        in_specs=[pl.BlockSpec((None, tm, MLA_Q_LORA), lambda bb, i: (bb, i, COL_CQ * LANE // MLA_Q_LORA)),
                  pl.BlockSpec((1, MLA_Q_LORA), lambda bb, i: (0, 0)),
                  pl.BlockSpec((cw, MLA_Q_LORA), lambda bb, i: (0, 0)),
                  pl.BlockSpec((LANE, tm), lambda bb, i: (0, i)),
                  pl.BlockSpec((LANE, tm), lambda bb, i: (0, i))],
        out_specs=pl.BlockSpec((None, cw, tm), lambda bb, i: (bb, 0, i)),
        out_shape=jax.ShapeDtypeStruct((b, cw, n), BF16),
        compiler_params=_cparams(("parallel", "parallel"), VMEM_MIB_MEDIUM),
        name="mla_q_prep",
    )(p, g.reshape(1, MLA_Q_LORA), w_q_t, cos_t, sin_t)


def _mla_kv_kernel(ckv_ref, kr_ref, g_ref, wuk_ref, wuvt_ref, cos_ref, sin_ref, k_ref, v_ref, *, n_heads):
    ckvn = _rms(ckv_ref[...].astype(F32)) * g_ref[...]
    kn = jnp.dot(ckvn.astype(BF16), wuk_ref[...], preferred_element_type=F32)
    v_ref[...] = jnp.dot(wuvt_ref[...], ckvn.T.astype(BF16), preferred_element_type=F32).astype(BF16)
    rot = _rope_rotate(kr_ref[...].astype(F32), cos_ref[...], sin_ref[...]).astype(BF16)
    for h in range(n_heads):
        c0 = h * MLA_QK_PAD
        k_ref[:, c0:c0 + LANE] = kn[:, h * LANE:(h + 1) * LANE].astype(BF16)
        k_ref[:, c0 + LANE:c0 + 2 * LANE] = rot


def _mla_kv_prep(p, g, w_uk, w_uv_t, cos, sin):
    b, n, _ = p.shape
    tm = min(n, 512)
    kw = N_HEADS_MLA * MLA_QK_PAD
    vw = N_HEADS_MLA * HEAD_DIM
    return pl.pallas_call(
        functools.partial(_mla_kv_kernel, n_heads=N_HEADS_MLA),
        grid=(b, n // tm),
        in_specs=[pl.BlockSpec((None, tm, MLA_KV_LORA), lambda bb, i: (bb, i, COL_CKV * LANE // MLA_KV_LORA)),
                  pl.BlockSpec((None, tm, LANE), lambda bb, i: (bb, i, COL_KR)),
                  pl.BlockSpec((1, MLA_KV_LORA), lambda bb, i: (0, 0)),
                  pl.BlockSpec((MLA_KV_LORA, vw), lambda bb, i: (0, 0)),
                  pl.BlockSpec((vw, MLA_KV_LORA), lambda bb, i: (0, 0)),
                  pl.BlockSpec((tm, LANE), lambda bb, i: (i, 0)),
                  pl.BlockSpec((tm, LANE), lambda bb, i: (i, 0))],
        out_specs=[pl.BlockSpec((None, tm, kw), lambda bb, i: (bb, i, 0)),
                   pl.BlockSpec((None, vw, tm), lambda bb, i: (bb, 0, i))],
        out_shape=[jax.ShapeDtypeStruct((b, n, kw), BF16), jax.ShapeDtypeStruct((b, vw, n), BF16)],
        compiler_params=_cparams(("parallel", "parallel"), VMEM_MIB_MEDIUM),
        name="mla_kv_prep",
    )(p, p, g.reshape(1, MLA_KV_LORA), w_uk, w_uv_t, cos, sin)


def _attn_kernel(*refs, n_src, scale, tk_max):
    q_ref = refs[0]
    o_ref = refs[1 + 2 * n_src]
    q = q_ref[...]
    if scale != 1.0:
        q = q.astype(F32) * scale
    q = q.astype(BF16)
    tq = q.shape[0]
    dv = o_ref.shape[-1]

    def step(k, v, carry):
        m, l, acc = carry
        s = lax.dot_general(q, k, (((1,), (1,)), ((), ())), preferred_element_type=F32)
        m_new = jnp.maximum(m, jnp.max(s, axis=-1, keepdims=True))
        p = jnp.exp(s - m_new)
        alpha = jnp.exp(m - m_new)
        l = alpha * l + jnp.sum(p, axis=-1, keepdims=True)
        acc = alpha * acc + jnp.dot(p.astype(BF16), v, preferred_element_type=F32)
        return m_new, l, acc

    carry = (jnp.full((tq, 1), NEG_BIG, F32), jnp.zeros((tq, 1), F32), jnp.zeros((tq, dv), F32))
    for s_i in range(n_src):
        k_ref = refs[1 + 2 * s_i]
        v_ref = refs[2 + 2 * s_i]
        nk = k_ref.shape[0]
        tk = min(nk, tk_max)
        if nk == tk:
            carry = step(k_ref[...].astype(BF16), v_ref[...].astype(BF16), carry)
        else:
            def body(c, cr, k_ref=k_ref, v_ref=v_ref, tk=tk):
                k0 = pl.multiple_of(c * tk, tk)
                return step(k_ref[pl.ds(k0, tk), :].astype(BF16), v_ref[pl.ds(k0, tk), :].astype(BF16), cr)
            carry = lax.fori_loop(0, nk // tk, body, carry)
    _, l, acc = carry
    o_ref[...] = acc / l


def _attention(q_arr, q_blk0, dq, sources, dv, n_heads, scale, tq=256, tk_max=512):
    b, nq, _ = q_arr.shape
    tq = min(nq, tq)
    in_specs = [pl.BlockSpec((None, tq, dq), lambda bb, h, i: (bb, i, q_blk0 + h))]
    args = [q_arr]
    for k_arr, k_blk0, v_arr, v_blk0 in sources:
        nk = k_arr.shape[1]
        in_specs.append(pl.BlockSpec((None, nk, dq), lambda bb, h, i, o=k_blk0: (bb, 0, o + h)))
        in_specs.append(pl.BlockSpec((None, nk, dv), lambda bb, h, i, o=v_blk0: (bb, 0, o + h)))
        args += [k_arr, v_arr]
    return pl.pallas_call(
        functools.partial(_attn_kernel, n_src=len(sources), scale=scale, tk_max=tk_max),
        grid=(b, n_heads, nq // tq),
        in_specs=in_specs,
        out_specs=pl.BlockSpec((None, tq, dv), lambda bb, h, i: (bb, i, h)),
        out_shape=jax.ShapeDtypeStruct((b, nq, n_heads * dv), F32),
        compiler_params=_cparams(("parallel", "parallel", "arbitrary"), VMEM_MIB_LARGE),
        name="attention",
    )(*args)


ATTN_TQ = 2048
ATTN_TK = 512
ATTN_GROUP = 4


def _attn_t_kernel(*refs, n_src):
    q_ref = refs[0]
    o_ref = refs[1 + 2 * n_src]
    q = q_ref[...]
    tq = q.shape[1]
    dv = o_ref.shape[-1]

    def scores(k):
        return jnp.dot(k, q, preferred_element_type=F32)

    def update(s, vt, carry):
        m, l, acc = carry
        m_new = jnp.maximum(m, jnp.max(s, axis=0, keepdims=True))
        p = jnp.exp2(s - m_new)
        alpha = jnp.exp2(m - m_new)
        l = alpha * l + jnp.sum(p, axis=0, keepdims=True)
        acc = alpha * acc + jnp.dot(vt, p.astype(BF16), preferred_element_type=F32)
        return m_new, l, acc

    carry = (jnp.full((1, tq), NEG_BIG, F32), jnp.zeros((1, tq), F32), jnp.zeros((dv, tq), F32))
    for s_i in range(n_src):
        k_ref = refs[1 + 2 * s_i]
        vt_ref = refs[2 + 2 * s_i]
        nk = k_ref.shape[0]
        tk = min(nk, ATTN_TK)
        n_chunks = nk // tk
        if n_chunks == 1:
            carry = update(scores(k_ref[...]), vt_ref[...], carry)
            continue
        group = max(g for g in (ATTN_GROUP, 2, 1) if n_chunks % g == 0)

        def body(c, cr, k_ref=k_ref, vt_ref=vt_ref, tk=tk, group=group):
            starts = [pl.multiple_of((c * group + g) * tk, tk) for g in range(group)]
            ss = [scores(k_ref[pl.ds(k0, tk), :]) for k0 in starts]
            for s, k0 in zip(ss, starts):
                cr = update(s, vt_ref[:, pl.ds(k0, tk)], cr)
            return cr
        carry = lax.fori_loop(0, n_chunks // group, body, carry)
    _, l, acc = carry
    o_ref[...] = (acc / l).T


def _attention_t(q_arr, sources, n_heads, dq, dv):
    b, _, nq = q_arr.shape
    tq = min(nq, ATTN_TQ)
    in_specs = [pl.BlockSpec((None, dq, tq), lambda bb, h, i: (bb, h, i))]
    args = [q_arr]
    for k_arr, vt_arr in sources:
        nk = k_arr.shape[1]
        in_specs.append(pl.BlockSpec((None, nk, dq), lambda bb, h, i: (bb, 0, h)))
        in_specs.append(pl.BlockSpec((None, dv, nk), lambda bb, h, i: (bb, h, 0)))
        args += [k_arr, vt_arr]
    return pl.pallas_call(
        functools.partial(_attn_t_kernel, n_src=len(sources)),
        grid=(b, n_heads, nq // tq),
        in_specs=in_specs,
        out_specs=pl.BlockSpec((None, tq, dv), lambda bb, h, i: (bb, i, h)),
        out_shape=jax.ShapeDtypeStruct((b, nq, n_heads * dv), F32),
        compiler_params=_cparams(("parallel", "parallel", "arbitrary"), VMEM_MIB_XLARGE),
        name="attention_t",
    )(*args)


NAT_GROUP = 8
NAT_SLAB = NAT_GROUP + NAT_ROWS
NAT_HALVES = 2


def _nat_kernel(q_ref, k_ref, v_ref, kc_ref, vc_ref, bias_ref, o_ref, *, n_rows):
    i = pl.program_id(2)
    scale = HEAD_DIM ** -0.5
    nt = (((1,), (1,)), ((), ()))
    u0 = jnp.clip(i * NAT_GROUP - NAT_ROWS // 2, 0, n_rows - NAT_SLAB)
    t0 = pl.multiple_of(u0 * GRID_W, GRID_W)
    ks = k_ref[pl.ds(t0, NAT_SLAB * GRID_W), :].astype(BF16)
    vs = v_ref[pl.ds(t0, NAT_SLAB * GRID_W), :].astype(BF16)
    kc = kc_ref[...].astype(BF16)
    vc = vc_ref[...].astype(BF16)
    hq = q_ref.shape[0] // NAT_HALVES
    scores = []
    for j in range(NAT_HALVES):
        rows = slice(j * hq, (j + 1) * hq)
        q = (q_ref[rows, :].astype(F32) * scale).astype(BF16)
        scores.append((lax.dot_general(q, ks, nt, preferred_element_type=F32) + bias_ref[rows, :],
                       lax.dot_general(q, kc, nt, preferred_element_type=F32)))
    for j in range(NAT_HALVES):
        s_w, s_c = scores[j]
        m = jnp.maximum(jnp.max(s_w, axis=-1, keepdims=True), jnp.max(s_c, axis=-1, keepdims=True))
        p_w = jnp.exp(s_w - m)
        p_c = jnp.exp(s_c - m)
        l = jnp.sum(p_w, axis=-1, keepdims=True) + jnp.sum(p_c, axis=-1, keepdims=True)
        o = (jnp.dot(p_w.astype(BF16), vs, preferred_element_type=F32)
             + jnp.dot(p_c.astype(BF16), vc, preferred_element_type=F32))
        o_ref[j * hq:(j + 1) * hq, :] = o / l


def _nat_bias_table(rpb):
    h = rpb.shape[0]
    col = np.arange(GRID_W)
    c0 = np.clip(col - NAT_COLS // 2, 0, GRID_W - NAT_COLS)
    kc = np.arange(GRID_W)
    col_ok = (kc[None, :] >= c0[:, None]) & (kc[None, :] < c0[:, None] + NAT_COLS)
    dc = kc[None, :] - col[:, None] + NAT_COLS - 1
    sel_c = ((dc[None] == np.arange(2 * NAT_COLS - 1)[:, None, None]) & col_ok[None]).astype(np.float32)
    sel_c = sel_c.reshape(2 * NAT_COLS - 1, GRID_W * GRID_W)
    per_dr = jnp.einsum('hdc,cb->hdb', rpb.astype(F32), sel_c, precision=lax.Precision.HIGHEST)
    per_dr = jnp.where(col_ok.reshape(-1)[None, None], per_dr, NEG_BIG).reshape(h, 2 * NAT_ROWS - 1, GRID_W, GRID_W)
    return pl.pallas_call(
        _nat_table_kernel,
        grid=(h, 3),
        in_specs=[pl.BlockSpec((None, 2 * NAT_ROWS - 1, GRID_W, GRID_W), lambda hh, v: (hh, 0, 0, 0))],
        out_specs=pl.BlockSpec((None, None, NAT_GROUP * GRID_W, NAT_SLAB * GRID_W), lambda hh, v: (hh, v, 0, 0)),
        out_shape=jax.ShapeDtypeStruct((h, 3, NAT_GROUP * GRID_W, NAT_SLAB * GRID_W), F32),
        compiler_params=_cparams(("parallel", "arbitrary"), VMEM_MIB_SMALL),
        name="nat_bias_table",
    )(per_dr)


def _nat_table_kernel(per_dr_ref, o_ref):
    masked = jnp.full((GRID_W, GRID_W), NEG_BIG, F32)
    for variant, delta in enumerate((0, NAT_ROWS // 2, NAT_ROWS)):
        @pl.when(pl.program_id(1) == variant)
        def _(delta=delta):
            for g in range(NAT_GROUP):
                w0 = min(max(g + delta - NAT_ROWS // 2, 0), NAT_SLAB - NAT_ROWS)
                tiles = [per_dr_ref[u - (g + delta) + NAT_ROWS - 1] if w0 <= u < w0 + NAT_ROWS else masked
                         for u in range(NAT_SLAB)]
                o_ref[g * GRID_W:(g + 1) * GRID_W, :] = jnp.concatenate(tiles, axis=1)


def _nat_latent(p_l, p_c, bias_tbl):
    b, n, _ = p_l.shape
    n_ctx = p_c.shape[1]
    n_rows = n // GRID_W
    assert n_rows >= NAT_SLAB and n_rows % NAT_GROUP == 0
    n_groups = n_rows // NAT_GROUP
    tq = NAT_GROUP * GRID_W
    tk = NAT_SLAB * GRID_W

    def variant(i):
        return jnp.where(i == 0, 0, jnp.where(i == n_groups - 1, 2, 1))

    return pl.pallas_call(
        functools.partial(_nat_kernel, n_rows=n_rows),
        grid=(b, N_HEADS_NAT, n_groups),
        in_specs=[pl.BlockSpec((None, tq, LANE), lambda bb, h, i: (bb, i, COL_NAT_Q + h)),
                  pl.BlockSpec((None, n, LANE), lambda bb, h, i: (bb, 0, COL_NAT_K + h)),
                  pl.BlockSpec((None, n, LANE), lambda bb, h, i: (bb, 0, COL_NAT_V + h)),
                  pl.BlockSpec((None, n_ctx, LANE), lambda bb, h, i: (bb, 0, COL_NAT_K + h)),
                  pl.BlockSpec((None, n_ctx, LANE), lambda bb, h, i: (bb, 0, COL_NAT_V + h)),
                  pl.BlockSpec((None, None, tq, tk), lambda bb, h, i: (h, variant(i), 0, 0))],
        out_specs=pl.BlockSpec((None, tq, LANE), lambda bb, h, i: (bb, i, h)),
        out_shape=jax.ShapeDtypeStruct((b, n, N_HEADS_NAT * HEAD_DIM), F32),
        compiler_params=_cparams(("parallel", "parallel", "arbitrary"), VMEM_MIB_LARGE),
        name="nat_latent",
    )(p_l, p_l, p_l, p_c, p_c, bias_tbl)


def _ret_consts(logit, reverse):
    c_len = RET_CHUNK
    z = jnp.full((c_len, c_len), -logit, F32)
    lg = -(jnp.maximum(z, 0.0) + jnp.log1p(jnp.exp(-jnp.abs(z))))
    ci = lax.broadcasted_iota(jnp.int32, (c_len, c_len), 0).astype(F32)
    si = lax.broadcasted_iota(jnp.int32, (c_len, c_len), 1).astype(F32)
    if reverse:
        rel = si - ci
        wq = jnp.exp(lg * (c_len - ci))
        wk = jnp.exp(lg * ci)
    else:
        rel = ci - si
        wq = jnp.exp(lg * (ci + 1.0))
        wk = jnp.exp(lg * (c_len - 1.0 - ci))
    dmat = jnp.where(rel >= 0, jnp.exp(lg * jnp.maximum(rel, 0.0)), 0.0)
    return wq, wk, dmat, jnp.exp(lg * c_len)


def _ret_kernel(lf_ref, lb_ref, qf_ref, kf_ref, vf_ref, gf_ref, qb_ref, kb_ref, vb_ref, gb_ref, s0f_ref, s0b_ref,
                yf_ref, yb_ref, sf_ref, sb_ref, c_scr, *, n_chunks):
    h = pl.program_id(1)

    @pl.when(pl.program_id(2) == 0)
    def _():
        sf_ref[...] = s0f_ref[...]
        sb_ref[...] = s0b_ref[...]
        for j, arr in enumerate(_ret_consts(lf_ref[h], False) + _ret_consts(lb_ref[h], True)):
            c_scr[j] = arr

    c_len = RET_CHUNK
    nt = (((1,), (1,)), ((), ()))
    k_scale = HEAD_DIM ** -0.5
    dirs = []
    for d, refs in enumerate(((qf_ref, kf_ref, vf_ref, gf_ref, yf_ref), (qb_ref, kb_ref, vb_ref, gb_ref, yb_ref))):
        order = list(range(n_chunks)) if d == 0 else list(range(n_chunks - 1, -1, -1))
        dirs.append(refs + tuple(c_scr[4 * d + j] for j in range(4)) + (order,))

    units = []
    for step in range(n_chunks):
        for d in range(2):
            q_ref, k_ref, v_ref, _, _, _, wk, dmat, _, order = dirs[d]
            sl = slice(order[step] * c_len, (order[step] + 1) * c_len)
            qb = q_ref[sl, :].astype(BF16)
            kb = k_ref[sl, :]
            vb = v_ref[sl, :].astype(BF16)
            sc = lax.dot_general(qb, kb.astype(BF16), nt, preferred_element_type=F32) * (dmat * k_scale)
            kv = jnp.dot((kb.astype(F32) * (wk * k_scale)).T.astype(BF16), vb, preferred_element_type=F32)
            units.append((d, sl, qb, vb, sc, kv))
    states = [sf_ref[...], sb_ref[...]]
    cross = []
    for (d, sl, qb, vb, sc, kv) in units:
        wq, g_chunk = dirs[d][5], dirs[d][8]
        cross.append(wq * jnp.dot(qb, states[d].astype(BF16), preferred_element_type=F32))
        states[d] = g_chunk * states[d] + kv
    sf_ref[...] = states[0]
    sb_ref[...] = states[1]
    for (d, sl, qb, vb, sc, kv), o_cross in zip(units, cross):
        g_ref, y_ref = dirs[d][3], dirs[d][4]
        o = jnp.dot(sc.astype(BF16), vb, preferred_element_type=F32) + o_cross
        mu = jnp.mean(o, axis=-1, keepdims=True)
        dev = o - mu
        gn = dev * lax.rsqrt(jnp.mean(dev * dev, axis=-1, keepdims=True) + EPS)
        y_ref[sl, :] = _silu(g_ref[sl, :].astype(F32)) * gn


def _retention_scan(p, logit_f, logit_b, s0_f, s0_b):
    b, n, _ = p.shape
    hh = N_HEADS_RET
    tb = min(n, 16 * RET_CHUNK)
    nblk = n // tb

    def fwd(c0):
        return pl.BlockSpec((None, tb, LANE), lambda bb, h, i: (bb, i, c0 + h))

    def bwd(c0):
        return pl.BlockSpec((None, tb, LANE), lambda bb, h, i: (bb, nblk - 1 - i, c0 + h))

    state = pl.BlockSpec((None, None, HEAD_DIM, HEAD_DIM), lambda bb, h, i: (bb, h, 0, 0))
    smem = pl.BlockSpec(memory_space=pltpu.SMEM)
    y_shape = jax.ShapeDtypeStruct((b, n, hh * HEAD_DIM), F32)
    s_shape = jax.ShapeDtypeStruct((b, hh, HEAD_DIM, HEAD_DIM), F32)
    return pl.pallas_call(
        functools.partial(_ret_kernel, n_chunks=tb // RET_CHUNK),
        grid=(b, hh, nblk),
        in_specs=[smem, smem,
                  fwd(COL_RET_Q), fwd(COL_RET_K), fwd(COL_RET_V), fwd(COL_RET_GF),
                  bwd(COL_RET_Q), bwd(COL_RET_K), bwd(COL_RET_V), bwd(COL_RET_GB),
                  state, state],
        out_specs=[fwd(0), bwd(0), state, state],
        out_shape=[y_shape, y_shape, s_shape, s_shape],
        scratch_shapes=[pltpu.VMEM((8, RET_CHUNK, RET_CHUNK), F32)],
        compiler_params=_cparams(("parallel", "parallel", "arbitrary"), VMEM_MIB_SMALL),
        name="retention",
    )(logit_f.astype(F32), logit_b.astype(F32), p, p, p, p, p, p, p, p, s0_f, s0_b)


def _retention(p_c, p_l, logit_f, logit_b):
    b = p_c.shape[0]
    s0 = jnp.zeros((b, N_HEADS_RET, HEAD_DIM, HEAD_DIM), F32)
    yc_f, yc_b, s_cf, s_cb = _retention_scan(p_c, logit_f, logit_b, s0, s0)
    yl_f, yl_b, _, _ = _retention_scan(p_l, logit_f, logit_b, s_cf, s_cb)
    return (yc_f, yc_b), (yl_f, yl_b)


MERGE_ROWS = 512
MERGE_HALVES = 2


def _merge_kernel(x_ref, ya_ref, yb_ref, yrf_ref, yrb_ref, g_ref, gate_ref, w_ref, g2_ref, sh2_ref, sc2_ref, wr_ref,
                  o_ref, h_ref, aff_ref):
    wa = ya_ref.shape[-1]
    wb = yb_ref.shape[-1]
    hr = x_ref.shape[0] // MERGE_HALVES
    halves = [slice(j * hr, (j + 1) * hr) for j in range(MERGE_HALVES)]
    wr = wr_ref[...]
    w_hi = wr.astype(BF16)
    w_lo_f = wr - w_hi.astype(F32)
    w_hi_lo = (w_hi.astype(F32) + pltpu.roll(w_lo_f, N_EXPERTS, 1)).astype(BF16)
    normed = []
    for rows in halves:
        na = (_rms(ya_ref[rows, :]) * g_ref[:, :wa]).astype(BF16)
        nb = (_rms(yb_ref[rows, :]) * g_ref[:, wa:wa + wb]).astype(BF16)
        nr = (_rms(yrf_ref[rows, :] + yrb_ref[rows, :]) * g_ref[:, wa + wb:]).astype(BF16)
        normed.append((na, nb, nr))
    accs = []
    for na, nb, nr in normed:
        acc = jnp.dot(na, w_ref[:wa, :], preferred_element_type=F32)
        acc = acc + jnp.dot(nb, w_ref[wa:wa + wb, :], preferred_element_type=F32)
        accs.append(acc + jnp.dot(nr, w_ref[wa + wb:, :], preferred_element_type=F32))
    for rows, acc in zip(halves, accs):
        x_new = x_ref[rows, :] + gate_ref[...] * acc
        o_ref[rows, :] = x_new
        h = (_rms(x_new) * g2_ref[...]) * (1.0 + sc2_ref[...]) + sh2_ref[...]
        h_hi = h.astype(BF16)
        h_ref[rows, :] = h_hi
        h_lo = (h - h_hi.astype(F32)).astype(BF16)
        hh = jnp.dot(h_hi, w_hi_lo, preferred_element_type=F32)
        logits = hh + (pltpu.roll(hh, LANE - N_EXPERTS, 1) + jnp.dot(h_lo, w_hi, preferred_element_type=F32))
        lane = lax.broadcasted_iota(jnp.int32, logits.shape, 1)
        logits = jnp.where(lane < N_EXPERTS, logits, NEG_BIG)
        e = jnp.exp(logits - jnp.max(logits, axis=-1, keepdims=True))
        aff_ref[rows, :] = e / jnp.sum(e, axis=-1, keepdims=True)


def _merge_out(x, ya, yb, yr, g, gate, w_out, g2, shift2, scale2, w_router_pad):
    b, n, d = x.shape
    yrf, yrb = yr
    tm = min(n, MERGE_ROWS)
    dm = w_out.shape[0]

    def tok(width):
        return pl.BlockSpec((None, tm, width), lambda bb, i: (bb, i, 0))

    row = pl.BlockSpec((1, d), lambda bb, i: (0, 0))
    per_batch = pl.BlockSpec((None, 1, d), lambda bb, i: (bb, 0, 0))
    return pl.pallas_call(
        _merge_kernel,
        grid=(b, n // tm),
        in_specs=[tok(d), tok(ya.shape[-1]), tok(yb.shape[-1]), tok(yrf.shape[-1]), tok(yrb.shape[-1]),
                  pl.BlockSpec((1, dm), lambda bb, i: (0, 0)), per_batch,
                  pl.BlockSpec((dm, d), lambda bb, i: (0, 0)),
                  row, per_batch, per_batch,
                  pl.BlockSpec((d, LANE), lambda bb, i: (0, 0))],
        out_specs=[tok(d), tok(d), tok(LANE)],
        out_shape=[jax.ShapeDtypeStruct((b, n, d), F32), jax.ShapeDtypeStruct((b, n, d), BF16),
                   jax.ShapeDtypeStruct((b, n, LANE), F32)],
        compiler_params=_cparams(("parallel", "parallel"), VMEM_MIB_MAX),
        name="merge_out",
    )(x, ya, yb, yrf, yrb, g.reshape(1, dm), gate, w_out, g2.reshape(1, d), shift2, scale2, w_router_pad)


def _ffn_up_kernel(xs_ref, wg_ref, wu_ref, hm_ref, wg_scr, wu_scr):
    @pl.when(pl.program_id(1) == 0)
    def _():
        wg_scr[...] = wg_ref[...].astype(BF16)
        wu_scr[...] = wu_ref[...].astype(BF16)

    xs = xs_ref[...]
    a = jnp.dot(xs, wg_scr[...], preferred_element_type=F32)
    u = jnp.dot(xs, wu_scr[...], preferred_element_type=F32)
    hm_ref[...] = (_silu(a) * u).astype(BF16)


def _ffn_down_kernel(hm_ref, gate_ref, wd_ref, o_ref, wd_scr):
    @pl.when(pl.program_id(1) == 0)
    def _():
        wd_scr[...] = wd_ref[...].astype(BF16)

    o_ref[...] = (jnp.dot(hm_ref[...], wd_scr[...], preferred_element_type=F32) * gate_ref[...]).astype(o_ref.dtype)


def _expert_ffn(xs, gates, w_gate, w_up, w_down, layer):
    e, t, d = xs.shape
    f = w_gate.shape[-1]
    tm = min(t, 512)
    tm_down = min(t, 1024)
    hm = pl.pallas_call(
        _ffn_up_kernel,
        grid=(e, t // tm),
        in_specs=[pl.BlockSpec((None, tm, d), lambda ee, i: (ee, i, 0)),
                  pl.BlockSpec((None, None, d, f), lambda ee, i: (layer, ee, 0, 0)),
                  pl.BlockSpec((None, None, d, f), lambda ee, i: (layer, ee, 0, 0))],
        out_specs=pl.BlockSpec((None, tm, f), lambda ee, i: (ee, i, 0)),
        out_shape=jax.ShapeDtypeStruct((e, t, f), BF16),
        scratch_shapes=[pltpu.VMEM((d, f), BF16), pltpu.VMEM((d, f), BF16)],
        compiler_params=_cparams(("parallel", "arbitrary"), VMEM_MIB_XLARGE),
        name="expert_ffn_up",
    )(xs, w_gate, w_up)
    return pl.pallas_call(
        _ffn_down_kernel,
        grid=(e, t // tm_down),
        in_specs=[pl.BlockSpec((None, tm_down, f), lambda ee, i: (ee, i, 0)),
                  pl.BlockSpec((None, tm_down, 1), lambda ee, i: (ee, i, 0)),
                  pl.BlockSpec((None, None, f, d), lambda ee, i: (layer, ee, 0, 0))],
        out_specs=pl.BlockSpec((None, tm_down, d), lambda ee, i: (ee, i, 0)),
        out_shape=jax.ShapeDtypeStruct((e, t, d), BF16),
        scratch_shapes=[pltpu.VMEM((f, d), BF16)],
        compiler_params=_cparams(("parallel", "arbitrary"), VMEM_MIB_XLARGE),
        name="expert_ffn_down",
    )(hm, gates, w_down)


COMBINE_TOKENS = 256
COMBINE_ROWS = 512


def _combine_kernel(*refs, final_norm, aliased):
    if aliased:
        tile_ref, win_ref, flag_ref, x_ref, yp_ref, tok_ref, gate_ref, gn_ref, _, o_ref, acc_scr = refs
    else:
        tile_ref, win_ref, flag_ref, x_ref, yp_ref, tok_ref, gate_ref, gn_ref, o_ref, acc_scr = refs
    s = pl.program_id(0)
    flags = flag_ref[s]
    t = acc_scr.shape[0]

    @pl.when((flags & 1) != 0)
    def _():
        acc_scr[...] = jnp.zeros(acc_scr.shape, F32)

    @pl.when((flags & 2) != 0)
    def _():
        tcol = tile_ref[s] * t + lax.broadcasted_iota(jnp.int32, (t, 1), 0)
        onehot = jnp.where(tcol == tok_ref[...], 1.0, 0.0).astype(BF16)
        acc_scr[...] += jnp.dot(onehot, yp_ref[...], preferred_element_type=F32)

    @pl.when((flags & 4) != 0)
    def _():
        y = x_ref[...] + gate_ref[...] * acc_scr[...]
        if final_norm:
            y = _rms(y) * gn_ref[...]
        o_ref[...] = y


def _moe_combine(x, y_rows, tok_rows, gate, norm_g, final_norm, batch0=0, n_batches=None, out_prev=None):
    b, n, d = x.shape
    r = y_rows.shape[0]
    t = min(COMBINE_TOKENS, n)
    w = COMBINE_ROWS
    tiles_per_batch = n // t
    ntiles = (b if n_batches is None else n_batches) * tiles_per_batch
    tile0 = batch0 * tiles_per_batch
    nwin_total = r // w
    n_steps = nwin_total + 2 * ntiles
    i32 = jnp.int32
    tok_sorted, order = lax.sort_key_val(tok_rows.astype(i32), jnp.arange(r, dtype=i32))
    yp = jnp.take(y_rows, order, axis=0, mode="clip")
    edges = (tile0 + jnp.arange(ntiles + 1, dtype=i32)) * t
    bounds = jnp.sum((tok_sorted[None, :] < edges[:, None]).astype(i32), axis=1)
    w_lo = jnp.minimum(bounds[:-1] // w, nwin_total - 1)
    w_hi = jnp.maximum((bounds[1:] + w - 1) // w, w_lo + 1)
    nwin = w_hi - w_lo
    step0 = jnp.cumsum(nwin) - nwin
    total = jnp.sum(nwin)
    s = jnp.arange(n_steps, dtype=i32)
    tile_s = jnp.sum((s[:, None] >= step0[None, :]).astype(i32), axis=1) - 1
    sel = tile_s[:, None] == jnp.arange(ntiles, dtype=i32)[None, :]
    tile_s = tile_s + tile0

    def of_tile(v):
        return jnp.sum(jnp.where(sel, v[None, :], 0), axis=1)

    step0_s, w_lo_s, nwin_s = of_tile(step0), of_tile(w_lo), of_tile(nwin)
    win_s = jnp.minimum(w_lo_s + (s - step0_s), w_lo_s + nwin_s - 1)
    valid = s < total
    first = valid & (s == step0_s)
    last = valid & (s == step0_s + nwin_s - 1)
    flags = first.astype(i32) + 2 * valid.astype(i32) + 4 * last.astype(i32)
    in_specs = [pl.BlockSpec((t, d), lambda s, tl, wn, fl: (tl[s], 0)),
                pl.BlockSpec((w, d), lambda s, tl, wn, fl: (wn[s], 0)),
                pl.BlockSpec((None, 1, w), lambda s, tl, wn, fl: (wn[s], 0, 0)),
                pl.BlockSpec((None, 1, d), lambda s, tl, wn, fl: (tl[s] // tiles_per_batch, 0, 0)),
                pl.BlockSpec((1, d), lambda s, tl, wn, fl: (0, 0))]
    args = [tile_s, win_s, flags, x.reshape(b * n, d), yp, tok_sorted.reshape(nwin_total, 1, w), gate,
            norm_g.reshape(1, d)]
    aliases = {}
    if out_prev is not None:
        in_specs.append(pl.BlockSpec(memory_space=pl.ANY))
        args.append(out_prev.reshape(b * n, d))
        aliases = {len(args) - 1: 0}
    out = pl.pallas_call(
        functools.partial(_combine_kernel, final_norm=final_norm, aliased=out_prev is not None),
        grid_spec=pltpu.PrefetchScalarGridSpec(
            num_scalar_prefetch=3,
            grid=(n_steps,),
            in_specs=in_specs,
            out_specs=pl.BlockSpec((t, d), lambda s, tl, wn, fl: (tl[s], 0)),
            scratch_shapes=[pltpu.VMEM((t, d), F32)]),
        out_shape=jax.ShapeDtypeStruct((b * n, d), F32),
        compiler_params=_cparams(("arbitrary",), VMEM_MIB_MEDIUM),
        input_output_aliases=aliases,
        name="moe_combine",
    )(*args)
    return out.reshape(b, n, d)


def _moe_select(aff):
    b, n, _ = aff.shape
    cap = EC_CAPACITY * n // N_EXPERTS
    gate, idx = lax.top_k(jnp.swapaxes(aff[..., :N_EXPERTS], 1, 2), cap)
    flat = (idx + (jnp.arange(b, dtype=idx.dtype) * n)[:, None, None]).transpose(1, 0, 2).reshape(-1)
    return flat, gate.transpose(1, 0, 2).reshape(N_EXPERTS, b * cap, 1)


def _moe_gather(h, flat):
    b, n, d = h.shape
    return jnp.take(h.reshape(b * n, d), flat, axis=0, mode="clip").reshape(N_EXPERTS, -1, d)


def _moe_latent_split(x, h, aff, gate2, experts, norm_g, final_norm):
    b, n, d = x.shape
    size = b // 2 if b % 2 == 0 else b
    routed = []
    for b0 in range(0, b, size):
        flat, gates = _moe_select(aff[b0:b0 + size])
        flat = flat + b0 * n
        routed.append((b0, flat, gates, _moe_gather(h, flat)))
    out = None
    for b0, flat, gates, xs in routed:
        y = _expert_ffn(xs, gates, *experts)
        out = _moe_combine(x, y.reshape(-1, d), flat, gate2, norm_g, final_norm, b0, size, out)
    return out


def _moe_experts(x, xs, gates, flat, gate2, experts, norm_g, final_norm):
    y = _expert_ffn(xs, gates, *experts)
    return _moe_combine(x, y.reshape(-1, x.shape[-1]), flat, gate2, norm_g, final_norm)


_ROPE_SRC = np.concatenate([np.arange(16, 32), np.arange(0, 16), np.arange(48, 64), np.arange(32, 48)])
_ROPE_SIGN = np.concatenate([-np.ones(16), np.ones(16), -np.ones(16), np.ones(16)]).astype(np.float32)


def _rope_perm_cols(w):
    return w[:, _ROPE_SRC] * _ROPE_SIGN


def _pad_w_in(w_in):
    c_kr = MLA_Q_LORA + MLA_KV_LORA
    kr = w_in[:, c_kr:c_kr + MLA_ROPE_DIM]
    return jnp.concatenate([w_in[:, :c_kr + MLA_ROPE_DIM], _rope_perm_cols(kr), w_in[:, c_kr + MLA_ROPE_DIM:]],
                           axis=1).astype(BF16)


def _expand_w_uq(w_uq):
    w = w_uq.reshape(MLA_Q_LORA, N_HEADS_MLA, HEAD_DIM + MLA_ROPE_DIM)
    nope = w[:, :, :HEAD_DIM]
    rope = w[:, :, HEAD_DIM:]
    partner = rope[:, :, _ROPE_SRC] * _ROPE_SIGN
    return jnp.concatenate([nope, rope, partner], axis=-1).reshape(MLA_Q_LORA, N_HEADS_MLA * MLA_QK_PAD).astype(BF16)


def _rope_tables(n):
    t = jnp.arange(n)
    n_freq = MLA_ROPE_DIM // 4
    inv = ROPE_BASE ** (-jnp.arange(n_freq, dtype=F32) / n_freq)
    ang_r = (t // GRID_W).astype(F32)[:, None] * inv
    ang_c = (t % GRID_W).astype(F32)[:, None] * inv
    zeros = jnp.zeros((n, LANE - MLA_ROPE_DIM), F32)
    cos = jnp.concatenate([jnp.cos(ang_r), jnp.cos(ang_r), jnp.cos(ang_c), jnp.cos(ang_c), zeros], axis=1)
    sin = jnp.concatenate([jnp.sin(ang_r), jnp.sin(ang_r), jnp.sin(ang_c), jnp.sin(ang_c), zeros], axis=1)
    return cos, sin


def _identity_tables(n):
    ones = jnp.ones((n, MLA_ROPE_DIM), F32)
    zeros = jnp.zeros((n, LANE - MLA_ROPE_DIM), F32)
    return jnp.concatenate([ones, zeros], axis=1), jnp.zeros((n, LANE), F32)


def _layer(x_l, x_c, mod, lp, last, final_g):
    b, n, d = x_l.shape
    n_ctx = x_c.shape[1]

    def chunk(k, ctx):
        rows = jnp.broadcast_to(mod[b, k * d:(k + 1) * d], (b, d)) if ctx else mod[:b, k * d:(k + 1) * d]
        return rows.reshape(b, 1, d)

    w_in = _pad_w_in(lp['w_in'])
    p_l = _norm_proj(x_l, lp['norm1_g'], chunk(0, False), chunk(1, False), w_in)
    p_c = _norm_proj(x_c, lp['norm1_g'], chunk(0, True), chunk(1, True), w_in)

    cos_l, sin_l = _rope_tables(n)
    cos_c, sin_c = _identity_tables(n_ctx)
    w_uk = lp['mla_w_uk'].reshape(MLA_KV_LORA, -1).astype(BF16)
    w_uv_t = lp['mla_w_uv'].reshape(MLA_KV_LORA, -1).T.astype(BF16)
    w_q_t = _expand_w_uq(lp['mla_w_uq']).T
    k_l, v_l = _mla_kv_prep(p_l, lp['mla_kv_norm_g'], w_uk, w_uv_t, cos_l, sin_l)
    k_c, v_c = _mla_kv_prep(p_c, lp['mla_kv_norm_g'], w_uk, w_uv_t, cos_c, sin_c)
    q_l = _mla_q_prep(p_l, lp['mla_q_norm_g'], w_q_t, cos_l.T, sin_l.T)
    a_l = _attention_t(q_l, [(k_c, v_c), (k_l, v_l)], N_HEADS_MLA, MLA_QK_PAD, HEAD_DIM)

    b_l = _nat_latent(p_l, p_c, _nat_bias_table(lp['nat_rpb']))
    r_c, r_l = _retention(p_c, p_l, lp['ret_decay_f'], lp['ret_decay_b'])

    w_out = lp['w_out'].astype(BF16)
    w_router = jnp.zeros((d, LANE), F32).at[:, :N_EXPERTS].set(lp['w_router'])

    x_l, h_l, aff_l = _merge_out(x_l, a_l, b_l, r_l, lp['out_norm_g'], chunk(2, False), w_out,
                                 lp['norm2_g'], chunk(3, False), chunk(4, False), w_router)
    if last:
        return _moe_latent_split(x_l, h_l, aff_l, chunk(5, False), lp['experts'], final_g, True), None
    flat_l, gates_l = _moe_select(aff_l)

    flat_l, p_c, x_c = lax.optimization_barrier((flat_l, p_c, x_c))
    xs_l = _moe_gather(h_l, flat_l)
    q_c = _mla_q_prep(p_c, lp['mla_q_norm_g'], w_q_t, cos_c.T, sin_c.T)
    a_c = _attention_t(q_c, [(k_c, v_c)], N_HEADS_MLA, MLA_QK_PAD, HEAD_DIM)
    b_c = _attention(p_c, COL_NAT_Q, HEAD_DIM, [(p_c, COL_NAT_K, p_c, COL_NAT_V)], HEAD_DIM, N_HEADS_NAT,
                     HEAD_DIM ** -0.5)
    x_c, h_c, aff_c = _merge_out(x_c, a_c, b_c, r_c, lp['out_norm_g'], chunk(2, True), w_out,
                                 lp['norm2_g'], chunk(3, True), chunk(4, True), w_router)
    flat_c, gates_c = _moe_select(aff_c)
    x_c = _moe_experts(x_c, _moe_gather(h_c, flat_c), gates_c, flat_c, chunk(5, True), lp['experts'], final_g, False)
    xs_l, x_c = lax.optimization_barrier((xs_l, x_c))
    x_l = _moe_experts(x_l, xs_l, gates_l, flat_l, chunk(5, False), lp['experts'], final_g, False)
    return x_l, x_c


def kernel(x, c, ctx, c_ctx, w_mod, b_mod, norm1_g, w_in, mla_q_norm_g, mla_kv_norm_g, mla_w_uq, mla_w_uk,
           mla_w_uv, nat_rpb, ret_decay_f, ret_decay_b, out_norm_g, w_out, norm2_g, w_router, w_gate, w_up,
           w_down, final_norm_g):
    depth = w_mod.shape[0]
    cvecs = jnp.concatenate([c, c_ctx[None, :]], axis=0)
    x_l, x_c = x, ctx
    for i in range(depth):
        lp = {
            'norm1_g': norm1_g[i], 'w_in': w_in[i], 'mla_q_norm_g': mla_q_norm_g[i],
            'mla_kv_norm_g': mla_kv_norm_g[i], 'mla_w_uq': mla_w_uq[i], 'mla_w_uk': mla_w_uk[i],
            'mla_w_uv': mla_w_uv[i], 'nat_rpb': nat_rpb[i], 'ret_decay_f': ret_decay_f[i],
            'ret_decay_b': ret_decay_b[i], 'out_norm_g': out_norm_g[i], 'w_out': w_out[i],
            'norm2_g': norm2_g[i], 'w_router': w_router[i], 'experts': (w_gate, w_up, w_down, i),
        }
        mod = _modulation(cvecs, w_mod, b_mod[i], i)
        x_l, x_c = _layer(x_l, x_c, mod, lp, i == depth - 1, final_norm_g)
    return x_l
```

```python
import functools

import numpy as np
import jax
import jax.numpy as jnp
from jax import lax
from jax.experimental import pallas as pl
from jax.experimental.pallas import tpu as pltpu

F32 = jnp.float32
BF16 = jnp.bfloat16

GRID_W = 64
HEAD_DIM = 128
N_HEADS_MLA = 8
N_HEADS_NAT = 4
N_HEADS_RET = 4
MLA_ROPE_DIM = 64
MLA_Q_LORA = 512
MLA_KV_LORA = 256
MLA_SCALE = (HEAD_DIM + MLA_ROPE_DIM) ** -0.5
LOG2E = 1.4426950408889634
MLA_Q_SCALE = MLA_SCALE * LOG2E
MLA_QK_PAD = 256
NAT_ROWS = 8
NAT_COLS = 16
RET_CHUNK = 128
N_EXPERTS = 16
EC_CAPACITY = 2
ROPE_BASE = 10000.0
EPS = 1e-6
NEG_BIG = -1e30
LANE = 128

COL_CQ = 0
COL_CKV = 4
COL_KR = 6
COL_NAT_Q = 7
COL_NAT_K = 11
COL_NAT_V = 15
COL_RET_Q = 19
COL_RET_K = 23
COL_RET_V = 27
COL_RET_GF = 31
COL_RET_GB = 35
IN_COLS_PAD = 39 * LANE


VMEM_MIB_SMALL = 32
VMEM_MIB_MEDIUM = 40
VMEM_MIB_LARGE = 48
VMEM_MIB_XLARGE = 56
VMEM_MIB_MAX = 58


def _cparams(sem, vmem_mib):
    return pltpu.CompilerParams(dimension_semantics=sem, vmem_limit_bytes=vmem_mib * 1024 * 1024)


def _silu(a):
    return a * (1.0 / (1.0 + jnp.exp(-a)))


def _rms(x):
    return x * lax.rsqrt(jnp.mean(x * x, axis=-1, keepdims=True) + EPS)


def _mod_kernel(c_ref, w_ref, b_ref, o_ref):
    a = _silu(c_ref[...])
    a_hi = a.astype(BF16)
    a_lo = (a - a_hi.astype(F32)).astype(BF16)
    w = w_ref[...]
    w_hi = w.astype(BF16)
    w_lo = (w - w_hi.astype(F32)).astype(BF16)
    r = jnp.dot(jnp.concatenate([a_hi, a_lo], axis=0), w_hi, preferred_element_type=F32)
    o_ref[...] = r[:8] + (r[8:] + jnp.dot(a_hi, w_lo, preferred_element_type=F32)) + b_ref[...]


def _modulation(cvecs, w_mod, b_mod, layer):
    n_rows, d = cvecs.shape
    n_out = w_mod.shape[-1]
    tn = 1024
    c8 = jnp.zeros((8, d), F32).at[:n_rows].set(cvecs)
    return pl.pallas_call(
        _mod_kernel,
        grid=(n_out // tn,),
        in_specs=[pl.BlockSpec((8, d), lambda j: (0, 0)),
                  pl.BlockSpec((None, d, tn), lambda j: (layer, 0, j)),
                  pl.BlockSpec((1, tn), lambda j: (0, j))],
        out_specs=pl.BlockSpec((8, tn), lambda j: (0, j)),
        out_shape=jax.ShapeDtypeStruct((8, n_out), F32),
        compiler_params=_cparams(("parallel",), VMEM_MIB_MEDIUM),
        name="modulation",
    )(c8, w_mod, b_mod.reshape(1, n_out))


NORM_PROJ_PARTS = 2


def _norm_proj_kernel(x_ref, g_ref, sh_ref, sc_ref, w_ref, o_ref, h_scr):
    j = pl.program_id(2)
    pr = x_ref.shape[0] // NORM_PROJ_PARTS
    parts = [slice(k * pr, (k + 1) * pr) for k in range(NORM_PROJ_PARTS)]

    @pl.when(j == 0)
    def _():
        hs = []
        for rows in parts:
            y = _rms(x_ref[rows, :]) * g_ref[...]
            h = (y * (1.0 + sc_ref[...]) + sh_ref[...]).astype(BF16)
            h_scr[rows, :] = h
            hs.append(h)
        for rows, h in zip(parts, hs):
            o_ref[rows, :] = jnp.dot(h, w_ref[...], preferred_element_type=F32).astype(o_ref.dtype)

    @pl.when(j != 0)
    def _():
        o_ref[...] = jnp.dot(h_scr[...], w_ref[...], preferred_element_type=F32).astype(o_ref.dtype)


def _norm_proj(x, g, shift, scale, w):
    b, n, d = x.shape
    c = w.shape[1]
    tm = min(n, 1024)
    tn = c // 3 if (c % (3 * LANE) == 0) else c
    return pl.pallas_call(
        _norm_proj_kernel,
        grid=(b, n // tm, c // tn),
        in_specs=[pl.BlockSpec((None, tm, d), lambda bb, i, j: (bb, i, 0)),
                  pl.BlockSpec((1, d), lambda bb, i, j: (0, 0)),
                  pl.BlockSpec((None, 1, d), lambda bb, i, j: (bb, 0, 0)),
                  pl.BlockSpec((None, 1, d), lambda bb, i, j: (bb, 0, 0)),
                  pl.BlockSpec((d, tn), lambda bb, i, j: (0, j))],
        out_specs=pl.BlockSpec((None, tm, tn), lambda bb, i, j: (bb, i, j)),
        out_shape=jax.ShapeDtypeStruct((b, n, c), BF16),
        scratch_shapes=[pltpu.VMEM((tm, d), BF16)],
        compiler_params=_cparams(("parallel", "parallel", "arbitrary"), VMEM_MIB_XLARGE),
        name="norm_proj",
    )(x, g.reshape(1, d), shift, scale, w)


def _rope_rotate(r, cos, sin):
    return r * cos + pltpu.roll(r, 64, 1) * sin


def _mla_q_kernel(cq_ref, g_ref, wt_ref, cos_ref, sin_ref, o_ref, *, n_heads):
    cqn_t = (_rms(cq_ref[...].astype(F32)) * g_ref[...]).T.astype(BF16)
    qe = jnp.dot(wt_ref[...], cqn_t, preferred_element_type=F32)
    cos = cos_ref[...]
    sin = sin_ref[...]
    for h in range(n_heads):
        c0 = h * MLA_QK_PAD
        o_ref[c0:c0 + LANE, :] = (qe[c0:c0 + LANE, :] * MLA_Q_SCALE).astype(BF16)
        r = qe[c0 + LANE:c0 + 2 * LANE, :]
        swapped = jnp.concatenate([r[MLA_ROPE_DIM:, :], r[:MLA_ROPE_DIM, :]], axis=0)
        o_ref[c0 + LANE:c0 + 2 * LANE, :] = ((r * cos + swapped * sin) * MLA_Q_SCALE).astype(BF16)


def _mla_q_prep(p, g, w_q_t, cos_t, sin_t):
    b, n, _ = p.shape
    tm = min(n, 512)
    cw = N_HEADS_MLA * MLA_QK_PAD
    return pl.pallas_call(
        functools.partial(_mla_q_kernel, n_heads=N_HEADS_MLA),
        grid=(b, n // tm),
        in_specs=[pl.BlockSpec((None, tm, MLA_Q_LORA), lambda bb, i: (bb, i, COL_CQ * LANE // MLA_Q_LORA)),
                  pl.BlockSpec((1, MLA_Q_LORA), lambda bb, i: (0, 0)),
                  pl.BlockSpec((cw, MLA_Q_LORA), lambda bb, i: (0, 0)),
                  pl.BlockSpec((LANE, tm), lambda bb, i: (0, i)),
                  pl.BlockSpec((LANE, tm), lambda bb, i: (0, i))],
        out_specs=pl.BlockSpec((None, cw, tm), lambda bb, i: (bb, 0, i)),
        out_shape=jax.ShapeDtypeStruct((b, cw, n), BF16),
        compiler_params=_cparams(("parallel", "parallel"), VMEM_MIB_MEDIUM),
        name="mla_q_prep",
    )(p, g.reshape(1, MLA_Q_LORA), w_q_t, cos_t, sin_t)


def _mla_kv_kernel(ckv_ref, kr_ref, g_ref, wuk_ref, wuvt_ref, cos_ref, sin_ref, k_ref, v_ref, *, n_heads):
    ckvn = _rms(ckv_ref[...].astype(F32)) * g_ref[...]
    kn = jnp.dot(ckvn.astype(BF16), wuk_ref[...], preferred_element_type=F32)
    v_ref[...] = jnp.dot(wuvt_ref[...], ckvn.T.astype(BF16), preferred_element_type=F32).astype(BF16)
    rot = _rope_rotate(kr_ref[...].astype(F32), cos_ref[...], sin_ref[...]).astype(BF16)
    for h in range(n_heads):
        c0 = h * MLA_QK_PAD
        k_ref[:, c0:c0 + LANE] = kn[:, h * LANE:(h + 1) * LANE].astype(BF16)
        k_ref[:, c0 + LANE:c0 + 2 * LANE] = rot


def _mla_kv_prep(p, g, w_uk, w_uv_t, cos, sin):
    b, n, _ = p.shape
    tm = min(n, 512)
    kw = N_HEADS_MLA * MLA_QK_PAD
    vw = N_HEADS_MLA * HEAD_DIM
    return pl.pallas_call(
        functools.partial(_mla_kv_kernel, n_heads=N_HEADS_MLA),
        grid=(b, n // tm),
        in_specs=[pl.BlockSpec((None, tm, MLA_KV_LORA), lambda bb, i: (bb, i, COL_CKV * LANE // MLA_KV_LORA)),
                  pl.BlockSpec((None, tm, LANE), lambda bb, i: (bb, i, COL_KR)),
                  pl.BlockSpec((1, MLA_KV_LORA), lambda bb, i: (0, 0)),
                  pl.BlockSpec((MLA_KV_LORA, vw), lambda bb, i: (0, 0)),
                  pl.BlockSpec((vw, MLA_KV_LORA), lambda bb, i: (0, 0)),
                  pl.BlockSpec((tm, LANE), lambda bb, i: (i, 0)),
                  pl.BlockSpec((tm, LANE), lambda bb, i: (i, 0))],
        out_specs=[pl.BlockSpec((None, tm, kw), lambda bb, i: (bb, i, 0)),
                   pl.BlockSpec((None, vw, tm), lambda bb, i: (bb, 0, i))],
        out_shape=[jax.ShapeDtypeStruct((b, n, kw), BF16), jax.ShapeDtypeStruct((b, vw, n), BF16)],
        compiler_params=_cparams(("parallel", "parallel"), VMEM_MIB_MEDIUM),
        name="mla_kv_prep",
    )(p, p, g.reshape(1, MLA_KV_LORA), w_uk, w_uv_t, cos, sin)


def _attn_kernel(*refs, n_src, scale, tk_max):
    q_ref = refs[0]
    o_ref = refs[1 + 2 * n_src]
    q = q_ref[...]
    if scale != 1.0:
        q = q.astype(F32) * scale
    q = q.astype(BF16)
    tq = q.shape[0]
    dv = o_ref.shape[-1]

    def step(k, v, carry):
        m, l, acc = carry
        s = lax.dot_general(q, k, (((1,), (1,)), ((), ())), preferred_element_type=F32)
        m_new = jnp.maximum(m, jnp.max(s, axis=-1, keepdims=True))
        p = jnp.exp(s - m_new)
        alpha = jnp.exp(m - m_new)
        l = alpha * l + jnp.sum(p, axis=-1, keepdims=True)
        acc = alpha * acc + jnp.dot(p.astype(BF16), v, preferred_element_type=F32)
        return m_new, l, acc

    carry = (jnp.full((tq, 1), NEG_BIG, F32), jnp.zeros((tq, 1), F32), jnp.zeros((tq, dv), F32))
    for s_i in range(n_src):
        k_ref = refs[1 + 2 * s_i]
        v_ref = refs[2 + 2 * s_i]
        nk = k_ref.shape[0]
        tk = min(nk, tk_max)
        if nk == tk:
            carry = step(k_ref[...].astype(BF16), v_ref[...].astype(BF16), carry)
        else:
            def body(c, cr, k_ref=k_ref, v_ref=v_ref, tk=tk):
                k0 = pl.multiple_of(c * tk, tk)
                return step(k_ref[pl.ds(k0, tk), :].astype(BF16), v_ref[pl.ds(k0, tk), :].astype(BF16), cr)
            carry = lax.fori_loop(0, nk // tk, body, carry)
    _, l, acc = carry
    o_ref[...] = acc / l


def _attention(q_arr, q_blk0, dq, sources, dv, n_heads, scale, tq=256, tk_max=512):
    b, nq, _ = q_arr.shape
    tq = min(nq, tq)
    in_specs = [pl.BlockSpec((None, tq, dq), lambda bb, h, i: (bb, i, q_blk0 + h))]
    args = [q_arr]
    for k_arr, k_blk0, v_arr, v_blk0 in sources:
        nk = k_arr.shape[1]
        in_specs.append(pl.BlockSpec((None, nk, dq), lambda bb, h, i, o=k_blk0: (bb, 0, o + h)))
        in_specs.append(pl.BlockSpec((None, nk, dv), lambda bb, h, i, o=v_blk0: (bb, 0, o + h)))
        args += [k_arr, v_arr]
    return pl.pallas_call(
        functools.partial(_attn_kernel, n_src=len(sources), scale=scale, tk_max=tk_max),
        grid=(b, n_heads, nq // tq),
        in_specs=in_specs,
        out_specs=pl.BlockSpec((None, tq, dv), lambda bb, h, i: (bb, i, h)),
        out_shape=jax.ShapeDtypeStruct((b, nq, n_heads * dv), F32),
        compiler_params=_cparams(("parallel", "parallel", "arbitrary"), VMEM_MIB_LARGE),
        name="attention",
    )(*args)


ATTN_TQ = 2048
ATTN_TK = 512
ATTN_GROUP = 8


def _attn_t_kernel(*refs, n_src):
    q_ref = refs[0]
    o_ref = refs[1 + 2 * n_src]
    q = q_ref[...]
    tq = q.shape[1]
    dv = o_ref.shape[-1]

    def scores(k):
        return jnp.dot(k, q, preferred_element_type=F32)

    def update(s, vt, carry):
        m, l, acc = carry
        m_new = jnp.maximum(m, jnp.max(s, axis=0, keepdims=True))
        p = jnp.exp2(s - m_new)
        alpha = jnp.exp2(m - m_new)
        l = alpha * l + jnp.sum(p, axis=0, keepdims=True)
        acc = alpha * acc + jnp.dot(vt, p.astype(BF16), preferred_element_type=F32)
        return m_new, l, acc

    carry = (jnp.full((1, tq), NEG_BIG, F32), jnp.zeros((1, tq), F32), jnp.zeros((dv, tq), F32))
    for s_i in range(n_src):
        k_ref = refs[1 + 2 * s_i]
        vt_ref = refs[2 + 2 * s_i]
        nk = k_ref.shape[0]
        tk = min(nk, ATTN_TK)
        n_chunks = nk // tk
        if n_chunks == 1:
            carry = update(scores(k_ref[...]), vt_ref[...], carry)
            continue
        group = max(g for g in (ATTN_GROUP, 2, 1) if n_chunks % g == 0)

        def body(c, cr, k_ref=k_ref, vt_ref=vt_ref, tk=tk, group=group):
            starts = [pl.multiple_of((c * group + g) * tk, tk) for g in range(group)]
            ss = [scores(k_ref[pl.ds(k0, tk), :]) for k0 in starts]
            for s, k0 in zip(ss, starts):
                cr = update(s, vt_ref[:, pl.ds(k0, tk)], cr)
            return cr
        carry = lax.fori_loop(0, n_chunks // group, body, carry)
    _, l, acc = carry
    o_ref[...] = (acc / l).T


def _attention_t(q_arr, sources, n_heads, dq, dv):
    b, _, nq = q_arr.shape
    tq = min(nq, ATTN_TQ)
    in_specs = [pl.BlockSpec((None, dq, tq), lambda bb, h, i: (bb, h, i))]
    args = [q_arr]
    for k_arr, vt_arr in sources:
        nk = k_arr.shape[1]
        in_specs.append(pl.BlockSpec((None, nk, dq), lambda bb, h, i: (bb, 0, h)))
        in_specs.append(pl.BlockSpec((None, dv, nk), lambda bb, h, i: (bb, h, 0)))
        args += [k_arr, vt_arr]
    return pl.pallas_call(
        functools.partial(_attn_t_kernel, n_src=len(sources)),
        grid=(b, n_heads, nq // tq),
        in_specs=in_specs,
        out_specs=pl.BlockSpec((None, tq, dv), lambda bb, h, i: (bb, i, h)),
        out_shape=jax.ShapeDtypeStruct((b, nq, n_heads * dv), F32),
        compiler_params=_cparams(("parallel", "parallel", "arbitrary"), VMEM_MIB_XLARGE),
        name="attention_t",
    )(*args)


NAT_GROUP = 8
NAT_SLAB = NAT_GROUP + NAT_ROWS
NAT_HALVES = 2


def _nat_kernel(q_ref, k_ref, v_ref, kc_ref, vc_ref, bias_ref, o_ref, *, n_rows):
    i = pl.program_id(2)
    scale = HEAD_DIM ** -0.5
    nt = (((1,), (1,)), ((), ()))
    u0 = jnp.clip(i * NAT_GROUP - NAT_ROWS // 2, 0, n_rows - NAT_SLAB)
    t0 = pl.multiple_of(u0 * GRID_W, GRID_W)
    ks = k_ref[pl.ds(t0, NAT_SLAB * GRID_W), :].astype(BF16)
    vs = v_ref[pl.ds(t0, NAT_SLAB * GRID_W), :].astype(BF16)
    kc = kc_ref[...].astype(BF16)
    vc = vc_ref[...].astype(BF16)
    hq = q_ref.shape[0] // NAT_HALVES
    scores = []
    for j in range(NAT_HALVES):
        rows = slice(j * hq, (j + 1) * hq)
        q = (q_ref[rows, :].astype(F32) * scale).astype(BF16)
        scores.append((lax.dot_general(q, ks, nt, preferred_element_type=F32) + bias_ref[rows, :],
                       lax.dot_general(q, kc, nt, preferred_element_type=F32)))
    for j in range(NAT_HALVES):
        s_w, s_c = scores[j]
        m = jnp.maximum(jnp.max(s_w, axis=-1, keepdims=True), jnp.max(s_c, axis=-1, keepdims=True))
        p_w = jnp.exp(s_w - m)
        p_c = jnp.exp(s_c - m)
        l = jnp.sum(p_w, axis=-1, keepdims=True) + jnp.sum(p_c, axis=-1, keepdims=True)
        o = (jnp.dot(p_w.astype(BF16), vs, preferred_element_type=F32)
             + jnp.dot(p_c.astype(BF16), vc, preferred_element_type=F32))
        o_ref[j * hq:(j + 1) * hq, :] = o / l


def _nat_bias_table(rpb):
    h = rpb.shape[0]
    col = np.arange(GRID_W)
    c0 = np.clip(col - NAT_COLS // 2, 0, GRID_W - NAT_COLS)
    kc = np.arange(GRID_W)
    col_ok = (kc[None, :] >= c0[:, None]) & (kc[None, :] < c0[:, None] + NAT_COLS)
    dc = kc[None, :] - col[:, None] + NAT_COLS - 1
    sel_c = ((dc[None] == np.arange(2 * NAT_COLS - 1)[:, None, None]) & col_ok[None]).astype(np.float32)
    sel_c = sel_c.reshape(2 * NAT_COLS - 1, GRID_W * GRID_W)
    per_dr = jnp.einsum('hdc,cb->hdb', rpb.astype(F32), sel_c, precision=lax.Precision.HIGHEST)
    per_dr = jnp.where(col_ok.reshape(-1)[None, None], per_dr, NEG_BIG).reshape(h, 2 * NAT_ROWS - 1, GRID_W, GRID_W)
    return pl.pallas_call(
        _nat_table_kernel,
        grid=(h, 3),
        in_specs=[pl.BlockSpec((None, 2 * NAT_ROWS - 1, GRID_W, GRID_W), lambda hh, v: (hh, 0, 0, 0))],
        out_specs=pl.BlockSpec((None, None, NAT_GROUP * GRID_W, NAT_SLAB * GRID_W), lambda hh, v: (hh, v, 0, 0)),
        out_shape=jax.ShapeDtypeStruct((h, 3, NAT_GROUP * GRID_W, NAT_SLAB * GRID_W), F32),
        compiler_params=_cparams(("parallel", "arbitrary"), VMEM_MIB_SMALL),
        name="nat_bias_table",
    )(per_dr)


def _nat_table_kernel(per_dr_ref, o_ref):
    masked = jnp.full((GRID_W, GRID_W), NEG_BIG, F32)
    for variant, delta in enumerate((0, NAT_ROWS // 2, NAT_ROWS)):
        @pl.when(pl.program_id(1) == variant)
        def _(delta=delta):
            for g in range(NAT_GROUP):
                w0 = min(max(g + delta - NAT_ROWS // 2, 0), NAT_SLAB - NAT_ROWS)
                tiles = [per_dr_ref[u - (g + delta) + NAT_ROWS - 1] if w0 <= u < w0 + NAT_ROWS else masked
                         for u in range(NAT_SLAB)]
                o_ref[g * GRID_W:(g + 1) * GRID_W, :] = jnp.concatenate(tiles, axis=1)


def _nat_latent(p_l, p_c, bias_tbl):
    b, n, _ = p_l.shape
    n_ctx = p_c.shape[1]
    n_rows = n // GRID_W
    assert n_rows >= NAT_SLAB and n_rows % NAT_GROUP == 0
    n_groups = n_rows // NAT_GROUP
    tq = NAT_GROUP * GRID_W
    tk = NAT_SLAB * GRID_W

    def variant(i):
        return jnp.where(i == 0, 0, jnp.where(i == n_groups - 1, 2, 1))

    return pl.pallas_call(
        functools.partial(_nat_kernel, n_rows=n_rows),
        grid=(b, N_HEADS_NAT, n_groups),
        in_specs=[pl.BlockSpec((None, tq, LANE), lambda bb, h, i: (bb, i, COL_NAT_Q + h)),
                  pl.BlockSpec((None, n, LANE), lambda bb, h, i: (bb, 0, COL_NAT_K + h)),
                  pl.BlockSpec((None, n, LANE), lambda bb, h, i: (bb, 0, COL_NAT_V + h)),
                  pl.BlockSpec((None, n_ctx, LANE), lambda bb, h, i: (bb, 0, COL_NAT_K + h)),
                  pl.BlockSpec((None, n_ctx, LANE), lambda bb, h, i: (bb, 0, COL_NAT_V + h)),
                  pl.BlockSpec((None, None, tq, tk), lambda bb, h, i: (h, variant(i), 0, 0))],
        out_specs=pl.BlockSpec((None, tq, LANE), lambda bb, h, i: (bb, i, h)),
        out_shape=jax.ShapeDtypeStruct((b, n, N_HEADS_NAT * HEAD_DIM), F32),
        compiler_params=_cparams(("parallel", "parallel", "arbitrary"), VMEM_MIB_LARGE),
        name="nat_latent",
    )(p_l, p_l, p_l, p_c, p_c, bias_tbl)


def _ret_consts(logit, reverse):
    c_len = RET_CHUNK
    z = jnp.full((c_len, c_len), -logit, F32)
    lg = -(jnp.maximum(z, 0.0) + jnp.log1p(jnp.exp(-jnp.abs(z))))
    ci = lax.broadcasted_iota(jnp.int32, (c_len, c_len), 0).astype(F32)
    si = lax.broadcasted_iota(jnp.int32, (c_len, c_len), 1).astype(F32)
    if reverse:
        rel = si - ci
        wq = jnp.exp(lg * (c_len - ci))
        wk = jnp.exp(lg * ci)
    else:
        rel = ci - si
        wq = jnp.exp(lg * (ci + 1.0))
        wk = jnp.exp(lg * (c_len - 1.0 - ci))
    dmat = jnp.where(rel >= 0, jnp.exp(lg * jnp.maximum(rel, 0.0)), 0.0)
    return wq, wk, dmat, jnp.exp(lg * c_len)


def _ret_kernel(lf_ref, lb_ref, qf_ref, kf_ref, vf_ref, gf_ref, qb_ref, kb_ref, vb_ref, gb_ref, s0f_ref, s0b_ref,
                yf_ref, yb_ref, sf_ref, sb_ref, c_scr, *, n_chunks):
    h = pl.program_id(1)

    @pl.when(pl.program_id(2) == 0)
    def _():
        sf_ref[...] = s0f_ref[...]
        sb_ref[...] = s0b_ref[...]
        for j, arr in enumerate(_ret_consts(lf_ref[h], False) + _ret_consts(lb_ref[h], True)):
            c_scr[j] = arr

    c_len = RET_CHUNK
    nt = (((1,), (1,)), ((), ()))
    k_scale = HEAD_DIM ** -0.5
    dirs = []
    for d, refs in enumerate(((qf_ref, kf_ref, vf_ref, gf_ref, yf_ref), (qb_ref, kb_ref, vb_ref, gb_ref, yb_ref))):
        order = list(range(n_chunks)) if d == 0 else list(range(n_chunks - 1, -1, -1))
        dirs.append(refs + tuple(c_scr[4 * d + j] for j in range(4)) + (order,))

    units = []
    for step in range(n_chunks):
        for d in range(2):
            q_ref, k_ref, v_ref, _, _, _, wk, dmat, _, order = dirs[d]
            sl = slice(order[step] * c_len, (order[step] + 1) * c_len)
            qb = q_ref[sl, :].astype(BF16)
            kb = k_ref[sl, :]
            vb = v_ref[sl, :].astype(BF16)
            sc = lax.dot_general(qb, kb.astype(BF16), nt, preferred_element_type=F32) * (dmat * k_scale)
            kv = jnp.dot((kb.astype(F32) * (wk * k_scale)).T.astype(BF16), vb, preferred_element_type=F32)
            units.append((d, sl, qb, vb, sc, kv))
    states = [sf_ref[...], sb_ref[...]]
    cross = []
    for (d, sl, qb, vb, sc, kv) in units:
        wq, g_chunk = dirs[d][5], dirs[d][8]
        cross.append(wq * jnp.dot(qb, states[d].astype(BF16), preferred_element_type=F32))
        states[d] = g_chunk * states[d] + kv
    sf_ref[...] = states[0]
    sb_ref[...] = states[1]
    for (d, sl, qb, vb, sc, kv), o_cross in zip(units, cross):
        g_ref, y_ref = dirs[d][3], dirs[d][4]
        o = jnp.dot(sc.astype(BF16), vb, preferred_element_type=F32) + o_cross
        mu = jnp.mean(o, axis=-1, keepdims=True)
        dev = o - mu
        gn = dev * lax.rsqrt(jnp.mean(dev * dev, axis=-1, keepdims=True) + EPS)
        y_ref[sl, :] = _silu(g_ref[sl, :].astype(F32)) * gn


def _retention_scan(p, logit_f, logit_b, s0_f, s0_b):
    b, n, _ = p.shape
    hh = N_HEADS_RET
    tb = min(n, 16 * RET_CHUNK)
    nblk = n // tb

    def fwd(c0):
        return pl.BlockSpec((None, tb, LANE), lambda bb, h, i: (bb, i, c0 + h))

    def bwd(c0):
        return pl.BlockSpec((None, tb, LANE), lambda bb, h, i: (bb, nblk - 1 - i, c0 + h))

    state = pl.BlockSpec((None, None, HEAD_DIM, HEAD_DIM), lambda bb, h, i: (bb, h, 0, 0))
    smem = pl.BlockSpec(memory_space=pltpu.SMEM)
    y_shape = jax.ShapeDtypeStruct((b, n, hh * HEAD_DIM), F32)
    s_shape = jax.ShapeDtypeStruct((b, hh, HEAD_DIM, HEAD_DIM), F32)
    return pl.pallas_call(
        functools.partial(_ret_kernel, n_chunks=tb // RET_CHUNK),
        grid=(b, hh, nblk),
        in_specs=[smem, smem,
                  fwd(COL_RET_Q), fwd(COL_RET_K), fwd(COL_RET_V), fwd(COL_RET_GF),
                  bwd(COL_RET_Q), bwd(COL_RET_K), bwd(COL_RET_V), bwd(COL_RET_GB),
                  state, state],
        out_specs=[fwd(0), bwd(0), state, state],
        out_shape=[y_shape, y_shape, s_shape, s_shape],
        scratch_shapes=[pltpu.VMEM((8, RET_CHUNK, RET_CHUNK), F32)],
        compiler_params=_cparams(("parallel", "parallel", "arbitrary"), VMEM_MIB_SMALL),
        name="retention",
    )(logit_f.astype(F32), logit_b.astype(F32), p, p, p, p, p, p, p, p, s0_f, s0_b)


def _retention(p_c, p_l, logit_f, logit_b):
    b = p_c.shape[0]
    s0 = jnp.zeros((b, N_HEADS_RET, HEAD_DIM, HEAD_DIM), F32)
    yc_f, yc_b, s_cf, s_cb = _retention_scan(p_c, logit_f, logit_b, s0, s0)
    yl_f, yl_b, _, _ = _retention_scan(p_l, logit_f, logit_b, s_cf, s_cb)
    return (yc_f, yc_b), (yl_f, yl_b)


MERGE_ROWS = 512
MERGE_HALVES = 2


def _merge_kernel(x_ref, ya_ref, yb_ref, yrf_ref, yrb_ref, g_ref, gate_ref, w_ref, g2_ref, sh2_ref, sc2_ref, wr_ref,
                  o_ref, h_ref, aff_ref):
    wa = ya_ref.shape[-1]
    wb = yb_ref.shape[-1]
    hr = x_ref.shape[0] // MERGE_HALVES
    halves = [slice(j * hr, (j + 1) * hr) for j in range(MERGE_HALVES)]
    wr = wr_ref[...]
    w_hi = wr.astype(BF16)
    w_lo_f = wr - w_hi.astype(F32)
    w_hi_lo = (w_hi.astype(F32) + pltpu.roll(w_lo_f, N_EXPERTS, 1)).astype(BF16)
    normed = []
    for rows in halves:
        na = (_rms(ya_ref[rows, :]) * g_ref[:, :wa]).astype(BF16)
        nb = (_rms(yb_ref[rows, :]) * g_ref[:, wa:wa + wb]).astype(BF16)
        nr = (_rms(yrf_ref[rows, :] + yrb_ref[rows, :]) * g_ref[:, wa + wb:]).astype(BF16)
        normed.append((na, nb, nr))
    accs = []
    for na, nb, nr in normed:
        acc = jnp.dot(na, w_ref[:wa, :], preferred_element_type=F32)
        acc = acc + jnp.dot(nb, w_ref[wa:wa + wb, :], preferred_element_type=F32)
        accs.append(acc + jnp.dot(nr, w_ref[wa + wb:, :], preferred_element_type=F32))
    for rows, acc in zip(halves, accs):
        x_new = x_ref[rows, :] + gate_ref[...] * acc
        o_ref[rows, :] = x_new
        h = (_rms(x_new) * g2_ref[...]) * (1.0 + sc2_ref[...]) + sh2_ref[...]
        h_hi = h.astype(BF16)
        h_ref[rows, :] = h_hi
        h_lo = (h - h_hi.astype(F32)).astype(BF16)
        hh = jnp.dot(h_hi, w_hi_lo, preferred_element_type=F32)
        logits = hh + (pltpu.roll(hh, LANE - N_EXPERTS, 1) + jnp.dot(h_lo, w_hi, preferred_element_type=F32))
        lane = lax.broadcasted_iota(jnp.int32, logits.shape, 1)
        logits = jnp.where(lane < N_EXPERTS, logits, NEG_BIG)
        e = jnp.exp(logits - jnp.max(logits, axis=-1, keepdims=True))
        aff_ref[rows, :] = e / jnp.sum(e, axis=-1, keepdims=True)


def _merge_out(x, ya, yb, yr, g, gate, w_out, g2, shift2, scale2, w_router_pad):
    b, n, d = x.shape
    yrf, yrb = yr
    tm = min(n, MERGE_ROWS)
    dm = w_out.shape[0]

    def tok(width):
        return pl.BlockSpec((None, tm, width), lambda bb, i: (bb, i, 0))

    row = pl.BlockSpec((1, d), lambda bb, i: (0, 0))
    per_batch = pl.BlockSpec((None, 1, d), lambda bb, i: (bb, 0, 0))
    return pl.pallas_call(
        _merge_kernel,
        grid=(b, n // tm),
        in_specs=[tok(d), tok(ya.shape[-1]), tok(yb.shape[-1]), tok(yrf.shape[-1]), tok(yrb.shape[-1]),
                  pl.BlockSpec((1, dm), lambda bb, i: (0, 0)), per_batch,
                  pl.BlockSpec((dm, d), lambda bb, i: (0, 0)),
                  row, per_batch, per_batch,
                  pl.BlockSpec((d, LANE), lambda bb, i: (0, 0))],
        out_specs=[tok(d), tok(d), tok(LANE)],
        out_shape=[jax.ShapeDtypeStruct((b, n, d), F32), jax.ShapeDtypeStruct((b, n, d), BF16),
                   jax.ShapeDtypeStruct((b, n, LANE), F32)],
        compiler_params=_cparams(("parallel", "parallel"), VMEM_MIB_MAX),
        name="merge_out",
    )(x, ya, yb, yrf, yrb, g.reshape(1, dm), gate, w_out, g2.reshape(1, d), shift2, scale2, w_router_pad)


def _ffn_up_kernel(xs_ref, wg_ref, wu_ref, hm_ref, wg_scr, wu_scr):
    @pl.when(pl.program_id(1) == 0)
    def _():
        wg_scr[...] = wg_ref[...].astype(BF16)
        wu_scr[...] = wu_ref[...].astype(BF16)

    xs = xs_ref[...]
    a = jnp.dot(xs, wg_scr[...], preferred_element_type=F32)
    u = jnp.dot(xs, wu_scr[...], preferred_element_type=F32)
    hm_ref[...] = (_silu(a) * u).astype(BF16)


def _ffn_down_kernel(hm_ref, gate_ref, wd_ref, o_ref, wd_scr):
    @pl.when(pl.program_id(1) == 0)
    def _():
        wd_scr[...] = wd_ref[...].astype(BF16)

    o_ref[...] = (jnp.dot(hm_ref[...], wd_scr[...], preferred_element_type=F32) * gate_ref[...]).astype(o_ref.dtype)


def _expert_ffn(xs, gates, w_gate, w_up, w_down, layer):
    e, t, d = xs.shape
    f = w_gate.shape[-1]
    tm = min(t, 512)
    tm_down = min(t, 1024)
    hm = pl.pallas_call(
        _ffn_up_kernel,
        grid=(e, t // tm),
        in_specs=[pl.BlockSpec((None, tm, d), lambda ee, i: (ee, i, 0)),
                  pl.BlockSpec((None, None, d, f), lambda ee, i: (layer, ee, 0, 0)),
                  pl.BlockSpec((None, None, d, f), lambda ee, i: (layer, ee, 0, 0))],
        out_specs=pl.BlockSpec((None, tm, f), lambda ee, i: (ee, i, 0)),
        out_shape=jax.ShapeDtypeStruct((e, t, f), BF16),
        scratch_shapes=[pltpu.VMEM((d, f), BF16), pltpu.VMEM((d, f), BF16)],
        compiler_params=_cparams(("parallel", "arbitrary"), VMEM_MIB_XLARGE),
        name="expert_ffn_up",
    )(xs, w_gate, w_up)
    return pl.pallas_call(
        _ffn_down_kernel,
        grid=(e, t // tm_down),
        in_specs=[pl.BlockSpec((None, tm_down, f), lambda ee, i: (ee, i, 0)),
                  pl.BlockSpec((None, tm_down, 1), lambda ee, i: (ee, i, 0)),
                  pl.BlockSpec((None, None, f, d), lambda ee, i: (layer, ee, 0, 0))],
        out_specs=pl.BlockSpec((None, tm_down, d), lambda ee, i: (ee, i, 0)),
        out_shape=jax.ShapeDtypeStruct((e, t, d), BF16),
        scratch_shapes=[pltpu.VMEM((f, d), BF16)],
        compiler_params=_cparams(("parallel", "arbitrary"), VMEM_MIB_XLARGE),
        name="expert_ffn_down",
    )(hm, gates, w_down)


COMBINE_TOKENS = 256
COMBINE_ROWS = 512


def _combine_kernel(*refs, final_norm, aliased):
    if aliased:
        tile_ref, win_ref, flag_ref, x_ref, yp_ref, tok_ref, gate_ref, gn_ref, _, o_ref, acc_scr = refs
    else:
        tile_ref, win_ref, flag_ref, x_ref, yp_ref, tok_ref, gate_ref, gn_ref, o_ref, acc_scr = refs
    s = pl.program_id(0)
    flags = flag_ref[s]
    t = acc_scr.shape[0]

    @pl.when((flags & 1) != 0)
    def _():
        acc_scr[...] = jnp.zeros(acc_scr.shape, F32)

    @pl.when((flags & 2) != 0)
    def _():
        tcol = tile_ref[s] * t + lax.broadcasted_iota(jnp.int32, (t, 1), 0)
        onehot = jnp.where(tcol == tok_ref[...], 1.0, 0.0).astype(BF16)
        acc_scr[...] += jnp.dot(onehot, yp_ref[...], preferred_element_type=F32)

    @pl.when((flags & 4) != 0)
    def _():
        y = x_ref[...] + gate_ref[...] * acc_scr[...]
        if final_norm:
            y = _rms(y) * gn_ref[...]
        o_ref[...] = y


def _moe_combine(x, y_rows, tok_rows, gate, norm_g, final_norm, batch0=0, n_batches=None, out_prev=None):
    b, n, d = x.shape
    r = y_rows.shape[0]
    t = min(COMBINE_TOKENS, n)
    w = COMBINE_ROWS
    tiles_per_batch = n // t
    ntiles = (b if n_batches is None else n_batches) * tiles_per_batch
    tile0 = batch0 * tiles_per_batch
    nwin_total = r // w
    n_steps = nwin_total + 2 * ntiles
    i32 = jnp.int32
    tok_sorted, order = lax.sort_key_val(tok_rows.astype(i32), jnp.arange(r, dtype=i32))
    yp = jnp.take(y_rows, order, axis=0, mode="clip")
    edges = (tile0 + jnp.arange(ntiles + 1, dtype=i32)) * t
    bounds = jnp.sum((tok_sorted[None, :] < edges[:, None]).astype(i32), axis=1)
    w_lo = jnp.minimum(bounds[:-1] // w, nwin_total - 1)
    w_hi = jnp.maximum((bounds[1:] + w - 1) // w, w_lo + 1)
    nwin = w_hi - w_lo
    step0 = jnp.cumsum(nwin) - nwin
    total = jnp.sum(nwin)
    s = jnp.arange(n_steps, dtype=i32)
    tile_s = jnp.sum((s[:, None] >= step0[None, :]).astype(i32), axis=1) - 1
    sel = tile_s[:, None] == jnp.arange(ntiles, dtype=i32)[None, :]
    tile_s = tile_s + tile0

    def of_tile(v):
        return jnp.sum(jnp.where(sel, v[None, :], 0), axis=1)

    step0_s, w_lo_s, nwin_s = of_tile(step0), of_tile(w_lo), of_tile(nwin)
    win_s = jnp.minimum(w_lo_s + (s - step0_s), w_lo_s + nwin_s - 1)
    valid = s < total
    first = valid & (s == step0_s)
    last = valid & (s == step0_s + nwin_s - 1)
    flags = first.astype(i32) + 2 * valid.astype(i32) + 4 * last.astype(i32)
    in_specs = [pl.BlockSpec((t, d), lambda s, tl, wn, fl: (tl[s], 0)),
                pl.BlockSpec((w, d), lambda s, tl, wn, fl: (wn[s], 0)),
                pl.BlockSpec((None, 1, w), lambda s, tl, wn, fl: (wn[s], 0, 0)),
                pl.BlockSpec((None, 1, d), lambda s, tl, wn, fl: (tl[s] // tiles_per_batch, 0, 0)),
                pl.BlockSpec((1, d), lambda s, tl, wn, fl: (0, 0))]
    args = [tile_s, win_s, flags, x.reshape(b * n, d), yp, tok_sorted.reshape(nwin_total, 1, w), gate,
            norm_g.reshape(1, d)]
    aliases = {}
    if out_prev is not None:
        in_specs.append(pl.BlockSpec(memory_space=pl.ANY))
        args.append(out_prev.reshape(b * n, d))
        aliases = {len(args) - 1: 0}
    out = pl.pallas_call(
        functools.partial(_combine_kernel, final_norm=final_norm, aliased=out_prev is not None),
        grid_spec=pltpu.PrefetchScalarGridSpec(
            num_scalar_prefetch=3,
            grid=(n_steps,),
            in_specs=in_specs,
            out_specs=pl.BlockSpec((t, d), lambda s, tl, wn, fl: (tl[s], 0)),
            scratch_shapes=[pltpu.VMEM((t, d), F32)]),
        out_shape=jax.ShapeDtypeStruct((b * n, d), F32),
        compiler_params=_cparams(("arbitrary",), VMEM_MIB_MEDIUM),
        input_output_aliases=aliases,
        name="moe_combine",
    )(*args)
    return out.reshape(b, n, d)


def _moe_select(aff):
    b, n, _ = aff.shape
    cap = EC_CAPACITY * n // N_EXPERTS
    gate, idx = lax.top_k(jnp.swapaxes(aff[..., :N_EXPERTS], 1, 2), cap)
    flat = (idx + (jnp.arange(b, dtype=idx.dtype) * n)[:, None, None]).transpose(1, 0, 2).reshape(-1)
    return flat, gate.transpose(1, 0, 2).reshape(N_EXPERTS, b * cap, 1)


def _moe_gather(h, flat):
    b, n, d = h.shape
    return jnp.take(h.reshape(b * n, d), flat, axis=0, mode="clip").reshape(N_EXPERTS, -1, d)


def _moe_latent_split(x, h, aff, gate2, experts, norm_g, final_norm):
    b, n, d = x.shape
    size = b // 2 if b % 2 == 0 else b
    routed = []
    for b0 in range(0, b, size):
        flat, gates = _moe_select(aff[b0:b0 + size])
        flat = flat + b0 * n
        routed.append((b0, flat, gates, _moe_gather(h, flat)))
    out = None
    for b0, flat, gates, xs in routed:
        y = _expert_ffn(xs, gates, *experts)
        out = _moe_combine(x, y.reshape(-1, d), flat, gate2, norm_g, final_norm, b0, size, out)
    return out


def _moe_experts(x, xs, gates, flat, gate2, experts, norm_g, final_norm):
    y = _expert_ffn(xs, gates, *experts)
    return _moe_combine(x, y.reshape(-1, x.shape[-1]), flat, gate2, norm_g, final_norm)


_ROPE_SRC = np.concatenate([np.arange(16, 32), np.arange(0, 16), np.arange(48, 64), np.arange(32, 48)])
_ROPE_SIGN = np.concatenate([-np.ones(16), np.ones(16), -np.ones(16), np.ones(16)]).astype(np.float32)


def _rope_perm_cols(w):
    return w[:, _ROPE_SRC] * _ROPE_SIGN


def _pad_w_in(w_in):
    c_kr = MLA_Q_LORA + MLA_KV_LORA
    kr = w_in[:, c_kr:c_kr + MLA_ROPE_DIM]
    return jnp.concatenate([w_in[:, :c_kr + MLA_ROPE_DIM], _rope_perm_cols(kr), w_in[:, c_kr + MLA_ROPE_DIM:]],
                           axis=1).astype(BF16)


def _expand_w_uq(w_uq):
    w = w_uq.reshape(MLA_Q_LORA, N_HEADS_MLA, HEAD_DIM + MLA_ROPE_DIM)
    nope = w[:, :, :HEAD_DIM]
    rope = w[:, :, HEAD_DIM:]
    partner = rope[:, :, _ROPE_SRC] * _ROPE_SIGN
    return jnp.concatenate([nope, rope, partner], axis=-1).reshape(MLA_Q_LORA, N_HEADS_MLA * MLA_QK_PAD).astype(BF16)


def _rope_tables(n):
    t = jnp.arange(n)
    n_freq = MLA_ROPE_DIM // 4
    inv = ROPE_BASE ** (-jnp.arange(n_freq, dtype=F32) / n_freq)
    ang_r = (t // GRID_W).astype(F32)[:, None] * inv
    ang_c = (t % GRID_W).astype(F32)[:, None] * inv
    zeros = jnp.zeros((n, LANE - MLA_ROPE_DIM), F32)
    cos = jnp.concatenate([jnp.cos(ang_r), jnp.cos(ang_r), jnp.cos(ang_c), jnp.cos(ang_c), zeros], axis=1)
    sin = jnp.concatenate([jnp.sin(ang_r), jnp.sin(ang_r), jnp.sin(ang_c), jnp.sin(ang_c), zeros], axis=1)
    return cos, sin


def _identity_tables(n):
    ones = jnp.ones((n, MLA_ROPE_DIM), F32)
    zeros = jnp.zeros((n, LANE - MLA_ROPE_DIM), F32)
    return jnp.concatenate([ones, zeros], axis=1), jnp.zeros((n, LANE), F32)


def _layer(x_l, x_c, mod, lp, last, final_g):
    b, n, d = x_l.shape
    n_ctx = x_c.shape[1]

    def chunk(k, ctx):
        rows = jnp.broadcast_to(mod[b, k * d:(k + 1) * d], (b, d)) if ctx else mod[:b, k * d:(k + 1) * d]
        return rows.reshape(b, 1, d)

    w_in = _pad_w_in(lp['w_in'])
    p_l = _norm_proj(x_l, lp['norm1_g'], chunk(0, False), chunk(1, False), w_in)
    p_c = _norm_proj(x_c, lp['norm1_g'], chunk(0, True), chunk(1, True), w_in)

    cos_l, sin_l = _rope_tables(n)
    cos_c, sin_c = _identity_tables(n_ctx)
    w_uk = lp['mla_w_uk'].reshape(MLA_KV_LORA, -1).astype(BF16)
    w_uv_t = lp['mla_w_uv'].reshape(MLA_KV_LORA, -1).T.astype(BF16)
    w_q_t = _expand_w_uq(lp['mla_w_uq']).T
    k_l, v_l = _mla_kv_prep(p_l, lp['mla_kv_norm_g'], w_uk, w_uv_t, cos_l, sin_l)
    k_c, v_c = _mla_kv_prep(p_c, lp['mla_kv_norm_g'], w_uk, w_uv_t, cos_c, sin_c)
    q_l = _mla_q_prep(p_l, lp['mla_q_norm_g'], w_q_t, cos_l.T, sin_l.T)
    a_l = _attention_t(q_l, [(k_c, v_c), (k_l, v_l)], N_HEADS_MLA, MLA_QK_PAD, HEAD_DIM)

    b_l = _nat_latent(p_l, p_c, _nat_bias_table(lp['nat_rpb']))
    r_c, r_l = _retention(p_c, p_l, lp['ret_decay_f'], lp['ret_decay_b'])

    w_out = lp['w_out'].astype(BF16)
    w_router = jnp.zeros((d, LANE), F32).at[:, :N_EXPERTS].set(lp['w_router'])

    x_l, h_l, aff_l = _merge_out(x_l, a_l, b_l, r_l, lp['out_norm_g'], chunk(2, False), w_out,
                                 lp['norm2_g'], chunk(3, False), chunk(4, False), w_router)
    if last:
        return _moe_latent_split(x_l, h_l, aff_l, chunk(5, False), lp['experts'], final_g, True), None
    flat_l, gates_l = _moe_select(aff_l)

    flat_l, p_c, x_c = lax.optimization_barrier((flat_l, p_c, x_c))
    xs_l = _moe_gather(h_l, flat_l)
    q_c = _mla_q_prep(p_c, lp['mla_q_norm_g'], w_q_t, cos_c.T, sin_c.T)
    a_c = _attention_t(q_c, [(k_c, v_c)], N_HEADS_MLA, MLA_QK_PAD, HEAD_DIM)
    b_c = _attention(p_c, COL_NAT_Q, HEAD_DIM, [(p_c, COL_NAT_K, p_c, COL_NAT_V)], HEAD_DIM, N_HEADS_NAT,
                     HEAD_DIM ** -0.5)
    x_c, h_c, aff_c = _merge_out(x_c, a_c, b_c, r_c, lp['out_norm_g'], chunk(2, True), w_out,
                                 lp['norm2_g'], chunk(3, True), chunk(4, True), w_router)
    flat_c, gates_c = _moe_select(aff_c)
    x_c = _moe_experts(x_c, _moe_gather(h_c, flat_c), gates_c, flat_c, chunk(5, True), lp['experts'], final_g, False)
    xs_l, x_c = lax.optimization_barrier((xs_l, x_c))
    x_l = _moe_experts(x_l, xs_l, gates_l, flat_l, chunk(5, False), lp['experts'], final_g, False)
    return x_l, x_c


def kernel(x, c, ctx, c_ctx, w_mod, b_mod, norm1_g, w_in, mla_q_norm_g, mla_kv_norm_g, mla_w_uq, mla_w_uk,
           mla_w_uv, nat_rpb, ret_decay_f, ret_decay_b, out_norm_g, w_out, norm2_g, w_router, w_gate, w_up,
           w_down, final_norm_g):
    depth = w_mod.shape[0]
    cvecs = jnp.concatenate([c, c_ctx[None, :]], axis=0)
    x_l, x_c = x, ctx
    for i in range(depth):
        lp = {
            'norm1_g': norm1_g[i], 'w_in': w_in[i], 'mla_q_norm_g': mla_q_norm_g[i],
            'mla_kv_norm_g': mla_kv_norm_g[i], 'mla_w_uq': mla_w_uq[i], 'mla_w_uk': mla_w_uk[i],
            'mla_w_uv': mla_w_uv[i], 'nat_rpb': nat_rpb[i], 'ret_decay_f': ret_decay_f[i],
            'ret_decay_b': ret_decay_b[i], 'out_norm_g': out_norm_g[i], 'w_out': w_out[i],
            'norm2_g': norm2_g[i], 'w_router': w_router[i], 'experts': (w_gate, w_up, w_down, i),
        }
        mod = _modulation(cvecs, w_mod, b_mod[i], i)
        x_l, x_c = _layer(x_l, x_c, mod, lp, i == depth - 1, final_norm_g)
    return x_l
```

```python
import functools

import numpy as np
import jax
import jax.numpy as jnp
from jax import lax
from jax.experimental import pallas as pl
from jax.experimental.pallas import tpu as pltpu

F32 = jnp.float32
BF16 = jnp.bfloat16

GRID_W = 64
HEAD_DIM = 128
N_HEADS_MLA = 8
N_HEADS_NAT = 4
N_HEADS_RET = 4
MLA_ROPE_DIM = 64
MLA_Q_LORA = 512
MLA_KV_LORA = 256
MLA_SCALE = (HEAD_DIM + MLA_ROPE_DIM) ** -0.5
LOG2E = 1.4426950408889634
MLA_Q_SCALE = MLA_SCALE * LOG2E
MLA_QK_PAD = 256
NAT_ROWS = 8
NAT_COLS = 16
RET_CHUNK = 128
N_EXPERTS = 16
EC_CAPACITY = 2
ROPE_BASE = 10000.0
EPS = 1e-6
NEG_BIG = -1e30
LANE = 128

COL_CQ = 0
COL_CKV = 4
COL_KR = 6
COL_NAT_Q = 7
COL_NAT_K = 11
COL_NAT_V = 15
COL_RET_Q = 19
COL_RET_K = 23
COL_RET_V = 27
COL_RET_GF = 31
COL_RET_GB = 35
IN_COLS_PAD = 39 * LANE


VMEM_MIB_SMALL = 32
VMEM_MIB_MEDIUM = 40
VMEM_MIB_LARGE = 48
VMEM_MIB_XLARGE = 56
VMEM_MIB_MAX = 58


def _cparams(sem, vmem_mib):
    return pltpu.CompilerParams(dimension_semantics=sem, vmem_limit_bytes=vmem_mib * 1024 * 1024)


def _silu(a):
    return a * (1.0 / (1.0 + jnp.exp(-a)))


def _rms(x):
    return x * lax.rsqrt(jnp.mean(x * x, axis=-1, keepdims=True) + EPS)


def _mod_kernel(c_ref, w_ref, b_ref, o_ref):
    a = _silu(c_ref[...])
    a_hi = a.astype(BF16)
    a_lo = (a - a_hi.astype(F32)).astype(BF16)
    w = w_ref[...]
    w_hi = w.astype(BF16)
    w_lo = (w - w_hi.astype(F32)).astype(BF16)
    r = jnp.dot(jnp.concatenate([a_hi, a_lo], axis=0), w_hi, preferred_element_type=F32)
    o_ref[...] = r[:8] + (r[8:] + jnp.dot(a_hi, w_lo, preferred_element_type=F32)) + b_ref[...]


def _modulation(cvecs, w_mod, b_mod, layer):
    n_rows, d = cvecs.shape
    n_out = w_mod.shape[-1]
    tn = 1024
    c8 = jnp.zeros((8, d), F32).at[:n_rows].set(cvecs)
    return pl.pallas_call(
        _mod_kernel,
        grid=(n_out // tn,),
        in_specs=[pl.BlockSpec((8, d), lambda j: (0, 0)),
                  pl.BlockSpec((None, d, tn), lambda j: (layer, 0, j)),
                  pl.BlockSpec((1, tn), lambda j: (0, j))],
        out_specs=pl.BlockSpec((8, tn), lambda j: (0, j)),
        out_shape=jax.ShapeDtypeStruct((8, n_out), F32),
        compiler_params=_cparams(("parallel",), VMEM_MIB_MEDIUM),
        name="modulation",
    )(c8, w_mod, b_mod.reshape(1, n_out))


NORM_PROJ_PARTS = 2


def _norm_proj_kernel(x_ref, g_ref, sh_ref, sc_ref, w_ref, o_ref, h_scr):
    j = pl.program_id(2)
    pr = x_ref.shape[0] // NORM_PROJ_PARTS
    parts = [slice(k * pr, (k + 1) * pr) for k in range(NORM_PROJ_PARTS)]

    @pl.when(j == 0)
    def _():
        hs = []
        for rows in parts:
            y = _rms(x_ref[rows, :]) * g_ref[...]
            h = (y * (1.0 + sc_ref[...]) + sh_ref[...]).astype(BF16)
            h_scr[rows, :] = h
            hs.append(h)
        for rows, h in zip(parts, hs):
            o_ref[rows, :] = jnp.dot(h, w_ref[...], preferred_element_type=F32).astype(o_ref.dtype)

    @pl.when(j != 0)
    def _():
        o_ref[...] = jnp.dot(h_scr[...], w_ref[...], preferred_element_type=F32).astype(o_ref.dtype)


def _norm_proj(x, g, shift, scale, w):
    b, n, d = x.shape
    c = w.shape[1]
    tm = min(n, 1024)
    tn = c // 3 if (c % (3 * LANE) == 0) else c
    return pl.pallas_call(
        _norm_proj_kernel,
        grid=(b, n // tm, c // tn),
        in_specs=[pl.BlockSpec((None, tm, d), lambda bb, i, j: (bb, i, 0)),
                  pl.BlockSpec((1, d), lambda bb, i, j: (0, 0)),
                  pl.BlockSpec((None, 1, d), lambda bb, i, j: (bb, 0, 0)),
                  pl.BlockSpec((None, 1, d), lambda bb, i, j: (bb, 0, 0)),
                  pl.BlockSpec((d, tn), lambda bb, i, j: (0, j))],
        out_specs=pl.BlockSpec((None, tm, tn), lambda bb, i, j: (bb, i, j)),
        out_shape=jax.ShapeDtypeStruct((b, n, c), BF16),
        scratch_shapes=[pltpu.VMEM((tm, d), BF16)],
        compiler_params=_cparams(("parallel", "parallel", "arbitrary"), VMEM_MIB_XLARGE),
        name="norm_proj",
    )(x, g.reshape(1, d), shift, scale, w)


def _rope_rotate(r, cos, sin):
    return r * cos + pltpu.roll(r, 64, 1) * sin


def _mla_q_kernel(cq_ref, g_ref, wt_ref, cos_ref, sin_ref, o_ref, *, n_heads):
    cqn_t = (_rms(cq_ref[...].astype(F32)) * g_ref[...]).T.astype(BF16)
    qe = jnp.dot(wt_ref[...], cqn_t, preferred_element_type=F32)
    cos = cos_ref[...]
    sin = sin_ref[...]
    for h in range(n_heads):
        c0 = h * MLA_QK_PAD
        o_ref[c0:c0 + LANE, :] = (qe[c0:c0 + LANE, :] * MLA_Q_SCALE).astype(BF16)
        r = qe[c0 + LANE:c0 + 2 * LANE, :]
        swapped = jnp.concatenate([r[MLA_ROPE_DIM:, :], r[:MLA_ROPE_DIM, :]], axis=0)
        o_ref[c0 + LANE:c0 + 2 * LANE, :] = ((r * cos + swapped * sin) * MLA_Q_SCALE).astype(BF16)


def _mla_q_prep(p, g, w_q_t, cos_t, sin_t):
    b, n, _ = p.shape
    tm = min(n, 512)
    cw = N_HEADS_MLA * MLA_QK_PAD
    return pl.pallas_call(
        functools.partial(_mla_q_kernel, n_heads=N_HEADS_MLA),
        grid=(b, n // tm),
        in_specs=[pl.BlockSpec((None, tm, MLA_Q_LORA), lambda bb, i: (bb, i, COL_CQ * LANE // MLA_Q_LORA)),
                  pl.BlockSpec((1, MLA_Q_LORA), lambda bb, i: (0, 0)),
                  pl.BlockSpec((cw, MLA_Q_LORA), lambda bb, i: (0, 0)),
                  pl.BlockSpec((LANE, tm), lambda bb, i: (0, i)),
                  pl.BlockSpec((LANE, tm), lambda bb, i: (0, i))],
        out_specs=pl.BlockSpec((None, cw, tm), lambda bb, i: (bb, 0, i)),
        out_shape=jax.ShapeDtypeStruct((b, cw, n), BF16),
        compiler_params=_cparams(("parallel", "parallel"), VMEM_MIB_MEDIUM),
        name="mla_q_prep",
    )(p, g.reshape(1, MLA_Q_LORA), w_q_t, cos_t, sin_t)


def _mla_kv_kernel(ckv_ref, kr_ref, g_ref, wuk_ref, wuvt_ref, cos_ref, sin_ref, k_ref, v_ref, *, n_heads):
    ckvn = _rms(ckv_ref[...].astype(F32)) * g_ref[...]
    kn = jnp.dot(ckvn.astype(BF16), wuk_ref[...], preferred_element_type=F32)
    v_ref[...] = jnp.dot(wuvt_ref[...], ckvn.T.astype(BF16), preferred_element_type=F32).astype(BF16)
    rot = _rope_rotate(kr_ref[...].astype(F32), cos_ref[...], sin_ref[...]).astype(BF16)
    for h in range(n_heads):
        c0 = h * MLA_QK_PAD
        k_ref[:, c0:c0 + LANE] = kn[:, h * LANE:(h + 1) * LANE].astype(BF16)
        k_ref[:, c0 + LANE:c0 + 2 * LANE] = rot


def _mla_kv_prep(p, g, w_uk, w_uv_t, cos, sin):
    b, n, _ = p.shape
    tm = min(n, 512)
    kw = N_HEADS_MLA * MLA_QK_PAD
    vw = N_HEADS_MLA * HEAD_DIM
    return pl.pallas_call(
        functools.partial(_mla_kv_kernel, n_heads=N_HEADS_MLA),
        grid=(b, n // tm),
        in_specs=[pl.BlockSpec((None, tm, MLA_KV_LORA), lambda bb, i: (bb, i, COL_CKV * LANE // MLA_KV_LORA)),
                  pl.BlockSpec((None, tm, LANE), lambda bb, i: (bb, i, COL_KR)),
                  pl.BlockSpec((1, MLA_KV_LORA), lambda bb, i: (0, 0)),
                  pl.BlockSpec((MLA_KV_LORA, vw), lambda bb, i: (0, 0)),
                  pl.BlockSpec((vw, MLA_KV_LORA), lambda bb, i: (0, 0)),
                  pl.BlockSpec((tm, LANE), lambda bb, i: (i, 0)),
                  pl.BlockSpec((tm, LANE), lambda bb, i: (i, 0))],
        out_specs=[pl.BlockSpec((None, tm, kw), lambda bb, i: (bb, i, 0)),
                   pl.BlockSpec((None, vw, tm), lambda bb, i: (bb, 0, i))],
        out_shape=[jax.ShapeDtypeStruct((b, n, kw), BF16), jax.ShapeDtypeStruct((b, vw, n), BF16)],
        compiler_params=_cparams(("parallel", "parallel"), VMEM_MIB_MEDIUM),
        name="mla_kv_prep",
    )(p, p, g.reshape(1, MLA_KV_LORA), w_uk, w_uv_t, cos, sin)


def _attn_kernel(*refs, n_src, scale, tk_max):
    q_ref = refs[0]
    o_ref = refs[1 + 2 * n_src]
    q = q_ref[...]
    if scale != 1.0:
        q = q.astype(F32) * scale
    q = q.astype(BF16)
    tq = q.shape[0]
    dv = o_ref.shape[-1]

    def step(k, v, carry):
        m, l, acc = carry
        s = lax.dot_general(q, k, (((1,), (1,)), ((), ())), preferred_element_type=F32)
        m_new = jnp.maximum(m, jnp.max(s, axis=-1, keepdims=True))
        p = jnp.exp(s - m_new)
        alpha = jnp.exp(m - m_new)
        l = alpha * l + jnp.sum(p, axis=-1, keepdims=True)
        acc = alpha * acc + jnp.dot(p.astype(BF16), v, preferred_element_type=F32)
        return m_new, l, acc

    carry = (jnp.full((tq, 1), NEG_BIG, F32), jnp.zeros((tq, 1), F32), jnp.zeros((tq, dv), F32))
    for s_i in range(n_src):
        k_ref = refs[1 + 2 * s_i]
        v_ref = refs[2 + 2 * s_i]
        nk = k_ref.shape[0]
        tk = min(nk, tk_max)
        if nk == tk:
            carry = step(k_ref[...].astype(BF16), v_ref[...].astype(BF16), carry)
        else:
            def body(c, cr, k_ref=k_ref, v_ref=v_ref, tk=tk):
                k0 = pl.multiple_of(c * tk, tk)
                return step(k_ref[pl.ds(k0, tk), :].astype(BF16), v_ref[pl.ds(k0, tk), :].astype(BF16), cr)
            carry = lax.fori_loop(0, nk // tk, body, carry)
    _, l, acc = carry
    o_ref[...] = acc / l


def _attention(q_arr, q_blk0, dq, sources, dv, n_heads, scale, tq=256, tk_max=512):
    b, nq, _ = q_arr.shape
    tq = min(nq, tq)
    in_specs = [pl.BlockSpec((None, tq, dq), lambda bb, h, i: (bb, i, q_blk0 + h))]
    args = [q_arr]
    for k_arr, k_blk0, v_arr, v_blk0 in sources:
        nk = k_arr.shape[1]
        in_specs.append(pl.BlockSpec((None, nk, dq), lambda bb, h, i, o=k_blk0: (bb, 0, o + h)))
        in_specs.append(pl.BlockSpec((None, nk, dv), lambda bb, h, i, o=v_blk0: (bb, 0, o + h)))
        args += [k_arr, v_arr]
    return pl.pallas_call(
        functools.partial(_attn_kernel, n_src=len(sources), scale=scale, tk_max=tk_max),
        grid=(b, n_heads, nq // tq),
        in_specs=in_specs,
        out_specs=pl.BlockSpec((None, tq, dv), lambda bb, h, i: (bb, i, h)),
        out_shape=jax.ShapeDtypeStruct((b, nq, n_heads * dv), F32),
        compiler_params=_cparams(("parallel", "parallel", "arbitrary"), VMEM_MIB_LARGE),
        name="attention",
    )(*args)


ATTN_TQ = 2048
ATTN_TK = 256
ATTN_GROUP = 16


def _attn_t_kernel(*refs, n_src):
    q_ref = refs[0]
    o_ref = refs[1 + 2 * n_src]
    q = q_ref[...]
    tq = q.shape[1]
    dv = o_ref.shape[-1]

    def scores(k):
        return jnp.dot(k, q, preferred_element_type=F32)

    def update(s, vt, carry):
        m, l, acc = carry
        m_new = jnp.maximum(m, jnp.max(s, axis=0, keepdims=True))
        p = jnp.exp2(s - m_new)
        alpha = jnp.exp2(m - m_new)
        l = alpha * l + jnp.sum(p, axis=0, keepdims=True)
        acc = alpha * acc + jnp.dot(vt, p.astype(BF16), preferred_element_type=F32)
        return m_new, l, acc

    carry = (jnp.full((1, tq), NEG_BIG, F32), jnp.zeros((1, tq), F32), jnp.zeros((dv, tq), F32))
    for s_i in range(n_src):
        k_ref = refs[1 + 2 * s_i]
        vt_ref = refs[2 + 2 * s_i]
        nk = k_ref.shape[0]
        tk = min(nk, ATTN_TK)
        n_chunks = nk // tk
        if n_chunks == 1:
            carry = update(scores(k_ref[...]), vt_ref[...], carry)
            continue
        group = max(g for g in (ATTN_GROUP, 2, 1) if n_chunks % g == 0)

        def body(c, cr, k_ref=k_ref, vt_ref=vt_ref, tk=tk, group=group):
            starts = [pl.multiple_of((c * group + g) * tk, tk) for g in range(group)]
            ss = [scores(k_ref[pl.ds(k0, tk), :]) for k0 in starts]
            for s, k0 in zip(ss, starts):
                cr = update(s, vt_ref[:, pl.ds(k0, tk)], cr)
            return cr
        carry = lax.fori_loop(0, n_chunks // group, body, carry)
    _, l, acc = carry
    o_ref[...] = (acc / l).T


def _attention_t(q_arr, sources, n_heads, dq, dv):
    b, _, nq = q_arr.shape
    tq = min(nq, ATTN_TQ)
    in_specs = [pl.BlockSpec((None, dq, tq), lambda bb, h, i: (bb, h, i))]
    args = [q_arr]
    for k_arr, vt_arr in sources:
        nk = k_arr.shape[1]
        in_specs.append(pl.BlockSpec((None, nk, dq), lambda bb, h, i: (bb, 0, h)))
        in_specs.append(pl.BlockSpec((None, dv, nk), lambda bb, h, i: (bb, h, 0)))
        args += [k_arr, vt_arr]
    return pl.pallas_call(
        functools.partial(_attn_t_kernel, n_src=len(sources)),
        grid=(b, n_heads, nq // tq),
        in_specs=in_specs,
        out_specs=pl.BlockSpec((None, tq, dv), lambda bb, h, i: (bb, i, h)),
        out_shape=jax.ShapeDtypeStruct((b, nq, n_heads * dv), F32),
        compiler_params=_cparams(("parallel", "parallel", "arbitrary"), VMEM_MIB_XLARGE),
        name="attention_t",
    )(*args)


NAT_GROUP = 8
NAT_SLAB = NAT_GROUP + NAT_ROWS
NAT_HALVES = 2


def _nat_kernel(q_ref, k_ref, v_ref, kc_ref, vc_ref, bias_ref, o_ref, *, n_rows):
    i = pl.program_id(2)
    scale = HEAD_DIM ** -0.5
    nt = (((1,), (1,)), ((), ()))
    u0 = jnp.clip(i * NAT_GROUP - NAT_ROWS // 2, 0, n_rows - NAT_SLAB)
    t0 = pl.multiple_of(u0 * GRID_W, GRID_W)
    ks = k_ref[pl.ds(t0, NAT_SLAB * GRID_W), :].astype(BF16)
    vs = v_ref[pl.ds(t0, NAT_SLAB * GRID_W), :].astype(BF16)
    kc = kc_ref[...].astype(BF16)
    vc = vc_ref[...].astype(BF16)
    hq = q_ref.shape[0] // NAT_HALVES
    scores = []
    for j in range(NAT_HALVES):
        rows = slice(j * hq, (j + 1) * hq)
        q = (q_ref[rows, :].astype(F32) * scale).astype(BF16)
        scores.append((lax.dot_general(q, ks, nt, preferred_element_type=F32) + bias_ref[rows, :],
                       lax.dot_general(q, kc, nt, preferred_element_type=F32)))
    for j in range(NAT_HALVES):
        s_w, s_c = scores[j]
        m = jnp.maximum(jnp.max(s_w, axis=-1, keepdims=True), jnp.max(s_c, axis=-1, keepdims=True))
        p_w = jnp.exp(s_w - m)
        p_c = jnp.exp(s_c - m)
        l = jnp.sum(p_w, axis=-1, keepdims=True) + jnp.sum(p_c, axis=-1, keepdims=True)
        o = (jnp.dot(p_w.astype(BF16), vs, preferred_element_type=F32)
             + jnp.dot(p_c.astype(BF16), vc, preferred_element_type=F32))
        o_ref[j * hq:(j + 1) * hq, :] = o / l


def _nat_bias_table(rpb):
    h = rpb.shape[0]
    col = np.arange(GRID_W)
    c0 = np.clip(col - NAT_COLS // 2, 0, GRID_W - NAT_COLS)
    kc = np.arange(GRID_W)
    col_ok = (kc[None, :] >= c0[:, None]) & (kc[None, :] < c0[:, None] + NAT_COLS)
    dc = kc[None, :] - col[:, None] + NAT_COLS - 1
    sel_c = ((dc[None] == np.arange(2 * NAT_COLS - 1)[:, None, None]) & col_ok[None]).astype(np.float32)
    sel_c = sel_c.reshape(2 * NAT_COLS - 1, GRID_W * GRID_W)
    per_dr = jnp.einsum('hdc,cb->hdb', rpb.astype(F32), sel_c, precision=lax.Precision.HIGHEST)
    per_dr = jnp.where(col_ok.reshape(-1)[None, None], per_dr, NEG_BIG).reshape(h, 2 * NAT_ROWS - 1, GRID_W, GRID_W)
    return pl.pallas_call(
        _nat_table_kernel,
        grid=(h, 3),
        in_specs=[pl.BlockSpec((None, 2 * NAT_ROWS - 1, GRID_W, GRID_W), lambda hh, v: (hh, 0, 0, 0))],
        out_specs=pl.BlockSpec((None, None, NAT_GROUP * GRID_W, NAT_SLAB * GRID_W), lambda hh, v: (hh, v, 0, 0)),
        out_shape=jax.ShapeDtypeStruct((h, 3, NAT_GROUP * GRID_W, NAT_SLAB * GRID_W), F32),
        compiler_params=_cparams(("parallel", "arbitrary"), VMEM_MIB_SMALL),
        name="nat_bias_table",
    )(per_dr)


def _nat_table_kernel(per_dr_ref, o_ref):
    masked = jnp.full((GRID_W, GRID_W), NEG_BIG, F32)
    for variant, delta in enumerate((0, NAT_ROWS // 2, NAT_ROWS)):
        @pl.when(pl.program_id(1) == variant)
        def _(delta=delta):
            for g in range(NAT_GROUP):
                w0 = min(max(g + delta - NAT_ROWS // 2, 0), NAT_SLAB - NAT_ROWS)
                tiles = [per_dr_ref[u - (g + delta) + NAT_ROWS - 1] if w0 <= u < w0 + NAT_ROWS else masked
                         for u in range(NAT_SLAB)]
                o_ref[g * GRID_W:(g + 1) * GRID_W, :] = jnp.concatenate(tiles, axis=1)


def _nat_latent(p_l, p_c, bias_tbl):
    b, n, _ = p_l.shape
    n_ctx = p_c.shape[1]
    n_rows = n // GRID_W
    assert n_rows >= NAT_SLAB and n_rows % NAT_GROUP == 0
    n_groups = n_rows // NAT_GROUP
    tq = NAT_GROUP * GRID_W
    tk = NAT_SLAB * GRID_W

    def variant(i):
        return jnp.where(i == 0, 0, jnp.where(i == n_groups - 1, 2, 1))

    return pl.pallas_call(
        functools.partial(_nat_kernel, n_rows=n_rows),
        grid=(b, N_HEADS_NAT, n_groups),
        in_specs=[pl.BlockSpec((None, tq, LANE), lambda bb, h, i: (bb, i, COL_NAT_Q + h)),
                  pl.BlockSpec((None, n, LANE), lambda bb, h, i: (bb, 0, COL_NAT_K + h)),
                  pl.BlockSpec((None, n, LANE), lambda bb, h, i: (bb, 0, COL_NAT_V + h)),
                  pl.BlockSpec((None, n_ctx, LANE), lambda bb, h, i: (bb, 0, COL_NAT_K + h)),
                  pl.BlockSpec((None, n_ctx, LANE), lambda bb, h, i: (bb, 0, COL_NAT_V + h)),
                  pl.BlockSpec((None, None, tq, tk), lambda bb, h, i: (h, variant(i), 0, 0))],
        out_specs=pl.BlockSpec((None, tq, LANE), lambda bb, h, i: (bb, i, h)),
        out_shape=jax.ShapeDtypeStruct((b, n, N_HEADS_NAT * HEAD_DIM), F32),
        compiler_params=_cparams(("parallel", "parallel", "arbitrary"), VMEM_MIB_LARGE),
        name="nat_latent",
    )(p_l, p_l, p_l, p_c, p_c, bias_tbl)


def _ret_consts(logit, reverse):
    c_len = RET_CHUNK
    z = jnp.full((c_len, c_len), -logit, F32)
    lg = -(jnp.maximum(z, 0.0) + jnp.log1p(jnp.exp(-jnp.abs(z))))
    ci = lax.broadcasted_iota(jnp.int32, (c_len, c_len), 0).astype(F32)
    si = lax.broadcasted_iota(jnp.int32, (c_len, c_len), 1).astype(F32)
    if reverse:
        rel = si - ci
        wq = jnp.exp(lg * (c_len - ci))
        wk = jnp.exp(lg * ci)
    else:
        rel = ci - si
        wq = jnp.exp(lg * (ci + 1.0))
        wk = jnp.exp(lg * (c_len - 1.0 - ci))
    dmat = jnp.where(rel >= 0, jnp.exp(lg * jnp.maximum(rel, 0.0)), 0.0)
    return wq, wk, dmat, jnp.exp(lg * c_len)


def _ret_kernel(lf_ref, lb_ref, qf_ref, kf_ref, vf_ref, gf_ref, qb_ref, kb_ref, vb_ref, gb_ref, s0f_ref, s0b_ref,
                yf_ref, yb_ref, sf_ref, sb_ref, c_scr, *, n_chunks):
    h = pl.program_id(1)

    @pl.when(pl.program_id(2) == 0)
    def _():
        sf_ref[...] = s0f_ref[...]
        sb_ref[...] = s0b_ref[...]
        for j, arr in enumerate(_ret_consts(lf_ref[h], False) + _ret_consts(lb_ref[h], True)):
            c_scr[j] = arr

    c_len = RET_CHUNK
    nt = (((1,), (1,)), ((), ()))
    k_scale = HEAD_DIM ** -0.5
    dirs = []
    for d, refs in enumerate(((qf_ref, kf_ref, vf_ref, gf_ref, yf_ref), (qb_ref, kb_ref, vb_ref, gb_ref, yb_ref))):
        order = list(range(n_chunks)) if d == 0 else list(range(n_chunks - 1, -1, -1))
        dirs.append(refs + tuple(c_scr[4 * d + j] for j in range(4)) + (order,))

    units = []
    for step in range(n_chunks):
        for d in range(2):
            q_ref, k_ref, v_ref, _, _, _, wk, dmat, _, order = dirs[d]
            sl = slice(order[step] * c_len, (order[step] + 1) * c_len)
            qb = q_ref[sl, :].astype(BF16)
            kb = k_ref[sl, :]
            vb = v_ref[sl, :].astype(BF16)
            sc = lax.dot_general(qb, kb.astype(BF16), nt, preferred_element_type=F32) * (dmat * k_scale)
            kv = jnp.dot((kb.astype(F32) * (wk * k_scale)).T.astype(BF16), vb, preferred_element_type=F32)
            units.append((d, sl, qb, vb, sc, kv))
    states = [sf_ref[...], sb_ref[...]]
    cross = []
    for (d, sl, qb, vb, sc, kv) in units:
        wq, g_chunk = dirs[d][5], dirs[d][8]
        cross.append(wq * jnp.dot(qb, states[d].astype(BF16), preferred_element_type=F32))
        states[d] = g_chunk * states[d] + kv
    sf_ref[...] = states[0]
    sb_ref[...] = states[1]
    for (d, sl, qb, vb, sc, kv), o_cross in zip(units, cross):
        g_ref, y_ref = dirs[d][3], dirs[d][4]
        o = jnp.dot(sc.astype(BF16), vb, preferred_element_type=F32) + o_cross
        mu = jnp.mean(o, axis=-1, keepdims=True)
        dev = o - mu
        gn = dev * lax.rsqrt(jnp.mean(dev * dev, axis=-1, keepdims=True) + EPS)
        y_ref[sl, :] = _silu(g_ref[sl, :].astype(F32)) * gn


def _retention_scan(p, logit_f, logit_b, s0_f, s0_b):
    b, n, _ = p.shape
    hh = N_HEADS_RET
    tb = min(n, 16 * RET_CHUNK)
    nblk = n // tb

    def fwd(c0):
        return pl.BlockSpec((None, tb, LANE), lambda bb, h, i: (bb, i, c0 + h))

    def bwd(c0):
        return pl.BlockSpec((None, tb, LANE), lambda bb, h, i: (bb, nblk - 1 - i, c0 + h))

    state = pl.BlockSpec((None, None, HEAD_DIM, HEAD_DIM), lambda bb, h, i: (bb, h, 0, 0))
    smem = pl.BlockSpec(memory_space=pltpu.SMEM)
    y_shape = jax.ShapeDtypeStruct((b, n, hh * HEAD_DIM), F32)
    s_shape = jax.ShapeDtypeStruct((b, hh, HEAD_DIM, HEAD_DIM), F32)
    return pl.pallas_call(
        functools.partial(_ret_kernel, n_chunks=tb // RET_CHUNK),
        grid=(b, hh, nblk),
        in_specs=[smem, smem,
                  fwd(COL_RET_Q), fwd(COL_RET_K), fwd(COL_RET_V), fwd(COL_RET_GF),
                  bwd(COL_RET_Q), bwd(COL_RET_K), bwd(COL_RET_V), bwd(COL_RET_GB),
                  state, state],
        out_specs=[fwd(0), bwd(0), state, state],
        out_shape=[y_shape, y_shape, s_shape, s_shape],
        scratch_shapes=[pltpu.VMEM((8, RET_CHUNK, RET_CHUNK), F32)],
        compiler_params=_cparams(("parallel", "parallel", "arbitrary"), VMEM_MIB_SMALL),
        name="retention",
    )(logit_f.astype(F32), logit_b.astype(F32), p, p, p, p, p, p, p, p, s0_f, s0_b)


def _retention(p_c, p_l, logit_f, logit_b):
    b = p_c.shape[0]
    s0 = jnp.zeros((b, N_HEADS_RET, HEAD_DIM, HEAD_DIM), F32)
    yc_f, yc_b, s_cf, s_cb = _retention_scan(p_c, logit_f, logit_b, s0, s0)
    yl_f, yl_b, _, _ = _retention_scan(p_l, logit_f, logit_b, s_cf, s_cb)
    return (yc_f, yc_b), (yl_f, yl_b)


MERGE_ROWS = 512
MERGE_HALVES = 2


def _merge_kernel(x_ref, ya_ref, yb_ref, yrf_ref, yrb_ref, g_ref, gate_ref, w_ref, g2_ref, sh2_ref, sc2_ref, wr_ref,
                  o_ref, h_ref, aff_ref):
    wa = ya_ref.shape[-1]
    wb = yb_ref.shape[-1]
    hr = x_ref.shape[0] // MERGE_HALVES
    halves = [slice(j * hr, (j + 1) * hr) for j in range(MERGE_HALVES)]
    wr = wr_ref[...]
    w_hi = wr.astype(BF16)
    w_lo_f = wr - w_hi.astype(F32)
    w_hi_lo = (w_hi.astype(F32) + pltpu.roll(w_lo_f, N_EXPERTS, 1)).astype(BF16)
    normed = []
    for rows in halves:
        na = (_rms(ya_ref[rows, :]) * g_ref[:, :wa]).astype(BF16)
        nb = (_rms(yb_ref[rows, :]) * g_ref[:, wa:wa + wb]).astype(BF16)
        nr = (_rms(yrf_ref[rows, :] + yrb_ref[rows, :]) * g_ref[:, wa + wb:]).astype(BF16)
        normed.append((na, nb, nr))
    accs = []
    for na, nb, nr in normed:
        acc = jnp.dot(na, w_ref[:wa, :], preferred_element_type=F32)
        acc = acc + jnp.dot(nb, w_ref[wa:wa + wb, :], preferred_element_type=F32)
        accs.append(acc + jnp.dot(nr, w_ref[wa + wb:, :], preferred_element_type=F32))
    for rows, acc in zip(halves, accs):
        x_new = x_ref[rows, :] + gate_ref[...] * acc
        o_ref[rows, :] = x_new
        h = (_rms(x_new) * g2_ref[...]) * (1.0 + sc2_ref[...]) + sh2_ref[...]
        h_hi = h.astype(BF16)
        h_ref[rows, :] = h_hi
        h_lo = (h - h_hi.astype(F32)).astype(BF16)
        hh = jnp.dot(h_hi, w_hi_lo, preferred_element_type=F32)
        logits = hh + (pltpu.roll(hh, LANE - N_EXPERTS, 1) + jnp.dot(h_lo, w_hi, preferred_element_type=F32))
        lane = lax.broadcasted_iota(jnp.int32, logits.shape, 1)
        logits = jnp.where(lane < N_EXPERTS, logits, NEG_BIG)
        e = jnp.exp(logits - jnp.max(logits, axis=-1, keepdims=True))
        aff_ref[rows, :] = e / jnp.sum(e, axis=-1, keepdims=True)


def _merge_out(x, ya, yb, yr, g, gate, w_out, g2, shift2, scale2, w_router_pad):
    b, n, d = x.shape
    yrf, yrb = yr
    tm = min(n, MERGE_ROWS)
    dm = w_out.shape[0]

    def tok(width):
        return pl.BlockSpec((None, tm, width), lambda bb, i: (bb, i, 0))

    row = pl.BlockSpec((1, d), lambda bb, i: (0, 0))
    per_batch = pl.BlockSpec((None, 1, d), lambda bb, i: (bb, 0, 0))
    return pl.pallas_call(
        _merge_kernel,
        grid=(b, n // tm),
        in_specs=[tok(d), tok(ya.shape[-1]), tok(yb.shape[-1]), tok(yrf.shape[-1]), tok(yrb.shape[-1]),
                  pl.BlockSpec((1, dm), lambda bb, i: (0, 0)), per_batch,
                  pl.BlockSpec((dm, d), lambda bb, i: (0, 0)),
                  row, per_batch, per_batch,
                  pl.BlockSpec((d, LANE), lambda bb, i: (0, 0))],
        out_specs=[tok(d), tok(d), tok(LANE)],
        out_shape=[jax.ShapeDtypeStruct((b, n, d), F32), jax.ShapeDtypeStruct((b, n, d), BF16),
                   jax.ShapeDtypeStruct((b, n, LANE), F32)],
        compiler_params=_cparams(("parallel", "parallel"), VMEM_MIB_MAX),
        name="merge_out",
    )(x, ya, yb, yrf, yrb, g.reshape(1, dm), gate, w_out, g2.reshape(1, d), shift2, scale2, w_router_pad)


def _ffn_up_kernel(xs_ref, wg_ref, wu_ref, hm_ref, wg_scr, wu_scr):
    @pl.when(pl.program_id(1) == 0)
    def _():
        wg_scr[...] = wg_ref[...].astype(BF16)
        wu_scr[...] = wu_ref[...].astype(BF16)

    xs = xs_ref[...]
    a = jnp.dot(xs, wg_scr[...], preferred_element_type=F32)
    u = jnp.dot(xs, wu_scr[...], preferred_element_type=F32)
    hm_ref[...] = (_silu(a) * u).astype(BF16)


def _ffn_down_kernel(hm_ref, gate_ref, wd_ref, o_ref, wd_scr):
    @pl.when(pl.program_id(1) == 0)
    def _():
        wd_scr[...] = wd_ref[...].astype(BF16)

    o_ref[...] = (jnp.dot(hm_ref[...], wd_scr[...], preferred_element_type=F32) * gate_ref[...]).astype(o_ref.dtype)


def _expert_ffn(xs, gates, w_gate, w_up, w_down, layer):
    e, t, d = xs.shape
    f = w_gate.shape[-1]
    tm = min(t, 512)
    tm_down = min(t, 1024)
    hm = pl.pallas_call(
        _ffn_up_kernel,
        grid=(e, t // tm),
        in_specs=[pl.BlockSpec((None, tm, d), lambda ee, i: (ee, i, 0)),
                  pl.BlockSpec((None, None, d, f), lambda ee, i: (layer, ee, 0, 0)),
                  pl.BlockSpec((None, None, d, f), lambda ee, i: (layer, ee, 0, 0))],
        out_specs=pl.BlockSpec((None, tm, f), lambda ee, i: (ee, i, 0)),
        out_shape=jax.ShapeDtypeStruct((e, t, f), BF16),
        scratch_shapes=[pltpu.VMEM((d, f), BF16), pltpu.VMEM((d, f), BF16)],
        compiler_params=_cparams(("parallel", "arbitrary"), VMEM_MIB_XLARGE),
        name="expert_ffn_up",
    )(xs, w_gate, w_up)
    return pl.pallas_call(
        _ffn_down_kernel,
        grid=(e, t // tm_down),
        in_specs=[pl.BlockSpec((None, tm_down, f), lambda ee, i: (ee, i, 0)),
                  pl.BlockSpec((None, tm_down, 1), lambda ee, i: (ee, i, 0)),
                  pl.BlockSpec((None, None, f, d), lambda ee, i: (layer, ee, 0, 0))],
        out_specs=pl.BlockSpec((None, tm_down, d), lambda ee, i: (ee, i, 0)),
        out_shape=jax.ShapeDtypeStruct((e, t, d), BF16),
        scratch_shapes=[pltpu.VMEM((f, d), BF16)],
        compiler_params=_cparams(("parallel", "arbitrary"), VMEM_MIB_XLARGE),
        name="expert_ffn_down",
    )(hm, gates, w_down)


COMBINE_TOKENS = 256
COMBINE_ROWS = 512


def _combine_kernel(*refs, final_norm, aliased):
    if aliased:
        tile_ref, win_ref, flag_ref, x_ref, yp_ref, tok_ref, gate_ref, gn_ref, _, o_ref, acc_scr = refs
    else:
        tile_ref, win_ref, flag_ref, x_ref, yp_ref, tok_ref, gate_ref, gn_ref, o_ref, acc_scr = refs
    s = pl.program_id(0)
    flags = flag_ref[s]
    t = acc_scr.shape[0]

    @pl.when((flags & 1) != 0)
    def _():
        acc_scr[...] = jnp.zeros(acc_scr.shape, F32)

    @pl.when((flags & 2) != 0)
    def _():
        tcol = tile_ref[s] * t + lax.broadcasted_iota(jnp.int32, (t, 1), 0)
        onehot = jnp.where(tcol == tok_ref[...], 1.0, 0.0).astype(BF16)
        acc_scr[...] += jnp.dot(onehot, yp_ref[...], preferred_element_type=F32)

    @pl.when((flags & 4) != 0)
    def _():
        y = x_ref[...] + gate_ref[...] * acc_scr[...]
        if final_norm:
            y = _rms(y) * gn_ref[...]
        o_ref[...] = y


def _moe_combine(x, y_rows, tok_rows, gate, norm_g, final_norm, batch0=0, n_batches=None, out_prev=None):
    b, n, d = x.shape
    r = y_rows.shape[0]
    t = min(COMBINE_TOKENS, n)
    w = COMBINE_ROWS
    tiles_per_batch = n // t
    ntiles = (b if n_batches is None else n_batches) * tiles_per_batch
    tile0 = batch0 * tiles_per_batch
    nwin_total = r // w
    n_steps = nwin_total + 2 * ntiles
    i32 = jnp.int32
    tok_sorted, order = lax.sort_key_val(tok_rows.astype(i32), jnp.arange(r, dtype=i32))
    yp = jnp.take(y_rows, order, axis=0, mode="clip")
    edges = (tile0 + jnp.arange(ntiles + 1, dtype=i32)) * t
    bounds = jnp.sum((tok_sorted[None, :] < edges[:, None]).astype(i32), axis=1)
    w_lo = jnp.minimum(bounds[:-1] // w, nwin_total - 1)
    w_hi = jnp.maximum((bounds[1:] + w - 1) // w, w_lo + 1)
    nwin = w_hi - w_lo
    step0 = jnp.cumsum(nwin) - nwin
    total = jnp.sum(nwin)
    s = jnp.arange(n_steps, dtype=i32)
    tile_s = jnp.sum((s[:, None] >= step0[None, :]).astype(i32), axis=1) - 1
    sel = tile_s[:, None] == jnp.arange(ntiles, dtype=i32)[None, :]
    tile_s = tile_s + tile0

    def of_tile(v):
        return jnp.sum(jnp.where(sel, v[None, :], 0), axis=1)

    step0_s, w_lo_s, nwin_s = of_tile(step0), of_tile(w_lo), of_tile(nwin)
    win_s = jnp.minimum(w_lo_s + (s - step0_s), w_lo_s + nwin_s - 1)
    valid = s < total
    first = valid & (s == step0_s)
    last = valid & (s == step0_s + nwin_s - 1)
    flags = first.astype(i32) + 2 * valid.astype(i32) + 4 * last.astype(i32)
    in_specs = [pl.BlockSpec((t, d), lambda s, tl, wn, fl: (tl[s], 0)),
                pl.BlockSpec((w, d), lambda s, tl, wn, fl: (wn[s], 0)),
                pl.BlockSpec((None, 1, w), lambda s, tl, wn, fl: (wn[s], 0, 0)),
                pl.BlockSpec((None, 1, d), lambda s, tl, wn, fl: (tl[s] // tiles_per_batch, 0, 0)),
                pl.BlockSpec((1, d), lambda s, tl, wn, fl: (0, 0))]
    args = [tile_s, win_s, flags, x.reshape(b * n, d), yp, tok_sorted.reshape(nwin_total, 1, w), gate,
            norm_g.reshape(1, d)]
    aliases = {}
    if out_prev is not None:
        in_specs.append(pl.BlockSpec(memory_space=pl.ANY))
        args.append(out_prev.reshape(b * n, d))
        aliases = {len(args) - 1: 0}
    out = pl.pallas_call(
        functools.partial(_combine_kernel, final_norm=final_norm, aliased=out_prev is not None),
        grid_spec=pltpu.PrefetchScalarGridSpec(
            num_scalar_prefetch=3,
            grid=(n_steps,),
            in_specs=in_specs,
            out_specs=pl.BlockSpec((t, d), lambda s, tl, wn, fl: (tl[s], 0)),
            scratch_shapes=[pltpu.VMEM((t, d), F32)]),
        out_shape=jax.ShapeDtypeStruct((b * n, d), F32),
        compiler_params=_cparams(("arbitrary",), VMEM_MIB_MEDIUM),
        input_output_aliases=aliases,
        name="moe_combine",
    )(*args)
    return out.reshape(b, n, d)


def _moe_select(aff):
    b, n, _ = aff.shape
    cap = EC_CAPACITY * n // N_EXPERTS
    gate, idx = lax.top_k(jnp.swapaxes(aff[..., :N_EXPERTS], 1, 2), cap)
    flat = (idx + (jnp.arange(b, dtype=idx.dtype) * n)[:, None, None]).transpose(1, 0, 2).reshape(-1)
    return flat, gate.transpose(1, 0, 2).reshape(N_EXPERTS, b * cap, 1)


def _moe_gather(h, flat):
    b, n, d = h.shape
    return jnp.take(h.reshape(b * n, d), flat, axis=0, mode="clip").reshape(N_EXPERTS, -1, d)


def _moe_latent_split(x, h, aff, gate2, experts, norm_g, final_norm):
    b, n, d = x.shape
    size = b // 2 if b % 2 == 0 else b
    routed = []
    for b0 in range(0, b, size):
        flat, gates = _moe_select(aff[b0:b0 + size])
        flat = flat + b0 * n
        routed.append((b0, flat, gates, _moe_gather(h, flat)))
    out = None
    for b0, flat, gates, xs in routed:
        y = _expert_ffn(xs, gates, *experts)
        out = _moe_combine(x, y.reshape(-1, d), flat, gate2, norm_g, final_norm, b0, size, out)
    return out


def _moe_experts(x, xs, gates, flat, gate2, experts, norm_g, final_norm):
    y = _expert_ffn(xs, gates, *experts)
    return _moe_combine(x, y.reshape(-1, x.shape[-1]), flat, gate2, norm_g, final_norm)


_ROPE_SRC = np.concatenate([np.arange(16, 32), np.arange(0, 16), np.arange(48, 64), np.arange(32, 48)])
_ROPE_SIGN = np.concatenate([-np.ones(16), np.ones(16), -np.ones(16), np.ones(16)]).astype(np.float32)


def _rope_perm_cols(w):
    return w[:, _ROPE_SRC] * _ROPE_SIGN


def _pad_w_in(w_in):
    c_kr = MLA_Q_LORA + MLA_KV_LORA
    kr = w_in[:, c_kr:c_kr + MLA_ROPE_DIM]
    return jnp.concatenate([w_in[:, :c_kr + MLA_ROPE_DIM], _rope_perm_cols(kr), w_in[:, c_kr + MLA_ROPE_DIM:]],
                           axis=1).astype(BF16)


def _expand_w_uq(w_uq):
    w = w_uq.reshape(MLA_Q_LORA, N_HEADS_MLA, HEAD_DIM + MLA_ROPE_DIM)
    nope = w[:, :, :HEAD_DIM]
    rope = w[:, :, HEAD_DIM:]
    partner = rope[:, :, _ROPE_SRC] * _ROPE_SIGN
    return jnp.concatenate([nope, rope, partner], axis=-1).reshape(MLA_Q_LORA, N_HEADS_MLA * MLA_QK_PAD).astype(BF16)


def _rope_tables(n):
    t = jnp.arange(n)
    n_freq = MLA_ROPE_DIM // 4
    inv = ROPE_BASE ** (-jnp.arange(n_freq, dtype=F32) / n_freq)
    ang_r = (t // GRID_W).astype(F32)[:, None] * inv
    ang_c = (t % GRID_W).astype(F32)[:, None] * inv
    zeros = jnp.zeros((n, LANE - MLA_ROPE_DIM), F32)
    cos = jnp.concatenate([jnp.cos(ang_r), jnp.cos(ang_r), jnp.cos(ang_c), jnp.cos(ang_c), zeros], axis=1)
    sin = jnp.concatenate([jnp.sin(ang_r), jnp.sin(ang_r), jnp.sin(ang_c), jnp.sin(ang_c), zeros], axis=1)
    return cos, sin


def _identity_tables(n):
    ones = jnp.ones((n, MLA_ROPE_DIM), F32)
    zeros = jnp.zeros((n, LANE - MLA_ROPE_DIM), F32)
    return jnp.concatenate([ones, zeros], axis=1), jnp.zeros((n, LANE), F32)


def _layer(x_l, x_c, mod, lp, last, final_g):
    b, n, d = x_l.shape
    n_ctx = x_c.shape[1]

    def chunk(k, ctx):
        rows = jnp.broadcast_to(mod[b, k * d:(k + 1) * d], (b, d)) if ctx else mod[:b, k * d:(k + 1) * d]
        return rows.reshape(b, 1, d)

    w_in = _pad_w_in(lp['w_in'])
    p_l = _norm_proj(x_l, lp['norm1_g'], chunk(0, False), chunk(1, False), w_in)
    p_c = _norm_proj(x_c, lp['norm1_g'], chunk(0, True), chunk(1, True), w_in)

    cos_l, sin_l = _rope_tables(n)
    cos_c, sin_c = _identity_tables(n_ctx)
    w_uk = lp['mla_w_uk'].reshape(MLA_KV_LORA, -1).astype(BF16)
    w_uv_t = lp['mla_w_uv'].reshape(MLA_KV_LORA, -1).T.astype(BF16)
    w_q_t = _expand_w_uq(lp['mla_w_uq']).T
    k_l, v_l = _mla_kv_prep(p_l, lp['mla_kv_norm_g'], w_uk, w_uv_t, cos_l, sin_l)
    k_c, v_c = _mla_kv_prep(p_c, lp['mla_kv_norm_g'], w_uk, w_uv_t, cos_c, sin_c)
    q_l = _mla_q_prep(p_l, lp['mla_q_norm_g'], w_q_t, cos_l.T, sin_l.T)
    a_l = _attention_t(q_l, [(k_c, v_c), (k_l, v_l)], N_HEADS_MLA, MLA_QK_PAD, HEAD_DIM)

    b_l = _nat_latent(p_l, p_c, _nat_bias_table(lp['nat_rpb']))
    r_c, r_l = _retention(p_c, p_l, lp['ret_decay_f'], lp['ret_decay_b'])

    w_out = lp['w_out'].astype(BF16)
    w_router = jnp.zeros((d, LANE), F32).at[:, :N_EXPERTS].set(lp['w_router'])

    x_l, h_l, aff_l = _merge_out(x_l, a_l, b_l, r_l, lp['out_norm_g'], chunk(2, False), w_out,
                                 lp['norm2_g'], chunk(3, False), chunk(4, False), w_router)
    if last:
        return _moe_latent_split(x_l, h_l, aff_l, chunk(5, False), lp['experts'], final_g, True), None
    flat_l, gates_l = _moe_select(aff_l)

    flat_l, p_c, x_c = lax.optimization_barrier((flat_l, p_c, x_c))
    xs_l = _moe_gather(h_l, flat_l)
    q_c = _mla_q_prep(p_c, lp['mla_q_norm_g'], w_q_t, cos_c.T, sin_c.T)
    a_c = _attention_t(q_c, [(k_c, v_c)], N_HEADS_MLA, MLA_QK_PAD, HEAD_DIM)
    b_c = _attention(p_c, COL_NAT_Q, HEAD_DIM, [(p_c, COL_NAT_K, p_c, COL_NAT_V)], HEAD_DIM, N_HEADS_NAT,
                     HEAD_DIM ** -0.5)
    x_c, h_c, aff_c = _merge_out(x_c, a_c, b_c, r_c, lp['out_norm_g'], chunk(2, True), w_out,
                                 lp['norm2_g'], chunk(3, True), chunk(4, True), w_router)
    flat_c, gates_c = _moe_select(aff_c)
    x_c = _moe_experts(x_c, _moe_gather(h_c, flat_c), gates_c, flat_c, chunk(5, True), lp['experts'], final_g, False)
    xs_l, x_c = lax.optimization_barrier((xs_l, x_c))
    x_l = _moe_experts(x_l, xs_l, gates_l, flat_l, chunk(5, False), lp['experts'], final_g, False)
    return x_l, x_c


def kernel(x, c, ctx, c_ctx, w_mod, b_mod, norm1_g, w_in, mla_q_norm_g, mla_kv_norm_g, mla_w_uq, mla_w_uk,
           mla_w_uv, nat_rpb, ret_decay_f, ret_decay_b, out_norm_g, w_out, norm2_g, w_router, w_gate, w_up,
           w_down, final_norm_g):
    depth = w_mod.shape[0]
    cvecs = jnp.concatenate([c, c_ctx[None, :]], axis=0)
    x_l, x_c = x, ctx
    for i in range(depth):
        lp = {
            'norm1_g': norm1_g[i], 'w_in': w_in[i], 'mla_q_norm_g': mla_q_norm_g[i],
            'mla_kv_norm_g': mla_kv_norm_g[i], 'mla_w_uq': mla_w_uq[i], 'mla_w_uk': mla_w_uk[i],
            'mla_w_uv': mla_w_uv[i], 'nat_rpb': nat_rpb[i], 'ret_decay_f': ret_decay_f[i],
            'ret_decay_b': ret_decay_b[i], 'out_norm_g': out_norm_g[i], 'w_out': w_out[i],
            'norm2_g': norm2_g[i], 'w_router': w_router[i], 'experts': (w_gate, w_up, w_down, i),
        }
        mod = _modulation(cvecs, w_mod, b_mod[i], i)
        x_l, x_c = _layer(x_l, x_c, mod, lp, i == depth - 1, final_norm_g)
    return x_l
```

```python
import functools

import numpy as np
import jax
import jax.numpy as jnp
from jax import lax
from jax.experimental import pallas as pl
from jax.experimental.pallas import tpu as pltpu

F32 = jnp.float32
BF16 = jnp.bfloat16

GRID_W = 64
HEAD_DIM = 128
N_HEADS_MLA = 8
N_HEADS_NAT = 4
N_HEADS_RET = 4
MLA_ROPE_DIM = 64
MLA_Q_LORA = 512
MLA_KV_LORA = 256
MLA_SCALE = (HEAD_DIM + MLA_ROPE_DIM) ** -0.5
LOG2E = 1.4426950408889634
MLA_Q_SCALE = MLA_SCALE * LOG2E
MLA_QK_PAD = 256
NAT_ROWS = 8
NAT_COLS = 16
RET_CHUNK = 128
N_EXPERTS = 16
EC_CAPACITY = 2
ROPE_BASE = 10000.0
EPS = 1e-6
NEG_BIG = -1e30
LANE = 128

COL_CQ = 0
COL_CKV = 4
COL_KR = 6
COL_NAT_Q = 7
COL_NAT_K = 11
COL_NAT_V = 15
COL_RET_Q = 19
COL_RET_K = 23
COL_RET_V = 27
COL_RET_GF = 31
COL_RET_GB = 35
IN_COLS_PAD = 39 * LANE


VMEM_MIB_SMALL = 32
VMEM_MIB_MEDIUM = 40
VMEM_MIB_LARGE = 48
VMEM_MIB_XLARGE = 56
VMEM_MIB_MAX = 58


def _cparams(sem, vmem_mib):
    return pltpu.CompilerParams(dimension_semantics=sem, vmem_limit_bytes=vmem_mib * 1024 * 1024)


def _silu(a):
    return a * (1.0 / (1.0 + jnp.exp(-a)))


def _rms(x):
    return x * lax.rsqrt(jnp.mean(x * x, axis=-1, keepdims=True) + EPS)


def _mod_kernel(c_ref, w_ref, b_ref, o_ref):
    a = _silu(c_ref[...])
    a_hi = a.astype(BF16)
    a_lo = (a - a_hi.astype(F32)).astype(BF16)
    w = w_ref[...]
    w_hi = w.astype(BF16)
    w_lo = (w - w_hi.astype(F32)).astype(BF16)
    r = jnp.dot(jnp.concatenate([a_hi, a_lo], axis=0), w_hi, preferred_element_type=F32)
    o_ref[...] = r[:8] + (r[8:] + jnp.dot(a_hi, w_lo, preferred_element_type=F32)) + b_ref[...]


def _modulation(cvecs, w_mod, b_mod, layer):
    n_rows, d = cvecs.shape
    n_out = w_mod.shape[-1]
    tn = 1024
    c8 = jnp.zeros((8, d), F32).at[:n_rows].set(cvecs)
    return pl.pallas_call(
        _mod_kernel,
        grid=(n_out // tn,),
        in_specs=[pl.BlockSpec((8, d), lambda j: (0, 0)),
                  pl.BlockSpec((None, d, tn), lambda j: (layer, 0, j)),
                  pl.BlockSpec((1, tn), lambda j: (0, j))],
        out_specs=pl.BlockSpec((8, tn), lambda j: (0, j)),
        out_shape=jax.ShapeDtypeStruct((8, n_out), F32),
        compiler_params=_cparams(("parallel",), VMEM_MIB_MEDIUM),
        name="modulation",
    )(c8, w_mod, b_mod.reshape(1, n_out))


NORM_PROJ_PARTS = 2


def _norm_proj_kernel(x_ref, g_ref, sh_ref, sc_ref, w_ref, o_ref, h_scr):
    j = pl.program_id(2)
    pr = x_ref.shape[0] // NORM_PROJ_PARTS
    parts = [slice(k * pr, (k + 1) * pr) for k in range(NORM_PROJ_PARTS)]

    @pl.when(j == 0)
    def _():
        hs = []
        for rows in parts:
            y = _rms(x_ref[rows, :]) * g_ref[...]
            h = (y * (1.0 + sc_ref[...]) + sh_ref[...]).astype(BF16)
            h_scr[rows, :] = h
            hs.append(h)
        for rows, h in zip(parts, hs):
            o_ref[rows, :] = jnp.dot(h, w_ref[...], preferred_element_type=F32).astype(o_ref.dtype)

    @pl.when(j != 0)
    def _():
        o_ref[...] = jnp.dot(h_scr[...], w_ref[...], preferred_element_type=F32).astype(o_ref.dtype)


def _norm_proj(x, g, shift, scale, w):
    b, n, d = x.shape
    c = w.shape[1]
    tm = min(n, 1024)
    tn = c // 3 if (c % (3 * LANE) == 0) else c
    return pl.pallas_call(
        _norm_proj_kernel,
        grid=(b, n // tm, c // tn),
        in_specs=[pl.BlockSpec((None, tm, d), lambda bb, i, j: (bb, i, 0)),
                  pl.BlockSpec((1, d), lambda bb, i, j: (0, 0)),
                  pl.BlockSpec((None, 1, d), lambda bb, i, j: (bb, 0, 0)),
                  pl.BlockSpec((None, 1, d), lambda bb, i, j: (bb, 0, 0)),
                  pl.BlockSpec((d, tn), lambda bb, i, j: (0, j))],
        out_specs=pl.BlockSpec((None, tm, tn), lambda bb, i, j: (bb, i, j)),
        out_shape=jax.ShapeDtypeStruct((b, n, c), BF16),
        scratch_shapes=[pltpu.VMEM((tm, d), BF16)],
        compiler_params=_cparams(("parallel", "parallel", "arbitrary"), VMEM_MIB_XLARGE),
        name="norm_proj",
    )(x, g.reshape(1, d), shift, scale, w)


def _rope_rotate(r, cos, sin):
    return r * cos + pltpu.roll(r, 64, 1) * sin


def _mla_q_kernel(cq_ref, g_ref, wt_ref, cos_ref, sin_ref, o_ref, *, n_heads):
    cqn_t = (_rms(cq_ref[...].astype(F32)) * g_ref[...]).T.astype(BF16)
    qe = jnp.dot(wt_ref[...], cqn_t, preferred_element_type=F32)
    cos = cos_ref[...]
    sin = sin_ref[...]
    for h in range(n_heads):
        c0 = h * MLA_QK_PAD
        o_ref[c0:c0 + LANE, :] = (qe[c0:c0 + LANE, :] * MLA_Q_SCALE).astype(BF16)
        r = qe[c0 + LANE:c0 + 2 * LANE, :]
        swapped = jnp.concatenate([r[MLA_ROPE_DIM:, :], r[:MLA_ROPE_DIM, :]], axis=0)
        o_ref[c0 + LANE:c0 + 2 * LANE, :] = ((r * cos + swapped * sin) * MLA_Q_SCALE).astype(BF16)


def _mla_q_prep(p, g, w_q_t, cos_t, sin_t):
    b, n, _ = p.shape
    tm = min(n, 512)
    cw = N_HEADS_MLA * MLA_QK_PAD
    return pl.pallas_call(
        functools.partial(_mla_q_kernel, n_heads=N_HEADS_MLA),
        grid=(b, n // tm),
        in_specs=[pl.BlockSpec((None, tm, MLA_Q_LORA), lambda bb, i: (bb, i, COL_CQ * LANE // MLA_Q_LORA)),
                  pl.BlockSpec((1, MLA_Q_LORA), lambda bb, i: (0, 0)),
                  pl.BlockSpec((cw, MLA_Q_LORA), lambda bb, i: (0, 0)),
                  pl.BlockSpec((LANE, tm), lambda bb, i: (0, i)),
                  pl.BlockSpec((LANE, tm), lambda bb, i: (0, i))],
        out_specs=pl.BlockSpec((None, cw, tm), lambda bb, i: (bb, 0, i)),
        out_shape=jax.ShapeDtypeStruct((b, cw, n), BF16),
        compiler_params=_cparams(("parallel", "parallel"), VMEM_MIB_MEDIUM),
        name="mla_q_prep",
    )(p, g.reshape(1, MLA_Q_LORA), w_q_t, cos_t, sin_t)


def _mla_kv_kernel(ckv_ref, kr_ref, g_ref, wuk_ref, wuvt_ref, cos_ref, sin_ref, k_ref, v_ref, *, n_heads):
    ckvn = _rms(ckv_ref[...].astype(F32)) * g_ref[...]
    kn = jnp.dot(ckvn.astype(BF16), wuk_ref[...], preferred_element_type=F32)
    v_ref[...] = jnp.dot(wuvt_ref[...], ckvn.T.astype(BF16), preferred_element_type=F32).astype(BF16)
    rot = _rope_rotate(kr_ref[...].astype(F32), cos_ref[...], sin_ref[...]).astype(BF16)
    for h in range(n_heads):
        c0 = h * MLA_QK_PAD
        k_ref[:, c0:c0 + LANE] = kn[:, h * LANE:(h + 1) * LANE].astype(BF16)
        k_ref[:, c0 + LANE:c0 + 2 * LANE] = rot


def _mla_kv_prep(p, g, w_uk, w_uv_t, cos, sin):
    b, n, _ = p.shape
    tm = min(n, 512)
    kw = N_HEADS_MLA * MLA_QK_PAD
    vw = N_HEADS_MLA * HEAD_DIM
    return pl.pallas_call(
        functools.partial(_mla_kv_kernel, n_heads=N_HEADS_MLA),
        grid=(b, n // tm),
        in_specs=[pl.BlockSpec((None, tm, MLA_KV_LORA), lambda bb, i: (bb, i, COL_CKV * LANE // MLA_KV_LORA)),
                  pl.BlockSpec((None, tm, LANE), lambda bb, i: (bb, i, COL_KR)),
                  pl.BlockSpec((1, MLA_KV_LORA), lambda bb, i: (0, 0)),
                  pl.BlockSpec((MLA_KV_LORA, vw), lambda bb, i: (0, 0)),
                  pl.BlockSpec((vw, MLA_KV_LORA), lambda bb, i: (0, 0)),
                  pl.BlockSpec((tm, LANE), lambda bb, i: (i, 0)),
                  pl.BlockSpec((tm, LANE), lambda bb, i: (i, 0))],
        out_specs=[pl.BlockSpec((None, tm, kw), lambda bb, i: (bb, i, 0)),
                   pl.BlockSpec((None, vw, tm), lambda bb, i: (bb, 0, i))],
        out_shape=[jax.ShapeDtypeStruct((b, n, kw), BF16), jax.ShapeDtypeStruct((b, vw, n), BF16)],
        compiler_params=_cparams(("parallel", "parallel"), VMEM_MIB_MEDIUM),
        name="mla_kv_prep",
    )(p, p, g.reshape(1, MLA_KV_LORA), w_uk, w_uv_t, cos, sin)


def _attn_kernel(*refs, n_src, scale, tk_max):
    q_ref = refs[0]
    o_ref = refs[1 + 2 * n_src]
    q = q_ref[...]
    if scale != 1.0:
        q = q.astype(F32) * scale
    q = q.astype(BF16)
    tq = q.shape[0]
    dv = o_ref.shape[-1]

    def step(k, v, carry):
        m, l, acc = carry
        s = lax.dot_general(q, k, (((1,), (1,)), ((), ())), preferred_element_type=F32)
        m_new = jnp.maximum(m, jnp.max(s, axis=-1, keepdims=True))
        p = jnp.exp(s - m_new)
        alpha = jnp.exp(m - m_new)
        l = alpha * l + jnp.sum(p, axis=-1, keepdims=True)
        acc = alpha * acc + jnp.dot(p.astype(BF16), v, preferred_element_type=F32)
        return m_new, l, acc

    carry = (jnp.full((tq, 1), NEG_BIG, F32), jnp.zeros((tq, 1), F32), jnp.zeros((tq, dv), F32))
    for s_i in range(n_src):
        k_ref = refs[1 + 2 * s_i]
        v_ref = refs[2 + 2 * s_i]
        nk = k_ref.shape[0]
        tk = min(nk, tk_max)
        if nk == tk:
            carry = step(k_ref[...].astype(BF16), v_ref[...].astype(BF16), carry)
        else:
            def body(c, cr, k_ref=k_ref, v_ref=v_ref, tk=tk):
                k0 = pl.multiple_of(c * tk, tk)
                return step(k_ref[pl.ds(k0, tk), :].astype(BF16), v_ref[pl.ds(k0, tk), :].astype(BF16), cr)
            carry = lax.fori_loop(0, nk // tk, body, carry)
    _, l, acc = carry
    o_ref[...] = acc / l


def _attention(q_arr, q_blk0, dq, sources, dv, n_heads, scale, tq=256, tk_max=512):
    b, nq, _ = q_arr.shape
    tq = min(nq, tq)
    in_specs = [pl.BlockSpec((None, tq, dq), lambda bb, h, i: (bb, i, q_blk0 + h))]
    args = [q_arr]
    for k_arr, k_blk0, v_arr, v_blk0 in sources:
        nk = k_arr.shape[1]
        in_specs.append(pl.BlockSpec((None, nk, dq), lambda bb, h, i, o=k_blk0: (bb, 0, o + h)))
        in_specs.append(pl.BlockSpec((None, nk, dv), lambda bb, h, i, o=v_blk0: (bb, 0, o + h)))
        args += [k_arr, v_arr]
    return pl.pallas_call(
        functools.partial(_attn_kernel, n_src=len(sources), scale=scale, tk_max=tk_max),
        grid=(b, n_heads, nq // tq),
        in_specs=in_specs,
        out_specs=pl.BlockSpec((None, tq, dv), lambda bb, h, i: (bb, i, h)),
        out_shape=jax.ShapeDtypeStruct((b, nq, n_heads * dv), F32),
        compiler_params=_cparams(("parallel", "parallel", "arbitrary"), VMEM_MIB_LARGE),
        name="attention",
    )(*args)


ATTN_TQ = 1024
ATTN_TK = 256
ATTN_GROUP = 32


def _attn_t_kernel(*refs, n_src):
    q_ref = refs[0]
    o_ref = refs[1 + 2 * n_src]
    q = q_ref[...]
    tq = q.shape[1]
    dv = o_ref.shape[-1]

    def scores(k):
        return jnp.dot(k, q, preferred_element_type=F32)

    def update(s, vt, carry):
        m, l, acc = carry
        m_new = jnp.maximum(m, jnp.max(s, axis=0, keepdims=True))
        p = jnp.exp2(s - m_new)
        alpha = jnp.exp2(m - m_new)
        l = alpha * l + jnp.sum(p, axis=0, keepdims=True)
        acc = alpha * acc + jnp.dot(vt, p.astype(BF16), preferred_element_type=F32)
        return m_new, l, acc

    carry = (jnp.full((1, tq), NEG_BIG, F32), jnp.zeros((1, tq), F32), jnp.zeros((dv, tq), F32))
    for s_i in range(n_src):
        k_ref = refs[1 + 2 * s_i]
        vt_ref = refs[2 + 2 * s_i]
        nk = k_ref.shape[0]
        tk = min(nk, ATTN_TK)
        n_chunks = nk // tk
        if n_chunks == 1:
            carry = update(scores(k_ref[...]), vt_ref[...], carry)
            continue
        group = max(g for g in (ATTN_GROUP, 2, 1) if n_chunks % g == 0)

        def body(c, cr, k_ref=k_ref, vt_ref=vt_ref, tk=tk, group=group):
            starts = [pl.multiple_of((c * group + g) * tk, tk) for g in range(group)]
            ss = [scores(k_ref[pl.ds(k0, tk), :]) for k0 in starts]
            for s, k0 in zip(ss, starts):
                cr = update(s, vt_ref[:, pl.ds(k0, tk)], cr)
            return cr
        carry = lax.fori_loop(0, n_chunks // group, body, carry)
    _, l, acc = carry
    o_ref[...] = (acc / l).T


def _attention_t(q_arr, sources, n_heads, dq, dv):
    b, _, nq = q_arr.shape
    tq = min(nq, ATTN_TQ)
    in_specs = [pl.BlockSpec((None, dq, tq), lambda bb, h, i: (bb, h, i))]
    args = [q_arr]
    for k_arr, vt_arr in sources:
        nk = k_arr.shape[1]
        in_specs.append(pl.BlockSpec((None, nk, dq), lambda bb, h, i: (bb, 0, h)))
        in_specs.append(pl.BlockSpec((None, dv, nk), lambda bb, h, i: (bb, h, 0)))
        args += [k_arr, vt_arr]
    return pl.pallas_call(
        functools.partial(_attn_t_kernel, n_src=len(sources)),
        grid=(b, n_heads, nq // tq),
        in_specs=in_specs,
        out_specs=pl.BlockSpec((None, tq, dv), lambda bb, h, i: (bb, i, h)),
        out_shape=jax.ShapeDtypeStruct((b, nq, n_heads * dv), F32),
        compiler_params=_cparams(("parallel", "parallel", "arbitrary"), VMEM_MIB_XLARGE),
        name="attention_t",
    )(*args)


NAT_GROUP = 8
NAT_SLAB = NAT_GROUP + NAT_ROWS
NAT_HALVES = 2


def _nat_kernel(q_ref, k_ref, v_ref, kc_ref, vc_ref, bias_ref, o_ref, *, n_rows):
    i = pl.program_id(2)
    scale = HEAD_DIM ** -0.5
    nt = (((1,), (1,)), ((), ()))
    u0 = jnp.clip(i * NAT_GROUP - NAT_ROWS // 2, 0, n_rows - NAT_SLAB)
    t0 = pl.multiple_of(u0 * GRID_W, GRID_W)
    ks = k_ref[pl.ds(t0, NAT_SLAB * GRID_W), :].astype(BF16)
    vs = v_ref[pl.ds(t0, NAT_SLAB * GRID_W), :].astype(BF16)
    kc = kc_ref[...].astype(BF16)
    vc = vc_ref[...].astype(BF16)
    hq = q_ref.shape[0] // NAT_HALVES
    scores = []
    for j in range(NAT_HALVES):
        rows = slice(j * hq, (j + 1) * hq)
        q = (q_ref[rows, :].astype(F32) * scale).astype(BF16)
        scores.append((lax.dot_general(q, ks, nt, preferred_element_type=F32) + bias_ref[rows, :],
                       lax.dot_general(q, kc, nt, preferred_element_type=F32)))
    for j in range(NAT_HALVES):
        s_w, s_c = scores[j]
        m = jnp.maximum(jnp.max(s_w, axis=-1, keepdims=True), jnp.max(s_c, axis=-1, keepdims=True))
        p_w = jnp.exp(s_w - m)
        p_c = jnp.exp(s_c - m)
        l = jnp.sum(p_w, axis=-1, keepdims=True) + jnp.sum(p_c, axis=-1, keepdims=True)
        o = (jnp.dot(p_w.astype(BF16), vs, preferred_element_type=F32)
             + jnp.dot(p_c.astype(BF16), vc, preferred_element_type=F32))
        o_ref[j * hq:(j + 1) * hq, :] = o / l


def _nat_bias_table(rpb):
    h = rpb.shape[0]
    col = np.arange(GRID_W)
    c0 = np.clip(col - NAT_COLS // 2, 0, GRID_W - NAT_COLS)
    kc = np.arange(GRID_W)
    col_ok = (kc[None, :] >= c0[:, None]) & (kc[None, :] < c0[:, None] + NAT_COLS)
    dc = kc[None, :] - col[:, None] + NAT_COLS - 1
    sel_c = ((dc[None] == np.arange(2 * NAT_COLS - 1)[:, None, None]) & col_ok[None]).astype(np.float32)
    sel_c = sel_c.reshape(2 * NAT_COLS - 1, GRID_W * GRID_W)
    per_dr = jnp.einsum('hdc,cb->hdb', rpb.astype(F32), sel_c, precision=lax.Precision.HIGHEST)
    per_dr = jnp.where(col_ok.reshape(-1)[None, None], per_dr, NEG_BIG).reshape(h, 2 * NAT_ROWS - 1, GRID_W, GRID_W)
    return pl.pallas_call(
        _nat_table_kernel,
        grid=(h, 3),
        in_specs=[pl.BlockSpec((None, 2 * NAT_ROWS - 1, GRID_W, GRID_W), lambda hh, v: (hh, 0, 0, 0))],
        out_specs=pl.BlockSpec((None, None, NAT_GROUP * GRID_W, NAT_SLAB * GRID_W), lambda hh, v: (hh, v, 0, 0)),
        out_shape=jax.ShapeDtypeStruct((h, 3, NAT_GROUP * GRID_W, NAT_SLAB * GRID_W), F32),
        compiler_params=_cparams(("parallel", "arbitrary"), VMEM_MIB_SMALL),
        name="nat_bias_table",
    )(per_dr)


def _nat_table_kernel(per_dr_ref, o_ref):
    masked = jnp.full((GRID_W, GRID_W), NEG_BIG, F32)
    for variant, delta in enumerate((0, NAT_ROWS // 2, NAT_ROWS)):
        @pl.when(pl.program_id(1) == variant)
        def _(delta=delta):
            for g in range(NAT_GROUP):
                w0 = min(max(g + delta - NAT_ROWS // 2, 0), NAT_SLAB - NAT_ROWS)
                tiles = [per_dr_ref[u - (g + delta) + NAT_ROWS - 1] if w0 <= u < w0 + NAT_ROWS else masked
                         for u in range(NAT_SLAB)]
                o_ref[g * GRID_W:(g + 1) * GRID_W, :] = jnp.concatenate(tiles, axis=1)


def _nat_latent(p_l, p_c, bias_tbl):
    b, n, _ = p_l.shape
    n_ctx = p_c.shape[1]
    n_rows = n // GRID_W
    assert n_rows >= NAT_SLAB and n_rows % NAT_GROUP == 0
    n_groups = n_rows // NAT_GROUP
    tq = NAT_GROUP * GRID_W
    tk = NAT_SLAB * GRID_W

    def variant(i):
        return jnp.where(i == 0, 0, jnp.where(i == n_groups - 1, 2, 1))

    return pl.pallas_call(
        functools.partial(_nat_kernel, n_rows=n_rows),
        grid=(b, N_HEADS_NAT, n_groups),
        in_specs=[pl.BlockSpec((None, tq, LANE), lambda bb, h, i: (bb, i, COL_NAT_Q + h)),
                  pl.BlockSpec((None, n, LANE), lambda bb, h, i: (bb, 0, COL_NAT_K + h)),
                  pl.BlockSpec((None, n, LANE), lambda bb, h, i: (bb, 0, COL_NAT_V + h)),
                  pl.BlockSpec((None, n_ctx, LANE), lambda bb, h, i: (bb, 0, COL_NAT_K + h)),
                  pl.BlockSpec((None, n_ctx, LANE), lambda bb, h, i: (bb, 0, COL_NAT_V + h)),
                  pl.BlockSpec((None, None, tq, tk), lambda bb, h, i: (h, variant(i), 0, 0))],
        out_specs=pl.BlockSpec((None, tq, LANE), lambda bb, h, i: (bb, i, h)),
        out_shape=jax.ShapeDtypeStruct((b, n, N_HEADS_NAT * HEAD_DIM), F32),
        compiler_params=_cparams(("parallel", "parallel", "arbitrary"), VMEM_MIB_LARGE),
        name="nat_latent",
    )(p_l, p_l, p_l, p_c, p_c, bias_tbl)


def _ret_consts(logit, reverse):
    c_len = RET_CHUNK
    z = jnp.full((c_len, c_len), -logit, F32)
    lg = -(jnp.maximum(z, 0.0) + jnp.log1p(jnp.exp(-jnp.abs(z))))
    ci = lax.broadcasted_iota(jnp.int32, (c_len, c_len), 0).astype(F32)
    si = lax.broadcasted_iota(jnp.int32, (c_len, c_len), 1).astype(F32)
    if reverse:
        rel = si - ci
        wq = jnp.exp(lg * (c_len - ci))
        wk = jnp.exp(lg * ci)
    else:
        rel = ci - si
        wq = jnp.exp(lg * (ci + 1.0))
        wk = jnp.exp(lg * (c_len - 1.0 - ci))
    dmat = jnp.where(rel >= 0, jnp.exp(lg * jnp.maximum(rel, 0.0)), 0.0)
    return wq, wk, dmat, jnp.exp(lg * c_len)


def _ret_kernel(lf_ref, lb_ref, qf_ref, kf_ref, vf_ref, gf_ref, qb_ref, kb_ref, vb_ref, gb_ref, s0f_ref, s0b_ref,
                yf_ref, yb_ref, sf_ref, sb_ref, c_scr, *, n_chunks):
    h = pl.program_id(1)

    @pl.when(pl.program_id(2) == 0)
    def _():
        sf_ref[...] = s0f_ref[...]
        sb_ref[...] = s0b_ref[...]
        for j, arr in enumerate(_ret_consts(lf_ref[h], False) + _ret_consts(lb_ref[h], True)):
            c_scr[j] = arr

    c_len = RET_CHUNK
    nt = (((1,), (1,)), ((), ()))
    k_scale = HEAD_DIM ** -0.5
    dirs = []
    for d, refs in enumerate(((qf_ref, kf_ref, vf_ref, gf_ref, yf_ref), (qb_ref, kb_ref, vb_ref, gb_ref, yb_ref))):
        order = list(range(n_chunks)) if d == 0 else list(range(n_chunks - 1, -1, -1))
        dirs.append(refs + tuple(c_scr[4 * d + j] for j in range(4)) + (order,))

    units = []
    for step in range(n_chunks):
        for d in range(2):
            q_ref, k_ref, v_ref, _, _, _, wk, dmat, _, order = dirs[d]
            sl = slice(order[step] * c_len, (order[step] + 1) * c_len)
            qb = q_ref[sl, :].astype(BF16)
            kb = k_ref[sl, :]
            vb = v_ref[sl, :].astype(BF16)
            sc = lax.dot_general(qb, kb.astype(BF16), nt, preferred_element_type=F32) * (dmat * k_scale)
            kv = jnp.dot((kb.astype(F32) * (wk * k_scale)).T.astype(BF16), vb, preferred_element_type=F32)
            units.append((d, sl, qb, vb, sc, kv))
    states = [sf_ref[...], sb_ref[...]]
    cross = []
    for (d, sl, qb, vb, sc, kv) in units:
        wq, g_chunk = dirs[d][5], dirs[d][8]
        cross.append(wq * jnp.dot(qb, states[d].astype(BF16), preferred_element_type=F32))
        states[d] = g_chunk * states[d] + kv
    sf_ref[...] = states[0]
    sb_ref[...] = states[1]
    for (d, sl, qb, vb, sc, kv), o_cross in zip(units, cross):
        g_ref, y_ref = dirs[d][3], dirs[d][4]
        o = jnp.dot(sc.astype(BF16), vb, preferred_element_type=F32) + o_cross
        mu = jnp.mean(o, axis=-1, keepdims=True)
        dev = o - mu
        gn = dev * lax.rsqrt(jnp.mean(dev * dev, axis=-1, keepdims=True) + EPS)
        y_ref[sl, :] = _silu(g_ref[sl, :].astype(F32)) * gn


def _retention_scan(p, logit_f, logit_b, s0_f, s0_b):
    b, n, _ = p.shape
    hh = N_HEADS_RET
    tb = min(n, 16 * RET_CHUNK)
    nblk = n // tb

    def fwd(c0):
        return pl.BlockSpec((None, tb, LANE), lambda bb, h, i: (bb, i, c0 + h))

    def bwd(c0):
        return pl.BlockSpec((None, tb, LANE), lambda bb, h, i: (bb, nblk - 1 - i, c0 + h))

    state = pl.BlockSpec((None, None, HEAD_DIM, HEAD_DIM), lambda bb, h, i: (bb, h, 0, 0))
    smem = pl.BlockSpec(memory_space=pltpu.SMEM)
    y_shape = jax.ShapeDtypeStruct((b, n, hh * HEAD_DIM), F32)
    s_shape = jax.ShapeDtypeStruct((b, hh, HEAD_DIM, HEAD_DIM), F32)
    return pl.pallas_call(
        functools.partial(_ret_kernel, n_chunks=tb // RET_CHUNK),
        grid=(b, hh, nblk),
        in_specs=[smem, smem,
                  fwd(COL_RET_Q), fwd(COL_RET_K), fwd(COL_RET_V), fwd(COL_RET_GF),
                  bwd(COL_RET_Q), bwd(COL_RET_K), bwd(COL_RET_V), bwd(COL_RET_GB),
                  state, state],
        out_specs=[fwd(0), bwd(0), state, state],
        out_shape=[y_shape, y_shape, s_shape, s_shape],
        scratch_shapes=[pltpu.VMEM((8, RET_CHUNK, RET_CHUNK), F32)],
        compiler_params=_cparams(("parallel", "parallel", "arbitrary"), VMEM_MIB_SMALL),
        name="retention",
    )(logit_f.astype(F32), logit_b.astype(F32), p, p, p, p, p, p, p, p, s0_f, s0_b)


def _retention(p_c, p_l, logit_f, logit_b):
    b = p_c.shape[0]
    s0 = jnp.zeros((b, N_HEADS_RET, HEAD_DIM, HEAD_DIM), F32)
    yc_f, yc_b, s_cf, s_cb = _retention_scan(p_c, logit_f, logit_b, s0, s0)
    yl_f, yl_b, _, _ = _retention_scan(p_l, logit_f, logit_b, s_cf, s_cb)
    return (yc_f, yc_b), (yl_f, yl_b)


MERGE_ROWS = 512
MERGE_HALVES = 2


def _merge_kernel(x_ref, ya_ref, yb_ref, yrf_ref, yrb_ref, g_ref, gate_ref, w_ref, g2_ref, sh2_ref, sc2_ref, wr_ref,
                  o_ref, h_ref, aff_ref):
    wa = ya_ref.shape[-1]
    wb = yb_ref.shape[-1]
    hr = x_ref.shape[0] // MERGE_HALVES
    halves = [slice(j * hr, (j + 1) * hr) for j in range(MERGE_HALVES)]
    wr = wr_ref[...]
    w_hi = wr.astype(BF16)
    w_lo_f = wr - w_hi.astype(F32)
    w_hi_lo = (w_hi.astype(F32) + pltpu.roll(w_lo_f, N_EXPERTS, 1)).astype(BF16)
    normed = []
    for rows in halves:
        na = (_rms(ya_ref[rows, :]) * g_ref[:, :wa]).astype(BF16)
        nb = (_rms(yb_ref[rows, :]) * g_ref[:, wa:wa + wb]).astype(BF16)
        nr = (_rms(yrf_ref[rows, :] + yrb_ref[rows, :]) * g_ref[:, wa + wb:]).astype(BF16)
        normed.append((na, nb, nr))
    accs = []
    for na, nb, nr in normed:
        acc = jnp.dot(na, w_ref[:wa, :], preferred_element_type=F32)
        acc = acc + jnp.dot(nb, w_ref[wa:wa + wb, :], preferred_element_type=F32)
        accs.append(acc + jnp.dot(nr, w_ref[wa + wb:, :], preferred_element_type=F32))
    for rows, acc in zip(halves, accs):
        x_new = x_ref[rows, :] + gate_ref[...] * acc
        o_ref[rows, :] = x_new
        h = (_rms(x_new) * g2_ref[...]) * (1.0 + sc2_ref[...]) + sh2_ref[...]
        h_hi = h.astype(BF16)
        h_ref[rows, :] = h_hi
        h_lo = (h - h_hi.astype(F32)).astype(BF16)
        hh = jnp.dot(h_hi, w_hi_lo, preferred_element_type=F32)
        logits = hh + (pltpu.roll(hh, LANE - N_EXPERTS, 1) + jnp.dot(h_lo, w_hi, preferred_element_type=F32))
        lane = lax.broadcasted_iota(jnp.int32, logits.shape, 1)
        logits = jnp.where(lane < N_EXPERTS, logits, NEG_BIG)
        e = jnp.exp(logits - jnp.max(logits, axis=-1, keepdims=True))
        aff_ref[rows, :] = e / jnp.sum(e, axis=-1, keepdims=True)


def _merge_out(x, ya, yb, yr, g, gate, w_out, g2, shift2, scale2, w_router_pad):
    b, n, d = x.shape
    yrf, yrb = yr
    tm = min(n, MERGE_ROWS)
    dm = w_out.shape[0]

    def tok(width):
        return pl.BlockSpec((None, tm, width), lambda bb, i: (bb, i, 0))

    row = pl.BlockSpec((1, d), lambda bb, i: (0, 0))
    per_batch = pl.BlockSpec((None, 1, d), lambda bb, i: (bb, 0, 0))
    return pl.pallas_call(
        _merge_kernel,
        grid=(b, n // tm),
        in_specs=[tok(d), tok(ya.shape[-1]), tok(yb.shape[-1]), tok(yrf.shape[-1]), tok(yrb.shape[-1]),
                  pl.BlockSpec((1, dm), lambda bb, i: (0, 0)), per_batch,
                  pl.BlockSpec((dm, d), lambda bb, i: (0, 0)),
                  row, per_batch, per_batch,
                  pl.BlockSpec((d, LANE), lambda bb, i: (0, 0))],
        out_specs=[tok(d), tok(d), tok(LANE)],
        out_shape=[jax.ShapeDtypeStruct((b, n, d), F32), jax.ShapeDtypeStruct((b, n, d), BF16),
                   jax.ShapeDtypeStruct((b, n, LANE), F32)],
        compiler_params=_cparams(("parallel", "parallel"), VMEM_MIB_MAX),
        name="merge_out",
    )(x, ya, yb, yrf, yrb, g.reshape(1, dm), gate, w_out, g2.reshape(1, d), shift2, scale2, w_router_pad)


def _ffn_up_kernel(xs_ref, wg_ref, wu_ref, hm_ref, wg_scr, wu_scr):
    @pl.when(pl.program_id(1) == 0)
    def _():
        wg_scr[...] = wg_ref[...].astype(BF16)
        wu_scr[...] = wu_ref[...].astype(BF16)

    xs = xs_ref[...]
    a = jnp.dot(xs, wg_scr[...], preferred_element_type=F32)
    u = jnp.dot(xs, wu_scr[...], preferred_element_type=F32)
    hm_ref[...] = (_silu(a) * u).astype(BF16)


def _ffn_down_kernel(hm_ref, gate_ref, wd_ref, o_ref, wd_scr):
    @pl.when(pl.program_id(1) == 0)
    def _():
        wd_scr[...] = wd_ref[...].astype(BF16)

    o_ref[...] = (jnp.dot(hm_ref[...], wd_scr[...], preferred_element_type=F32) * gate_ref[...]).astype(o_ref.dtype)


def _expert_ffn(xs, gates, w_gate, w_up, w_down, layer):
    e, t, d = xs.shape
    f = w_gate.shape[-1]
    tm = min(t, 512)
    tm_down = min(t, 1024)
    hm = pl.pallas_call(
        _ffn_up_kernel,
        grid=(e, t // tm),
        in_specs=[pl.BlockSpec((None, tm, d), lambda ee, i: (ee, i, 0)),
                  pl.BlockSpec((None, None, d, f), lambda ee, i: (layer, ee, 0, 0)),
                  pl.BlockSpec((None, None, d, f), lambda ee, i: (layer, ee, 0, 0))],
        out_specs=pl.BlockSpec((None, tm, f), lambda ee, i: (ee, i, 0)),
        out_shape=jax.ShapeDtypeStruct((e, t, f), BF16),
        scratch_shapes=[pltpu.VMEM((d, f), BF16), pltpu.VMEM((d, f), BF16)],
        compiler_params=_cparams(("parallel", "arbitrary"), VMEM_MIB_XLARGE),
        name="expert_ffn_up",
    )(xs, w_gate, w_up)
    return pl.pallas_call(
        _ffn_down_kernel,
        grid=(e, t // tm_down),
        in_specs=[pl.BlockSpec((None, tm_down, f), lambda ee, i: (ee, i, 0)),
                  pl.BlockSpec((None, tm_down, 1), lambda ee, i: (ee, i, 0)),
                  pl.BlockSpec((None, None, f, d), lambda ee, i: (layer, ee, 0, 0))],
        out_specs=pl.BlockSpec((None, tm_down, d), lambda ee, i: (ee, i, 0)),
        out_shape=jax.ShapeDtypeStruct((e, t, d), BF16),
        scratch_shapes=[pltpu.VMEM((f, d), BF16)],
        compiler_params=_cparams(("parallel", "arbitrary"), VMEM_MIB_XLARGE),
        name="expert_ffn_down",
    )(hm, gates, w_down)


COMBINE_TOKENS = 256
COMBINE_ROWS = 512


def _combine_kernel(*refs, final_norm, aliased):
    if aliased:
        tile_ref, win_ref, flag_ref, x_ref, yp_ref, tok_ref, gate_ref, gn_ref, _, o_ref, acc_scr = refs
    else:
        tile_ref, win_ref, flag_ref, x_ref, yp_ref, tok_ref, gate_ref, gn_ref, o_ref, acc_scr = refs
    s = pl.program_id(0)
    flags = flag_ref[s]
    t = acc_scr.shape[0]

    @pl.when((flags & 1) != 0)
    def _():
        acc_scr[...] = jnp.zeros(acc_scr.shape, F32)

    @pl.when((flags & 2) != 0)
    def _():
        tcol = tile_ref[s] * t + lax.broadcasted_iota(jnp.int32, (t, 1), 0)
        onehot = jnp.where(tcol == tok_ref[...], 1.0, 0.0).astype(BF16)
        acc_scr[...] += jnp.dot(onehot, yp_ref[...], preferred_element_type=F32)

    @pl.when((flags & 4) != 0)
    def _():
        y = x_ref[...] + gate_ref[...] * acc_scr[...]
        if final_norm:
            y = _rms(y) * gn_ref[...]
        o_ref[...] = y


def _moe_combine(x, y_rows, tok_rows, gate, norm_g, final_norm, batch0=0, n_batches=None, out_prev=None):
    b, n, d = x.shape
    r = y_rows.shape[0]
    t = min(COMBINE_TOKENS, n)
    w = COMBINE_ROWS
    tiles_per_batch = n // t
    ntiles = (b if n_batches is None else n_batches) * tiles_per_batch
    tile0 = batch0 * tiles_per_batch
    nwin_total = r // w
    n_steps = nwin_total + 2 * ntiles
    i32 = jnp.int32
    tok_sorted, order = lax.sort_key_val(tok_rows.astype(i32), jnp.arange(r, dtype=i32))
    yp = jnp.take(y_rows, order, axis=0, mode="clip")
    edges = (tile0 + jnp.arange(ntiles + 1, dtype=i32)) * t
    bounds = jnp.sum((tok_sorted[None, :] < edges[:, None]).astype(i32), axis=1)
    w_lo = jnp.minimum(bounds[:-1] // w, nwin_total - 1)
    w_hi = jnp.maximum((bounds[1:] + w - 1) // w, w_lo + 1)
    nwin = w_hi - w_lo
    step0 = jnp.cumsum(nwin) - nwin
    total = jnp.sum(nwin)
    s = jnp.arange(n_steps, dtype=i32)
    tile_s = jnp.sum((s[:, None] >= step0[None, :]).astype(i32), axis=1) - 1
    sel = tile_s[:, None] == jnp.arange(ntiles, dtype=i32)[None, :]
    tile_s = tile_s + tile0

    def of_tile(v):
        return jnp.sum(jnp.where(sel, v[None, :], 0), axis=1)

    step0_s, w_lo_s, nwin_s = of_tile(step0), of_tile(w_lo), of_tile(nwin)
    win_s = jnp.minimum(w_lo_s + (s - step0_s), w_lo_s + nwin_s - 1)
    valid = s < total
    first = valid & (s == step0_s)
    last = valid & (s == step0_s + nwin_s - 1)
    flags = first.astype(i32) + 2 * valid.astype(i32) + 4 * last.astype(i32)
    in_specs = [pl.BlockSpec((t, d), lambda s, tl, wn, fl: (tl[s], 0)),
                pl.BlockSpec((w, d), lambda s, tl, wn, fl: (wn[s], 0)),
                pl.BlockSpec((None, 1, w), lambda s, tl, wn, fl: (wn[s], 0, 0)),
                pl.BlockSpec((None, 1, d), lambda s, tl, wn, fl: (tl[s] // tiles_per_batch, 0, 0)),
                pl.BlockSpec((1, d), lambda s, tl, wn, fl: (0, 0))]
    args = [tile_s, win_s, flags, x.reshape(b * n, d), yp, tok_sorted.reshape(nwin_total, 1, w), gate,
            norm_g.reshape(1, d)]
    aliases = {}
    if out_prev is not None:
        in_specs.append(pl.BlockSpec(memory_space=pl.ANY))
        args.append(out_prev.reshape(b * n, d))
        aliases = {len(args) - 1: 0}
    out = pl.pallas_call(
        functools.partial(_combine_kernel, final_norm=final_norm, aliased=out_prev is not None),
        grid_spec=pltpu.PrefetchScalarGridSpec(
            num_scalar_prefetch=3,
            grid=(n_steps,),
            in_specs=in_specs,
            out_specs=pl.BlockSpec((t, d), lambda s, tl, wn, fl: (tl[s], 0)),
            scratch_shapes=[pltpu.VMEM((t, d), F32)]),
        out_shape=jax.ShapeDtypeStruct((b * n, d), F32),
        compiler_params=_cparams(("arbitrary",), VMEM_MIB_MEDIUM),
        input_output_aliases=aliases,
        name="moe_combine",
    )(*args)
    return out.reshape(b, n, d)


def _moe_select(aff):
    b, n, _ = aff.shape
    cap = EC_CAPACITY * n // N_EXPERTS
    gate, idx = lax.top_k(jnp.swapaxes(aff[..., :N_EXPERTS], 1, 2), cap)
    flat = (idx + (jnp.arange(b, dtype=idx.dtype) * n)[:, None, None]).transpose(1, 0, 2).reshape(-1)
    return flat, gate.transpose(1, 0, 2).reshape(N_EXPERTS, b * cap, 1)


def _moe_gather(h, flat):
    b, n, d = h.shape
    return jnp.take(h.reshape(b * n, d), flat, axis=0, mode="clip").reshape(N_EXPERTS, -1, d)


def _moe_latent_split(x, h, aff, gate2, experts, norm_g, final_norm):
    b, n, d = x.shape
    size = b // 2 if b % 2 == 0 else b
    routed = []
    for b0 in range(0, b, size):
        flat, gates = _moe_select(aff[b0:b0 + size])
        flat = flat + b0 * n
        routed.append((b0, flat, gates, _moe_gather(h, flat)))
    out = None
    for b0, flat, gates, xs in routed:
        y = _expert_ffn(xs, gates, *experts)
        out = _moe_combine(x, y.reshape(-1, d), flat, gate2, norm_g, final_norm, b0, size, out)
    return out


def _moe_experts(x, xs, gates, flat, gate2, experts, norm_g, final_norm):
    y = _expert_ffn(xs, gates, *experts)
    return _moe_combine(x, y.reshape(-1, x.shape[-1]), flat, gate2, norm_g, final_norm)


_ROPE_SRC = np.concatenate([np.arange(16, 32), np.arange(0, 16), np.arange(48, 64), np.arange(32, 48)])
_ROPE_SIGN = np.concatenate([-np.ones(16), np.ones(16), -np.ones(16), np.ones(16)]).astype(np.float32)


def _rope_perm_cols(w):
    return w[:, _ROPE_SRC] * _ROPE_SIGN


def _pad_w_in(w_in):
    c_kr = MLA_Q_LORA + MLA_KV_LORA
    kr = w_in[:, c_kr:c_kr + MLA_ROPE_DIM]
    return jnp.concatenate([w_in[:, :c_kr + MLA_ROPE_DIM], _rope_perm_cols(kr), w_in[:, c_kr + MLA_ROPE_DIM:]],
                           axis=1).astype(BF16)


def _expand_w_uq(w_uq):
    w = w_uq.reshape(MLA_Q_LORA, N_HEADS_MLA, HEAD_DIM + MLA_ROPE_DIM)
    nope = w[:, :, :HEAD_DIM]
    rope = w[:, :, HEAD_DIM:]
    partner = rope[:, :, _ROPE_SRC] * _ROPE_SIGN
    return jnp.concatenate([nope, rope, partner], axis=-1).reshape(MLA_Q_LORA, N_HEADS_MLA * MLA_QK_PAD).astype(BF16)


def _rope_tables(n):
    t = jnp.arange(n)
    n_freq = MLA_ROPE_DIM // 4
    inv = ROPE_BASE ** (-jnp.arange(n_freq, dtype=F32) / n_freq)
    ang_r = (t // GRID_W).astype(F32)[:, None] * inv
    ang_c = (t % GRID_W).astype(F32)[:, None] * inv
    zeros = jnp.zeros((n, LANE - MLA_ROPE_DIM), F32)
    cos = jnp.concatenate([jnp.cos(ang_r), jnp.cos(ang_r), jnp.cos(ang_c), jnp.cos(ang_c), zeros], axis=1)
    sin = jnp.concatenate([jnp.sin(ang_r), jnp.sin(ang_r), jnp.sin(ang_c), jnp.sin(ang_c), zeros], axis=1)
    return cos, sin


def _identity_tables(n):
    ones = jnp.ones((n, MLA_ROPE_DIM), F32)
    zeros = jnp.zeros((n, LANE - MLA_ROPE_DIM), F32)
    return jnp.concatenate([ones, zeros], axis=1), jnp.zeros((n, LANE), F32)


def _layer(x_l, x_c, mod, lp, last, final_g):
    b, n, d = x_l.shape
    n_ctx = x_c.shape[1]

    def chunk(k, ctx):
        rows = jnp.broadcast_to(mod[b, k * d:(k + 1) * d], (b, d)) if ctx else mod[:b, k * d:(k + 1) * d]
        return rows.reshape(b, 1, d)

    w_in = _pad_w_in(lp['w_in'])
    p_l = _norm_proj(x_l, lp['norm1_g'], chunk(0, False), chunk(1, False), w_in)
    p_c = _norm_proj(x_c, lp['norm1_g'], chunk(0, True), chunk(1, True), w_in)

    cos_l, sin_l = _rope_tables(n)
    cos_c, sin_c = _identity_tables(n_ctx)
    w_uk = lp['mla_w_uk'].reshape(MLA_KV_LORA, -1).astype(BF16)
    w_uv_t = lp['mla_w_uv'].reshape(MLA_KV_LORA, -1).T.astype(BF16)
    w_q_t = _expand_w_uq(lp['mla_w_uq']).T
    k_l, v_l = _mla_kv_prep(p_l, lp['mla_kv_norm_g'], w_uk, w_uv_t, cos_l, sin_l)
    k_c, v_c = _mla_kv_prep(p_c, lp['mla_kv_norm_g'], w_uk, w_uv_t, cos_c, sin_c)
    q_l = _mla_q_prep(p_l, lp['mla_q_norm_g'], w_q_t, cos_l.T, sin_l.T)
    a_l = _attention_t(q_l, [(k_c, v_c), (k_l, v_l)], N_HEADS_MLA, MLA_QK_PAD, HEAD_DIM)

    b_l = _nat_latent(p_l, p_c, _nat_bias_table(lp['nat_rpb']))
    r_c, r_l = _retention(p_c, p_l, lp['ret_decay_f'], lp['ret_decay_b'])

    w_out = lp['w_out'].astype(BF16)
    w_router = jnp.zeros((d, LANE), F32).at[:, :N_EXPERTS].set(lp['w_router'])

    x_l, h_l, aff_l = _merge_out(x_l, a_l, b_l, r_l, lp['out_norm_g'], chunk(2, False), w_out,
                                 lp['norm2_g'], chunk(3, False), chunk(4, False), w_router)
    if last:
        return _moe_latent_split(x_l, h_l, aff_l, chunk(5, False), lp['experts'], final_g, True), None
    flat_l, gates_l = _moe_select(aff_l)

    flat_l, p_c, x_c = lax.optimization_barrier((flat_l, p_c, x_c))
    xs_l = _moe_gather(h_l, flat_l)
    q_c = _mla_q_prep(p_c, lp['mla_q_norm_g'], w_q_t, cos_c.T, sin_c.T)
    a_c = _attention_t(q_c, [(k_c, v_c)], N_HEADS_MLA, MLA_QK_PAD, HEAD_DIM)
    b_c = _attention(p_c, COL_NAT_Q, HEAD_DIM, [(p_c, COL_NAT_K, p_c, COL_NAT_V)], HEAD_DIM, N_HEADS_NAT,
                     HEAD_DIM ** -0.5)
    x_c, h_c, aff_c = _merge_out(x_c, a_c, b_c, r_c, lp['out_norm_g'], chunk(2, True), w_out,
                                 lp['norm2_g'], chunk(3, True), chunk(4, True), w_router)
    flat_c, gates_c = _moe_select(aff_c)
    x_c = _moe_experts(x_c, _moe_gather(h_c, flat_c), gates_c, flat_c, chunk(5, True), lp['experts'], final_g, False)
    xs_l, x_c = lax.optimization_barrier((xs_l, x_c))
    x_l = _moe_experts(x_l, xs_l, gates_l, flat_l, chunk(5, False), lp['experts'], final_g, False)
    return x_l, x_c


def kernel(x, c, ctx, c_ctx, w_mod, b_mod, norm1_g, w_in, mla_q_norm_g, mla_kv_norm_g, mla_w_uq, mla_w_uk,
           mla_w_uv, nat_rpb, ret_decay_f, ret_decay_b, out_norm_g, w_out, norm2_g, w_router, w_gate, w_up,
           w_down, final_norm_g):
    depth = w_mod.shape[0]
    cvecs = jnp.concatenate([c, c_ctx[None, :]], axis=0)
    x_l, x_c = x, ctx
    for i in range(depth):
        lp = {
            'norm1_g': norm1_g[i], 'w_in': w_in[i], 'mla_q_norm_g': mla_q_norm_g[i],
            'mla_kv_norm_g': mla_kv_norm_g[i], 'mla_w_uq': mla_w_uq[i], 'mla_w_uk': mla_w_uk[i],
            'mla_w_uv': mla_w_uv[i], 'nat_rpb': nat_rpb[i], 'ret_decay_f': ret_decay_f[i],
            'ret_decay_b': ret_decay_b[i], 'out_norm_g': out_norm_g[i], 'w_out': w_out[i],
            'norm2_g': norm2_g[i], 'w_router': w_router[i], 'experts': (w_gate, w_up, w_down, i),
        }
        mod = _modulation(cvecs, w_mod, b_mod[i], i)
        x_l, x_c = _layer(x_l, x_c, mod, lp, i == depth - 1, final_norm_g)
    return x_l
```

```python
import functools

import numpy as np
import jax
import jax.numpy as jnp
from jax import lax
from jax.experimental import pallas as pl
from jax.experimental.pallas import tpu as pltpu

F32 = jnp.float32
BF16 = jnp.bfloat16

GRID_W = 64
HEAD_DIM = 128
N_HEADS_MLA = 8
N_HEADS_NAT = 4
N_HEADS_RET = 4
MLA_ROPE_DIM = 64
MLA_Q_LORA = 512
MLA_KV_LORA = 256
MLA_SCALE = (HEAD_DIM + MLA_ROPE_DIM) ** -0.5
LOG2E = 1.4426950408889634
MLA_Q_SCALE = MLA_SCALE * LOG2E
MLA_QK_PAD = 256
NAT_ROWS = 8
NAT_COLS = 16
RET_CHUNK = 128
N_EXPERTS = 16
EC_CAPACITY = 2
ROPE_BASE = 10000.0
EPS = 1e-6
NEG_BIG = -1e30
LANE = 128

COL_CQ = 0
COL_CKV = 4
COL_KR = 6
COL_NAT_Q = 7
COL_NAT_K = 11
COL_NAT_V = 15
COL_RET_Q = 19
COL_RET_K = 23
COL_RET_V = 27
COL_RET_GF = 31
COL_RET_GB = 35
IN_COLS_PAD = 39 * LANE


VMEM_MIB_SMALL = 32
VMEM_MIB_MEDIUM = 40
VMEM_MIB_LARGE = 48
VMEM_MIB_XLARGE = 56
VMEM_MIB_MAX = 58


def _cparams(sem, vmem_mib):
    return pltpu.CompilerParams(dimension_semantics=sem, vmem_limit_bytes=vmem_mib * 1024 * 1024)


def _silu(a):
    return a * (1.0 / (1.0 + jnp.exp(-a)))


def _rms(x):
    return x * lax.rsqrt(jnp.mean(x * x, axis=-1, keepdims=True) + EPS)


def _mod_kernel(c_ref, w_ref, b_ref, o_ref):
    a = _silu(c_ref[...])
    a_hi = a.astype(BF16)
    a_lo = (a - a_hi.astype(F32)).astype(BF16)
    w = w_ref[...]
    w_hi = w.astype(BF16)
    w_lo = (w - w_hi.astype(F32)).astype(BF16)
    r = jnp.dot(jnp.concatenate([a_hi, a_lo], axis=0), w_hi, preferred_element_type=F32)
    o_ref[...] = r[:8] + (r[8:] + jnp.dot(a_hi, w_lo, preferred_element_type=F32)) + b_ref[...]


def _modulation(cvecs, w_mod, b_mod, layer):
    n_rows, d = cvecs.shape
    n_out = w_mod.shape[-1]
    tn = 1024
    c8 = jnp.zeros((8, d), F32).at[:n_rows].set(cvecs)
    return pl.pallas_call(
        _mod_kernel,
        grid=(n_out // tn,),
        in_specs=[pl.BlockSpec((8, d), lambda j: (0, 0)),
                  pl.BlockSpec((None, d, tn), lambda j: (layer, 0, j)),
                  pl.BlockSpec((1, tn), lambda j: (0, j))],
        out_specs=pl.BlockSpec((8, tn), lambda j: (0, j)),
        out_shape=jax.ShapeDtypeStruct((8, n_out), F32),
        compiler_params=_cparams(("parallel",), VMEM_MIB_MEDIUM),
        name="modulation",
    )(c8, w_mod, b_mod.reshape(1, n_out))


NORM_PROJ_PARTS = 2


def _norm_proj_kernel(x_ref, g_ref, sh_ref, sc_ref, w_ref, o_ref, h_scr):
    j = pl.program_id(2)
    pr = x_ref.shape[0] // NORM_PROJ_PARTS
    parts = [slice(k * pr, (k + 1) * pr) for k in range(NORM_PROJ_PARTS)]

    @pl.when(j == 0)
    def _():
        hs = []
        for rows in parts:
            y = _rms(x_ref[rows, :]) * g_ref[...]
            h = (y * (1.0 + sc_ref[...]) + sh_ref[...]).astype(BF16)
            h_scr[rows, :] = h
            hs.append(h)
        for rows, h in zip(parts, hs):
            o_ref[rows, :] = jnp.dot(h, w_ref[...], preferred_element_type=F32).astype(o_ref.dtype)

    @pl.when(j != 0)
    def _():
        o_ref[...] = jnp.dot(h_scr[...], w_ref[...], preferred_element_type=F32).astype(o_ref.dtype)


def _norm_proj(x, g, shift, scale, w):
    b, n, d = x.shape
    c = w.shape[1]
    tm = min(n, 1024)
    tn = c // 3 if (c % (3 * LANE) == 0) else c
    return pl.pallas_call(
        _norm_proj_kernel,
        grid=(b, n // tm, c // tn),
        in_specs=[pl.BlockSpec((None, tm, d), lambda bb, i, j: (bb, i, 0)),
                  pl.BlockSpec((1, d), lambda bb, i, j: (0, 0)),
                  pl.BlockSpec((None, 1, d), lambda bb, i, j: (bb, 0, 0)),
                  pl.BlockSpec((None, 1, d), lambda bb, i, j: (bb, 0, 0)),
                  pl.BlockSpec((d, tn), lambda bb, i, j: (0, j))],
        out_specs=pl.BlockSpec((None, tm, tn), lambda bb, i, j: (bb, i, j)),
        out_shape=jax.ShapeDtypeStruct((b, n, c), BF16),
        scratch_shapes=[pltpu.VMEM((tm, d), BF16)],
        compiler_params=_cparams(("parallel", "parallel", "arbitrary"), VMEM_MIB_XLARGE),
        name="norm_proj",
    )(x, g.reshape(1, d), shift, scale, w)


def _rope_rotate(r, cos, sin):
    return r * cos + pltpu.roll(r, 64, 1) * sin


def _mla_q_kernel(cq_ref, g_ref, wt_ref, cos_ref, sin_ref, o_ref, *, n_heads):
    cqn_t = (_rms(cq_ref[...].astype(F32)) * g_ref[...]).T.astype(BF16)
    qe = jnp.dot(wt_ref[...], cqn_t, preferred_element_type=F32)
    cos = cos_ref[...]
    sin = sin_ref[...]
    for h in range(n_heads):
        c0 = h * MLA_QK_PAD
        o_ref[c0:c0 + LANE, :] = (qe[c0:c0 + LANE, :] * MLA_Q_SCALE).astype(BF16)
        r = qe[c0 + LANE:c0 + 2 * LANE, :]
        swapped = jnp.concatenate([r[MLA_ROPE_DIM:, :], r[:MLA_ROPE_DIM, :]], axis=0)
        o_ref[c0 + LANE:c0 + 2 * LANE, :] = ((r * cos + swapped * sin) * MLA_Q_SCALE).astype(BF16)


def _mla_q_prep(p, g, w_q_t, cos_t, sin_t):
    b, n, _ = p.shape
    tm = min(n, 512)
    cw = N_HEADS_MLA * MLA_QK_PAD
    return pl.pallas_call(
        functools.partial(_mla_q_kernel, n_heads=N_HEADS_MLA),
        grid=(b, n // tm),
        in_specs=[pl.BlockSpec((None, tm, MLA_Q_LORA), lambda bb, i: (bb, i, COL_CQ * LANE // MLA_Q_LORA)),
                  pl.BlockSpec((1, MLA_Q_LORA), lambda bb, i: (0, 0)),
                  pl.BlockSpec((cw, MLA_Q_LORA), lambda bb, i: (0, 0)),
                  pl.BlockSpec((LANE, tm), lambda bb, i: (0, i)),
                  pl.BlockSpec((LANE, tm), lambda bb, i: (0, i))],
        out_specs=pl.BlockSpec((None, cw, tm), lambda bb, i: (bb, 0, i)),
        out_shape=jax.ShapeDtypeStruct((b, cw, n), BF16),
        compiler_params=_cparams(("parallel", "parallel"), VMEM_MIB_MEDIUM),
        name="mla_q_prep",
    )(p, g.reshape(1, MLA_Q_LORA), w_q_t, cos_t, sin_t)


def _mla_kv_kernel(ckv_ref, kr_ref, g_ref, wuk_ref, wuvt_ref, cos_ref, sin_ref, k_ref, v_ref, *, n_heads):
    ckvn = _rms(ckv_ref[...].astype(F32)) * g_ref[...]
    kn = jnp.dot(ckvn.astype(BF16), wuk_ref[...], preferred_element_type=F32)
    v_ref[...] = jnp.dot(wuvt_ref[...], ckvn.T.astype(BF16), preferred_element_type=F32).astype(BF16)
    rot = _rope_rotate(kr_ref[...].astype(F32), cos_ref[...], sin_ref[...]).astype(BF16)
    for h in range(n_heads):
        c0 = h * MLA_QK_PAD
        k_ref[:, c0:c0 + LANE] = kn[:, h * LANE:(h + 1) * LANE].astype(BF16)
        k_ref[:, c0 + LANE:c0 + 2 * LANE] = rot


def _mla_kv_prep(p, g, w_uk, w_uv_t, cos, sin):
    b, n, _ = p.shape
    tm = min(n, 512)
    kw = N_HEADS_MLA * MLA_QK_PAD
    vw = N_HEADS_MLA * HEAD_DIM
    return pl.pallas_call(
        functools.partial(_mla_kv_kernel, n_heads=N_HEADS_MLA),
        grid=(b, n // tm),
        in_specs=[pl.BlockSpec((None, tm, MLA_KV_LORA), lambda bb, i: (bb, i, COL_CKV * LANE // MLA_KV_LORA)),
                  pl.BlockSpec((None, tm, LANE), lambda bb, i: (bb, i, COL_KR)),
                  pl.BlockSpec((1, MLA_KV_LORA), lambda bb, i: (0, 0)),
                  pl.BlockSpec((MLA_KV_LORA, vw), lambda bb, i: (0, 0)),
                  pl.BlockSpec((vw, MLA_KV_LORA), lambda bb, i: (0, 0)),
                  pl.BlockSpec((tm, LANE), lambda bb, i: (i, 0)),
                  pl.BlockSpec((tm, LANE), lambda bb, i: (i, 0))],
        out_specs=[pl.BlockSpec((None, tm, kw), lambda bb, i: (bb, i, 0)),
                   pl.BlockSpec((None, vw, tm), lambda bb, i: (bb, 0, i))],
        out_shape=[jax.ShapeDtypeStruct((b, n, kw), BF16), jax.ShapeDtypeStruct((b, vw, n), BF16)],
        compiler_params=_cparams(("parallel", "parallel"), VMEM_MIB_MEDIUM),
        name="mla_kv_prep",
    )(p, p, g.reshape(1, MLA_KV_LORA), w_uk, w_uv_t, cos, sin)


def _attn_kernel(*refs, n_src, scale, tk_max):
    q_ref = refs[0]
    o_ref = refs[1 + 2 * n_src]
    q = q_ref[...]
    if scale != 1.0:
        q = q.astype(F32) * scale
    q = q.astype(BF16)
    tq = q.shape[0]
    dv = o_ref.shape[-1]

    def step(k, v, carry):
        m, l, acc = carry
        s = lax.dot_general(q, k, (((1,), (1,)), ((), ())), preferred_element_type=F32)
        m_new = jnp.maximum(m, jnp.max(s, axis=-1, keepdims=True))
        p = jnp.exp(s - m_new)
        alpha = jnp.exp(m - m_new)
        l = alpha * l + jnp.sum(p, axis=-1, keepdims=True)
        acc = alpha * acc + jnp.dot(p.astype(BF16), v, preferred_element_type=F32)
        return m_new, l, acc

    carry = (jnp.full((tq, 1), NEG_BIG, F32), jnp.zeros((tq, 1), F32), jnp.zeros((tq, dv), F32))
    for s_i in range(n_src):
        k_ref = refs[1 + 2 * s_i]
        v_ref = refs[2 + 2 * s_i]
        nk = k_ref.shape[0]
        tk = min(nk, tk_max)
        if nk == tk:
            carry = step(k_ref[...].astype(BF16), v_ref[...].astype(BF16), carry)
        else:
            def body(c, cr, k_ref=k_ref, v_ref=v_ref, tk=tk):
                k0 = pl.multiple_of(c * tk, tk)
                return step(k_ref[pl.ds(k0, tk), :].astype(BF16), v_ref[pl.ds(k0, tk), :].astype(BF16), cr)
            carry = lax.fori_loop(0, nk // tk, body, carry)
    _, l, acc = carry
    o_ref[...] = acc / l


def _attention(q_arr, q_blk0, dq, sources, dv, n_heads, scale, tq=256, tk_max=512):
    b, nq, _ = q_arr.shape
    tq = min(nq, tq)
    in_specs = [pl.BlockSpec((None, tq, dq), lambda bb, h, i: (bb, i, q_blk0 + h))]
    args = [q_arr]
    for k_arr, k_blk0, v_arr, v_blk0 in sources:
        nk = k_arr.shape[1]
        in_specs.append(pl.BlockSpec((None, nk, dq), lambda bb, h, i, o=k_blk0: (bb, 0, o + h)))
        in_specs.append(pl.BlockSpec((None, nk, dv), lambda bb, h, i, o=v_blk0: (bb, 0, o + h)))
        args += [k_arr, v_arr]
    return pl.pallas_call(
        functools.partial(_attn_kernel, n_src=len(sources), scale=scale, tk_max=tk_max),
        grid=(b, n_heads, nq // tq),
        in_specs=in_specs,
        out_specs=pl.BlockSpec((None, tq, dv), lambda bb, h, i: (bb, i, h)),
        out_shape=jax.ShapeDtypeStruct((b, nq, n_heads * dv), F32),
        compiler_params=_cparams(("parallel", "parallel", "arbitrary"), VMEM_MIB_LARGE),
        name="attention",
    )(*args)


ATTN_TQ = 1024
ATTN_TK = 256
ATTN_GROUP = 32


def _attn_t_kernel(*refs, n_src):
    q_ref = refs[0]
    o_ref = refs[1 + 2 * n_src]
    q = q_ref[...]
    tq = q.shape[1]
    dv = o_ref.shape[-1]

    def scores(k):
        return jnp.dot(k, q, preferred_element_type=F32)

    def update(s, vt, carry):
        m, l, acc = carry
        m_new = jnp.maximum(m, jnp.max(s, axis=0, keepdims=True))
        p = jnp.exp2(s - m_new)
        alpha = jnp.exp2(m - m_new)
        l = alpha * l + jnp.sum(p, axis=0, keepdims=True)
        acc = alpha * acc + jnp.dot(vt, p.astype(BF16), preferred_element_type=F32)
        return m_new, l, acc

    carry = (jnp.full((1, tq), NEG_BIG, F32), jnp.zeros((1, tq), F32), jnp.zeros((dv, tq), F32))
    for s_i in range(n_src):
        k_ref = refs[1 + 2 * s_i]
        vt_ref = refs[2 + 2 * s_i]
        nk = k_ref.shape[0]
        tk = min(nk, ATTN_TK)
        n_chunks = nk // tk
        if n_chunks == 1:
            carry = update(scores(k_ref[...]), vt_ref[...], carry)
            continue
        group = max(g for g in (ATTN_GROUP, 2, 1) if n_chunks % g == 0)

        def body(c, cr, k_ref=k_ref, vt_ref=vt_ref, tk=tk, group=group):
            starts = [pl.multiple_of((c * group + g) * tk, tk) for g in range(group)]
            ss = [scores(k_ref[pl.ds(k0, tk), :]) for k0 in starts]
            for s, k0 in zip(ss, starts):
                cr = update(s, vt_ref[:, pl.ds(k0, tk)], cr)
            return cr
        carry = lax.fori_loop(0, n_chunks // group, body, carry)
    _, l, acc = carry
    o_ref[...] = (acc / l).T


def _attention_t(q_arr, sources, n_heads, dq, dv):
    b, _, nq = q_arr.shape
    tq = min(nq, ATTN_TQ)
    in_specs = [pl.BlockSpec((None, dq, tq), lambda bb, h, i: (bb, h, i))]
    args = [q_arr]
    for k_arr, vt_arr in sources:
        nk = k_arr.shape[1]
        in_specs.append(pl.BlockSpec((None, nk, dq), lambda bb, h, i: (bb, 0, h)))
        in_specs.append(pl.BlockSpec((None, dv, nk), lambda bb, h, i: (bb, h, 0)))
        args += [k_arr, vt_arr]
    return pl.pallas_call(
        functools.partial(_attn_t_kernel, n_src=len(sources)),
        grid=(b, n_heads, nq // tq),
        in_specs=in_specs,
        out_specs=pl.BlockSpec((None, tq, dv), lambda bb, h, i: (bb, i, h)),
        out_shape=jax.ShapeDtypeStruct((b, nq, n_heads * dv), F32),
        compiler_params=_cparams(("parallel", "parallel", "arbitrary"), VMEM_MIB_XLARGE),
        name="attention_t",
    )(*args)


NAT_GROUP = 8
NAT_SLAB = NAT_GROUP + NAT_ROWS
NAT_HALVES = 2


def _nat_kernel(q_ref, k_ref, v_ref, kc_ref, vc_ref, bias_ref, o_ref, *, n_rows):
    i = pl.program_id(2)
    scale = HEAD_DIM ** -0.5
    nt = (((1,), (1,)), ((), ()))
    u0 = jnp.clip(i * NAT_GROUP - NAT_ROWS // 2, 0, n_rows - NAT_SLAB)
    t0 = pl.multiple_of(u0 * GRID_W, GRID_W)
    ks = k_ref[pl.ds(t0, NAT_SLAB * GRID_W), :].astype(BF16)
    vs = v_ref[pl.ds(t0, NAT_SLAB * GRID_W), :].astype(BF16)
    kc = kc_ref[...].astype(BF16)
    vc = vc_ref[...].astype(BF16)
    hq = q_ref.shape[0] // NAT_HALVES
    scores = []
    for j in range(NAT_HALVES):
        rows = slice(j * hq, (j + 1) * hq)
        q = (q_ref[rows, :].astype(F32) * scale).astype(BF16)
        scores.append((lax.dot_general(q, ks, nt, preferred_element_type=F32) + bias_ref[rows, :],
                       lax.dot_general(q, kc, nt, preferred_element_type=F32)))
    for j in range(NAT_HALVES):
        s_w, s_c = scores[j]
        m = jnp.maximum(jnp.max(s_w, axis=-1, keepdims=True), jnp.max(s_c, axis=-1, keepdims=True))
        p_w = jnp.exp(s_w - m)
        p_c = jnp.exp(s_c - m)
        l = jnp.sum(p_w, axis=-1, keepdims=True) + jnp.sum(p_c, axis=-1, keepdims=True)
        o = (jnp.dot(p_w.astype(BF16), vs, preferred_element_type=F32)
             + jnp.dot(p_c.astype(BF16), vc, preferred_element_type=F32))
        o_ref[j * hq:(j + 1) * hq, :] = o / l


def _nat_bias_table(rpb):
    h = rpb.shape[0]
    col = np.arange(GRID_W)
    c0 = np.clip(col - NAT_COLS // 2, 0, GRID_W - NAT_COLS)
    kc = np.arange(GRID_W)
    col_ok = (kc[None, :] >= c0[:, None]) & (kc[None, :] < c0[:, None] + NAT_COLS)
    dc = kc[None, :] - col[:, None] + NAT_COLS - 1
    sel_c = ((dc[None] == np.arange(2 * NAT_COLS - 1)[:, None, None]) & col_ok[None]).astype(np.float32)
    sel_c = sel_c.reshape(2 * NAT_COLS - 1, GRID_W * GRID_W)
    per_dr = jnp.einsum('hdc,cb->hdb', rpb.astype(F32), sel_c, precision=lax.Precision.HIGHEST)
    per_dr = jnp.where(col_ok.reshape(-1)[None, None], per_dr, NEG_BIG).reshape(h, 2 * NAT_ROWS - 1, GRID_W, GRID_W)
    return pl.pallas_call(
        _nat_table_kernel,
        grid=(h, 3),
        in_specs=[pl.BlockSpec((None, 2 * NAT_ROWS - 1, GRID_W, GRID_W), lambda hh, v: (hh, 0, 0, 0))],
        out_specs=pl.BlockSpec((None, None, NAT_GROUP * GRID_W, NAT_SLAB * GRID_W), lambda hh, v: (hh, v, 0, 0)),
        out_shape=jax.ShapeDtypeStruct((h, 3, NAT_GROUP * GRID_W, NAT_SLAB * GRID_W), F32),
        compiler_params=_cparams(("parallel", "arbitrary"), VMEM_MIB_SMALL),
        name="nat_bias_table",
    )(per_dr)


def _nat_table_kernel(per_dr_ref, o_ref):
    masked = jnp.full((GRID_W, GRID_W), NEG_BIG, F32)
    for variant, delta in enumerate((0, NAT_ROWS // 2, NAT_ROWS)):
        @pl.when(pl.program_id(1) == variant)
        def _(delta=delta):
            for g in range(NAT_GROUP):
                w0 = min(max(g + delta - NAT_ROWS // 2, 0), NAT_SLAB - NAT_ROWS)
                tiles = [per_dr_ref[u - (g + delta) + NAT_ROWS - 1] if w0 <= u < w0 + NAT_ROWS else masked
                         for u in range(NAT_SLAB)]
                o_ref[g * GRID_W:(g + 1) * GRID_W, :] = jnp.concatenate(tiles, axis=1)


def _nat_latent(p_l, p_c, bias_tbl):
    b, n, _ = p_l.shape
    n_ctx = p_c.shape[1]
    n_rows = n // GRID_W
    assert n_rows >= NAT_SLAB and n_rows % NAT_GROUP == 0
    n_groups = n_rows // NAT_GROUP
    tq = NAT_GROUP * GRID_W
    tk = NAT_SLAB * GRID_W

    def variant(i):
        return jnp.where(i == 0, 0, jnp.where(i == n_groups - 1, 2, 1))

    return pl.pallas_call(
        functools.partial(_nat_kernel, n_rows=n_rows),
        grid=(b, N_HEADS_NAT, n_groups),
        in_specs=[pl.BlockSpec((None, tq, LANE), lambda bb, h, i: (bb, i, COL_NAT_Q + h)),
                  pl.BlockSpec((None, n, LANE), lambda bb, h, i: (bb, 0, COL_NAT_K + h)),
                  pl.BlockSpec((None, n, LANE), lambda bb, h, i: (bb, 0, COL_NAT_V + h)),
                  pl.BlockSpec((None, n_ctx, LANE), lambda bb, h, i: (bb, 0, COL_NAT_K + h)),
                  pl.BlockSpec((None, n_ctx, LANE), lambda bb, h, i: (bb, 0, COL_NAT_V + h)),
                  pl.BlockSpec((None, None, tq, tk), lambda bb, h, i: (h, variant(i), 0, 0))],
        out_specs=pl.BlockSpec((None, tq, LANE), lambda bb, h, i: (bb, i, h)),
        out_shape=jax.ShapeDtypeStruct((b, n, N_HEADS_NAT * HEAD_DIM), F32),
        compiler_params=_cparams(("parallel", "parallel", "arbitrary"), VMEM_MIB_LARGE),
        name="nat_latent",
    )(p_l, p_l, p_l, p_c, p_c, bias_tbl)


def _ret_consts(logit, reverse):
    c_len = RET_CHUNK
    z = jnp.full((c_len, c_len), -logit, F32)
    lg = -(jnp.maximum(z, 0.0) + jnp.log1p(jnp.exp(-jnp.abs(z))))
    ci = lax.broadcasted_iota(jnp.int32, (c_len, c_len), 0).astype(F32)
    si = lax.broadcasted_iota(jnp.int32, (c_len, c_len), 1).astype(F32)
    if reverse:
        rel = si - ci
        wq = jnp.exp(lg * (c_len - ci))
        wk = jnp.exp(lg * ci)
    else:
        rel = ci - si
        wq = jnp.exp(lg * (ci + 1.0))
        wk = jnp.exp(lg * (c_len - 1.0 - ci))
    dmat = jnp.where(rel >= 0, jnp.exp(lg * jnp.maximum(rel, 0.0)), 0.0)
    return wq, wk, dmat, jnp.exp(lg * c_len)


def _ret_kernel(lf_ref, lb_ref, qf_ref, kf_ref, vf_ref, gf_ref, qb_ref, kb_ref, vb_ref, gb_ref, s0f_ref, s0b_ref,
                yf_ref, yb_ref, sf_ref, sb_ref, c_scr, *, n_chunks):
    h = pl.program_id(1)

    @pl.when(pl.program_id(2) == 0)
    def _():
        sf_ref[...] = s0f_ref[...]
        sb_ref[...] = s0b_ref[...]
        for j, arr in enumerate(_ret_consts(lf_ref[h], False) + _ret_consts(lb_ref[h], True)):
            c_scr[j] = arr

    c_len = RET_CHUNK
    nt = (((1,), (1,)), ((), ()))
    k_scale = HEAD_DIM ** -0.5
    dirs = []
    for d, refs in enumerate(((qf_ref, kf_ref, vf_ref, gf_ref, yf_ref), (qb_ref, kb_ref, vb_ref, gb_ref, yb_ref))):
        order = list(range(n_chunks)) if d == 0 else list(range(n_chunks - 1, -1, -1))
        dirs.append(refs + tuple(c_scr[4 * d + j] for j in range(4)) + (order,))

    units = []
    for step in range(n_chunks):
        for d in range(2):
            q_ref, k_ref, v_ref, _, _, _, wk, dmat, _, order = dirs[d]
            sl = slice(order[step] * c_len, (order[step] + 1) * c_len)
            qb = q_ref[sl, :].astype(BF16)
            kb = k_ref[sl, :]
            vb = v_ref[sl, :].astype(BF16)
            sc = lax.dot_general(qb, kb.astype(BF16), nt, preferred_element_type=F32) * (dmat * k_scale)
            kv = jnp.dot((kb.astype(F32) * (wk * k_scale)).T.astype(BF16), vb, preferred_element_type=F32)
            units.append((d, sl, qb, vb, sc, kv))
    states = [sf_ref[...], sb_ref[...]]
    cross = []
    for (d, sl, qb, vb, sc, kv) in units:
        wq, g_chunk = dirs[d][5], dirs[d][8]
        cross.append(wq * jnp.dot(qb, states[d].astype(BF16), preferred_element_type=F32))
        states[d] = g_chunk * states[d] + kv
    sf_ref[...] = states[0]
    sb_ref[...] = states[1]
    for (d, sl, qb, vb, sc, kv), o_cross in zip(units, cross):
        g_ref, y_ref = dirs[d][3], dirs[d][4]
        o = jnp.dot(sc.astype(BF16), vb, preferred_element_type=F32) + o_cross
        mu = jnp.mean(o, axis=-1, keepdims=True)
        dev = o - mu
        gn = dev * lax.rsqrt(jnp.mean(dev * dev, axis=-1, keepdims=True) + EPS)
        y_ref[sl, :] = _silu(g_ref[sl, :].astype(F32)) * gn


def _retention_scan(p, logit_f, logit_b, s0_f, s0_b):
    b, n, _ = p.shape
    hh = N_HEADS_RET
    tb = min(n, 16 * RET_CHUNK)
    nblk = n // tb

    def fwd(c0):
        return pl.BlockSpec((None, tb, LANE), lambda bb, h, i: (bb, i, c0 + h))

    def bwd(c0):
        return pl.BlockSpec((None, tb, LANE), lambda bb, h, i: (bb, nblk - 1 - i, c0 + h))

    state = pl.BlockSpec((None, None, HEAD_DIM, HEAD_DIM), lambda bb, h, i: (bb, h, 0, 0))
    smem = pl.BlockSpec(memory_space=pltpu.SMEM)
    y_shape = jax.ShapeDtypeStruct((b, n, hh * HEAD_DIM), F32)
    s_shape = jax.ShapeDtypeStruct((b, hh, HEAD_DIM, HEAD_DIM), F32)
    return pl.pallas_call(
        functools.partial(_ret_kernel, n_chunks=tb // RET_CHUNK),
        grid=(b, hh, nblk),
        in_specs=[smem, smem,
                  fwd(COL_RET_Q), fwd(COL_RET_K), fwd(COL_RET_V), fwd(COL_RET_GF),
                  bwd(COL_RET_Q), bwd(COL_RET_K), bwd(COL_RET_V), bwd(COL_RET_GB),
                  state, state],
        out_specs=[fwd(0), bwd(0), state, state],
        out_shape=[y_shape, y_shape, s_shape, s_shape],
        scratch_shapes=[pltpu.VMEM((8, RET_CHUNK, RET_CHUNK), F32)],
        compiler_params=_cparams(("parallel", "parallel", "arbitrary"), VMEM_MIB_SMALL),
        name="retention",
    )(logit_f.astype(F32), logit_b.astype(F32), p, p, p, p, p, p, p, p, s0_f, s0_b)


def _retention(p_c, p_l, logit_f, logit_b):
    b = p_c.shape[0]
    s0 = jnp.zeros((b, N_HEADS_RET, HEAD_DIM, HEAD_DIM), F32)
    yc_f, yc_b, s_cf, s_cb = _retention_scan(p_c, logit_f, logit_b, s0, s0)
    yl_f, yl_b, _, _ = _retention_scan(p_l, logit_f, logit_b, s_cf, s_cb)
    return (yc_f, yc_b), (yl_f, yl_b)


MERGE_ROWS = 512
MERGE_HALVES = 2


def _merge_kernel(x_ref, ya_ref, yb_ref, yrf_ref, yrb_ref, g_ref, gate_ref, w_ref, g2_ref, sh2_ref, sc2_ref, wr_ref,
                  o_ref, h_ref, aff_ref):
    wa = ya_ref.shape[-1]
    wb = yb_ref.shape[-1]
    hr = x_ref.shape[0] // MERGE_HALVES
    halves = [slice(j * hr, (j + 1) * hr) for j in range(MERGE_HALVES)]
    wr = wr_ref[...]
    w_hi = wr.astype(BF16)
    w_lo_f = wr - w_hi.astype(F32)
    w_hi_lo = (w_hi.astype(F32) + pltpu.roll(w_lo_f, N_EXPERTS, 1)).astype(BF16)
    normed = []
    for rows in halves:
        na = (_rms(ya_ref[rows, :]) * g_ref[:, :wa]).astype(BF16)
        nb = (_rms(yb_ref[rows, :]) * g_ref[:, wa:wa + wb]).astype(BF16)
        nr = (_rms(yrf_ref[rows, :] + yrb_ref[rows, :]) * g_ref[:, wa + wb:]).astype(BF16)
        normed.append((na, nb, nr))
    accs = []
    for na, nb, nr in normed:
        acc = jnp.dot(na, w_ref[:wa, :], preferred_element_type=F32)
        acc = acc + jnp.dot(nb, w_ref[wa:wa + wb, :], preferred_element_type=F32)
        accs.append(acc + jnp.dot(nr, w_ref[wa + wb:, :], preferred_element_type=F32))
    for rows, acc in zip(halves, accs):
        x_new = x_ref[rows, :] + gate_ref[...] * acc
        o_ref[rows, :] = x_new
        h = (_rms(x_new) * g2_ref[...]) * (1.0 + sc2_ref[...]) + sh2_ref[...]
        h_hi = h.astype(BF16)
        h_ref[rows, :] = h_hi
        h_lo = (h - h_hi.astype(F32)).astype(BF16)
        hh = jnp.dot(h_hi, w_hi_lo, preferred_element_type=F32)
        logits = hh + (pltpu.roll(hh, LANE - N_EXPERTS, 1) + jnp.dot(h_lo, w_hi, preferred_element_type=F32))
        lane = lax.broadcasted_iota(jnp.int32, logits.shape, 1)
        logits = jnp.where(lane < N_EXPERTS, logits, NEG_BIG)
        e = jnp.exp(logits - jnp.max(logits, axis=-1, keepdims=True))
        aff_ref[rows, :] = e / jnp.sum(e, axis=-1, keepdims=True)


def _merge_out(x, ya, yb, yr, g, gate, w_out, g2, shift2, scale2, w_router_pad):
    b, n, d = x.shape
    yrf, yrb = yr
    tm = min(n, MERGE_ROWS)
    dm = w_out.shape[0]

    def tok(width):
        return pl.BlockSpec((None, tm, width), lambda bb, i: (bb, i, 0))

    row = pl.BlockSpec((1, d), lambda bb, i: (0, 0))
    per_batch = pl.BlockSpec((None, 1, d), lambda bb, i: (bb, 0, 0))
    return pl.pallas_call(
        _merge_kernel,
        grid=(b, n // tm),
        in_specs=[tok(d), tok(ya.shape[-1]), tok(yb.shape[-1]), tok(yrf.shape[-1]), tok(yrb.shape[-1]),
                  pl.BlockSpec((1, dm), lambda bb, i: (0, 0)), per_batch,
                  pl.BlockSpec((dm, d), lambda bb, i: (0, 0)),
                  row, per_batch, per_batch,
                  pl.BlockSpec((d, LANE), lambda bb, i: (0, 0))],
        out_specs=[tok(d), tok(d), tok(LANE)],
        out_shape=[jax.ShapeDtypeStruct((b, n, d), F32), jax.ShapeDtypeStruct((b, n, d), BF16),
                   jax.ShapeDtypeStruct((b, n, LANE), F32)],
        compiler_params=_cparams(("parallel", "parallel"), VMEM_MIB_MAX),
        name="merge_out",
    )(x, ya, yb, yrf, yrb, g.reshape(1, dm), gate, w_out, g2.reshape(1, d), shift2, scale2, w_router_pad)


def _ffn_up_kernel(xs_ref, wg_ref, wu_ref, hm_ref, wg_scr, wu_scr):
    @pl.when(pl.program_id(1) == 0)
    def _():
        wg_scr[...] = wg_ref[...].astype(BF16)
        wu_scr[...] = wu_ref[...].astype(BF16)

    xs = xs_ref[...]
    a = jnp.dot(xs, wg_scr[...], preferred_element_type=F32)
    u = jnp.dot(xs, wu_scr[...], preferred_element_type=F32)
    hm_ref[...] = (_silu(a) * u).astype(BF16)


def _ffn_down_kernel(hm_ref, gate_ref, wd_ref, o_ref, wd_scr):
    @pl.when(pl.program_id(1) == 0)
    def _():
        wd_scr[...] = wd_ref[...].astype(BF16)

    o_ref[...] = (jnp.dot(hm_ref[...], wd_scr[...], preferred_element_type=F32) * gate_ref[...]).astype(o_ref.dtype)


def _expert_ffn(xs, gates, w_gate, w_up, w_down, layer):
    e, t, d = xs.shape
    f = w_gate.shape[-1]
    tm = min(t, 512)
    tm_down = min(t, 1024)
    hm = pl.pallas_call(
        _ffn_up_kernel,
        grid=(e, t // tm),
        in_specs=[pl.BlockSpec((None, tm, d), lambda ee, i: (ee, i, 0)),
                  pl.BlockSpec((None, None, d, f), lambda ee, i: (layer, ee, 0, 0)),
                  pl.BlockSpec((None, None, d, f), lambda ee, i: (layer, ee, 0, 0))],
        out_specs=pl.BlockSpec((None, tm, f), lambda ee, i: (ee, i, 0)),
        out_shape=jax.ShapeDtypeStruct((e, t, f), BF16),
        scratch_shapes=[pltpu.VMEM((d, f), BF16), pltpu.VMEM((d, f), BF16)],
        compiler_params=_cparams(("parallel", "arbitrary"), VMEM_MIB_XLARGE),
        name="expert_ffn_up",
    )(xs, w_gate, w_up)
    return pl.pallas_call(
        _ffn_down_kernel,
        grid=(e, t // tm_down),
        in_specs=[pl.BlockSpec((None, tm_down, f), lambda ee, i: (ee, i, 0)),
                  pl.BlockSpec((None, tm_down, 1), lambda ee, i: (ee, i, 0)),
                  pl.BlockSpec((None, None, f, d), lambda ee, i: (layer, ee, 0, 0))],
        out_specs=pl.BlockSpec((None, tm_down, d), lambda ee, i: (ee, i, 0)),
        out_shape=jax.ShapeDtypeStruct((e, t, d), BF16),
        scratch_shapes=[pltpu.VMEM((f, d), BF16)],
        compiler_params=_cparams(("parallel", "arbitrary"), VMEM_MIB_XLARGE),
        name="expert_ffn_down",
    )(hm, gates, w_down)


COMBINE_TOKENS = 256
COMBINE_ROWS = 512


def _combine_kernel(tile_ref, win_ref, flag_ref, x_ref, yp_ref, tok_ref, gate_ref, gn_ref, o_ref, acc_scr, *,
                    final_norm):
    s = pl.program_id(0)
    flags = flag_ref[s]
    t = acc_scr.shape[0]

    @pl.when((flags & 1) != 0)
    def _():
        acc_scr[...] = jnp.zeros(acc_scr.shape, F32)

    @pl.when((flags & 2) != 0)
    def _():
        tcol = tile_ref[s] * t + lax.broadcasted_iota(jnp.int32, (t, 1), 0)
        onehot = jnp.where(tcol == tok_ref[...], 1.0, 0.0).astype(BF16)
        acc_scr[...] += jnp.dot(onehot, yp_ref[...], preferred_element_type=F32)

    @pl.when((flags & 4) != 0)
    def _():
        y = x_ref[...] + gate_ref[...] * acc_scr[...]
        if final_norm:
            y = _rms(y) * gn_ref[...]
        o_ref[...] = y


def _moe_combine(x, y_rows, tok_rows, gate, norm_g, final_norm, batch0=0, n_batches=None):
    b, n, d = x.shape
    r = y_rows.shape[0]
    t = min(COMBINE_TOKENS, n)
    w = COMBINE_ROWS
    tiles_per_batch = n // t
    ntiles = (b if n_batches is None else n_batches) * tiles_per_batch
    tile0 = batch0 * tiles_per_batch
    nwin_total = r // w
    n_steps = nwin_total + 2 * ntiles
    i32 = jnp.int32
    tok_sorted, order = lax.sort_key_val(tok_rows.astype(i32), jnp.arange(r, dtype=i32))
    yp = jnp.take(y_rows, order, axis=0, mode="clip")
    edges = (tile0 + jnp.arange(ntiles + 1, dtype=i32)) * t
    bounds = jnp.sum((tok_sorted[None, :] < edges[:, None]).astype(i32), axis=1)
    w_lo = jnp.minimum(bounds[:-1] // w, nwin_total - 1)
    w_hi = jnp.maximum((bounds[1:] + w - 1) // w, w_lo + 1)
    nwin = w_hi - w_lo
    step0 = jnp.cumsum(nwin) - nwin
    total = jnp.sum(nwin)
    s = jnp.arange(n_steps, dtype=i32)
    tile_s = jnp.sum((s[:, None] >= step0[None, :]).astype(i32), axis=1) - 1
    sel = tile_s[:, None] == jnp.arange(ntiles, dtype=i32)[None, :]
    tile_s = tile_s + tile0

    def of_tile(v):
        return jnp.sum(jnp.where(sel, v[None, :], 0), axis=1)

    step0_s, w_lo_s, nwin_s = of_tile(step0), of_tile(w_lo), of_tile(nwin)
    win_s = jnp.minimum(w_lo_s + (s - step0_s), w_lo_s + nwin_s - 1)
    valid = s < total
    first = valid & (s == step0_s)
    last = valid & (s == step0_s + nwin_s - 1)
    flags = first.astype(i32) + 2 * valid.astype(i32) + 4 * last.astype(i32)
    in_specs = [pl.BlockSpec((t, d), lambda s, tl, wn, fl: (tl[s], 0)),
                pl.BlockSpec((w, d), lambda s, tl, wn, fl: (wn[s], 0)),
                pl.BlockSpec((None, 1, w), lambda s, tl, wn, fl: (wn[s], 0, 0)),
                pl.BlockSpec((None, 1, d), lambda s, tl, wn, fl: (tl[s] // tiles_per_batch, 0, 0)),
                pl.BlockSpec((1, d), lambda s, tl, wn, fl: (0, 0))]
    args = [tile_s, win_s, flags, x.reshape(b * n, d), yp, tok_sorted.reshape(nwin_total, 1, w), gate,
            norm_g.reshape(1, d)]
    out = pl.pallas_call(
        functools.partial(_combine_kernel, final_norm=final_norm),
        grid_spec=pltpu.PrefetchScalarGridSpec(
            num_scalar_prefetch=3,
            grid=(n_steps,),
            in_specs=in_specs,
            out_specs=pl.BlockSpec((t, d), lambda s, tl, wn, fl: (tl[s], 0)),
            scratch_shapes=[pltpu.VMEM((t, d), F32)]),
        out_shape=jax.ShapeDtypeStruct((b * n, d), F32),
        compiler_params=_cparams(("arbitrary",), VMEM_MIB_MEDIUM),
        input_output_aliases={3: 0},
        name="moe_combine",
    )(*args)
    return out.reshape(b, n, d)


def _moe_select(aff):
    b, n, _ = aff.shape
    cap = EC_CAPACITY * n // N_EXPERTS
    gate, idx = lax.top_k(jnp.swapaxes(aff[..., :N_EXPERTS], 1, 2), cap)
    flat = (idx + (jnp.arange(b, dtype=idx.dtype) * n)[:, None, None]).transpose(1, 0, 2).reshape(-1)
    return flat, gate.transpose(1, 0, 2).reshape(N_EXPERTS, b * cap, 1)


def _moe_gather(h, flat):
    b, n, d = h.shape
    return jnp.take(h.reshape(b * n, d), flat, axis=0, mode="clip").reshape(N_EXPERTS, -1, d)


def _moe_latent_split(x, h, aff, gate2, experts, norm_g, final_norm):
    b, n, d = x.shape
    size = b // 2 if b % 2 == 0 else b
    routed = []
    for b0 in range(0, b, size):
        flat, gates = _moe_select(aff[b0:b0 + size])
        flat = flat + b0 * n
        routed.append((b0, flat, gates, _moe_gather(h, flat)))
    for b0, flat, gates, xs in routed:
        y = _expert_ffn(xs, gates, *experts)
        x = _moe_combine(x, y.reshape(-1, d), flat, gate2, norm_g, final_norm, b0, size)
    return x


def _moe_experts(x, xs, gates, flat, gate2, experts, norm_g, final_norm):
    y = _expert_ffn(xs, gates, *experts)
    return _moe_combine(x, y.reshape(-1, x.shape[-1]), flat, gate2, norm_g, final_norm)


_ROPE_SRC = np.concatenate([np.arange(16, 32), np.arange(0, 16), np.arange(48, 64), np.arange(32, 48)])
_ROPE_SIGN = np.concatenate([-np.ones(16), np.ones(16), -np.ones(16), np.ones(16)]).astype(np.float32)


def _rope_perm_cols(w):
    return w[:, _ROPE_SRC] * _ROPE_SIGN


def _pad_w_in(w_in):
    c_kr = MLA_Q_LORA + MLA_KV_LORA
    kr = w_in[:, c_kr:c_kr + MLA_ROPE_DIM]
    return jnp.concatenate([w_in[:, :c_kr + MLA_ROPE_DIM], _rope_perm_cols(kr), w_in[:, c_kr + MLA_ROPE_DIM:]],
                           axis=1).astype(BF16)


def _expand_w_uq(w_uq):
    w = w_uq.reshape(MLA_Q_LORA, N_HEADS_MLA, HEAD_DIM + MLA_ROPE_DIM)
    nope = w[:, :, :HEAD_DIM]
    rope = w[:, :, HEAD_DIM:]
    partner = rope[:, :, _ROPE_SRC] * _ROPE_SIGN
    return jnp.concatenate([nope, rope, partner], axis=-1).reshape(MLA_Q_LORA, N_HEADS_MLA * MLA_QK_PAD).astype(BF16)


def _rope_tables(n):
    t = jnp.arange(n)
    n_freq = MLA_ROPE_DIM // 4
    inv = ROPE_BASE ** (-jnp.arange(n_freq, dtype=F32) / n_freq)
    ang_r = (t // GRID_W).astype(F32)[:, None] * inv
    ang_c = (t % GRID_W).astype(F32)[:, None] * inv
    zeros = jnp.zeros((n, LANE - MLA_ROPE_DIM), F32)
    cos = jnp.concatenate([jnp.cos(ang_r), jnp.cos(ang_r), jnp.cos(ang_c), jnp.cos(ang_c), zeros], axis=1)
    sin = jnp.concatenate([jnp.sin(ang_r), jnp.sin(ang_r), jnp.sin(ang_c), jnp.sin(ang_c), zeros], axis=1)
    return cos, sin


def _identity_tables(n):
    ones = jnp.ones((n, MLA_ROPE_DIM), F32)
    zeros = jnp.zeros((n, LANE - MLA_ROPE_DIM), F32)
    return jnp.concatenate([ones, zeros], axis=1), jnp.zeros((n, LANE), F32)


def _layer(x_l, x_c, mod, lp, last, final_g):
    b, n, d = x_l.shape
    n_ctx = x_c.shape[1]

    def chunk(k, ctx):
        rows = jnp.broadcast_to(mod[b, k * d:(k + 1) * d], (b, d)) if ctx else mod[:b, k * d:(k + 1) * d]
        return rows.reshape(b, 1, d)

    w_in = _pad_w_in(lp['w_in'])
    p_l = _norm_proj(x_l, lp['norm1_g'], chunk(0, False), chunk(1, False), w_in)
    p_c = _norm_proj(x_c, lp['norm1_g'], chunk(0, True), chunk(1, True), w_in)

    cos_l, sin_l = _rope_tables(n)
    cos_c, sin_c = _identity_tables(n_ctx)
    w_uk = lp['mla_w_uk'].reshape(MLA_KV_LORA, -1).astype(BF16)
    w_uv_t = lp['mla_w_uv'].reshape(MLA_KV_LORA, -1).T.astype(BF16)
    w_q_t = _expand_w_uq(lp['mla_w_uq']).T
    k_l, v_l = _mla_kv_prep(p_l, lp['mla_kv_norm_g'], w_uk, w_uv_t, cos_l, sin_l)
    k_c, v_c = _mla_kv_prep(p_c, lp['mla_kv_norm_g'], w_uk, w_uv_t, cos_c, sin_c)
    q_l = _mla_q_prep(p_l, lp['mla_q_norm_g'], w_q_t, cos_l.T, sin_l.T)
    a_l = _attention_t(q_l, [(k_c, v_c), (k_l, v_l)], N_HEADS_MLA, MLA_QK_PAD, HEAD_DIM)

    b_l = _nat_latent(p_l, p_c, _nat_bias_table(lp['nat_rpb']))
    r_c, r_l = _retention(p_c, p_l, lp['ret_decay_f'], lp['ret_decay_b'])

    w_out = lp['w_out'].astype(BF16)
    w_router = jnp.zeros((d, LANE), F32).at[:, :N_EXPERTS].set(lp['w_router'])

    x_l, h_l, aff_l = _merge_out(x_l, a_l, b_l, r_l, lp['out_norm_g'], chunk(2, False), w_out,
                                 lp['norm2_g'], chunk(3, False), chunk(4, False), w_router)
    if last:
        return _moe_latent_split(x_l, h_l, aff_l, chunk(5, False), lp['experts'], final_g, True), None
    flat_l, gates_l = _moe_select(aff_l)

    flat_l, p_c, x_c = lax.optimization_barrier((flat_l, p_c, x_c))
    xs_l = _moe_gather(h_l, flat_l)
    q_c = _mla_q_prep(p_c, lp['mla_q_norm_g'], w_q_t, cos_c.T, sin_c.T)
    a_c = _attention_t(q_c, [(k_c, v_c)], N_HEADS_MLA, MLA_QK_PAD, HEAD_DIM)
    b_c = _attention(p_c, COL_NAT_Q, HEAD_DIM, [(p_c, COL_NAT_K, p_c, COL_NAT_V)], HEAD_DIM, N_HEADS_NAT,
                     HEAD_DIM ** -0.5)
    x_c, h_c, aff_c = _merge_out(x_c, a_c, b_c, r_c, lp['out_norm_g'], chunk(2, True), w_out,
                                 lp['norm2_g'], chunk(3, True), chunk(4, True), w_router)
    flat_c, gates_c = _moe_select(aff_c)
    x_c = _moe_experts(x_c, _moe_gather(h_c, flat_c), gates_c, flat_c, chunk(5, True), lp['experts'], final_g, False)
    xs_l, x_c = lax.optimization_barrier((xs_l, x_c))
    x_l = _moe_experts(x_l, xs_l, gates_l, flat_l, chunk(5, False), lp['experts'], final_g, False)
    return x_l, x_c


def kernel(x, c, ctx, c_ctx, w_mod, b_mod, norm1_g, w_in, mla_q_norm_g, mla_kv_norm_g, mla_w_uq, mla_w_uk,
           mla_w_uv, nat_rpb, ret_decay_f, ret_decay_b, out_norm_g, w_out, norm2_g, w_router, w_gate, w_up,
           w_down, final_norm_g):
    depth = w_mod.shape[0]
    cvecs = jnp.concatenate([c, c_ctx[None, :]], axis=0)
    x_l, x_c = x, ctx
    for i in range(depth):
        lp = {
            'norm1_g': norm1_g[i], 'w_in': w_in[i], 'mla_q_norm_g': mla_q_norm_g[i],
            'mla_kv_norm_g': mla_kv_norm_g[i], 'mla_w_uq': mla_w_uq[i], 'mla_w_uk': mla_w_uk[i],
            'mla_w_uv': mla_w_uv[i], 'nat_rpb': nat_rpb[i], 'ret_decay_f': ret_decay_f[i],
            'ret_decay_b': ret_decay_b[i], 'out_norm_g': out_norm_g[i], 'w_out': w_out[i],
            'norm2_g': norm2_g[i], 'w_router': w_router[i], 'experts': (w_gate, w_up, w_down, i),
        }
        mod = _modulation(cvecs, w_mod, b_mod[i], i)
        x_l, x_c = _layer(x_l, x_c, mod, lp, i == depth - 1, final_norm_g)
    return x_l
```
